```python
import math
import jax, jax.numpy as jnp
from jax import lax
import numpy as np

D_MODEL = 1024
BATCH = 8
SEQ = 2048
DEPTH = 1
DEC_BATCH = 128
DEC_SEQ = 4
PAST_LEN = 16384
PAGE_SIZE = 128

MIX_WIDTH = D_MODEL
POOL_WINDOWS = (2, 4, 8, 16)
N_POOL_GROUPS = len(POOL_WINDOWS)
POOL_WIDTH = MIX_WIDTH // 4
POOL_GROUP = POOL_WIDTH // N_POOL_GROUPS
POOL_BUF = max(POOL_WINDOWS) - 1
GDN_WIDTH = MIX_WIDTH - POOL_WIDTH
GDN_HEAD_DIM = 128
GDN_HEADS = GDN_WIDTH // GDN_HEAD_DIM
CONV_WIDTH = 4
CONV_CH = 3 * GDN_WIDTH
GDN_CHUNK = 64
IN_WIDTH = POOL_WIDTH + CONV_CH + GDN_WIDTH + 2 * GDN_HEADS
N_EXPERTS = 32
TOP_K = 4
D_FF = D_MODEL
SWIGLU_LIMIT = 7.0
SWIGLU_ALPHA = 1.702
MOE_BLOCK = 128
EPS = 1e-6

kernel_name = 'hybrid_pool_gdn_moe_step'


def rmsnorm(x, g):
    xf = x.astype(jnp.float32)
    y = xf * lax.rsqrt(jnp.mean(xf * xf, axis=-1, keepdims=True) + EPS)
    return (y * g.astype(jnp.float32)).astype(x.dtype)


def l2norm(x):
    return x * lax.rsqrt(jnp.sum(x * x, axis=-1, keepdims=True) + EPS)


def pool_mixer(u, buf, start_pos, w_pool, pool_scale):
    bsz, t, _ = u.shape
    ext = jnp.concatenate([buf.astype(u.dtype), u], axis=1)
    extf = ext.astype(jnp.float32)
    cs = jnp.concatenate([jnp.zeros((bsz, 1, POOL_WIDTH), jnp.float32), jnp.cumsum(extf, axis=1)], axis=1)
    pos = start_pos + jnp.arange(t)
    means = []
    for gi, w in enumerate(POOL_WINDOWS):
        lo_c, hi_c = gi * POOL_GROUP, (gi + 1) * POOL_GROUP
        hi = cs[:, POOL_BUF + 1:POOL_BUF + 1 + t, lo_c:hi_c]
        lo = cs[:, POOL_BUF + 1 - w:POOL_BUF + 1 - w + t, lo_c:hi_c]
        cnt = jnp.minimum(pos + 1, w).astype(jnp.float32)[None, :, None]
        means.append((hi - lo) / cnt)
    pooled = jnp.concatenate(means, axis=-1) - extf[:, POOL_BUF:]
    pooled = pooled.reshape(bsz, t, N_POOL_GROUPS, POOL_GROUP)
    out = jnp.einsum('btgc,gcd->btgd', pooled, w_pool.astype(jnp.float32)).reshape(bsz, t, POOL_WIDTH)
    out = out * pool_scale.astype(jnp.float32)
    return out.astype(u.dtype), ext[:, -POOL_BUF:].astype(buf.dtype)


def causal_conv(x, buf, w):
    t = x.shape[1]
    ext = jnp.concatenate([buf.astype(x.dtype), x], axis=1)
    out = sum(ext[:, j:j + t] * w[j] for j in range(CONV_WIDTH))
    return jax.nn.silu(out), ext[:, -(CONV_WIDTH - 1):].astype(buf.dtype)


def gated_delta_rule(q, k, v, g, beta, s0):
    b, t, h, dk = q.shape
    dv = v.shape[-1]
    c = math.gcd(t, GDN_CHUNK)
    n = t // c

    def blocks(a):
        return jnp.moveaxis(a.reshape((b, n, c) + a.shape[2:]), 2, 3)

    q, k, v, g, beta = (blocks(a) for a in (q, k, v, g, beta))
    gc = jnp.cumsum(g, axis=-1)
    idx = jnp.arange(c)
    tril = idx[:, None] >= idx[None, :]
    strict = idx[:, None] > idx[None, :]
    diff = gc[..., :, None] - gc[..., None, :]
    decay = jnp.where(tril, jnp.exp(jnp.where(tril, diff, 0.0)), 0.0)
    kb = k * beta[..., None]
    vb = v * beta[..., None]
    lmat = jnp.where(strict, jnp.einsum('bnhid,bnhjd->bnhij', kb, k) * decay, 0.0)
    eye = jnp.eye(c, dtype=q.dtype)
    tinv = lax.linalg.triangular_solve(eye + lmat, jnp.broadcast_to(eye, lmat.shape),
                                       left_side=True, lower=True, unit_diagonal=True)
    u = jnp.einsum('bnhij,bnhje->bnhie', tinv, vb)
    w = jnp.einsum('bnhij,bnhjd->bnhid', tinv, kb * jnp.exp(gc)[..., None])
    qk = jnp.where(tril, jnp.einsum('bnhid,bnhjd->bnhij', q, k) * decay, 0.0)
    q_dec = q * jnp.exp(gc)[..., None]
    g_last = gc[..., -1]
    k_dec = k * jnp.exp(g_last[..., None] - gc)[..., None]

    def step(s, xs):
        q_i, k_i, u_i, w_i, qk_i, gl_i = xs
        v_new = u_i - jnp.einsum('bhcd,bhde->bhce', w_i, s)
        o_i = jnp.einsum('bhcd,bhde->bhce', q_i, s) + jnp.einsum('bhij,bhje->bhie', qk_i, v_new)
        s = s * jnp.exp(gl_i)[..., None, None] + jnp.einsum('bhcd,bhce->bhde', k_i, v_new)
        return s, o_i

    xs = tuple(jnp.moveaxis(a, 1, 0) for a in (q_dec, k_dec, u, w, qk, g_last))
    s_final, o = lax.scan(step, s0, xs)
    o = jnp.moveaxis(jnp.moveaxis(o, 0, 1), 3, 2).reshape(b, t, h, dv)
    return o, s_final


def moe_ffn(h, router_w, router_b, w_gu, b_gu, w_down, b_down):
    n, d = h.shape
    logits = (h @ router_w + router_b).astype(jnp.float32)
    top_vals, top_idx = lax.top_k(logits, TOP_K)
    gates = jax.nn.softmax(top_vals, axis=-1)
    nk = n * TOP_K
    flat_e = top_idx.reshape(nk)
    order = jnp.argsort(flat_e)
    sorted_e = flat_e[order]
    sorted_tok = (order // TOP_K).astype(jnp.int32)
    sorted_gate = gates.reshape(nk)[order]
    counts = jnp.bincount(flat_e, length=N_EXPERTS)
    padded = (counts + MOE_BLOCK - 1) // MOE_BLOCK * MOE_BLOCK
    start = jnp.cumsum(counts) - counts
    padded_end = jnp.cumsum(padded)
    padded_start = padded_end - padded
    dest = padded_start[sorted_e] + jnp.arange(nk) - start[sorted_e]
    n_blocks = -(-nk // MOE_BLOCK) + N_EXPERTS
    rows = n_blocks * MOE_BLOCK
    row_tok = jnp.full((rows,), n, jnp.int32).at[dest].set(sorted_tok)
    row_gate = jnp.zeros((rows,), jnp.float32).at[dest].set(sorted_gate)
    block_e = jnp.minimum(jnp.searchsorted(padded_end, jnp.arange(n_blocks) * MOE_BLOCK, side='right'),
                          N_EXPERTS - 1)
    h_pad = jnp.concatenate([h, jnp.zeros((1, d), h.dtype)], axis=0)
    xb = h_pad[row_tok].reshape(n_blocks, MOE_BLOCK, d)

    def expert_block(args):
        x_blk, e = args
        gu = x_blk @ w_gu[e] + b_gu[e]
        gate = jnp.minimum(gu[:, :D_FF], SWIGLU_LIMIT)
        up = jnp.clip(gu[:, D_FF:], -SWIGLU_LIMIT, SWIGLU_LIMIT)
        glu = gate * jax.nn.sigmoid(SWIGLU_ALPHA * gate)
        return ((up + 1.0) * glu) @ w_down[e] + b_down[e]

    yb = lax.map(expert_block, (xb, block_e)).reshape(rows, d)
    out = jnp.zeros((n + 1, d), jnp.float32).at[row_tok].add(yb.astype(jnp.float32) * row_gate[:, None])
    return out[:n].astype(h.dtype)


def hybrid_layer(x, c, pool_buf, conv_buf, s0, start_pos,
                 w_ada, b_ada, g_pre_mix, g_post_mix, g_pre_ffn, g_post_ffn,
                 w_in, w_pool, pool_scale, conv_w, a_log, dt_bias, gdn_norm_w, w_out,
                 router_w, router_b, w_gu, b_gu, w_down, b_down):
    bsz, t, _ = x.shape
    f32 = jnp.float32
    mod = (jax.nn.silu(c) @ w_ada + b_ada)[:, None, :]
    sh_mix, sc_mix, gt_mix, sh_ffn, sc_ffn, gt_ffn = jnp.split(mod, 6, axis=-1)

    h = rmsnorm(x, g_pre_mix) * (1.0 + sc_mix) + sh_mix
    proj = h @ w_in
    o1 = POOL_WIDTH
    o2 = o1 + CONV_CH
    o3 = o2 + GDN_WIDTH
    o4 = o3 + GDN_HEADS
    u, qkv, z, a, bb = proj[..., :o1], proj[..., o1:o2], proj[..., o2:o3], proj[..., o3:o4], proj[..., o4:]
    pool_out, pool_buf_new = pool_mixer(u, pool_buf, start_pos, w_pool, pool_scale)
    qkv_c, conv_buf_new = causal_conv(qkv, conv_buf, conv_w)
    qkv_c = qkv_c.astype(f32).reshape(bsz, t, 3, GDN_HEADS, GDN_HEAD_DIM)
    q = l2norm(qkv_c[:, :, 0]) * (GDN_HEAD_DIM ** -0.5)
    k = l2norm(qkv_c[:, :, 1])
    v = qkv_c[:, :, 2]
    g = -jnp.exp(a_log.astype(f32)) * jax.nn.softplus(a.astype(f32) + dt_bias.astype(f32))
    beta = jax.nn.sigmoid(bb.astype(f32))
    o, s_new = gated_delta_rule(q, k, v, g, beta, s0.astype(f32))
    zf = z.astype(f32).reshape(bsz, t, GDN_HEADS, GDN_HEAD_DIM)
    o = o * lax.rsqrt(jnp.mean(o * o, axis=-1, keepdims=True) + EPS) * gdn_norm_w.astype(f32) * jax.nn.silu(zf)
    mix = jnp.concatenate([pool_out, o.reshape(bsz, t, GDN_WIDTH).astype(x.dtype)], axis=-1) @ w_out
    x = x + gt_mix * rmsnorm(mix, g_post_mix)

    h = rmsnorm(x, g_pre_ffn) * (1.0 + sc_ffn) + sh_ffn
    f = moe_ffn(h.reshape(bsz * t, D_MODEL), router_w, router_b, w_gu, b_gu, w_down, b_down)
    x = x + gt_ffn * rmsnorm(f.reshape(bsz, t, D_MODEL), g_post_ffn)
    return x, pool_buf_new, conv_buf_new, s_new.astype(s0.dtype)


def setup_inputs(seed: int = 0) -> dict:
    key = jax.random.key(seed)
    ks = jax.random.split(key, 32)

    def nrm(k, shape, s):
        return jax.random.normal(k, shape, jnp.float32) * s

    L = DEPTH
    return {
        'x_prompt': nrm(ks[0], (BATCH, SEQ, D_MODEL), 1.0),
        'x_sample': nrm(ks[1], (DEC_BATCH, DEC_SEQ, D_MODEL), 1.0),
        'c_prompt': nrm(ks[2], (BATCH, D_MODEL), 1.0),
        'c_sample': nrm(ks[3], (DEC_BATCH, D_MODEL), 1.0),
        'state_pool': nrm(ks[4], (L, DEC_BATCH, POOL_BUF, POOL_WIDTH), 1.0),
        'state_conv': nrm(ks[5], (L, DEC_BATCH, CONV_WIDTH - 1, CONV_CH), 1.0),
        'state_ssm': nrm(ks[6], (L, DEC_BATCH, GDN_HEADS, GDN_HEAD_DIM, GDN_HEAD_DIM), 0.1),
        'w_ada': nrm(ks[7], (L, D_MODEL, 6 * D_MODEL), D_MODEL ** -0.5),
        'b_ada': nrm(ks[8], (L, 6 * D_MODEL), 0.02),
        'g_pre_mix': 1.0 + nrm(ks[9], (L, D_MODEL), 0.05),
        'g_post_mix': 1.0 + nrm(ks[10], (L, D_MODEL), 0.05),
        'g_pre_ffn': 1.0 + nrm(ks[11], (L, D_MODEL), 0.05),
        'g_post_ffn': 1.0 + nrm(ks[12], (L, D_MODEL), 0.05),
        'w_in': nrm(ks[13], (L, D_MODEL, IN_WIDTH), D_MODEL ** -0.5),
        'w_pool': nrm(ks[14], (L, N_POOL_GROUPS, POOL_GROUP, POOL_GROUP), POOL_GROUP ** -0.5),
        'pool_scale': 1.0 + nrm(ks[15], (L, POOL_WIDTH), 0.1),
        'conv_w': nrm(ks[16], (L, CONV_WIDTH, CONV_CH), CONV_WIDTH ** -0.5),
        'a_log': jnp.log(jax.random.uniform(ks[17], (L, GDN_HEADS), jnp.float32, 1.0, 16.0)),
        'dt_bias': nrm(ks[18], (L, GDN_HEADS), 0.1),
        'gdn_norm_w': 1.0 + nrm(ks[19], (L, GDN_HEAD_DIM), 0.05),
        'w_out': nrm(ks[20], (L, MIX_WIDTH, D_MODEL), MIX_WIDTH ** -0.5),
        'router_w': nrm(ks[21], (L, D_MODEL, N_EXPERTS), D_MODEL ** -0.5),
        'router_b': nrm(ks[22], (L, N_EXPERTS), 0.01),
        'w_gu': nrm(ks[23], (L, N_EXPERTS, D_MODEL, 2 * D_FF), D_MODEL ** -0.5),
        'b_gu': nrm(ks[24], (L, N_EXPERTS, 2 * D_FF), 0.01),
        'w_down': nrm(ks[25], (L, N_EXPERTS, D_FF, D_MODEL), D_FF ** -0.5),
        'b_down': nrm(ks[26], (L, N_EXPERTS, D_MODEL), 0.01),
    }


def reference(x_prompt, x_sample, c_prompt, c_sample, state_pool, state_conv, state_ssm,
              w_ada, b_ada, g_pre_mix, g_post_mix, g_pre_ffn, g_post_ffn,
              w_in, w_pool, pool_scale, conv_w, a_log, dt_bias, gdn_norm_w, w_out,
              router_w, router_b, w_gu, b_gu, w_down, b_down):
    y_prompt, y_sample = x_prompt, x_sample
    bp = x_prompt.shape[0]
    pool_p, conv_p, ssm_p, pool_s, conv_s, ssm_s = [], [], [], [], [], []
    for l in range(DEPTH):
        lp = (w_ada[l], b_ada[l], g_pre_mix[l], g_post_mix[l], g_pre_ffn[l], g_post_ffn[l],
              w_in[l], w_pool[l], pool_scale[l], conv_w[l], a_log[l], dt_bias[l], gdn_norm_w[l], w_out[l],
              router_w[l], router_b[l], w_gu[l], b_gu[l], w_down[l], b_down[l])
        zero_pool = jnp.zeros((bp, POOL_BUF, POOL_WIDTH), state_pool.dtype)
        zero_conv = jnp.zeros((bp, CONV_WIDTH - 1, CONV_CH), state_conv.dtype)
        zero_ssm = jnp.zeros((bp, GDN_HEADS, GDN_HEAD_DIM, GDN_HEAD_DIM), state_ssm.dtype)
        y_prompt, pp, cp, sp = hybrid_layer(y_prompt, c_prompt, zero_pool, zero_conv, zero_ssm, 0, *lp)
        y_sample, ps, cs, ss = hybrid_layer(y_sample, c_sample, state_pool[l], state_conv[l], state_ssm[l],
                                            PAST_LEN, *lp)
        pool_p.append(pp)
        conv_p.append(cp)
        ssm_p.append(sp)
        pool_s.append(ps)
        conv_s.append(cs)
        ssm_s.append(ss)
    return (y_prompt, y_sample, jnp.stack(pool_p), jnp.stack(conv_p), jnp.stack(ssm_p),
            jnp.stack(pool_s), jnp.stack(conv_s), jnp.stack(ssm_s))
```

```python
import functools
import math

import jax
import jax.numpy as jnp
from jax import lax
from jax.experimental import pallas as pl
from jax.experimental.pallas import tpu as pltpu

F32 = jnp.float32
BF16 = jnp.bfloat16

D_MODEL = 1024
PAST_LEN = 16384
POOL_WINDOWS = (2, 4, 8, 16)
POOL_WIDTH = D_MODEL // 4
POOL_GROUP = POOL_WIDTH // len(POOL_WINDOWS)
POOL_BUF = max(POOL_WINDOWS) - 1
GDN_WIDTH = D_MODEL - POOL_WIDTH
HEAD_DIM = 128
GDN_HEADS = GDN_WIDTH // HEAD_DIM
CONV_WIDTH = 4
CONV_CH = 3 * GDN_WIDTH
GDN_CHUNK = 64
N_EXPERTS = 32
TOP_K = 4
D_FF = D_MODEL
SWIGLU_LIMIT = 7.0
SWIGLU_ALPHA = 1.702
EPS = 1e-6

SUBLANES = 8
LANES = 128
ROW_TILES = D_MODEL // LANES
OFF_U = 0
OFF_QKV = POOL_WIDTH
OFF_Z = OFF_QKV + CONV_CH
OFF_AB = OFF_Z + GDN_WIDTH
PROJ_WIDTH = OFF_AB + LANES
MOE_BLOCK = 256
VMEM_LIMIT = 56 * 1024 * 1024


def _cparams(sem):
    return pltpu.CompilerParams(dimension_semantics=sem, vmem_limit_bytes=VMEM_LIMIT)


def _dot(a, b):
    return jnp.dot(a.astype(BF16), b.astype(BF16), preferred_element_type=F32)


def _dot_nt(a, b):
    return lax.dot_general(a.astype(BF16), b.astype(BF16), (((1,), (1,)), ((), ())),
                           preferred_element_type=F32)


def _silu(x):
    return x * jax.nn.sigmoid(x)


def _rms(x, g):
    return x * lax.rsqrt(jnp.mean(x * x, axis=-1, keepdims=True) + EPS) * g


def _ada_kernel(c_ref, w_ref, b_ref, o_ref):
    o_ref[...] = _dot(_silu(c_ref[...]), w_ref[...]) + b_ref[...]


def _ada(c, w_ada, b_ada):
    n = c.shape[0]
    cols = w_ada.shape[1]
    blk = D_MODEL
    return pl.pallas_call(
        _ada_kernel,
        grid=(cols // blk,),
        in_specs=[pl.BlockSpec((n, D_MODEL), lambda j: (0, 0)),
                  pl.BlockSpec((D_MODEL, blk), lambda j: (0, j)),
                  pl.BlockSpec((1, blk), lambda j: (0, j))],
        out_specs=pl.BlockSpec((n, blk), lambda j: (0, j)),
        out_shape=jax.ShapeDtypeStruct((n, cols), F32),
        compiler_params=_cparams(("arbitrary",)),
        name="ada",
    )(c, w_ada, b_ada.reshape(1, cols))


def _in_kernel(x_ref, sc_ref, sh_ref, g_ref, w_ref, o_ref):
    h = _rms(x_ref[0], g_ref[...]) * (1.0 + sc_ref[0]) + sh_ref[0]
    o_ref[0] = jnp.dot(h.astype(BF16), w_ref[...], preferred_element_type=F32)


def _mod_spec(per_token, tm):
    if per_token:
        return pl.BlockSpec((1, tm, D_MODEL), lambda b, t: (0, t, 0))
    return pl.BlockSpec((1, 1, D_MODEL), lambda b, t: (b, 0, 0))


def _in_proj(x, sc, sh, g, w_in_p, tm, per_token):
    bsz, t, _ = x.shape
    return pl.pallas_call(
        _in_kernel,
        grid=(bsz, t // tm),
        in_specs=[pl.BlockSpec((1, tm, D_MODEL), lambda b, i: (b, i, 0)),
                  _mod_spec(per_token, tm), _mod_spec(per_token, tm),
                  pl.BlockSpec((1, D_MODEL), lambda b, i: (0, 0)),
                  pl.BlockSpec((D_MODEL, PROJ_WIDTH), lambda b, i: (0, 0))],
        out_specs=pl.BlockSpec((1, tm, PROJ_WIDTH), lambda b, i: (b, i, 0)),
        out_shape=jax.ShapeDtypeStruct((bsz, t, PROJ_WIDTH), F32),
        compiler_params=_cparams(("arbitrary", "arbitrary")),
        name="in_proj",
    )(x, sc, sh, g.reshape(1, D_MODEL), w_in_p)


def _shift_rows(prev, x, d):
    if d == 0:
        return x
    e = jnp.concatenate([prev, x], axis=0)
    return pltpu.roll(e, d, axis=0)[SUBLANES:]


def _cumsum_rows(x):
    n = x.shape[0]
    row = lax.broadcasted_iota(jnp.int32, x.shape, 0)
    s = 1
    while s < n:
        x = x + jnp.where(row >= s, pltpu.roll(x, s, axis=0), 0.0)
        s *= 2
    return x


def _gdn_kernel(q_ref, k_ref, v_ref, z_ref, ab_ref, cq_ref, ck_ref, cv_ref, wq_ref, wk_ref, wv_ref,
                s0_ref, alog_ref, dtb_ref, nw_ref, o_ref, sn_ref, s_scr, prev_scr, *, nb, tm, tmp, chunk):
    h = pl.program_id(1)
    ti = pl.program_id(2)
    nt = pl.num_programs(2)
    n_chunks = tmp // chunk
    n_sq = int(math.log2(chunk)) - 1

    @pl.when(ti == 0)
    def _():
        s_scr[...] = s0_ref[:, 0]
        for b in range(nb):
            for si, c_ref in enumerate((cq_ref, ck_ref, cv_ref)):
                prev_scr[b, si] = jnp.concatenate(
                    [jnp.zeros((SUBLANES - (CONV_WIDTH - 1), HEAD_DIM), F32), c_ref[b]], axis=0)

    lane = lax.broadcasted_iota(jnp.int32, (tmp, LANES), 1)
    rowi = lax.broadcasted_iota(jnp.int32, (tmp, LANES), 0)
    ri = lax.broadcasted_iota(jnp.int32, (chunk, chunk), 0)
    ci = lax.broadcasted_iota(jnp.int32, (chunk, chunk), 1)
    tril = ri >= ci
    strict = ri > ci
    eye = (ri == ci).astype(F32)
    neg_a = -jnp.exp(alog_ref[...])
    dtb = dtb_ref[...]
    nw = nw_ref[...]

    def padrows(x):
        if tm == tmp:
            return x
        return jnp.concatenate([x, jnp.zeros((tmp - tm, x.shape[1]), F32)], axis=0)

    for b in range(nb):
        def conv(x_ref, w_ref, si):
            x = padrows(x_ref[b])
            prev = prev_scr[b, si]
            w = w_ref[...]
            acc = x * w[CONV_WIDTH - 1:CONV_WIDTH]
            for d in range(1, CONV_WIDTH):
                acc = acc + _shift_rows(prev, x, d) * w[CONV_WIDTH - 1 - d:CONV_WIDTH - d]
            if tm >= SUBLANES:
                prev_scr[b, si] = x[tm - SUBLANES:tm]
            return _silu(acc)

        qc = conv(q_ref, wq_ref, 0)
        kc = conv(k_ref, wk_ref, 1)
        v = conv(v_ref, wv_ref, 2)
        q = qc * lax.rsqrt(jnp.sum(qc * qc, axis=-1, keepdims=True) + EPS) * (HEAD_DIM ** -0.5)
        k = kc * lax.rsqrt(jnp.sum(kc * kc, axis=-1, keepdims=True) + EPS)
        ab = padrows(ab_ref[b])
        g_all = neg_a * jax.nn.softplus(ab + dtb)
        beta_all = jax.nn.sigmoid(ab)
        valid = rowi < tm
        g_col = jnp.sum(jnp.where((lane == h) & valid, g_all, 0.0), axis=-1, keepdims=True)
        beta_col = jnp.sum(jnp.where((lane == GDN_HEADS + h) & valid, beta_all, 0.0), axis=-1, keepdims=True)
        gb = jnp.broadcast_to(g_col, (tmp, LANES))
        bb = jnp.broadcast_to(beta_col, (tmp, LANES))
        zf = padrows(z_ref[b])

        parts = []
        for c in range(n_chunks):
            sl = slice(c * chunk, (c + 1) * chunk)
            qx, kx, vx, bx = q[sl], k[sl], v[sl], bb[sl]
            gcb = _cumsum_rows(gb[sl])
            gct = jnp.transpose(gcb)[:chunk]
            diff = gcb[:, :chunk] - gct
            decay = jnp.where(tril, jnp.exp(jnp.where(tril, diff, 0.0)), 0.0)
            kb = kx * bx
            vb = vx * bx
            amat = jnp.where(strict, _dot_nt(kb, kx) * decay, 0.0)
            qk = jnp.where(tril, _dot_nt(qx, kx) * decay, 0.0)
            inv = eye - amat
            p = amat
            for _ in range(n_sq):
                p = _dot(p, p)
                inv = inv + _dot(inv, p)
            eg = jnp.exp(gcb)
            u = _dot(inv, vb)
            w = _dot(inv, kb * eg)
            gl = gcb[chunk - 1:chunk]
            kdt = jnp.transpose(kx * jnp.exp(gl - gcb))
            mp = _dot(kdt, w)
            nc = _dot(kdt, u)
            qp = qx * eg - _dot(qk, w)
            oc = _dot(qk, u)
            parts.append((qp, oc, mp, nc, jnp.exp(gl)))

        s = s_scr[b]
        outs = []
        for qp, oc, mp, nc, egl in parts:
            outs.append(_dot(qp, s) + oc)
            s = s * egl - _dot(mp, s) + nc
        s_scr[b] = s
        o = outs[0] if n_chunks == 1 else jnp.concatenate(outs, axis=0)
        o = o * lax.rsqrt(jnp.mean(o * o, axis=-1, keepdims=True) + EPS) * nw * _silu(zf)
        o_ref[b] = o[:tm]

    @pl.when(ti == nt - 1)
    def _():
        sn_ref[:, 0] = s_scr[...]


def _gdn(proj, state_conv, s0, conv_w, alog_row, dtb_row, norm_w, nb, tm):
    bsz, t, _ = proj.shape
    tmp = max(tm, SUBLANES)
    chunk = min(GDN_CHUNK, tmp)
    qb, zb, abb = OFF_QKV // LANES, OFF_Z // LANES, OFF_AB // LANES

    def pspec(base):
        return pl.BlockSpec((nb, tm, LANES), lambda b, h, i: (b, i, base + h))

    def cspec(base):
        return pl.BlockSpec((nb, CONV_WIDTH - 1, LANES), lambda b, h, i: (b, 0, base + h))

    def wspec(base):
        return pl.BlockSpec((CONV_WIDTH, LANES), lambda b, h, i: (0, base + h))

    row = pl.BlockSpec((1, LANES), lambda b, h, i: (0, 0))
    sspec = pl.BlockSpec((nb, 1, HEAD_DIM, HEAD_DIM), lambda b, h, i: (b, h, 0, 0))
    kern = functools.partial(_gdn_kernel, nb=nb, tm=tm, tmp=tmp, chunk=chunk)
    return pl.pallas_call(
        kern,
        grid=(bsz // nb, GDN_HEADS, t // tm),
        in_specs=[pspec(qb), pspec(qb + GDN_HEADS), pspec(qb + 2 * GDN_HEADS), pspec(zb),
                  pl.BlockSpec((nb, tm, LANES), lambda b, h, i: (b, i, abb)),
                  cspec(0), cspec(GDN_HEADS), cspec(2 * GDN_HEADS),
                  wspec(0), wspec(GDN_HEADS), wspec(2 * GDN_HEADS),
                  sspec, row, row, row],
        out_specs=[pl.BlockSpec((nb, tm, LANES), lambda b, h, i: (b, i, h)), sspec],
        out_shape=[jax.ShapeDtypeStruct((bsz, t, GDN_WIDTH), F32),
                   jax.ShapeDtypeStruct(s0.shape, F32)],
        scratch_shapes=[pltpu.VMEM((nb, HEAD_DIM, HEAD_DIM), F32),
                        pltpu.VMEM((nb, 3, SUBLANES, HEAD_DIM), F32)],
        compiler_params=_cparams(("arbitrary", "arbitrary", "arbitrary")),
        name="gdn",
    )(proj, proj, proj, proj, proj, state_conv, state_conv, state_conv, conv_w, conv_w, conv_w,
      s0, alog_row, dtb_row, norm_w)


def _pool_kernel(u_ref, st_ref, w_ref, sc_ref, o_ref, prev_scr, *, nb, tm, tmp, start_pos):
    ti = pl.program_id(1)
    halo = 2 * SUBLANES

    @pl.when(ti == 0)
    def _():
        for b in range(nb):
            prev_scr[b] = jnp.concatenate([jnp.zeros((halo - POOL_BUF, POOL_WIDTH), F32), st_ref[b]], axis=0)

    lane = lax.broadcasted_iota(jnp.int32, (tmp, POOL_WIDTH), 1)
    rowi = lax.broadcasted_iota(jnp.int32, (tmp, POOL_WIDTH), 0)
    pos1 = start_pos + ti * tm + rowi + 1
    grp = lane // POOL_GROUP
    win = jnp.where(grp == 0, POOL_WINDOWS[0],
                    jnp.where(grp == 1, POOL_WINDOWS[1], jnp.where(grp == 2, POOL_WINDOWS[2], POOL_WINDOWS[3])))
    cnt = jnp.minimum(pos1, win).astype(F32)
    for b in range(nb):
        x = u_ref[b]
        if tm != tmp:
            x = jnp.concatenate([x, jnp.zeros((tmp - tm, POOL_WIDTH), F32)], axis=0)
        e = jnp.concatenate([prev_scr[b], x], axis=0)
        sums = []
        s = e
        for w in POOL_WINDOWS:
            s = s + pltpu.roll(s, w // 2, axis=0)
            sums.append(s[halo:])
        sel = jnp.where(grp == 0, sums[0], jnp.where(grp == 1, sums[1], jnp.where(grp == 2, sums[2], sums[3])))
        pooled = sel / cnt - x
        out = _dot(pooled, w_ref[...]) * sc_ref[...]
        o_ref[b] = out[:tm]
        if tm >= halo:
            prev_scr[b] = x[tm - halo:tm]


def _pool(proj, state_pool, w_bd, pool_scale, nb, tm, start_pos):
    bsz, t, _ = proj.shape
    tmp = max(tm, SUBLANES)
    kern = functools.partial(_pool_kernel, nb=nb, tm=tm, tmp=tmp, start_pos=start_pos)
    return pl.pallas_call(
        kern,
        grid=(bsz // nb, t // tm),
        in_specs=[pl.BlockSpec((nb, tm, POOL_WIDTH), lambda b, i: (b, i, 0)),
                  pl.BlockSpec((nb, POOL_BUF, POOL_WIDTH), lambda b, i: (b, 0, 0)),
                  pl.BlockSpec((POOL_WIDTH, POOL_WIDTH), lambda b, i: (0, 0)),
                  pl.BlockSpec((1, POOL_WIDTH), lambda b, i: (0, 0))],
        out_specs=pl.BlockSpec((nb, tm, POOL_WIDTH), lambda b, i: (b, i, 0)),
        out_shape=jax.ShapeDtypeStruct((bsz, t, POOL_WIDTH), F32),
        scratch_shapes=[pltpu.VMEM((nb, 2 * SUBLANES, POOL_WIDTH), F32)],
        compiler_params=_cparams(("arbitrary", "arbitrary")),
        name="pool",
    )(proj, state_pool, w_bd, pool_scale.reshape(1, POOL_WIDTH))


def _out_kernel(x_ref, po_ref, og_ref, gt_ref, sc_ref, sh_ref, gpm_ref, gpf_ref, wo_ref, rw_ref, rb_ref,
                x1_ref, ht_ref, g_ref, *, tm):
    mix = _dot(po_ref[0], wo_ref[:POOL_WIDTH]) + _dot(og_ref[0], wo_ref[POOL_WIDTH:])
    x1 = x_ref[0] + gt_ref[0] * _rms(mix, gpm_ref[...])
    x1_ref[0] = x1
    h = _rms(x1, gpf_ref[...]) * (1.0 + sc_ref[0]) + sh_ref[0]
    for s in range(ROW_TILES):
        ht_ref[pl.ds(s, tm, stride=ROW_TILES), :] = h[:, s * LANES:(s + 1) * LANES]
    logits = _dot(h, rw_ref[...]) + rb_ref[...]
    lane = lax.broadcasted_iota(jnp.int32, logits.shape, 1)
    work = logits
    sel = jnp.zeros(logits.shape, jnp.bool_)
    top = None
    den = jnp.zeros((tm, 1), F32)
    for j in range(TOP_K):
        m = jnp.max(work, axis=-1, keepdims=True)
        idx = jnp.min(jnp.where(work == m, lane, N_EXPERTS), axis=-1, keepdims=True)
        hit = lane == idx
        sel = sel | hit
        work = jnp.where(hit, -jnp.inf, work)
        if j == 0:
            top = m
        den = den + jnp.exp(m - top)
    gates = jnp.exp(logits - top) / den
    g_ref[...] = jnp.where(sel, gates, -1.0)


def _out_proj(x, pool_out, o_gdn, gt, sc, sh, g_post_mix, g_pre_ffn, w_out_b, router_w, router_b, tm, per_token):
    bsz, t, _ = x.shape
    n = bsz * t
    tiles = t // tm

    def tok(width):
        return pl.BlockSpec((1, tm, width), lambda b, i: (b, i, 0))

    def const(shape):
        return pl.BlockSpec(shape, lambda b, i: tuple(0 for _ in shape))

    kern = functools.partial(_out_kernel, tm=tm)
    return pl.pallas_call(
        kern,
        grid=(bsz, tiles),
        in_specs=[tok(D_MODEL), tok(POOL_WIDTH), tok(GDN_WIDTH),
                  _mod_spec(per_token, tm), _mod_spec(per_token, tm), _mod_spec(per_token, tm),
                  const((1, D_MODEL)), const((1, D_MODEL)), const((D_MODEL, D_MODEL)),
                  const((D_MODEL, N_EXPERTS)), const((1, N_EXPERTS))],
        out_specs=[tok(D_MODEL),
                   pl.BlockSpec((tm * ROW_TILES, LANES), lambda b, i: (b * tiles + i, 0)),
                   pl.BlockSpec((tm, N_EXPERTS), lambda b, i: (b * tiles + i, 0))],
        out_shape=[jax.ShapeDtypeStruct((bsz, t, D_MODEL), F32),
                   jax.ShapeDtypeStruct((n * ROW_TILES, LANES), F32),
                   jax.ShapeDtypeStruct((n, N_EXPERTS), F32)],
        compiler_params=_cparams(("arbitrary", "arbitrary")),
        name="out_proj",
    )(x, pool_out, o_gdn, gt, sc, sh, g_post_mix.reshape(1, D_MODEL), g_pre_ffn.reshape(1, D_MODEL),
      w_out_b, router_w, router_b.reshape(1, N_EXPERTS))


def _rank_kernel(g_ref, r_ref, cnt_ref, carry, *, tm):
    i = pl.program_id(0)

    @pl.when(i == 0)
    def _():
        carry[...] = jnp.zeros_like(carry)

    mask = (g_ref[...] >= 0.0).astype(F32)
    ri = lax.broadcasted_iota(jnp.int32, (tm, tm), 0)
    ci = lax.broadcasted_iota(jnp.int32, (tm, tm), 1)
    before = (ri > ci).astype(F32)
    c = carry[...]
    r_ref[...] = _dot(before, mask) + c
    c = c + jnp.sum(mask, axis=0, keepdims=True)
    carry[...] = c
    cnt_ref[...] = c


def _rank(gsel, tm):
    n = gsel.shape[0]
    kern = functools.partial(_rank_kernel, tm=tm)
    return pl.pallas_call(
        kern,
        grid=(n // tm,),
        in_specs=[pl.BlockSpec((tm, N_EXPERTS), lambda i: (i, 0))],
        out_specs=[pl.BlockSpec((tm, N_EXPERTS), lambda i: (i, 0)),
                   pl.BlockSpec((1, N_EXPERTS), lambda i: (0, 0))],
        out_shape=[jax.ShapeDtypeStruct((n, N_EXPERTS), F32),
                   jax.ShapeDtypeStruct((1, N_EXPERTS), F32)],
        scratch_shapes=[pltpu.VMEM((1, N_EXPERTS), F32)],
        compiler_params=_cparams(("arbitrary",)),
        name="rank",
    )(gsel)


def _dest_kernel(g_ref, r_ref, start_ref, d_ref, gt_ref, *, tm):
    g = g_ref[...]
    mask = g >= 0.0
    maskf = mask.astype(F32)
    dest = start_ref[...] + r_ref[...]
    ri = lax.broadcasted_iota(jnp.int32, (N_EXPERTS, N_EXPERTS), 0)
    ci = lax.broadcasted_iota(jnp.int32, (N_EXPERTS, N_EXPERTS), 1)
    slot = _dot(maskf, (ri < ci).astype(F32))
    lane4 = lax.broadcasted_iota(jnp.int32, (tm, TOP_K), 1)
    d_out = jnp.zeros((tm, TOP_K), F32)
    g_out = jnp.zeros((tm, TOP_K), F32)
    for j in range(TOP_K):
        pick = mask & (slot == float(j))
        dj = jnp.sum(jnp.where(pick, dest, 0.0), axis=-1, keepdims=True)
        gj = jnp.sum(jnp.where(pick, g, 0.0), axis=-1, keepdims=True)
        d_out = jnp.where(lane4 == j, dj, d_out)
        g_out = jnp.where(lane4 == j, gj, g_out)
    d_ref[...] = d_out.astype(jnp.int32)
    gt_ref[...] = g_out


def _dest(gsel, rank, start_row, tm):
    n = gsel.shape[0]
    kern = functools.partial(_dest_kernel, tm=tm)
    return pl.pallas_call(
        kern,
        grid=(n // tm,),
        in_specs=[pl.BlockSpec((tm, N_EXPERTS), lambda i: (i, 0)),
                  pl.BlockSpec((tm, N_EXPERTS), lambda i: (i, 0)),
                  pl.BlockSpec((1, N_EXPERTS), lambda i: (0, 0))],
        out_specs=[pl.BlockSpec((tm, TOP_K), lambda i: (i, 0)),
                   pl.BlockSpec((tm, TOP_K), lambda i: (i, 0))],
        out_shape=[jax.ShapeDtypeStruct((n, TOP_K), jnp.int32),
                   jax.ShapeDtypeStruct((n, TOP_K), F32)],
        compiler_params=_cparams(("arbitrary",)),
        name="dest",
    )(gsel, rank, start_row)


def _row_copy(src, dst, s_row, d_row, sem):
    return pltpu.make_async_copy(src.at[pl.ds(pl.multiple_of(s_row * ROW_TILES, ROW_TILES), ROW_TILES), :],
                                 dst.at[pl.ds(pl.multiple_of(d_row * ROW_TILES, ROW_TILES), ROW_TILES), :], sem)


def _drain(src, dst, sem, rows):
    pltpu.make_async_copy(src.at[pl.ds(0, rows * ROW_TILES), :], dst.at[pl.ds(0, rows * ROW_TILES), :], sem).wait()


def _scatter_kernel(cnt_ref, start_ref, end_ref, nu_ref, dest_ref, ht_ref, xs_ref, sem, *, tm, bm, n_blocks):
    i = pl.program_id(0)
    base = i * tm

    def body(j, carry):
        _row_copy(ht_ref, xs_ref, base + (j >> 2), dest_ref[j], sem).start()
        return carry

    lax.fori_loop(0, tm * TOP_K, body, 0)
    _drain(ht_ref, xs_ref, sem, tm * TOP_K)

    @pl.when(i == pl.num_programs(0) - 1)
    def _():
        def block_copy(b):
            return pltpu.make_async_copy(
                ht_ref.at[pl.ds(0, bm * ROW_TILES), :],
                xs_ref.at[pl.ds(pl.multiple_of(b * (bm * ROW_TILES), bm * ROW_TILES), bm * ROW_TILES), :], sem)

        def tail_start(b, carry):
            block_copy(b).start()
            return carry

        def tail_wait(b, carry):
            block_copy(b).wait()
            return carry

        lax.fori_loop(nu_ref[0], n_blocks, tail_start, 0)
        lax.fori_loop(nu_ref[0], n_blocks, tail_wait, 0)

        def per_expert(e, carry):
            lo = start_ref[e] + cnt_ref[e]
            hi = end_ref[e]

            def fill(r, c2):
                _row_copy(ht_ref, xs_ref, 0, r, sem).start()
                return c2

            lax.fori_loop(lo, hi, fill, 0)

            def drain(r, c2):
                _row_copy(ht_ref, xs_ref, 0, r, sem).wait()
                return c2

            lax.fori_loop(lo, hi, drain, 0)
            return carry

        lax.fori_loop(0, N_EXPERTS, per_expert, 0)


def _scatter_rows(ht, dest_flat, cnt, start, end, n_used, n_blocks, tm, bm):
    n = ht.shape[0] // ROW_TILES
    rows = n_blocks * bm
    assert n >= bm
    kern = functools.partial(_scatter_kernel, tm=tm, bm=bm, n_blocks=n_blocks)
    return pl.pallas_call(
        kern,
        grid_spec=pltpu.PrefetchScalarGridSpec(
            num_scalar_prefetch=4,
            grid=(n // tm,),
            in_specs=[pl.BlockSpec((tm * TOP_K,), lambda i, *_: (i,), memory_space=pltpu.SMEM),
                      pl.BlockSpec(memory_space=pl.ANY)],
            out_specs=pl.BlockSpec(memory_space=pl.ANY),
            scratch_shapes=[pltpu.SemaphoreType.DMA(())]),
        out_shape=jax.ShapeDtypeStruct((rows * ROW_TILES, LANES), F32),
        compiler_params=_cparams(("arbitrary",)),
        name="scatter_rows",
    )(cnt, start, end, n_used, dest_flat, ht)


def _gather_kernel(dest_ref, ys_ref, sl_ref, sem, *, tm):
    base = pl.program_id(0) * tm * TOP_K

    def body(j, carry):
        _row_copy(ys_ref, sl_ref, dest_ref[j], base + j, sem).start()
        return carry

    lax.fori_loop(0, tm * TOP_K, body, 0)
    _drain(ys_ref, sl_ref, sem, tm * TOP_K)


def _gather_rows(ys, dest_flat, tm):
    n = dest_flat.shape[0] // TOP_K
    kern = functools.partial(_gather_kernel, tm=tm)
    return pl.pallas_call(
        kern,
        grid=(n // tm,),
        in_specs=[pl.BlockSpec((tm * TOP_K,), lambda i: (i,), memory_space=pltpu.SMEM),
                  pl.BlockSpec(memory_space=pl.ANY)],
        out_specs=pl.BlockSpec(memory_space=pl.ANY),
        out_shape=jax.ShapeDtypeStruct((n * TOP_K * ROW_TILES, LANES), F32),
        scratch_shapes=[pltpu.SemaphoreType.DMA(())],
        compiler_params=_cparams(("arbitrary",)),
        name="gather_rows",
    )(dest_flat, ys)


def _expert_kernel(be_ref, nu_ref, xs_ref, wg_ref, bg_ref, wd_ref, bd_ref, ys_ref, wg_s, wd_s, *, bm, fchunk):
    j = pl.program_id(0)
    prev = be_ref[jnp.maximum(j - 1, 0)]
    fresh = (j == 0) | (be_ref[j] != prev)
    active = j < nu_ref[0]

    @pl.when(active & fresh)
    def _():
        wg_s[...] = wg_ref[0].astype(BF16)
        wd_s[...] = wd_ref[0].astype(BF16)

    @pl.when(active)
    def _():
        x = jnp.concatenate([xs_ref[pl.ds(s, bm, stride=ROW_TILES), :] for s in range(ROW_TILES)], axis=-1)
        xb = x.astype(BF16)
        acc = jnp.zeros((bm, D_MODEL), F32) + bd_ref[0]
        for f in range(D_FF // fchunk):
            lo = f * fchunk
            gate = jnp.dot(xb, wg_s[:, lo:lo + fchunk], preferred_element_type=F32) + bg_ref[0, :, lo:lo + fchunk]
            up = (jnp.dot(xb, wg_s[:, D_FF + lo:D_FF + lo + fchunk], preferred_element_type=F32)
                  + bg_ref[0, :, D_FF + lo:D_FF + lo + fchunk])
            gate = jnp.minimum(gate, SWIGLU_LIMIT)
            up = jnp.clip(up, -SWIGLU_LIMIT, SWIGLU_LIMIT)
            act = (up + 1.0) * (gate * jax.nn.sigmoid(SWIGLU_ALPHA * gate))
            acc = acc + jnp.dot(act.astype(BF16), wd_s[lo:lo + fchunk, :], preferred_element_type=F32)
        for s in range(ROW_TILES):
            ys_ref[pl.ds(s, bm, stride=ROW_TILES), :] = acc[:, s * LANES:(s + 1) * LANES]

    @pl.when(jnp.logical_not(active))
    def _():
        ys_ref[...] = jnp.zeros_like(ys_ref)


def _experts(xs, block_expert, n_used, w_gu, b_gu, w_down, b_down, bm):
    rows = xs.shape[0] // ROW_TILES
    n_blocks = rows // bm

    def blk(j, be, nu):
        return (jnp.minimum(j, nu[0] - 1), 0)

    def wsel(j, be, nu):
        return (be[j], 0, 0)

    kern = functools.partial(_expert_kernel, bm=bm, fchunk=512)
    return pl.pallas_call(
        kern,
        grid_spec=pltpu.PrefetchScalarGridSpec(
            num_scalar_prefetch=2,
            grid=(n_blocks,),
            in_specs=[pl.BlockSpec((bm * ROW_TILES, LANES), blk),
                      pl.BlockSpec((1, D_MODEL, 2 * D_FF), wsel),
                      pl.BlockSpec((1, 1, 2 * D_FF), wsel),
                      pl.BlockSpec((1, D_FF, D_MODEL), wsel),
                      pl.BlockSpec((1, 1, D_MODEL), wsel)],
            out_specs=pl.BlockSpec((bm * ROW_TILES, LANES), lambda j, be, nu: (j, 0)),
            scratch_shapes=[pltpu.VMEM((D_MODEL, 2 * D_FF), BF16),
                            pltpu.VMEM((D_FF, D_MODEL), BF16)]),
        out_shape=jax.ShapeDtypeStruct((rows * ROW_TILES, LANES), F32),
        compiler_params=_cparams(("arbitrary",)),
        name="experts",
    )(block_expert, n_used, xs, w_gu, b_gu.reshape(N_EXPERTS, 1, 2 * D_FF), w_down,
      b_down.reshape(N_EXPERTS, 1, D_MODEL))


def _final_kernel(sl_ref, gate_ref, x1_ref, gt_ref, g_ref, o_ref, *, tm):
    gates = gate_ref[...]
    per_tok = TOP_K * ROW_TILES
    cols = []
    for s in range(ROW_TILES):
        acc = None
        for j in range(TOP_K):
            part = sl_ref[pl.ds(j * ROW_TILES + s, tm, stride=per_tok), :] * gates[:, j:j + 1]
            acc = part if acc is None else acc + part
        cols.append(acc)
    f = jnp.concatenate(cols, axis=-1)
    o_ref[0] = x1_ref[0] + gt_ref[0] * _rms(f, g_ref[...])


def _final(slots, gates, x1, gt, g_post_ffn, tok_off, tm, per_token):
    bsz, t, _ = x1.shape
    tiles = t // tm
    off = tok_off // tm
    kern = functools.partial(_final_kernel, tm=tm)
    return pl.pallas_call(
        kern,
        grid=(bsz, tiles),
        in_specs=[pl.BlockSpec((tm * TOP_K * ROW_TILES, LANES), lambda b, i: (off + b * tiles + i, 0)),
                  pl.BlockSpec((tm, TOP_K), lambda b, i: (off + b * tiles + i, 0)),
                  pl.BlockSpec((1, tm, D_MODEL), lambda b, i: (b, i, 0)),
                  _mod_spec(per_token, tm),
                  pl.BlockSpec((1, D_MODEL), lambda b, i: (0, 0))],
        out_specs=pl.BlockSpec((1, tm, D_MODEL), lambda b, i: (b, i, 0)),
        out_shape=jax.ShapeDtypeStruct((bsz, t, D_MODEL), F32),
        compiler_params=_cparams(("arbitrary", "arbitrary")),
        name="final",
    )(slots, gates, x1, gt, g_post_ffn.reshape(1, D_MODEL))


def _moe(ht, gsel, w_gu, b_gu, w_down, b_down, tm, bm):
    n = gsel.shape[0]
    rank, counts = _rank(gsel, tm)
    cnt = counts[0].astype(jnp.int32)
    nblk = (cnt + bm - 1) // bm
    blk_end = jnp.cumsum(nblk)
    blk_start = blk_end - nblk
    n_blocks = -(-(n * TOP_K) // bm) + N_EXPERTS
    block_expert = jnp.minimum(
        jnp.searchsorted(blk_end, jnp.arange(n_blocks, dtype=jnp.int32), side="right"), N_EXPERTS - 1).astype(jnp.int32)
    n_used = blk_end[-1:].astype(jnp.int32)
    row_start = (blk_start * bm).astype(jnp.int32)
    row_end = (blk_end * bm).astype(jnp.int32)
    dest, gates = _dest(gsel, rank, row_start.astype(F32).reshape(1, N_EXPERTS), tm)
    dest_flat = dest.reshape(n * TOP_K)
    xs = _scatter_rows(ht, dest_flat, cnt, row_start, row_end, n_used, n_blocks, tm, bm)
    ys = _experts(xs, block_expert, n_used, w_gu, b_gu, w_down, b_down, bm)
    slots = _gather_rows(ys, dest_flat, tm)
    return slots, gates


def _mix_path(x, mod, pool_buf, conv_buf, s0, start_pos, lp, tm, nb, per_token):
    (g_pre_mix, g_post_mix, g_pre_ffn, w_in_p, w_bd, pool_scale, conv_w, alog_row, dtb_row, norm_w,
     w_out_b, router_w, router_b) = lp
    bsz, t, _ = x.shape
    sh_mix, sc_mix, gt_mix, sh_ffn, sc_ffn, gt_ffn = (mod[:, i] for i in range(6))

    def shape_mod(m):
        if per_token:
            return jnp.repeat(m, t, axis=0).reshape(1, bsz * t, D_MODEL)
        return m.reshape(bsz, 1, D_MODEL)

    sh_mix, sc_mix, gt_mix, sh_ffn, sc_ffn, gt_ffn = map(shape_mod, (sh_mix, sc_mix, gt_mix, sh_ffn, sc_ffn, gt_ffn))
    xf = x.reshape(1, bsz * t, D_MODEL) if per_token else x
    proj = _in_proj(xf, sc_mix, sh_mix, g_pre_mix, w_in_p, tm, per_token).reshape(bsz, t, PROJ_WIDTH)
    tseq = min(tm, t)
    o_gdn, s_new = _gdn(proj, conv_buf, s0, conv_w, alog_row, dtb_row, norm_w, nb, tseq)
    pool_out = _pool(proj, pool_buf, w_bd, pool_scale, nb, tseq, start_pos)
    fl = (lambda a: a.reshape(1, bsz * t, a.shape[-1])) if per_token else (lambda a: a)
    x1, ht, gsel = _out_proj(xf, fl(pool_out), fl(o_gdn), gt_mix, sc_ffn, sh_ffn, g_post_mix, g_pre_ffn,
                             w_out_b, router_w, router_b, tm, per_token)
    pool_new = jnp.concatenate([pool_buf, proj[:, :, OFF_U:OFF_U + POOL_WIDTH]], axis=1)[:, -POOL_BUF:]
    conv_new = jnp.concatenate([conv_buf, proj[:, :, OFF_QKV:OFF_QKV + CONV_CH]], axis=1)[:, -(CONV_WIDTH - 1):]
    return x1, ht, gsel, gt_ffn, pool_new, conv_new, s_new


def kernel(x_prompt, x_sample, c_prompt, c_sample, state_pool, state_conv, state_ssm, w_ada, b_ada, g_pre_mix, g_post_mix, g_pre_ffn, g_post_ffn, w_in, w_pool, pool_scale, conv_w, a_log, dt_bias, gdn_norm_w, w_out, router_w, router_b, w_gu, b_gu, w_down, b_down):
    depth = w_ada.shape[0]
    bp, tp, _ = x_prompt.shape
    bs, ts, _ = x_sample.shape
    tm = 256
    y_p, y_s = x_prompt, x_sample
    outs = [[] for _ in range(6)]
    for l in range(depth):
        mod = _ada(jnp.concatenate([c_prompt, c_sample], axis=0), w_ada[l], b_ada[l]).reshape(bp + bs, 6, D_MODEL)
        w_in_p = jnp.pad(w_in[l], ((0, 0), (0, PROJ_WIDTH - w_in.shape[2]))).astype(BF16)
        w_bd = jax.scipy.linalg.block_diag(*[w_pool[l, g] for g in range(len(POOL_WINDOWS))])
        pad_row = lambda v: jnp.pad(v, (0, LANES - v.shape[0])).reshape(1, LANES)
        lp = (g_pre_mix[l], g_post_mix[l], g_pre_ffn[l], w_in_p, w_bd, pool_scale[l], conv_w[l],
              pad_row(a_log[l]), pad_row(dt_bias[l]), gdn_norm_w[l].reshape(1, HEAD_DIM),
              w_out[l].astype(BF16), router_w[l], router_b[l])
        zero_pool = jnp.zeros((bp, POOL_BUF, POOL_WIDTH), F32)
        zero_conv = jnp.zeros((bp, CONV_WIDTH - 1, CONV_CH), F32)
        zero_ssm = jnp.zeros((bp, GDN_HEADS, HEAD_DIM, HEAD_DIM), F32)
        x1p, htp, gp, gtp, pp, cp, sp = _mix_path(y_p, mod[:bp], zero_pool, zero_conv, zero_ssm, 0, lp, tm, 1, False)
        x1s, hts, gs, gts, ps, cs, ss = _mix_path(y_s, mod[bp:], state_pool[l], state_conv[l], state_ssm[l],
                                                  PAST_LEN, lp, tm, 8, True)
        ht = jnp.concatenate([htp, hts], axis=0)
        gsel = jnp.concatenate([gp, gs], axis=0)
        slots, gates = _moe(ht, gsel, w_gu[l], b_gu[l], w_down[l], b_down[l], tm, MOE_BLOCK)
        y_p = _final(slots, gates, x1p, gtp, g_post_ffn[l], 0, tm, False)
        y_s = _final(slots, gates, x1s, gts, g_post_ffn[l], bp * tp, tm, True).reshape(bs, ts, D_MODEL)
        for lst, val in zip(outs, (pp, cp, sp, ps, cs, ss)):
            lst.append(val)
    return (y_p, y_s) + tuple(jnp.stack(o) for o in outs)
```

```python
import functools
import math

import jax
import jax.numpy as jnp
from jax import lax
from jax.experimental import pallas as pl
from jax.experimental.pallas import tpu as pltpu

F32 = jnp.float32
BF16 = jnp.bfloat16

D_MODEL = 1024
PAST_LEN = 16384
POOL_WINDOWS = (2, 4, 8, 16)
POOL_WIDTH = D_MODEL // 4
POOL_GROUP = POOL_WIDTH // len(POOL_WINDOWS)
POOL_BUF = max(POOL_WINDOWS) - 1
GDN_WIDTH = D_MODEL - POOL_WIDTH
HEAD_DIM = 128
GDN_HEADS = GDN_WIDTH // HEAD_DIM
CONV_WIDTH = 4
CONV_CH = 3 * GDN_WIDTH
GDN_CHUNK = 64
N_EXPERTS = 32
TOP_K = 4
D_FF = D_MODEL
SWIGLU_LIMIT = 7.0
SWIGLU_ALPHA = 1.702
EPS = 1e-6

SUBLANES = 8
LANES = 128
ROW_TILES = D_MODEL // LANES
OFF_U = 0
OFF_QKV = POOL_WIDTH
OFF_Z = OFF_QKV + CONV_CH
OFF_AB = OFF_Z + GDN_WIDTH
PROJ_WIDTH = OFF_AB + LANES
MOE_BLOCK = 256
VMEM_LIMIT = 56 * 1024 * 1024


def _cparams(sem):
    return pltpu.CompilerParams(dimension_semantics=sem, vmem_limit_bytes=VMEM_LIMIT)


def _dot(a, b):
    return jnp.dot(a.astype(BF16), b.astype(BF16), preferred_element_type=F32)


def _dot_nt(a, b):
    return lax.dot_general(a.astype(BF16), b.astype(BF16), (((1,), (1,)), ((), ())),
                           preferred_element_type=F32)


def _silu(x):
    return x * jax.nn.sigmoid(x)


def _rms(x, g):
    return x * lax.rsqrt(jnp.mean(x * x, axis=-1, keepdims=True) + EPS) * g


def _ada_kernel(c_ref, w_ref, b_ref, o_ref):
    o_ref[...] = _dot(_silu(c_ref[...]), w_ref[...]) + b_ref[...]


def _ada(c, w_ada, b_ada):
    n = c.shape[0]
    cols = w_ada.shape[1]
    blk = D_MODEL
    return pl.pallas_call(
        _ada_kernel,
        grid=(cols // blk,),
        in_specs=[pl.BlockSpec((n, D_MODEL), lambda j: (0, 0)),
                  pl.BlockSpec((D_MODEL, blk), lambda j: (0, j)),
                  pl.BlockSpec((1, blk), lambda j: (0, j))],
        out_specs=pl.BlockSpec((n, blk), lambda j: (0, j)),
        out_shape=jax.ShapeDtypeStruct((n, cols), F32),
        compiler_params=_cparams(("arbitrary",)),
        name="ada",
    )(c, w_ada, b_ada.reshape(1, cols))


def _in_kernel(x_ref, sc_ref, sh_ref, g_ref, w_ref, o_ref):
    h = _rms(x_ref[0], g_ref[...]) * (1.0 + sc_ref[0]) + sh_ref[0]
    o_ref[0] = jnp.dot(h.astype(BF16), w_ref[...], preferred_element_type=F32)


def _mod_spec(per_token, tm):
    if per_token:
        return pl.BlockSpec((1, tm, D_MODEL), lambda b, t: (0, t, 0))
    return pl.BlockSpec((1, 1, D_MODEL), lambda b, t: (b, 0, 0))


def _in_proj(x, sc, sh, g, w_in_p, tm, per_token):
    bsz, t, _ = x.shape
    return pl.pallas_call(
        _in_kernel,
        grid=(bsz, t // tm),
        in_specs=[pl.BlockSpec((1, tm, D_MODEL), lambda b, i: (b, i, 0)),
                  _mod_spec(per_token, tm), _mod_spec(per_token, tm),
                  pl.BlockSpec((1, D_MODEL), lambda b, i: (0, 0)),
                  pl.BlockSpec((D_MODEL, PROJ_WIDTH), lambda b, i: (0, 0))],
        out_specs=pl.BlockSpec((1, tm, PROJ_WIDTH), lambda b, i: (b, i, 0)),
        out_shape=jax.ShapeDtypeStruct((bsz, t, PROJ_WIDTH), F32),
        compiler_params=_cparams(("arbitrary", "arbitrary")),
        name="in_proj",
    )(x, sc, sh, g.reshape(1, D_MODEL), w_in_p)


def _shift_rows(prev, x, d):
    if d == 0:
        return x
    e = jnp.concatenate([prev, x], axis=0)
    return pltpu.roll(e, d, axis=0)[SUBLANES:]


def _cumsum_rows(x):
    n = x.shape[0]
    row = lax.broadcasted_iota(jnp.int32, x.shape, 0)
    s = 1
    while s < n:
        x = x + jnp.where(row >= s, pltpu.roll(x, s, axis=0), 0.0)
        s *= 2
    return x


def _gdn_kernel(q_ref, k_ref, v_ref, z_ref, ab_ref, cq_ref, ck_ref, cv_ref, wq_ref, wk_ref, wv_ref,
                s0_ref, alog_ref, dtb_ref, nw_ref, o_ref, sn_ref, s_scr, prev_scr, *, nb, tm, tmp, chunk):
    h = pl.program_id(1)
    ti = pl.program_id(2)
    nt = pl.num_programs(2)
    n_chunks = tmp // chunk
    n_sq = int(math.log2(chunk)) - 1

    @pl.when(ti == 0)
    def _():
        s_scr[...] = s0_ref[:, 0]
        for b in range(nb):
            for si, c_ref in enumerate((cq_ref, ck_ref, cv_ref)):
                prev_scr[b, si] = jnp.concatenate(
                    [jnp.zeros((SUBLANES - (CONV_WIDTH - 1), HEAD_DIM), F32), c_ref[b]], axis=0)

    lane = lax.broadcasted_iota(jnp.int32, (tmp, LANES), 1)
    rowi = lax.broadcasted_iota(jnp.int32, (tmp, LANES), 0)
    ri = lax.broadcasted_iota(jnp.int32, (chunk, chunk), 0)
    ci = lax.broadcasted_iota(jnp.int32, (chunk, chunk), 1)
    tril = ri >= ci
    strict = ri > ci
    eye = (ri == ci).astype(F32)
    neg_a = -jnp.exp(alog_ref[...])
    dtb = dtb_ref[...]
    nw = nw_ref[...]

    def padrows(x):
        if tm == tmp:
            return x
        return jnp.concatenate([x, jnp.zeros((tmp - tm, x.shape[1]), F32)], axis=0)

    for b in range(nb):
        def conv(x_ref, w_ref, si):
            x = padrows(x_ref[b])
            prev = prev_scr[b, si]
            w = w_ref[...]
            acc = x * w[CONV_WIDTH - 1:CONV_WIDTH]
            for d in range(1, CONV_WIDTH):
                acc = acc + _shift_rows(prev, x, d) * w[CONV_WIDTH - 1 - d:CONV_WIDTH - d]
            if tm >= SUBLANES:
                prev_scr[b, si] = x[tm - SUBLANES:tm]
            return _silu(acc)

        qc = conv(q_ref, wq_ref, 0)
        kc = conv(k_ref, wk_ref, 1)
        v = conv(v_ref, wv_ref, 2)
        q = qc * lax.rsqrt(jnp.sum(qc * qc, axis=-1, keepdims=True) + EPS) * (HEAD_DIM ** -0.5)
        k = kc * lax.rsqrt(jnp.sum(kc * kc, axis=-1, keepdims=True) + EPS)
        ab = padrows(ab_ref[b])
        g_all = neg_a * jax.nn.softplus(ab + dtb)
        beta_all = jax.nn.sigmoid(ab)
        valid = rowi < tm
        g_col = jnp.sum(jnp.where((lane == h) & valid, g_all, 0.0), axis=-1, keepdims=True)
        beta_col = jnp.sum(jnp.where((lane == GDN_HEADS + h) & valid, beta_all, 0.0), axis=-1, keepdims=True)
        gb = jnp.broadcast_to(g_col, (tmp, LANES))
        bb = jnp.broadcast_to(beta_col, (tmp, LANES))
        zf = padrows(z_ref[b])

        parts = []
        for c in range(n_chunks):
            sl = slice(c * chunk, (c + 1) * chunk)
            qx, kx, vx, bx = q[sl], k[sl], v[sl], bb[sl]
            gcb = _cumsum_rows(gb[sl])
            gct = jnp.transpose(gcb)[:chunk]
            diff = gcb[:, :chunk] - gct
            decay = jnp.where(tril, jnp.exp(jnp.where(tril, diff, 0.0)), 0.0)
            kb = kx * bx
            vb = vx * bx
            amat = jnp.where(strict, _dot_nt(kb, kx) * decay, 0.0)
            qk = jnp.where(tril, _dot_nt(qx, kx) * decay, 0.0)
            inv = eye - amat
            p = amat
            for _ in range(n_sq):
                p = _dot(p, p)
                inv = inv + _dot(inv, p)
            eg = jnp.exp(gcb)
            u = _dot(inv, vb)
            w = _dot(inv, kb * eg)
            gl = gcb[chunk - 1:chunk]
            kdt = jnp.transpose(kx * jnp.exp(gl - gcb))
            mp = _dot(kdt, w)
            nc = _dot(kdt, u)
            qp = qx * eg - _dot(qk, w)
            oc = _dot(qk, u)
            parts.append((qp, oc, mp, nc, jnp.exp(gl)))

        s = s_scr[b]
        outs = []
        for qp, oc, mp, nc, egl in parts:
            outs.append(_dot(qp, s) + oc)
            s = s * egl - _dot(mp, s) + nc
        s_scr[b] = s
        o = outs[0] if n_chunks == 1 else jnp.concatenate(outs, axis=0)
        o = o * lax.rsqrt(jnp.mean(o * o, axis=-1, keepdims=True) + EPS) * nw * _silu(zf)
        o_ref[b] = o[:tm]

    @pl.when(ti == nt - 1)
    def _():
        sn_ref[:, 0] = s_scr[...]


def _gdn(proj, state_conv, s0, conv_w, alog_row, dtb_row, norm_w, nb, tm):
    bsz, t, _ = proj.shape
    tmp = max(tm, SUBLANES)
    chunk = min(GDN_CHUNK, tmp)
    qb, zb, abb = OFF_QKV // LANES, OFF_Z // LANES, OFF_AB // LANES

    def pspec(base):
        return pl.BlockSpec((nb, tm, LANES), lambda b, h, i: (b, i, base + h))

    def cspec(base):
        return pl.BlockSpec((nb, CONV_WIDTH - 1, LANES), lambda b, h, i: (b, 0, base + h))

    def wspec(base):
        return pl.BlockSpec((CONV_WIDTH, LANES), lambda b, h, i: (0, base + h))

    row = pl.BlockSpec((1, LANES), lambda b, h, i: (0, 0))
    sspec = pl.BlockSpec((nb, 1, HEAD_DIM, HEAD_DIM), lambda b, h, i: (b, h, 0, 0))
    kern = functools.partial(_gdn_kernel, nb=nb, tm=tm, tmp=tmp, chunk=chunk)
    return pl.pallas_call(
        kern,
        grid=(bsz // nb, GDN_HEADS, t // tm),
        in_specs=[pspec(qb), pspec(qb + GDN_HEADS), pspec(qb + 2 * GDN_HEADS), pspec(zb),
                  pl.BlockSpec((nb, tm, LANES), lambda b, h, i: (b, i, abb)),
                  cspec(0), cspec(GDN_HEADS), cspec(2 * GDN_HEADS),
                  wspec(0), wspec(GDN_HEADS), wspec(2 * GDN_HEADS),
                  sspec, row, row, row],
        out_specs=[pl.BlockSpec((nb, tm, LANES), lambda b, h, i: (b, i, h)), sspec],
        out_shape=[jax.ShapeDtypeStruct((bsz, t, GDN_WIDTH), F32),
                   jax.ShapeDtypeStruct(s0.shape, F32)],
        scratch_shapes=[pltpu.VMEM((nb, HEAD_DIM, HEAD_DIM), F32),
                        pltpu.VMEM((nb, 3, SUBLANES, HEAD_DIM), F32)],
        compiler_params=_cparams(("arbitrary", "arbitrary", "arbitrary")),
        name="gdn",
    )(proj, proj, proj, proj, proj, state_conv, state_conv, state_conv, conv_w, conv_w, conv_w,
      s0, alog_row, dtb_row, norm_w)


def _pool_kernel(u_ref, st_ref, w_ref, sc_ref, o_ref, prev_scr, *, nb, tm, tmp, start_pos):
    ti = pl.program_id(1)
    halo = 2 * SUBLANES

    @pl.when(ti == 0)
    def _():
        for b in range(nb):
            prev_scr[b] = jnp.concatenate([jnp.zeros((halo - POOL_BUF, POOL_WIDTH), F32), st_ref[b]], axis=0)

    lane = lax.broadcasted_iota(jnp.int32, (tmp, POOL_WIDTH), 1)
    rowi = lax.broadcasted_iota(jnp.int32, (tmp, POOL_WIDTH), 0)
    pos1 = start_pos + ti * tm + rowi + 1
    grp = lane // POOL_GROUP
    win = jnp.where(grp == 0, POOL_WINDOWS[0],
                    jnp.where(grp == 1, POOL_WINDOWS[1], jnp.where(grp == 2, POOL_WINDOWS[2], POOL_WINDOWS[3])))
    cnt = jnp.minimum(pos1, win).astype(F32)
    for b in range(nb):
        x = u_ref[b]
        if tm != tmp:
            x = jnp.concatenate([x, jnp.zeros((tmp - tm, POOL_WIDTH), F32)], axis=0)
        e = jnp.concatenate([prev_scr[b], x], axis=0)
        sums = []
        s = e
        for w in POOL_WINDOWS:
            s = s + pltpu.roll(s, w // 2, axis=0)
            sums.append(s[halo:])
        sel = jnp.where(grp == 0, sums[0], jnp.where(grp == 1, sums[1], jnp.where(grp == 2, sums[2], sums[3])))
        pooled = sel / cnt - x
        out = _dot(pooled, w_ref[...]) * sc_ref[...]
        o_ref[b] = out[:tm]
        if tm >= halo:
            prev_scr[b] = x[tm - halo:tm]


def _pool(proj, state_pool, w_bd, pool_scale, nb, tm, start_pos):
    bsz, t, _ = proj.shape
    tmp = max(tm, SUBLANES)
    kern = functools.partial(_pool_kernel, nb=nb, tm=tm, tmp=tmp, start_pos=start_pos)
    return pl.pallas_call(
        kern,
        grid=(bsz // nb, t // tm),
        in_specs=[pl.BlockSpec((nb, tm, POOL_WIDTH), lambda b, i: (b, i, 0)),
                  pl.BlockSpec((nb, POOL_BUF, POOL_WIDTH), lambda b, i: (b, 0, 0)),
                  pl.BlockSpec((POOL_WIDTH, POOL_WIDTH), lambda b, i: (0, 0)),
                  pl.BlockSpec((1, POOL_WIDTH), lambda b, i: (0, 0))],
        out_specs=pl.BlockSpec((nb, tm, POOL_WIDTH), lambda b, i: (b, i, 0)),
        out_shape=jax.ShapeDtypeStruct((bsz, t, POOL_WIDTH), F32),
        scratch_shapes=[pltpu.VMEM((nb, 2 * SUBLANES, POOL_WIDTH), F32)],
        compiler_params=_cparams(("arbitrary", "arbitrary")),
        name="pool",
    )(proj, state_pool, w_bd, pool_scale.reshape(1, POOL_WIDTH))


def _out_kernel(x_ref, po_ref, og_ref, gt_ref, sc_ref, sh_ref, gpm_ref, gpf_ref, wo_ref, rw_ref, rb_ref,
                x1_ref, ht_ref, g_ref, *, tm):
    mix = _dot(po_ref[0], wo_ref[:POOL_WIDTH]) + _dot(og_ref[0], wo_ref[POOL_WIDTH:])
    x1 = x_ref[0] + gt_ref[0] * _rms(mix, gpm_ref[...])
    x1_ref[0] = x1
    h = _rms(x1, gpf_ref[...]) * (1.0 + sc_ref[0]) + sh_ref[0]
    for s in range(ROW_TILES):
        ht_ref[pl.ds(s, tm, stride=ROW_TILES), :] = h[:, s * LANES:(s + 1) * LANES]
    logits = _dot(h, rw_ref[...]) + rb_ref[...]
    lane = lax.broadcasted_iota(jnp.int32, logits.shape, 1)
    work = logits
    sel = jnp.zeros(logits.shape, jnp.bool_)
    top = None
    den = jnp.zeros((tm, 1), F32)
    for j in range(TOP_K):
        m = jnp.max(work, axis=-1, keepdims=True)
        idx = jnp.min(jnp.where(work == m, lane, N_EXPERTS), axis=-1, keepdims=True)
        hit = lane == idx
        sel = sel | hit
        work = jnp.where(hit, -jnp.inf, work)
        if j == 0:
            top = m
        den = den + jnp.exp(m - top)
    gates = jnp.exp(logits - top) / den
    g_ref[...] = jnp.where(sel, gates, -1.0)


def _out_proj(x, pool_out, o_gdn, gt, sc, sh, g_post_mix, g_pre_ffn, w_out_b, router_w, router_b, tm, per_token):
    bsz, t, _ = x.shape
    n = bsz * t
    tiles = t // tm

    def tok(width):
        return pl.BlockSpec((1, tm, width), lambda b, i: (b, i, 0))

    def const(shape):
        return pl.BlockSpec(shape, lambda b, i: tuple(0 for _ in shape))

    kern = functools.partial(_out_kernel, tm=tm)
    return pl.pallas_call(
        kern,
        grid=(bsz, tiles),
        in_specs=[tok(D_MODEL), tok(POOL_WIDTH), tok(GDN_WIDTH),
                  _mod_spec(per_token, tm), _mod_spec(per_token, tm), _mod_spec(per_token, tm),
                  const((1, D_MODEL)), const((1, D_MODEL)), const((D_MODEL, D_MODEL)),
                  const((D_MODEL, N_EXPERTS)), const((1, N_EXPERTS))],
        out_specs=[tok(D_MODEL),
                   pl.BlockSpec((tm * ROW_TILES, LANES), lambda b, i: (b * tiles + i, 0)),
                   pl.BlockSpec((tm, N_EXPERTS), lambda b, i: (b * tiles + i, 0))],
        out_shape=[jax.ShapeDtypeStruct((bsz, t, D_MODEL), F32),
                   jax.ShapeDtypeStruct((n * ROW_TILES, LANES), F32),
                   jax.ShapeDtypeStruct((n, N_EXPERTS), F32)],
        compiler_params=_cparams(("arbitrary", "arbitrary")),
        name="out_proj",
    )(x, pool_out, o_gdn, gt, sc, sh, g_post_mix.reshape(1, D_MODEL), g_pre_ffn.reshape(1, D_MODEL),
      w_out_b, router_w, router_b.reshape(1, N_EXPERTS))


def _rank_kernel(g_ref, r_ref, cnt_ref, carry, *, tm):
    i = pl.program_id(0)

    @pl.when(i == 0)
    def _():
        carry[...] = jnp.zeros_like(carry)

    mask = (g_ref[...] >= 0.0).astype(F32)
    ri = lax.broadcasted_iota(jnp.int32, (tm, tm), 0)
    ci = lax.broadcasted_iota(jnp.int32, (tm, tm), 1)
    before = (ri > ci).astype(F32)
    c = carry[...]
    r_ref[...] = _dot(before, mask) + c
    c = c + jnp.sum(mask, axis=0, keepdims=True)
    carry[...] = c
    cnt_ref[...] = c


def _rank(gsel, tm):
    n = gsel.shape[0]
    kern = functools.partial(_rank_kernel, tm=tm)
    return pl.pallas_call(
        kern,
        grid=(n // tm,),
        in_specs=[pl.BlockSpec((tm, N_EXPERTS), lambda i: (i, 0))],
        out_specs=[pl.BlockSpec((tm, N_EXPERTS), lambda i: (i, 0)),
                   pl.BlockSpec((1, N_EXPERTS), lambda i: (0, 0))],
        out_shape=[jax.ShapeDtypeStruct((n, N_EXPERTS), F32),
                   jax.ShapeDtypeStruct((1, N_EXPERTS), F32)],
        scratch_shapes=[pltpu.VMEM((1, N_EXPERTS), F32)],
        compiler_params=_cparams(("arbitrary",)),
        name="rank",
    )(gsel)


def _dest_kernel(g_ref, r_ref, start_ref, d_ref, gt_ref, *, tm):
    g = g_ref[...]
    mask = g >= 0.0
    maskf = mask.astype(F32)
    dest = start_ref[...] + r_ref[...]
    ri = lax.broadcasted_iota(jnp.int32, (N_EXPERTS, N_EXPERTS), 0)
    ci = lax.broadcasted_iota(jnp.int32, (N_EXPERTS, N_EXPERTS), 1)
    slot = _dot(maskf, (ri < ci).astype(F32))
    lane4 = lax.broadcasted_iota(jnp.int32, (tm, TOP_K), 1)
    d_out = jnp.zeros((tm, TOP_K), F32)
    g_out = jnp.zeros((tm, TOP_K), F32)
    for j in range(TOP_K):
        pick = mask & (slot == float(j))
        dj = jnp.sum(jnp.where(pick, dest, 0.0), axis=-1, keepdims=True)
        gj = jnp.sum(jnp.where(pick, g, 0.0), axis=-1, keepdims=True)
        d_out = jnp.where(lane4 == j, dj, d_out)
        g_out = jnp.where(lane4 == j, gj, g_out)
    d_ref[...] = d_out.astype(jnp.int32)
    gt_ref[...] = g_out


def _dest(gsel, rank, start_row, tm):
    n = gsel.shape[0]
    kern = functools.partial(_dest_kernel, tm=tm)
    return pl.pallas_call(
        kern,
        grid=(n // tm,),
        in_specs=[pl.BlockSpec((tm, N_EXPERTS), lambda i: (i, 0)),
                  pl.BlockSpec((tm, N_EXPERTS), lambda i: (i, 0)),
                  pl.BlockSpec((1, N_EXPERTS), lambda i: (0, 0))],
        out_specs=[pl.BlockSpec((tm, TOP_K), lambda i: (i, 0)),
                   pl.BlockSpec((tm, TOP_K), lambda i: (i, 0))],
        out_shape=[jax.ShapeDtypeStruct((n, TOP_K), jnp.int32),
                   jax.ShapeDtypeStruct((n, TOP_K), F32)],
        compiler_params=_cparams(("arbitrary",)),
        name="dest",
    )(gsel, rank, start_row)


def _row_copy(src, dst, s_row, d_row, sem):
    return pltpu.make_async_copy(src.at[pl.ds(pl.multiple_of(s_row * ROW_TILES, ROW_TILES), ROW_TILES), :],
                                 dst.at[pl.ds(pl.multiple_of(d_row * ROW_TILES, ROW_TILES), ROW_TILES), :], sem)


def _scatter_kernel(cnt_ref, start_ref, end_ref, nu_ref, dest_ref, ht_ref, xs_ref, sem, *, tm, bm, n_blocks):
    i = pl.program_id(0)

    def body(t, carry):
        src = ht_ref.at[pl.ds(pl.multiple_of(t * ROW_TILES, ROW_TILES), ROW_TILES), :]
        for k in range(TOP_K):
            d = dest_ref[t * TOP_K + k]
            pltpu.make_async_copy(src, xs_ref.at[pl.ds(pl.multiple_of(d * ROW_TILES, ROW_TILES), ROW_TILES), :],
                                  sem).start()
        return carry

    lax.fori_loop(0, tm, body, 0)

    def tile_copy(u):
        return pltpu.make_async_copy(
            ht_ref, xs_ref.at[pl.ds(pl.multiple_of(u * (tm * ROW_TILES), tm * ROW_TILES), tm * ROW_TILES), :], sem)

    for k in range(TOP_K):
        tile_copy(0).wait()

    @pl.when(i == pl.num_programs(0) - 1)
    def _():
        per = bm // tm

        def tail_start(u, carry):
            tile_copy(u).start()
            return carry

        def tail_wait(u, carry):
            tile_copy(u).wait()
            return carry

        lax.fori_loop(nu_ref[0] * per, n_blocks * per, tail_start, 0)
        lax.fori_loop(nu_ref[0] * per, n_blocks * per, tail_wait, 0)

        def per_expert(e, carry):
            lo = start_ref[e] + cnt_ref[e]
            hi = end_ref[e]

            def fill(r, c2):
                _row_copy(ht_ref, xs_ref, 0, r, sem).start()
                return c2

            lax.fori_loop(lo, hi, fill, 0)

            def drain(r, c2):
                _row_copy(ht_ref, xs_ref, 0, r, sem).wait()
                return c2

            lax.fori_loop(lo, hi, drain, 0)
            return carry

        lax.fori_loop(0, N_EXPERTS, per_expert, 0)


def _scatter_rows(ht, dest_flat, cnt, start, end, n_used, n_blocks, tm, bm):
    n = ht.shape[0] // ROW_TILES
    rows = n_blocks * bm
    assert bm % tm == 0
    kern = functools.partial(_scatter_kernel, tm=tm, bm=bm, n_blocks=n_blocks)
    return pl.pallas_call(
        kern,
        grid_spec=pltpu.PrefetchScalarGridSpec(
            num_scalar_prefetch=4,
            grid=(n // tm,),
            in_specs=[pl.BlockSpec((tm * TOP_K,), lambda i, *_: (i,), memory_space=pltpu.SMEM),
                      pl.BlockSpec((tm * ROW_TILES, LANES), lambda i, *_: (i, 0))],
            out_specs=pl.BlockSpec(memory_space=pl.ANY),
            scratch_shapes=[pltpu.SemaphoreType.DMA(())]),
        out_shape=jax.ShapeDtypeStruct((rows * ROW_TILES, LANES), F32),
        compiler_params=_cparams(("arbitrary",)),
        name="scatter_rows",
    )(cnt, start, end, n_used, dest_flat, ht)


def _expert_kernel(be_ref, nu_ref, xs_ref, wg_ref, bg_ref, wd_ref, bd_ref, ys_ref, wg_s, wd_s, *, bm, fchunk):
    j = pl.program_id(0)
    prev = be_ref[jnp.maximum(j - 1, 0)]
    fresh = (j == 0) | (be_ref[j] != prev)
    active = j < nu_ref[0]

    @pl.when(active & fresh)
    def _():
        wg_s[...] = wg_ref[0].astype(BF16)
        wd_s[...] = wd_ref[0].astype(BF16)

    @pl.when(active)
    def _():
        x = jnp.concatenate([xs_ref[pl.ds(s, bm, stride=ROW_TILES), :] for s in range(ROW_TILES)], axis=-1)
        xb = x.astype(BF16)
        acc = jnp.zeros((bm, D_MODEL), F32) + bd_ref[0]
        for f in range(D_FF // fchunk):
            lo = f * fchunk
            gate = jnp.dot(xb, wg_s[:, lo:lo + fchunk], preferred_element_type=F32) + bg_ref[0, :, lo:lo + fchunk]
            up = (jnp.dot(xb, wg_s[:, D_FF + lo:D_FF + lo + fchunk], preferred_element_type=F32)
                  + bg_ref[0, :, D_FF + lo:D_FF + lo + fchunk])
            gate = jnp.minimum(gate, SWIGLU_LIMIT)
            up = jnp.clip(up, -SWIGLU_LIMIT, SWIGLU_LIMIT)
            act = (up + 1.0) * (gate * jax.nn.sigmoid(SWIGLU_ALPHA * gate))
            acc = acc + jnp.dot(act.astype(BF16), wd_s[lo:lo + fchunk, :], preferred_element_type=F32)
        for s in range(ROW_TILES):
            ys_ref[pl.ds(s, bm, stride=ROW_TILES), :] = acc[:, s * LANES:(s + 1) * LANES]

    @pl.when(jnp.logical_not(active))
    def _():
        ys_ref[...] = jnp.zeros_like(ys_ref)


def _experts(xs, block_expert, n_used, w_gu, b_gu, w_down, b_down, bm):
    rows = xs.shape[0] // ROW_TILES
    n_blocks = rows // bm

    def blk(j, be, nu):
        return (jnp.minimum(j, nu[0] - 1), 0)

    def wsel(j, be, nu):
        return (be[j], 0, 0)

    kern = functools.partial(_expert_kernel, bm=bm, fchunk=512)
    return pl.pallas_call(
        kern,
        grid_spec=pltpu.PrefetchScalarGridSpec(
            num_scalar_prefetch=2,
            grid=(n_blocks,),
            in_specs=[pl.BlockSpec((bm * ROW_TILES, LANES), blk),
                      pl.BlockSpec((1, D_MODEL, 2 * D_FF), wsel),
                      pl.BlockSpec((1, 1, 2 * D_FF), wsel),
                      pl.BlockSpec((1, D_FF, D_MODEL), wsel),
                      pl.BlockSpec((1, 1, D_MODEL), wsel)],
            out_specs=pl.BlockSpec((bm * ROW_TILES, LANES), lambda j, be, nu: (j, 0)),
            scratch_shapes=[pltpu.VMEM((D_MODEL, 2 * D_FF), BF16),
                            pltpu.VMEM((D_FF, D_MODEL), BF16)]),
        out_shape=jax.ShapeDtypeStruct((rows * ROW_TILES, LANES), F32),
        compiler_params=_cparams(("arbitrary",)),
        name="experts",
    )(block_expert, n_used, xs, w_gu, b_gu.reshape(N_EXPERTS, 1, 2 * D_FF), w_down,
      b_down.reshape(N_EXPERTS, 1, D_MODEL))


def _final_kernel(dest_ref, ys_ref, gate_ref, x1_ref, gt_ref, g_ref, o_ref, sl_ref, sem, *, tm):
    def body(t, carry):
        for k in range(TOP_K):
            j = t * TOP_K + k
            _row_copy(ys_ref, sl_ref, dest_ref[j], j, sem).start()
        return carry

    lax.fori_loop(0, tm, body, 0)
    pltpu.make_async_copy(ys_ref.at[pl.ds(0, tm * TOP_K * ROW_TILES), :], sl_ref, sem).wait()

    gates = gate_ref[...]
    per_tok = TOP_K * ROW_TILES
    cols = []
    for s in range(ROW_TILES):
        acc = None
        for j in range(TOP_K):
            part = sl_ref[pl.ds(j * ROW_TILES + s, tm, stride=per_tok), :] * gates[:, j:j + 1]
            acc = part if acc is None else acc + part
        cols.append(acc)
    f = jnp.concatenate(cols, axis=-1)
    o_ref[0] = x1_ref[0] + gt_ref[0] * _rms(f, g_ref[...])


def _final(ys, dest_flat, gates, x1, gt, g_post_ffn, tok_off, tm, per_token):
    bsz, t, _ = x1.shape
    tiles = t // tm
    off = tok_off // tm
    kern = functools.partial(_final_kernel, tm=tm)
    return pl.pallas_call(
        kern,
        grid=(bsz, tiles),
        in_specs=[pl.BlockSpec((tm * TOP_K,), lambda b, i: (off + b * tiles + i,), memory_space=pltpu.SMEM),
                  pl.BlockSpec(memory_space=pl.ANY),
                  pl.BlockSpec((tm, TOP_K), lambda b, i: (off + b * tiles + i, 0)),
                  pl.BlockSpec((1, tm, D_MODEL), lambda b, i: (b, i, 0)),
                  _mod_spec(per_token, tm),
                  pl.BlockSpec((1, D_MODEL), lambda b, i: (0, 0))],
        out_specs=pl.BlockSpec((1, tm, D_MODEL), lambda b, i: (b, i, 0)),
        out_shape=jax.ShapeDtypeStruct((bsz, t, D_MODEL), F32),
        scratch_shapes=[pltpu.VMEM((tm * TOP_K * ROW_TILES, LANES), F32), pltpu.SemaphoreType.DMA(())],
        compiler_params=_cparams(("arbitrary", "arbitrary")),
        name="final",
    )(dest_flat, ys, gates, x1, gt, g_post_ffn.reshape(1, D_MODEL))


def _moe(ht, gsel, w_gu, b_gu, w_down, b_down, tm, bm):
    n = gsel.shape[0]
    rank, counts = _rank(gsel, tm)
    cnt = counts[0].astype(jnp.int32)
    nblk = (cnt + bm - 1) // bm
    blk_end = jnp.cumsum(nblk)
    blk_start = blk_end - nblk
    n_blocks = -(-(n * TOP_K) // bm) + N_EXPERTS
    blocks = jnp.arange(n_blocks, dtype=jnp.int32)
    block_expert = jnp.minimum(jnp.sum((blk_end[None, :] <= blocks[:, None]).astype(jnp.int32), axis=1),
                               N_EXPERTS - 1).astype(jnp.int32)
    n_used = blk_end[-1:].astype(jnp.int32)
    row_start = (blk_start * bm).astype(jnp.int32)
    row_end = (blk_end * bm).astype(jnp.int32)
    dest, gates = _dest(gsel, rank, row_start.astype(F32).reshape(1, N_EXPERTS), tm)
    dest_flat = dest.reshape(n * TOP_K)
    xs = _scatter_rows(ht, dest_flat, cnt, row_start, row_end, n_used, n_blocks, tm, bm)
    ys = _experts(xs, block_expert, n_used, w_gu, b_gu, w_down, b_down, bm)
    return ys, dest_flat, gates


def _mix_path(x, mod, pool_buf, conv_buf, s0, start_pos, lp, tm, nb, per_token):
    (g_pre_mix, g_post_mix, g_pre_ffn, w_in_p, w_bd, pool_scale, conv_w, alog_row, dtb_row, norm_w,
     w_out_b, router_w, router_b) = lp
    bsz, t, _ = x.shape
    sh_mix, sc_mix, gt_mix, sh_ffn, sc_ffn, gt_ffn = (mod[:, i] for i in range(6))

    def shape_mod(m):
        if per_token:
            return jnp.repeat(m, t, axis=0).reshape(1, bsz * t, D_MODEL)
        return m.reshape(bsz, 1, D_MODEL)

    sh_mix, sc_mix, gt_mix, sh_ffn, sc_ffn, gt_ffn = map(shape_mod, (sh_mix, sc_mix, gt_mix, sh_ffn, sc_ffn, gt_ffn))
    xf = x.reshape(1, bsz * t, D_MODEL) if per_token else x
    proj = _in_proj(xf, sc_mix, sh_mix, g_pre_mix, w_in_p, tm, per_token).reshape(bsz, t, PROJ_WIDTH)
    tseq = min(tm, t)
    o_gdn, s_new = _gdn(proj, conv_buf, s0, conv_w, alog_row, dtb_row, norm_w, nb, tseq)
    pool_out = _pool(proj, pool_buf, w_bd, pool_scale, nb, tseq, start_pos)
    fl = (lambda a: a.reshape(1, bsz * t, a.shape[-1])) if per_token else (lambda a: a)
    x1, ht, gsel = _out_proj(xf, fl(pool_out), fl(o_gdn), gt_mix, sc_ffn, sh_ffn, g_post_mix, g_pre_ffn,
                             w_out_b, router_w, router_b, tm, per_token)
    def last_rows(buf, off, width):
        keep = buf.shape[1]
        if t >= keep:
            return proj[:, t - keep:, off:off + width]
        return jnp.concatenate([buf[:, t:], proj[:, :, off:off + width]], axis=1)

    pool_new = last_rows(pool_buf, OFF_U, POOL_WIDTH)
    conv_new = last_rows(conv_buf, OFF_QKV, CONV_CH)
    return x1, ht, gsel, gt_ffn, pool_new, conv_new, s_new


def kernel(x_prompt, x_sample, c_prompt, c_sample, state_pool, state_conv, state_ssm, w_ada, b_ada, g_pre_mix, g_post_mix, g_pre_ffn, g_post_ffn, w_in, w_pool, pool_scale, conv_w, a_log, dt_bias, gdn_norm_w, w_out, router_w, router_b, w_gu, b_gu, w_down, b_down):
    depth = w_ada.shape[0]
    bp, tp, _ = x_prompt.shape
    bs, ts, _ = x_sample.shape
    tm = 256
    y_p, y_s = x_prompt, x_sample
    outs = [[] for _ in range(6)]
    for l in range(depth):
        mod = _ada(jnp.concatenate([c_prompt, c_sample], axis=0), w_ada[l], b_ada[l]).reshape(bp + bs, 6, D_MODEL)
        w_in_p = jnp.pad(w_in[l], ((0, 0), (0, PROJ_WIDTH - w_in.shape[2]))).astype(BF16)
        w_bd = jax.scipy.linalg.block_diag(*[w_pool[l, g] for g in range(len(POOL_WINDOWS))])
        pad_row = lambda v: jnp.pad(v, (0, LANES - v.shape[0])).reshape(1, LANES)
        lp = (g_pre_mix[l], g_post_mix[l], g_pre_ffn[l], w_in_p, w_bd, pool_scale[l], conv_w[l],
              pad_row(a_log[l]), pad_row(dt_bias[l]), gdn_norm_w[l].reshape(1, HEAD_DIM),
              w_out[l].astype(BF16), router_w[l], router_b[l])
        zero_pool = jnp.zeros((bp, POOL_BUF, POOL_WIDTH), F32)
        zero_conv = jnp.zeros((bp, CONV_WIDTH - 1, CONV_CH), F32)
        zero_ssm = jnp.zeros((bp, GDN_HEADS, HEAD_DIM, HEAD_DIM), F32)
        x1p, htp, gp, gtp, pp, cp, sp = _mix_path(y_p, mod[:bp], zero_pool, zero_conv, zero_ssm, 0, lp, tm, 1, False)
        x1s, hts, gs, gts, ps, cs, ss = _mix_path(y_s, mod[bp:], state_pool[l], state_conv[l], state_ssm[l],
                                                  PAST_LEN, lp, tm, 8, True)
        ht = jnp.concatenate([htp, hts], axis=0)
        gsel = jnp.concatenate([gp, gs], axis=0)
        ys, dest_flat, gates = _moe(ht, gsel, w_gu[l], b_gu[l], w_down[l], b_down[l], tm, MOE_BLOCK)
        y_p = _final(ys, dest_flat, gates, x1p, gtp, g_post_ffn[l], 0, tm, False)
        y_s = _final(ys, dest_flat, gates, x1s, gts, g_post_ffn[l], bp * tp, tm, True).reshape(bs, ts, D_MODEL)
        for lst, val in zip(outs, (pp, cp, sp, ps, cs, ss)):
            lst.append(val)
    return (y_p, y_s) + tuple(jnp.stack(o) for o in outs)
```

```python
import functools
import math

import jax
import jax.numpy as jnp
from jax import lax
from jax.experimental import pallas as pl
from jax.experimental.pallas import tpu as pltpu

F32 = jnp.float32
BF16 = jnp.bfloat16

D_MODEL = 1024
PAST_LEN = 16384
POOL_WINDOWS = (2, 4, 8, 16)
POOL_WIDTH = D_MODEL // 4
POOL_GROUP = POOL_WIDTH // len(POOL_WINDOWS)
POOL_BUF = max(POOL_WINDOWS) - 1
GDN_WIDTH = D_MODEL - POOL_WIDTH
HEAD_DIM = 128
GDN_HEADS = GDN_WIDTH // HEAD_DIM
CONV_WIDTH = 4
CONV_CH = 3 * GDN_WIDTH
GDN_CHUNK = 64
N_EXPERTS = 32
TOP_K = 4
D_FF = D_MODEL
SWIGLU_LIMIT = 7.0
SWIGLU_ALPHA = 1.702
EPS = 1e-6

SUBLANES = 8
LANES = 128
ROW_TILES = D_MODEL // LANES
OFF_QKV = 0
OFF_Z = OFF_QKV + CONV_CH
OFF_U = OFF_Z + GDN_WIDTH
OFF_AB = OFF_U + POOL_WIDTH
PROJ_WIDTH = OFF_AB + LANES
assert OFF_Z % GDN_WIDTH == 0 and OFF_U % POOL_WIDTH == 0 and OFF_AB % LANES == 0
MOE_BLOCK = 256
VMEM_LIMIT = 56 * 1024 * 1024


def _cparams(sem):
    return pltpu.CompilerParams(dimension_semantics=sem, vmem_limit_bytes=VMEM_LIMIT)


def _dot(a, b):
    return jnp.dot(a.astype(BF16), b.astype(BF16), preferred_element_type=F32)


def _dot_nt(a, b):
    return lax.dot_general(a.astype(BF16), b.astype(BF16), (((1,), (1,)), ((), ())),
                           preferred_element_type=F32)


def _silu(x):
    return x * jax.nn.sigmoid(x)


def _rms(x, g):
    return x * lax.rsqrt(jnp.mean(x * x, axis=-1, keepdims=True) + EPS) * g


def _ada_kernel(c_ref, w_ref, b_ref, o_ref):
    o_ref[...] = _dot(_silu(c_ref[...]), w_ref[...]) + b_ref[...]


def _ada(c, w_ada, b_ada):
    n = c.shape[0]
    cols = w_ada.shape[1]
    blk = D_MODEL
    return pl.pallas_call(
        _ada_kernel,
        grid=(cols // blk,),
        in_specs=[pl.BlockSpec((n, D_MODEL), lambda j: (0, 0)),
                  pl.BlockSpec((D_MODEL, blk), lambda j: (0, j)),
                  pl.BlockSpec((1, blk), lambda j: (0, j))],
        out_specs=pl.BlockSpec((n, blk), lambda j: (0, j)),
        out_shape=jax.ShapeDtypeStruct((n, cols), F32),
        compiler_params=_cparams(("arbitrary",)),
        name="ada",
    )(c, w_ada, b_ada.reshape(1, cols))


def _in_kernel(x_ref, sc_ref, sh_ref, g_ref, w_ref, o_ref):
    h = _rms(x_ref[0], g_ref[...]) * (1.0 + sc_ref[0]) + sh_ref[0]
    o_ref[0] = jnp.dot(h.astype(BF16), w_ref[...], preferred_element_type=F32)


def _mod_spec(per_token, tm):
    if per_token:
        return pl.BlockSpec((1, tm, D_MODEL), lambda b, t: (0, t, 0))
    return pl.BlockSpec((1, 1, D_MODEL), lambda b, t: (b, 0, 0))


def _in_proj(x, sc, sh, g, w_in_p, tm, per_token):
    bsz, t, _ = x.shape
    return pl.pallas_call(
        _in_kernel,
        grid=(bsz, t // tm),
        in_specs=[pl.BlockSpec((1, tm, D_MODEL), lambda b, i: (b, i, 0)),
                  _mod_spec(per_token, tm), _mod_spec(per_token, tm),
                  pl.BlockSpec((1, D_MODEL), lambda b, i: (0, 0)),
                  pl.BlockSpec((D_MODEL, PROJ_WIDTH), lambda b, i: (0, 0))],
        out_specs=pl.BlockSpec((1, tm, PROJ_WIDTH), lambda b, i: (b, i, 0)),
        out_shape=jax.ShapeDtypeStruct((bsz, t, PROJ_WIDTH), F32),
        compiler_params=_cparams(("arbitrary", "arbitrary")),
        name="in_proj",
    )(x, sc, sh, g.reshape(1, D_MODEL), w_in_p)


def _shift_rows(prev, x, d):
    if d == 0:
        return x
    e = jnp.concatenate([prev, x], axis=0)
    return pltpu.roll(e, d, axis=0)[SUBLANES:]


def _cumsum_rows(x):
    n = x.shape[0]
    row = lax.broadcasted_iota(jnp.int32, x.shape, 0)
    s = 1
    while s < n:
        x = x + jnp.where(row >= s, pltpu.roll(x, s, axis=0), 0.0)
        s *= 2
    return x


def _gdn_kernel(qkv_ref, z_ref, ab_ref, cs_ref, cw_ref, s0_ref, alog_ref, dtb_ref, nw_ref, o_ref, sn_ref,
                s_scr, prev_scr, *, nb, tm, tmp, chunk):
    ti = pl.program_id(1)
    nt = pl.num_programs(1)
    n_chunks = tmp // chunk
    n_sq = int(math.log2(chunk)) - 1

    @pl.when(ti == 0)
    def _():
        s_scr[...] = s0_ref[...]
        for b in range(nb):
            prev_scr[b] = jnp.concatenate(
                [jnp.zeros((SUBLANES - (CONV_WIDTH - 1), CONV_CH), F32), cs_ref[b]], axis=0)

    rowi = lax.broadcasted_iota(jnp.int32, (tmp, LANES), 0)
    ri = lax.broadcasted_iota(jnp.int32, (chunk, chunk), 0)
    ci = lax.broadcasted_iota(jnp.int32, (chunk, chunk), 1)
    tril = ri >= ci
    strict = ri > ci
    eye = (ri == ci).astype(F32)
    neg_a = -jnp.exp(alog_ref[...])
    dtb = dtb_ref[...]
    nw = nw_ref[...]

    def padrows(x):
        if tm == tmp:
            return x
        return jnp.concatenate([x, jnp.zeros((tmp - tm, x.shape[1]), F32)], axis=0)

    cw = cw_ref[...]
    valid = rowi < tm
    items = []
    for b in range(nb):
        x = padrows(qkv_ref[b])
        prev = prev_scr[b]
        acc = x * cw[CONV_WIDTH - 1:CONV_WIDTH]
        for d in range(1, CONV_WIDTH):
            acc = acc + _shift_rows(prev, x, d) * cw[CONV_WIDTH - 1 - d:CONV_WIDTH - d]
        if tm >= SUBLANES:
            prev_scr[b] = x[tm - SUBLANES:tm]
        y = _silu(acc)
        ab = padrows(ab_ref[b])
        g_all = jnp.where(valid, neg_a * jax.nn.softplus(ab + dtb), 0.0)
        beta_all = jnp.where(valid, jax.nn.sigmoid(ab), 0.0)
        for h in range(GDN_HEADS):
            qc = y[:, h * HEAD_DIM:(h + 1) * HEAD_DIM]
            kc = y[:, GDN_WIDTH + h * HEAD_DIM:GDN_WIDTH + (h + 1) * HEAD_DIM]
            v = y[:, 2 * GDN_WIDTH + h * HEAD_DIM:2 * GDN_WIDTH + (h + 1) * HEAD_DIM]
            q = qc * lax.rsqrt(jnp.sum(qc * qc, axis=-1, keepdims=True) + EPS) * (HEAD_DIM ** -0.5)
            k = kc * lax.rsqrt(jnp.sum(kc * kc, axis=-1, keepdims=True) + EPS)
            gb = jnp.broadcast_to(g_all[:, h:h + 1], (tmp, LANES))
            bb = jnp.broadcast_to(beta_all[:, GDN_HEADS + h:GDN_HEADS + h + 1], (tmp, LANES))
            for c in range(n_chunks):
                sl = slice(c * chunk, (c + 1) * chunk)
                items.append(dict(b=b, h=h, c=c, q=q[sl], k=k[sl], v=v[sl], bb=bb[sl], gb=gb[sl]))

    for it in items:
        gcb = _cumsum_rows(it["gb"])
        gct = jnp.transpose(gcb)[:chunk]
        diff = gcb[:, :chunk] - gct
        it["decay"] = jnp.where(tril, jnp.exp(jnp.where(tril, diff, 0.0)), 0.0)
        it["kb"] = it["k"] * it["bb"]
        it["eg"] = jnp.exp(gcb)
        gl = gcb[chunk - 1:chunk]
        it["egl"] = jnp.exp(gl)
        it["kdt"] = jnp.transpose(it["k"] * jnp.exp(gl - gcb))
    for it in items:
        kq = _dot_nt(jnp.concatenate([it["kb"], it["q"]], axis=0), it["k"])
        it["a"] = jnp.where(strict, kq[:chunk] * it["decay"], 0.0)
        it["qk"] = jnp.where(tril, kq[chunk:] * it["decay"], 0.0)
    for it in items:
        it["inv"] = eye - it["a"]
        it["p"] = _dot(it["a"], it["a"])
    for lvl in range(n_sq):
        for it in items:
            if lvl < n_sq - 1:
                r = _dot(jnp.concatenate([it["inv"], it["p"]], axis=0), it["p"])
                it["inv"] = it["inv"] + r[:chunk]
                it["p"] = r[chunk:]
            else:
                it["inv"] = it["inv"] + _dot(it["inv"], it["p"])
    for it in items:
        it["wu"] = _dot(it["inv"], jnp.concatenate([it["kb"] * it["eg"], it["v"] * it["bb"]], axis=1))
    for it in items:
        mn = _dot(it["kdt"], it["wu"])
        qo = _dot(it["qk"], it["wu"])
        it["mp"], it["nc"] = mn[:, :HEAD_DIM], mn[:, HEAD_DIM:]
        it["qp"] = it["q"] * it["eg"] - qo[:, :HEAD_DIM]
        it["oc"] = qo[:, HEAD_DIM:]

    state = {(b, h): s_scr[b, h] for b in range(nb) for h in range(GDN_HEADS)}
    for c in range(n_chunks):
        for it in items:
            if it["c"] != c:
                continue
            s = state[(it["b"], it["h"])]
            r = _dot(jnp.concatenate([it["qp"], it["mp"]], axis=0), s)
            it["o"] = r[:chunk] + it["oc"]
            state[(it["b"], it["h"])] = s * it["egl"] - r[chunk:] + it["nc"]
    for (b, h), s in state.items():
        s_scr[b, h] = s

    for b in range(nb):
        zf = padrows(z_ref[b])
        for h in range(GDN_HEADS):
            outs = [it["o"] for it in items if it["b"] == b and it["h"] == h]
            o = outs[0] if n_chunks == 1 else jnp.concatenate(outs, axis=0)
            o = (o * lax.rsqrt(jnp.mean(o * o, axis=-1, keepdims=True) + EPS) * nw
                 * _silu(zf[:, h * HEAD_DIM:(h + 1) * HEAD_DIM]))
            o_ref[b, :, h * HEAD_DIM:(h + 1) * HEAD_DIM] = o[:tm]

    @pl.when(ti == nt - 1)
    def _():
        sn_ref[...] = s_scr[...]


def _gdn(proj, state_conv, s0, conv_w, alog_row, dtb_row, norm_w, nb, tm):
    bsz, t, _ = proj.shape
    tmp = max(tm, SUBLANES)
    chunk = min(GDN_CHUNK, tmp)
    row = pl.BlockSpec((1, LANES), lambda b, i: (0, 0))
    sspec = pl.BlockSpec((nb, GDN_HEADS, HEAD_DIM, HEAD_DIM), lambda b, i: (b, 0, 0, 0))
    kern = functools.partial(_gdn_kernel, nb=nb, tm=tm, tmp=tmp, chunk=chunk)
    return pl.pallas_call(
        kern,
        grid=(bsz // nb, t // tm),
        in_specs=[pl.BlockSpec((nb, tm, CONV_CH), lambda b, i: (b, i, OFF_QKV // CONV_CH)),
                  pl.BlockSpec((nb, tm, GDN_WIDTH), lambda b, i: (b, i, OFF_Z // GDN_WIDTH)),
                  pl.BlockSpec((nb, tm, LANES), lambda b, i: (b, i, OFF_AB // LANES)),
                  pl.BlockSpec((nb, CONV_WIDTH - 1, CONV_CH), lambda b, i: (b, 0, 0)),
                  pl.BlockSpec((CONV_WIDTH, CONV_CH), lambda b, i: (0, 0)),
                  sspec, row, row, row],
        out_specs=[pl.BlockSpec((nb, tm, GDN_WIDTH), lambda b, i: (b, i, 0)), sspec],
        out_shape=[jax.ShapeDtypeStruct((bsz, t, GDN_WIDTH), F32),
                   jax.ShapeDtypeStruct(s0.shape, F32)],
        scratch_shapes=[pltpu.VMEM((nb, GDN_HEADS, HEAD_DIM, HEAD_DIM), F32),
                        pltpu.VMEM((nb, SUBLANES, CONV_CH), F32)],
        compiler_params=_cparams(("arbitrary", "arbitrary")),
        name="gdn",
    )(proj, proj, proj, state_conv, conv_w, s0, alog_row, dtb_row, norm_w)


def _pool_kernel(u_ref, st_ref, w_ref, sc_ref, o_ref, prev_scr, *, nb, tm, tmp, start_pos):
    ti = pl.program_id(1)
    halo = 2 * SUBLANES

    @pl.when(ti == 0)
    def _():
        for b in range(nb):
            prev_scr[b] = jnp.concatenate([jnp.zeros((halo - POOL_BUF, POOL_WIDTH), F32), st_ref[b]], axis=0)

    lane = lax.broadcasted_iota(jnp.int32, (tmp, POOL_WIDTH), 1)
    rowi = lax.broadcasted_iota(jnp.int32, (tmp, POOL_WIDTH), 0)
    pos1 = start_pos + ti * tm + rowi + 1
    grp = lane // POOL_GROUP
    win = jnp.where(grp == 0, POOL_WINDOWS[0],
                    jnp.where(grp == 1, POOL_WINDOWS[1], jnp.where(grp == 2, POOL_WINDOWS[2], POOL_WINDOWS[3])))
    cnt = jnp.minimum(pos1, win).astype(F32)
    for b in range(nb):
        x = u_ref[b]
        if tm != tmp:
            x = jnp.concatenate([x, jnp.zeros((tmp - tm, POOL_WIDTH), F32)], axis=0)
        e = jnp.concatenate([prev_scr[b], x], axis=0)
        sums = []
        s = e
        for w in POOL_WINDOWS:
            s = s + pltpu.roll(s, w // 2, axis=0)
            sums.append(s[halo:])
        sel = jnp.where(grp == 0, sums[0], jnp.where(grp == 1, sums[1], jnp.where(grp == 2, sums[2], sums[3])))
        pooled = sel / cnt - x
        out = _dot(pooled, w_ref[...]) * sc_ref[...]
        o_ref[b] = out[:tm]
        if tm >= halo:
            prev_scr[b] = x[tm - halo:tm]


def _pool(proj, state_pool, w_bd, pool_scale, nb, tm, start_pos):
    bsz, t, _ = proj.shape
    tmp = max(tm, SUBLANES)
    kern = functools.partial(_pool_kernel, nb=nb, tm=tm, tmp=tmp, start_pos=start_pos)
    return pl.pallas_call(
        kern,
        grid=(bsz // nb, t // tm),
        in_specs=[pl.BlockSpec((nb, tm, POOL_WIDTH), lambda b, i: (b, i, OFF_U // POOL_WIDTH)),
                  pl.BlockSpec((nb, POOL_BUF, POOL_WIDTH), lambda b, i: (b, 0, 0)),
                  pl.BlockSpec((POOL_WIDTH, POOL_WIDTH), lambda b, i: (0, 0)),
                  pl.BlockSpec((1, POOL_WIDTH), lambda b, i: (0, 0))],
        out_specs=pl.BlockSpec((nb, tm, POOL_WIDTH), lambda b, i: (b, i, 0)),
        out_shape=jax.ShapeDtypeStruct((bsz, t, POOL_WIDTH), F32),
        scratch_shapes=[pltpu.VMEM((nb, 2 * SUBLANES, POOL_WIDTH), F32)],
        compiler_params=_cparams(("arbitrary", "arbitrary")),
        name="pool",
    )(proj, state_pool, w_bd, pool_scale.reshape(1, POOL_WIDTH))


def _out_kernel(x_ref, po_ref, og_ref, gt_ref, sc_ref, sh_ref, gpm_ref, gpf_ref, wo_ref, rw_ref, rb_ref,
                x1_ref, ht_ref, g_ref, *, tm):
    mix = _dot(po_ref[0], wo_ref[:POOL_WIDTH]) + _dot(og_ref[0], wo_ref[POOL_WIDTH:])
    x1 = x_ref[0] + gt_ref[0] * _rms(mix, gpm_ref[...])
    x1_ref[0] = x1
    h = _rms(x1, gpf_ref[...]) * (1.0 + sc_ref[0]) + sh_ref[0]
    for s in range(ROW_TILES):
        ht_ref[pl.ds(s, tm, stride=ROW_TILES), :] = h[:, s * LANES:(s + 1) * LANES]
    logits = _dot(h, rw_ref[...]) + rb_ref[...]
    lane = lax.broadcasted_iota(jnp.int32, logits.shape, 1)
    work = logits
    sel = jnp.zeros(logits.shape, jnp.bool_)
    top = None
    den = jnp.zeros((tm, 1), F32)
    for j in range(TOP_K):
        m = jnp.max(work, axis=-1, keepdims=True)
        idx = jnp.min(jnp.where(work == m, lane, N_EXPERTS), axis=-1, keepdims=True)
        hit = lane == idx
        sel = sel | hit
        work = jnp.where(hit, -jnp.inf, work)
        if j == 0:
            top = m
        den = den + jnp.exp(m - top)
    gates = jnp.exp(logits - top) / den
    g_ref[...] = jnp.where(sel, gates, -1.0)


def _out_proj(x, pool_out, o_gdn, gt, sc, sh, g_post_mix, g_pre_ffn, w_out_b, router_w, router_b, tm, per_token):
    bsz, t, _ = x.shape
    n = bsz * t
    tiles = t // tm

    def tok(width):
        return pl.BlockSpec((1, tm, width), lambda b, i: (b, i, 0))

    def const(shape):
        return pl.BlockSpec(shape, lambda b, i: tuple(0 for _ in shape))

    kern = functools.partial(_out_kernel, tm=tm)
    return pl.pallas_call(
        kern,
        grid=(bsz, tiles),
        in_specs=[tok(D_MODEL), tok(POOL_WIDTH), tok(GDN_WIDTH),
                  _mod_spec(per_token, tm), _mod_spec(per_token, tm), _mod_spec(per_token, tm),
                  const((1, D_MODEL)), const((1, D_MODEL)), const((D_MODEL, D_MODEL)),
                  const((D_MODEL, N_EXPERTS)), const((1, N_EXPERTS))],
        out_specs=[tok(D_MODEL),
                   pl.BlockSpec((tm * ROW_TILES, LANES), lambda b, i: (b * tiles + i, 0)),
                   pl.BlockSpec((tm, N_EXPERTS), lambda b, i: (b * tiles + i, 0))],
        out_shape=[jax.ShapeDtypeStruct((bsz, t, D_MODEL), F32),
                   jax.ShapeDtypeStruct((n * ROW_TILES, LANES), F32),
                   jax.ShapeDtypeStruct((n, N_EXPERTS), F32)],
        compiler_params=_cparams(("arbitrary", "arbitrary")),
        name="out_proj",
    )(x, pool_out, o_gdn, gt, sc, sh, g_post_mix.reshape(1, D_MODEL), g_pre_ffn.reshape(1, D_MODEL),
      w_out_b, router_w, router_b.reshape(1, N_EXPERTS))


def _rank_kernel(g_ref, r_ref, cnt_ref, carry, *, tm):
    i = pl.program_id(0)

    @pl.when(i == 0)
    def _():
        carry[...] = jnp.zeros_like(carry)

    mask = (g_ref[...] >= 0.0).astype(F32)
    ri = lax.broadcasted_iota(jnp.int32, (tm, tm), 0)
    ci = lax.broadcasted_iota(jnp.int32, (tm, tm), 1)
    before = (ri > ci).astype(F32)
    c = carry[...]
    r_ref[...] = _dot(before, mask) + c
    c = c + jnp.sum(mask, axis=0, keepdims=True)
    carry[...] = c
    cnt_ref[...] = c


def _rank(gsel, tm):
    n = gsel.shape[0]
    kern = functools.partial(_rank_kernel, tm=tm)
    return pl.pallas_call(
        kern,
        grid=(n // tm,),
        in_specs=[pl.BlockSpec((tm, N_EXPERTS), lambda i: (i, 0))],
        out_specs=[pl.BlockSpec((tm, N_EXPERTS), lambda i: (i, 0)),
                   pl.BlockSpec((1, N_EXPERTS), lambda i: (0, 0))],
        out_shape=[jax.ShapeDtypeStruct((n, N_EXPERTS), F32),
                   jax.ShapeDtypeStruct((1, N_EXPERTS), F32)],
        scratch_shapes=[pltpu.VMEM((1, N_EXPERTS), F32)],
        compiler_params=_cparams(("arbitrary",)),
        name="rank",
    )(gsel)


def _dest_kernel(g_ref, r_ref, start_ref, d_ref, gt_ref, *, tm):
    g = g_ref[...]
    mask = g >= 0.0
    maskf = mask.astype(F32)
    dest = start_ref[...] + r_ref[...]
    ri = lax.broadcasted_iota(jnp.int32, (N_EXPERTS, N_EXPERTS), 0)
    ci = lax.broadcasted_iota(jnp.int32, (N_EXPERTS, N_EXPERTS), 1)
    slot = _dot(maskf, (ri < ci).astype(F32))
    lane4 = lax.broadcasted_iota(jnp.int32, (tm, TOP_K), 1)
    d_out = jnp.zeros((tm, TOP_K), F32)
    g_out = jnp.zeros((tm, TOP_K), F32)
    for j in range(TOP_K):
        pick = mask & (slot == float(j))
        dj = jnp.sum(jnp.where(pick, dest, 0.0), axis=-1, keepdims=True)
        gj = jnp.sum(jnp.where(pick, g, 0.0), axis=-1, keepdims=True)
        d_out = jnp.where(lane4 == j, dj, d_out)
        g_out = jnp.where(lane4 == j, gj, g_out)
    d_ref[...] = d_out.astype(jnp.int32)
    gt_ref[...] = g_out


def _dest(gsel, rank, start_row, tm):
    n = gsel.shape[0]
    kern = functools.partial(_dest_kernel, tm=tm)
    return pl.pallas_call(
        kern,
        grid=(n // tm,),
        in_specs=[pl.BlockSpec((tm, N_EXPERTS), lambda i: (i, 0)),
                  pl.BlockSpec((tm, N_EXPERTS), lambda i: (i, 0)),
                  pl.BlockSpec((1, N_EXPERTS), lambda i: (0, 0))],
        out_specs=[pl.BlockSpec((tm, TOP_K), lambda i: (i, 0)),
                   pl.BlockSpec((tm, TOP_K), lambda i: (i, 0))],
        out_shape=[jax.ShapeDtypeStruct((n, TOP_K), jnp.int32),
                   jax.ShapeDtypeStruct((n, TOP_K), F32)],
        compiler_params=_cparams(("arbitrary",)),
        name="dest",
    )(gsel, rank, start_row)


def _row_copy(src, dst, s_row, d_row, sem):
    return pltpu.make_async_copy(src.at[pl.ds(pl.multiple_of(s_row * ROW_TILES, ROW_TILES), ROW_TILES), :],
                                 dst.at[pl.ds(pl.multiple_of(d_row * ROW_TILES, ROW_TILES), ROW_TILES), :], sem)


def _scatter_kernel(cnt_ref, start_ref, end_ref, nu_ref, dest_ref, ht_ref, xs_ref, sem, *, tm, bm, n_blocks):
    i = pl.program_id(0)

    def body(t, carry):
        src = ht_ref.at[pl.ds(pl.multiple_of(t * ROW_TILES, ROW_TILES), ROW_TILES), :]
        for k in range(TOP_K):
            d = dest_ref[t * TOP_K + k]
            pltpu.make_async_copy(src, xs_ref.at[pl.ds(pl.multiple_of(d * ROW_TILES, ROW_TILES), ROW_TILES), :],
                                  sem).start()
        return carry

    lax.fori_loop(0, tm, body, 0)

    def tile_copy(u):
        return pltpu.make_async_copy(
            ht_ref, xs_ref.at[pl.ds(pl.multiple_of(u * (tm * ROW_TILES), tm * ROW_TILES), tm * ROW_TILES), :], sem)

    for k in range(TOP_K):
        tile_copy(0).wait()

    @pl.when(i == pl.num_programs(0) - 1)
    def _():
        per = bm // tm

        def tail_start(u, carry):
            tile_copy(u).start()
            return carry

        def tail_wait(u, carry):
            tile_copy(u).wait()
            return carry

        lax.fori_loop(nu_ref[0] * per, n_blocks * per, tail_start, 0)
        lax.fori_loop(nu_ref[0] * per, n_blocks * per, tail_wait, 0)

        def per_expert(e, carry):
            lo = start_ref[e] + cnt_ref[e]
            hi = end_ref[e]

            def fill(r, c2):
                _row_copy(ht_ref, xs_ref, 0, r, sem).start()
                return c2

            lax.fori_loop(lo, hi, fill, 0)

            def drain(r, c2):
                _row_copy(ht_ref, xs_ref, 0, r, sem).wait()
                return c2

            lax.fori_loop(lo, hi, drain, 0)
            return carry

        lax.fori_loop(0, N_EXPERTS, per_expert, 0)


def _scatter_rows(ht, dest_flat, cnt, start, end, n_used, n_blocks, tm, bm):
    n = ht.shape[0] // ROW_TILES
    rows = n_blocks * bm
    assert bm % tm == 0
    kern = functools.partial(_scatter_kernel, tm=tm, bm=bm, n_blocks=n_blocks)
    return pl.pallas_call(
        kern,
        grid_spec=pltpu.PrefetchScalarGridSpec(
            num_scalar_prefetch=4,
            grid=(n // tm,),
            in_specs=[pl.BlockSpec((tm * TOP_K,), lambda i, *_: (i,), memory_space=pltpu.SMEM),
                      pl.BlockSpec((tm * ROW_TILES, LANES), lambda i, *_: (i, 0))],
            out_specs=pl.BlockSpec(memory_space=pl.ANY),
            scratch_shapes=[pltpu.SemaphoreType.DMA(())]),
        out_shape=jax.ShapeDtypeStruct((rows * ROW_TILES, LANES), F32),
        compiler_params=_cparams(("arbitrary",)),
        name="scatter_rows",
    )(cnt, start, end, n_used, dest_flat, ht)


def _expert_kernel(be_ref, nu_ref, xs_ref, wg_ref, bg_ref, wd_ref, bd_ref, ys_ref, wg_s, wd_s, *, bm, fchunk):
    j = pl.program_id(0)
    prev = be_ref[jnp.maximum(j - 1, 0)]
    fresh = (j == 0) | (be_ref[j] != prev)
    active = j < nu_ref[0]

    @pl.when(active & fresh)
    def _():
        wg_s[...] = wg_ref[0].astype(BF16)
        wd_s[...] = wd_ref[0].astype(BF16)

    @pl.when(active)
    def _():
        x = jnp.concatenate([xs_ref[pl.ds(s, bm, stride=ROW_TILES), :] for s in range(ROW_TILES)], axis=-1)
        xb = x.astype(BF16)
        acc = jnp.zeros((bm, D_MODEL), F32) + bd_ref[0]
        for f in range(D_FF // fchunk):
            lo = f * fchunk
            gate = jnp.dot(xb, wg_s[:, lo:lo + fchunk], preferred_element_type=F32) + bg_ref[0, :, lo:lo + fchunk]
            up = (jnp.dot(xb, wg_s[:, D_FF + lo:D_FF + lo + fchunk], preferred_element_type=F32)
                  + bg_ref[0, :, D_FF + lo:D_FF + lo + fchunk])
            gate = jnp.minimum(gate, SWIGLU_LIMIT)
            up = jnp.clip(up, -SWIGLU_LIMIT, SWIGLU_LIMIT)
            act = (up + 1.0) * (gate * jax.nn.sigmoid(SWIGLU_ALPHA * gate))
            acc = acc + jnp.dot(act.astype(BF16), wd_s[lo:lo + fchunk, :], preferred_element_type=F32)
        for s in range(ROW_TILES):
            ys_ref[pl.ds(s, bm, stride=ROW_TILES), :] = acc[:, s * LANES:(s + 1) * LANES]

    @pl.when(jnp.logical_not(active))
    def _():
        ys_ref[...] = jnp.zeros_like(ys_ref)


def _experts(xs, block_expert, n_used, w_gu, b_gu, w_down, b_down, bm):
    rows = xs.shape[0] // ROW_TILES
    n_blocks = rows // bm

    def blk(j, be, nu):
        return (jnp.minimum(j, nu[0] - 1), 0)

    def wsel(j, be, nu):
        return (be[j], 0, 0)

    kern = functools.partial(_expert_kernel, bm=bm, fchunk=512)
    return pl.pallas_call(
        kern,
        grid_spec=pltpu.PrefetchScalarGridSpec(
            num_scalar_prefetch=2,
            grid=(n_blocks,),
            in_specs=[pl.BlockSpec((bm * ROW_TILES, LANES), blk),
                      pl.BlockSpec((1, D_MODEL, 2 * D_FF), wsel),
                      pl.BlockSpec((1, 1, 2 * D_FF), wsel),
                      pl.BlockSpec((1, D_FF, D_MODEL), wsel),
                      pl.BlockSpec((1, 1, D_MODEL), wsel)],
            out_specs=pl.BlockSpec((bm * ROW_TILES, LANES), lambda j, be, nu: (j, 0)),
            scratch_shapes=[pltpu.VMEM((D_MODEL, 2 * D_FF), BF16),
                            pltpu.VMEM((D_FF, D_MODEL), BF16)]),
        out_shape=jax.ShapeDtypeStruct((rows * ROW_TILES, LANES), F32),
        compiler_params=_cparams(("arbitrary",)),
        name="experts",
    )(block_expert, n_used, xs, w_gu, b_gu.reshape(N_EXPERTS, 1, 2 * D_FF), w_down,
      b_down.reshape(N_EXPERTS, 1, D_MODEL))


def _final_kernel(dest_ref, ys_ref, gate_ref, x1_ref, gt_ref, g_ref, o_ref, sl_ref, sem, *, tm):
    def body(t, carry):
        for k in range(TOP_K):
            j = t * TOP_K + k
            _row_copy(ys_ref, sl_ref, dest_ref[j], j, sem).start()
        return carry

    lax.fori_loop(0, tm, body, 0)
    pltpu.make_async_copy(ys_ref.at[pl.ds(0, tm * TOP_K * ROW_TILES), :], sl_ref, sem).wait()

    gates = gate_ref[...]
    per_tok = TOP_K * ROW_TILES
    cols = []
    for s in range(ROW_TILES):
        acc = None
        for j in range(TOP_K):
            part = sl_ref[pl.ds(j * ROW_TILES + s, tm, stride=per_tok), :] * gates[:, j:j + 1]
            acc = part if acc is None else acc + part
        cols.append(acc)
    f = jnp.concatenate(cols, axis=-1)
    o_ref[0] = x1_ref[0] + gt_ref[0] * _rms(f, g_ref[...])


def _final(ys, dest_flat, gates, x1, gt, g_post_ffn, tok_off, tm, per_token):
    bsz, t, _ = x1.shape
    tiles = t // tm
    off = tok_off // tm
    kern = functools.partial(_final_kernel, tm=tm)
    return pl.pallas_call(
        kern,
        grid=(bsz, tiles),
        in_specs=[pl.BlockSpec((tm * TOP_K,), lambda b, i: (off + b * tiles + i,), memory_space=pltpu.SMEM),
                  pl.BlockSpec(memory_space=pl.ANY),
                  pl.BlockSpec((tm, TOP_K), lambda b, i: (off + b * tiles + i, 0)),
                  pl.BlockSpec((1, tm, D_MODEL), lambda b, i: (b, i, 0)),
                  _mod_spec(per_token, tm),
                  pl.BlockSpec((1, D_MODEL), lambda b, i: (0, 0))],
        out_specs=pl.BlockSpec((1, tm, D_MODEL), lambda b, i: (b, i, 0)),
        out_shape=jax.ShapeDtypeStruct((bsz, t, D_MODEL), F32),
        scratch_shapes=[pltpu.VMEM((tm * TOP_K * ROW_TILES, LANES), F32), pltpu.SemaphoreType.DMA(())],
        compiler_params=_cparams(("arbitrary", "arbitrary")),
        name="final",
    )(dest_flat, ys, gates, x1, gt, g_post_ffn.reshape(1, D_MODEL))


def _moe(ht, gsel, w_gu, b_gu, w_down, b_down, tm, bm):
    n = gsel.shape[0]
    rank, counts = _rank(gsel, tm)
    cnt = counts[0].astype(jnp.int32)
    nblk = (cnt + bm - 1) // bm
    blk_end = jnp.cumsum(nblk)
    blk_start = blk_end - nblk
    n_blocks = -(-(n * TOP_K) // bm) + N_EXPERTS
    blocks = jnp.arange(n_blocks, dtype=jnp.int32)
    block_expert = jnp.minimum(jnp.sum((blk_end[None, :] <= blocks[:, None]).astype(jnp.int32), axis=1),
                               N_EXPERTS - 1).astype(jnp.int32)
    n_used = blk_end[-1:].astype(jnp.int32)
    row_start = (blk_start * bm).astype(jnp.int32)
    row_end = (blk_end * bm).astype(jnp.int32)
    dest, gates = _dest(gsel, rank, row_start.astype(F32).reshape(1, N_EXPERTS), tm)
    dest_flat = dest.reshape(n * TOP_K)
    xs = _scatter_rows(ht, dest_flat, cnt, row_start, row_end, n_used, n_blocks, tm, bm)
    ys = _experts(xs, block_expert, n_used, w_gu, b_gu, w_down, b_down, bm)
    return ys, dest_flat, gates


def _mix_path(x, mod, pool_buf, conv_buf, s0, start_pos, lp, tm, nb, per_token):
    (g_pre_mix, g_post_mix, g_pre_ffn, w_in_p, w_bd, pool_scale, conv_w, alog_row, dtb_row, norm_w,
     w_out_b, router_w, router_b) = lp
    bsz, t, _ = x.shape
    sh_mix, sc_mix, gt_mix, sh_ffn, sc_ffn, gt_ffn = (mod[:, i] for i in range(6))

    def shape_mod(m):
        if per_token:
            return jnp.repeat(m, t, axis=0).reshape(1, bsz * t, D_MODEL)
        return m.reshape(bsz, 1, D_MODEL)

    sh_mix, sc_mix, gt_mix, sh_ffn, sc_ffn, gt_ffn = map(shape_mod, (sh_mix, sc_mix, gt_mix, sh_ffn, sc_ffn, gt_ffn))
    xf = x.reshape(1, bsz * t, D_MODEL) if per_token else x
    proj = _in_proj(xf, sc_mix, sh_mix, g_pre_mix, w_in_p, tm, per_token).reshape(bsz, t, PROJ_WIDTH)
    tseq = min(tm, t)
    o_gdn, s_new = _gdn(proj, conv_buf, s0, conv_w, alog_row, dtb_row, norm_w, nb, tseq)
    pool_out = _pool(proj, pool_buf, w_bd, pool_scale, nb, tseq, start_pos)
    fl = (lambda a: a.reshape(1, bsz * t, a.shape[-1])) if per_token else (lambda a: a)
    x1, ht, gsel = _out_proj(xf, fl(pool_out), fl(o_gdn), gt_mix, sc_ffn, sh_ffn, g_post_mix, g_pre_ffn,
                             w_out_b, router_w, router_b, tm, per_token)
    def last_rows(buf, off, width):
        keep = buf.shape[1]
        if t >= keep:
            return proj[:, t - keep:, off:off + width]
        return jnp.concatenate([buf[:, t:], proj[:, :, off:off + width]], axis=1)

    pool_new = last_rows(pool_buf, OFF_U, POOL_WIDTH)
    conv_new = last_rows(conv_buf, OFF_QKV, CONV_CH)
    return x1, ht, gsel, gt_ffn, pool_new, conv_new, s_new


def kernel(x_prompt, x_sample, c_prompt, c_sample, state_pool, state_conv, state_ssm, w_ada, b_ada, g_pre_mix, g_post_mix, g_pre_ffn, g_post_ffn, w_in, w_pool, pool_scale, conv_w, a_log, dt_bias, gdn_norm_w, w_out, router_w, router_b, w_gu, b_gu, w_down, b_down):
    depth = w_ada.shape[0]
    bp, tp, _ = x_prompt.shape
    bs, ts, _ = x_sample.shape
    tm = 256
    y_p, y_s = x_prompt, x_sample
    outs = [[] for _ in range(6)]
    for l in range(depth):
        mod = _ada(jnp.concatenate([c_prompt, c_sample], axis=0), w_ada[l], b_ada[l]).reshape(bp + bs, 6, D_MODEL)
        o1, o2, o3 = POOL_WIDTH, POOL_WIDTH + CONV_CH, POOL_WIDTH + CONV_CH + GDN_WIDTH
        w_l = w_in[l]
        w_in_p = jnp.concatenate(
            [w_l[:, o1:o2], w_l[:, o2:o3], w_l[:, :o1], w_l[:, o3:],
             jnp.zeros((D_MODEL, LANES - (w_l.shape[1] - o3)), F32)], axis=1).astype(BF16)
        w_bd = jax.scipy.linalg.block_diag(*[w_pool[l, g] for g in range(len(POOL_WINDOWS))])
        pad_row = lambda v: jnp.pad(v, (0, LANES - v.shape[0])).reshape(1, LANES)
        lp = (g_pre_mix[l], g_post_mix[l], g_pre_ffn[l], w_in_p, w_bd, pool_scale[l], conv_w[l],
              pad_row(a_log[l]), pad_row(dt_bias[l]), gdn_norm_w[l].reshape(1, HEAD_DIM),
              w_out[l].astype(BF16), router_w[l], router_b[l])
        zero_pool = jnp.zeros((bp, POOL_BUF, POOL_WIDTH), F32)
        zero_conv = jnp.zeros((bp, CONV_WIDTH - 1, CONV_CH), F32)
        zero_ssm = jnp.zeros((bp, GDN_HEADS, HEAD_DIM, HEAD_DIM), F32)
        x1p, htp, gp, gtp, pp, cp, sp = _mix_path(y_p, mod[:bp], zero_pool, zero_conv, zero_ssm, 0, lp, tm, 1, False)
        x1s, hts, gs, gts, ps, cs, ss = _mix_path(y_s, mod[bp:], state_pool[l], state_conv[l], state_ssm[l],
                                                  PAST_LEN, lp, tm, 8, True)
        ht = jnp.concatenate([htp, hts], axis=0)
        gsel = jnp.concatenate([gp, gs], axis=0)
        ys, dest_flat, gates = _moe(ht, gsel, w_gu[l], b_gu[l], w_down[l], b_down[l], tm, MOE_BLOCK)
        y_p = _final(ys, dest_flat, gates, x1p, gtp, g_post_ffn[l], 0, tm, False)
        y_s = _final(ys, dest_flat, gates, x1s, gts, g_post_ffn[l], bp * tp, tm, True).reshape(bs, ts, D_MODEL)
        for lst, val in zip(outs, (pp, cp, sp, ps, cs, ss)):
            lst.append(val)
    return (y_p, y_s) + tuple(jnp.stack(o) for o in outs)
```

```python
import functools
import math

import jax
import jax.numpy as jnp
from jax import lax
from jax.experimental import pallas as pl
from jax.experimental.pallas import tpu as pltpu

F32 = jnp.float32
BF16 = jnp.bfloat16

D_MODEL = 1024
PAST_LEN = 16384
POOL_WINDOWS = (2, 4, 8, 16)
POOL_WIDTH = D_MODEL // 4
POOL_GROUP = POOL_WIDTH // len(POOL_WINDOWS)
POOL_BUF = max(POOL_WINDOWS) - 1
GDN_WIDTH = D_MODEL - POOL_WIDTH
HEAD_DIM = 128
GDN_HEADS = GDN_WIDTH // HEAD_DIM
CONV_WIDTH = 4
CONV_CH = 3 * GDN_WIDTH
GDN_CHUNK = 64
N_EXPERTS = 32
TOP_K = 4
D_FF = D_MODEL
SWIGLU_LIMIT = 7.0
SWIGLU_ALPHA = 1.702
EPS = 1e-6

SUBLANES = 8
LANES = 128
ROW_TILES = D_MODEL // LANES
OFF_QKV = 0
OFF_Z = OFF_QKV + CONV_CH
OFF_U = OFF_Z + GDN_WIDTH
OFF_AB = OFF_U + POOL_WIDTH
PROJ_WIDTH = OFF_AB + LANES
assert OFF_Z % GDN_WIDTH == 0 and OFF_U % POOL_WIDTH == 0 and OFF_AB % LANES == 0
MOE_BLOCK = 512
DMA_QUEUES = 2
VMEM_LIMIT = 56 * 1024 * 1024


def _cparams(sem):
    return pltpu.CompilerParams(dimension_semantics=sem, vmem_limit_bytes=VMEM_LIMIT)


def _dot(a, b):
    return jnp.dot(a.astype(BF16), b.astype(BF16), preferred_element_type=F32)


def _dot_nt(a, b):
    return lax.dot_general(a.astype(BF16), b.astype(BF16), (((1,), (1,)), ((), ())),
                           preferred_element_type=F32)


def _silu(x):
    return x * jax.nn.sigmoid(x)


def _rms(x, g):
    return x * lax.rsqrt(jnp.mean(x * x, axis=-1, keepdims=True) + EPS) * g


def _ada_kernel(c_ref, w_ref, b_ref, o_ref):
    o_ref[...] = _dot(_silu(c_ref[...]), w_ref[...]) + b_ref[...]


def _ada(c, w_ada, b_ada):
    n = c.shape[0]
    cols = w_ada.shape[1]
    blk = D_MODEL
    return pl.pallas_call(
        _ada_kernel,
        grid=(cols // blk,),
        in_specs=[pl.BlockSpec((n, D_MODEL), lambda j: (0, 0)),
                  pl.BlockSpec((D_MODEL, blk), lambda j: (0, j)),
                  pl.BlockSpec((1, blk), lambda j: (0, j))],
        out_specs=pl.BlockSpec((n, blk), lambda j: (0, j)),
        out_shape=jax.ShapeDtypeStruct((n, cols), F32),
        compiler_params=_cparams(("arbitrary",)),
        name="ada",
    )(c, w_ada, b_ada.reshape(1, cols))


def _in_kernel(x_ref, sc_ref, sh_ref, g_ref, w_ref, o_ref):
    h = _rms(x_ref[0], g_ref[...]) * (1.0 + sc_ref[0]) + sh_ref[0]
    o_ref[0] = jnp.dot(h.astype(BF16), w_ref[...], preferred_element_type=F32)


def _mod_spec(per_token, tm):
    if per_token:
        return pl.BlockSpec((1, tm, D_MODEL), lambda b, t: (0, t, 0))
    return pl.BlockSpec((1, 1, D_MODEL), lambda b, t: (b, 0, 0))


def _in_proj(x, sc, sh, g, w_in_p, tm, per_token):
    bsz, t, _ = x.shape
    return pl.pallas_call(
        _in_kernel,
        grid=(bsz, t // tm),
        in_specs=[pl.BlockSpec((1, tm, D_MODEL), lambda b, i: (b, i, 0)),
                  _mod_spec(per_token, tm), _mod_spec(per_token, tm),
                  pl.BlockSpec((1, D_MODEL), lambda b, i: (0, 0)),
                  pl.BlockSpec((D_MODEL, PROJ_WIDTH), lambda b, i: (0, 0))],
        out_specs=pl.BlockSpec((1, tm, PROJ_WIDTH), lambda b, i: (b, i, 0)),
        out_shape=jax.ShapeDtypeStruct((bsz, t, PROJ_WIDTH), F32),
        compiler_params=_cparams(("arbitrary", "arbitrary")),
        name="in_proj",
    )(x, sc, sh, g.reshape(1, D_MODEL), w_in_p)


def _shift_rows(prev, x, d):
    if d == 0:
        return x
    e = jnp.concatenate([prev, x], axis=0)
    return pltpu.roll(e, d, axis=0)[SUBLANES:]


def _cumsum_rows(x):
    n = x.shape[0]
    row = lax.broadcasted_iota(jnp.int32, x.shape, 0)
    s = 1
    while s < n:
        x = x + jnp.where(row >= s, pltpu.roll(x, s, axis=0), 0.0)
        s *= 2
    return x


def _gdn_kernel(qkv_ref, z_ref, ab_ref, cs_ref, cw_ref, s0_ref, alog_ref, dtb_ref, nw_ref, o_ref, sn_ref,
                s_scr, prev_scr, *, nb, tm, tmp, chunk):
    ti = pl.program_id(1)
    nt = pl.num_programs(1)
    n_chunks = tmp // chunk
    n_sq = int(math.log2(chunk)) - 1

    @pl.when(ti == 0)
    def _():
        s_scr[...] = s0_ref[...]
        for b in range(nb):
            prev_scr[b] = jnp.concatenate(
                [jnp.zeros((SUBLANES - (CONV_WIDTH - 1), CONV_CH), F32), cs_ref[b]], axis=0)

    rowi = lax.broadcasted_iota(jnp.int32, (tmp, LANES), 0)
    ri = lax.broadcasted_iota(jnp.int32, (chunk, chunk), 0)
    ci = lax.broadcasted_iota(jnp.int32, (chunk, chunk), 1)
    tril = ri >= ci
    strict = ri > ci
    eye = (ri == ci).astype(F32)
    neg_a = -jnp.exp(alog_ref[...])
    dtb = dtb_ref[...]
    nw = nw_ref[...]

    def padrows(x):
        if tm == tmp:
            return x
        return jnp.concatenate([x, jnp.zeros((tmp - tm, x.shape[1]), F32)], axis=0)

    cw = cw_ref[...]
    valid = rowi < tm
    items = []
    for b in range(nb):
        x = padrows(qkv_ref[b])
        prev = prev_scr[b]
        acc = x * cw[CONV_WIDTH - 1:CONV_WIDTH]
        for d in range(1, CONV_WIDTH):
            acc = acc + _shift_rows(prev, x, d) * cw[CONV_WIDTH - 1 - d:CONV_WIDTH - d]
        if tm >= SUBLANES:
            prev_scr[b] = x[tm - SUBLANES:tm]
        y = _silu(acc)
        ab = padrows(ab_ref[b])
        g_all = jnp.where(valid, neg_a * jax.nn.softplus(ab + dtb), 0.0)
        beta_all = jnp.where(valid, jax.nn.sigmoid(ab), 0.0)
        for h in range(GDN_HEADS):
            qc = y[:, h * HEAD_DIM:(h + 1) * HEAD_DIM]
            kc = y[:, GDN_WIDTH + h * HEAD_DIM:GDN_WIDTH + (h + 1) * HEAD_DIM]
            v = y[:, 2 * GDN_WIDTH + h * HEAD_DIM:2 * GDN_WIDTH + (h + 1) * HEAD_DIM]
            q = qc * lax.rsqrt(jnp.sum(qc * qc, axis=-1, keepdims=True) + EPS) * (HEAD_DIM ** -0.5)
            k = kc * lax.rsqrt(jnp.sum(kc * kc, axis=-1, keepdims=True) + EPS)
            gb = jnp.broadcast_to(g_all[:, h:h + 1], (tmp, LANES))
            bb = jnp.broadcast_to(beta_all[:, GDN_HEADS + h:GDN_HEADS + h + 1], (tmp, LANES))
            for c in range(n_chunks):
                sl = slice(c * chunk, (c + 1) * chunk)
                items.append(dict(b=b, h=h, c=c, q=q[sl], k=k[sl], v=v[sl], bb=bb[sl], gb=gb[sl]))

    for it in items:
        gcb = _cumsum_rows(it["gb"])
        gct = jnp.transpose(gcb)[:chunk]
        diff = gcb[:, :chunk] - gct
        it["decay"] = jnp.where(tril, jnp.exp(jnp.where(tril, diff, 0.0)), 0.0)
        it["kb"] = it["k"] * it["bb"]
        it["eg"] = jnp.exp(gcb)
        gl = gcb[chunk - 1:chunk]
        it["egl"] = jnp.exp(gl)
        it["kdt"] = jnp.transpose(it["k"] * jnp.exp(gl - gcb))
    for it in items:
        kq = _dot_nt(jnp.concatenate([it["kb"], it["q"]], axis=0), it["k"])
        it["a"] = jnp.where(strict, kq[:chunk] * it["decay"], 0.0)
        it["qk"] = jnp.where(tril, kq[chunk:] * it["decay"], 0.0)
    for it in items:
        it["inv"] = eye - it["a"]
        it["p"] = _dot(it["a"], it["a"])
    for lvl in range(n_sq):
        for it in items:
            if lvl < n_sq - 1:
                r = _dot(jnp.concatenate([it["inv"], it["p"]], axis=0), it["p"])
                it["inv"] = it["inv"] + r[:chunk]
                it["p"] = r[chunk:]
            else:
                it["inv"] = it["inv"] + _dot(it["inv"], it["p"])
    for it in items:
        it["wu"] = _dot(it["inv"], jnp.concatenate([it["kb"] * it["eg"], it["v"] * it["bb"]], axis=1))
    for it in items:
        mn = _dot(it["kdt"], it["wu"])
        qo = _dot(it["qk"], it["wu"])
        it["mp"], it["nc"] = mn[:, :HEAD_DIM], mn[:, HEAD_DIM:]
        it["qp"] = it["q"] * it["eg"] - qo[:, :HEAD_DIM]
        it["oc"] = qo[:, HEAD_DIM:]

    state = {(b, h): s_scr[b, h] for b in range(nb) for h in range(GDN_HEADS)}
    for c in range(n_chunks):
        for it in items:
            if it["c"] != c:
                continue
            s = state[(it["b"], it["h"])]
            r = _dot(jnp.concatenate([it["qp"], it["mp"]], axis=0), s)
            it["o"] = r[:chunk] + it["oc"]
            state[(it["b"], it["h"])] = s * it["egl"] - r[chunk:] + it["nc"]
    for (b, h), s in state.items():
        s_scr[b, h] = s

    for b in range(nb):
        zf = padrows(z_ref[b])
        for h in range(GDN_HEADS):
            outs = [it["o"] for it in items if it["b"] == b and it["h"] == h]
            o = outs[0] if n_chunks == 1 else jnp.concatenate(outs, axis=0)
            o = (o * lax.rsqrt(jnp.mean(o * o, axis=-1, keepdims=True) + EPS) * nw
                 * _silu(zf[:, h * HEAD_DIM:(h + 1) * HEAD_DIM]))
            o_ref[b, :, h * HEAD_DIM:(h + 1) * HEAD_DIM] = o[:tm]

    @pl.when(ti == nt - 1)
    def _():
        sn_ref[...] = s_scr[...]


def _gdn(proj, state_conv, s0, conv_w, alog_row, dtb_row, norm_w, nb, tm):
    bsz, t, _ = proj.shape
    tmp = max(tm, SUBLANES)
    chunk = min(GDN_CHUNK, tmp)
    row = pl.BlockSpec((1, LANES), lambda b, i: (0, 0))
    sspec = pl.BlockSpec((nb, GDN_HEADS, HEAD_DIM, HEAD_DIM), lambda b, i: (b, 0, 0, 0))
    kern = functools.partial(_gdn_kernel, nb=nb, tm=tm, tmp=tmp, chunk=chunk)
    return pl.pallas_call(
        kern,
        grid=(bsz // nb, t // tm),
        in_specs=[pl.BlockSpec((nb, tm, CONV_CH), lambda b, i: (b, i, OFF_QKV // CONV_CH)),
                  pl.BlockSpec((nb, tm, GDN_WIDTH), lambda b, i: (b, i, OFF_Z // GDN_WIDTH)),
                  pl.BlockSpec((nb, tm, LANES), lambda b, i: (b, i, OFF_AB // LANES)),
                  pl.BlockSpec((nb, CONV_WIDTH - 1, CONV_CH), lambda b, i: (b, 0, 0)),
                  pl.BlockSpec((CONV_WIDTH, CONV_CH), lambda b, i: (0, 0)),
                  sspec, row, row, row],
        out_specs=[pl.BlockSpec((nb, tm, GDN_WIDTH), lambda b, i: (b, i, 0)), sspec],
        out_shape=[jax.ShapeDtypeStruct((bsz, t, GDN_WIDTH), F32),
                   jax.ShapeDtypeStruct(s0.shape, F32)],
        scratch_shapes=[pltpu.VMEM((nb, GDN_HEADS, HEAD_DIM, HEAD_DIM), F32),
                        pltpu.VMEM((nb, SUBLANES, CONV_CH), F32)],
        compiler_params=_cparams(("arbitrary", "arbitrary")),
        name="gdn",
    )(proj, proj, proj, state_conv, conv_w, s0, alog_row, dtb_row, norm_w)


def _pool_kernel(u_ref, st_ref, w_ref, sc_ref, o_ref, prev_scr, *, nb, tm, tmp, start_pos):
    ti = pl.program_id(1)
    halo = 2 * SUBLANES

    @pl.when(ti == 0)
    def _():
        for b in range(nb):
            prev_scr[b] = jnp.concatenate([jnp.zeros((halo - POOL_BUF, POOL_WIDTH), F32), st_ref[b]], axis=0)

    lane = lax.broadcasted_iota(jnp.int32, (tmp, POOL_WIDTH), 1)
    rowi = lax.broadcasted_iota(jnp.int32, (tmp, POOL_WIDTH), 0)
    pos1 = start_pos + ti * tm + rowi + 1
    grp = lane // POOL_GROUP
    win = jnp.where(grp == 0, POOL_WINDOWS[0],
                    jnp.where(grp == 1, POOL_WINDOWS[1], jnp.where(grp == 2, POOL_WINDOWS[2], POOL_WINDOWS[3])))
    cnt = jnp.minimum(pos1, win).astype(F32)
    for b in range(nb):
        x = u_ref[b]
        if tm != tmp:
            x = jnp.concatenate([x, jnp.zeros((tmp - tm, POOL_WIDTH), F32)], axis=0)
        e = jnp.concatenate([prev_scr[b], x], axis=0)
        sums = []
        s = e
        for w in POOL_WINDOWS:
            s = s + pltpu.roll(s, w // 2, axis=0)
            sums.append(s[halo:])
        sel = jnp.where(grp == 0, sums[0], jnp.where(grp == 1, sums[1], jnp.where(grp == 2, sums[2], sums[3])))
        pooled = sel / cnt - x
        out = _dot(pooled, w_ref[...]) * sc_ref[...]
        o_ref[b] = out[:tm]
        if tm >= halo:
            prev_scr[b] = x[tm - halo:tm]


def _pool(proj, state_pool, w_bd, pool_scale, nb, tm, start_pos):
    bsz, t, _ = proj.shape
    tmp = max(tm, SUBLANES)
    kern = functools.partial(_pool_kernel, nb=nb, tm=tm, tmp=tmp, start_pos=start_pos)
    return pl.pallas_call(
        kern,
        grid=(bsz // nb, t // tm),
        in_specs=[pl.BlockSpec((nb, tm, POOL_WIDTH), lambda b, i: (b, i, OFF_U // POOL_WIDTH)),
                  pl.BlockSpec((nb, POOL_BUF, POOL_WIDTH), lambda b, i: (b, 0, 0)),
                  pl.BlockSpec((POOL_WIDTH, POOL_WIDTH), lambda b, i: (0, 0)),
                  pl.BlockSpec((1, POOL_WIDTH), lambda b, i: (0, 0))],
        out_specs=pl.BlockSpec((nb, tm, POOL_WIDTH), lambda b, i: (b, i, 0)),
        out_shape=jax.ShapeDtypeStruct((bsz, t, POOL_WIDTH), F32),
        scratch_shapes=[pltpu.VMEM((nb, 2 * SUBLANES, POOL_WIDTH), F32)],
        compiler_params=_cparams(("arbitrary", "arbitrary")),
        name="pool",
    )(proj, state_pool, w_bd, pool_scale.reshape(1, POOL_WIDTH))


def _out_kernel(x_ref, po_ref, og_ref, gt_ref, sc_ref, sh_ref, gpm_ref, gpf_ref, wo_ref, rw_ref, rb_ref,
                x1_ref, ht_ref, g_ref, *, tm):
    mix = _dot(po_ref[0], wo_ref[:POOL_WIDTH]) + _dot(og_ref[0], wo_ref[POOL_WIDTH:])
    x1 = x_ref[0] + gt_ref[0] * _rms(mix, gpm_ref[...])
    x1_ref[0] = x1
    h = _rms(x1, gpf_ref[...]) * (1.0 + sc_ref[0]) + sh_ref[0]
    for s in range(ROW_TILES):
        ht_ref[pl.ds(s, tm, stride=ROW_TILES), :] = h[:, s * LANES:(s + 1) * LANES]
    logits = _dot(h, rw_ref[...]) + rb_ref[...]
    lane = lax.broadcasted_iota(jnp.int32, logits.shape, 1)
    work = logits
    sel = jnp.zeros(logits.shape, jnp.bool_)
    top = None
    den = jnp.zeros((tm, 1), F32)
    for j in range(TOP_K):
        m = jnp.max(work, axis=-1, keepdims=True)
        idx = jnp.min(jnp.where(work == m, lane, N_EXPERTS), axis=-1, keepdims=True)
        hit = lane == idx
        sel = sel | hit
        work = jnp.where(hit, -jnp.inf, work)
        if j == 0:
            top = m
        den = den + jnp.exp(m - top)
    gates = jnp.exp(logits - top) / den
    g_ref[...] = jnp.where(sel, gates, -1.0)


def _out_proj(x, pool_out, o_gdn, gt, sc, sh, g_post_mix, g_pre_ffn, w_out_b, router_w, router_b, tm, per_token):
    bsz, t, _ = x.shape
    n = bsz * t
    tiles = t // tm

    def tok(width):
        return pl.BlockSpec((1, tm, width), lambda b, i: (b, i, 0))

    def const(shape):
        return pl.BlockSpec(shape, lambda b, i: tuple(0 for _ in shape))

    kern = functools.partial(_out_kernel, tm=tm)
    return pl.pallas_call(
        kern,
        grid=(bsz, tiles),
        in_specs=[tok(D_MODEL), tok(POOL_WIDTH), tok(GDN_WIDTH),
                  _mod_spec(per_token, tm), _mod_spec(per_token, tm), _mod_spec(per_token, tm),
                  const((1, D_MODEL)), const((1, D_MODEL)), const((D_MODEL, D_MODEL)),
                  const((D_MODEL, N_EXPERTS)), const((1, N_EXPERTS))],
        out_specs=[tok(D_MODEL),
                   pl.BlockSpec((tm * ROW_TILES, LANES), lambda b, i: (b * tiles + i, 0)),
                   pl.BlockSpec((tm, N_EXPERTS), lambda b, i: (b * tiles + i, 0))],
        out_shape=[jax.ShapeDtypeStruct((bsz, t, D_MODEL), F32),
                   jax.ShapeDtypeStruct((n * ROW_TILES, LANES), F32),
                   jax.ShapeDtypeStruct((n, N_EXPERTS), F32)],
        compiler_params=_cparams(("arbitrary", "arbitrary")),
        name="out_proj",
    )(x, pool_out, o_gdn, gt, sc, sh, g_post_mix.reshape(1, D_MODEL), g_pre_ffn.reshape(1, D_MODEL),
      w_out_b, router_w, router_b.reshape(1, N_EXPERTS))


def _rank_kernel(g_ref, r_ref, cnt_ref, carry, *, tm):
    i = pl.program_id(0)

    @pl.when(i == 0)
    def _():
        carry[...] = jnp.zeros_like(carry)

    mask = (g_ref[...] >= 0.0).astype(F32)
    ri = lax.broadcasted_iota(jnp.int32, (tm, tm), 0)
    ci = lax.broadcasted_iota(jnp.int32, (tm, tm), 1)
    before = (ri > ci).astype(F32)
    c = carry[...]
    r_ref[...] = _dot(before, mask) + c
    c = c + jnp.sum(mask, axis=0, keepdims=True)
    carry[...] = c
    cnt_ref[...] = c


def _rank(gsel, tm):
    n = gsel.shape[0]
    kern = functools.partial(_rank_kernel, tm=tm)
    return pl.pallas_call(
        kern,
        grid=(n // tm,),
        in_specs=[pl.BlockSpec((tm, N_EXPERTS), lambda i: (i, 0))],
        out_specs=[pl.BlockSpec((tm, N_EXPERTS), lambda i: (i, 0)),
                   pl.BlockSpec((1, N_EXPERTS), lambda i: (0, 0))],
        out_shape=[jax.ShapeDtypeStruct((n, N_EXPERTS), F32),
                   jax.ShapeDtypeStruct((1, N_EXPERTS), F32)],
        scratch_shapes=[pltpu.VMEM((1, N_EXPERTS), F32)],
        compiler_params=_cparams(("arbitrary",)),
        name="rank",
    )(gsel)


def _dest_kernel(g_ref, r_ref, start_ref, d_ref, gt_ref, *, tm):
    g = g_ref[...]
    mask = g >= 0.0
    maskf = mask.astype(F32)
    dest = start_ref[...] + r_ref[...]
    ri = lax.broadcasted_iota(jnp.int32, (N_EXPERTS, N_EXPERTS), 0)
    ci = lax.broadcasted_iota(jnp.int32, (N_EXPERTS, N_EXPERTS), 1)
    slot = _dot(maskf, (ri < ci).astype(F32))
    lane4 = lax.broadcasted_iota(jnp.int32, (tm, TOP_K), 1)
    d_out = jnp.zeros((tm, TOP_K), F32)
    g_out = jnp.zeros((tm, TOP_K), F32)
    for j in range(TOP_K):
        pick = mask & (slot == float(j))
        dj = jnp.sum(jnp.where(pick, dest, 0.0), axis=-1, keepdims=True)
        gj = jnp.sum(jnp.where(pick, g, 0.0), axis=-1, keepdims=True)
        d_out = jnp.where(lane4 == j, dj, d_out)
        g_out = jnp.where(lane4 == j, gj, g_out)
    d_ref[...] = d_out.astype(jnp.int32)
    gt_ref[...] = g_out


def _dest(gsel, rank, start_row, tm):
    n = gsel.shape[0]
    kern = functools.partial(_dest_kernel, tm=tm)
    return pl.pallas_call(
        kern,
        grid=(n // tm,),
        in_specs=[pl.BlockSpec((tm, N_EXPERTS), lambda i: (i, 0)),
                  pl.BlockSpec((tm, N_EXPERTS), lambda i: (i, 0)),
                  pl.BlockSpec((1, N_EXPERTS), lambda i: (0, 0))],
        out_specs=[pl.BlockSpec((tm, TOP_K), lambda i: (i, 0)),
                   pl.BlockSpec((tm, TOP_K), lambda i: (i, 0))],
        out_shape=[jax.ShapeDtypeStruct((n, TOP_K), jnp.int32),
                   jax.ShapeDtypeStruct((n, TOP_K), F32)],
        compiler_params=_cparams(("arbitrary",)),
        name="dest",
    )(gsel, rank, start_row)


def _row_copy(src, dst, s_row, d_row, sem):
    return pltpu.make_async_copy(src.at[pl.ds(pl.multiple_of(s_row * ROW_TILES, ROW_TILES), ROW_TILES), :],
                                 dst.at[pl.ds(pl.multiple_of(d_row * ROW_TILES, ROW_TILES), ROW_TILES), :], sem)


def _scatter_kernel(cnt_ref, start_ref, end_ref, nu_ref, dest_ref, ht_ref, xs_ref, sem, *, tm, bm, n_blocks):
    i = pl.program_id(0)

    def body(t, carry):
        src = ht_ref.at[pl.ds(pl.multiple_of(t * ROW_TILES, ROW_TILES), ROW_TILES), :]
        for k in range(TOP_K):
            d = dest_ref[t * TOP_K + k]
            pltpu.make_async_copy(src, xs_ref.at[pl.ds(pl.multiple_of(d * ROW_TILES, ROW_TILES), ROW_TILES), :],
                                  sem).start(priority=k % DMA_QUEUES)
        return carry

    lax.fori_loop(0, tm, body, 0)

    def tile_copy(u):
        return pltpu.make_async_copy(
            ht_ref, xs_ref.at[pl.ds(pl.multiple_of(u * (tm * ROW_TILES), tm * ROW_TILES), tm * ROW_TILES), :], sem)

    for k in range(TOP_K):
        tile_copy(0).wait()

    @pl.when(i == pl.num_programs(0) - 1)
    def _():
        per = bm // tm

        def tail_start(u, carry):
            tile_copy(u).start()
            return carry

        def tail_wait(u, carry):
            tile_copy(u).wait()
            return carry

        lax.fori_loop(nu_ref[0] * per, n_blocks * per, tail_start, 0)
        lax.fori_loop(nu_ref[0] * per, n_blocks * per, tail_wait, 0)

        def per_expert(e, carry):
            lo = start_ref[e] + cnt_ref[e]
            hi = end_ref[e]

            def fill(r, c2):
                _row_copy(ht_ref, xs_ref, 0, r, sem).start()
                return c2

            lax.fori_loop(lo, hi, fill, 0)

            def drain(r, c2):
                _row_copy(ht_ref, xs_ref, 0, r, sem).wait()
                return c2

            lax.fori_loop(lo, hi, drain, 0)
            return carry

        lax.fori_loop(0, N_EXPERTS, per_expert, 0)


def _scatter_rows(ht, dest_flat, cnt, start, end, n_used, n_blocks, tm, bm):
    n = ht.shape[0] // ROW_TILES
    rows = n_blocks * bm
    assert bm % tm == 0
    kern = functools.partial(_scatter_kernel, tm=tm, bm=bm, n_blocks=n_blocks)
    return pl.pallas_call(
        kern,
        grid_spec=pltpu.PrefetchScalarGridSpec(
            num_scalar_prefetch=4,
            grid=(n // tm,),
            in_specs=[pl.BlockSpec((tm * TOP_K,), lambda i, *_: (i,), memory_space=pltpu.SMEM),
                      pl.BlockSpec((tm * ROW_TILES, LANES), lambda i, *_: (i, 0))],
            out_specs=pl.BlockSpec(memory_space=pl.ANY),
            scratch_shapes=[pltpu.SemaphoreType.DMA(())]),
        out_shape=jax.ShapeDtypeStruct((rows * ROW_TILES, LANES), F32),
        compiler_params=_cparams(("arbitrary",)),
        name="scatter_rows",
    )(cnt, start, end, n_used, dest_flat, ht)


def _expert_kernel(be_ref, nu_ref, xs_ref, wg_ref, bg_ref, wd_ref, bd_ref, ys_ref, wg_s, wd_s, *, bm, fchunk):
    j = pl.program_id(0)
    prev = be_ref[jnp.maximum(j - 1, 0)]
    fresh = (j == 0) | (be_ref[j] != prev)
    active = j < nu_ref[0]

    @pl.when(active & fresh)
    def _():
        wg_s[...] = wg_ref[0].astype(BF16)
        wd_s[...] = wd_ref[0].astype(BF16)

    @pl.when(active)
    def _():
        x = jnp.concatenate([xs_ref[pl.ds(s, bm, stride=ROW_TILES), :] for s in range(ROW_TILES)], axis=-1)
        xb = x.astype(BF16)
        acc = jnp.zeros((bm, D_MODEL), F32) + bd_ref[0]
        for f in range(D_FF // fchunk):
            lo = f * fchunk
            gate = jnp.dot(xb, wg_s[:, lo:lo + fchunk], preferred_element_type=F32) + bg_ref[0, :, lo:lo + fchunk]
            up = (jnp.dot(xb, wg_s[:, D_FF + lo:D_FF + lo + fchunk], preferred_element_type=F32)
                  + bg_ref[0, :, D_FF + lo:D_FF + lo + fchunk])
            gate = jnp.minimum(gate, SWIGLU_LIMIT)
            up = jnp.clip(up, -SWIGLU_LIMIT, SWIGLU_LIMIT)
            act = (up + 1.0) * (gate * jax.nn.sigmoid(SWIGLU_ALPHA * gate))
            acc = acc + jnp.dot(act.astype(BF16), wd_s[lo:lo + fchunk, :], preferred_element_type=F32)
        for s in range(ROW_TILES):
            ys_ref[pl.ds(s, bm, stride=ROW_TILES), :] = acc[:, s * LANES:(s + 1) * LANES]

    @pl.when(jnp.logical_not(active))
    def _():
        ys_ref[...] = jnp.zeros_like(ys_ref)


def _experts(xs, block_expert, n_used, w_gu, b_gu, w_down, b_down, bm):
    rows = xs.shape[0] // ROW_TILES
    n_blocks = rows // bm

    def blk(j, be, nu):
        return (jnp.minimum(j, nu[0] - 1), 0)

    def wsel(j, be, nu):
        return (be[j], 0, 0)

    kern = functools.partial(_expert_kernel, bm=bm, fchunk=D_FF)
    return pl.pallas_call(
        kern,
        grid_spec=pltpu.PrefetchScalarGridSpec(
            num_scalar_prefetch=2,
            grid=(n_blocks,),
            in_specs=[pl.BlockSpec((bm * ROW_TILES, LANES), blk),
                      pl.BlockSpec((1, D_MODEL, 2 * D_FF), wsel),
                      pl.BlockSpec((1, 1, 2 * D_FF), wsel),
                      pl.BlockSpec((1, D_FF, D_MODEL), wsel),
                      pl.BlockSpec((1, 1, D_MODEL), wsel)],
            out_specs=pl.BlockSpec((bm * ROW_TILES, LANES), lambda j, be, nu: (j, 0)),
            scratch_shapes=[pltpu.VMEM((D_MODEL, 2 * D_FF), BF16),
                            pltpu.VMEM((D_FF, D_MODEL), BF16)]),
        out_shape=jax.ShapeDtypeStruct((rows * ROW_TILES, LANES), F32),
        compiler_params=_cparams(("arbitrary",)),
        name="experts",
    )(block_expert, n_used, xs, w_gu, b_gu.reshape(N_EXPERTS, 1, 2 * D_FF), w_down,
      b_down.reshape(N_EXPERTS, 1, D_MODEL))


def _final_kernel(dest_ref, ys_ref, gate_ref, x1_ref, gt_ref, g_ref, o_ref, sl_ref, sem, *, tm):
    def body(t, carry):
        for k in range(TOP_K):
            j = t * TOP_K + k
            _row_copy(ys_ref, sl_ref, dest_ref[j], j, sem).start(priority=k % DMA_QUEUES)
        return carry

    lax.fori_loop(0, tm, body, 0)
    pltpu.make_async_copy(ys_ref.at[pl.ds(0, tm * TOP_K * ROW_TILES), :], sl_ref, sem).wait()

    gates = gate_ref[...]
    per_tok = TOP_K * ROW_TILES
    cols = []
    for s in range(ROW_TILES):
        acc = None
        for j in range(TOP_K):
            part = sl_ref[pl.ds(j * ROW_TILES + s, tm, stride=per_tok), :] * gates[:, j:j + 1]
            acc = part if acc is None else acc + part
        cols.append(acc)
    f = jnp.concatenate(cols, axis=-1)
    o_ref[0] = x1_ref[0] + gt_ref[0] * _rms(f, g_ref[...])


def _final(ys, dest_flat, gates, x1, gt, g_post_ffn, tok_off, tm, per_token):
    bsz, t, _ = x1.shape
    tiles = t // tm
    off = tok_off // tm
    kern = functools.partial(_final_kernel, tm=tm)
    return pl.pallas_call(
        kern,
        grid=(bsz, tiles),
        in_specs=[pl.BlockSpec((tm * TOP_K,), lambda b, i: (off + b * tiles + i,), memory_space=pltpu.SMEM),
                  pl.BlockSpec(memory_space=pl.ANY),
                  pl.BlockSpec((tm, TOP_K), lambda b, i: (off + b * tiles + i, 0)),
                  pl.BlockSpec((1, tm, D_MODEL), lambda b, i: (b, i, 0)),
                  _mod_spec(per_token, tm),
                  pl.BlockSpec((1, D_MODEL), lambda b, i: (0, 0))],
        out_specs=pl.BlockSpec((1, tm, D_MODEL), lambda b, i: (b, i, 0)),
        out_shape=jax.ShapeDtypeStruct((bsz, t, D_MODEL), F32),
        scratch_shapes=[pltpu.VMEM((tm * TOP_K * ROW_TILES, LANES), F32), pltpu.SemaphoreType.DMA(())],
        compiler_params=_cparams(("arbitrary", "arbitrary")),
        name="final",
    )(dest_flat, ys, gates, x1, gt, g_post_ffn.reshape(1, D_MODEL))


def _moe(ht, gsel, w_gu, b_gu, w_down, b_down, tm, bm):
    n = gsel.shape[0]
    rank, counts = _rank(gsel, tm)
    cnt = counts[0].astype(jnp.int32)
    nblk = (cnt + bm - 1) // bm
    blk_end = jnp.cumsum(nblk)
    blk_start = blk_end - nblk
    n_blocks = -(-(n * TOP_K) // bm) + N_EXPERTS
    blocks = jnp.arange(n_blocks, dtype=jnp.int32)
    block_expert = jnp.minimum(jnp.sum((blk_end[None, :] <= blocks[:, None]).astype(jnp.int32), axis=1),
                               N_EXPERTS - 1).astype(jnp.int32)
    n_used = blk_end[-1:].astype(jnp.int32)
    row_start = (blk_start * bm).astype(jnp.int32)
    row_end = (blk_end * bm).astype(jnp.int32)
    dest, gates = _dest(gsel, rank, row_start.astype(F32).reshape(1, N_EXPERTS), tm)
    dest_flat = dest.reshape(n * TOP_K)
    xs = _scatter_rows(ht, dest_flat, cnt, row_start, row_end, n_used, n_blocks, tm, bm)
    ys = _experts(xs, block_expert, n_used, w_gu, b_gu, w_down, b_down, bm)
    return ys, dest_flat, gates


def _mix_path(x, mod, pool_buf, conv_buf, s0, start_pos, lp, tm, nb, per_token):
    (g_pre_mix, g_post_mix, g_pre_ffn, w_in_p, w_bd, pool_scale, conv_w, alog_row, dtb_row, norm_w,
     w_out_b, router_w, router_b) = lp
    bsz, t, _ = x.shape
    sh_mix, sc_mix, gt_mix, sh_ffn, sc_ffn, gt_ffn = (mod[:, i] for i in range(6))

    def shape_mod(m):
        if per_token:
            return jnp.repeat(m, t, axis=0).reshape(1, bsz * t, D_MODEL)
        return m.reshape(bsz, 1, D_MODEL)

    sh_mix, sc_mix, gt_mix, sh_ffn, sc_ffn, gt_ffn = map(shape_mod, (sh_mix, sc_mix, gt_mix, sh_ffn, sc_ffn, gt_ffn))
    xf = x.reshape(1, bsz * t, D_MODEL) if per_token else x
    proj = _in_proj(xf, sc_mix, sh_mix, g_pre_mix, w_in_p, tm, per_token).reshape(bsz, t, PROJ_WIDTH)
    tseq = min(tm, t)
    o_gdn, s_new = _gdn(proj, conv_buf, s0, conv_w, alog_row, dtb_row, norm_w, nb, tseq)
    pool_out = _pool(proj, pool_buf, w_bd, pool_scale, nb, tseq, start_pos)
    fl = (lambda a: a.reshape(1, bsz * t, a.shape[-1])) if per_token else (lambda a: a)
    x1, ht, gsel = _out_proj(xf, fl(pool_out), fl(o_gdn), gt_mix, sc_ffn, sh_ffn, g_post_mix, g_pre_ffn,
                             w_out_b, router_w, router_b, tm, per_token)
    def last_rows(buf, off, width):
        keep = buf.shape[1]
        if t >= keep:
            return proj[:, t - keep:, off:off + width]
        return jnp.concatenate([buf[:, t:], proj[:, :, off:off + width]], axis=1)

    pool_new = last_rows(pool_buf, OFF_U, POOL_WIDTH)
    conv_new = last_rows(conv_buf, OFF_QKV, CONV_CH)
    return x1, ht, gsel, gt_ffn, pool_new, conv_new, s_new


def kernel(x_prompt, x_sample, c_prompt, c_sample, state_pool, state_conv, state_ssm, w_ada, b_ada, g_pre_mix, g_post_mix, g_pre_ffn, g_post_ffn, w_in, w_pool, pool_scale, conv_w, a_log, dt_bias, gdn_norm_w, w_out, router_w, router_b, w_gu, b_gu, w_down, b_down):
    depth = w_ada.shape[0]
    bp, tp, _ = x_prompt.shape
    bs, ts, _ = x_sample.shape
    tm = 256
    y_p, y_s = x_prompt, x_sample
    outs = [[] for _ in range(6)]
    for l in range(depth):
        mod = _ada(jnp.concatenate([c_prompt, c_sample], axis=0), w_ada[l], b_ada[l]).reshape(bp + bs, 6, D_MODEL)
        o1, o2, o3 = POOL_WIDTH, POOL_WIDTH + CONV_CH, POOL_WIDTH + CONV_CH + GDN_WIDTH
        w_l = w_in[l]
        w_in_p = jnp.concatenate(
            [w_l[:, o1:o2], w_l[:, o2:o3], w_l[:, :o1], w_l[:, o3:],
             jnp.zeros((D_MODEL, LANES - (w_l.shape[1] - o3)), F32)], axis=1).astype(BF16)
        w_bd = jax.scipy.linalg.block_diag(*[w_pool[l, g] for g in range(len(POOL_WINDOWS))])
        pad_row = lambda v: jnp.pad(v, (0, LANES - v.shape[0])).reshape(1, LANES)
        lp = (g_pre_mix[l], g_post_mix[l], g_pre_ffn[l], w_in_p, w_bd, pool_scale[l], conv_w[l],
              pad_row(a_log[l]), pad_row(dt_bias[l]), gdn_norm_w[l].reshape(1, HEAD_DIM),
              w_out[l].astype(BF16), router_w[l], router_b[l])
        zero_pool = jnp.zeros((bp, POOL_BUF, POOL_WIDTH), F32)
        zero_conv = jnp.zeros((bp, CONV_WIDTH - 1, CONV_CH), F32)
        zero_ssm = jnp.zeros((bp, GDN_HEADS, HEAD_DIM, HEAD_DIM), F32)
        x1p, htp, gp, gtp, pp, cp, sp = _mix_path(y_p, mod[:bp], zero_pool, zero_conv, zero_ssm, 0, lp, tm, 1, False)
        x1s, hts, gs, gts, ps, cs, ss = _mix_path(y_s, mod[bp:], state_pool[l], state_conv[l], state_ssm[l],
                                                  PAST_LEN, lp, tm, 8, True)
        ht = jnp.concatenate([htp, hts], axis=0)
        gsel = jnp.concatenate([gp, gs], axis=0)
        ys, dest_flat, gates = _moe(ht, gsel, w_gu[l], b_gu[l], w_down[l], b_down[l], tm, MOE_BLOCK)
        y_p = _final(ys, dest_flat, gates, x1p, gtp, g_post_ffn[l], 0, tm, False)
        y_s = _final(ys, dest_flat, gates, x1s, gts, g_post_ffn[l], bp * tp, tm, True).reshape(bs, ts, D_MODEL)
        for lst, val in zip(outs, (pp, cp, sp, ps, cs, ss)):
            lst.append(val)
    return (y_p, y_s) + tuple(jnp.stack(o) for o in outs)
```

```python
import functools
import math

import jax
import jax.numpy as jnp
from jax import lax
from jax.experimental import pallas as pl
from jax.experimental.pallas import tpu as pltpu

F32 = jnp.float32
BF16 = jnp.bfloat16

D_MODEL = 1024
PAST_LEN = 16384
POOL_WINDOWS = (2, 4, 8, 16)
POOL_WIDTH = D_MODEL // 4
POOL_GROUP = POOL_WIDTH // len(POOL_WINDOWS)
POOL_BUF = max(POOL_WINDOWS) - 1
GDN_WIDTH = D_MODEL - POOL_WIDTH
HEAD_DIM = 128
GDN_HEADS = GDN_WIDTH // HEAD_DIM
CONV_WIDTH = 4
CONV_CH = 3 * GDN_WIDTH
GDN_CHUNK = 64
N_EXPERTS = 32
TOP_K = 4
D_FF = D_MODEL
SWIGLU_LIMIT = 7.0
SWIGLU_ALPHA = 1.702
EPS = 1e-6

SUBLANES = 8
LANES = 128
ROW_TILES = D_MODEL // LANES
OFF_QKV = 0
OFF_Z = OFF_QKV + CONV_CH
OFF_U = OFF_Z + GDN_WIDTH
OFF_AB = OFF_U + POOL_WIDTH
PROJ_WIDTH = OFF_AB + LANES
assert OFF_Z % GDN_WIDTH == 0 and OFF_U % POOL_WIDTH == 0 and OFF_AB % LANES == 0
MOE_BLOCK = 512
VMEM_LIMIT = 56 * 1024 * 1024


def _cparams(sem):
    return pltpu.CompilerParams(dimension_semantics=sem, vmem_limit_bytes=VMEM_LIMIT)


def _dot(a, b):
    return jnp.dot(a.astype(BF16), b.astype(BF16), preferred_element_type=F32)


def _dot_nt(a, b):
    return lax.dot_general(a.astype(BF16), b.astype(BF16), (((1,), (1,)), ((), ())),
                           preferred_element_type=F32)


def _silu(x):
    return x * jax.nn.sigmoid(x)


def _rms(x, g):
    return x * lax.rsqrt(jnp.mean(x * x, axis=-1, keepdims=True) + EPS) * g


def _ada_kernel(c_ref, w_ref, b_ref, o_ref):
    o_ref[...] = _dot(_silu(c_ref[...]), w_ref[...]) + b_ref[...]


def _ada(c, w_ada, b_ada):
    n = c.shape[0]
    cols = w_ada.shape[1]
    blk = D_MODEL
    return pl.pallas_call(
        _ada_kernel,
        grid=(cols // blk,),
        in_specs=[pl.BlockSpec((n, D_MODEL), lambda j: (0, 0)),
                  pl.BlockSpec((D_MODEL, blk), lambda j: (0, j)),
                  pl.BlockSpec((1, blk), lambda j: (0, j))],
        out_specs=pl.BlockSpec((n, blk), lambda j: (0, j)),
        out_shape=jax.ShapeDtypeStruct((n, cols), F32),
        compiler_params=_cparams(("arbitrary",)),
        name="ada",
    )(c, w_ada, b_ada.reshape(1, cols))


def _in_kernel(x_ref, sc_ref, sh_ref, g_ref, w_ref, o_ref):
    h = _rms(x_ref[0], g_ref[...]) * (1.0 + sc_ref[0]) + sh_ref[0]
    o_ref[0] = jnp.dot(h.astype(BF16), w_ref[...], preferred_element_type=F32)


def _mod_spec(per_token, tm):
    if per_token:
        return pl.BlockSpec((1, tm, D_MODEL), lambda b, t: (0, t, 0))
    return pl.BlockSpec((1, 1, D_MODEL), lambda b, t: (b, 0, 0))


def _in_proj(x, sc, sh, g, w_in_p, tm, per_token):
    bsz, t, _ = x.shape
    return pl.pallas_call(
        _in_kernel,
        grid=(bsz, t // tm),
        in_specs=[pl.BlockSpec((1, tm, D_MODEL), lambda b, i: (b, i, 0)),
                  _mod_spec(per_token, tm), _mod_spec(per_token, tm),
                  pl.BlockSpec((1, D_MODEL), lambda b, i: (0, 0)),
                  pl.BlockSpec((D_MODEL, PROJ_WIDTH), lambda b, i: (0, 0))],
        out_specs=pl.BlockSpec((1, tm, PROJ_WIDTH), lambda b, i: (b, i, 0)),
        out_shape=jax.ShapeDtypeStruct((bsz, t, PROJ_WIDTH), F32),
        compiler_params=_cparams(("arbitrary", "arbitrary")),
        name="in_proj",
    )(x, sc, sh, g.reshape(1, D_MODEL), w_in_p)


def _shift_rows(prev, x, d):
    if d == 0:
        return x
    e = jnp.concatenate([prev, x], axis=0)
    return pltpu.roll(e, d, axis=0)[SUBLANES:]


def _cumsum_rows(x):
    n = x.shape[0]
    row = lax.broadcasted_iota(jnp.int32, x.shape, 0)
    s = 1
    while s < n:
        x = x + jnp.where(row >= s, pltpu.roll(x, s, axis=0), 0.0)
        s *= 2
    return x


def _gdn_kernel(qkv_ref, z_ref, ab_ref, cs_ref, cw_ref, s0_ref, alog_ref, dtb_ref, nw_ref, o_ref, sn_ref,
                s_scr, prev_scr, *, nb, tm, tmp, chunk):
    ti = pl.program_id(1)
    nt = pl.num_programs(1)
    n_chunks = tmp // chunk
    n_sq = int(math.log2(chunk)) - 1

    @pl.when(ti == 0)
    def _():
        s_scr[...] = s0_ref[...]
        for b in range(nb):
            prev_scr[b] = jnp.concatenate(
                [jnp.zeros((SUBLANES - (CONV_WIDTH - 1), CONV_CH), F32), cs_ref[b]], axis=0)

    rowi = lax.broadcasted_iota(jnp.int32, (tmp, LANES), 0)
    ri = lax.broadcasted_iota(jnp.int32, (chunk, chunk), 0)
    ci = lax.broadcasted_iota(jnp.int32, (chunk, chunk), 1)
    tril = ri >= ci
    strict = ri > ci
    eye = (ri == ci).astype(F32)
    neg_a = -jnp.exp(alog_ref[...])
    dtb = dtb_ref[...]
    nw = nw_ref[...]

    def padrows(x):
        if tm == tmp:
            return x
        return jnp.concatenate([x, jnp.zeros((tmp - tm, x.shape[1]), F32)], axis=0)

    cw = cw_ref[...]
    valid = rowi < tm
    items = []
    for b in range(nb):
        x = padrows(qkv_ref[b])
        prev = prev_scr[b]
        acc = x * cw[CONV_WIDTH - 1:CONV_WIDTH]
        for d in range(1, CONV_WIDTH):
            acc = acc + _shift_rows(prev, x, d) * cw[CONV_WIDTH - 1 - d:CONV_WIDTH - d]
        if tm >= SUBLANES:
            prev_scr[b] = x[tm - SUBLANES:tm]
        y = _silu(acc)
        ab = padrows(ab_ref[b])
        g_all = jnp.where(valid, neg_a * jax.nn.softplus(ab + dtb), 0.0)
        beta_all = jnp.where(valid, jax.nn.sigmoid(ab), 0.0)
        for h in range(GDN_HEADS):
            qc = y[:, h * HEAD_DIM:(h + 1) * HEAD_DIM]
            kc = y[:, GDN_WIDTH + h * HEAD_DIM:GDN_WIDTH + (h + 1) * HEAD_DIM]
            v = y[:, 2 * GDN_WIDTH + h * HEAD_DIM:2 * GDN_WIDTH + (h + 1) * HEAD_DIM]
            q = qc * lax.rsqrt(jnp.sum(qc * qc, axis=-1, keepdims=True) + EPS) * (HEAD_DIM ** -0.5)
            k = kc * lax.rsqrt(jnp.sum(kc * kc, axis=-1, keepdims=True) + EPS)
            gb = jnp.broadcast_to(g_all[:, h:h + 1], (tmp, LANES))
            bb = jnp.broadcast_to(beta_all[:, GDN_HEADS + h:GDN_HEADS + h + 1], (tmp, LANES))
            for c in range(n_chunks):
                sl = slice(c * chunk, (c + 1) * chunk)
                items.append(dict(b=b, h=h, c=c, q=q[sl], k=k[sl], v=v[sl], bb=bb[sl], gb=gb[sl]))

    for it in items:
        gcb = _cumsum_rows(it["gb"])
        gct = jnp.transpose(gcb)[:chunk]
        diff = gcb[:, :chunk] - gct
        it["decay"] = jnp.where(tril, jnp.exp(jnp.where(tril, diff, 0.0)), 0.0)
        it["kb"] = it["k"] * it["bb"]
        it["eg"] = jnp.exp(gcb)
        gl = gcb[chunk - 1:chunk]
        it["egl"] = jnp.exp(gl)
        it["kdt"] = jnp.transpose(it["k"] * jnp.exp(gl - gcb))
    for it in items:
        kq = _dot_nt(jnp.concatenate([it["kb"], it["q"]], axis=0), it["k"])
        it["a"] = jnp.where(strict, kq[:chunk] * it["decay"], 0.0)
        it["qk"] = jnp.where(tril, kq[chunk:] * it["decay"], 0.0)
    for it in items:
        it["inv"] = eye - it["a"]
        it["p"] = _dot(it["a"], it["a"])
    for lvl in range(n_sq):
        for it in items:
            if lvl < n_sq - 1:
                r = _dot(jnp.concatenate([it["inv"], it["p"]], axis=0), it["p"])
                it["inv"] = it["inv"] + r[:chunk]
                it["p"] = r[chunk:]
            else:
                it["inv"] = it["inv"] + _dot(it["inv"], it["p"])
    for it in items:
        it["wu"] = _dot(it["inv"], jnp.concatenate([it["kb"] * it["eg"], it["v"] * it["bb"]], axis=1))
    for it in items:
        mn = _dot(it["kdt"], it["wu"])
        qo = _dot(it["qk"], it["wu"])
        it["mp"], it["nc"] = mn[:, :HEAD_DIM], mn[:, HEAD_DIM:]
        it["qp"] = it["q"] * it["eg"] - qo[:, :HEAD_DIM]
        it["oc"] = qo[:, HEAD_DIM:]

    state = {(b, h): s_scr[b, h] for b in range(nb) for h in range(GDN_HEADS)}
    for c in range(n_chunks):
        for it in items:
            if it["c"] != c:
                continue
            s = state[(it["b"], it["h"])]
            r = _dot(jnp.concatenate([it["qp"], it["mp"]], axis=0), s)
            it["o"] = r[:chunk] + it["oc"]
            state[(it["b"], it["h"])] = s * it["egl"] - r[chunk:] + it["nc"]
    for (b, h), s in state.items():
        s_scr[b, h] = s

    for b in range(nb):
        zf = padrows(z_ref[b])
        for h in range(GDN_HEADS):
            outs = [it["o"] for it in items if it["b"] == b and it["h"] == h]
            o = outs[0] if n_chunks == 1 else jnp.concatenate(outs, axis=0)
            o = (o * lax.rsqrt(jnp.mean(o * o, axis=-1, keepdims=True) + EPS) * nw
                 * _silu(zf[:, h * HEAD_DIM:(h + 1) * HEAD_DIM]))
            o_ref[b, :, h * HEAD_DIM:(h + 1) * HEAD_DIM] = o[:tm]

    @pl.when(ti == nt - 1)
    def _():
        sn_ref[...] = s_scr[...]


def _gdn(proj, state_conv, s0, conv_w, alog_row, dtb_row, norm_w, nb, tm):
    bsz, t, _ = proj.shape
    tmp = max(tm, SUBLANES)
    chunk = min(GDN_CHUNK, tmp)
    row = pl.BlockSpec((1, LANES), lambda b, i: (0, 0))
    sspec = pl.BlockSpec((nb, GDN_HEADS, HEAD_DIM, HEAD_DIM), lambda b, i: (b, 0, 0, 0))
    kern = functools.partial(_gdn_kernel, nb=nb, tm=tm, tmp=tmp, chunk=chunk)
    return pl.pallas_call(
        kern,
        grid=(bsz // nb, t // tm),
        in_specs=[pl.BlockSpec((nb, tm, CONV_CH), lambda b, i: (b, i, OFF_QKV // CONV_CH)),
                  pl.BlockSpec((nb, tm, GDN_WIDTH), lambda b, i: (b, i, OFF_Z // GDN_WIDTH)),
                  pl.BlockSpec((nb, tm, LANES), lambda b, i: (b, i, OFF_AB // LANES)),
                  pl.BlockSpec((nb, CONV_WIDTH - 1, CONV_CH), lambda b, i: (b, 0, 0)),
                  pl.BlockSpec((CONV_WIDTH, CONV_CH), lambda b, i: (0, 0)),
                  sspec, row, row, row],
        out_specs=[pl.BlockSpec((nb, tm, GDN_WIDTH), lambda b, i: (b, i, 0)), sspec],
        out_shape=[jax.ShapeDtypeStruct((bsz, t, GDN_WIDTH), F32),
                   jax.ShapeDtypeStruct(s0.shape, F32)],
        scratch_shapes=[pltpu.VMEM((nb, GDN_HEADS, HEAD_DIM, HEAD_DIM), F32),
                        pltpu.VMEM((nb, SUBLANES, CONV_CH), F32)],
        compiler_params=_cparams(("arbitrary", "arbitrary")),
        name="gdn",
    )(proj, proj, proj, state_conv, conv_w, s0, alog_row, dtb_row, norm_w)


def _pool_kernel(u_ref, st_ref, w_ref, sc_ref, o_ref, prev_scr, *, nb, tm, tmp, start_pos):
    ti = pl.program_id(1)
    halo = 2 * SUBLANES

    @pl.when(ti == 0)
    def _():
        for b in range(nb):
            prev_scr[b] = jnp.concatenate([jnp.zeros((halo - POOL_BUF, POOL_WIDTH), F32), st_ref[b]], axis=0)

    lane = lax.broadcasted_iota(jnp.int32, (tmp, POOL_WIDTH), 1)
    rowi = lax.broadcasted_iota(jnp.int32, (tmp, POOL_WIDTH), 0)
    pos1 = start_pos + ti * tm + rowi + 1
    grp = lane // POOL_GROUP
    win = jnp.where(grp == 0, POOL_WINDOWS[0],
                    jnp.where(grp == 1, POOL_WINDOWS[1], jnp.where(grp == 2, POOL_WINDOWS[2], POOL_WINDOWS[3])))
    cnt = jnp.minimum(pos1, win).astype(F32)
    for b in range(nb):
        x = u_ref[b]
        if tm != tmp:
            x = jnp.concatenate([x, jnp.zeros((tmp - tm, POOL_WIDTH), F32)], axis=0)
        e = jnp.concatenate([prev_scr[b], x], axis=0)
        sums = []
        s = e
        for w in POOL_WINDOWS:
            s = s + pltpu.roll(s, w // 2, axis=0)
            sums.append(s[halo:])
        sel = jnp.where(grp == 0, sums[0], jnp.where(grp == 1, sums[1], jnp.where(grp == 2, sums[2], sums[3])))
        pooled = sel / cnt - x
        out = _dot(pooled, w_ref[...]) * sc_ref[...]
        o_ref[b] = out[:tm]
        if tm >= halo:
            prev_scr[b] = x[tm - halo:tm]


def _pool(proj, state_pool, w_bd, pool_scale, nb, tm, start_pos):
    bsz, t, _ = proj.shape
    tmp = max(tm, SUBLANES)
    kern = functools.partial(_pool_kernel, nb=nb, tm=tm, tmp=tmp, start_pos=start_pos)
    return pl.pallas_call(
        kern,
        grid=(bsz // nb, t // tm),
        in_specs=[pl.BlockSpec((nb, tm, POOL_WIDTH), lambda b, i: (b, i, OFF_U // POOL_WIDTH)),
                  pl.BlockSpec((nb, POOL_BUF, POOL_WIDTH), lambda b, i: (b, 0, 0)),
                  pl.BlockSpec((POOL_WIDTH, POOL_WIDTH), lambda b, i: (0, 0)),
                  pl.BlockSpec((1, POOL_WIDTH), lambda b, i: (0, 0))],
        out_specs=pl.BlockSpec((nb, tm, POOL_WIDTH), lambda b, i: (b, i, 0)),
        out_shape=jax.ShapeDtypeStruct((bsz, t, POOL_WIDTH), F32),
        scratch_shapes=[pltpu.VMEM((nb, 2 * SUBLANES, POOL_WIDTH), F32)],
        compiler_params=_cparams(("arbitrary", "arbitrary")),
        name="pool",
    )(proj, state_pool, w_bd, pool_scale.reshape(1, POOL_WIDTH))


def _out_kernel(x_ref, po_ref, og_ref, gt_ref, sc_ref, sh_ref, gpm_ref, gpf_ref, wo_ref, rw_ref, rb_ref,
                x1_ref, ht_ref, g_ref, *, tm):
    mix = _dot(po_ref[0], wo_ref[:POOL_WIDTH]) + _dot(og_ref[0], wo_ref[POOL_WIDTH:])
    x1 = x_ref[0] + gt_ref[0] * _rms(mix, gpm_ref[...])
    x1_ref[0] = x1
    h = _rms(x1, gpf_ref[...]) * (1.0 + sc_ref[0]) + sh_ref[0]
    for s in range(ROW_TILES):
        ht_ref[pl.ds(s, tm, stride=ROW_TILES), :] = h[:, s * LANES:(s + 1) * LANES]
    logits = _dot(h, rw_ref[...]) + rb_ref[...]
    lane = lax.broadcasted_iota(jnp.int32, logits.shape, 1)
    work = logits
    sel = jnp.zeros(logits.shape, jnp.bool_)
    top = None
    den = jnp.zeros((tm, 1), F32)
    for j in range(TOP_K):
        m = jnp.max(work, axis=-1, keepdims=True)
        idx = jnp.min(jnp.where(work == m, lane, N_EXPERTS), axis=-1, keepdims=True)
        hit = lane == idx
        sel = sel | hit
        work = jnp.where(hit, -jnp.inf, work)
        if j == 0:
            top = m
        den = den + jnp.exp(m - top)
    gates = jnp.exp(logits - top) / den
    g_ref[...] = jnp.where(sel, gates, -1.0)


def _out_proj(x, pool_out, o_gdn, gt, sc, sh, g_post_mix, g_pre_ffn, w_out_b, router_w, router_b, tm, per_token):
    bsz, t, _ = x.shape
    n = bsz * t
    tiles = t // tm

    def tok(width):
        return pl.BlockSpec((1, tm, width), lambda b, i: (b, i, 0))

    def const(shape):
        return pl.BlockSpec(shape, lambda b, i: tuple(0 for _ in shape))

    kern = functools.partial(_out_kernel, tm=tm)
    return pl.pallas_call(
        kern,
        grid=(bsz, tiles),
        in_specs=[tok(D_MODEL), tok(POOL_WIDTH), tok(GDN_WIDTH),
                  _mod_spec(per_token, tm), _mod_spec(per_token, tm), _mod_spec(per_token, tm),
                  const((1, D_MODEL)), const((1, D_MODEL)), const((D_MODEL, D_MODEL)),
                  const((D_MODEL, N_EXPERTS)), const((1, N_EXPERTS))],
        out_specs=[tok(D_MODEL),
                   pl.BlockSpec((tm * ROW_TILES, LANES), lambda b, i: (b * tiles + i, 0)),
                   pl.BlockSpec((tm, N_EXPERTS), lambda b, i: (b * tiles + i, 0))],
        out_shape=[jax.ShapeDtypeStruct((bsz, t, D_MODEL), F32),
                   jax.ShapeDtypeStruct((n * ROW_TILES, LANES), F32),
                   jax.ShapeDtypeStruct((n, N_EXPERTS), F32)],
        compiler_params=_cparams(("arbitrary", "arbitrary")),
        name="out_proj",
    )(x, pool_out, o_gdn, gt, sc, sh, g_post_mix.reshape(1, D_MODEL), g_pre_ffn.reshape(1, D_MODEL),
      w_out_b, router_w, router_b.reshape(1, N_EXPERTS))


def _rank_kernel(g_ref, r_ref, cnt_ref, carry, *, tm):
    i = pl.program_id(0)

    @pl.when(i == 0)
    def _():
        carry[...] = jnp.zeros_like(carry)

    mask = (g_ref[...] >= 0.0).astype(F32)
    ri = lax.broadcasted_iota(jnp.int32, (tm, tm), 0)
    ci = lax.broadcasted_iota(jnp.int32, (tm, tm), 1)
    before = (ri > ci).astype(F32)
    c = carry[...]
    r_ref[...] = _dot(before, mask) + c
    c = c + jnp.sum(mask, axis=0, keepdims=True)
    carry[...] = c
    cnt_ref[...] = c


def _rank(gsel, tm):
    n = gsel.shape[0]
    kern = functools.partial(_rank_kernel, tm=tm)
    return pl.pallas_call(
        kern,
        grid=(n // tm,),
        in_specs=[pl.BlockSpec((tm, N_EXPERTS), lambda i: (i, 0))],
        out_specs=[pl.BlockSpec((tm, N_EXPERTS), lambda i: (i, 0)),
                   pl.BlockSpec((1, N_EXPERTS), lambda i: (0, 0))],
        out_shape=[jax.ShapeDtypeStruct((n, N_EXPERTS), F32),
                   jax.ShapeDtypeStruct((1, N_EXPERTS), F32)],
        scratch_shapes=[pltpu.VMEM((1, N_EXPERTS), F32)],
        compiler_params=_cparams(("arbitrary",)),
        name="rank",
    )(gsel)


def _dest_kernel(g_ref, r_ref, start_ref, d_ref, gt_ref, *, tm):
    g = g_ref[...]
    mask = g >= 0.0
    maskf = mask.astype(F32)
    dest = start_ref[...] + r_ref[...]
    ri = lax.broadcasted_iota(jnp.int32, (N_EXPERTS, N_EXPERTS), 0)
    ci = lax.broadcasted_iota(jnp.int32, (N_EXPERTS, N_EXPERTS), 1)
    slot = _dot(maskf, (ri < ci).astype(F32))
    lane4 = lax.broadcasted_iota(jnp.int32, (tm, TOP_K), 1)
    d_out = jnp.zeros((tm, TOP_K), F32)
    g_out = jnp.zeros((tm, TOP_K), F32)
    for j in range(TOP_K):
        pick = mask & (slot == float(j))
        dj = jnp.sum(jnp.where(pick, dest, 0.0), axis=-1, keepdims=True)
        gj = jnp.sum(jnp.where(pick, g, 0.0), axis=-1, keepdims=True)
        d_out = jnp.where(lane4 == j, dj, d_out)
        g_out = jnp.where(lane4 == j, gj, g_out)
    d_ref[...] = d_out.astype(jnp.int32)
    gt_ref[...] = g_out


def _dest(gsel, rank, start_row, tm):
    n = gsel.shape[0]
    kern = functools.partial(_dest_kernel, tm=tm)
    return pl.pallas_call(
        kern,
        grid=(n // tm,),
        in_specs=[pl.BlockSpec((tm, N_EXPERTS), lambda i: (i, 0)),
                  pl.BlockSpec((tm, N_EXPERTS), lambda i: (i, 0)),
                  pl.BlockSpec((1, N_EXPERTS), lambda i: (0, 0))],
        out_specs=[pl.BlockSpec((tm, TOP_K), lambda i: (i, 0)),
                   pl.BlockSpec((tm, TOP_K), lambda i: (i, 0))],
        out_shape=[jax.ShapeDtypeStruct((n, TOP_K), jnp.int32),
                   jax.ShapeDtypeStruct((n, TOP_K), F32)],
        compiler_params=_cparams(("arbitrary",)),
        name="dest",
    )(gsel, rank, start_row)


def _inv_kernel(cnt_ref, start_ref, end_ref, nu_ref, dest_ref, src_ref, dst_ref, *, tm, bm, n_blocks, n4):
    i = pl.program_id(0)
    log_bm = bm.bit_length() - 1

    @pl.when(i == 0)
    def _():
        def fill(r, carry):
            src_ref[r] = 0
            dst_ref[bm + r] = (n4 + ((r >> log_bm) & 1) * bm + (r & (bm - 1))) * ROW_TILES
            return carry

        def per_expert(e, carry):
            lax.fori_loop(start_ref[e] + cnt_ref[e], end_ref[e], fill, 0)
            return carry

        lax.fori_loop(0, N_EXPERTS, per_expert, 0)
        lax.fori_loop(nu_ref[0] * bm, n_blocks * bm, fill, 0)

        def placeholder(r, carry):
            dst_ref[r] = (n4 + bm + r) * ROW_TILES
            return carry

        lax.fori_loop(0, bm, placeholder, 0)

    base = i * (tm * TOP_K)
    unroll = 8

    def body(g, carry):
        for u in range(unroll):
            j = g * unroll + u
            d = dest_ref[j]
            src_ref[d] = ((base + j) >> 2) * ROW_TILES
            dst_ref[bm + d] = (base + j) * ROW_TILES
        return carry

    lax.fori_loop(0, tm * TOP_K // unroll, body, 0)


def _inverse_map(dest_flat, cnt, start, end, n_used, n_blocks, tm, bm):
    n4 = dest_flat.shape[0]
    assert bm & (bm - 1) == 0 and TOP_K == 4
    kern = functools.partial(_inv_kernel, tm=tm, bm=bm, n_blocks=n_blocks, n4=n4)
    smem = pl.BlockSpec(memory_space=pltpu.SMEM)
    return pl.pallas_call(
        kern,
        grid_spec=pltpu.PrefetchScalarGridSpec(
            num_scalar_prefetch=4,
            grid=(n4 // (tm * TOP_K),),
            in_specs=[pl.BlockSpec((tm * TOP_K,), lambda i, *_: (i,), memory_space=pltpu.SMEM)],
            out_specs=[smem, smem]),
        out_shape=[jax.ShapeDtypeStruct((n_blocks * bm,), jnp.int32),
                   jax.ShapeDtypeStruct(((n_blocks + 1) * bm,), jnp.int32)],
        compiler_params=_cparams(("arbitrary",)),
        name="inverse_map",
    )(cnt, start, end, n_used, dest_flat)


def _expert_kernel(be_ref, nu_ref, src_ref, dst_ref, ht_ref, wg_ref, bg_ref, wd_ref, bd_ref, sl_ref,
                   wg_s, wd_s, xbuf, ybuf, gsem, ssem, *, bm, n4, nsplit):
    j = pl.program_id(0)
    nu = nu_ref[0]
    prev = be_ref[jnp.maximum(j - 1, 0)]
    fresh = (j == 0) | (be_ref[j] != prev)
    active = j < nu
    cur = j % 2
    nxt = 1 - cur
    block_rows = bm * ROW_TILES

    def gather_copy(blk, r, slot):
        src = pl.multiple_of(src_ref[blk * bm + r], ROW_TILES)
        return pltpu.make_async_copy(ht_ref.at[pl.ds(src, ROW_TILES), :],
                                     xbuf.at[slot, pl.ds(r * ROW_TILES, ROW_TILES), :], gsem)

    def scatter_copy(blk, r, slot):
        dst = pl.multiple_of(dst_ref[(blk + 1) * bm + r], ROW_TILES)
        return pltpu.make_async_copy(ybuf.at[slot, pl.ds(r * ROW_TILES, ROW_TILES), :],
                                     sl_ref.at[pl.ds(dst, ROW_TILES), :], ssem.at[cur])

    def spare_copy(r):
        return pltpu.make_async_copy(ybuf.at[0, pl.ds(r * ROW_TILES, ROW_TILES), :],
                                     sl_ref.at[pl.ds((n4 + r) * ROW_TILES, ROW_TILES), :], ssem.at[1])

    def wait_gather(slot):
        pltpu.make_async_copy(ht_ref.at[pl.ds(0, block_rows), :], xbuf.at[slot], gsem).wait()

    def wait_scatter_piece(rows, sem_slot):
        pltpu.make_async_copy(ybuf.at[0, pl.ds(0, rows * ROW_TILES), :],
                              sl_ref.at[pl.ds(0, rows * ROW_TILES), :], ssem.at[sem_slot]).wait()

    @pl.when(j == 0)
    def _():
        ybuf[...] = jnp.zeros_like(ybuf)

        def first(r, carry):
            gather_copy(0, r, 0).start()
            spare_copy(r).start()
            return carry

        lax.fori_loop(0, bm, first, 0)

    @pl.when(active & fresh)
    def _():
        wg_s[...] = wg_ref[0].astype(BF16)
        wd_s[...] = wd_ref[0].astype(BF16)

    @pl.when(active)
    def _():
        wait_gather(cur)
        x = jnp.concatenate([xbuf[cur, pl.ds(s, bm, stride=ROW_TILES), :] for s in range(ROW_TILES)], axis=-1)
        xb = x.astype(BF16)
        per = bm // nsplit
        width = D_FF // nsplit

        def issue(piece):
            wait_scatter_piece(per, nxt)
            for r in range(piece * per, (piece + 1) * per):
                gather_copy(j + 1, r, nxt).start(priority=0)
                scatter_copy(j - 1, r, nxt).start(priority=1)

        acts = []
        for f in range(nsplit):
            lo = f * width
            gate = jnp.dot(xb, wg_s[:, lo:lo + width], preferred_element_type=F32) + bg_ref[0, :, lo:lo + width]
            up = (jnp.dot(xb, wg_s[:, D_FF + lo:D_FF + lo + width], preferred_element_type=F32)
                  + bg_ref[0, :, D_FF + lo:D_FF + lo + width])
            issue(f)
            gate = jnp.minimum(gate, SWIGLU_LIMIT)
            up = jnp.clip(up, -SWIGLU_LIMIT, SWIGLU_LIMIT)
            acts.append(((up + 1.0) * (gate * jax.nn.sigmoid(SWIGLU_ALPHA * gate))).astype(BF16))
        act = jnp.concatenate(acts, axis=-1)
        tiles_per = D_MODEL // nsplit // LANES
        for f in range(nsplit):
            lo = f * (D_MODEL // nsplit)
            y = (jnp.dot(act, wd_s[:, lo:lo + D_MODEL // nsplit], preferred_element_type=F32)
                 + bd_ref[0, :, lo:lo + D_MODEL // nsplit])
            for s in range(tiles_per):
                ybuf[cur, pl.ds(f * tiles_per + s, bm, stride=ROW_TILES), :] = y[:, s * LANES:(s + 1) * LANES]

    @pl.when(j == nu)
    def _():
        wait_gather(cur)
        wait_scatter_piece(bm, nxt)

        def last(r, carry):
            scatter_copy(j - 1, r, nxt).start()
            return carry

        lax.fori_loop(0, bm, last, 0)
        wait_scatter_piece(bm, cur)


def _experts(ht, src_rows, dst_rows, block_expert, n_used, w_gu, b_gu, w_down, b_down, bm):
    n4 = ht.shape[0] // ROW_TILES * TOP_K
    n_blocks = src_rows.shape[0] // bm

    def wsel(j, be, nu, src, dst):
        return (be[j], 0, 0)

    kern = functools.partial(_expert_kernel, bm=bm, n4=n4, nsplit=4)
    return pl.pallas_call(
        kern,
        grid_spec=pltpu.PrefetchScalarGridSpec(
            num_scalar_prefetch=4,
            grid=(n_blocks,),
            in_specs=[pl.BlockSpec(memory_space=pl.ANY),
                      pl.BlockSpec((1, D_MODEL, 2 * D_FF), wsel),
                      pl.BlockSpec((1, 1, 2 * D_FF), wsel),
                      pl.BlockSpec((1, D_FF, D_MODEL), wsel),
                      pl.BlockSpec((1, 1, D_MODEL), wsel)],
            out_specs=pl.BlockSpec(memory_space=pl.ANY),
            scratch_shapes=[pltpu.VMEM((D_MODEL, 2 * D_FF), BF16),
                            pltpu.VMEM((D_FF, D_MODEL), BF16),
                            pltpu.VMEM((2, bm * ROW_TILES, LANES), F32),
                            pltpu.VMEM((2, bm * ROW_TILES, LANES), F32),
                            pltpu.SemaphoreType.DMA(()),
                            pltpu.SemaphoreType.DMA((2,))]),
        out_shape=jax.ShapeDtypeStruct(((n4 + 2 * bm) * ROW_TILES, LANES), F32),
        compiler_params=_cparams(("arbitrary",)),
        name="experts",
    )(block_expert, n_used, src_rows, dst_rows, ht, w_gu, b_gu.reshape(N_EXPERTS, 1, 2 * D_FF), w_down,
      b_down.reshape(N_EXPERTS, 1, D_MODEL))


def _final_kernel(sl_ref, gate_ref, x1_ref, gt_ref, g_ref, o_ref, *, tm):
    gates = gate_ref[...]
    per_tok = TOP_K * ROW_TILES
    cols = []
    for s in range(ROW_TILES):
        acc = None
        for j in range(TOP_K):
            part = sl_ref[pl.ds(j * ROW_TILES + s, tm, stride=per_tok), :] * gates[:, j:j + 1]
            acc = part if acc is None else acc + part
        cols.append(acc)
    f = jnp.concatenate(cols, axis=-1)
    o_ref[0] = x1_ref[0] + gt_ref[0] * _rms(f, g_ref[...])


def _final(slots, gates, x1, gt, g_post_ffn, tok_off, tm, per_token):
    bsz, t, _ = x1.shape
    tiles = t // tm
    off = tok_off // tm
    kern = functools.partial(_final_kernel, tm=tm)
    return pl.pallas_call(
        kern,
        grid=(bsz, tiles),
        in_specs=[pl.BlockSpec((tm * TOP_K * ROW_TILES, LANES), lambda b, i: (off + b * tiles + i, 0)),
                  pl.BlockSpec((tm, TOP_K), lambda b, i: (off + b * tiles + i, 0)),
                  pl.BlockSpec((1, tm, D_MODEL), lambda b, i: (b, i, 0)),
                  _mod_spec(per_token, tm),
                  pl.BlockSpec((1, D_MODEL), lambda b, i: (0, 0))],
        out_specs=pl.BlockSpec((1, tm, D_MODEL), lambda b, i: (b, i, 0)),
        out_shape=jax.ShapeDtypeStruct((bsz, t, D_MODEL), F32),
        compiler_params=_cparams(("arbitrary", "arbitrary")),
        name="final",
    )(slots, gates, x1, gt, g_post_ffn.reshape(1, D_MODEL))


def _moe(ht, gsel, w_gu, b_gu, w_down, b_down, tm, bm):
    n = gsel.shape[0]
    rank, counts = _rank(gsel, tm)
    cnt = counts[0].astype(jnp.int32)
    nblk = (cnt + bm - 1) // bm
    blk_end = jnp.cumsum(nblk)
    blk_start = blk_end - nblk
    n_blocks = -(-(n * TOP_K) // bm) + N_EXPERTS
    blocks = jnp.arange(n_blocks, dtype=jnp.int32)
    block_expert = jnp.minimum(jnp.sum((blk_end[None, :] <= blocks[:, None]).astype(jnp.int32), axis=1),
                               N_EXPERTS - 1).astype(jnp.int32)
    n_used = blk_end[-1:].astype(jnp.int32)
    row_start = (blk_start * bm).astype(jnp.int32)
    row_end = (blk_end * bm).astype(jnp.int32)
    dest, gates = _dest(gsel, rank, row_start.astype(F32).reshape(1, N_EXPERTS), tm)
    dest_flat = dest.reshape(n * TOP_K)
    src_rows, dst_rows = _inverse_map(dest_flat, cnt, row_start, row_end, n_used, n_blocks, tm, bm)
    slots = _experts(ht, src_rows, dst_rows, block_expert, n_used, w_gu, b_gu, w_down, b_down, bm)
    return slots, gates


def _mix_path(x, mod, pool_buf, conv_buf, s0, start_pos, lp, tm, nb, per_token):
    (g_pre_mix, g_post_mix, g_pre_ffn, w_in_p, w_bd, pool_scale, conv_w, alog_row, dtb_row, norm_w,
     w_out_b, router_w, router_b) = lp
    bsz, t, _ = x.shape
    sh_mix, sc_mix, gt_mix, sh_ffn, sc_ffn, gt_ffn = (mod[:, i] for i in range(6))

    def shape_mod(m):
        if per_token:
            return jnp.repeat(m, t, axis=0).reshape(1, bsz * t, D_MODEL)
        return m.reshape(bsz, 1, D_MODEL)

    sh_mix, sc_mix, gt_mix, sh_ffn, sc_ffn, gt_ffn = map(shape_mod, (sh_mix, sc_mix, gt_mix, sh_ffn, sc_ffn, gt_ffn))
    xf = x.reshape(1, bsz * t, D_MODEL) if per_token else x
    proj = _in_proj(xf, sc_mix, sh_mix, g_pre_mix, w_in_p, tm, per_token).reshape(bsz, t, PROJ_WIDTH)
    tseq = min(tm, t)
    o_gdn, s_new = _gdn(proj, conv_buf, s0, conv_w, alog_row, dtb_row, norm_w, nb, tseq)
    pool_out = _pool(proj, pool_buf, w_bd, pool_scale, nb, tseq, start_pos)
    fl = (lambda a: a.reshape(1, bsz * t, a.shape[-1])) if per_token else (lambda a: a)
    x1, ht, gsel = _out_proj(xf, fl(pool_out), fl(o_gdn), gt_mix, sc_ffn, sh_ffn, g_post_mix, g_pre_ffn,
                             w_out_b, router_w, router_b, tm, per_token)
    def last_rows(buf, off, width):
        keep = buf.shape[1]
        if t >= keep:
            return proj[:, t - keep:, off:off + width]
        return jnp.concatenate([buf[:, t:], proj[:, :, off:off + width]], axis=1)

    pool_new = last_rows(pool_buf, OFF_U, POOL_WIDTH)
    conv_new = last_rows(conv_buf, OFF_QKV, CONV_CH)
    return x1, ht, gsel, gt_ffn, pool_new, conv_new, s_new


def kernel(x_prompt, x_sample, c_prompt, c_sample, state_pool, state_conv, state_ssm, w_ada, b_ada, g_pre_mix, g_post_mix, g_pre_ffn, g_post_ffn, w_in, w_pool, pool_scale, conv_w, a_log, dt_bias, gdn_norm_w, w_out, router_w, router_b, w_gu, b_gu, w_down, b_down):
    depth = w_ada.shape[0]
    bp, tp, _ = x_prompt.shape
    bs, ts, _ = x_sample.shape
    tm = 256
    y_p, y_s = x_prompt, x_sample
    outs = [[] for _ in range(6)]
    for l in range(depth):
        mod = _ada(jnp.concatenate([c_prompt, c_sample], axis=0), w_ada[l], b_ada[l]).reshape(bp + bs, 6, D_MODEL)
        o1, o2, o3 = POOL_WIDTH, POOL_WIDTH + CONV_CH, POOL_WIDTH + CONV_CH + GDN_WIDTH
        w_l = w_in[l]
        w_in_p = jnp.concatenate(
            [w_l[:, o1:o2], w_l[:, o2:o3], w_l[:, :o1], w_l[:, o3:],
             jnp.zeros((D_MODEL, LANES - (w_l.shape[1] - o3)), F32)], axis=1).astype(BF16)
        w_bd = jax.scipy.linalg.block_diag(*[w_pool[l, g] for g in range(len(POOL_WINDOWS))])
        pad_row = lambda v: jnp.pad(v, (0, LANES - v.shape[0])).reshape(1, LANES)
        lp = (g_pre_mix[l], g_post_mix[l], g_pre_ffn[l], w_in_p, w_bd, pool_scale[l], conv_w[l],
              pad_row(a_log[l]), pad_row(dt_bias[l]), gdn_norm_w[l].reshape(1, HEAD_DIM),
              w_out[l].astype(BF16), router_w[l], router_b[l])
        zero_pool = jnp.zeros((bp, POOL_BUF, POOL_WIDTH), F32)
        zero_conv = jnp.zeros((bp, CONV_WIDTH - 1, CONV_CH), F32)
        zero_ssm = jnp.zeros((bp, GDN_HEADS, HEAD_DIM, HEAD_DIM), F32)
        x1p, htp, gp, gtp, pp, cp, sp = _mix_path(y_p, mod[:bp], zero_pool, zero_conv, zero_ssm, 0, lp, tm, 1, False)
        x1s, hts, gs, gts, ps, cs, ss = _mix_path(y_s, mod[bp:], state_pool[l], state_conv[l], state_ssm[l],
                                                  PAST_LEN, lp, tm, 8, True)
        ht = jnp.concatenate([htp, hts], axis=0)
        gsel = jnp.concatenate([gp, gs], axis=0)
        slots, gates = _moe(ht, gsel, w_gu[l], b_gu[l], w_down[l], b_down[l], tm, MOE_BLOCK)
        y_p = _final(slots, gates, x1p, gtp, g_post_ffn[l], 0, tm, False)
        y_s = _final(slots, gates, x1s, gts, g_post_ffn[l], bp * tp, tm, True).reshape(bs, ts, D_MODEL)
        for lst, val in zip(outs, (pp, cp, sp, ps, cs, ss)):
            lst.append(val)
    return (y_p, y_s) + tuple(jnp.stack(o) for o in outs)
```

```python
import functools
import math

import jax
import jax.numpy as jnp
from jax import lax
from jax.experimental import pallas as pl
from jax.experimental.pallas import tpu as pltpu

F32 = jnp.float32
BF16 = jnp.bfloat16

D_MODEL = 1024
PAST_LEN = 16384
POOL_WINDOWS = (2, 4, 8, 16)
POOL_WIDTH = D_MODEL // 4
POOL_GROUP = POOL_WIDTH // len(POOL_WINDOWS)
POOL_BUF = max(POOL_WINDOWS) - 1
GDN_WIDTH = D_MODEL - POOL_WIDTH
HEAD_DIM = 128
GDN_HEADS = GDN_WIDTH // HEAD_DIM
CONV_WIDTH = 4
CONV_CH = 3 * GDN_WIDTH
GDN_CHUNK = 64
N_EXPERTS = 32
TOP_K = 4
D_FF = D_MODEL
SWIGLU_LIMIT = 7.0
SWIGLU_ALPHA = 1.702
EPS = 1e-6

SUBLANES = 8
LANES = 128
ROW_TILES = D_MODEL // LANES
OFF_QKV = 0
OFF_Z = OFF_QKV + CONV_CH
OFF_U = OFF_Z + GDN_WIDTH
OFF_AB = OFF_U + POOL_WIDTH
PROJ_WIDTH = OFF_AB + LANES
assert OFF_Z % GDN_WIDTH == 0 and OFF_U % POOL_WIDTH == 0 and OFF_AB % LANES == 0
MOE_BLOCK = 512
VMEM_LIMIT = 56 * 1024 * 1024


def _cparams(sem):
    return pltpu.CompilerParams(dimension_semantics=sem, vmem_limit_bytes=VMEM_LIMIT)


def _dot(a, b):
    return jnp.dot(a.astype(BF16), b.astype(BF16), preferred_element_type=F32)


def _dot_nt(a, b):
    return lax.dot_general(a.astype(BF16), b.astype(BF16), (((1,), (1,)), ((), ())),
                           preferred_element_type=F32)


def _silu(x):
    return x * jax.nn.sigmoid(x)


def _rms(x, g):
    return x * lax.rsqrt(jnp.mean(x * x, axis=-1, keepdims=True) + EPS) * g


def _ada_kernel(c_ref, w_ref, b_ref, o_ref):
    o_ref[...] = _dot(_silu(c_ref[...]), w_ref[...]) + b_ref[...]


def _ada(c, w_ada, b_ada):
    n = c.shape[0]
    cols = w_ada.shape[1]
    blk = D_MODEL
    return pl.pallas_call(
        _ada_kernel,
        grid=(cols // blk,),
        in_specs=[pl.BlockSpec((n, D_MODEL), lambda j: (0, 0)),
                  pl.BlockSpec((D_MODEL, blk), lambda j: (0, j)),
                  pl.BlockSpec((1, blk), lambda j: (0, j))],
        out_specs=pl.BlockSpec((n, blk), lambda j: (0, j)),
        out_shape=jax.ShapeDtypeStruct((n, cols), F32),
        compiler_params=_cparams(("arbitrary",)),
        name="ada",
    )(c, w_ada, b_ada.reshape(1, cols))


def _in_kernel(x_ref, sc_ref, sh_ref, g_ref, w_ref, o_ref):
    h = _rms(x_ref[0], g_ref[...]) * (1.0 + sc_ref[0]) + sh_ref[0]
    o_ref[0] = jnp.dot(h.astype(BF16), w_ref[...], preferred_element_type=F32)


def _mod_spec(per_token, tm):
    if per_token:
        return pl.BlockSpec((1, tm, D_MODEL), lambda b, t: (0, t, 0))
    return pl.BlockSpec((1, 1, D_MODEL), lambda b, t: (b, 0, 0))


def _in_proj(x, sc, sh, g, w_in_p, tm, per_token):
    bsz, t, _ = x.shape
    return pl.pallas_call(
        _in_kernel,
        grid=(bsz, t // tm),
        in_specs=[pl.BlockSpec((1, tm, D_MODEL), lambda b, i: (b, i, 0)),
                  _mod_spec(per_token, tm), _mod_spec(per_token, tm),
                  pl.BlockSpec((1, D_MODEL), lambda b, i: (0, 0)),
                  pl.BlockSpec((D_MODEL, PROJ_WIDTH), lambda b, i: (0, 0))],
        out_specs=pl.BlockSpec((1, tm, PROJ_WIDTH), lambda b, i: (b, i, 0)),
        out_shape=jax.ShapeDtypeStruct((bsz, t, PROJ_WIDTH), F32),
        compiler_params=_cparams(("arbitrary", "arbitrary")),
        name="in_proj",
    )(x, sc, sh, g.reshape(1, D_MODEL), w_in_p)


def _shift_rows(prev, x, d):
    if d == 0:
        return x
    e = jnp.concatenate([prev, x], axis=0)
    return pltpu.roll(e, d, axis=0)[SUBLANES:]


def _cumsum_rows(x):
    n = x.shape[0]
    row = lax.broadcasted_iota(jnp.int32, x.shape, 0)
    s = 1
    while s < n:
        x = x + jnp.where(row >= s, pltpu.roll(x, s, axis=0), 0.0)
        s *= 2
    return x


def _gdn_kernel(qkv_ref, z_ref, ab_ref, cs_ref, cw_ref, s0_ref, alog_ref, dtb_ref, nw_ref, o_ref, sn_ref,
                s_scr, prev_scr, *, nb, tm, tmp, chunk):
    ti = pl.program_id(1)
    nt = pl.num_programs(1)
    n_chunks = tmp // chunk
    n_sq = int(math.log2(chunk)) - 1

    @pl.when(ti == 0)
    def _():
        s_scr[...] = s0_ref[...]
        for b in range(nb):
            prev_scr[b] = jnp.concatenate(
                [jnp.zeros((SUBLANES - (CONV_WIDTH - 1), CONV_CH), F32), cs_ref[b]], axis=0)

    rowi = lax.broadcasted_iota(jnp.int32, (tmp, LANES), 0)
    ri = lax.broadcasted_iota(jnp.int32, (chunk, chunk), 0)
    ci = lax.broadcasted_iota(jnp.int32, (chunk, chunk), 1)
    tril = ri >= ci
    strict = ri > ci
    eye = (ri == ci).astype(F32)
    neg_a = -jnp.exp(alog_ref[...])
    dtb = dtb_ref[...]
    nw = nw_ref[...]

    def padrows(x):
        if tm == tmp:
            return x
        return jnp.concatenate([x, jnp.zeros((tmp - tm, x.shape[1]), F32)], axis=0)

    cw = cw_ref[...]
    valid = rowi < tm
    items = []
    for b in range(nb):
        x = padrows(qkv_ref[b])
        prev = prev_scr[b]
        acc = x * cw[CONV_WIDTH - 1:CONV_WIDTH]
        for d in range(1, CONV_WIDTH):
            acc = acc + _shift_rows(prev, x, d) * cw[CONV_WIDTH - 1 - d:CONV_WIDTH - d]
        if tm >= SUBLANES:
            prev_scr[b] = x[tm - SUBLANES:tm]
        y = _silu(acc)
        ab = padrows(ab_ref[b])
        g_all = jnp.where(valid, neg_a * jax.nn.softplus(ab + dtb), 0.0)
        beta_all = jnp.where(valid, jax.nn.sigmoid(ab), 0.0)
        for h in range(GDN_HEADS):
            qc = y[:, h * HEAD_DIM:(h + 1) * HEAD_DIM]
            kc = y[:, GDN_WIDTH + h * HEAD_DIM:GDN_WIDTH + (h + 1) * HEAD_DIM]
            v = y[:, 2 * GDN_WIDTH + h * HEAD_DIM:2 * GDN_WIDTH + (h + 1) * HEAD_DIM]
            q = qc * lax.rsqrt(jnp.sum(qc * qc, axis=-1, keepdims=True) + EPS) * (HEAD_DIM ** -0.5)
            k = kc * lax.rsqrt(jnp.sum(kc * kc, axis=-1, keepdims=True) + EPS)
            gb = jnp.broadcast_to(g_all[:, h:h + 1], (tmp, LANES))
            bb = jnp.broadcast_to(beta_all[:, GDN_HEADS + h:GDN_HEADS + h + 1], (tmp, LANES))
            for c in range(n_chunks):
                sl = slice(c * chunk, (c + 1) * chunk)
                items.append(dict(b=b, h=h, c=c, q=q[sl], k=k[sl], v=v[sl], bb=bb[sl], gb=gb[sl]))

    for it in items:
        gcb = _cumsum_rows(it["gb"])
        gct = jnp.transpose(gcb)[:chunk]
        diff = gcb[:, :chunk] - gct
        it["decay"] = jnp.where(tril, jnp.exp(jnp.where(tril, diff, 0.0)), 0.0)
        it["kb"] = it["k"] * it["bb"]
        it["eg"] = jnp.exp(gcb)
        gl = gcb[chunk - 1:chunk]
        it["egl"] = jnp.exp(gl)
        it["kdt"] = jnp.transpose(it["k"] * jnp.exp(gl - gcb))
    for it in items:
        kq = _dot_nt(jnp.concatenate([it["kb"], it["q"]], axis=0), it["k"])
        it["a"] = jnp.where(strict, kq[:chunk] * it["decay"], 0.0)
        it["qk"] = jnp.where(tril, kq[chunk:] * it["decay"], 0.0)
    for it in items:
        it["inv"] = eye - it["a"]
        it["p"] = _dot(it["a"], it["a"])
    for lvl in range(n_sq):
        for it in items:
            if lvl < n_sq - 1:
                r = _dot(jnp.concatenate([it["inv"], it["p"]], axis=0), it["p"])
                it["inv"] = it["inv"] + r[:chunk]
                it["p"] = r[chunk:]
            else:
                it["inv"] = it["inv"] + _dot(it["inv"], it["p"])
    for it in items:
        it["wu"] = _dot(it["inv"], jnp.concatenate([it["kb"] * it["eg"], it["v"] * it["bb"]], axis=1))
    for it in items:
        mn = _dot(it["kdt"], it["wu"])
        qo = _dot(it["qk"], it["wu"])
        it["mp"], it["nc"] = mn[:, :HEAD_DIM], mn[:, HEAD_DIM:]
        it["qp"] = it["q"] * it["eg"] - qo[:, :HEAD_DIM]
        it["oc"] = qo[:, HEAD_DIM:]

    state = {(b, h): s_scr[b, h] for b in range(nb) for h in range(GDN_HEADS)}
    for c in range(n_chunks):
        for it in items:
            if it["c"] != c:
                continue
            s = state[(it["b"], it["h"])]
            r = _dot(jnp.concatenate([it["qp"], it["mp"]], axis=0), s)
            it["o"] = r[:chunk] + it["oc"]
            state[(it["b"], it["h"])] = s * it["egl"] - r[chunk:] + it["nc"]
    for (b, h), s in state.items():
        s_scr[b, h] = s

    for b in range(nb):
        zf = padrows(z_ref[b])
        for h in range(GDN_HEADS):
            outs = [it["o"] for it in items if it["b"] == b and it["h"] == h]
            o = outs[0] if n_chunks == 1 else jnp.concatenate(outs, axis=0)
            o = (o * lax.rsqrt(jnp.mean(o * o, axis=-1, keepdims=True) + EPS) * nw
                 * _silu(zf[:, h * HEAD_DIM:(h + 1) * HEAD_DIM]))
            o_ref[b, :, h * HEAD_DIM:(h + 1) * HEAD_DIM] = o[:tm]

    @pl.when(ti == nt - 1)
    def _():
        sn_ref[...] = s_scr[...]


def _gdn(proj, state_conv, s0, conv_w, alog_row, dtb_row, norm_w, nb, tm):
    bsz, t, _ = proj.shape
    tmp = max(tm, SUBLANES)
    chunk = min(GDN_CHUNK, tmp)
    row = pl.BlockSpec((1, LANES), lambda b, i: (0, 0))
    sspec = pl.BlockSpec((nb, GDN_HEADS, HEAD_DIM, HEAD_DIM), lambda b, i: (b, 0, 0, 0))
    kern = functools.partial(_gdn_kernel, nb=nb, tm=tm, tmp=tmp, chunk=chunk)
    return pl.pallas_call(
        kern,
        grid=(bsz // nb, t // tm),
        in_specs=[pl.BlockSpec((nb, tm, CONV_CH), lambda b, i: (b, i, OFF_QKV // CONV_CH)),
                  pl.BlockSpec((nb, tm, GDN_WIDTH), lambda b, i: (b, i, OFF_Z // GDN_WIDTH)),
                  pl.BlockSpec((nb, tm, LANES), lambda b, i: (b, i, OFF_AB // LANES)),
                  pl.BlockSpec((nb, CONV_WIDTH - 1, CONV_CH), lambda b, i: (b, 0, 0)),
                  pl.BlockSpec((CONV_WIDTH, CONV_CH), lambda b, i: (0, 0)),
                  sspec, row, row, row],
        out_specs=[pl.BlockSpec((nb, tm, GDN_WIDTH), lambda b, i: (b, i, 0)), sspec],
        out_shape=[jax.ShapeDtypeStruct((bsz, t, GDN_WIDTH), F32),
                   jax.ShapeDtypeStruct(s0.shape, F32)],
        scratch_shapes=[pltpu.VMEM((nb, GDN_HEADS, HEAD_DIM, HEAD_DIM), F32),
                        pltpu.VMEM((nb, SUBLANES, CONV_CH), F32)],
        compiler_params=_cparams(("arbitrary", "arbitrary")),
        name="gdn",
    )(proj, proj, proj, state_conv, conv_w, s0, alog_row, dtb_row, norm_w)


def _pool_kernel(u_ref, st_ref, w_ref, sc_ref, o_ref, prev_scr, *, nb, tm, tmp, start_pos):
    ti = pl.program_id(1)
    halo = 2 * SUBLANES

    @pl.when(ti == 0)
    def _():
        for b in range(nb):
            prev_scr[b] = jnp.concatenate([jnp.zeros((halo - POOL_BUF, POOL_WIDTH), F32), st_ref[b]], axis=0)

    lane = lax.broadcasted_iota(jnp.int32, (tmp, POOL_WIDTH), 1)
    rowi = lax.broadcasted_iota(jnp.int32, (tmp, POOL_WIDTH), 0)
    pos1 = start_pos + ti * tm + rowi + 1
    grp = lane // POOL_GROUP
    win = jnp.where(grp == 0, POOL_WINDOWS[0],
                    jnp.where(grp == 1, POOL_WINDOWS[1], jnp.where(grp == 2, POOL_WINDOWS[2], POOL_WINDOWS[3])))
    cnt = jnp.minimum(pos1, win).astype(F32)
    for b in range(nb):
        x = u_ref[b]
        if tm != tmp:
            x = jnp.concatenate([x, jnp.zeros((tmp - tm, POOL_WIDTH), F32)], axis=0)
        e = jnp.concatenate([prev_scr[b], x], axis=0)
        sums = []
        s = e
        for w in POOL_WINDOWS:
            s = s + pltpu.roll(s, w // 2, axis=0)
            sums.append(s[halo:])
        sel = jnp.where(grp == 0, sums[0], jnp.where(grp == 1, sums[1], jnp.where(grp == 2, sums[2], sums[3])))
        pooled = sel / cnt - x
        out = _dot(pooled, w_ref[...]) * sc_ref[...]
        o_ref[b] = out[:tm]
        if tm >= halo:
            prev_scr[b] = x[tm - halo:tm]


def _pool(proj, state_pool, w_bd, pool_scale, nb, tm, start_pos):
    bsz, t, _ = proj.shape
    tmp = max(tm, SUBLANES)
    kern = functools.partial(_pool_kernel, nb=nb, tm=tm, tmp=tmp, start_pos=start_pos)
    return pl.pallas_call(
        kern,
        grid=(bsz // nb, t // tm),
        in_specs=[pl.BlockSpec((nb, tm, POOL_WIDTH), lambda b, i: (b, i, OFF_U // POOL_WIDTH)),
                  pl.BlockSpec((nb, POOL_BUF, POOL_WIDTH), lambda b, i: (b, 0, 0)),
                  pl.BlockSpec((POOL_WIDTH, POOL_WIDTH), lambda b, i: (0, 0)),
                  pl.BlockSpec((1, POOL_WIDTH), lambda b, i: (0, 0))],
        out_specs=pl.BlockSpec((nb, tm, POOL_WIDTH), lambda b, i: (b, i, 0)),
        out_shape=jax.ShapeDtypeStruct((bsz, t, POOL_WIDTH), F32),
        scratch_shapes=[pltpu.VMEM((nb, 2 * SUBLANES, POOL_WIDTH), F32)],
        compiler_params=_cparams(("arbitrary", "arbitrary")),
        name="pool",
    )(proj, state_pool, w_bd, pool_scale.reshape(1, POOL_WIDTH))


def _out_kernel(x_ref, po_ref, og_ref, gt_ref, sc_ref, sh_ref, gpm_ref, gpf_ref, wo_ref, rw_ref, rb_ref,
                x1_ref, ht_ref, g_ref, *, tm):
    mix = _dot(po_ref[0], wo_ref[:POOL_WIDTH]) + _dot(og_ref[0], wo_ref[POOL_WIDTH:])
    x1 = x_ref[0] + gt_ref[0] * _rms(mix, gpm_ref[...])
    x1_ref[0] = x1
    h = _rms(x1, gpf_ref[...]) * (1.0 + sc_ref[0]) + sh_ref[0]
    for s in range(ROW_TILES):
        ht_ref[pl.ds(s, tm, stride=ROW_TILES), :] = h[:, s * LANES:(s + 1) * LANES]
    logits = _dot(h, rw_ref[...]) + rb_ref[...]
    lane = lax.broadcasted_iota(jnp.int32, logits.shape, 1)
    work = logits
    sel = jnp.zeros(logits.shape, jnp.bool_)
    top = None
    den = jnp.zeros((tm, 1), F32)
    for j in range(TOP_K):
        m = jnp.max(work, axis=-1, keepdims=True)
        idx = jnp.min(jnp.where(work == m, lane, N_EXPERTS), axis=-1, keepdims=True)
        hit = lane == idx
        sel = sel | hit
        work = jnp.where(hit, -jnp.inf, work)
        if j == 0:
            top = m
        den = den + jnp.exp(m - top)
    gates = jnp.exp(logits - top) / den
    g_ref[...] = jnp.where(sel, gates, -1.0)


def _out_proj(x, pool_out, o_gdn, gt, sc, sh, g_post_mix, g_pre_ffn, w_out_b, router_w, router_b, tm, per_token):
    bsz, t, _ = x.shape
    n = bsz * t
    tiles = t // tm

    def tok(width):
        return pl.BlockSpec((1, tm, width), lambda b, i: (b, i, 0))

    def const(shape):
        return pl.BlockSpec(shape, lambda b, i: tuple(0 for _ in shape))

    kern = functools.partial(_out_kernel, tm=tm)
    return pl.pallas_call(
        kern,
        grid=(bsz, tiles),
        in_specs=[tok(D_MODEL), tok(POOL_WIDTH), tok(GDN_WIDTH),
                  _mod_spec(per_token, tm), _mod_spec(per_token, tm), _mod_spec(per_token, tm),
                  const((1, D_MODEL)), const((1, D_MODEL)), const((D_MODEL, D_MODEL)),
                  const((D_MODEL, N_EXPERTS)), const((1, N_EXPERTS))],
        out_specs=[tok(D_MODEL),
                   pl.BlockSpec((tm * ROW_TILES, LANES), lambda b, i: (b * tiles + i, 0)),
                   pl.BlockSpec((tm, N_EXPERTS), lambda b, i: (b * tiles + i, 0))],
        out_shape=[jax.ShapeDtypeStruct((bsz, t, D_MODEL), F32),
                   jax.ShapeDtypeStruct((n * ROW_TILES, LANES), F32),
                   jax.ShapeDtypeStruct((n, N_EXPERTS), F32)],
        compiler_params=_cparams(("arbitrary", "arbitrary")),
        name="out_proj",
    )(x, pool_out, o_gdn, gt, sc, sh, g_post_mix.reshape(1, D_MODEL), g_pre_ffn.reshape(1, D_MODEL),
      w_out_b, router_w, router_b.reshape(1, N_EXPERTS))


def _rank_kernel(g_ref, r_ref, cnt_ref, carry, *, tm):
    i = pl.program_id(0)

    @pl.when(i == 0)
    def _():
        carry[...] = jnp.zeros_like(carry)

    mask = (g_ref[...] >= 0.0).astype(F32)
    ri = lax.broadcasted_iota(jnp.int32, (tm, tm), 0)
    ci = lax.broadcasted_iota(jnp.int32, (tm, tm), 1)
    before = (ri > ci).astype(F32)
    c = carry[...]
    r_ref[...] = _dot(before, mask) + c
    c = c + jnp.sum(mask, axis=0, keepdims=True)
    carry[...] = c
    cnt_ref[...] = c


def _rank(gsel, tm):
    n = gsel.shape[0]
    kern = functools.partial(_rank_kernel, tm=tm)
    return pl.pallas_call(
        kern,
        grid=(n // tm,),
        in_specs=[pl.BlockSpec((tm, N_EXPERTS), lambda i: (i, 0))],
        out_specs=[pl.BlockSpec((tm, N_EXPERTS), lambda i: (i, 0)),
                   pl.BlockSpec((1, N_EXPERTS), lambda i: (0, 0))],
        out_shape=[jax.ShapeDtypeStruct((n, N_EXPERTS), F32),
                   jax.ShapeDtypeStruct((1, N_EXPERTS), F32)],
        scratch_shapes=[pltpu.VMEM((1, N_EXPERTS), F32)],
        compiler_params=_cparams(("arbitrary",)),
        name="rank",
    )(gsel)


def _dest_kernel(g_ref, r_ref, start_ref, d_ref, gt_ref, *, tm):
    g = g_ref[...]
    mask = g >= 0.0
    maskf = mask.astype(F32)
    dest = start_ref[...] + r_ref[...]
    ri = lax.broadcasted_iota(jnp.int32, (N_EXPERTS, N_EXPERTS), 0)
    ci = lax.broadcasted_iota(jnp.int32, (N_EXPERTS, N_EXPERTS), 1)
    slot = _dot(maskf, (ri < ci).astype(F32))
    lane4 = lax.broadcasted_iota(jnp.int32, (tm, TOP_K), 1)
    d_out = jnp.zeros((tm, TOP_K), F32)
    g_out = jnp.zeros((tm, TOP_K), F32)
    for j in range(TOP_K):
        pick = mask & (slot == float(j))
        dj = jnp.sum(jnp.where(pick, dest, 0.0), axis=-1, keepdims=True)
        gj = jnp.sum(jnp.where(pick, g, 0.0), axis=-1, keepdims=True)
        d_out = jnp.where(lane4 == j, dj, d_out)
        g_out = jnp.where(lane4 == j, gj, g_out)
    d_ref[...] = d_out.astype(jnp.int32)
    gt_ref[...] = g_out


def _dest(gsel, rank, start_row, tm):
    n = gsel.shape[0]
    kern = functools.partial(_dest_kernel, tm=tm)
    return pl.pallas_call(
        kern,
        grid=(n // tm,),
        in_specs=[pl.BlockSpec((tm, N_EXPERTS), lambda i: (i, 0)),
                  pl.BlockSpec((tm, N_EXPERTS), lambda i: (i, 0)),
                  pl.BlockSpec((1, N_EXPERTS), lambda i: (0, 0))],
        out_specs=[pl.BlockSpec((tm, TOP_K), lambda i: (i, 0)),
                   pl.BlockSpec((tm, TOP_K), lambda i: (i, 0))],
        out_shape=[jax.ShapeDtypeStruct((n, TOP_K), jnp.int32),
                   jax.ShapeDtypeStruct((n, TOP_K), F32)],
        compiler_params=_cparams(("arbitrary",)),
        name="dest",
    )(gsel, rank, start_row)


def _inv_kernel(cnt_ref, start_ref, end_ref, nu_ref, dest_ref, inv_ref, *, tm, bm, n_blocks, n4):
    i = pl.program_id(0)
    log_bm = bm.bit_length() - 1

    @pl.when(i == 0)
    def _():
        def fill(r, carry):
            inv_ref[bm + r] = n4 + ((r >> log_bm) & 1) * bm + (r & (bm - 1))
            return carry

        def per_expert(e, carry):
            lax.fori_loop(start_ref[e] + cnt_ref[e], end_ref[e], fill, 0)
            return carry

        lax.fori_loop(0, N_EXPERTS, per_expert, 0)
        lax.fori_loop(nu_ref[0] * bm, n_blocks * bm, fill, 0)

        def placeholder(r, carry):
            inv_ref[r] = n4 + bm + r
            return carry

        lax.fori_loop(0, bm, placeholder, 0)

    base = i * (tm * TOP_K)
    unroll = 8

    def body(g, carry):
        ds = [dest_ref[g * unroll + u] for u in range(unroll)]
        for u in range(unroll):
            inv_ref[bm + ds[u]] = base + g * unroll + u
        return carry

    lax.fori_loop(0, tm * TOP_K // unroll, body, 0)


def _inverse_map(dest_flat, cnt, start, end, n_used, n_blocks, tm, bm):
    n4 = dest_flat.shape[0]
    assert bm & (bm - 1) == 0
    kern = functools.partial(_inv_kernel, tm=tm, bm=bm, n_blocks=n_blocks, n4=n4)
    return pl.pallas_call(
        kern,
        grid_spec=pltpu.PrefetchScalarGridSpec(
            num_scalar_prefetch=4,
            grid=(n4 // (tm * TOP_K),),
            in_specs=[pl.BlockSpec((tm * TOP_K,), lambda i, *_: (i,), memory_space=pltpu.SMEM)],
            out_specs=pl.BlockSpec(memory_space=pltpu.SMEM)),
        out_shape=jax.ShapeDtypeStruct(((n_blocks + 1) * bm,), jnp.int32),
        compiler_params=_cparams(("arbitrary",)),
        name="inverse_map",
    )(cnt, start, end, n_used, dest_flat)


def _expert_kernel(be_ref, nu_ref, inv_prev, inv_cur, inv_next, ht_ref, wg_ref, bg_ref, wd_ref, bd_ref, sl_ref,
                   wg_s, wd_s, xbuf, ybuf, gsem, ssem, *, bm, n4, nsplit):
    j = pl.program_id(0)
    nu = nu_ref[0]
    prev = be_ref[jnp.maximum(j - 1, 0)]
    fresh = (j == 0) | (be_ref[j] != prev)
    active = j < nu
    cur = j % 2
    nxt = 1 - cur
    block_rows = bm * ROW_TILES

    def gather_copy(inv_blk, r, slot):
        src = pl.multiple_of((inv_blk[0, 0, r] >> 2) * ROW_TILES, ROW_TILES)
        return pltpu.make_async_copy(ht_ref.at[pl.ds(src, ROW_TILES), :],
                                     xbuf.at[slot, pl.ds(r * ROW_TILES, ROW_TILES), :], gsem)

    def scatter_copy(r, slot):
        dst = pl.multiple_of(inv_prev[0, 0, r] * ROW_TILES, ROW_TILES)
        return pltpu.make_async_copy(ybuf.at[slot, pl.ds(r * ROW_TILES, ROW_TILES), :],
                                     sl_ref.at[pl.ds(dst, ROW_TILES), :], ssem.at[cur])

    def spare_copy(r):
        return pltpu.make_async_copy(ybuf.at[0, pl.ds(r * ROW_TILES, ROW_TILES), :],
                                     sl_ref.at[pl.ds((n4 + r) * ROW_TILES, ROW_TILES), :], ssem.at[1])

    def wait_gather(slot):
        pltpu.make_async_copy(ht_ref.at[pl.ds(0, block_rows), :], xbuf.at[slot], gsem).wait()

    def wait_scatter_piece(rows, sem_slot):
        pltpu.make_async_copy(ybuf.at[0, pl.ds(0, rows * ROW_TILES), :],
                              sl_ref.at[pl.ds(0, rows * ROW_TILES), :], ssem.at[sem_slot]).wait()

    @pl.when(j == 0)
    def _():
        ybuf[...] = jnp.zeros_like(ybuf)

        def first(r, carry):
            gather_copy(inv_cur, r, 0).start()
            spare_copy(r).start()
            return carry

        lax.fori_loop(0, bm, first, 0)

    @pl.when(active & fresh)
    def _():
        wg_s[...] = wg_ref[0].astype(BF16)
        wd_s[...] = wd_ref[0].astype(BF16)

    @pl.when(active)
    def _():
        wait_gather(cur)
        x = jnp.concatenate([xbuf[cur, pl.ds(s, bm, stride=ROW_TILES), :] for s in range(ROW_TILES)], axis=-1)
        xb = x.astype(BF16)
        per = bm // nsplit
        width = D_FF // nsplit

        def issue(piece):
            wait_scatter_piece(per, nxt)
            for r in range(piece * per, (piece + 1) * per):
                gather_copy(inv_next, r, nxt).start(priority=0)
                scatter_copy(r, nxt).start(priority=1)

        acts = []
        for f in range(nsplit):
            lo = f * width
            gate = jnp.dot(xb, wg_s[:, lo:lo + width], preferred_element_type=F32) + bg_ref[0, :, lo:lo + width]
            up = (jnp.dot(xb, wg_s[:, D_FF + lo:D_FF + lo + width], preferred_element_type=F32)
                  + bg_ref[0, :, D_FF + lo:D_FF + lo + width])
            issue(f)
            gate = jnp.minimum(gate, SWIGLU_LIMIT)
            up = jnp.clip(up, -SWIGLU_LIMIT, SWIGLU_LIMIT)
            acts.append(((up + 1.0) * (gate * jax.nn.sigmoid(SWIGLU_ALPHA * gate))).astype(BF16))
        act = jnp.concatenate(acts, axis=-1)
        tiles_per = D_MODEL // nsplit // LANES
        for f in range(nsplit):
            lo = f * (D_MODEL // nsplit)
            y = (jnp.dot(act, wd_s[:, lo:lo + D_MODEL // nsplit], preferred_element_type=F32)
                 + bd_ref[0, :, lo:lo + D_MODEL // nsplit])
            for s in range(tiles_per):
                ybuf[cur, pl.ds(f * tiles_per + s, bm, stride=ROW_TILES), :] = y[:, s * LANES:(s + 1) * LANES]

    @pl.when(j == nu)
    def _():
        wait_gather(cur)
        wait_scatter_piece(bm, nxt)

        def last(r, carry):
            scatter_copy(r, nxt).start()
            return carry

        lax.fori_loop(0, bm, last, 0)
        wait_scatter_piece(bm, cur)


def _experts(ht, inv, n4, block_expert, n_used, w_gu, b_gu, w_down, b_down, bm):
    n_blocks = inv.shape[0] // bm - 1
    assert ht.shape[0] // ROW_TILES * TOP_K == n4 + 2 * bm
    inv3 = inv.reshape(n_blocks + 1, 1, bm)

    def wsel(j, be, nu):
        return (be[j], 0, 0)

    def inv_spec(shift):
        return pl.BlockSpec((1, 1, bm), lambda j, be, nu: (jnp.minimum(j + shift + 1, n_blocks), 0, 0),
                            memory_space=pltpu.SMEM)

    kern = functools.partial(_expert_kernel, bm=bm, n4=n4, nsplit=4)
    return pl.pallas_call(
        kern,
        grid_spec=pltpu.PrefetchScalarGridSpec(
            num_scalar_prefetch=2,
            grid=(n_blocks,),
            in_specs=[inv_spec(-1), inv_spec(0), inv_spec(1),
                      pl.BlockSpec(memory_space=pl.ANY),
                      pl.BlockSpec((1, D_MODEL, 2 * D_FF), wsel),
                      pl.BlockSpec((1, 1, 2 * D_FF), wsel),
                      pl.BlockSpec((1, D_FF, D_MODEL), wsel),
                      pl.BlockSpec((1, 1, D_MODEL), wsel)],
            out_specs=pl.BlockSpec(memory_space=pl.ANY),
            scratch_shapes=[pltpu.VMEM((D_MODEL, 2 * D_FF), BF16),
                            pltpu.VMEM((D_FF, D_MODEL), BF16),
                            pltpu.VMEM((2, bm * ROW_TILES, LANES), F32),
                            pltpu.VMEM((2, bm * ROW_TILES, LANES), F32),
                            pltpu.SemaphoreType.DMA(()),
                            pltpu.SemaphoreType.DMA((2,))]),
        out_shape=jax.ShapeDtypeStruct(((n4 + 2 * bm) * ROW_TILES, LANES), F32),
        compiler_params=_cparams(("arbitrary",)),
        name="experts",
    )(block_expert, n_used, inv3, inv3, inv3, ht, w_gu, b_gu.reshape(N_EXPERTS, 1, 2 * D_FF), w_down,
      b_down.reshape(N_EXPERTS, 1, D_MODEL))


def _final_kernel(sl_ref, gate_ref, x1_ref, gt_ref, g_ref, o_ref, *, tm):
    gates = gate_ref[...]
    per_tok = TOP_K * ROW_TILES
    cols = []
    for s in range(ROW_TILES):
        acc = None
        for j in range(TOP_K):
            part = sl_ref[pl.ds(j * ROW_TILES + s, tm, stride=per_tok), :] * gates[:, j:j + 1]
            acc = part if acc is None else acc + part
        cols.append(acc)
    f = jnp.concatenate(cols, axis=-1)
    o_ref[0] = x1_ref[0] + gt_ref[0] * _rms(f, g_ref[...])


def _final(slots, gates, x1, gt, g_post_ffn, tok_off, tm, per_token):
    bsz, t, _ = x1.shape
    tiles = t // tm
    off = tok_off // tm
    kern = functools.partial(_final_kernel, tm=tm)
    return pl.pallas_call(
        kern,
        grid=(bsz, tiles),
        in_specs=[pl.BlockSpec((tm * TOP_K * ROW_TILES, LANES), lambda b, i: (off + b * tiles + i, 0)),
                  pl.BlockSpec((tm, TOP_K), lambda b, i: (off + b * tiles + i, 0)),
                  pl.BlockSpec((1, tm, D_MODEL), lambda b, i: (b, i, 0)),
                  _mod_spec(per_token, tm),
                  pl.BlockSpec((1, D_MODEL), lambda b, i: (0, 0))],
        out_specs=pl.BlockSpec((1, tm, D_MODEL), lambda b, i: (b, i, 0)),
        out_shape=jax.ShapeDtypeStruct((bsz, t, D_MODEL), F32),
        compiler_params=_cparams(("arbitrary", "arbitrary")),
        name="final",
    )(slots, gates, x1, gt, g_post_ffn.reshape(1, D_MODEL))


def _moe(ht, gsel, w_gu, b_gu, w_down, b_down, tm, bm):
    n = gsel.shape[0]
    rank, counts = _rank(gsel, tm)
    cnt = counts[0].astype(jnp.int32)
    nblk = (cnt + bm - 1) // bm
    blk_end = jnp.cumsum(nblk)
    blk_start = blk_end - nblk
    n_blocks = -(-(n * TOP_K) // bm) + N_EXPERTS
    blocks = jnp.arange(n_blocks, dtype=jnp.int32)
    block_expert = jnp.minimum(jnp.sum((blk_end[None, :] <= blocks[:, None]).astype(jnp.int32), axis=1),
                               N_EXPERTS - 1).astype(jnp.int32)
    n_used = blk_end[-1:].astype(jnp.int32)
    row_start = (blk_start * bm).astype(jnp.int32)
    row_end = (blk_end * bm).astype(jnp.int32)
    dest, gates = _dest(gsel, rank, row_start.astype(F32).reshape(1, N_EXPERTS), tm)
    dest_flat = dest.reshape(n * TOP_K)
    inv = _inverse_map(dest_flat, cnt, row_start, row_end, n_used, n_blocks, tm, bm)
    slots = _experts(ht, inv, n * TOP_K, block_expert, n_used, w_gu, b_gu, w_down, b_down, bm)
    return slots, gates


def _mix_path(x, mod, pool_buf, conv_buf, s0, start_pos, lp, tm, nb, per_token):
    (g_pre_mix, g_post_mix, g_pre_ffn, w_in_p, w_bd, pool_scale, conv_w, alog_row, dtb_row, norm_w,
     w_out_b, router_w, router_b) = lp
    bsz, t, _ = x.shape
    sh_mix, sc_mix, gt_mix, sh_ffn, sc_ffn, gt_ffn = (mod[:, i] for i in range(6))

    def shape_mod(m):
        if per_token:
            return jnp.repeat(m, t, axis=0).reshape(1, bsz * t, D_MODEL)
        return m.reshape(bsz, 1, D_MODEL)

    sh_mix, sc_mix, gt_mix, sh_ffn, sc_ffn, gt_ffn = map(shape_mod, (sh_mix, sc_mix, gt_mix, sh_ffn, sc_ffn, gt_ffn))
    xf = x.reshape(1, bsz * t, D_MODEL) if per_token else x
    proj = _in_proj(xf, sc_mix, sh_mix, g_pre_mix, w_in_p, tm, per_token).reshape(bsz, t, PROJ_WIDTH)
    tseq = min(tm, t)
    o_gdn, s_new = _gdn(proj, conv_buf, s0, conv_w, alog_row, dtb_row, norm_w, nb, tseq)
    pool_out = _pool(proj, pool_buf, w_bd, pool_scale, nb, tseq, start_pos)
    fl = (lambda a: a.reshape(1, bsz * t, a.shape[-1])) if per_token else (lambda a: a)
    x1, ht, gsel = _out_proj(xf, fl(pool_out), fl(o_gdn), gt_mix, sc_ffn, sh_ffn, g_post_mix, g_pre_ffn,
                             w_out_b, router_w, router_b, tm, per_token)
    def last_rows(buf, off, width):
        keep = buf.shape[1]
        if t >= keep:
            return proj[:, t - keep:, off:off + width]
        return jnp.concatenate([buf[:, t:], proj[:, :, off:off + width]], axis=1)

    pool_new = last_rows(pool_buf, OFF_U, POOL_WIDTH)
    conv_new = last_rows(conv_buf, OFF_QKV, CONV_CH)
    return x1, ht, gsel, gt_ffn, pool_new, conv_new, s_new


def kernel(x_prompt, x_sample, c_prompt, c_sample, state_pool, state_conv, state_ssm, w_ada, b_ada, g_pre_mix, g_post_mix, g_pre_ffn, g_post_ffn, w_in, w_pool, pool_scale, conv_w, a_log, dt_bias, gdn_norm_w, w_out, router_w, router_b, w_gu, b_gu, w_down, b_down):
    depth = w_ada.shape[0]
    bp, tp, _ = x_prompt.shape
    bs, ts, _ = x_sample.shape
    tm = 256
    y_p, y_s = x_prompt, x_sample
    outs = [[] for _ in range(6)]
    for l in range(depth):
        mod = _ada(jnp.concatenate([c_prompt, c_sample], axis=0), w_ada[l], b_ada[l]).reshape(bp + bs, 6, D_MODEL)
        o1, o2, o3 = POOL_WIDTH, POOL_WIDTH + CONV_CH, POOL_WIDTH + CONV_CH + GDN_WIDTH
        w_l = w_in[l]
        w_in_p = jnp.concatenate(
            [w_l[:, o1:o2], w_l[:, o2:o3], w_l[:, :o1], w_l[:, o3:],
             jnp.zeros((D_MODEL, LANES - (w_l.shape[1] - o3)), F32)], axis=1).astype(BF16)
        w_bd = jax.scipy.linalg.block_diag(*[w_pool[l, g] for g in range(len(POOL_WINDOWS))])
        pad_row = lambda v: jnp.pad(v, (0, LANES - v.shape[0])).reshape(1, LANES)
        lp = (g_pre_mix[l], g_post_mix[l], g_pre_ffn[l], w_in_p, w_bd, pool_scale[l], conv_w[l],
              pad_row(a_log[l]), pad_row(dt_bias[l]), gdn_norm_w[l].reshape(1, HEAD_DIM),
              w_out[l].astype(BF16), router_w[l], router_b[l])
        zero_pool = jnp.zeros((bp, POOL_BUF, POOL_WIDTH), F32)
        zero_conv = jnp.zeros((bp, CONV_WIDTH - 1, CONV_CH), F32)
        zero_ssm = jnp.zeros((bp, GDN_HEADS, HEAD_DIM, HEAD_DIM), F32)
        x1p, htp, gp, gtp, pp, cp, sp = _mix_path(y_p, mod[:bp], zero_pool, zero_conv, zero_ssm, 0, lp, tm, 1, False)
        x1s, hts, gs, gts, ps, cs, ss = _mix_path(y_s, mod[bp:], state_pool[l], state_conv[l], state_ssm[l],
                                                  PAST_LEN, lp, tm, 8, True)
        spare = jnp.zeros((2 * MOE_BLOCK // TOP_K * ROW_TILES, LANES), F32)
        ht = jnp.concatenate([htp, hts, spare], axis=0)
        gsel = jnp.concatenate([gp, gs], axis=0)
        slots, gates = _moe(ht, gsel, w_gu[l], b_gu[l], w_down[l], b_down[l], tm, MOE_BLOCK)
        y_p = _final(slots, gates, x1p, gtp, g_post_ffn[l], 0, tm, False)
        y_s = _final(slots, gates, x1s, gts, g_post_ffn[l], bp * tp, tm, True).reshape(bs, ts, D_MODEL)
        for lst, val in zip(outs, (pp, cp, sp, ps, cs, ss)):
            lst.append(val)
    return (y_p, y_s) + tuple(jnp.stack(o) for o in outs)
```

```python
import functools
import math

import jax
import jax.numpy as jnp
from jax import lax
from jax.experimental import pallas as pl
from jax.experimental.pallas import tpu as pltpu

F32 = jnp.float32
BF16 = jnp.bfloat16

D_MODEL = 1024
PAST_LEN = 16384
POOL_WINDOWS = (2, 4, 8, 16)
POOL_WIDTH = D_MODEL // 4
POOL_GROUP = POOL_WIDTH // len(POOL_WINDOWS)
POOL_BUF = max(POOL_WINDOWS) - 1
GDN_WIDTH = D_MODEL - POOL_WIDTH
HEAD_DIM = 128
GDN_HEADS = GDN_WIDTH // HEAD_DIM
CONV_WIDTH = 4
CONV_CH = 3 * GDN_WIDTH
GDN_CHUNK = 64
N_EXPERTS = 32
TOP_K = 4
D_FF = D_MODEL
SWIGLU_LIMIT = 7.0
SWIGLU_ALPHA = 1.702
EPS = 1e-6

SUBLANES = 8
LANES = 128
ROW_TILES = D_MODEL // LANES
OFF_QKV = 0
OFF_Z = OFF_QKV + CONV_CH
OFF_U = OFF_Z + GDN_WIDTH
OFF_AB = OFF_U + POOL_WIDTH
PROJ_WIDTH = OFF_AB + LANES
assert OFF_Z % GDN_WIDTH == 0 and OFF_U % POOL_WIDTH == 0 and OFF_AB % LANES == 0
MOE_BLOCK = 512
VMEM_LIMIT = 56 * 1024 * 1024


def _cparams(sem):
    return pltpu.CompilerParams(dimension_semantics=sem, vmem_limit_bytes=VMEM_LIMIT)


def _dot(a, b):
    return jnp.dot(a.astype(BF16), b.astype(BF16), preferred_element_type=F32)


def _dot_nt(a, b):
    return lax.dot_general(a.astype(BF16), b.astype(BF16), (((1,), (1,)), ((), ())),
                           preferred_element_type=F32)


def _silu(x):
    return x * jax.nn.sigmoid(x)


def _rms(x, g):
    return x * lax.rsqrt(jnp.mean(x * x, axis=-1, keepdims=True) + EPS) * g


def _ada_kernel(c_ref, w_ref, b_ref, o_ref):
    o_ref[...] = _dot(_silu(c_ref[...]), w_ref[...]) + b_ref[...]


def _ada(c, w_ada, b_ada):
    n = c.shape[0]
    cols = w_ada.shape[1]
    blk = D_MODEL
    return pl.pallas_call(
        _ada_kernel,
        grid=(cols // blk,),
        in_specs=[pl.BlockSpec((n, D_MODEL), lambda j: (0, 0)),
                  pl.BlockSpec((D_MODEL, blk), lambda j: (0, j)),
                  pl.BlockSpec((1, blk), lambda j: (0, j))],
        out_specs=pl.BlockSpec((n, blk), lambda j: (0, j)),
        out_shape=jax.ShapeDtypeStruct((n, cols), F32),
        compiler_params=_cparams(("arbitrary",)),
        name="ada",
    )(c, w_ada, b_ada.reshape(1, cols))


def _in_kernel(x_ref, sc_ref, sh_ref, g_ref, w_ref, o_ref):
    h = _rms(x_ref[0], g_ref[...]) * (1.0 + sc_ref[0]) + sh_ref[0]
    o_ref[0] = jnp.dot(h.astype(BF16), w_ref[...], preferred_element_type=F32)


def _mod_spec(per_token, tm):
    if per_token:
        return pl.BlockSpec((1, tm, D_MODEL), lambda b, t: (0, t, 0))
    return pl.BlockSpec((1, 1, D_MODEL), lambda b, t: (b, 0, 0))


def _in_proj(x, sc, sh, g, w_in_p, tm, per_token):
    bsz, t, _ = x.shape
    return pl.pallas_call(
        _in_kernel,
        grid=(bsz, t // tm),
        in_specs=[pl.BlockSpec((1, tm, D_MODEL), lambda b, i: (b, i, 0)),
                  _mod_spec(per_token, tm), _mod_spec(per_token, tm),
                  pl.BlockSpec((1, D_MODEL), lambda b, i: (0, 0)),
                  pl.BlockSpec((D_MODEL, PROJ_WIDTH), lambda b, i: (0, 0))],
        out_specs=pl.BlockSpec((1, tm, PROJ_WIDTH), lambda b, i: (b, i, 0)),
        out_shape=jax.ShapeDtypeStruct((bsz, t, PROJ_WIDTH), F32),
        compiler_params=_cparams(("arbitrary", "arbitrary")),
        name="in_proj",
    )(x, sc, sh, g.reshape(1, D_MODEL), w_in_p)


def _shift_rows(prev, x, d):
    if d == 0:
        return x
    e = jnp.concatenate([prev, x], axis=0)
    return pltpu.roll(e, d, axis=0)[SUBLANES:]


def _cumsum_rows(x):
    n = x.shape[0]
    row = lax.broadcasted_iota(jnp.int32, x.shape, 0)
    s = 1
    while s < n:
        x = x + jnp.where(row >= s, pltpu.roll(x, s, axis=0), 0.0)
        s *= 2
    return x


def _gdn_kernel(qkv_ref, z_ref, ab_ref, cs_ref, cw_ref, s0_ref, alog_ref, dtb_ref, nw_ref, o_ref, sn_ref,
                s_scr, prev_scr, *, nb, tm, tmp, chunk):
    ti = pl.program_id(1)
    nt = pl.num_programs(1)
    n_chunks = tmp // chunk
    n_sq = int(math.log2(chunk)) - 1

    @pl.when(ti == 0)
    def _():
        s_scr[...] = s0_ref[...]
        for b in range(nb):
            prev_scr[b] = jnp.concatenate(
                [jnp.zeros((SUBLANES - (CONV_WIDTH - 1), CONV_CH), F32), cs_ref[b]], axis=0)

    rowi = lax.broadcasted_iota(jnp.int32, (tmp, LANES), 0)
    ri = lax.broadcasted_iota(jnp.int32, (chunk, chunk), 0)
    ci = lax.broadcasted_iota(jnp.int32, (chunk, chunk), 1)
    tril = ri >= ci
    strict = ri > ci
    eye = (ri == ci).astype(F32)
    neg_a = -jnp.exp(alog_ref[...])
    dtb = dtb_ref[...]
    nw = nw_ref[...]

    def padrows(x):
        if tm == tmp:
            return x
        return jnp.concatenate([x, jnp.zeros((tmp - tm, x.shape[1]), F32)], axis=0)

    cw = cw_ref[...]
    valid = rowi < tm
    items = []
    for b in range(nb):
        x = padrows(qkv_ref[b])
        prev = prev_scr[b]
        acc = x * cw[CONV_WIDTH - 1:CONV_WIDTH]
        for d in range(1, CONV_WIDTH):
            acc = acc + _shift_rows(prev, x, d) * cw[CONV_WIDTH - 1 - d:CONV_WIDTH - d]
        if tm >= SUBLANES:
            prev_scr[b] = x[tm - SUBLANES:tm]
        y = _silu(acc)
        ab = padrows(ab_ref[b])
        g_all = jnp.where(valid, neg_a * jax.nn.softplus(ab + dtb), 0.0)
        beta_all = jnp.where(valid, jax.nn.sigmoid(ab), 0.0)
        for h in range(GDN_HEADS):
            qc = y[:, h * HEAD_DIM:(h + 1) * HEAD_DIM]
            kc = y[:, GDN_WIDTH + h * HEAD_DIM:GDN_WIDTH + (h + 1) * HEAD_DIM]
            v = y[:, 2 * GDN_WIDTH + h * HEAD_DIM:2 * GDN_WIDTH + (h + 1) * HEAD_DIM]
            q = qc * lax.rsqrt(jnp.sum(qc * qc, axis=-1, keepdims=True) + EPS) * (HEAD_DIM ** -0.5)
            k = kc * lax.rsqrt(jnp.sum(kc * kc, axis=-1, keepdims=True) + EPS)
            gb = jnp.broadcast_to(g_all[:, h:h + 1], (tmp, LANES))
            bb = jnp.broadcast_to(beta_all[:, GDN_HEADS + h:GDN_HEADS + h + 1], (tmp, LANES))
            for c in range(n_chunks):
                sl = slice(c * chunk, (c + 1) * chunk)
                items.append(dict(b=b, h=h, c=c, q=q[sl], k=k[sl], v=v[sl], bb=bb[sl], gb=gb[sl]))

    for it in items:
        gcb = _cumsum_rows(it["gb"])
        gct = jnp.transpose(gcb)[:chunk]
        diff = gcb[:, :chunk] - gct
        it["decay"] = jnp.where(tril, jnp.exp(jnp.where(tril, diff, 0.0)), 0.0)
        it["kb"] = it["k"] * it["bb"]
        it["eg"] = jnp.exp(gcb)
        gl = gcb[chunk - 1:chunk]
        it["egl"] = jnp.exp(gl)
        it["kdt"] = jnp.transpose(it["k"] * jnp.exp(gl - gcb))
    for it in items:
        kq = _dot_nt(jnp.concatenate([it["kb"], it["q"]], axis=0), it["k"])
        it["a"] = jnp.where(strict, kq[:chunk] * it["decay"], 0.0)
        it["qk"] = jnp.where(tril, kq[chunk:] * it["decay"], 0.0)
    for it in items:
        it["inv"] = eye - it["a"]
        it["p"] = _dot(it["a"], it["a"])
    for lvl in range(n_sq):
        for it in items:
            if lvl < n_sq - 1:
                r = _dot(jnp.concatenate([it["inv"], it["p"]], axis=0), it["p"])
                it["inv"] = it["inv"] + r[:chunk]
                it["p"] = r[chunk:]
            else:
                it["inv"] = it["inv"] + _dot(it["inv"], it["p"])
    for it in items:
        it["wu"] = _dot(it["inv"], jnp.concatenate([it["kb"] * it["eg"], it["v"] * it["bb"]], axis=1))
    for it in items:
        mn = _dot(it["kdt"], it["wu"])
        qo = _dot(it["qk"], it["wu"])
        it["mp"], it["nc"] = mn[:, :HEAD_DIM], mn[:, HEAD_DIM:]
        it["qp"] = it["q"] * it["eg"] - qo[:, :HEAD_DIM]
        it["oc"] = qo[:, HEAD_DIM:]

    state = {(b, h): s_scr[b, h] for b in range(nb) for h in range(GDN_HEADS)}
    for c in range(n_chunks):
        for it in items:
            if it["c"] != c:
                continue
            s = state[(it["b"], it["h"])]
            r = _dot(jnp.concatenate([it["qp"], it["mp"]], axis=0), s)
            it["o"] = r[:chunk] + it["oc"]
            state[(it["b"], it["h"])] = s * it["egl"] - r[chunk:] + it["nc"]
    for (b, h), s in state.items():
        s_scr[b, h] = s

    for b in range(nb):
        zf = padrows(z_ref[b])
        for h in range(GDN_HEADS):
            outs = [it["o"] for it in items if it["b"] == b and it["h"] == h]
            o = outs[0] if n_chunks == 1 else jnp.concatenate(outs, axis=0)
            o = (o * lax.rsqrt(jnp.mean(o * o, axis=-1, keepdims=True) + EPS) * nw
                 * _silu(zf[:, h * HEAD_DIM:(h + 1) * HEAD_DIM]))
            o_ref[b, :, h * HEAD_DIM:(h + 1) * HEAD_DIM] = o[:tm]

    @pl.when(ti == nt - 1)
    def _():
        sn_ref[...] = s_scr[...]


def _gdn(proj, state_conv, s0, conv_w, alog_row, dtb_row, norm_w, nb, tm):
    bsz, t, _ = proj.shape
    tmp = max(tm, SUBLANES)
    chunk = min(GDN_CHUNK, tmp)
    row = pl.BlockSpec((1, LANES), lambda b, i: (0, 0))
    sspec = pl.BlockSpec((nb, GDN_HEADS, HEAD_DIM, HEAD_DIM), lambda b, i: (b, 0, 0, 0))
    kern = functools.partial(_gdn_kernel, nb=nb, tm=tm, tmp=tmp, chunk=chunk)
    return pl.pallas_call(
        kern,
        grid=(bsz // nb, t // tm),
        in_specs=[pl.BlockSpec((nb, tm, CONV_CH), lambda b, i: (b, i, OFF_QKV // CONV_CH)),
                  pl.BlockSpec((nb, tm, GDN_WIDTH), lambda b, i: (b, i, OFF_Z // GDN_WIDTH)),
                  pl.BlockSpec((nb, tm, LANES), lambda b, i: (b, i, OFF_AB // LANES)),
                  pl.BlockSpec((nb, CONV_WIDTH - 1, CONV_CH), lambda b, i: (b, 0, 0)),
                  pl.BlockSpec((CONV_WIDTH, CONV_CH), lambda b, i: (0, 0)),
                  sspec, row, row, row],
        out_specs=[pl.BlockSpec((nb, tm, GDN_WIDTH), lambda b, i: (b, i, 0)), sspec],
        out_shape=[jax.ShapeDtypeStruct((bsz, t, GDN_WIDTH), F32),
                   jax.ShapeDtypeStruct(s0.shape, F32)],
        scratch_shapes=[pltpu.VMEM((nb, GDN_HEADS, HEAD_DIM, HEAD_DIM), F32),
                        pltpu.VMEM((nb, SUBLANES, CONV_CH), F32)],
        compiler_params=_cparams(("arbitrary", "arbitrary")),
        name="gdn",
    )(proj, proj, proj, state_conv, conv_w, s0, alog_row, dtb_row, norm_w)


def _pool_kernel(u_ref, st_ref, w_ref, sc_ref, o_ref, prev_scr, *, nb, tm, tmp, start_pos):
    ti = pl.program_id(1)
    halo = 2 * SUBLANES

    @pl.when(ti == 0)
    def _():
        for b in range(nb):
            prev_scr[b] = jnp.concatenate([jnp.zeros((halo - POOL_BUF, POOL_WIDTH), F32), st_ref[b]], axis=0)

    lane = lax.broadcasted_iota(jnp.int32, (tmp, POOL_WIDTH), 1)
    rowi = lax.broadcasted_iota(jnp.int32, (tmp, POOL_WIDTH), 0)
    pos1 = start_pos + ti * tm + rowi + 1
    grp = lane // POOL_GROUP
    win = jnp.where(grp == 0, POOL_WINDOWS[0],
                    jnp.where(grp == 1, POOL_WINDOWS[1], jnp.where(grp == 2, POOL_WINDOWS[2], POOL_WINDOWS[3])))
    cnt = jnp.minimum(pos1, win).astype(F32)
    for b in range(nb):
        x = u_ref[b]
        if tm != tmp:
            x = jnp.concatenate([x, jnp.zeros((tmp - tm, POOL_WIDTH), F32)], axis=0)
        e = jnp.concatenate([prev_scr[b], x], axis=0)
        sums = []
        s = e
        for w in POOL_WINDOWS:
            s = s + pltpu.roll(s, w // 2, axis=0)
            sums.append(s[halo:])
        sel = jnp.where(grp == 0, sums[0], jnp.where(grp == 1, sums[1], jnp.where(grp == 2, sums[2], sums[3])))
        pooled = sel / cnt - x
        out = _dot(pooled, w_ref[...]) * sc_ref[...]
        o_ref[b] = out[:tm]
        if tm >= halo:
            prev_scr[b] = x[tm - halo:tm]


def _pool(proj, state_pool, w_bd, pool_scale, nb, tm, start_pos):
    bsz, t, _ = proj.shape
    tmp = max(tm, SUBLANES)
    kern = functools.partial(_pool_kernel, nb=nb, tm=tm, tmp=tmp, start_pos=start_pos)
    return pl.pallas_call(
        kern,
        grid=(bsz // nb, t // tm),
        in_specs=[pl.BlockSpec((nb, tm, POOL_WIDTH), lambda b, i: (b, i, OFF_U // POOL_WIDTH)),
                  pl.BlockSpec((nb, POOL_BUF, POOL_WIDTH), lambda b, i: (b, 0, 0)),
                  pl.BlockSpec((POOL_WIDTH, POOL_WIDTH), lambda b, i: (0, 0)),
                  pl.BlockSpec((1, POOL_WIDTH), lambda b, i: (0, 0))],
        out_specs=pl.BlockSpec((nb, tm, POOL_WIDTH), lambda b, i: (b, i, 0)),
        out_shape=jax.ShapeDtypeStruct((bsz, t, POOL_WIDTH), F32),
        scratch_shapes=[pltpu.VMEM((nb, 2 * SUBLANES, POOL_WIDTH), F32)],
        compiler_params=_cparams(("arbitrary", "arbitrary")),
        name="pool",
    )(proj, state_pool, w_bd, pool_scale.reshape(1, POOL_WIDTH))


def _out_kernel(x_ref, po_ref, og_ref, gt_ref, sc_ref, sh_ref, gpm_ref, gpf_ref, wo_ref, rw_ref, rb_ref,
                x1_ref, ht_ref, g_ref, *, tm):
    mix = _dot(po_ref[0], wo_ref[:POOL_WIDTH]) + _dot(og_ref[0], wo_ref[POOL_WIDTH:])
    x1 = x_ref[0] + gt_ref[0] * _rms(mix, gpm_ref[...])
    x1_ref[0] = x1
    h = _rms(x1, gpf_ref[...]) * (1.0 + sc_ref[0]) + sh_ref[0]
    for s in range(ROW_TILES):
        ht_ref[pl.ds(s, tm, stride=ROW_TILES), :] = h[:, s * LANES:(s + 1) * LANES]
    logits = _dot(h, rw_ref[...]) + rb_ref[...]
    lane = lax.broadcasted_iota(jnp.int32, logits.shape, 1)
    work = logits
    sel = jnp.zeros(logits.shape, jnp.bool_)
    top = None
    den = jnp.zeros((tm, 1), F32)
    for j in range(TOP_K):
        m = jnp.max(work, axis=-1, keepdims=True)
        idx = jnp.min(jnp.where(work == m, lane, N_EXPERTS), axis=-1, keepdims=True)
        hit = lane == idx
        sel = sel | hit
        work = jnp.where(hit, -jnp.inf, work)
        if j == 0:
            top = m
        den = den + jnp.exp(m - top)
    gates = jnp.exp(logits - top) / den
    g_ref[...] = jnp.where(sel, gates, -1.0)


def _out_proj(x, pool_out, o_gdn, gt, sc, sh, g_post_mix, g_pre_ffn, w_out_b, router_w, router_b, tm, per_token):
    bsz, t, _ = x.shape
    n = bsz * t
    tiles = t // tm

    def tok(width):
        return pl.BlockSpec((1, tm, width), lambda b, i: (b, i, 0))

    def const(shape):
        return pl.BlockSpec(shape, lambda b, i: tuple(0 for _ in shape))

    kern = functools.partial(_out_kernel, tm=tm)
    return pl.pallas_call(
        kern,
        grid=(bsz, tiles),
        in_specs=[tok(D_MODEL), tok(POOL_WIDTH), tok(GDN_WIDTH),
                  _mod_spec(per_token, tm), _mod_spec(per_token, tm), _mod_spec(per_token, tm),
                  const((1, D_MODEL)), const((1, D_MODEL)), const((D_MODEL, D_MODEL)),
                  const((D_MODEL, N_EXPERTS)), const((1, N_EXPERTS))],
        out_specs=[tok(D_MODEL),
                   pl.BlockSpec((tm * ROW_TILES, LANES), lambda b, i: (b * tiles + i, 0)),
                   pl.BlockSpec((tm, N_EXPERTS), lambda b, i: (b * tiles + i, 0))],
        out_shape=[jax.ShapeDtypeStruct((bsz, t, D_MODEL), F32),
                   jax.ShapeDtypeStruct((n * ROW_TILES, LANES), F32),
                   jax.ShapeDtypeStruct((n, N_EXPERTS), F32)],
        compiler_params=_cparams(("arbitrary", "arbitrary")),
        name="out_proj",
    )(x, pool_out, o_gdn, gt, sc, sh, g_post_mix.reshape(1, D_MODEL), g_pre_ffn.reshape(1, D_MODEL),
      w_out_b, router_w, router_b.reshape(1, N_EXPERTS))


def _rank_kernel(g_ref, r_ref, cnt_ref, carry, *, tm):
    i = pl.program_id(0)

    @pl.when(i == 0)
    def _():
        carry[...] = jnp.zeros_like(carry)

    mask = (g_ref[...] >= 0.0).astype(F32)
    ri = lax.broadcasted_iota(jnp.int32, (tm, tm), 0)
    ci = lax.broadcasted_iota(jnp.int32, (tm, tm), 1)
    before = (ri > ci).astype(F32)
    c = carry[...]
    r_ref[...] = _dot(before, mask) + c
    c = c + jnp.sum(mask, axis=0, keepdims=True)
    carry[...] = c
    cnt_ref[...] = c


def _rank(gsel, tm):
    n = gsel.shape[0]
    kern = functools.partial(_rank_kernel, tm=tm)
    return pl.pallas_call(
        kern,
        grid=(n // tm,),
        in_specs=[pl.BlockSpec((tm, N_EXPERTS), lambda i: (i, 0))],
        out_specs=[pl.BlockSpec((tm, N_EXPERTS), lambda i: (i, 0)),
                   pl.BlockSpec((1, N_EXPERTS), lambda i: (0, 0))],
        out_shape=[jax.ShapeDtypeStruct((n, N_EXPERTS), F32),
                   jax.ShapeDtypeStruct((1, N_EXPERTS), F32)],
        scratch_shapes=[pltpu.VMEM((1, N_EXPERTS), F32)],
        compiler_params=_cparams(("arbitrary",)),
        name="rank",
    )(gsel)


def _dest_kernel(g_ref, r_ref, start_ref, d_ref, gt_ref, *, tm):
    g = g_ref[...]
    mask = g >= 0.0
    maskf = mask.astype(F32)
    dest = start_ref[...] + r_ref[...]
    ri = lax.broadcasted_iota(jnp.int32, (N_EXPERTS, N_EXPERTS), 0)
    ci = lax.broadcasted_iota(jnp.int32, (N_EXPERTS, N_EXPERTS), 1)
    slot = _dot(maskf, (ri < ci).astype(F32))
    lane4 = lax.broadcasted_iota(jnp.int32, (tm, TOP_K), 1)
    d_out = jnp.zeros((tm, TOP_K), F32)
    g_out = jnp.zeros((tm, TOP_K), F32)
    for j in range(TOP_K):
        pick = mask & (slot == float(j))
        dj = jnp.sum(jnp.where(pick, dest, 0.0), axis=-1, keepdims=True)
        gj = jnp.sum(jnp.where(pick, g, 0.0), axis=-1, keepdims=True)
        d_out = jnp.where(lane4 == j, dj, d_out)
        g_out = jnp.where(lane4 == j, gj, g_out)
    d_ref[...] = d_out.astype(jnp.int32)
    gt_ref[...] = g_out


def _dest(gsel, rank, start_row, tm):
    n = gsel.shape[0]
    kern = functools.partial(_dest_kernel, tm=tm)
    return pl.pallas_call(
        kern,
        grid=(n // tm,),
        in_specs=[pl.BlockSpec((tm, N_EXPERTS), lambda i: (i, 0)),
                  pl.BlockSpec((tm, N_EXPERTS), lambda i: (i, 0)),
                  pl.BlockSpec((1, N_EXPERTS), lambda i: (0, 0))],
        out_specs=[pl.BlockSpec((tm, TOP_K), lambda i: (i, 0)),
                   pl.BlockSpec((tm, TOP_K), lambda i: (i, 0))],
        out_shape=[jax.ShapeDtypeStruct((n, TOP_K), jnp.int32),
                   jax.ShapeDtypeStruct((n, TOP_K), F32)],
        compiler_params=_cparams(("arbitrary",)),
        name="dest",
    )(gsel, rank, start_row)


def _inv_kernel(cnt_ref, start_ref, end_ref, nu_ref, dest_ref, inv_ref, *, tm, bm, n_blocks, n4):
    i = pl.program_id(0)
    log_bm = bm.bit_length() - 1

    @pl.when(i == 0)
    def _():
        def fill(r, carry):
            inv_ref[bm + r] = n4 + ((r >> log_bm) & 1) * bm + (r & (bm - 1))
            return carry

        def per_expert(e, carry):
            lax.fori_loop(start_ref[e] + cnt_ref[e], end_ref[e], fill, 0)
            return carry

        lax.fori_loop(0, N_EXPERTS, per_expert, 0)
        lax.fori_loop(nu_ref[0] * bm, n_blocks * bm, fill, 0)

        def placeholder(r, carry):
            inv_ref[r] = n4 + bm + r
            return carry

        lax.fori_loop(0, bm, placeholder, 0)

    base = i * (tm * TOP_K)
    unroll = 8

    def body(g, carry):
        ds = [dest_ref[g * unroll + u] for u in range(unroll)]
        for u in range(unroll):
            inv_ref[bm + ds[u]] = base + g * unroll + u
        return carry

    lax.fori_loop(0, tm * TOP_K // unroll, body, 0)


def _inverse_map(dest_flat, cnt, start, end, n_used, n_blocks, tm, bm):
    n4 = dest_flat.shape[0]
    assert bm & (bm - 1) == 0
    kern = functools.partial(_inv_kernel, tm=tm, bm=bm, n_blocks=n_blocks, n4=n4)
    return pl.pallas_call(
        kern,
        grid_spec=pltpu.PrefetchScalarGridSpec(
            num_scalar_prefetch=4,
            grid=(n4 // (tm * TOP_K),),
            in_specs=[pl.BlockSpec((tm * TOP_K,), lambda i, *_: (i,), memory_space=pltpu.SMEM)],
            out_specs=pl.BlockSpec(memory_space=pltpu.SMEM)),
        out_shape=jax.ShapeDtypeStruct(((n_blocks + 1) * bm,), jnp.int32),
        compiler_params=_cparams(("arbitrary",)),
        name="inverse_map",
    )(cnt, start, end, n_used, dest_flat)


def _expert_kernel(be_ref, nx_ref, nu_ref, inv_prev, inv_cur, inv_next, ht_ref, wg_ref, bg_ref, wd_ref, bd_ref,
                   sl_ref, wg_f, wd_f, wg_s, wd_s, xbuf, ybuf, wsem, gsem, ssem, *, bm, n4, nsplit):
    j = pl.program_id(0)
    nu = nu_ref[0]
    prev = be_ref[jnp.maximum(j - 1, 0)]
    fresh = (j == 0) | (be_ref[j] != prev)
    active = j < nu
    cur = j % 2
    nxt = 1 - cur
    block_rows = bm * ROW_TILES

    def gather_copy(inv_blk, r, slot):
        src = pl.multiple_of(jnp.minimum(inv_blk[0, 0, r] >> 2, n4 // TOP_K - 1) * ROW_TILES, ROW_TILES)
        return pltpu.make_async_copy(ht_ref.at[pl.ds(src, ROW_TILES), :],
                                     xbuf.at[slot, pl.ds(r * ROW_TILES, ROW_TILES), :], gsem)

    def scatter_copy(r, slot):
        dst = pl.multiple_of(inv_prev[0, 0, r] * ROW_TILES, ROW_TILES)
        return pltpu.make_async_copy(ybuf.at[slot, pl.ds(r * ROW_TILES, ROW_TILES), :],
                                     sl_ref.at[pl.ds(dst, ROW_TILES), :], ssem.at[cur])

    def spare_copy(r):
        return pltpu.make_async_copy(ybuf.at[0, pl.ds(r * ROW_TILES, ROW_TILES), :],
                                     sl_ref.at[pl.ds((n4 + r) * ROW_TILES, ROW_TILES), :], ssem.at[1])

    def weight_copies(e):
        return (pltpu.make_async_copy(wg_ref.at[e], wg_f, wsem.at[0]),
                pltpu.make_async_copy(wd_ref.at[e], wd_f, wsem.at[1]))

    def wait_gather(slot):
        pltpu.make_async_copy(ht_ref.at[pl.ds(0, block_rows), :], xbuf.at[slot], gsem).wait()

    def wait_scatter_piece(rows, sem_slot):
        pltpu.make_async_copy(ybuf.at[0, pl.ds(0, rows * ROW_TILES), :],
                              sl_ref.at[pl.ds(0, rows * ROW_TILES), :], ssem.at[sem_slot]).wait()

    @pl.when(j == 0)
    def _():
        ybuf[...] = jnp.zeros_like(ybuf)

        def first(r, carry):
            gather_copy(inv_cur, r, 0).start()
            spare_copy(r).start()
            return carry

        lax.fori_loop(0, bm, first, 0)
        for c in weight_copies(be_ref[0]):
            c.start(priority=1)

    @pl.when(active & fresh)
    def _():
        for c in weight_copies(be_ref[j]):
            c.wait()
        wg_s[...] = wg_f[...].astype(BF16)
        wd_s[...] = wd_f[...].astype(BF16)

        @pl.when(nx_ref[j] != be_ref[j])
        def _():
            for c in weight_copies(nx_ref[j]):
                c.start(priority=1)

    @pl.when(active)
    def _():
        wait_gather(cur)
        x = jnp.concatenate([xbuf[cur, pl.ds(s, bm, stride=ROW_TILES), :] for s in range(ROW_TILES)], axis=-1)
        xb = x.astype(BF16)
        per = bm // nsplit
        width = D_FF // nsplit

        wait_scatter_piece(bm, nxt)

        def issue(piece):
            for r in range(piece * per, (piece + 1) * per):
                gather_copy(inv_next, r, nxt).start(priority=0)
                scatter_copy(r, nxt).start(priority=1)

        acts = []
        for f in range(nsplit):
            lo = f * width
            gate = jnp.dot(xb, wg_s[:, lo:lo + width], preferred_element_type=F32) + bg_ref[0, :, lo:lo + width]
            up = (jnp.dot(xb, wg_s[:, D_FF + lo:D_FF + lo + width], preferred_element_type=F32)
                  + bg_ref[0, :, D_FF + lo:D_FF + lo + width])
            issue(f)
            gate = jnp.minimum(gate, SWIGLU_LIMIT)
            up = jnp.clip(up, -SWIGLU_LIMIT, SWIGLU_LIMIT)
            acts.append(((up + 1.0) * (gate * jax.nn.sigmoid(SWIGLU_ALPHA * gate))).astype(BF16))
        act = jnp.concatenate(acts, axis=-1)
        tiles_per = D_MODEL // nsplit // LANES
        for f in range(nsplit):
            lo = f * (D_MODEL // nsplit)
            y = (jnp.dot(act, wd_s[:, lo:lo + D_MODEL // nsplit], preferred_element_type=F32)
                 + bd_ref[0, :, lo:lo + D_MODEL // nsplit])
            for s in range(tiles_per):
                ybuf[cur, pl.ds(f * tiles_per + s, bm, stride=ROW_TILES), :] = y[:, s * LANES:(s + 1) * LANES]

    @pl.when(j == nu)
    def _():
        wait_gather(cur)
        wait_scatter_piece(bm, nxt)

        def last(r, carry):
            scatter_copy(r, nxt).start()
            return carry

        lax.fori_loop(0, bm, last, 0)
        wait_scatter_piece(bm, cur)


def _experts(ht, inv, n4, block_expert, next_expert, n_used, w_gu, b_gu, w_down, b_down, bm):
    n_blocks = inv.shape[0] // bm - 1
    assert ht.shape[0] // ROW_TILES * TOP_K == n4
    inv3 = inv.reshape(n_blocks + 1, 1, bm)

    def bsel(j, be, nx, nu):
        return (be[j], 0, 0)

    def inv_spec(shift):
        return pl.BlockSpec((1, 1, bm), lambda j, be, nx, nu: (jnp.minimum(j + shift + 1, n_blocks), 0, 0),
                            memory_space=pltpu.SMEM)

    hbm = pl.BlockSpec(memory_space=pl.ANY)
    kern = functools.partial(_expert_kernel, bm=bm, n4=n4, nsplit=4)
    return pl.pallas_call(
        kern,
        grid_spec=pltpu.PrefetchScalarGridSpec(
            num_scalar_prefetch=3,
            grid=(n_blocks,),
            in_specs=[inv_spec(-1), inv_spec(0), inv_spec(1), hbm, hbm,
                      pl.BlockSpec((1, 1, 2 * D_FF), bsel), hbm,
                      pl.BlockSpec((1, 1, D_MODEL), bsel)],
            out_specs=hbm,
            scratch_shapes=[pltpu.VMEM((D_MODEL, 2 * D_FF), F32),
                            pltpu.VMEM((D_FF, D_MODEL), F32),
                            pltpu.VMEM((D_MODEL, 2 * D_FF), BF16),
                            pltpu.VMEM((D_FF, D_MODEL), BF16),
                            pltpu.VMEM((2, bm * ROW_TILES, LANES), F32),
                            pltpu.VMEM((2, bm * ROW_TILES, LANES), F32),
                            pltpu.SemaphoreType.DMA((2,)),
                            pltpu.SemaphoreType.DMA(()),
                            pltpu.SemaphoreType.DMA((2,))]),
        out_shape=jax.ShapeDtypeStruct(((n4 + 2 * bm) * ROW_TILES, LANES), F32),
        compiler_params=_cparams(("arbitrary",)),
        name="experts",
    )(block_expert, next_expert, n_used, inv3, inv3, inv3, ht, w_gu, b_gu.reshape(N_EXPERTS, 1, 2 * D_FF), w_down,
      b_down.reshape(N_EXPERTS, 1, D_MODEL))


def _final_kernel(sl_ref, gate_ref, x1_ref, gt_ref, g_ref, o_ref, *, tm):
    gates = gate_ref[...]
    per_tok = TOP_K * ROW_TILES
    cols = []
    for s in range(ROW_TILES):
        acc = None
        for j in range(TOP_K):
            part = sl_ref[pl.ds(j * ROW_TILES + s, tm, stride=per_tok), :] * gates[:, j:j + 1]
            acc = part if acc is None else acc + part
        cols.append(acc)
    f = jnp.concatenate(cols, axis=-1)
    o_ref[0] = x1_ref[0] + gt_ref[0] * _rms(f, g_ref[...])


def _final(slots, gates, x1, gt, g_post_ffn, tok_off, tm, per_token):
    bsz, t, _ = x1.shape
    tiles = t // tm
    off = tok_off // tm
    kern = functools.partial(_final_kernel, tm=tm)
    return pl.pallas_call(
        kern,
        grid=(bsz, tiles),
        in_specs=[pl.BlockSpec((tm * TOP_K * ROW_TILES, LANES), lambda b, i: (off + b * tiles + i, 0)),
                  pl.BlockSpec((tm, TOP_K), lambda b, i: (off + b * tiles + i, 0)),
                  pl.BlockSpec((1, tm, D_MODEL), lambda b, i: (b, i, 0)),
                  _mod_spec(per_token, tm),
                  pl.BlockSpec((1, D_MODEL), lambda b, i: (0, 0))],
        out_specs=pl.BlockSpec((1, tm, D_MODEL), lambda b, i: (b, i, 0)),
        out_shape=jax.ShapeDtypeStruct((bsz, t, D_MODEL), F32),
        compiler_params=_cparams(("arbitrary", "arbitrary")),
        name="final",
    )(slots, gates, x1, gt, g_post_ffn.reshape(1, D_MODEL))


def _moe(ht, gsel, w_gu, b_gu, w_down, b_down, tm, bm):
    n = gsel.shape[0]
    rank, counts = _rank(gsel, tm)
    cnt = counts[0].astype(jnp.int32)
    nblk = (cnt + bm - 1) // bm
    blk_end = jnp.cumsum(nblk)
    blk_start = blk_end - nblk
    n_blocks = -(-(n * TOP_K) // bm) + N_EXPERTS
    blocks = jnp.arange(n_blocks, dtype=jnp.int32)
    block_expert = jnp.minimum(jnp.sum((blk_end[None, :] <= blocks[:, None]).astype(jnp.int32), axis=1),
                               N_EXPERTS - 1).astype(jnp.int32)
    n_used = blk_end[-1:].astype(jnp.int32)
    row_start = (blk_start * bm).astype(jnp.int32)
    row_end = (blk_end * bm).astype(jnp.int32)
    dest, gates = _dest(gsel, rank, row_start.astype(F32).reshape(1, N_EXPERTS), tm)
    dest_flat = dest.reshape(n * TOP_K)
    inv = _inverse_map(dest_flat, cnt, row_start, row_end, n_used, n_blocks, tm, bm)
    ids = jnp.arange(N_EXPERTS, dtype=jnp.int32)
    later = jnp.where((nblk[None, :] > 0) & (ids[None, :] > ids[:, None]), ids[None, :], N_EXPERTS)
    nearest = jnp.min(later, axis=1)
    next_used = jnp.where(nearest < N_EXPERTS, nearest, ids).astype(jnp.int32)
    slots = _experts(ht, inv, n * TOP_K, block_expert, next_used[block_expert], n_used,
                     w_gu, b_gu, w_down, b_down, bm)
    return slots, gates


def _mix_path(x, mod, pool_buf, conv_buf, s0, start_pos, lp, tm, nb, per_token):
    (g_pre_mix, g_post_mix, g_pre_ffn, w_in_p, w_bd, pool_scale, conv_w, alog_row, dtb_row, norm_w,
     w_out_b, router_w, router_b) = lp
    bsz, t, _ = x.shape
    sh_mix, sc_mix, gt_mix, sh_ffn, sc_ffn, gt_ffn = (mod[:, i] for i in range(6))

    def shape_mod(m):
        if per_token:
            return jnp.repeat(m, t, axis=0).reshape(1, bsz * t, D_MODEL)
        return m.reshape(bsz, 1, D_MODEL)

    sh_mix, sc_mix, gt_mix, sh_ffn, sc_ffn, gt_ffn = map(shape_mod, (sh_mix, sc_mix, gt_mix, sh_ffn, sc_ffn, gt_ffn))
    xf = x.reshape(1, bsz * t, D_MODEL) if per_token else x
    proj = _in_proj(xf, sc_mix, sh_mix, g_pre_mix, w_in_p, tm, per_token).reshape(bsz, t, PROJ_WIDTH)
    tseq = min(tm, t)
    o_gdn, s_new = _gdn(proj, conv_buf, s0, conv_w, alog_row, dtb_row, norm_w, nb, tseq)
    pool_out = _pool(proj, pool_buf, w_bd, pool_scale, nb, tseq, start_pos)
    fl = (lambda a: a.reshape(1, bsz * t, a.shape[-1])) if per_token else (lambda a: a)
    x1, ht, gsel = _out_proj(xf, fl(pool_out), fl(o_gdn), gt_mix, sc_ffn, sh_ffn, g_post_mix, g_pre_ffn,
                             w_out_b, router_w, router_b, tm, per_token)
    def last_rows(buf, off, width):
        keep = buf.shape[1]
        if t >= keep:
            return proj[:, t - keep:, off:off + width]
        return jnp.concatenate([buf[:, t:], proj[:, :, off:off + width]], axis=1)

    pool_new = last_rows(pool_buf, OFF_U, POOL_WIDTH)
    conv_new = last_rows(conv_buf, OFF_QKV, CONV_CH)
    return x1, ht, gsel, gt_ffn, pool_new, conv_new, s_new


def kernel(x_prompt, x_sample, c_prompt, c_sample, state_pool, state_conv, state_ssm, w_ada, b_ada, g_pre_mix, g_post_mix, g_pre_ffn, g_post_ffn, w_in, w_pool, pool_scale, conv_w, a_log, dt_bias, gdn_norm_w, w_out, router_w, router_b, w_gu, b_gu, w_down, b_down):
    depth = w_ada.shape[0]
    bp, tp, _ = x_prompt.shape
    bs, ts, _ = x_sample.shape
    tm = 256
    y_p, y_s = x_prompt, x_sample
    outs = [[] for _ in range(6)]
    for l in range(depth):
        mod = _ada(jnp.concatenate([c_prompt, c_sample], axis=0), w_ada[l], b_ada[l]).reshape(bp + bs, 6, D_MODEL)
        o1, o2, o3 = POOL_WIDTH, POOL_WIDTH + CONV_CH, POOL_WIDTH + CONV_CH + GDN_WIDTH
        w_l = w_in[l]
        w_in_p = jnp.concatenate(
            [w_l[:, o1:o2], w_l[:, o2:o3], w_l[:, :o1], w_l[:, o3:],
             jnp.zeros((D_MODEL, LANES - (w_l.shape[1] - o3)), F32)], axis=1).astype(BF16)
        w_bd = jax.scipy.linalg.block_diag(*[w_pool[l, g] for g in range(len(POOL_WINDOWS))])
        pad_row = lambda v: jnp.pad(v, (0, LANES - v.shape[0])).reshape(1, LANES)
        lp = (g_pre_mix[l], g_post_mix[l], g_pre_ffn[l], w_in_p, w_bd, pool_scale[l], conv_w[l],
              pad_row(a_log[l]), pad_row(dt_bias[l]), gdn_norm_w[l].reshape(1, HEAD_DIM),
              w_out[l].astype(BF16), router_w[l], router_b[l])
        zero_pool = jnp.zeros((bp, POOL_BUF, POOL_WIDTH), F32)
        zero_conv = jnp.zeros((bp, CONV_WIDTH - 1, CONV_CH), F32)
        zero_ssm = jnp.zeros((bp, GDN_HEADS, HEAD_DIM, HEAD_DIM), F32)
        x1p, htp, gp, gtp, pp, cp, sp = _mix_path(y_p, mod[:bp], zero_pool, zero_conv, zero_ssm, 0, lp, tm, 1, False)
        x1s, hts, gs, gts, ps, cs, ss = _mix_path(y_s, mod[bp:], state_pool[l], state_conv[l], state_ssm[l],
                                                  PAST_LEN, lp, tm, 8, True)
        ht = jnp.concatenate([htp, hts], axis=0)
        gsel = jnp.concatenate([gp, gs], axis=0)
        slots, gates = _moe(ht, gsel, w_gu[l], b_gu[l], w_down[l], b_down[l], tm, MOE_BLOCK)
        y_p = _final(slots, gates, x1p, gtp, g_post_ffn[l], 0, tm, False)
        y_s = _final(slots, gates, x1s, gts, g_post_ffn[l], bp * tp, tm, True).reshape(bs, ts, D_MODEL)
        for lst, val in zip(outs, (pp, cp, sp, ps, cs, ss)):
            lst.append(val)
    return (y_p, y_s) + tuple(jnp.stack(o) for o in outs)
```

```python
import functools
import math

import jax
import jax.numpy as jnp
from jax import lax
from jax.experimental import pallas as pl
from jax.experimental.pallas import tpu as pltpu

F32 = jnp.float32
BF16 = jnp.bfloat16

D_MODEL = 1024
PAST_LEN = 16384
POOL_WINDOWS = (2, 4, 8, 16)
POOL_WIDTH = D_MODEL // 4
POOL_GROUP = POOL_WIDTH // len(POOL_WINDOWS)
POOL_BUF = max(POOL_WINDOWS) - 1
GDN_WIDTH = D_MODEL - POOL_WIDTH
HEAD_DIM = 128
GDN_HEADS = GDN_WIDTH // HEAD_DIM
CONV_WIDTH = 4
CONV_CH = 3 * GDN_WIDTH
GDN_CHUNK = 64
N_EXPERTS = 32
TOP_K = 4
D_FF = D_MODEL
SWIGLU_LIMIT = 7.0
SWIGLU_ALPHA = 1.702
EPS = 1e-6

SUBLANES = 8
LANES = 128
ROW_TILES = D_MODEL // LANES
OFF_QKV = 0
OFF_Z = OFF_QKV + CONV_CH
OFF_U = OFF_Z + GDN_WIDTH
OFF_AB = OFF_U + POOL_WIDTH
PROJ_WIDTH = OFF_AB + LANES
assert OFF_Z % GDN_WIDTH == 0 and OFF_U % POOL_WIDTH == 0 and OFF_AB % LANES == 0
TOKEN_TILE = 256
PROJ_TILE = 512
SCATTER_TILE = 512
MOE_BLOCK = 512
DMA_QUEUES = 2
VMEM_LIMIT = 56 * 1024 * 1024


def _cparams(sem):
    return pltpu.CompilerParams(dimension_semantics=sem, vmem_limit_bytes=VMEM_LIMIT)


def _dot(a, b):
    return jnp.dot(a.astype(BF16), b.astype(BF16), preferred_element_type=F32)


def _dot_nt(a, b):
    return lax.dot_general(a.astype(BF16), b.astype(BF16), (((1,), (1,)), ((), ())),
                           preferred_element_type=F32)


def _silu(x):
    return x * jax.nn.sigmoid(x)


def _rms(x, g):
    return x * lax.rsqrt(jnp.mean(x * x, axis=-1, keepdims=True) + EPS) * g


def _ada_kernel(c_ref, w_ref, b_ref, o_ref):
    o_ref[...] = _dot(_silu(c_ref[...]), w_ref[...]) + b_ref[...]


def _ada(c, w_ada, b_ada):
    n = c.shape[0]
    cols = w_ada.shape[1]
    blk = D_MODEL
    return pl.pallas_call(
        _ada_kernel,
        grid=(cols // blk,),
        in_specs=[pl.BlockSpec((n, D_MODEL), lambda j: (0, 0)),
                  pl.BlockSpec((D_MODEL, blk), lambda j: (0, j)),
                  pl.BlockSpec((1, blk), lambda j: (0, j))],
        out_specs=pl.BlockSpec((n, blk), lambda j: (0, j)),
        out_shape=jax.ShapeDtypeStruct((n, cols), F32),
        compiler_params=_cparams(("arbitrary",)),
        name="ada",
    )(c, w_ada, b_ada.reshape(1, cols))


def _in_kernel(x_ref, sc_ref, sh_ref, g_ref, w_ref, o_ref):
    h = _rms(x_ref[0], g_ref[...]) * (1.0 + sc_ref[0]) + sh_ref[0]
    o_ref[0] = jnp.dot(h.astype(BF16), w_ref[...], preferred_element_type=F32)


def _mod_spec(per_token, tm):
    if per_token:
        return pl.BlockSpec((1, tm, D_MODEL), lambda b, t: (0, t, 0))
    return pl.BlockSpec((1, 1, D_MODEL), lambda b, t: (b, 0, 0))


def _in_proj(x, sc, sh, g, w_in_p, tm, per_token):
    bsz, t, _ = x.shape
    return pl.pallas_call(
        _in_kernel,
        grid=(bsz, t // tm),
        in_specs=[pl.BlockSpec((1, tm, D_MODEL), lambda b, i: (b, i, 0)),
                  _mod_spec(per_token, tm), _mod_spec(per_token, tm),
                  pl.BlockSpec((1, D_MODEL), lambda b, i: (0, 0)),
                  pl.BlockSpec((D_MODEL, PROJ_WIDTH), lambda b, i: (0, 0))],
        out_specs=pl.BlockSpec((1, tm, PROJ_WIDTH), lambda b, i: (b, i, 0)),
        out_shape=jax.ShapeDtypeStruct((bsz, t, PROJ_WIDTH), F32),
        compiler_params=_cparams(("arbitrary", "arbitrary")),
        name="in_proj",
    )(x, sc, sh, g.reshape(1, D_MODEL), w_in_p)


def _shift_rows(prev, x, d):
    if d == 0:
        return x
    e = jnp.concatenate([prev, x], axis=0)
    return pltpu.roll(e, d, axis=0)[SUBLANES:]


def _cumsum_rows(x):
    n = x.shape[0]
    row = lax.broadcasted_iota(jnp.int32, x.shape, 0)
    s = 1
    while s < n:
        x = x + jnp.where(row >= s, pltpu.roll(x, s, axis=0), 0.0)
        s *= 2
    return x


def _gdn_kernel(qkv_ref, z_ref, ab_ref, cs_ref, cw_ref, s0_ref, alog_ref, dtb_ref, nw_ref, o_ref, sn_ref,
                s_scr, prev_scr, *, nb, tm, tmp, chunk):
    ti = pl.program_id(1)
    nt = pl.num_programs(1)
    n_chunks = tmp // chunk
    n_sq = int(math.log2(chunk)) - 1

    @pl.when(ti == 0)
    def _():
        s_scr[...] = s0_ref[...]
        for b in range(nb):
            prev_scr[b] = jnp.concatenate(
                [jnp.zeros((SUBLANES - (CONV_WIDTH - 1), CONV_CH), F32), cs_ref[b]], axis=0)

    rowi = lax.broadcasted_iota(jnp.int32, (tmp, LANES), 0)
    ri = lax.broadcasted_iota(jnp.int32, (chunk, chunk), 0)
    ci = lax.broadcasted_iota(jnp.int32, (chunk, chunk), 1)
    tril = ri >= ci
    strict = ri > ci
    eye = (ri == ci).astype(F32)
    neg_a = -jnp.exp(alog_ref[...])
    dtb = dtb_ref[...]
    nw = nw_ref[...]

    def padrows(x):
        if tm == tmp:
            return x
        return jnp.concatenate([x, jnp.zeros((tmp - tm, x.shape[1]), F32)], axis=0)

    cw = cw_ref[...]
    valid = rowi < tm
    items = []
    for b in range(nb):
        x = padrows(qkv_ref[b])
        prev = prev_scr[b]
        acc = x * cw[CONV_WIDTH - 1:CONV_WIDTH]
        for d in range(1, CONV_WIDTH):
            acc = acc + _shift_rows(prev, x, d) * cw[CONV_WIDTH - 1 - d:CONV_WIDTH - d]
        if tm >= SUBLANES:
            prev_scr[b] = x[tm - SUBLANES:tm]
        y = _silu(acc)
        ab = padrows(ab_ref[b])
        g_all = jnp.where(valid, neg_a * jax.nn.softplus(ab + dtb), 0.0)
        beta_all = jnp.where(valid, jax.nn.sigmoid(ab), 0.0)
        for h in range(GDN_HEADS):
            qc = y[:, h * HEAD_DIM:(h + 1) * HEAD_DIM]
            kc = y[:, GDN_WIDTH + h * HEAD_DIM:GDN_WIDTH + (h + 1) * HEAD_DIM]
            v = y[:, 2 * GDN_WIDTH + h * HEAD_DIM:2 * GDN_WIDTH + (h + 1) * HEAD_DIM]
            q = qc * lax.rsqrt(jnp.sum(qc * qc, axis=-1, keepdims=True) + EPS) * (HEAD_DIM ** -0.5)
            k = kc * lax.rsqrt(jnp.sum(kc * kc, axis=-1, keepdims=True) + EPS)
            gb = jnp.broadcast_to(g_all[:, h:h + 1], (tmp, LANES))
            bb = jnp.broadcast_to(beta_all[:, GDN_HEADS + h:GDN_HEADS + h + 1], (tmp, LANES))
            for c in range(n_chunks):
                sl = slice(c * chunk, (c + 1) * chunk)
                items.append(dict(b=b, h=h, c=c, q=q[sl], k=k[sl], v=v[sl], bb=bb[sl], gb=gb[sl]))

    for it in items:
        gcb = _cumsum_rows(it["gb"])
        gct = jnp.transpose(gcb)[:chunk]
        diff = gcb[:, :chunk] - gct
        it["decay"] = jnp.where(tril, jnp.exp(jnp.where(tril, diff, 0.0)), 0.0)
        it["kb"] = it["k"] * it["bb"]
        it["eg"] = jnp.exp(gcb)
        gl = gcb[chunk - 1:chunk]
        it["egl"] = jnp.exp(gl)
        it["kdt"] = jnp.transpose(it["k"] * jnp.exp(gl - gcb))
    for it in items:
        kq = _dot_nt(jnp.concatenate([it["kb"], it["q"]], axis=0), it["k"])
        it["a"] = jnp.where(strict, kq[:chunk] * it["decay"], 0.0)
        it["qk"] = jnp.where(tril, kq[chunk:] * it["decay"], 0.0)
    for it in items:
        it["inv"] = eye - it["a"]
        it["p"] = _dot(it["a"], it["a"])
    for lvl in range(n_sq):
        for it in items:
            if lvl < n_sq - 1:
                r = _dot(jnp.concatenate([it["inv"], it["p"]], axis=0), it["p"])
                it["inv"] = it["inv"] + r[:chunk]
                it["p"] = r[chunk:]
            else:
                it["inv"] = it["inv"] + _dot(it["inv"], it["p"])
    for it in items:
        it["wu"] = _dot(it["inv"], jnp.concatenate([it["kb"] * it["eg"], it["v"] * it["bb"]], axis=1))
    for it in items:
        mn = _dot(it["kdt"], it["wu"])
        qo = _dot(it["qk"], it["wu"])
        it["mp"], it["nc"] = mn[:, :HEAD_DIM], mn[:, HEAD_DIM:]
        it["qp"] = it["q"] * it["eg"] - qo[:, :HEAD_DIM]
        it["oc"] = qo[:, HEAD_DIM:]

    state = {(b, h): s_scr[b, h] for b in range(nb) for h in range(GDN_HEADS)}
    for c in range(n_chunks):
        for it in items:
            if it["c"] != c:
                continue
            s = state[(it["b"], it["h"])]
            r = _dot(jnp.concatenate([it["qp"], it["mp"]], axis=0), s)
            it["o"] = r[:chunk] + it["oc"]
            state[(it["b"], it["h"])] = s * it["egl"] - r[chunk:] + it["nc"]
    for (b, h), s in state.items():
        s_scr[b, h] = s

    for b in range(nb):
        zf = padrows(z_ref[b])
        for h in range(GDN_HEADS):
            outs = [it["o"] for it in items if it["b"] == b and it["h"] == h]
            o = outs[0] if n_chunks == 1 else jnp.concatenate(outs, axis=0)
            o = (o * lax.rsqrt(jnp.mean(o * o, axis=-1, keepdims=True) + EPS) * nw
                 * _silu(zf[:, h * HEAD_DIM:(h + 1) * HEAD_DIM]))
            o_ref[b, :, h * HEAD_DIM:(h + 1) * HEAD_DIM] = o[:tm]

    @pl.when(ti == nt - 1)
    def _():
        sn_ref[...] = s_scr[...]


def _gdn(proj, state_conv, s0, conv_w, alog_row, dtb_row, norm_w, nb, tm):
    bsz, t, _ = proj.shape
    tmp = max(tm, SUBLANES)
    chunk = min(GDN_CHUNK, tmp)
    row = pl.BlockSpec((1, LANES), lambda b, i: (0, 0))
    sspec = pl.BlockSpec((nb, GDN_HEADS, HEAD_DIM, HEAD_DIM), lambda b, i: (b, 0, 0, 0))
    kern = functools.partial(_gdn_kernel, nb=nb, tm=tm, tmp=tmp, chunk=chunk)
    return pl.pallas_call(
        kern,
        grid=(bsz // nb, t // tm),
        in_specs=[pl.BlockSpec((nb, tm, CONV_CH), lambda b, i: (b, i, OFF_QKV // CONV_CH)),
                  pl.BlockSpec((nb, tm, GDN_WIDTH), lambda b, i: (b, i, OFF_Z // GDN_WIDTH)),
                  pl.BlockSpec((nb, tm, LANES), lambda b, i: (b, i, OFF_AB // LANES)),
                  pl.BlockSpec((nb, CONV_WIDTH - 1, CONV_CH), lambda b, i: (b, 0, 0)),
                  pl.BlockSpec((CONV_WIDTH, CONV_CH), lambda b, i: (0, 0)),
                  sspec, row, row, row],
        out_specs=[pl.BlockSpec((nb, tm, GDN_WIDTH), lambda b, i: (b, i, 0)), sspec],
        out_shape=[jax.ShapeDtypeStruct((bsz, t, GDN_WIDTH), F32),
                   jax.ShapeDtypeStruct(s0.shape, F32)],
        scratch_shapes=[pltpu.VMEM((nb, GDN_HEADS, HEAD_DIM, HEAD_DIM), F32),
                        pltpu.VMEM((nb, SUBLANES, CONV_CH), F32)],
        compiler_params=_cparams(("arbitrary", "arbitrary")),
        name="gdn",
    )(proj, proj, proj, state_conv, conv_w, s0, alog_row, dtb_row, norm_w)


def _pool_kernel(u_ref, st_ref, w_ref, sc_ref, o_ref, prev_scr, *, nb, tm, tmp, start_pos):
    ti = pl.program_id(1)
    halo = 2 * SUBLANES

    @pl.when(ti == 0)
    def _():
        for b in range(nb):
            prev_scr[b] = jnp.concatenate([jnp.zeros((halo - POOL_BUF, POOL_WIDTH), F32), st_ref[b]], axis=0)

    lane = lax.broadcasted_iota(jnp.int32, (tmp, POOL_WIDTH), 1)
    rowi = lax.broadcasted_iota(jnp.int32, (tmp, POOL_WIDTH), 0)
    pos1 = start_pos + ti * tm + rowi + 1
    grp = lane // POOL_GROUP
    win = jnp.where(grp == 0, POOL_WINDOWS[0],
                    jnp.where(grp == 1, POOL_WINDOWS[1], jnp.where(grp == 2, POOL_WINDOWS[2], POOL_WINDOWS[3])))
    cnt = jnp.minimum(pos1, win).astype(F32)
    for b in range(nb):
        x = u_ref[b]
        if tm != tmp:
            x = jnp.concatenate([x, jnp.zeros((tmp - tm, POOL_WIDTH), F32)], axis=0)
        e = jnp.concatenate([prev_scr[b], x], axis=0)
        sums = []
        s = e
        for w in POOL_WINDOWS:
            s = s + pltpu.roll(s, w // 2, axis=0)
            sums.append(s[halo:])
        sel = jnp.where(grp == 0, sums[0], jnp.where(grp == 1, sums[1], jnp.where(grp == 2, sums[2], sums[3])))
        pooled = sel / cnt - x
        out = _dot(pooled, w_ref[...]) * sc_ref[...]
        o_ref[b] = out[:tm]
        if tm >= halo:
            prev_scr[b] = x[tm - halo:tm]


def _pool(proj, state_pool, w_bd, pool_scale, nb, tm, start_pos):
    bsz, t, _ = proj.shape
    tmp = max(tm, SUBLANES)
    kern = functools.partial(_pool_kernel, nb=nb, tm=tm, tmp=tmp, start_pos=start_pos)
    return pl.pallas_call(
        kern,
        grid=(bsz // nb, t // tm),
        in_specs=[pl.BlockSpec((nb, tm, POOL_WIDTH), lambda b, i: (b, i, OFF_U // POOL_WIDTH)),
                  pl.BlockSpec((nb, POOL_BUF, POOL_WIDTH), lambda b, i: (b, 0, 0)),
                  pl.BlockSpec((POOL_WIDTH, POOL_WIDTH), lambda b, i: (0, 0)),
                  pl.BlockSpec((1, POOL_WIDTH), lambda b, i: (0, 0))],
        out_specs=pl.BlockSpec((nb, tm, POOL_WIDTH), lambda b, i: (b, i, 0)),
        out_shape=jax.ShapeDtypeStruct((bsz, t, POOL_WIDTH), F32),
        scratch_shapes=[pltpu.VMEM((nb, 2 * SUBLANES, POOL_WIDTH), F32)],
        compiler_params=_cparams(("arbitrary", "arbitrary")),
        name="pool",
    )(proj, state_pool, w_bd, pool_scale.reshape(1, POOL_WIDTH))


def _out_kernel(x_ref, po_ref, og_ref, gt_ref, sc_ref, sh_ref, gpm_ref, gpf_ref, wo_ref, rw_ref, rb_ref,
                x1_ref, ht_ref, g_ref, *, tm):
    mix = _dot(po_ref[0], wo_ref[:POOL_WIDTH]) + _dot(og_ref[0], wo_ref[POOL_WIDTH:])
    x1 = x_ref[0] + gt_ref[0] * _rms(mix, gpm_ref[...])
    x1_ref[0] = x1
    h = _rms(x1, gpf_ref[...]) * (1.0 + sc_ref[0]) + sh_ref[0]
    for s in range(ROW_TILES):
        ht_ref[pl.ds(s, tm, stride=ROW_TILES), :] = h[:, s * LANES:(s + 1) * LANES]
    logits = _dot(h, rw_ref[...]) + rb_ref[...]
    lane = lax.broadcasted_iota(jnp.int32, logits.shape, 1)
    work = logits
    sel = jnp.zeros(logits.shape, jnp.bool_)
    top = None
    den = jnp.zeros((tm, 1), F32)
    for j in range(TOP_K):
        m = jnp.max(work, axis=-1, keepdims=True)
        idx = jnp.min(jnp.where(work == m, lane, N_EXPERTS), axis=-1, keepdims=True)
        hit = lane == idx
        sel = sel | hit
        work = jnp.where(hit, -jnp.inf, work)
        if j == 0:
            top = m
        den = den + jnp.exp(m - top)
    gates = jnp.exp(logits - top) / den
    g_ref[...] = jnp.where(sel, gates, -1.0)


def _out_proj(x, pool_out, o_gdn, gt, sc, sh, g_post_mix, g_pre_ffn, w_out_b, router_w, router_b, tm, per_token):
    bsz, t, _ = x.shape
    n = bsz * t
    tiles = t // tm

    def tok(width):
        return pl.BlockSpec((1, tm, width), lambda b, i: (b, i, 0))

    def const(shape):
        return pl.BlockSpec(shape, lambda b, i: tuple(0 for _ in shape))

    kern = functools.partial(_out_kernel, tm=tm)
    return pl.pallas_call(
        kern,
        grid=(bsz, tiles),
        in_specs=[tok(D_MODEL), tok(POOL_WIDTH), tok(GDN_WIDTH),
                  _mod_spec(per_token, tm), _mod_spec(per_token, tm), _mod_spec(per_token, tm),
                  const((1, D_MODEL)), const((1, D_MODEL)), const((D_MODEL, D_MODEL)),
                  const((D_MODEL, N_EXPERTS)), const((1, N_EXPERTS))],
        out_specs=[tok(D_MODEL),
                   pl.BlockSpec((tm * ROW_TILES, LANES), lambda b, i: (b * tiles + i, 0)),
                   pl.BlockSpec((tm, N_EXPERTS), lambda b, i: (b * tiles + i, 0))],
        out_shape=[jax.ShapeDtypeStruct((bsz, t, D_MODEL), F32),
                   jax.ShapeDtypeStruct((n * ROW_TILES, LANES), F32),
                   jax.ShapeDtypeStruct((n, N_EXPERTS), F32)],
        compiler_params=_cparams(("arbitrary", "arbitrary")),
        name="out_proj",
    )(x, pool_out, o_gdn, gt, sc, sh, g_post_mix.reshape(1, D_MODEL), g_pre_ffn.reshape(1, D_MODEL),
      w_out_b, router_w, router_b.reshape(1, N_EXPERTS))


def _rank_kernel(g_ref, r_ref, cnt_ref, carry, *, tm):
    i = pl.program_id(0)

    @pl.when(i == 0)
    def _():
        carry[...] = jnp.zeros_like(carry)

    mask = (g_ref[...] >= 0.0).astype(F32)
    ri = lax.broadcasted_iota(jnp.int32, (tm, tm), 0)
    ci = lax.broadcasted_iota(jnp.int32, (tm, tm), 1)
    before = (ri > ci).astype(F32)
    c = carry[...]
    r_ref[...] = _dot(before, mask) + c
    c = c + jnp.sum(mask, axis=0, keepdims=True)
    carry[...] = c
    cnt_ref[...] = c


def _rank(gsel, tm):
    n = gsel.shape[0]
    kern = functools.partial(_rank_kernel, tm=tm)
    return pl.pallas_call(
        kern,
        grid=(n // tm,),
        in_specs=[pl.BlockSpec((tm, N_EXPERTS), lambda i: (i, 0))],
        out_specs=[pl.BlockSpec((tm, N_EXPERTS), lambda i: (i, 0)),
                   pl.BlockSpec((1, N_EXPERTS), lambda i: (0, 0))],
        out_shape=[jax.ShapeDtypeStruct((n, N_EXPERTS), F32),
                   jax.ShapeDtypeStruct((1, N_EXPERTS), F32)],
        scratch_shapes=[pltpu.VMEM((1, N_EXPERTS), F32)],
        compiler_params=_cparams(("arbitrary",)),
        name="rank",
    )(gsel)


def _dest_kernel(g_ref, r_ref, start_ref, d_ref, gt_ref, *, tm):
    g = g_ref[...]
    mask = g >= 0.0
    maskf = mask.astype(F32)
    dest = start_ref[...] + r_ref[...]
    ri = lax.broadcasted_iota(jnp.int32, (N_EXPERTS, N_EXPERTS), 0)
    ci = lax.broadcasted_iota(jnp.int32, (N_EXPERTS, N_EXPERTS), 1)
    slot = _dot(maskf, (ri < ci).astype(F32))
    lane4 = lax.broadcasted_iota(jnp.int32, (tm, TOP_K), 1)
    d_out = jnp.zeros((tm, TOP_K), F32)
    g_out = jnp.zeros((tm, TOP_K), F32)
    for j in range(TOP_K):
        pick = mask & (slot == float(j))
        dj = jnp.sum(jnp.where(pick, dest, 0.0), axis=-1, keepdims=True)
        gj = jnp.sum(jnp.where(pick, g, 0.0), axis=-1, keepdims=True)
        d_out = jnp.where(lane4 == j, dj, d_out)
        g_out = jnp.where(lane4 == j, gj, g_out)
    d_ref[...] = d_out.astype(jnp.int32)
    gt_ref[...] = g_out


def _dest(gsel, rank, start_row, tm):
    n = gsel.shape[0]
    kern = functools.partial(_dest_kernel, tm=tm)
    return pl.pallas_call(
        kern,
        grid=(n // tm,),
        in_specs=[pl.BlockSpec((tm, N_EXPERTS), lambda i: (i, 0)),
                  pl.BlockSpec((tm, N_EXPERTS), lambda i: (i, 0)),
                  pl.BlockSpec((1, N_EXPERTS), lambda i: (0, 0))],
        out_specs=[pl.BlockSpec((tm, TOP_K), lambda i: (i, 0)),
                   pl.BlockSpec((tm, TOP_K), lambda i: (i, 0))],
        out_shape=[jax.ShapeDtypeStruct((n, TOP_K), jnp.int32),
                   jax.ShapeDtypeStruct((n, TOP_K), F32)],
        compiler_params=_cparams(("arbitrary",)),
        name="dest",
    )(gsel, rank, start_row)


def _row_copy(src, dst, s_row, d_row, sem):
    return pltpu.make_async_copy(src.at[pl.ds(pl.multiple_of(s_row * ROW_TILES, ROW_TILES), ROW_TILES), :],
                                 dst.at[pl.ds(pl.multiple_of(d_row * ROW_TILES, ROW_TILES), ROW_TILES), :], sem)


def _scatter_kernel(cnt_ref, start_ref, end_ref, nu_ref, dest_ref, ht_ref, xs_ref, sem, *, tm, bm, n_blocks):
    i = pl.program_id(0)

    def body(t, carry):
        src = ht_ref.at[pl.ds(pl.multiple_of(t * ROW_TILES, ROW_TILES), ROW_TILES), :]
        for k in range(TOP_K):
            d = dest_ref[t * TOP_K + k]
            pltpu.make_async_copy(src, xs_ref.at[pl.ds(pl.multiple_of(d * ROW_TILES, ROW_TILES), ROW_TILES), :],
                                  sem).start(priority=k % DMA_QUEUES)
        return carry

    lax.fori_loop(0, tm, body, 0)

    def tile_copy(u):
        return pltpu.make_async_copy(
            ht_ref, xs_ref.at[pl.ds(pl.multiple_of(u * (tm * ROW_TILES), tm * ROW_TILES), tm * ROW_TILES), :], sem)

    for k in range(TOP_K):
        tile_copy(0).wait()

    @pl.when(i == pl.num_programs(0) - 1)
    def _():
        per = bm // tm

        def tail_start(u, carry):
            tile_copy(u).start()
            return carry

        def tail_wait(u, carry):
            tile_copy(u).wait()
            return carry

        lax.fori_loop(nu_ref[0] * per, n_blocks * per, tail_start, 0)
        lax.fori_loop(nu_ref[0] * per, n_blocks * per, tail_wait, 0)

        def per_expert(e, carry):
            lo = start_ref[e] + cnt_ref[e]
            hi = end_ref[e]

            def fill(r, c2):
                _row_copy(ht_ref, xs_ref, 0, r, sem).start()
                return c2

            lax.fori_loop(lo, hi, fill, 0)

            def drain(r, c2):
                _row_copy(ht_ref, xs_ref, 0, r, sem).wait()
                return c2

            lax.fori_loop(lo, hi, drain, 0)
            return carry

        lax.fori_loop(0, N_EXPERTS, per_expert, 0)


def _scatter_rows(ht, dest_flat, cnt, start, end, n_used, n_blocks, tm, bm):
    n = ht.shape[0] // ROW_TILES
    rows = n_blocks * bm
    assert bm % tm == 0 and n % tm == 0
    kern = functools.partial(_scatter_kernel, tm=tm, bm=bm, n_blocks=n_blocks)
    return pl.pallas_call(
        kern,
        grid_spec=pltpu.PrefetchScalarGridSpec(
            num_scalar_prefetch=4,
            grid=(n // tm,),
            in_specs=[pl.BlockSpec((tm * TOP_K,), lambda i, *_: (i,), memory_space=pltpu.SMEM),
                      pl.BlockSpec((tm * ROW_TILES, LANES), lambda i, *_: (i, 0))],
            out_specs=pl.BlockSpec(memory_space=pl.ANY),
            scratch_shapes=[pltpu.SemaphoreType.DMA(())]),
        out_shape=jax.ShapeDtypeStruct((rows * ROW_TILES, LANES), F32),
        compiler_params=_cparams(("arbitrary",)),
        name="scatter_rows",
    )(cnt, start, end, n_used, dest_flat, ht)


def _expert_kernel(be_ref, nx_ref, nu_ref, xs_ref, wg_ref, bg_ref, wd_ref, bd_ref, ys_ref,
                   wg_f, wd_f, wg_s, wd_s, wsem, *, bm):
    j = pl.program_id(0)
    prev = be_ref[jnp.maximum(j - 1, 0)]
    fresh = (j == 0) | (be_ref[j] != prev)
    active = j < nu_ref[0]

    def weight_copies(e):
        return (pltpu.make_async_copy(wg_ref.at[e], wg_f, wsem.at[0]),
                pltpu.make_async_copy(wd_ref.at[e], wd_f, wsem.at[1]))

    @pl.when(j == 0)
    def _():
        for c in weight_copies(be_ref[0]):
            c.start()

    @pl.when(active & fresh)
    def _():
        for c in weight_copies(be_ref[j]):
            c.wait()
        wg_s[...] = wg_f[...].astype(BF16)
        wd_s[...] = wd_f[...].astype(BF16)

        @pl.when(nx_ref[j] != be_ref[j])
        def _():
            for c in weight_copies(nx_ref[j]):
                c.start()

    @pl.when(active)
    def _():
        x = jnp.concatenate([xs_ref[pl.ds(s, bm, stride=ROW_TILES), :] for s in range(ROW_TILES)], axis=-1)
        xb = x.astype(BF16)
        gate = jnp.dot(xb, wg_s[:, :D_FF], preferred_element_type=F32) + bg_ref[0, :, :D_FF]
        up = jnp.dot(xb, wg_s[:, D_FF:], preferred_element_type=F32) + bg_ref[0, :, D_FF:]
        gate = jnp.minimum(gate, SWIGLU_LIMIT)
        up = jnp.clip(up, -SWIGLU_LIMIT, SWIGLU_LIMIT)
        act = (up + 1.0) * (gate * jax.nn.sigmoid(SWIGLU_ALPHA * gate))
        y = jnp.dot(act.astype(BF16), wd_s[...], preferred_element_type=F32) + bd_ref[0]
        for s in range(ROW_TILES):
            ys_ref[pl.ds(s, bm, stride=ROW_TILES), :] = y[:, s * LANES:(s + 1) * LANES]

    @pl.when(jnp.logical_not(active))
    def _():
        ys_ref[...] = jnp.zeros_like(ys_ref)


def _experts(xs, block_expert, next_expert, n_used, w_gu, b_gu, w_down, b_down, bm):
    rows = xs.shape[0] // ROW_TILES
    n_blocks = rows // bm

    def blk(j, be, nx, nu):
        return (jnp.minimum(j, nu[0] - 1), 0)

    def bsel(j, be, nx, nu):
        return (be[j], 0, 0)

    hbm = pl.BlockSpec(memory_space=pl.ANY)
    kern = functools.partial(_expert_kernel, bm=bm)
    return pl.pallas_call(
        kern,
        grid_spec=pltpu.PrefetchScalarGridSpec(
            num_scalar_prefetch=3,
            grid=(n_blocks,),
            in_specs=[pl.BlockSpec((bm * ROW_TILES, LANES), blk), hbm,
                      pl.BlockSpec((1, 1, 2 * D_FF), bsel), hbm,
                      pl.BlockSpec((1, 1, D_MODEL), bsel)],
            out_specs=pl.BlockSpec((bm * ROW_TILES, LANES), lambda j, be, nx, nu: (j, 0)),
            scratch_shapes=[pltpu.VMEM((D_MODEL, 2 * D_FF), F32),
                            pltpu.VMEM((D_FF, D_MODEL), F32),
                            pltpu.VMEM((D_MODEL, 2 * D_FF), BF16),
                            pltpu.VMEM((D_FF, D_MODEL), BF16),
                            pltpu.SemaphoreType.DMA((2,))]),
        out_shape=jax.ShapeDtypeStruct((rows * ROW_TILES, LANES), F32),
        compiler_params=_cparams(("arbitrary",)),
        name="experts",
    )(block_expert, next_expert, n_used, xs, w_gu, b_gu.reshape(N_EXPERTS, 1, 2 * D_FF), w_down,
      b_down.reshape(N_EXPERTS, 1, D_MODEL))


def _final_kernel(dcur_ref, dnext_ref, ys_ref, gate_ref, x1_ref, gt_ref, g_ref, o_ref, sl_ref, sem, *, tm, steps):
    g = pl.program_id(0) * pl.num_programs(1) + pl.program_id(1)
    cur = g % 2

    def request(d_ref, slot):
        def body(t, carry):
            for k in range(TOP_K):
                j = t * TOP_K + k
                pltpu.make_async_copy(
                    ys_ref.at[pl.ds(pl.multiple_of(d_ref[j] * ROW_TILES, ROW_TILES), ROW_TILES), :],
                    sl_ref.at[slot, pl.ds(pl.multiple_of(j * ROW_TILES, ROW_TILES), ROW_TILES), :],
                    sem.at[slot]).start(priority=k % DMA_QUEUES)
            return carry

        lax.fori_loop(0, tm, body, 0)

    @pl.when(g == 0)
    def _():
        request(dcur_ref, 0)

    @pl.when(g + 1 < steps)
    def _():
        request(dnext_ref, 1 - cur)

    pltpu.make_async_copy(ys_ref.at[pl.ds(0, tm * TOP_K * ROW_TILES), :], sl_ref.at[cur], sem.at[cur]).wait()

    gates = gate_ref[...]
    per_tok = TOP_K * ROW_TILES
    cols = []
    for s in range(ROW_TILES):
        acc = None
        for j in range(TOP_K):
            part = sl_ref[cur, pl.ds(j * ROW_TILES + s, tm, stride=per_tok), :] * gates[:, j:j + 1]
            acc = part if acc is None else acc + part
        cols.append(acc)
    f = jnp.concatenate(cols, axis=-1)
    o_ref[0] = x1_ref[0] + gt_ref[0] * _rms(f, g_ref[...])


def _final(ys, dest_flat, gates, x1, gt, g_post_ffn, tok_off, tm, per_token):
    bsz, t, _ = x1.shape
    tiles = t // tm
    off = tok_off // tm
    steps = bsz * tiles
    kern = functools.partial(_final_kernel, tm=tm, steps=steps)

    def dspec(ahead):
        return pl.BlockSpec((tm * TOP_K,), lambda b, i: (off + jnp.minimum(b * tiles + i + ahead, steps - 1),),
                            memory_space=pltpu.SMEM)

    return pl.pallas_call(
        kern,
        grid=(bsz, tiles),
        in_specs=[dspec(0), dspec(1),
                  pl.BlockSpec(memory_space=pl.ANY),
                  pl.BlockSpec((tm, TOP_K), lambda b, i: (off + b * tiles + i, 0)),
                  pl.BlockSpec((1, tm, D_MODEL), lambda b, i: (b, i, 0)),
                  _mod_spec(per_token, tm),
                  pl.BlockSpec((1, D_MODEL), lambda b, i: (0, 0))],
        out_specs=pl.BlockSpec((1, tm, D_MODEL), lambda b, i: (b, i, 0)),
        out_shape=jax.ShapeDtypeStruct((bsz, t, D_MODEL), F32),
        scratch_shapes=[pltpu.VMEM((2, tm * TOP_K * ROW_TILES, LANES), F32), pltpu.SemaphoreType.DMA((2,))],
        compiler_params=_cparams(("arbitrary", "arbitrary")),
        name="final",
    )(dest_flat, dest_flat, ys, gates, x1, gt, g_post_ffn.reshape(1, D_MODEL))


def _moe(ht, gsel, w_gu, b_gu, w_down, b_down, tm, bm):
    n = gsel.shape[0]
    rank, counts = _rank(gsel, tm)
    cnt = counts[0].astype(jnp.int32)
    nblk = (cnt + bm - 1) // bm
    blk_end = jnp.cumsum(nblk)
    blk_start = blk_end - nblk
    n_blocks = -(-(n * TOP_K) // bm) + N_EXPERTS
    blocks = jnp.arange(n_blocks, dtype=jnp.int32)
    block_expert = jnp.minimum(jnp.sum((blk_end[None, :] <= blocks[:, None]).astype(jnp.int32), axis=1),
                               N_EXPERTS - 1).astype(jnp.int32)
    ids = jnp.arange(N_EXPERTS, dtype=jnp.int32)
    later = jnp.where((nblk[None, :] > 0) & (ids[None, :] > ids[:, None]), ids[None, :], N_EXPERTS)
    nearest = jnp.min(later, axis=1)
    next_used = jnp.where(nearest < N_EXPERTS, nearest, ids).astype(jnp.int32)
    n_used = blk_end[-1:].astype(jnp.int32)
    row_start = (blk_start * bm).astype(jnp.int32)
    row_end = (blk_end * bm).astype(jnp.int32)
    dest, gates = _dest(gsel, rank, row_start.astype(F32).reshape(1, N_EXPERTS), tm)
    dest_flat = dest.reshape(n * TOP_K)
    xs = _scatter_rows(ht, dest_flat, cnt, row_start, row_end, n_used, n_blocks, SCATTER_TILE, bm)
    ys = _experts(xs, block_expert, next_used[block_expert], n_used, w_gu, b_gu, w_down, b_down, bm)
    return ys, dest_flat, gates


def _mix_path(x, mod, pool_buf, conv_buf, s0, start_pos, lp, tm, nb, per_token):
    (g_pre_mix, g_post_mix, g_pre_ffn, w_in_p, w_bd, pool_scale, conv_w, alog_row, dtb_row, norm_w,
     w_out_b, router_w, router_b) = lp
    bsz, t, _ = x.shape
    sh_mix, sc_mix, gt_mix, sh_ffn, sc_ffn, gt_ffn = (mod[:, i] for i in range(6))

    def shape_mod(m):
        if per_token:
            return jnp.repeat(m, t, axis=0).reshape(1, bsz * t, D_MODEL)
        return m.reshape(bsz, 1, D_MODEL)

    sh_mix, sc_mix, gt_mix, sh_ffn, sc_ffn, gt_ffn = map(shape_mod, (sh_mix, sc_mix, gt_mix, sh_ffn, sc_ffn, gt_ffn))
    xf = x.reshape(1, bsz * t, D_MODEL) if per_token else x
    proj = _in_proj(xf, sc_mix, sh_mix, g_pre_mix, w_in_p, PROJ_TILE, per_token).reshape(bsz, t, PROJ_WIDTH)
    tseq = min(tm, t)
    o_gdn, s_new = _gdn(proj, conv_buf, s0, conv_w, alog_row, dtb_row, norm_w, nb, tseq)
    pool_out = _pool(proj, pool_buf, w_bd, pool_scale, nb, tseq, start_pos)
    fl = (lambda a: a.reshape(1, bsz * t, a.shape[-1])) if per_token else (lambda a: a)
    x1, ht, gsel = _out_proj(xf, fl(pool_out), fl(o_gdn), gt_mix, sc_ffn, sh_ffn, g_post_mix, g_pre_ffn,
                             w_out_b, router_w, router_b, tm, per_token)

    def last_rows(buf, off, width):
        keep = buf.shape[1]
        if t >= keep:
            return proj[:, t - keep:, off:off + width]
        return jnp.concatenate([buf[:, t:], proj[:, :, off:off + width]], axis=1)

    pool_new = last_rows(pool_buf, OFF_U, POOL_WIDTH)
    conv_new = last_rows(conv_buf, OFF_QKV, CONV_CH)
    return x1, ht, gsel, gt_ffn, pool_new, conv_new, s_new


def kernel(x_prompt, x_sample, c_prompt, c_sample, state_pool, state_conv, state_ssm, w_ada, b_ada, g_pre_mix, g_post_mix, g_pre_ffn, g_post_ffn, w_in, w_pool, pool_scale, conv_w, a_log, dt_bias, gdn_norm_w, w_out, router_w, router_b, w_gu, b_gu, w_down, b_down):
    depth = w_ada.shape[0]
    bp, tp, _ = x_prompt.shape
    bs, ts, _ = x_sample.shape
    tm = TOKEN_TILE
    y_p, y_s = x_prompt, x_sample
    outs = [[] for _ in range(6)]
    for l in range(depth):
        mod = _ada(jnp.concatenate([c_prompt, c_sample], axis=0), w_ada[l], b_ada[l]).reshape(bp + bs, 6, D_MODEL)
        o1, o2, o3 = POOL_WIDTH, POOL_WIDTH + CONV_CH, POOL_WIDTH + CONV_CH + GDN_WIDTH
        w_l = w_in[l]
        w_in_p = jnp.concatenate(
            [w_l[:, o1:o2], w_l[:, o2:o3], w_l[:, :o1], w_l[:, o3:],
             jnp.zeros((D_MODEL, LANES - (w_l.shape[1] - o3)), F32)], axis=1).astype(BF16)
        w_bd = jax.scipy.linalg.block_diag(*[w_pool[l, g] for g in range(len(POOL_WINDOWS))])
        pad_row = lambda v: jnp.pad(v, (0, LANES - v.shape[0])).reshape(1, LANES)
        lp = (g_pre_mix[l], g_post_mix[l], g_pre_ffn[l], w_in_p, w_bd, pool_scale[l], conv_w[l],
              pad_row(a_log[l]), pad_row(dt_bias[l]), gdn_norm_w[l].reshape(1, HEAD_DIM),
              w_out[l].astype(BF16), router_w[l], router_b[l])
        zero_pool = jnp.zeros((bp, POOL_BUF, POOL_WIDTH), F32)
        zero_conv = jnp.zeros((bp, CONV_WIDTH - 1, CONV_CH), F32)
        zero_ssm = jnp.zeros((bp, GDN_HEADS, HEAD_DIM, HEAD_DIM), F32)
        x1p, htp, gp, gtp, pp, cp, sp = _mix_path(y_p, mod[:bp], zero_pool, zero_conv, zero_ssm, 0, lp, tm, 1, False)
        x1s, hts, gs, gts, ps, cs, ss = _mix_path(y_s, mod[bp:], state_pool[l], state_conv[l], state_ssm[l],
                                                  PAST_LEN, lp, tm, 8, True)
        ht = jnp.concatenate([htp, hts], axis=0)
        gsel = jnp.concatenate([gp, gs], axis=0)
        ys, dest_flat, gates = _moe(ht, gsel, w_gu[l], b_gu[l], w_down[l], b_down[l], tm, MOE_BLOCK)
        y_p = _final(ys, dest_flat, gates, x1p, gtp, g_post_ffn[l], 0, tm, False)
        y_s = _final(ys, dest_flat, gates, x1s, gts, g_post_ffn[l], bp * tp, tm, True).reshape(bs, ts, D_MODEL)
        for lst, val in zip(outs, (pp, cp, sp, ps, cs, ss)):
            lst.append(val)
    return (y_p, y_s) + tuple(jnp.stack(o) for o in outs)
```

```python
import functools
import math

import jax
import jax.numpy as jnp
from jax import lax
from jax.experimental import pallas as pl
from jax.experimental.pallas import tpu as pltpu

F32 = jnp.float32
BF16 = jnp.bfloat16

D_MODEL = 1024
PAST_LEN = 16384
POOL_WINDOWS = (2, 4, 8, 16)
POOL_WIDTH = D_MODEL // 4
POOL_GROUP = POOL_WIDTH // len(POOL_WINDOWS)
POOL_BUF = max(POOL_WINDOWS) - 1
GDN_WIDTH = D_MODEL - POOL_WIDTH
HEAD_DIM = 128
GDN_HEADS = GDN_WIDTH // HEAD_DIM
CONV_WIDTH = 4
CONV_CH = 3 * GDN_WIDTH
GDN_CHUNK = 64
N_EXPERTS = 32
TOP_K = 4
D_FF = D_MODEL
SWIGLU_LIMIT = 7.0
SWIGLU_ALPHA = 1.702
EPS = 1e-6

SUBLANES = 8
LANES = 128
ROW_TILES = D_MODEL // LANES
OFF_QKV = 0
OFF_Z = OFF_QKV + CONV_CH
OFF_U = OFF_Z + GDN_WIDTH
OFF_AB = OFF_U + POOL_WIDTH
PROJ_WIDTH = OFF_AB + LANES
assert OFF_Z % GDN_WIDTH == 0 and OFF_U % POOL_WIDTH == 0 and OFF_AB % LANES == 0
TOKEN_TILE = 256
PROJ_TILE = 512
SCATTER_TILE = 512
DEST_TILE = 512
MOE_BLOCK = 512
DMA_QUEUES = 2
VMEM_LIMIT = 56 * 1024 * 1024


def _cparams(sem):
    return pltpu.CompilerParams(dimension_semantics=sem, vmem_limit_bytes=VMEM_LIMIT)


def _dot(a, b):
    return jnp.dot(a.astype(BF16), b.astype(BF16), preferred_element_type=F32)


def _dot_nt(a, b):
    return lax.dot_general(a.astype(BF16), b.astype(BF16), (((1,), (1,)), ((), ())),
                           preferred_element_type=F32)


def _silu(x):
    return x * jax.nn.sigmoid(x)


def _rms(x, g):
    return x * lax.rsqrt(jnp.mean(x * x, axis=-1, keepdims=True) + EPS) * g


def _ada_kernel(c_ref, w_ref, b_ref, o_ref):
    o_ref[...] = _dot(_silu(c_ref[...]), w_ref[...]) + b_ref[...]


def _ada(c, w_ada, b_ada):
    n = c.shape[0]
    cols = w_ada.shape[1]
    blk = D_MODEL
    return pl.pallas_call(
        _ada_kernel,
        grid=(cols // blk,),
        in_specs=[pl.BlockSpec((n, D_MODEL), lambda j: (0, 0)),
                  pl.BlockSpec((D_MODEL, blk), lambda j: (0, j)),
                  pl.BlockSpec((1, blk), lambda j: (0, j))],
        out_specs=pl.BlockSpec((n, blk), lambda j: (0, j)),
        out_shape=jax.ShapeDtypeStruct((n, cols), F32),
        compiler_params=_cparams(("arbitrary",)),
        name="ada",
    )(c, w_ada, b_ada.reshape(1, cols))


def _in_kernel(x_ref, sc_ref, sh_ref, g_ref, w_ref, o_ref, wb):
    @pl.when((pl.program_id(0) == 0) & (pl.program_id(1) == 0))
    def _():
        o1, o2, o3 = POOL_WIDTH, POOL_WIDTH + CONV_CH, POOL_WIDTH + CONV_CH + GDN_WIDTH
        wb[:, OFF_QKV:OFF_QKV + CONV_CH] = w_ref[:, o1:o2].astype(BF16)
        wb[:, OFF_Z:OFF_Z + GDN_WIDTH] = w_ref[:, o2:o3].astype(BF16)
        wb[:, OFF_U:OFF_U + POOL_WIDTH] = w_ref[:, :o1].astype(BF16)
        tail = w_ref.shape[1] - o3
        wb[:, OFF_AB:] = jnp.concatenate(
            [w_ref[:, o3:], jnp.zeros((D_MODEL, LANES - tail), F32)], axis=1).astype(BF16)

    h = _rms(x_ref[0], g_ref[...]) * (1.0 + sc_ref[0]) + sh_ref[0]
    o_ref[0] = jnp.dot(h.astype(BF16), wb[...], preferred_element_type=F32)


def _mod_spec(per_token, tm):
    if per_token:
        return pl.BlockSpec((1, tm, D_MODEL), lambda b, t: (0, t, 0))
    return pl.BlockSpec((1, 1, D_MODEL), lambda b, t: (b, 0, 0))


def _in_proj(x, sc, sh, g, w_in, tm, per_token):
    bsz, t, _ = x.shape
    return pl.pallas_call(
        _in_kernel,
        grid=(bsz, t // tm),
        in_specs=[pl.BlockSpec((1, tm, D_MODEL), lambda b, i: (b, i, 0)),
                  _mod_spec(per_token, tm), _mod_spec(per_token, tm),
                  pl.BlockSpec((1, D_MODEL), lambda b, i: (0, 0)),
                  pl.BlockSpec(w_in.shape, lambda b, i: (0, 0), pipeline_mode=pl.Buffered(1))],
        out_specs=pl.BlockSpec((1, tm, PROJ_WIDTH), lambda b, i: (b, i, 0)),
        out_shape=jax.ShapeDtypeStruct((bsz, t, PROJ_WIDTH), F32),
        scratch_shapes=[pltpu.VMEM((D_MODEL, PROJ_WIDTH), BF16)],
        compiler_params=_cparams(("arbitrary", "arbitrary")),
        name="in_proj",
    )(x, sc, sh, g.reshape(1, D_MODEL), w_in)


def _shift_rows(prev, x, d):
    if d == 0:
        return x
    e = jnp.concatenate([prev, x], axis=0)
    return pltpu.roll(e, d, axis=0)[SUBLANES:]


def _cumsum_rows(x):
    n = x.shape[0]
    row = lax.broadcasted_iota(jnp.int32, x.shape, 0)
    s = 1
    while s < n:
        x = x + jnp.where(row >= s, pltpu.roll(x, s, axis=0), 0.0)
        s *= 2
    return x


def _gdn_kernel(qkv_ref, z_ref, ab_ref, cs_ref, cw_ref, s0_ref, alog_ref, dtb_ref, nw_ref, o_ref, sn_ref,
                s_scr, prev_scr, *, nb, tm, tmp, chunk):
    ti = pl.program_id(1)
    nt = pl.num_programs(1)
    n_chunks = tmp // chunk
    n_sq = int(math.log2(chunk)) - 1

    @pl.when(ti == 0)
    def _():
        s_scr[...] = s0_ref[...]
        for b in range(nb):
            prev_scr[b] = jnp.concatenate(
                [jnp.zeros((SUBLANES - (CONV_WIDTH - 1), CONV_CH), F32), cs_ref[b]], axis=0)

    rowi = lax.broadcasted_iota(jnp.int32, (tmp, LANES), 0)
    ri = lax.broadcasted_iota(jnp.int32, (chunk, chunk), 0)
    ci = lax.broadcasted_iota(jnp.int32, (chunk, chunk), 1)
    tril = ri >= ci
    strict = ri > ci
    eye = (ri == ci).astype(F32)
    neg_a = -jnp.exp(alog_ref[...])
    dtb = dtb_ref[...]
    nw = nw_ref[...]

    def padrows(x):
        if tm == tmp:
            return x
        return jnp.concatenate([x, jnp.zeros((tmp - tm, x.shape[1]), F32)], axis=0)

    cw = cw_ref[...]
    valid = rowi < tm
    items = []
    for b in range(nb):
        x = padrows(qkv_ref[b])
        prev = prev_scr[b]
        acc = x * cw[CONV_WIDTH - 1:CONV_WIDTH]
        for d in range(1, CONV_WIDTH):
            acc = acc + _shift_rows(prev, x, d) * cw[CONV_WIDTH - 1 - d:CONV_WIDTH - d]
        if tm >= SUBLANES:
            prev_scr[b] = x[tm - SUBLANES:tm]
        y = _silu(acc)
        ab = padrows(ab_ref[b])
        g_all = jnp.where(valid, neg_a * jax.nn.softplus(ab + dtb), 0.0)
        beta_all = jnp.where(valid, jax.nn.sigmoid(ab), 0.0)
        for h in range(GDN_HEADS):
            qc = y[:, h * HEAD_DIM:(h + 1) * HEAD_DIM]
            kc = y[:, GDN_WIDTH + h * HEAD_DIM:GDN_WIDTH + (h + 1) * HEAD_DIM]
            v = y[:, 2 * GDN_WIDTH + h * HEAD_DIM:2 * GDN_WIDTH + (h + 1) * HEAD_DIM]
            q = qc * lax.rsqrt(jnp.sum(qc * qc, axis=-1, keepdims=True) + EPS) * (HEAD_DIM ** -0.5)
            k = kc * lax.rsqrt(jnp.sum(kc * kc, axis=-1, keepdims=True) + EPS)
            gb = jnp.broadcast_to(g_all[:, h:h + 1], (tmp, LANES))
            bb = jnp.broadcast_to(beta_all[:, GDN_HEADS + h:GDN_HEADS + h + 1], (tmp, LANES))
            for c in range(n_chunks):
                sl = slice(c * chunk, (c + 1) * chunk)
                items.append(dict(b=b, h=h, c=c, q=q[sl], k=k[sl], v=v[sl], bb=bb[sl], gb=gb[sl]))

    for it in items:
        gcb = _cumsum_rows(it["gb"])
        gct = jnp.transpose(gcb)[:chunk]
        diff = gcb[:, :chunk] - gct
        it["decay"] = jnp.where(tril, jnp.exp(jnp.where(tril, diff, 0.0)), 0.0)
        it["kb"] = it["k"] * it["bb"]
        it["eg"] = jnp.exp(gcb)
        gl = gcb[chunk - 1:chunk]
        it["egl"] = jnp.exp(gl)
        it["kdt"] = jnp.transpose(it["k"] * jnp.exp(gl - gcb))
    for it in items:
        kq = _dot_nt(jnp.concatenate([it["kb"], it["q"]], axis=0), it["k"])
        it["a"] = jnp.where(strict, kq[:chunk] * it["decay"], 0.0)
        it["qk"] = jnp.where(tril, kq[chunk:] * it["decay"], 0.0)
    for it in items:
        it["inv"] = eye - it["a"]
        it["p"] = _dot(it["a"], it["a"])
    for lvl in range(n_sq):
        for it in items:
            if lvl < n_sq - 1:
                r = _dot(jnp.concatenate([it["inv"], it["p"]], axis=0), it["p"])
                it["inv"] = it["inv"] + r[:chunk]
                it["p"] = r[chunk:]
            else:
                it["inv"] = it["inv"] + _dot(it["inv"], it["p"])
    for it in items:
        it["wu"] = _dot(it["inv"], jnp.concatenate([it["kb"] * it["eg"], it["v"] * it["bb"]], axis=1))
    for it in items:
        mn = _dot(it["kdt"], it["wu"])
        qo = _dot(it["qk"], it["wu"])
        it["mp"], it["nc"] = mn[:, :HEAD_DIM], mn[:, HEAD_DIM:]
        it["qp"] = it["q"] * it["eg"] - qo[:, :HEAD_DIM]
        it["oc"] = qo[:, HEAD_DIM:]

    state = {(b, h): s_scr[b, h] for b in range(nb) for h in range(GDN_HEADS)}
    for c in range(n_chunks):
        for it in items:
            if it["c"] != c:
                continue
            s = state[(it["b"], it["h"])]
            r = _dot(jnp.concatenate([it["qp"], it["mp"]], axis=0), s)
            it["o"] = r[:chunk] + it["oc"]
            state[(it["b"], it["h"])] = s * it["egl"] - r[chunk:] + it["nc"]
    for (b, h), s in state.items():
        s_scr[b, h] = s

    for b in range(nb):
        zf = padrows(z_ref[b])
        for h in range(GDN_HEADS):
            outs = [it["o"] for it in items if it["b"] == b and it["h"] == h]
            o = outs[0] if n_chunks == 1 else jnp.concatenate(outs, axis=0)
            o = (o * lax.rsqrt(jnp.mean(o * o, axis=-1, keepdims=True) + EPS) * nw
                 * _silu(zf[:, h * HEAD_DIM:(h + 1) * HEAD_DIM]))
            o_ref[b, :, h * HEAD_DIM:(h + 1) * HEAD_DIM] = o[:tm]

    @pl.when(ti == nt - 1)
    def _():
        sn_ref[...] = s_scr[...]


def _gdn(proj, state_conv, s0, conv_w, alog_row, dtb_row, norm_w, nb, tm):
    bsz, t, _ = proj.shape
    tmp = max(tm, SUBLANES)
    chunk = min(GDN_CHUNK, tmp)
    row = pl.BlockSpec((1, LANES), lambda b, i: (0, 0))
    sspec = pl.BlockSpec((nb, GDN_HEADS, HEAD_DIM, HEAD_DIM), lambda b, i: (b, 0, 0, 0))
    kern = functools.partial(_gdn_kernel, nb=nb, tm=tm, tmp=tmp, chunk=chunk)
    return pl.pallas_call(
        kern,
        grid=(bsz // nb, t // tm),
        in_specs=[pl.BlockSpec((nb, tm, CONV_CH), lambda b, i: (b, i, OFF_QKV // CONV_CH)),
                  pl.BlockSpec((nb, tm, GDN_WIDTH), lambda b, i: (b, i, OFF_Z // GDN_WIDTH)),
                  pl.BlockSpec((nb, tm, LANES), lambda b, i: (b, i, OFF_AB // LANES)),
                  pl.BlockSpec((nb, CONV_WIDTH - 1, CONV_CH), lambda b, i: (b, 0, 0)),
                  pl.BlockSpec((CONV_WIDTH, CONV_CH), lambda b, i: (0, 0)),
                  sspec, row, row, row],
        out_specs=[pl.BlockSpec((nb, tm, GDN_WIDTH), lambda b, i: (b, i, 0)), sspec],
        out_shape=[jax.ShapeDtypeStruct((bsz, t, GDN_WIDTH), F32),
                   jax.ShapeDtypeStruct(s0.shape, F32)],
        scratch_shapes=[pltpu.VMEM((nb, GDN_HEADS, HEAD_DIM, HEAD_DIM), F32),
                        pltpu.VMEM((nb, SUBLANES, CONV_CH), F32)],
        compiler_params=_cparams(("arbitrary", "arbitrary")),
        name="gdn",
    )(proj, proj, proj, state_conv, conv_w, s0, alog_row, dtb_row, norm_w)


def _pool_kernel(u_ref, st_ref, w_ref, sc_ref, o_ref, prev_scr, *, nb, tm, tmp, start_pos):
    ti = pl.program_id(1)
    halo = 2 * SUBLANES

    @pl.when(ti == 0)
    def _():
        for b in range(nb):
            prev_scr[b] = jnp.concatenate([jnp.zeros((halo - POOL_BUF, POOL_WIDTH), F32), st_ref[b]], axis=0)

    lane = lax.broadcasted_iota(jnp.int32, (tmp, POOL_WIDTH), 1)
    rowi = lax.broadcasted_iota(jnp.int32, (tmp, POOL_WIDTH), 0)
    pos1 = start_pos + ti * tm + rowi + 1
    grp = lane // POOL_GROUP
    win = jnp.where(grp == 0, POOL_WINDOWS[0],
                    jnp.where(grp == 1, POOL_WINDOWS[1], jnp.where(grp == 2, POOL_WINDOWS[2], POOL_WINDOWS[3])))
    cnt = jnp.minimum(pos1, win).astype(F32)
    for b in range(nb):
        x = u_ref[b]
        if tm != tmp:
            x = jnp.concatenate([x, jnp.zeros((tmp - tm, POOL_WIDTH), F32)], axis=0)
        e = jnp.concatenate([prev_scr[b], x], axis=0)
        sums = []
        s = e
        for w in POOL_WINDOWS:
            s = s + pltpu.roll(s, w // 2, axis=0)
            sums.append(s[halo:])
        sel = jnp.where(grp == 0, sums[0], jnp.where(grp == 1, sums[1], jnp.where(grp == 2, sums[2], sums[3])))
        pooled = sel / cnt - x
        out = _dot(pooled, w_ref[...]) * sc_ref[...]
        o_ref[b] = out[:tm]
        if tm >= halo:
            prev_scr[b] = x[tm - halo:tm]


def _pool(proj, state_pool, w_bd, pool_scale, nb, tm, start_pos):
    bsz, t, _ = proj.shape
    tmp = max(tm, SUBLANES)
    kern = functools.partial(_pool_kernel, nb=nb, tm=tm, tmp=tmp, start_pos=start_pos)
    return pl.pallas_call(
        kern,
        grid=(bsz // nb, t // tm),
        in_specs=[pl.BlockSpec((nb, tm, POOL_WIDTH), lambda b, i: (b, i, OFF_U // POOL_WIDTH)),
                  pl.BlockSpec((nb, POOL_BUF, POOL_WIDTH), lambda b, i: (b, 0, 0)),
                  pl.BlockSpec((POOL_WIDTH, POOL_WIDTH), lambda b, i: (0, 0)),
                  pl.BlockSpec((1, POOL_WIDTH), lambda b, i: (0, 0))],
        out_specs=pl.BlockSpec((nb, tm, POOL_WIDTH), lambda b, i: (b, i, 0)),
        out_shape=jax.ShapeDtypeStruct((bsz, t, POOL_WIDTH), F32),
        scratch_shapes=[pltpu.VMEM((nb, 2 * SUBLANES, POOL_WIDTH), F32)],
        compiler_params=_cparams(("arbitrary", "arbitrary")),
        name="pool",
    )(proj, state_pool, w_bd, pool_scale.reshape(1, POOL_WIDTH))


def _out_kernel(x_ref, po_ref, og_ref, gt_ref, sc_ref, sh_ref, gpm_ref, gpf_ref, wo_ref, rw_ref, rb_ref, c0_ref,
                x1_ref, ht_ref, g_ref, r_ref, cnt_ref, carry, *, tm):
    @pl.when((pl.program_id(0) == 0) & (pl.program_id(1) == 0))
    def _():
        carry[...] = c0_ref[...]

    mix = _dot(po_ref[0], wo_ref[:POOL_WIDTH]) + _dot(og_ref[0], wo_ref[POOL_WIDTH:])
    x1 = x_ref[0] + gt_ref[0] * _rms(mix, gpm_ref[...])
    x1_ref[0] = x1
    h = _rms(x1, gpf_ref[...]) * (1.0 + sc_ref[0]) + sh_ref[0]
    for s in range(ROW_TILES):
        ht_ref[pl.ds(s, tm, stride=ROW_TILES), :] = h[:, s * LANES:(s + 1) * LANES]
    logits = _dot(h, rw_ref[...]) + rb_ref[...]
    lane = lax.broadcasted_iota(jnp.int32, logits.shape, 1).astype(F32)
    work = logits
    sel = jnp.zeros(logits.shape, jnp.bool_)
    top = None
    den = jnp.zeros((tm, 1), F32)
    for j in range(TOP_K):
        m = jnp.max(work, axis=-1, keepdims=True)
        idx = jnp.min(jnp.where(work == m, lane, float(N_EXPERTS)), axis=-1, keepdims=True)
        hit = lane == idx
        sel = sel | hit
        work = jnp.where(hit, -jnp.inf, work)
        if j == 0:
            top = m
        den = den + jnp.exp(m - top)
    gates = jnp.exp(logits - top) / den
    g_ref[...] = jnp.where(sel, gates, -1.0)
    mask = sel.astype(F32)
    ri = lax.broadcasted_iota(jnp.int32, (tm, tm), 0)
    ci = lax.broadcasted_iota(jnp.int32, (tm, tm), 1)
    c = carry[...]
    r_ref[...] = _dot((ri > ci).astype(F32), mask) + c
    c = c + jnp.sum(mask, axis=0, keepdims=True)
    carry[...] = c
    cnt_ref[...] = c


def _out_proj(x, pool_out, o_gdn, gt, sc, sh, g_post_mix, g_pre_ffn, w_out_b, router_w, router_b, counts0,
              tm, per_token):
    bsz, t, _ = x.shape
    n = bsz * t
    tiles = t // tm

    def tok(width):
        return pl.BlockSpec((1, tm, width), lambda b, i: (b, i, 0))

    def const(shape):
        return pl.BlockSpec(shape, lambda b, i: tuple(0 for _ in shape))

    kern = functools.partial(_out_kernel, tm=tm)
    return pl.pallas_call(
        kern,
        grid=(bsz, tiles),
        in_specs=[tok(D_MODEL), tok(POOL_WIDTH), tok(GDN_WIDTH),
                  _mod_spec(per_token, tm), _mod_spec(per_token, tm), _mod_spec(per_token, tm),
                  const((1, D_MODEL)), const((1, D_MODEL)), const((D_MODEL, D_MODEL)),
                  const((D_MODEL, N_EXPERTS)), const((1, N_EXPERTS)), const((1, N_EXPERTS))],
        out_specs=[tok(D_MODEL),
                   pl.BlockSpec((tm * ROW_TILES, LANES), lambda b, i: (b * tiles + i, 0)),
                   pl.BlockSpec((tm, N_EXPERTS), lambda b, i: (b * tiles + i, 0)),
                   pl.BlockSpec((tm, N_EXPERTS), lambda b, i: (b * tiles + i, 0)),
                   const((1, N_EXPERTS))],
        out_shape=[jax.ShapeDtypeStruct((bsz, t, D_MODEL), F32),
                   jax.ShapeDtypeStruct((n * ROW_TILES, LANES), F32),
                   jax.ShapeDtypeStruct((n, N_EXPERTS), F32),
                   jax.ShapeDtypeStruct((n, N_EXPERTS), F32),
                   jax.ShapeDtypeStruct((1, N_EXPERTS), F32)],
        scratch_shapes=[pltpu.VMEM((1, N_EXPERTS), F32)],
        compiler_params=_cparams(("arbitrary", "arbitrary")),
        name="out_proj",
    )(x, pool_out, o_gdn, gt, sc, sh, g_post_mix.reshape(1, D_MODEL), g_pre_ffn.reshape(1, D_MODEL),
      w_out_b, router_w, router_b.reshape(1, N_EXPERTS), counts0)


def _dest_kernel(g_ref, r_ref, start_ref, d_ref, gt_ref, *, tm):
    g = g_ref[...]
    mask = g >= 0.0
    maskf = mask.astype(F32)
    dest = start_ref[...] + r_ref[...]
    ri = lax.broadcasted_iota(jnp.int32, (N_EXPERTS, N_EXPERTS), 0)
    ci = lax.broadcasted_iota(jnp.int32, (N_EXPERTS, N_EXPERTS), 1)
    slot = _dot(maskf, (ri < ci).astype(F32))
    lane4 = lax.broadcasted_iota(jnp.int32, (tm, TOP_K), 1)
    d_out = jnp.zeros((tm, TOP_K), F32)
    g_out = jnp.zeros((tm, TOP_K), F32)
    for j in range(TOP_K):
        pick = mask & (slot == float(j))
        dj = jnp.sum(jnp.where(pick, dest, 0.0), axis=-1, keepdims=True)
        gj = jnp.sum(jnp.where(pick, g, 0.0), axis=-1, keepdims=True)
        d_out = jnp.where(lane4 == j, dj, d_out)
        g_out = jnp.where(lane4 == j, gj, g_out)
    d_ref[...] = d_out.astype(jnp.int32)
    gt_ref[...] = g_out


def _dest(gsel, rank, start_row, tm):
    n = gsel.shape[0]
    kern = functools.partial(_dest_kernel, tm=tm)
    return pl.pallas_call(
        kern,
        grid=(n // tm,),
        in_specs=[pl.BlockSpec((tm, N_EXPERTS), lambda i: (i, 0)),
                  pl.BlockSpec((tm, N_EXPERTS), lambda i: (i, 0)),
                  pl.BlockSpec((1, N_EXPERTS), lambda i: (0, 0))],
        out_specs=[pl.BlockSpec((tm, TOP_K), lambda i: (i, 0)),
                   pl.BlockSpec((tm, TOP_K), lambda i: (i, 0))],
        out_shape=[jax.ShapeDtypeStruct((n, TOP_K), jnp.int32),
                   jax.ShapeDtypeStruct((n, TOP_K), F32)],
        compiler_params=_cparams(("arbitrary",)),
        name="dest",
    )(gsel, rank, start_row)


def _row_copy(src, dst, s_row, d_row, sem):
    return pltpu.make_async_copy(src.at[pl.ds(pl.multiple_of(s_row * ROW_TILES, ROW_TILES), ROW_TILES), :],
                                 dst.at[pl.ds(pl.multiple_of(d_row * ROW_TILES, ROW_TILES), ROW_TILES), :], sem)


def _scatter_kernel(cnt_ref, start_ref, end_ref, nu_ref, dest_ref, ha_ref, hb_ref, xs_ref, sem, *,
                    tm, bm, n_blocks, tiles_a):
    i = pl.program_id(0)

    def scatter_from(ht_ref):
        def body(t, carry):
            src = ht_ref.at[pl.ds(pl.multiple_of(t * ROW_TILES, ROW_TILES), ROW_TILES), :]
            for k in range(TOP_K):
                d = dest_ref[t * TOP_K + k]
                pltpu.make_async_copy(
                    src, xs_ref.at[pl.ds(pl.multiple_of(d * ROW_TILES, ROW_TILES), ROW_TILES), :],
                    sem).start(priority=k % DMA_QUEUES)
            return carry

        lax.fori_loop(0, tm, body, 0)

    @pl.when(i < tiles_a)
    def _():
        scatter_from(ha_ref)

    @pl.when(i >= tiles_a)
    def _():
        scatter_from(hb_ref)

    ht_ref = hb_ref

    def tile_copy(u):
        return pltpu.make_async_copy(
            ht_ref, xs_ref.at[pl.ds(pl.multiple_of(u * (tm * ROW_TILES), tm * ROW_TILES), tm * ROW_TILES), :], sem)

    for k in range(TOP_K):
        tile_copy(0).wait()

    @pl.when(i == pl.num_programs(0) - 1)
    def _():
        per = bm // tm

        def tail_start(u, carry):
            tile_copy(u).start()
            return carry

        def tail_wait(u, carry):
            tile_copy(u).wait()
            return carry

        lax.fori_loop(nu_ref[0] * per, n_blocks * per, tail_start, 0)
        lax.fori_loop(nu_ref[0] * per, n_blocks * per, tail_wait, 0)

        def per_expert(e, carry):
            lo = start_ref[e] + cnt_ref[e]
            hi = end_ref[e]

            def fill(r, c2):
                _row_copy(ht_ref, xs_ref, 0, r, sem).start()
                return c2

            lax.fori_loop(lo, hi, fill, 0)

            def drain(r, c2):
                _row_copy(ht_ref, xs_ref, 0, r, sem).wait()
                return c2

            lax.fori_loop(lo, hi, drain, 0)
            return carry

        lax.fori_loop(0, N_EXPERTS, per_expert, 0)


def _scatter_rows(ht_a, ht_b, dest_flat, cnt, start, end, n_used, n_blocks, tm, bm):
    rows = n_blocks * bm
    tiles_a, tiles_b = ht_a.shape[0] // (tm * ROW_TILES), ht_b.shape[0] // (tm * ROW_TILES)
    assert bm % tm == 0 and ht_a.shape[0] % (tm * ROW_TILES) == 0 and ht_b.shape[0] % (tm * ROW_TILES) == 0
    assert tiles_b >= 1
    kern = functools.partial(_scatter_kernel, tm=tm, bm=bm, n_blocks=n_blocks, tiles_a=tiles_a)
    return pl.pallas_call(
        kern,
        grid_spec=pltpu.PrefetchScalarGridSpec(
            num_scalar_prefetch=4,
            grid=(tiles_a + tiles_b,),
            in_specs=[pl.BlockSpec((tm * TOP_K,), lambda i, *_: (i,), memory_space=pltpu.SMEM),
                      pl.BlockSpec((tm * ROW_TILES, LANES), lambda i, *_: (jnp.minimum(i, tiles_a - 1), 0)),
                      pl.BlockSpec((tm * ROW_TILES, LANES), lambda i, *_: (jnp.maximum(i - tiles_a, 0), 0))],
            out_specs=pl.BlockSpec(memory_space=pl.ANY),
            scratch_shapes=[pltpu.SemaphoreType.DMA(())]),
        out_shape=jax.ShapeDtypeStruct((rows * ROW_TILES, LANES), F32),
        compiler_params=_cparams(("arbitrary",)),
        name="scatter_rows",
    )(cnt, start, end, n_used, dest_flat, ht_a, ht_b)


def _expert_kernel(be_ref, nx_ref, nu_ref, xs_ref, wg_ref, bg_ref, wd_ref, bd_ref, ys_ref,
                   wg_f, wd_f, wg_s, wd_s, wsem, *, bm):
    j = pl.program_id(0)
    prev = be_ref[jnp.maximum(j - 1, 0)]
    fresh = (j == 0) | (be_ref[j] != prev)
    active = j < nu_ref[0]

    def weight_copies(e):
        return (pltpu.make_async_copy(wg_ref.at[e], wg_f, wsem.at[0]),
                pltpu.make_async_copy(wd_ref.at[e], wd_f, wsem.at[1]))

    @pl.when(j == 0)
    def _():
        for c in weight_copies(be_ref[0]):
            c.start()

    @pl.when(active & fresh)
    def _():
        for c in weight_copies(be_ref[j]):
            c.wait()
        wg_s[...] = wg_f[...].astype(BF16)
        wd_s[...] = wd_f[...].astype(BF16)

        @pl.when(nx_ref[j] != be_ref[j])
        def _():
            for c in weight_copies(nx_ref[j]):
                c.start()

    @pl.when(active)
    def _():
        x = jnp.concatenate([xs_ref[pl.ds(s, bm, stride=ROW_TILES), :] for s in range(ROW_TILES)], axis=-1)
        xb = x.astype(BF16)
        gate = jnp.dot(xb, wg_s[:, :D_FF], preferred_element_type=F32) + bg_ref[0, :, :D_FF]
        up = jnp.dot(xb, wg_s[:, D_FF:], preferred_element_type=F32) + bg_ref[0, :, D_FF:]
        gate = jnp.minimum(gate, SWIGLU_LIMIT)
        up = jnp.clip(up, -SWIGLU_LIMIT, SWIGLU_LIMIT)
        act = (up + 1.0) * (gate * jax.nn.sigmoid(SWIGLU_ALPHA * gate))
        y = jnp.dot(act.astype(BF16), wd_s[...], preferred_element_type=F32) + bd_ref[0]
        for s in range(ROW_TILES):
            ys_ref[pl.ds(s, bm, stride=ROW_TILES), :] = y[:, s * LANES:(s + 1) * LANES]

    @pl.when(jnp.logical_not(active))
    def _():
        ys_ref[...] = jnp.zeros_like(ys_ref)


def _experts(xs, block_expert, next_expert, n_used, w_gu, b_gu, w_down, b_down, bm):
    rows = xs.shape[0] // ROW_TILES
    n_blocks = rows // bm

    def blk(j, be, nx, nu):
        return (jnp.minimum(j, nu[0] - 1), 0)

    def bsel(j, be, nx, nu):
        return (be[j], 0, 0)

    hbm = pl.BlockSpec(memory_space=pl.ANY)
    kern = functools.partial(_expert_kernel, bm=bm)
    return pl.pallas_call(
        kern,
        grid_spec=pltpu.PrefetchScalarGridSpec(
            num_scalar_prefetch=3,
            grid=(n_blocks,),
            in_specs=[pl.BlockSpec((bm * ROW_TILES, LANES), blk), hbm,
                      pl.BlockSpec((1, 1, 2 * D_FF), bsel), hbm,
                      pl.BlockSpec((1, 1, D_MODEL), bsel)],
            out_specs=pl.BlockSpec((bm * ROW_TILES, LANES), lambda j, be, nx, nu: (j, 0)),
            scratch_shapes=[pltpu.VMEM((D_MODEL, 2 * D_FF), F32),
                            pltpu.VMEM((D_FF, D_MODEL), F32),
                            pltpu.VMEM((D_MODEL, 2 * D_FF), BF16),
                            pltpu.VMEM((D_FF, D_MODEL), BF16),
                            pltpu.SemaphoreType.DMA((2,))]),
        out_shape=jax.ShapeDtypeStruct((rows * ROW_TILES, LANES), F32),
        compiler_params=_cparams(("arbitrary",)),
        name="experts",
    )(block_expert, next_expert, n_used, xs, w_gu, b_gu.reshape(N_EXPERTS, 1, 2 * D_FF), w_down,
      b_down.reshape(N_EXPERTS, 1, D_MODEL))


def _final_kernel(dcur_ref, dnext_ref, ys_ref, gate_ref, x1_ref, gt_ref, g_ref, o_ref, sl_ref, sem, *, tm, steps):
    g = pl.program_id(0) * pl.num_programs(1) + pl.program_id(1)
    cur = g % 2

    def request(d_ref, slot):
        def body(t, carry):
            for k in range(TOP_K):
                j = t * TOP_K + k
                pltpu.make_async_copy(
                    ys_ref.at[pl.ds(pl.multiple_of(d_ref[j] * ROW_TILES, ROW_TILES), ROW_TILES), :],
                    sl_ref.at[slot, pl.ds(pl.multiple_of(j * ROW_TILES, ROW_TILES), ROW_TILES), :],
                    sem.at[slot]).start(priority=k % DMA_QUEUES)
            return carry

        lax.fori_loop(0, tm, body, 0)

    @pl.when(g == 0)
    def _():
        request(dcur_ref, 0)

    @pl.when(g + 1 < steps)
    def _():
        request(dnext_ref, 1 - cur)

    pltpu.make_async_copy(ys_ref.at[pl.ds(0, tm * TOP_K * ROW_TILES), :], sl_ref.at[cur], sem.at[cur]).wait()

    gates = gate_ref[...]
    per_tok = TOP_K * ROW_TILES
    cols = []
    for s in range(ROW_TILES):
        acc = None
        for j in range(TOP_K):
            part = sl_ref[cur, pl.ds(j * ROW_TILES + s, tm, stride=per_tok), :] * gates[:, j:j + 1]
            acc = part if acc is None else acc + part
        cols.append(acc)
    f = jnp.concatenate(cols, axis=-1)
    o_ref[0] = x1_ref[0] + gt_ref[0] * _rms(f, g_ref[...])


def _final(ys, dest_flat, gates, x1, gt, g_post_ffn, tok_off, tm, per_token):
    bsz, t, _ = x1.shape
    tiles = t // tm
    off = tok_off // tm
    steps = bsz * tiles
    kern = functools.partial(_final_kernel, tm=tm, steps=steps)

    def dspec(ahead):
        return pl.BlockSpec((tm * TOP_K,), lambda b, i: (off + jnp.minimum(b * tiles + i + ahead, steps - 1),),
                            memory_space=pltpu.SMEM)

    return pl.pallas_call(
        kern,
        grid=(bsz, tiles),
        in_specs=[dspec(0), dspec(1),
                  pl.BlockSpec(memory_space=pl.ANY),
                  pl.BlockSpec((tm, TOP_K), lambda b, i: (off + b * tiles + i, 0)),
                  pl.BlockSpec((1, tm, D_MODEL), lambda b, i: (b, i, 0)),
                  _mod_spec(per_token, tm),
                  pl.BlockSpec((1, D_MODEL), lambda b, i: (0, 0))],
        out_specs=pl.BlockSpec((1, tm, D_MODEL), lambda b, i: (b, i, 0)),
        out_shape=jax.ShapeDtypeStruct((bsz, t, D_MODEL), F32),
        scratch_shapes=[pltpu.VMEM((2, tm * TOP_K * ROW_TILES, LANES), F32), pltpu.SemaphoreType.DMA((2,))],
        compiler_params=_cparams(("arbitrary", "arbitrary")),
        name="final",
    )(dest_flat, dest_flat, ys, gates, x1, gt, g_post_ffn.reshape(1, D_MODEL))


def _moe(ht_a, ht_b, gsel, rank, counts, w_gu, b_gu, w_down, b_down, tm, bm):
    n = gsel.shape[0]
    cnt = counts[0].astype(jnp.int32)
    nblk = (cnt + bm - 1) // bm
    blk_end = jnp.cumsum(nblk)
    blk_start = blk_end - nblk
    n_blocks = -(-(n * TOP_K) // bm) + N_EXPERTS
    blocks = jnp.arange(n_blocks, dtype=jnp.int32)
    block_expert = jnp.minimum(jnp.sum((blk_end[None, :] <= blocks[:, None]).astype(jnp.int32), axis=1),
                               N_EXPERTS - 1).astype(jnp.int32)
    ids = jnp.arange(N_EXPERTS, dtype=jnp.int32)
    later = jnp.where((nblk[None, :] > 0) & (ids[None, :] > ids[:, None]), ids[None, :], N_EXPERTS)
    nearest = jnp.min(later, axis=1)
    next_used = jnp.where(nearest < N_EXPERTS, nearest, ids).astype(jnp.int32)
    n_used = blk_end[-1:].astype(jnp.int32)
    row_start = (blk_start * bm).astype(jnp.int32)
    row_end = (blk_end * bm).astype(jnp.int32)
    dest, gates = _dest(gsel, rank, row_start.astype(F32).reshape(1, N_EXPERTS), DEST_TILE)
    dest_flat = dest.reshape(n * TOP_K)
    xs = _scatter_rows(ht_a, ht_b, dest_flat, cnt, row_start, row_end, n_used, n_blocks, SCATTER_TILE, bm)
    ys = _experts(xs, block_expert, next_used[block_expert], n_used, w_gu, b_gu, w_down, b_down, bm)
    return ys, dest_flat, gates


def _mix_path(x, mod, pool_buf, conv_buf, s0, start_pos, counts0, lp, tm, nb, per_token):
    (g_pre_mix, g_post_mix, g_pre_ffn, w_in_p, w_bd, pool_scale, conv_w, alog_row, dtb_row, norm_w,
     w_out_b, router_w, router_b) = lp
    bsz, t, _ = x.shape
    sh_mix, sc_mix, gt_mix, sh_ffn, sc_ffn, gt_ffn = (mod[:, i] for i in range(6))

    def shape_mod(m):
        if per_token:
            return jnp.repeat(m, t, axis=0).reshape(1, bsz * t, D_MODEL)
        return m.reshape(bsz, 1, D_MODEL)

    sh_mix, sc_mix, gt_mix, sh_ffn, sc_ffn, gt_ffn = map(shape_mod, (sh_mix, sc_mix, gt_mix, sh_ffn, sc_ffn, gt_ffn))
    xf = x.reshape(1, bsz * t, D_MODEL) if per_token else x
    proj = _in_proj(xf, sc_mix, sh_mix, g_pre_mix, w_in_p, PROJ_TILE, per_token).reshape(bsz, t, PROJ_WIDTH)
    tseq = min(tm, t)
    o_gdn, s_new = _gdn(proj, conv_buf, s0, conv_w, alog_row, dtb_row, norm_w, nb, tseq)
    pool_out = _pool(proj, pool_buf, w_bd, pool_scale, nb, tseq, start_pos)
    fl = (lambda a: a.reshape(1, bsz * t, a.shape[-1])) if per_token else (lambda a: a)
    x1, ht, gsel, rank, counts = _out_proj(xf, fl(pool_out), fl(o_gdn), gt_mix, sc_ffn, sh_ffn, g_post_mix,
                                           g_pre_ffn, w_out_b, router_w, router_b, counts0, tm, per_token)

    def last_rows(buf, off, width):
        keep = buf.shape[1]
        if t >= keep:
            return proj[:, t - keep:, off:off + width]
        return jnp.concatenate([buf[:, t:], proj[:, :, off:off + width]], axis=1)

    pool_new = last_rows(pool_buf, OFF_U, POOL_WIDTH)
    conv_new = last_rows(conv_buf, OFF_QKV, CONV_CH)
    return x1, ht, gsel, rank, counts, gt_ffn, pool_new, conv_new, s_new


def kernel(x_prompt, x_sample, c_prompt, c_sample, state_pool, state_conv, state_ssm, w_ada, b_ada, g_pre_mix, g_post_mix, g_pre_ffn, g_post_ffn, w_in, w_pool, pool_scale, conv_w, a_log, dt_bias, gdn_norm_w, w_out, router_w, router_b, w_gu, b_gu, w_down, b_down):
    depth = w_ada.shape[0]
    bp, tp, _ = x_prompt.shape
    bs, ts, _ = x_sample.shape
    tm = TOKEN_TILE
    y_p, y_s = x_prompt, x_sample
    outs = [[] for _ in range(6)]
    for l in range(depth):
        mod = _ada(jnp.concatenate([c_prompt, c_sample], axis=0), w_ada[l], b_ada[l]).reshape(bp + bs, 6, D_MODEL)
        w_in_p = w_in[l]
        w_bd = jax.scipy.linalg.block_diag(*[w_pool[l, g] for g in range(len(POOL_WINDOWS))])
        pad_row = lambda v: jnp.pad(v, (0, LANES - v.shape[0])).reshape(1, LANES)
        lp = (g_pre_mix[l], g_post_mix[l], g_pre_ffn[l], w_in_p, w_bd, pool_scale[l], conv_w[l],
              pad_row(a_log[l]), pad_row(dt_bias[l]), gdn_norm_w[l].reshape(1, HEAD_DIM),
              w_out[l].astype(BF16), router_w[l], router_b[l])
        zero_pool = jnp.zeros((bp, POOL_BUF, POOL_WIDTH), F32)
        zero_conv = jnp.zeros((bp, CONV_WIDTH - 1, CONV_CH), F32)
        zero_ssm = jnp.zeros((bp, GDN_HEADS, HEAD_DIM, HEAD_DIM), F32)
        no_counts = jnp.zeros((1, N_EXPERTS), F32)
        x1p, htp, gp, rp, cnt_p, gtp, pp, cp, sp = _mix_path(
            y_p, mod[:bp], zero_pool, zero_conv, zero_ssm, 0, no_counts, lp, tm, 1, False)
        x1s, hts, gs, rs, counts, gts, ps, cs, ss = _mix_path(
            y_s, mod[bp:], state_pool[l], state_conv[l], state_ssm[l], PAST_LEN, cnt_p, lp, tm, 8, True)
        gsel = jnp.concatenate([gp, gs], axis=0)
        rank = jnp.concatenate([rp, rs], axis=0)
        ys, dest_flat, gates = _moe(htp, hts, gsel, rank, counts, w_gu[l], b_gu[l], w_down[l], b_down[l],
                                    tm, MOE_BLOCK)
        y_p = _final(ys, dest_flat, gates, x1p, gtp, g_post_ffn[l], 0, tm, False)
        y_s = _final(ys, dest_flat, gates, x1s, gts, g_post_ffn[l], bp * tp, tm, True).reshape(bs, ts, D_MODEL)
        for lst, val in zip(outs, (pp, cp, sp, ps, cs, ss)):
            lst.append(val)
    return (y_p, y_s) + tuple(jnp.stack(o) for o in outs)
```

```python
import functools
import math

import jax
import jax.numpy as jnp
from jax import lax
from jax.experimental import pallas as pl
from jax.experimental.pallas import tpu as pltpu

F32 = jnp.float32
BF16 = jnp.bfloat16

D_MODEL = 1024
PAST_LEN = 16384
POOL_WINDOWS = (2, 4, 8, 16)
POOL_WIDTH = D_MODEL // 4
POOL_GROUP = POOL_WIDTH // len(POOL_WINDOWS)
POOL_BUF = max(POOL_WINDOWS) - 1
GDN_WIDTH = D_MODEL - POOL_WIDTH
HEAD_DIM = 128
GDN_HEADS = GDN_WIDTH // HEAD_DIM
CONV_WIDTH = 4
CONV_CH = 3 * GDN_WIDTH
GDN_CHUNK = 64
N_EXPERTS = 32
TOP_K = 4
D_FF = D_MODEL
SWIGLU_LIMIT = 7.0
SWIGLU_ALPHA = 1.702
EPS = 1e-6

SUBLANES = 8
LANES = 128
ROW_TILES = D_MODEL // LANES
OFF_QKV = 0
OFF_Z = OFF_QKV + CONV_CH
OFF_U = OFF_Z + GDN_WIDTH
OFF_AB = OFF_U + POOL_WIDTH
PROJ_WIDTH = OFF_AB + LANES
assert OFF_Z % GDN_WIDTH == 0 and OFF_U % POOL_WIDTH == 0 and OFF_AB % LANES == 0
TOKEN_TILE = 256
PROJ_TILE = 512
SCATTER_TILE = 512
DEST_TILE = 512
GDN_ITEM_GROUPS = 2
MOE_BLOCK = 512
DMA_QUEUES = 2
VMEM_LIMIT = 56 * 1024 * 1024


def _cparams(sem):
    return pltpu.CompilerParams(dimension_semantics=sem, vmem_limit_bytes=VMEM_LIMIT)


def _dot(a, b):
    return jnp.dot(a.astype(BF16), b.astype(BF16), preferred_element_type=F32)


def _dot_nt(a, b):
    return lax.dot_general(a.astype(BF16), b.astype(BF16), (((1,), (1,)), ((), ())),
                           preferred_element_type=F32)


def _silu(x):
    return x * jax.nn.sigmoid(x)


def _rms(x, g):
    return x * lax.rsqrt(jnp.mean(x * x, axis=-1, keepdims=True) + EPS) * g


def _ada_kernel(c_ref, w_ref, b_ref, o_ref):
    o_ref[...] = _dot(_silu(c_ref[...]), w_ref[...]) + b_ref[...]


def _ada(c, w_ada, b_ada):
    n = c.shape[0]
    cols = w_ada.shape[1]
    blk = D_MODEL
    return pl.pallas_call(
        _ada_kernel,
        grid=(cols // blk,),
        in_specs=[pl.BlockSpec((n, D_MODEL), lambda j: (0, 0)),
                  pl.BlockSpec((D_MODEL, blk), lambda j: (0, j)),
                  pl.BlockSpec((1, blk), lambda j: (0, j))],
        out_specs=pl.BlockSpec((n, blk), lambda j: (0, j)),
        out_shape=jax.ShapeDtypeStruct((n, cols), F32),
        compiler_params=_cparams(("arbitrary",)),
        name="ada",
    )(c, w_ada, b_ada.reshape(1, cols))


def _in_kernel(x_ref, sc_ref, sh_ref, g_ref, w_ref, o_ref, wb):
    @pl.when((pl.program_id(0) == 0) & (pl.program_id(1) == 0))
    def _():
        o1, o2, o3 = POOL_WIDTH, POOL_WIDTH + CONV_CH, POOL_WIDTH + CONV_CH + GDN_WIDTH
        wb[:, OFF_QKV:OFF_QKV + CONV_CH] = w_ref[:, o1:o2].astype(BF16)
        wb[:, OFF_Z:OFF_Z + GDN_WIDTH] = w_ref[:, o2:o3].astype(BF16)
        wb[:, OFF_U:OFF_U + POOL_WIDTH] = w_ref[:, :o1].astype(BF16)
        tail = w_ref.shape[1] - o3
        wb[:, OFF_AB:] = jnp.concatenate(
            [w_ref[:, o3:], jnp.zeros((D_MODEL, LANES - tail), F32)], axis=1).astype(BF16)

    h = _rms(x_ref[0], g_ref[...]) * (1.0 + sc_ref[0]) + sh_ref[0]
    o_ref[0] = jnp.dot(h.astype(BF16), wb[...], preferred_element_type=F32)


def _mod_spec(per_token, tm):
    if per_token:
        return pl.BlockSpec((1, tm, D_MODEL), lambda b, t: (0, t, 0))
    return pl.BlockSpec((1, 1, D_MODEL), lambda b, t: (b, 0, 0))


def _in_proj(x, sc, sh, g, w_in, tm, per_token):
    bsz, t, _ = x.shape
    return pl.pallas_call(
        _in_kernel,
        grid=(bsz, t // tm),
        in_specs=[pl.BlockSpec((1, tm, D_MODEL), lambda b, i: (b, i, 0)),
                  _mod_spec(per_token, tm), _mod_spec(per_token, tm),
                  pl.BlockSpec((1, D_MODEL), lambda b, i: (0, 0)),
                  pl.BlockSpec(w_in.shape, lambda b, i: (0, 0), pipeline_mode=pl.Buffered(1))],
        out_specs=pl.BlockSpec((1, tm, PROJ_WIDTH), lambda b, i: (b, i, 0)),
        out_shape=jax.ShapeDtypeStruct((bsz, t, PROJ_WIDTH), F32),
        scratch_shapes=[pltpu.VMEM((D_MODEL, PROJ_WIDTH), BF16)],
        compiler_params=_cparams(("arbitrary", "arbitrary")),
        name="in_proj",
    )(x, sc, sh, g.reshape(1, D_MODEL), w_in)


def _cumsum_chunks(x, chunk):
    pos = lax.broadcasted_iota(jnp.int32, x.shape, 0) % chunk
    s = 1
    while s < chunk:
        x = x + jnp.where(pos >= s, pltpu.roll(x, s, axis=0), 0.0)
        s *= 2
    return x


def _gdn_kernel(qkv_ref, z_ref, ab_ref, cs_ref, cw_ref, s0_ref, alog_ref, dtb_ref, nw_ref, o_ref, sn_ref,
                s_scr, prev_scr, *, nb, tm, tmp, chunk, group_items):
    ti = pl.program_id(1)
    nt = pl.num_programs(1)
    n_chunks = tmp // chunk
    n_sq = int(math.log2(chunk)) - 1

    @pl.when(ti == 0)
    def _():
        s_scr[...] = s0_ref[...]
        for b in range(nb):
            prev_scr[b] = jnp.concatenate(
                [jnp.zeros((SUBLANES - (CONV_WIDTH - 1), CONV_CH), F32), cs_ref[b]], axis=0)

    rowi = lax.broadcasted_iota(jnp.int32, (tmp, LANES), 0)
    ri = lax.broadcasted_iota(jnp.int32, (chunk, chunk), 0)
    ci = lax.broadcasted_iota(jnp.int32, (chunk, chunk), 1)
    tril = ri >= ci
    strict = ri > ci
    eye = (ri == ci).astype(F32)
    neg_a = -jnp.exp(alog_ref[...])
    dtb = dtb_ref[...]
    nw = nw_ref[...]

    def padrows(x):
        if tm == tmp:
            return x
        return jnp.concatenate([x, jnp.zeros((tmp - tm, x.shape[1]), F32)], axis=0)

    cw = cw_ref[...]
    valid = rowi < tm
    items = []
    for b in range(nb):
        x = padrows(qkv_ref[b])
        e = jnp.concatenate([prev_scr[b], x], axis=0)
        acc = x * cw[CONV_WIDTH - 1:CONV_WIDTH]
        for d in range(1, CONV_WIDTH):
            acc = acc + pltpu.roll(e, d, axis=0)[SUBLANES:] * cw[CONV_WIDTH - 1 - d:CONV_WIDTH - d]
        if tm >= SUBLANES:
            prev_scr[b] = x[tm - SUBLANES:tm]
        y = _silu(acc)
        ab = padrows(ab_ref[b])
        g_all = jnp.where(valid, neg_a * jax.nn.softplus(ab + dtb), 0.0)
        beta_all = jnp.where(valid, jax.nn.sigmoid(ab), 0.0)
        gc_all = _cumsum_chunks(g_all, chunk)
        for h in range(GDN_HEADS):
            qc = y[:, h * HEAD_DIM:(h + 1) * HEAD_DIM]
            kc = y[:, GDN_WIDTH + h * HEAD_DIM:GDN_WIDTH + (h + 1) * HEAD_DIM]
            v = y[:, 2 * GDN_WIDTH + h * HEAD_DIM:2 * GDN_WIDTH + (h + 1) * HEAD_DIM]
            q = qc * lax.rsqrt(jnp.sum(qc * qc, axis=-1, keepdims=True) + EPS) * (HEAD_DIM ** -0.5)
            k = kc * lax.rsqrt(jnp.sum(kc * kc, axis=-1, keepdims=True) + EPS)
            gcb = jnp.broadcast_to(gc_all[:, h:h + 1], (tmp, LANES))
            bb = jnp.broadcast_to(beta_all[:, GDN_HEADS + h:GDN_HEADS + h + 1], (tmp, LANES))
            for c in range(n_chunks):
                sl = slice(c * chunk, (c + 1) * chunk)
                items.append(dict(b=b, h=h, c=c, q=q[sl], k=k[sl], v=v[sl], bb=bb[sl], gcb=gcb[sl]))

    def within_chunk(group):
        for it in group:
            gcb = it["gcb"]
            gct = jnp.transpose(gcb)[:chunk]
            diff = gcb[:, :chunk] - gct
            it["decay"] = jnp.where(tril, jnp.exp(jnp.where(tril, diff, 0.0)), 0.0)
            it["kb"] = it["k"] * it["bb"]
            it["eg"] = jnp.exp(gcb)
            gl = gcb[chunk - 1:chunk]
            it["egl"] = jnp.exp(gl)
            it["kdt"] = jnp.transpose(it["k"] * jnp.exp(gl - gcb))
        for it in group:
            kq = _dot_nt(jnp.concatenate([it["kb"], it["q"]], axis=0), it["k"])
            it["a"] = jnp.where(strict, kq[:chunk] * it["decay"], 0.0)
            it["qk"] = jnp.where(tril, kq[chunk:] * it["decay"], 0.0)
        for it in group:
            it["inv"] = eye - it["a"]
            it["p"] = _dot(it["a"], it["a"])
        for lvl in range(n_sq):
            for it in group:
                if lvl < n_sq - 1:
                    r = _dot(jnp.concatenate([it["inv"], it["p"]], axis=0), it["p"])
                    it["inv"] = it["inv"] + r[:chunk]
                    it["p"] = r[chunk:]
                else:
                    it["inv"] = it["inv"] + _dot(it["inv"], it["p"])
        for it in group:
            it["wu"] = _dot(it["inv"], jnp.concatenate([it["kb"] * it["eg"], it["v"] * it["bb"]], axis=1))
        for it in group:
            mn = _dot(it["kdt"], it["wu"])
            qo = _dot(it["qk"], it["wu"])
            it["mp"], it["nc"] = mn[:, :HEAD_DIM], mn[:, HEAD_DIM:]
            it["qp"] = it["q"] * it["eg"] - qo[:, :HEAD_DIM]
            it["oc"] = qo[:, HEAD_DIM:]
            for key in ("decay", "kb", "eg", "kdt", "a", "qk", "inv", "p", "wu", "gcb", "k", "v", "bb"):
                it.pop(key, None)

    for g0 in range(0, len(items), group_items):
        within_chunk(items[g0:g0 + group_items])

    state = {(b, h): s_scr[b, h] for b in range(nb) for h in range(GDN_HEADS)}
    for c in range(n_chunks):
        for it in items:
            if it["c"] != c:
                continue
            s = state[(it["b"], it["h"])]
            r = _dot(jnp.concatenate([it["qp"], it["mp"]], axis=0), s)
            it["o"] = r[:chunk] + it["oc"]
            state[(it["b"], it["h"])] = s * it["egl"] - r[chunk:] + it["nc"]
    for (b, h), s in state.items():
        s_scr[b, h] = s

    for b in range(nb):
        zf = padrows(z_ref[b])
        for h in range(GDN_HEADS):
            outs = [it["o"] for it in items if it["b"] == b and it["h"] == h]
            o = outs[0] if n_chunks == 1 else jnp.concatenate(outs, axis=0)
            o = (o * lax.rsqrt(jnp.mean(o * o, axis=-1, keepdims=True) + EPS) * nw
                 * _silu(zf[:, h * HEAD_DIM:(h + 1) * HEAD_DIM]))
            o_ref[b, :, h * HEAD_DIM:(h + 1) * HEAD_DIM] = o[:tm]

    @pl.when(ti == nt - 1)
    def _():
        sn_ref[...] = s_scr[...]


def _gdn(proj, state_conv, s0, conv_w, alog_row, dtb_row, norm_w, nb, tm):
    bsz, t, _ = proj.shape
    tmp = max(tm, SUBLANES)
    chunk = min(GDN_CHUNK, tmp)
    row = pl.BlockSpec((1, LANES), lambda b, i: (0, 0))
    sspec = pl.BlockSpec((nb, GDN_HEADS, HEAD_DIM, HEAD_DIM), lambda b, i: (b, 0, 0, 0))
    n_items = nb * GDN_HEADS * (tmp // chunk)
    kern = functools.partial(_gdn_kernel, nb=nb, tm=tm, tmp=tmp, chunk=chunk,
                             group_items=n_items // GDN_ITEM_GROUPS)
    return pl.pallas_call(
        kern,
        grid=(bsz // nb, t // tm),
        in_specs=[pl.BlockSpec((nb, tm, CONV_CH), lambda b, i: (b, i, OFF_QKV // CONV_CH)),
                  pl.BlockSpec((nb, tm, GDN_WIDTH), lambda b, i: (b, i, OFF_Z // GDN_WIDTH)),
                  pl.BlockSpec((nb, tm, LANES), lambda b, i: (b, i, OFF_AB // LANES)),
                  pl.BlockSpec((nb, CONV_WIDTH - 1, CONV_CH), lambda b, i: (b, 0, 0)),
                  pl.BlockSpec((CONV_WIDTH, CONV_CH), lambda b, i: (0, 0)),
                  sspec, row, row, row],
        out_specs=[pl.BlockSpec((nb, tm, GDN_WIDTH), lambda b, i: (b, i, 0)), sspec],
        out_shape=[jax.ShapeDtypeStruct((bsz, t, GDN_WIDTH), F32),
                   jax.ShapeDtypeStruct(s0.shape, F32)],
        scratch_shapes=[pltpu.VMEM((nb, GDN_HEADS, HEAD_DIM, HEAD_DIM), F32),
                        pltpu.VMEM((nb, SUBLANES, CONV_CH), F32)],
        compiler_params=_cparams(("arbitrary", "arbitrary")),
        name="gdn",
    )(proj, proj, proj, state_conv, conv_w, s0, alog_row, dtb_row, norm_w)


def _pool_kernel(u_ref, st_ref, w_ref, sc_ref, o_ref, prev_scr, *, nb, tm, tmp, start_pos):
    ti = pl.program_id(1)
    halo = 2 * SUBLANES

    @pl.when(ti == 0)
    def _():
        for b in range(nb):
            prev_scr[b] = jnp.concatenate([jnp.zeros((halo - POOL_BUF, POOL_WIDTH), F32), st_ref[b]], axis=0)

    lane = lax.broadcasted_iota(jnp.int32, (tmp, POOL_WIDTH), 1)
    rowi = lax.broadcasted_iota(jnp.int32, (tmp, POOL_WIDTH), 0)
    pos1 = start_pos + ti * tm + rowi + 1
    grp = lane // POOL_GROUP
    win = jnp.where(grp == 0, POOL_WINDOWS[0],
                    jnp.where(grp == 1, POOL_WINDOWS[1], jnp.where(grp == 2, POOL_WINDOWS[2], POOL_WINDOWS[3])))
    cnt = jnp.minimum(pos1, win).astype(F32)
    for b in range(nb):
        x = u_ref[b]
        if tm != tmp:
            x = jnp.concatenate([x, jnp.zeros((tmp - tm, POOL_WIDTH), F32)], axis=0)
        e = jnp.concatenate([prev_scr[b], x], axis=0)
        sums = []
        s = e
        for w in POOL_WINDOWS:
            s = s + pltpu.roll(s, w // 2, axis=0)
            sums.append(s[halo:])
        sel = jnp.where(grp == 0, sums[0], jnp.where(grp == 1, sums[1], jnp.where(grp == 2, sums[2], sums[3])))
        pooled = sel / cnt - x
        out = _dot(pooled, w_ref[...]) * sc_ref[...]
        o_ref[b] = out[:tm]
        if tm >= halo:
            prev_scr[b] = x[tm - halo:tm]


def _pool(proj, state_pool, w_bd, pool_scale, nb, tm, start_pos):
    bsz, t, _ = proj.shape
    tmp = max(tm, SUBLANES)
    kern = functools.partial(_pool_kernel, nb=nb, tm=tm, tmp=tmp, start_pos=start_pos)
    return pl.pallas_call(
        kern,
        grid=(bsz // nb, t // tm),
        in_specs=[pl.BlockSpec((nb, tm, POOL_WIDTH), lambda b, i: (b, i, OFF_U // POOL_WIDTH)),
                  pl.BlockSpec((nb, POOL_BUF, POOL_WIDTH), lambda b, i: (b, 0, 0)),
                  pl.BlockSpec((POOL_WIDTH, POOL_WIDTH), lambda b, i: (0, 0)),
                  pl.BlockSpec((1, POOL_WIDTH), lambda b, i: (0, 0))],
        out_specs=pl.BlockSpec((nb, tm, POOL_WIDTH), lambda b, i: (b, i, 0)),
        out_shape=jax.ShapeDtypeStruct((bsz, t, POOL_WIDTH), F32),
        scratch_shapes=[pltpu.VMEM((nb, 2 * SUBLANES, POOL_WIDTH), F32)],
        compiler_params=_cparams(("arbitrary", "arbitrary")),
        name="pool",
    )(proj, state_pool, w_bd, pool_scale.reshape(1, POOL_WIDTH))


def _out_kernel(x_ref, po_ref, og_ref, gt_ref, sc_ref, sh_ref, gpm_ref, gpf_ref, wo_ref, rw_ref, rb_ref, c0_ref,
                x1_ref, ht_ref, g_ref, r_ref, cnt_ref, carry, *, tm):
    @pl.when((pl.program_id(0) == 0) & (pl.program_id(1) == 0))
    def _():
        carry[...] = c0_ref[...]

    mix = _dot(po_ref[0], wo_ref[:POOL_WIDTH]) + _dot(og_ref[0], wo_ref[POOL_WIDTH:])
    x1 = x_ref[0] + gt_ref[0] * _rms(mix, gpm_ref[...])
    x1_ref[0] = x1
    h = _rms(x1, gpf_ref[...]) * (1.0 + sc_ref[0]) + sh_ref[0]
    for s in range(ROW_TILES):
        ht_ref[pl.ds(s, tm, stride=ROW_TILES), :] = h[:, s * LANES:(s + 1) * LANES]
    logits = _dot(h, rw_ref[...]) + rb_ref[...]
    lane = lax.broadcasted_iota(jnp.int32, logits.shape, 1).astype(F32)
    work = logits
    sel = jnp.zeros(logits.shape, jnp.bool_)
    top = None
    den = jnp.zeros((tm, 1), F32)
    for j in range(TOP_K):
        m = jnp.max(work, axis=-1, keepdims=True)
        idx = jnp.min(jnp.where(work == m, lane, float(N_EXPERTS)), axis=-1, keepdims=True)
        hit = lane == idx
        sel = sel | hit
        work = jnp.where(hit, -jnp.inf, work)
        if j == 0:
            top = m
        den = den + jnp.exp(m - top)
    gates = jnp.exp(logits - top) / den
    g_ref[...] = jnp.where(sel, gates, -1.0)
    mask = sel.astype(F32)
    ri = lax.broadcasted_iota(jnp.int32, (tm, tm), 0)
    ci = lax.broadcasted_iota(jnp.int32, (tm, tm), 1)
    c = carry[...]
    r_ref[...] = _dot((ri > ci).astype(F32), mask) + c
    c = c + jnp.sum(mask, axis=0, keepdims=True)
    carry[...] = c
    cnt_ref[...] = c


def _out_proj(x, pool_out, o_gdn, gt, sc, sh, g_post_mix, g_pre_ffn, w_out_b, router_w, router_b, counts0,
              tm, per_token):
    bsz, t, _ = x.shape
    n = bsz * t
    tiles = t // tm

    def tok(width):
        return pl.BlockSpec((1, tm, width), lambda b, i: (b, i, 0))

    def const(shape):
        return pl.BlockSpec(shape, lambda b, i: tuple(0 for _ in shape))

    kern = functools.partial(_out_kernel, tm=tm)
    return pl.pallas_call(
        kern,
        grid=(bsz, tiles),
        in_specs=[tok(D_MODEL), tok(POOL_WIDTH), tok(GDN_WIDTH),
                  _mod_spec(per_token, tm), _mod_spec(per_token, tm), _mod_spec(per_token, tm),
                  const((1, D_MODEL)), const((1, D_MODEL)), const((D_MODEL, D_MODEL)),
                  const((D_MODEL, N_EXPERTS)), const((1, N_EXPERTS)), const((1, N_EXPERTS))],
        out_specs=[tok(D_MODEL),
                   pl.BlockSpec((tm * ROW_TILES, LANES), lambda b, i: (b * tiles + i, 0)),
                   pl.BlockSpec((tm, N_EXPERTS), lambda b, i: (b * tiles + i, 0)),
                   pl.BlockSpec((tm, N_EXPERTS), lambda b, i: (b * tiles + i, 0)),
                   const((1, N_EXPERTS))],
        out_shape=[jax.ShapeDtypeStruct((bsz, t, D_MODEL), F32),
                   jax.ShapeDtypeStruct((n * ROW_TILES, LANES), F32),
                   jax.ShapeDtypeStruct((n, N_EXPERTS), F32),
                   jax.ShapeDtypeStruct((n, N_EXPERTS), F32),
                   jax.ShapeDtypeStruct((1, N_EXPERTS), F32)],
        scratch_shapes=[pltpu.VMEM((1, N_EXPERTS), F32)],
        compiler_params=_cparams(("arbitrary", "arbitrary")),
        name="out_proj",
    )(x, pool_out, o_gdn, gt, sc, sh, g_post_mix.reshape(1, D_MODEL), g_pre_ffn.reshape(1, D_MODEL),
      w_out_b, router_w, router_b.reshape(1, N_EXPERTS), counts0)


def _dest_kernel(g_ref, r_ref, start_ref, d_ref, gt_ref, *, tm):
    g = g_ref[...]
    mask = g >= 0.0
    maskf = mask.astype(F32)
    dest = start_ref[...] + r_ref[...]
    ri = lax.broadcasted_iota(jnp.int32, (N_EXPERTS, N_EXPERTS), 0)
    ci = lax.broadcasted_iota(jnp.int32, (N_EXPERTS, N_EXPERTS), 1)
    slot = _dot(maskf, (ri < ci).astype(F32))
    lane4 = lax.broadcasted_iota(jnp.int32, (tm, TOP_K), 1)
    d_out = jnp.zeros((tm, TOP_K), F32)
    g_out = jnp.zeros((tm, TOP_K), F32)
    for j in range(TOP_K):
        pick = mask & (slot == float(j))
        dj = jnp.sum(jnp.where(pick, dest, 0.0), axis=-1, keepdims=True)
        gj = jnp.sum(jnp.where(pick, g, 0.0), axis=-1, keepdims=True)
        d_out = jnp.where(lane4 == j, dj, d_out)
        g_out = jnp.where(lane4 == j, gj, g_out)
    d_ref[...] = d_out.astype(jnp.int32)
    gt_ref[...] = g_out


def _dest(gsel, rank, start_row, tm):
    n = gsel.shape[0]
    kern = functools.partial(_dest_kernel, tm=tm)
    return pl.pallas_call(
        kern,
        grid=(n // tm,),
        in_specs=[pl.BlockSpec((tm, N_EXPERTS), lambda i: (i, 0)),
                  pl.BlockSpec((tm, N_EXPERTS), lambda i: (i, 0)),
                  pl.BlockSpec((1, N_EXPERTS), lambda i: (0, 0))],
        out_specs=[pl.BlockSpec((tm, TOP_K), lambda i: (i, 0)),
                   pl.BlockSpec((tm, TOP_K), lambda i: (i, 0))],
        out_shape=[jax.ShapeDtypeStruct((n, TOP_K), jnp.int32),
                   jax.ShapeDtypeStruct((n, TOP_K), F32)],
        compiler_params=_cparams(("arbitrary",)),
        name="dest",
    )(gsel, rank, start_row)


def _row_copy(src, dst, s_row, d_row, sem):
    return pltpu.make_async_copy(src.at[pl.ds(pl.multiple_of(s_row * ROW_TILES, ROW_TILES), ROW_TILES), :],
                                 dst.at[pl.ds(pl.multiple_of(d_row * ROW_TILES, ROW_TILES), ROW_TILES), :], sem)


def _scatter_kernel(cnt_ref, start_ref, end_ref, nu_ref, dest_ref, ha_ref, hb_ref, xs_ref, sem, *,
                    tm, bm, n_blocks, tiles_a):
    i = pl.program_id(0)

    def scatter_from(ht_ref):
        def body(t, carry):
            src = ht_ref.at[pl.ds(pl.multiple_of(t * ROW_TILES, ROW_TILES), ROW_TILES), :]
            for k in range(TOP_K):
                d = dest_ref[t * TOP_K + k]
                pltpu.make_async_copy(
                    src, xs_ref.at[pl.ds(pl.multiple_of(d * ROW_TILES, ROW_TILES), ROW_TILES), :],
                    sem).start(priority=k % DMA_QUEUES)
            return carry

        lax.fori_loop(0, tm, body, 0)

    @pl.when(i < tiles_a)
    def _():
        scatter_from(ha_ref)

    @pl.when(i >= tiles_a)
    def _():
        scatter_from(hb_ref)

    ht_ref = hb_ref

    def tile_copy(u):
        return pltpu.make_async_copy(
            ht_ref, xs_ref.at[pl.ds(pl.multiple_of(u * (tm * ROW_TILES), tm * ROW_TILES), tm * ROW_TILES), :], sem)

    for k in range(TOP_K):
        tile_copy(0).wait()

    @pl.when(i == pl.num_programs(0) - 1)
    def _():
        per = bm // tm

        def tail_start(u, carry):
            tile_copy(u).start()
            return carry

        def tail_wait(u, carry):
            tile_copy(u).wait()
            return carry

        lax.fori_loop(nu_ref[0] * per, n_blocks * per, tail_start, 0)
        lax.fori_loop(nu_ref[0] * per, n_blocks * per, tail_wait, 0)

        def per_expert(e, carry):
            lo = start_ref[e] + cnt_ref[e]
            hi = end_ref[e]

            def fill(r, c2):
                _row_copy(ht_ref, xs_ref, 0, r, sem).start()
                return c2

            lax.fori_loop(lo, hi, fill, 0)

            def drain(r, c2):
                _row_copy(ht_ref, xs_ref, 0, r, sem).wait()
                return c2

            lax.fori_loop(lo, hi, drain, 0)
            return carry

        lax.fori_loop(0, N_EXPERTS, per_expert, 0)


def _scatter_rows(ht_a, ht_b, dest_flat, cnt, start, end, n_used, n_blocks, tm, bm):
    rows = n_blocks * bm
    tiles_a, tiles_b = ht_a.shape[0] // (tm * ROW_TILES), ht_b.shape[0] // (tm * ROW_TILES)
    assert bm % tm == 0 and ht_a.shape[0] % (tm * ROW_TILES) == 0 and ht_b.shape[0] % (tm * ROW_TILES) == 0
    assert tiles_b >= 1
    kern = functools.partial(_scatter_kernel, tm=tm, bm=bm, n_blocks=n_blocks, tiles_a=tiles_a)
    return pl.pallas_call(
        kern,
        grid_spec=pltpu.PrefetchScalarGridSpec(
            num_scalar_prefetch=4,
            grid=(tiles_a + tiles_b,),
            in_specs=[pl.BlockSpec((tm * TOP_K,), lambda i, *_: (i,), memory_space=pltpu.SMEM),
                      pl.BlockSpec((tm * ROW_TILES, LANES), lambda i, *_: (jnp.minimum(i, tiles_a - 1), 0)),
                      pl.BlockSpec((tm * ROW_TILES, LANES), lambda i, *_: (jnp.maximum(i - tiles_a, 0), 0))],
            out_specs=pl.BlockSpec(memory_space=pl.ANY),
            scratch_shapes=[pltpu.SemaphoreType.DMA(())]),
        out_shape=jax.ShapeDtypeStruct((rows * ROW_TILES, LANES), F32),
        compiler_params=_cparams(("arbitrary",)),
        name="scatter_rows",
    )(cnt, start, end, n_used, dest_flat, ht_a, ht_b)


def _expert_kernel(be_ref, nx_ref, nv_ref, nu_ref, xs_ref, wg_ref, bg_ref, wd_ref, bd_ref, ys_ref,
                   wg_f, wd_f, wg_s, wd_s, wsem, *, bm):
    j = pl.program_id(0)
    prev = be_ref[jnp.maximum(j - 1, 0)]
    fresh = (j == 0) | (be_ref[j] != prev)
    active = j < nu_ref[0]

    def weight_copies(e):
        return (pltpu.make_async_copy(wg_ref.at[e], wg_f, wsem.at[0]),
                pltpu.make_async_copy(wd_ref.at[e], wd_f, wsem.at[1]))

    @pl.when(j == 0)
    def _():
        for c in weight_copies(be_ref[0]):
            c.start()

    @pl.when(active & fresh)
    def _():
        for c in weight_copies(be_ref[j]):
            c.wait()
        wg_s[...] = wg_f[...].astype(BF16)
        wd_s[...] = wd_f[...].astype(BF16)

        @pl.when(nx_ref[j] != be_ref[j])
        def _():
            for c in weight_copies(nx_ref[j]):
                c.start()

    def run_rows(m):
        x = jnp.concatenate([xs_ref[pl.ds(s, m, stride=ROW_TILES), :] for s in range(ROW_TILES)], axis=-1)
        xb = x.astype(BF16)
        gate = jnp.dot(xb, wg_s[:, :D_FF], preferred_element_type=F32) + bg_ref[0, :, :D_FF]
        up = jnp.dot(xb, wg_s[:, D_FF:], preferred_element_type=F32) + bg_ref[0, :, D_FF:]
        gate = jnp.minimum(gate, SWIGLU_LIMIT)
        up = jnp.clip(up, -SWIGLU_LIMIT, SWIGLU_LIMIT)
        act = (up + 1.0) * (gate * jax.nn.sigmoid(SWIGLU_ALPHA * gate))
        y = jnp.dot(act.astype(BF16), wd_s[...], preferred_element_type=F32) + bd_ref[0]
        for s in range(ROW_TILES):
            ys_ref[pl.ds(s, m, stride=ROW_TILES), :] = y[:, s * LANES:(s + 1) * LANES]

    half = bm // 2
    few = nv_ref[j] <= half

    @pl.when(active & jnp.logical_not(few))
    def _():
        run_rows(bm)

    @pl.when(active & few)
    def _():
        run_rows(half)
        ys_ref[pl.ds(half * ROW_TILES, half * ROW_TILES), :] = jnp.zeros((half * ROW_TILES, LANES), F32)

    @pl.when(jnp.logical_not(active))
    def _():
        ys_ref[...] = jnp.zeros_like(ys_ref)


def _experts(xs, block_expert, next_expert, block_valid, n_used, w_gu, b_gu, w_down, b_down, bm):
    rows = xs.shape[0] // ROW_TILES
    n_blocks = rows // bm

    def blk(j, be, nx, nv, nu):
        return (jnp.minimum(j, nu[0] - 1), 0)

    def bsel(j, be, nx, nv, nu):
        return (be[j], 0, 0)

    hbm = pl.BlockSpec(memory_space=pl.ANY)
    kern = functools.partial(_expert_kernel, bm=bm)
    return pl.pallas_call(
        kern,
        grid_spec=pltpu.PrefetchScalarGridSpec(
            num_scalar_prefetch=4,
            grid=(n_blocks,),
            in_specs=[pl.BlockSpec((bm * ROW_TILES, LANES), blk), hbm,
                      pl.BlockSpec((1, 1, 2 * D_FF), bsel), hbm,
                      pl.BlockSpec((1, 1, D_MODEL), bsel)],
            out_specs=pl.BlockSpec((bm * ROW_TILES, LANES), lambda j, be, nx, nv, nu: (j, 0)),
            scratch_shapes=[pltpu.VMEM((D_MODEL, 2 * D_FF), F32),
                            pltpu.VMEM((D_FF, D_MODEL), F32),
                            pltpu.VMEM((D_MODEL, 2 * D_FF), BF16),
                            pltpu.VMEM((D_FF, D_MODEL), BF16),
                            pltpu.SemaphoreType.DMA((2,))]),
        out_shape=jax.ShapeDtypeStruct((rows * ROW_TILES, LANES), F32),
        compiler_params=_cparams(("arbitrary",)),
        name="experts",
    )(block_expert, next_expert, block_valid, n_used, xs, w_gu, b_gu.reshape(N_EXPERTS, 1, 2 * D_FF), w_down,
      b_down.reshape(N_EXPERTS, 1, D_MODEL))


def _final_kernel(dcur_ref, dnext_ref, ys_ref, gate_ref, x1_ref, gt_ref, g_ref, o_ref, sl_ref, sem, *, tm, steps):
    g = pl.program_id(0) * pl.num_programs(1) + pl.program_id(1)
    cur = g % 2

    def request(d_ref, slot):
        def body(t, carry):
            for k in range(TOP_K):
                j = t * TOP_K + k
                pltpu.make_async_copy(
                    ys_ref.at[pl.ds(pl.multiple_of(d_ref[j] * ROW_TILES, ROW_TILES), ROW_TILES), :],
                    sl_ref.at[slot, pl.ds(pl.multiple_of(j * ROW_TILES, ROW_TILES), ROW_TILES), :],
                    sem.at[slot]).start(priority=k % DMA_QUEUES)
            return carry

        lax.fori_loop(0, tm, body, 0)

    @pl.when(g == 0)
    def _():
        request(dcur_ref, 0)

    @pl.when(g + 1 < steps)
    def _():
        request(dnext_ref, 1 - cur)

    pltpu.make_async_copy(ys_ref.at[pl.ds(0, tm * TOP_K * ROW_TILES), :], sl_ref.at[cur], sem.at[cur]).wait()

    gates = gate_ref[...]
    per_tok = TOP_K * ROW_TILES
    cols = []
    for s in range(ROW_TILES):
        acc = None
        for j in range(TOP_K):
            part = sl_ref[cur, pl.ds(j * ROW_TILES + s, tm, stride=per_tok), :] * gates[:, j:j + 1]
            acc = part if acc is None else acc + part
        cols.append(acc)
    f = jnp.concatenate(cols, axis=-1)
    o_ref[0] = x1_ref[0] + gt_ref[0] * _rms(f, g_ref[...])


def _final(ys, dest_flat, gates, x1, gt, g_post_ffn, tok_off, tm, per_token):
    bsz, t, _ = x1.shape
    tiles = t // tm
    off = tok_off // tm
    steps = bsz * tiles
    kern = functools.partial(_final_kernel, tm=tm, steps=steps)

    def dspec(ahead):
        return pl.BlockSpec((tm * TOP_K,), lambda b, i: (off + jnp.minimum(b * tiles + i + ahead, steps - 1),),
                            memory_space=pltpu.SMEM)

    return pl.pallas_call(
        kern,
        grid=(bsz, tiles),
        in_specs=[dspec(0), dspec(1),
                  pl.BlockSpec(memory_space=pl.ANY),
                  pl.BlockSpec((tm, TOP_K), lambda b, i: (off + b * tiles + i, 0)),
                  pl.BlockSpec((1, tm, D_MODEL), lambda b, i: (b, i, 0)),
                  _mod_spec(per_token, tm),
                  pl.BlockSpec((1, D_MODEL), lambda b, i: (0, 0))],
        out_specs=pl.BlockSpec((1, tm, D_MODEL), lambda b, i: (b, i, 0)),
        out_shape=jax.ShapeDtypeStruct((bsz, t, D_MODEL), F32),
        scratch_shapes=[pltpu.VMEM((2, tm * TOP_K * ROW_TILES, LANES), F32), pltpu.SemaphoreType.DMA((2,))],
        compiler_params=_cparams(("arbitrary", "arbitrary")),
        name="final",
    )(dest_flat, dest_flat, ys, gates, x1, gt, g_post_ffn.reshape(1, D_MODEL))


def _moe(ht_a, ht_b, gsel, rank, counts, w_gu, b_gu, w_down, b_down, tm, bm):
    n = gsel.shape[0]
    cnt = counts[0].astype(jnp.int32)
    nblk = (cnt + bm - 1) // bm
    blk_end = jnp.cumsum(nblk)
    blk_start = blk_end - nblk
    n_blocks = -(-(n * TOP_K) // bm) + N_EXPERTS
    blocks = jnp.arange(n_blocks, dtype=jnp.int32)
    block_expert = jnp.minimum(jnp.sum((blk_end[None, :] <= blocks[:, None]).astype(jnp.int32), axis=1),
                               N_EXPERTS - 1).astype(jnp.int32)
    ids = jnp.arange(N_EXPERTS, dtype=jnp.int32)
    later = jnp.where((nblk[None, :] > 0) & (ids[None, :] > ids[:, None]), ids[None, :], N_EXPERTS)
    nearest = jnp.min(later, axis=1)
    next_used = jnp.where(nearest < N_EXPERTS, nearest, ids).astype(jnp.int32)
    n_used = blk_end[-1:].astype(jnp.int32)
    row_start = (blk_start * bm).astype(jnp.int32)
    row_end = (blk_end * bm).astype(jnp.int32)
    dest, gates = _dest(gsel, rank, row_start.astype(F32).reshape(1, N_EXPERTS), DEST_TILE)
    dest_flat = dest.reshape(n * TOP_K)
    xs = _scatter_rows(ht_a, ht_b, dest_flat, cnt, row_start, row_end, n_used, n_blocks, SCATTER_TILE, bm)
    block_valid = jnp.clip(cnt[block_expert] - (blocks - blk_start[block_expert]) * bm, 0, bm).astype(jnp.int32)
    ys = _experts(xs, block_expert, next_used[block_expert], block_valid, n_used, w_gu, b_gu, w_down, b_down, bm)
    return ys, dest_flat, gates


def _mix_path(x, mod, pool_buf, conv_buf, s0, start_pos, counts0, lp, tm, nb, per_token):
    (g_pre_mix, g_post_mix, g_pre_ffn, w_in_p, w_bd, pool_scale, conv_w, alog_row, dtb_row, norm_w,
     w_out_b, router_w, router_b) = lp
    bsz, t, _ = x.shape
    sh_mix, sc_mix, gt_mix, sh_ffn, sc_ffn, gt_ffn = (mod[:, i] for i in range(6))

    def shape_mod(m):
        if per_token:
            return jnp.repeat(m, t, axis=0).reshape(1, bsz * t, D_MODEL)
        return m.reshape(bsz, 1, D_MODEL)

    sh_mix, sc_mix, gt_mix, sh_ffn, sc_ffn, gt_ffn = map(shape_mod, (sh_mix, sc_mix, gt_mix, sh_ffn, sc_ffn, gt_ffn))
    xf = x.reshape(1, bsz * t, D_MODEL) if per_token else x
    proj = _in_proj(xf, sc_mix, sh_mix, g_pre_mix, w_in_p, PROJ_TILE, per_token).reshape(bsz, t, PROJ_WIDTH)
    tseq = min(tm, t)
    o_gdn, s_new = _gdn(proj, conv_buf, s0, conv_w, alog_row, dtb_row, norm_w, nb, tseq)
    pool_out = _pool(proj, pool_buf, w_bd, pool_scale, nb, tseq, start_pos)
    fl = (lambda a: a.reshape(1, bsz * t, a.shape[-1])) if per_token else (lambda a: a)
    x1, ht, gsel, rank, counts = _out_proj(xf, fl(pool_out), fl(o_gdn), gt_mix, sc_ffn, sh_ffn, g_post_mix,
                                           g_pre_ffn, w_out_b, router_w, router_b, counts0, tm, per_token)

    def last_rows(buf, off, width):
        keep = buf.shape[1]
        if t >= keep:
            return proj[:, t - keep:, off:off + width]
        return jnp.concatenate([buf[:, t:], proj[:, :, off:off + width]], axis=1)

    pool_new = last_rows(pool_buf, OFF_U, POOL_WIDTH)
    conv_new = last_rows(conv_buf, OFF_QKV, CONV_CH)
    return x1, ht, gsel, rank, counts, gt_ffn, pool_new, conv_new, s_new


def kernel(x_prompt, x_sample, c_prompt, c_sample, state_pool, state_conv, state_ssm, w_ada, b_ada, g_pre_mix, g_post_mix, g_pre_ffn, g_post_ffn, w_in, w_pool, pool_scale, conv_w, a_log, dt_bias, gdn_norm_w, w_out, router_w, router_b, w_gu, b_gu, w_down, b_down):
    depth = w_ada.shape[0]
    bp, tp, _ = x_prompt.shape
    bs, ts, _ = x_sample.shape
    tm = TOKEN_TILE
    y_p, y_s = x_prompt, x_sample
    outs = [[] for _ in range(6)]
    for l in range(depth):
        mod = _ada(jnp.concatenate([c_prompt, c_sample], axis=0), w_ada[l], b_ada[l]).reshape(bp + bs, 6, D_MODEL)
        w_in_p = w_in[l]
        w_bd = jax.scipy.linalg.block_diag(*[w_pool[l, g] for g in range(len(POOL_WINDOWS))])
        pad_row = lambda v: jnp.pad(v, (0, LANES - v.shape[0])).reshape(1, LANES)
        lp = (g_pre_mix[l], g_post_mix[l], g_pre_ffn[l], w_in_p, w_bd, pool_scale[l], conv_w[l],
              pad_row(a_log[l]), pad_row(dt_bias[l]), gdn_norm_w[l].reshape(1, HEAD_DIM),
              w_out[l].astype(BF16), router_w[l], router_b[l])
        zero_pool = jnp.zeros((bp, POOL_BUF, POOL_WIDTH), F32)
        zero_conv = jnp.zeros((bp, CONV_WIDTH - 1, CONV_CH), F32)
        zero_ssm = jnp.zeros((bp, GDN_HEADS, HEAD_DIM, HEAD_DIM), F32)
        no_counts = jnp.zeros((1, N_EXPERTS), F32)
        x1p, htp, gp, rp, cnt_p, gtp, pp, cp, sp = _mix_path(
            y_p, mod[:bp], zero_pool, zero_conv, zero_ssm, 0, no_counts, lp, tm, 1, False)
        x1s, hts, gs, rs, counts, gts, ps, cs, ss = _mix_path(
            y_s, mod[bp:], state_pool[l], state_conv[l], state_ssm[l], PAST_LEN, cnt_p, lp, tm, 8, True)
        gsel = jnp.concatenate([gp, gs], axis=0)
        rank = jnp.concatenate([rp, rs], axis=0)
        ys, dest_flat, gates = _moe(htp, hts, gsel, rank, counts, w_gu[l], b_gu[l], w_down[l], b_down[l],
                                    tm, MOE_BLOCK)
        y_p = _final(ys, dest_flat, gates, x1p, gtp, g_post_ffn[l], 0, tm, False)
        y_s = _final(ys, dest_flat, gates, x1s, gts, g_post_ffn[l], bp * tp, tm, True).reshape(bs, ts, D_MODEL)
        for lst, val in zip(outs, (pp, cp, sp, ps, cs, ss)):
            lst.append(val)
    return (y_p, y_s) + tuple(o[0][None] if depth == 1 else jnp.stack(o) for o in outs)
```

```python
import functools
import math

import jax
import jax.numpy as jnp
from jax import lax
from jax.experimental import pallas as pl
from jax.experimental.pallas import tpu as pltpu

F32 = jnp.float32
BF16 = jnp.bfloat16

D_MODEL = 1024
PAST_LEN = 16384
POOL_WINDOWS = (2, 4, 8, 16)
POOL_WIDTH = D_MODEL // 4
POOL_GROUP = POOL_WIDTH // len(POOL_WINDOWS)
POOL_BUF = max(POOL_WINDOWS) - 1
GDN_WIDTH = D_MODEL - POOL_WIDTH
HEAD_DIM = 128
GDN_HEADS = GDN_WIDTH // HEAD_DIM
CONV_WIDTH = 4
CONV_CH = 3 * GDN_WIDTH
GDN_CHUNK = 64
N_EXPERTS = 32
TOP_K = 4
D_FF = D_MODEL
SWIGLU_LIMIT = 7.0
SWIGLU_ALPHA = 1.702
EPS = 1e-6

SUBLANES = 8
LANES = 128
ROW_TILES = D_MODEL // LANES
OFF_QKV = 0
OFF_Z = OFF_QKV + CONV_CH
OFF_U = OFF_Z + GDN_WIDTH
OFF_AB = OFF_U + POOL_WIDTH
PROJ_WIDTH = OFF_AB + LANES
assert OFF_Z % GDN_WIDTH == 0 and OFF_U % POOL_WIDTH == 0 and OFF_AB % LANES == 0
TOKEN_TILE = 256
PROJ_TILE = 512
SCATTER_TILE = 512
DEST_TILE = 512
GDN_ITEM_GROUPS = 1
MOE_BLOCK = 512
DMA_QUEUES = 2
VMEM_LIMIT = 56 * 1024 * 1024


def _cparams(sem):
    return pltpu.CompilerParams(dimension_semantics=sem, vmem_limit_bytes=VMEM_LIMIT)


def _dot(a, b):
    return jnp.dot(a.astype(BF16), b.astype(BF16), preferred_element_type=F32)


def _dot_nt(a, b):
    return lax.dot_general(a.astype(BF16), b.astype(BF16), (((1,), (1,)), ((), ())),
                           preferred_element_type=F32)


def _silu(x):
    return x * jax.nn.sigmoid(x)


def _rms(x, g):
    return x * lax.rsqrt(jnp.mean(x * x, axis=-1, keepdims=True) + EPS) * g


def _ada_kernel(c_ref, w_ref, b_ref, o_ref):
    o_ref[...] = _dot(_silu(c_ref[...]), w_ref[...]) + b_ref[...]


def _ada(c, w_ada, b_ada):
    n = c.shape[0]
    cols = w_ada.shape[1]
    blk = D_MODEL
    return pl.pallas_call(
        _ada_kernel,
        grid=(cols // blk,),
        in_specs=[pl.BlockSpec((n, D_MODEL), lambda j: (0, 0)),
                  pl.BlockSpec((D_MODEL, blk), lambda j: (0, j)),
                  pl.BlockSpec((1, blk), lambda j: (0, j))],
        out_specs=pl.BlockSpec((n, blk), lambda j: (0, j)),
        out_shape=jax.ShapeDtypeStruct((n, cols), F32),
        compiler_params=_cparams(("arbitrary",)),
        name="ada",
    )(c, w_ada, b_ada.reshape(1, cols))


def _in_kernel(x_ref, sc_ref, sh_ref, g_ref, w_ref, o_ref, wb):
    @pl.when((pl.program_id(0) == 0) & (pl.program_id(1) == 0))
    def _():
        o1, o2, o3 = POOL_WIDTH, POOL_WIDTH + CONV_CH, POOL_WIDTH + CONV_CH + GDN_WIDTH
        wb[:, OFF_QKV:OFF_QKV + CONV_CH] = w_ref[:, o1:o2].astype(BF16)
        wb[:, OFF_Z:OFF_Z + GDN_WIDTH] = w_ref[:, o2:o3].astype(BF16)
        wb[:, OFF_U:OFF_U + POOL_WIDTH] = w_ref[:, :o1].astype(BF16)
        tail = w_ref.shape[1] - o3
        wb[:, OFF_AB:] = jnp.concatenate(
            [w_ref[:, o3:], jnp.zeros((D_MODEL, LANES - tail), F32)], axis=1).astype(BF16)

    h = _rms(x_ref[0], g_ref[...]) * (1.0 + sc_ref[0]) + sh_ref[0]
    o_ref[0] = jnp.dot(h.astype(BF16), wb[...], preferred_element_type=F32)


def _mod_spec(per_token, tm):
    if per_token:
        return pl.BlockSpec((1, tm, D_MODEL), lambda b, t: (0, t, 0))
    return pl.BlockSpec((1, 1, D_MODEL), lambda b, t: (b, 0, 0))


def _in_proj(x, sc, sh, g, w_in, tm, per_token):
    bsz, t, _ = x.shape
    return pl.pallas_call(
        _in_kernel,
        grid=(bsz, t // tm),
        in_specs=[pl.BlockSpec((1, tm, D_MODEL), lambda b, i: (b, i, 0)),
                  _mod_spec(per_token, tm), _mod_spec(per_token, tm),
                  pl.BlockSpec((1, D_MODEL), lambda b, i: (0, 0)),
                  pl.BlockSpec(w_in.shape, lambda b, i: (0, 0), pipeline_mode=pl.Buffered(1))],
        out_specs=pl.BlockSpec((1, tm, PROJ_WIDTH), lambda b, i: (b, i, 0)),
        out_shape=jax.ShapeDtypeStruct((bsz, t, PROJ_WIDTH), F32),
        scratch_shapes=[pltpu.VMEM((D_MODEL, PROJ_WIDTH), BF16)],
        compiler_params=_cparams(("arbitrary", "arbitrary")),
        name="in_proj",
    )(x, sc, sh, g.reshape(1, D_MODEL), w_in)


def _cumsum_rows(x):
    n = x.shape[0]
    row = lax.broadcasted_iota(jnp.int32, x.shape, 0)
    s = 1
    while s < n:
        x = x + jnp.where(row >= s, pltpu.roll(x, s, axis=0), 0.0)
        s *= 2
    return x


def _gdn_kernel(qkv_ref, z_ref, ab_ref, cs_ref, cw_ref, s0_ref, alog_ref, dtb_ref, nw_ref, o_ref, sn_ref,
                s_scr, prev_scr, *, nb, tm, tmp, chunk, group_items):
    ti = pl.program_id(1)
    nt = pl.num_programs(1)
    n_chunks = tmp // chunk
    n_sq = int(math.log2(chunk)) - 1

    @pl.when(ti == 0)
    def _():
        s_scr[...] = s0_ref[...]
        for b in range(nb):
            prev_scr[b] = jnp.concatenate(
                [jnp.zeros((SUBLANES - (CONV_WIDTH - 1), CONV_CH), F32), cs_ref[b]], axis=0)

    rowi = lax.broadcasted_iota(jnp.int32, (tmp, LANES), 0)
    ri = lax.broadcasted_iota(jnp.int32, (chunk, chunk), 0)
    ci = lax.broadcasted_iota(jnp.int32, (chunk, chunk), 1)
    tril = ri >= ci
    strict = ri > ci
    eye = (ri == ci).astype(F32)
    neg_a = -jnp.exp(alog_ref[...])
    dtb = dtb_ref[...]
    nw = nw_ref[...]

    def padrows(x):
        if tm == tmp:
            return x
        return jnp.concatenate([x, jnp.zeros((tmp - tm, x.shape[1]), F32)], axis=0)

    cw = cw_ref[...]
    valid = rowi < tm
    items = []
    for b in range(nb):
        x = padrows(qkv_ref[b])
        e = jnp.concatenate([prev_scr[b], x], axis=0)
        acc = x * cw[CONV_WIDTH - 1:CONV_WIDTH]
        for d in range(1, CONV_WIDTH):
            acc = acc + pltpu.roll(e, d, axis=0)[SUBLANES:] * cw[CONV_WIDTH - 1 - d:CONV_WIDTH - d]
        if tm >= SUBLANES:
            prev_scr[b] = x[tm - SUBLANES:tm]
        y = _silu(acc)
        ab = padrows(ab_ref[b])
        g_all = jnp.where(valid, neg_a * jax.nn.softplus(ab + dtb), 0.0)
        beta_all = jnp.where(valid, jax.nn.sigmoid(ab), 0.0)
        for h in range(GDN_HEADS):
            qc = y[:, h * HEAD_DIM:(h + 1) * HEAD_DIM]
            kc = y[:, GDN_WIDTH + h * HEAD_DIM:GDN_WIDTH + (h + 1) * HEAD_DIM]
            v = y[:, 2 * GDN_WIDTH + h * HEAD_DIM:2 * GDN_WIDTH + (h + 1) * HEAD_DIM]
            q = qc * lax.rsqrt(jnp.sum(qc * qc, axis=-1, keepdims=True) + EPS) * (HEAD_DIM ** -0.5)
            k = kc * lax.rsqrt(jnp.sum(kc * kc, axis=-1, keepdims=True) + EPS)
            gb = jnp.broadcast_to(g_all[:, h:h + 1], (tmp, LANES))
            bb = jnp.broadcast_to(beta_all[:, GDN_HEADS + h:GDN_HEADS + h + 1], (tmp, LANES))
            for c in range(n_chunks):
                sl = slice(c * chunk, (c + 1) * chunk)
                items.append(dict(b=b, h=h, c=c, q=q[sl], k=k[sl], v=v[sl], bb=bb[sl], gb=gb[sl]))

    def within_chunk(group):
        for it in group:
            gcb = _cumsum_rows(it["gb"])
            gct = jnp.transpose(gcb)[:chunk]
            diff = gcb[:, :chunk] - gct
            it["decay"] = jnp.where(tril, jnp.exp(jnp.where(tril, diff, 0.0)), 0.0)
            it["kb"] = it["k"] * it["bb"]
            it["eg"] = jnp.exp(gcb)
            gl = gcb[chunk - 1:chunk]
            it["egl"] = jnp.exp(gl)
            it["kdt"] = jnp.transpose(it["k"] * jnp.exp(gl - gcb))
        for it in group:
            kq = _dot_nt(jnp.concatenate([it["kb"], it["q"]], axis=0), it["k"])
            it["a"] = jnp.where(strict, kq[:chunk] * it["decay"], 0.0)
            it["qk"] = jnp.where(tril, kq[chunk:] * it["decay"], 0.0)
        for it in group:
            it["inv"] = eye - it["a"]
            it["p"] = _dot(it["a"], it["a"])
        for lvl in range(n_sq):
            for it in group:
                if lvl < n_sq - 1:
                    r = _dot(jnp.concatenate([it["inv"], it["p"]], axis=0), it["p"])
                    it["inv"] = it["inv"] + r[:chunk]
                    it["p"] = r[chunk:]
                else:
                    it["inv"] = it["inv"] + _dot(it["inv"], it["p"])
        for it in group:
            it["wu"] = _dot(it["inv"], jnp.concatenate([it["kb"] * it["eg"], it["v"] * it["bb"]], axis=1))
        for it in group:
            mn = _dot(it["kdt"], it["wu"])
            qo = _dot(it["qk"], it["wu"])
            it["mp"], it["nc"] = mn[:, :HEAD_DIM], mn[:, HEAD_DIM:]
            it["qp"] = it["q"] * it["eg"] - qo[:, :HEAD_DIM]
            it["oc"] = qo[:, HEAD_DIM:]

    for g0 in range(0, len(items), group_items):
        within_chunk(items[g0:g0 + group_items])

    state = {(b, h): s_scr[b, h] for b in range(nb) for h in range(GDN_HEADS)}
    for c in range(n_chunks):
        for it in items:
            if it["c"] != c:
                continue
            s = state[(it["b"], it["h"])]
            r = _dot(jnp.concatenate([it["qp"], it["mp"]], axis=0), s)
            it["o"] = r[:chunk] + it["oc"]
            state[(it["b"], it["h"])] = s * it["egl"] - r[chunk:] + it["nc"]
    for (b, h), s in state.items():
        s_scr[b, h] = s

    for b in range(nb):
        zf = padrows(z_ref[b])
        for h in range(GDN_HEADS):
            outs = [it["o"] for it in items if it["b"] == b and it["h"] == h]
            o = outs[0] if n_chunks == 1 else jnp.concatenate(outs, axis=0)
            o = (o * lax.rsqrt(jnp.mean(o * o, axis=-1, keepdims=True) + EPS) * nw
                 * _silu(zf[:, h * HEAD_DIM:(h + 1) * HEAD_DIM]))
            o_ref[b, :, h * HEAD_DIM:(h + 1) * HEAD_DIM] = o[:tm]

    @pl.when(ti == nt - 1)
    def _():
        sn_ref[...] = s_scr[...]


def _gdn(proj, state_conv, s0, conv_w, alog_row, dtb_row, norm_w, nb, tm):
    bsz, t, _ = proj.shape
    tmp = max(tm, SUBLANES)
    chunk = min(GDN_CHUNK, tmp)
    row = pl.BlockSpec((1, LANES), lambda b, i: (0, 0))
    s_all, layer = s0
    sspec = pl.BlockSpec((nb, GDN_HEADS, HEAD_DIM, HEAD_DIM), lambda b, i: (b, 0, 0, 0))
    s_in = pl.BlockSpec((None, nb, GDN_HEADS, HEAD_DIM, HEAD_DIM), lambda b, i: (layer, b, 0, 0, 0))
    n_items = nb * GDN_HEADS * (tmp // chunk)
    kern = functools.partial(_gdn_kernel, nb=nb, tm=tm, tmp=tmp, chunk=chunk,
                             group_items=n_items // GDN_ITEM_GROUPS)
    return pl.pallas_call(
        kern,
        grid=(bsz // nb, t // tm),
        in_specs=[pl.BlockSpec((nb, tm, CONV_CH), lambda b, i: (b, i, OFF_QKV // CONV_CH)),
                  pl.BlockSpec((nb, tm, GDN_WIDTH), lambda b, i: (b, i, OFF_Z // GDN_WIDTH)),
                  pl.BlockSpec((nb, tm, LANES), lambda b, i: (b, i, OFF_AB // LANES)),
                  pl.BlockSpec((nb, CONV_WIDTH - 1, CONV_CH), lambda b, i: (b, 0, 0)),
                  pl.BlockSpec((CONV_WIDTH, CONV_CH), lambda b, i: (0, 0)),
                  s_in, row, row, row],
        out_specs=[pl.BlockSpec((nb, tm, GDN_WIDTH), lambda b, i: (b, i, 0)), sspec],
        out_shape=[jax.ShapeDtypeStruct((bsz, t, GDN_WIDTH), F32),
                   jax.ShapeDtypeStruct(s_all.shape[1:], F32)],
        scratch_shapes=[pltpu.VMEM((nb, GDN_HEADS, HEAD_DIM, HEAD_DIM), F32),
                        pltpu.VMEM((nb, SUBLANES, CONV_CH), F32)],
        compiler_params=_cparams(("arbitrary", "arbitrary")),
        name="gdn",
    )(proj, proj, proj, state_conv, conv_w, s_all, alog_row, dtb_row, norm_w)


def _pool_kernel(u_ref, st_ref, w_ref, sc_ref, o_ref, prev_scr, *, nb, tm, tmp, start_pos):
    ti = pl.program_id(1)
    halo = 2 * SUBLANES

    @pl.when(ti == 0)
    def _():
        for b in range(nb):
            prev_scr[b] = jnp.concatenate([jnp.zeros((halo - POOL_BUF, POOL_WIDTH), F32), st_ref[b]], axis=0)

    lane = lax.broadcasted_iota(jnp.int32, (tmp, POOL_WIDTH), 1)
    rowi = lax.broadcasted_iota(jnp.int32, (tmp, POOL_WIDTH), 0)
    pos1 = start_pos + ti * tm + rowi + 1
    grp = lane // POOL_GROUP
    win = jnp.where(grp == 0, POOL_WINDOWS[0],
                    jnp.where(grp == 1, POOL_WINDOWS[1], jnp.where(grp == 2, POOL_WINDOWS[2], POOL_WINDOWS[3])))
    cnt = jnp.minimum(pos1, win).astype(F32)
    for b in range(nb):
        x = u_ref[b]
        if tm != tmp:
            x = jnp.concatenate([x, jnp.zeros((tmp - tm, POOL_WIDTH), F32)], axis=0)
        e = jnp.concatenate([prev_scr[b], x], axis=0)
        sums = []
        s = e
        for w in POOL_WINDOWS:
            s = s + pltpu.roll(s, w // 2, axis=0)
            sums.append(s[halo:])
        sel = jnp.where(grp == 0, sums[0], jnp.where(grp == 1, sums[1], jnp.where(grp == 2, sums[2], sums[3])))
        pooled = sel / cnt - x
        out = _dot(pooled, w_ref[...]) * sc_ref[...]
        o_ref[b] = out[:tm]
        if tm >= halo:
            prev_scr[b] = x[tm - halo:tm]


def _pool(proj, state_pool, w_bd, pool_scale, nb, tm, start_pos):
    bsz, t, _ = proj.shape
    tmp = max(tm, SUBLANES)
    kern = functools.partial(_pool_kernel, nb=nb, tm=tm, tmp=tmp, start_pos=start_pos)
    return pl.pallas_call(
        kern,
        grid=(bsz // nb, t // tm),
        in_specs=[pl.BlockSpec((nb, tm, POOL_WIDTH), lambda b, i: (b, i, OFF_U // POOL_WIDTH)),
                  pl.BlockSpec((nb, POOL_BUF, POOL_WIDTH), lambda b, i: (b, 0, 0)),
                  pl.BlockSpec((POOL_WIDTH, POOL_WIDTH), lambda b, i: (0, 0)),
                  pl.BlockSpec((1, POOL_WIDTH), lambda b, i: (0, 0))],
        out_specs=pl.BlockSpec((nb, tm, POOL_WIDTH), lambda b, i: (b, i, 0)),
        out_shape=jax.ShapeDtypeStruct((bsz, t, POOL_WIDTH), F32),
        scratch_shapes=[pltpu.VMEM((nb, 2 * SUBLANES, POOL_WIDTH), F32)],
        compiler_params=_cparams(("arbitrary", "arbitrary")),
        name="pool",
    )(proj, state_pool, w_bd, pool_scale.reshape(1, POOL_WIDTH))


def _out_kernel(x_ref, po_ref, og_ref, gt_ref, sc_ref, sh_ref, gpm_ref, gpf_ref, wo_ref, rw_ref, rb_ref, c0_ref,
                x1_ref, ht_ref, g_ref, r_ref, cnt_ref, carry, *, tm):
    @pl.when((pl.program_id(0) == 0) & (pl.program_id(1) == 0))
    def _():
        carry[...] = c0_ref[...]

    mix = _dot(po_ref[0], wo_ref[:POOL_WIDTH]) + _dot(og_ref[0], wo_ref[POOL_WIDTH:])
    x1 = x_ref[0] + gt_ref[0] * _rms(mix, gpm_ref[...])
    x1_ref[0] = x1
    h = _rms(x1, gpf_ref[...]) * (1.0 + sc_ref[0]) + sh_ref[0]
    for s in range(ROW_TILES):
        ht_ref[pl.ds(s, tm, stride=ROW_TILES), :] = h[:, s * LANES:(s + 1) * LANES]
    logits = _dot(h, rw_ref[...]) + rb_ref[...]
    lane = lax.broadcasted_iota(jnp.int32, logits.shape, 1).astype(F32)
    work = logits
    sel = jnp.zeros(logits.shape, jnp.bool_)
    top = None
    den = jnp.zeros((tm, 1), F32)
    for j in range(TOP_K):
        m = jnp.max(work, axis=-1, keepdims=True)
        idx = jnp.min(jnp.where(work == m, lane, float(N_EXPERTS)), axis=-1, keepdims=True)
        hit = lane == idx
        sel = sel | hit
        work = jnp.where(hit, -jnp.inf, work)
        if j == 0:
            top = m
        den = den + jnp.exp(m - top)
    gates = jnp.exp(logits - top) / den
    g_ref[...] = jnp.where(sel, gates, -1.0)
    mask = sel.astype(F32)
    ri = lax.broadcasted_iota(jnp.int32, (tm, tm), 0)
    ci = lax.broadcasted_iota(jnp.int32, (tm, tm), 1)
    c = carry[...]
    r_ref[...] = _dot((ri > ci).astype(F32), mask) + c
    c = c + jnp.sum(mask, axis=0, keepdims=True)
    carry[...] = c
    cnt_ref[...] = c


def _out_proj(x, pool_out, o_gdn, gt, sc, sh, g_post_mix, g_pre_ffn, w_out_b, router_w, router_b, counts0,
              tm, per_token):
    bsz, t, _ = x.shape
    n = bsz * t
    tiles = t // tm

    def tok(width):
        return pl.BlockSpec((1, tm, width), lambda b, i: (b, i, 0))

    def const(shape):
        return pl.BlockSpec(shape, lambda b, i: tuple(0 for _ in shape))

    kern = functools.partial(_out_kernel, tm=tm)
    return pl.pallas_call(
        kern,
        grid=(bsz, tiles),
        in_specs=[tok(D_MODEL), tok(POOL_WIDTH), tok(GDN_WIDTH),
                  _mod_spec(per_token, tm), _mod_spec(per_token, tm), _mod_spec(per_token, tm),
                  const((1, D_MODEL)), const((1, D_MODEL)), const((D_MODEL, D_MODEL)),
                  const((D_MODEL, N_EXPERTS)), const((1, N_EXPERTS)), const((1, N_EXPERTS))],
        out_specs=[tok(D_MODEL),
                   pl.BlockSpec((tm * ROW_TILES, LANES), lambda b, i: (b * tiles + i, 0)),
                   pl.BlockSpec((tm, N_EXPERTS), lambda b, i: (b * tiles + i, 0)),
                   pl.BlockSpec((tm, N_EXPERTS), lambda b, i: (b * tiles + i, 0)),
                   const((1, N_EXPERTS))],
        out_shape=[jax.ShapeDtypeStruct((bsz, t, D_MODEL), F32),
                   jax.ShapeDtypeStruct((n * ROW_TILES, LANES), F32),
                   jax.ShapeDtypeStruct((n, N_EXPERTS), F32),
                   jax.ShapeDtypeStruct((n, N_EXPERTS), F32),
                   jax.ShapeDtypeStruct((1, N_EXPERTS), F32)],
        scratch_shapes=[pltpu.VMEM((1, N_EXPERTS), F32)],
        compiler_params=_cparams(("arbitrary", "arbitrary")),
        name="out_proj",
    )(x, pool_out, o_gdn, gt, sc, sh, g_post_mix.reshape(1, D_MODEL), g_pre_ffn.reshape(1, D_MODEL),
      w_out_b, router_w, router_b.reshape(1, N_EXPERTS), counts0)


def _dest_kernel(g_ref, r_ref, start_ref, d_ref, gt_ref, *, tm):
    g = g_ref[...]
    mask = g >= 0.0
    maskf = mask.astype(F32)
    dest = start_ref[...] + r_ref[...]
    ri = lax.broadcasted_iota(jnp.int32, (N_EXPERTS, N_EXPERTS), 0)
    ci = lax.broadcasted_iota(jnp.int32, (N_EXPERTS, N_EXPERTS), 1)
    slot = _dot(maskf, (ri < ci).astype(F32))
    lane4 = lax.broadcasted_iota(jnp.int32, (tm, TOP_K), 1)
    d_out = jnp.zeros((tm, TOP_K), F32)
    g_out = jnp.zeros((tm, TOP_K), F32)
    for j in range(TOP_K):
        pick = mask & (slot == float(j))
        dj = jnp.sum(jnp.where(pick, dest, 0.0), axis=-1, keepdims=True)
        gj = jnp.sum(jnp.where(pick, g, 0.0), axis=-1, keepdims=True)
        d_out = jnp.where(lane4 == j, dj, d_out)
        g_out = jnp.where(lane4 == j, gj, g_out)
    d_ref[...] = d_out.astype(jnp.int32)
    gt_ref[...] = g_out


def _dest(gsel, rank, start_row, tm):
    n = gsel.shape[0]
    kern = functools.partial(_dest_kernel, tm=tm)
    return pl.pallas_call(
        kern,
        grid=(n // tm,),
        in_specs=[pl.BlockSpec((tm, N_EXPERTS), lambda i: (i, 0)),
                  pl.BlockSpec((tm, N_EXPERTS), lambda i: (i, 0)),
                  pl.BlockSpec((1, N_EXPERTS), lambda i: (0, 0))],
        out_specs=[pl.BlockSpec((tm, TOP_K), lambda i: (i, 0)),
                   pl.BlockSpec((tm, TOP_K), lambda i: (i, 0))],
        out_shape=[jax.ShapeDtypeStruct((n, TOP_K), jnp.int32),
                   jax.ShapeDtypeStruct((n, TOP_K), F32)],
        compiler_params=_cparams(("arbitrary",)),
        name="dest",
    )(gsel, rank, start_row)


def _row_copy(src, dst, s_row, d_row, sem):
    return pltpu.make_async_copy(src.at[pl.ds(pl.multiple_of(s_row * ROW_TILES, ROW_TILES), ROW_TILES), :],
                                 dst.at[pl.ds(pl.multiple_of(d_row * ROW_TILES, ROW_TILES), ROW_TILES), :], sem)


def _scatter_kernel(cnt_ref, start_ref, end_ref, nu_ref, dest_ref, ha_ref, hb_ref, xs_ref, sem, *,
                    tm, bm, n_blocks, tiles_a):
    i = pl.program_id(0)

    def scatter_from(ht_ref):
        def body(t, carry):
            src = ht_ref.at[pl.ds(pl.multiple_of(t * ROW_TILES, ROW_TILES), ROW_TILES), :]
            for k in range(TOP_K):
                d = dest_ref[t * TOP_K + k]
                pltpu.make_async_copy(
                    src, xs_ref.at[pl.ds(pl.multiple_of(d * ROW_TILES, ROW_TILES), ROW_TILES), :],
                    sem).start(priority=k % DMA_QUEUES)
            return carry

        lax.fori_loop(0, tm, body, 0)

    @pl.when(i < tiles_a)
    def _():
        scatter_from(ha_ref)

    @pl.when(i >= tiles_a)
    def _():
        scatter_from(hb_ref)

    ht_ref = hb_ref

    def tile_copy(u):
        return pltpu.make_async_copy(
            ht_ref, xs_ref.at[pl.ds(pl.multiple_of(u * (tm * ROW_TILES), tm * ROW_TILES), tm * ROW_TILES), :], sem)

    for k in range(TOP_K):
        tile_copy(0).wait()

    @pl.when(i == pl.num_programs(0) - 1)
    def _():
        per = bm // tm

        def tail_start(u, carry):
            tile_copy(u).start()
            return carry

        def tail_wait(u, carry):
            tile_copy(u).wait()
            return carry

        lax.fori_loop(nu_ref[0] * per, n_blocks * per, tail_start, 0)
        lax.fori_loop(nu_ref[0] * per, n_blocks * per, tail_wait, 0)

        def per_expert(e, carry):
            lo = start_ref[e] + cnt_ref[e]
            hi = end_ref[e]

            def fill(r, c2):
                _row_copy(ht_ref, xs_ref, 0, r, sem).start()
                return c2

            lax.fori_loop(lo, hi, fill, 0)

            def drain(r, c2):
                _row_copy(ht_ref, xs_ref, 0, r, sem).wait()
                return c2

            lax.fori_loop(lo, hi, drain, 0)
            return carry

        lax.fori_loop(0, N_EXPERTS, per_expert, 0)


def _scatter_rows(ht_a, ht_b, dest_flat, cnt, start, end, n_used, n_blocks, tm, bm):
    rows = n_blocks * bm
    tiles_a, tiles_b = ht_a.shape[0] // (tm * ROW_TILES), ht_b.shape[0] // (tm * ROW_TILES)
    assert bm % tm == 0 and ht_a.shape[0] % (tm * ROW_TILES) == 0 and ht_b.shape[0] % (tm * ROW_TILES) == 0
    assert tiles_b >= 1
    kern = functools.partial(_scatter_kernel, tm=tm, bm=bm, n_blocks=n_blocks, tiles_a=tiles_a)
    return pl.pallas_call(
        kern,
        grid_spec=pltpu.PrefetchScalarGridSpec(
            num_scalar_prefetch=4,
            grid=(tiles_a + tiles_b,),
            in_specs=[pl.BlockSpec((tm * TOP_K,), lambda i, *_: (i,), memory_space=pltpu.SMEM),
                      pl.BlockSpec((tm * ROW_TILES, LANES), lambda i, *_: (jnp.minimum(i, tiles_a - 1), 0)),
                      pl.BlockSpec((tm * ROW_TILES, LANES), lambda i, *_: (jnp.maximum(i - tiles_a, 0), 0))],
            out_specs=pl.BlockSpec(memory_space=pl.ANY),
            scratch_shapes=[pltpu.SemaphoreType.DMA(())]),
        out_shape=jax.ShapeDtypeStruct((rows * ROW_TILES, LANES), F32),
        compiler_params=_cparams(("arbitrary",)),
        name="scatter_rows",
    )(cnt, start, end, n_used, dest_flat, ht_a, ht_b)


def _expert_kernel(be_ref, nx_ref, nv_ref, nu_ref, xs_ref, wg_ref, bg_ref, wd_ref, bd_ref, ys_ref,
                   wg_f, wd_f, wg_s, wd_s, wsem, *, bm):
    j = pl.program_id(0)
    prev = be_ref[jnp.maximum(j - 1, 0)]
    fresh = (j == 0) | (be_ref[j] != prev)
    active = j < nu_ref[0]

    def weight_copies(e):
        return (pltpu.make_async_copy(wg_ref.at[e], wg_f, wsem.at[0]),
                pltpu.make_async_copy(wd_ref.at[e], wd_f, wsem.at[1]))

    @pl.when(j == 0)
    def _():
        for c in weight_copies(be_ref[0]):
            c.start()

    @pl.when(active & fresh)
    def _():
        for c in weight_copies(be_ref[j]):
            c.wait()
        wg_s[...] = wg_f[...].astype(BF16)
        wd_s[...] = wd_f[...].astype(BF16)

        @pl.when(nx_ref[j] != be_ref[j])
        def _():
            for c in weight_copies(nx_ref[j]):
                c.start()

    def run_rows(m):
        x = jnp.concatenate([xs_ref[pl.ds(s, m, stride=ROW_TILES), :] for s in range(ROW_TILES)], axis=-1)
        xb = x.astype(BF16)
        gate = jnp.dot(xb, wg_s[:, :D_FF], preferred_element_type=F32) + bg_ref[0, :, :D_FF]
        up = jnp.dot(xb, wg_s[:, D_FF:], preferred_element_type=F32) + bg_ref[0, :, D_FF:]
        gate = jnp.minimum(gate, SWIGLU_LIMIT)
        up = jnp.clip(up, -SWIGLU_LIMIT, SWIGLU_LIMIT)
        act = (up + 1.0) * (gate * jax.nn.sigmoid(SWIGLU_ALPHA * gate))
        y = jnp.dot(act.astype(BF16), wd_s[...], preferred_element_type=F32) + bd_ref[0]
        for s in range(ROW_TILES):
            ys_ref[pl.ds(s, m, stride=ROW_TILES), :] = y[:, s * LANES:(s + 1) * LANES]

    half = bm // 2
    few = nv_ref[j] <= half

    @pl.when(active & jnp.logical_not(few))
    def _():
        run_rows(bm)

    @pl.when(active & few)
    def _():
        run_rows(half)
        ys_ref[pl.ds(half * ROW_TILES, half * ROW_TILES), :] = jnp.zeros((half * ROW_TILES, LANES), F32)

    @pl.when(jnp.logical_not(active))
    def _():
        ys_ref[...] = jnp.zeros_like(ys_ref)


def _experts(xs, block_expert, next_expert, block_valid, n_used, w_gu, b_gu, w_down, b_down, bm):
    rows = xs.shape[0] // ROW_TILES
    n_blocks = rows // bm

    def blk(j, be, nx, nv, nu):
        return (jnp.minimum(j, nu[0] - 1), 0)

    def bsel(j, be, nx, nv, nu):
        return (be[j], 0, 0)

    hbm = pl.BlockSpec(memory_space=pl.ANY)
    kern = functools.partial(_expert_kernel, bm=bm)
    return pl.pallas_call(
        kern,
        grid_spec=pltpu.PrefetchScalarGridSpec(
            num_scalar_prefetch=4,
            grid=(n_blocks,),
            in_specs=[pl.BlockSpec((bm * ROW_TILES, LANES), blk), hbm,
                      pl.BlockSpec((1, 1, 2 * D_FF), bsel), hbm,
                      pl.BlockSpec((1, 1, D_MODEL), bsel)],
            out_specs=pl.BlockSpec((bm * ROW_TILES, LANES), lambda j, be, nx, nv, nu: (j, 0)),
            scratch_shapes=[pltpu.VMEM((D_MODEL, 2 * D_FF), F32),
                            pltpu.VMEM((D_FF, D_MODEL), F32),
                            pltpu.VMEM((D_MODEL, 2 * D_FF), BF16),
                            pltpu.VMEM((D_FF, D_MODEL), BF16),
                            pltpu.SemaphoreType.DMA((2,))]),
        out_shape=jax.ShapeDtypeStruct((rows * ROW_TILES, LANES), F32),
        compiler_params=_cparams(("arbitrary",)),
        name="experts",
    )(block_expert, next_expert, block_valid, n_used, xs, w_gu, b_gu.reshape(N_EXPERTS, 1, 2 * D_FF), w_down,
      b_down.reshape(N_EXPERTS, 1, D_MODEL))


def _final_kernel(dcur_ref, dnext_ref, ys_ref, gate_ref, x1_ref, gt_ref, g_ref, o_ref, sl_ref, sem, *, tm, steps):
    g = pl.program_id(0) * pl.num_programs(1) + pl.program_id(1)
    cur = g % 2

    def request(d_ref, slot):
        def body(t, carry):
            for k in range(TOP_K):
                j = t * TOP_K + k
                pltpu.make_async_copy(
                    ys_ref.at[pl.ds(pl.multiple_of(d_ref[j] * ROW_TILES, ROW_TILES), ROW_TILES), :],
                    sl_ref.at[slot, pl.ds(pl.multiple_of(j * ROW_TILES, ROW_TILES), ROW_TILES), :],
                    sem.at[slot]).start(priority=k % DMA_QUEUES)
            return carry

        lax.fori_loop(0, tm, body, 0)

    @pl.when(g == 0)
    def _():
        request(dcur_ref, 0)

    @pl.when(g + 1 < steps)
    def _():
        request(dnext_ref, 1 - cur)

    pltpu.make_async_copy(ys_ref.at[pl.ds(0, tm * TOP_K * ROW_TILES), :], sl_ref.at[cur], sem.at[cur]).wait()

    gates = gate_ref[...]
    per_tok = TOP_K * ROW_TILES
    cols = []
    for s in range(ROW_TILES):
        acc = None
        for j in range(TOP_K):
            part = sl_ref[cur, pl.ds(j * ROW_TILES + s, tm, stride=per_tok), :] * gates[:, j:j + 1]
            acc = part if acc is None else acc + part
        cols.append(acc)
    f = jnp.concatenate(cols, axis=-1)
    o_ref[0] = x1_ref[0] + gt_ref[0] * _rms(f, g_ref[...])


def _final(ys, dest_flat, gates, x1, gt, g_post_ffn, tok_off, tm, per_token):
    bsz, t, _ = x1.shape
    tiles = t // tm
    off = tok_off // tm
    steps = bsz * tiles
    kern = functools.partial(_final_kernel, tm=tm, steps=steps)

    def dspec(ahead):
        return pl.BlockSpec((tm * TOP_K,), lambda b, i: (off + jnp.minimum(b * tiles + i + ahead, steps - 1),),
                            memory_space=pltpu.SMEM)

    return pl.pallas_call(
        kern,
        grid=(bsz, tiles),
        in_specs=[dspec(0), dspec(1),
                  pl.BlockSpec(memory_space=pl.ANY),
                  pl.BlockSpec((tm, TOP_K), lambda b, i: (off + b * tiles + i, 0)),
                  pl.BlockSpec((1, tm, D_MODEL), lambda b, i: (b, i, 0)),
                  _mod_spec(per_token, tm),
                  pl.BlockSpec((1, D_MODEL), lambda b, i: (0, 0))],
        out_specs=pl.BlockSpec((1, tm, D_MODEL), lambda b, i: (b, i, 0)),
        out_shape=jax.ShapeDtypeStruct((bsz, t, D_MODEL), F32),
        scratch_shapes=[pltpu.VMEM((2, tm * TOP_K * ROW_TILES, LANES), F32), pltpu.SemaphoreType.DMA((2,))],
        compiler_params=_cparams(("arbitrary", "arbitrary")),
        name="final",
    )(dest_flat, dest_flat, ys, gates, x1, gt, g_post_ffn.reshape(1, D_MODEL))


def _moe(ht_a, ht_b, gsel, rank, counts, w_gu, b_gu, w_down, b_down, tm, bm):
    n = gsel.shape[0]
    cnt = counts[0].astype(jnp.int32)
    nblk = (cnt + bm - 1) // bm
    blk_end = jnp.cumsum(nblk)
    blk_start = blk_end - nblk
    n_blocks = -(-(n * TOP_K) // bm) + N_EXPERTS
    blocks = jnp.arange(n_blocks, dtype=jnp.int32)
    block_expert = jnp.minimum(jnp.sum((blk_end[None, :] <= blocks[:, None]).astype(jnp.int32), axis=1),
                               N_EXPERTS - 1).astype(jnp.int32)
    ids = jnp.arange(N_EXPERTS, dtype=jnp.int32)
    later = jnp.where((nblk[None, :] > 0) & (ids[None, :] > ids[:, None]), ids[None, :], N_EXPERTS)
    nearest = jnp.min(later, axis=1)
    next_used = jnp.where(nearest < N_EXPERTS, nearest, ids).astype(jnp.int32)
    n_used = blk_end[-1:].astype(jnp.int32)
    row_start = (blk_start * bm).astype(jnp.int32)
    row_end = (blk_end * bm).astype(jnp.int32)
    dest, gates = _dest(gsel, rank, row_start.astype(F32).reshape(1, N_EXPERTS), DEST_TILE)
    dest_flat = dest.reshape(n * TOP_K)
    xs = _scatter_rows(ht_a, ht_b, dest_flat, cnt, row_start, row_end, n_used, n_blocks, SCATTER_TILE, bm)
    owns = (blocks[:, None] >= blk_start[None, :]) & (blocks[:, None] < blk_end[None, :])

    def per_block(table):
        return jnp.sum(jnp.where(owns, table[None, :], 0), axis=1).astype(jnp.int32)

    block_valid = jnp.clip(per_block(cnt) - (blocks - per_block(blk_start)) * bm, 0, bm).astype(jnp.int32)
    ys = _experts(xs, block_expert, per_block(next_used), block_valid, n_used, w_gu, b_gu, w_down, b_down, bm)
    return ys, dest_flat, gates


def _mix_path(x, mod, pool_buf, conv_buf, s0, start_pos, counts0, lp, tm, nb, per_token):
    (g_pre_mix, g_post_mix, g_pre_ffn, w_in_p, w_bd, pool_scale, conv_w, alog_row, dtb_row, norm_w,
     w_out_b, router_w, router_b) = lp
    bsz, t, _ = x.shape
    sh_mix, sc_mix, gt_mix, sh_ffn, sc_ffn, gt_ffn = (mod[:, i] for i in range(6))

    def shape_mod(m):
        if per_token:
            return jnp.repeat(m, t, axis=0).reshape(1, bsz * t, D_MODEL)
        return m.reshape(bsz, 1, D_MODEL)

    sh_mix, sc_mix, gt_mix, sh_ffn, sc_ffn, gt_ffn = map(shape_mod, (sh_mix, sc_mix, gt_mix, sh_ffn, sc_ffn, gt_ffn))
    xf = x.reshape(1, bsz * t, D_MODEL) if per_token else x
    proj = _in_proj(xf, sc_mix, sh_mix, g_pre_mix, w_in_p, PROJ_TILE, per_token).reshape(bsz, t, PROJ_WIDTH)
    tseq = min(tm, t)
    o_gdn, s_new = _gdn(proj, conv_buf, s0, conv_w, alog_row, dtb_row, norm_w, nb, tseq)
    pool_out = _pool(proj, pool_buf, w_bd, pool_scale, nb, tseq, start_pos)
    fl = (lambda a: a.reshape(1, bsz * t, a.shape[-1])) if per_token else (lambda a: a)
    x1, ht, gsel, rank, counts = _out_proj(xf, fl(pool_out), fl(o_gdn), gt_mix, sc_ffn, sh_ffn, g_post_mix,
                                           g_pre_ffn, w_out_b, router_w, router_b, counts0, tm, per_token)

    def last_rows(buf, off, width):
        keep = buf.shape[1]
        if t >= keep:
            return proj[:, t - keep:, off:off + width]
        return jnp.concatenate([buf[:, t:], proj[:, :, off:off + width]], axis=1)

    pool_new = last_rows(pool_buf, OFF_U, POOL_WIDTH)
    conv_new = last_rows(conv_buf, OFF_QKV, CONV_CH)
    return x1, ht, gsel, rank, counts, gt_ffn, pool_new, conv_new, s_new


def kernel(x_prompt, x_sample, c_prompt, c_sample, state_pool, state_conv, state_ssm, w_ada, b_ada, g_pre_mix, g_post_mix, g_pre_ffn, g_post_ffn, w_in, w_pool, pool_scale, conv_w, a_log, dt_bias, gdn_norm_w, w_out, router_w, router_b, w_gu, b_gu, w_down, b_down):
    depth = w_ada.shape[0]
    bp, tp, _ = x_prompt.shape
    bs, ts, _ = x_sample.shape
    tm = TOKEN_TILE
    y_p, y_s = x_prompt, x_sample
    outs = [[] for _ in range(6)]
    for l in range(depth):
        mod = _ada(jnp.concatenate([c_prompt, c_sample], axis=0), w_ada[l], b_ada[l]).reshape(bp + bs, 6, D_MODEL)
        w_in_p = w_in[l]
        w_bd = jax.scipy.linalg.block_diag(*[w_pool[l, g] for g in range(len(POOL_WINDOWS))])
        pad_row = lambda v: jnp.pad(v, (0, LANES - v.shape[0])).reshape(1, LANES)
        lp = (g_pre_mix[l], g_post_mix[l], g_pre_ffn[l], w_in_p, w_bd, pool_scale[l], conv_w[l],
              pad_row(a_log[l]), pad_row(dt_bias[l]), gdn_norm_w[l].reshape(1, HEAD_DIM),
              w_out[l].astype(BF16), router_w[l], router_b[l])
        zero_pool = jnp.zeros((bp, POOL_BUF, POOL_WIDTH), F32)
        zero_conv = jnp.zeros((bp, CONV_WIDTH - 1, CONV_CH), F32)
        zero_ssm = (jnp.zeros((1, bp, GDN_HEADS, HEAD_DIM, HEAD_DIM), F32), 0)
        no_counts = jnp.zeros((1, N_EXPERTS), F32)
        x1p, htp, gp, rp, cnt_p, gtp, pp, cp, sp = _mix_path(
            y_p, mod[:bp], zero_pool, zero_conv, zero_ssm, 0, no_counts, lp, tm, 1, False)
        x1s, hts, gs, rs, counts, gts, ps, cs, ss = _mix_path(
            y_s, mod[bp:], state_pool[l], state_conv[l], (state_ssm, l), PAST_LEN, cnt_p, lp, tm, 8, True)
        gsel = jnp.concatenate([gp, gs], axis=0)
        rank = jnp.concatenate([rp, rs], axis=0)
        ys, dest_flat, gates = _moe(htp, hts, gsel, rank, counts, w_gu[l], b_gu[l], w_down[l], b_down[l],
                                    tm, MOE_BLOCK)
        y_p = _final(ys, dest_flat, gates, x1p, gtp, g_post_ffn[l], 0, tm, False)
        y_s = _final(ys, dest_flat, gates, x1s, gts, g_post_ffn[l], bp * tp, tm, True).reshape(bs, ts, D_MODEL)
        for lst, val in zip(outs, (pp, cp, sp, ps, cs, ss)):
            lst.append(val)
    return (y_p, y_s) + tuple(o[0][None] if depth == 1 else jnp.stack(o) for o in outs)
```

```python
import functools
import math

import jax
import jax.numpy as jnp
from jax import lax
from jax.experimental import pallas as pl
from jax.experimental.pallas import tpu as pltpu

F32 = jnp.float32
BF16 = jnp.bfloat16

D_MODEL = 1024
PAST_LEN = 16384
POOL_WINDOWS = (2, 4, 8, 16)
POOL_WIDTH = D_MODEL // 4
POOL_GROUP = POOL_WIDTH // len(POOL_WINDOWS)
POOL_BUF = max(POOL_WINDOWS) - 1
GDN_WIDTH = D_MODEL - POOL_WIDTH
HEAD_DIM = 128
GDN_HEADS = GDN_WIDTH // HEAD_DIM
CONV_WIDTH = 4
CONV_CH = 3 * GDN_WIDTH
GDN_CHUNK = 64
N_EXPERTS = 32
TOP_K = 4
D_FF = D_MODEL
SWIGLU_LIMIT = 7.0
SWIGLU_ALPHA = 1.702
EPS = 1e-6

SUBLANES = 8
LANES = 128
ROW_TILES = D_MODEL // LANES
OFF_QKV = 0
OFF_Z = OFF_QKV + CONV_CH
OFF_U = OFF_Z + GDN_WIDTH
OFF_AB = OFF_U + POOL_WIDTH
PROJ_WIDTH = OFF_AB + LANES
assert OFF_Z % GDN_WIDTH == 0 and OFF_U % POOL_WIDTH == 0 and OFF_AB % LANES == 0
TOKEN_TILE = 256
PROJ_TILE = 512
SCATTER_TILE = 512
DEST_TILE = 512
GDN_ITEM_GROUPS = 1
MOE_BLOCK = 512
DMA_QUEUES = 2
VMEM_LIMIT = 56 * 1024 * 1024


def _cparams(sem):
    return pltpu.CompilerParams(dimension_semantics=sem, vmem_limit_bytes=VMEM_LIMIT)


def _dot(a, b):
    return jnp.dot(a.astype(BF16), b.astype(BF16), preferred_element_type=F32)


def _dot_nt(a, b):
    return lax.dot_general(a.astype(BF16), b.astype(BF16), (((1,), (1,)), ((), ())),
                           preferred_element_type=F32)


def _silu(x):
    return x * jax.nn.sigmoid(x)


def _rms(x, g):
    return x * lax.rsqrt(jnp.mean(x * x, axis=-1, keepdims=True) + EPS) * g


def _ada_kernel(c_ref, w_ref, b_ref, o_ref):
    o_ref[...] = _dot(_silu(c_ref[...]), w_ref[...]) + b_ref[...]


def _ada(c, w_ada, b_ada):
    n = c.shape[0]
    cols = w_ada.shape[1]
    blk = D_MODEL
    return pl.pallas_call(
        _ada_kernel,
        grid=(cols // blk,),
        in_specs=[pl.BlockSpec((n, D_MODEL), lambda j: (0, 0)),
                  pl.BlockSpec((D_MODEL, blk), lambda j: (0, j)),
                  pl.BlockSpec((1, blk), lambda j: (0, j))],
        out_specs=pl.BlockSpec((n, blk), lambda j: (0, j)),
        out_shape=jax.ShapeDtypeStruct((n, cols), F32),
        compiler_params=_cparams(("arbitrary",)),
        name="ada",
    )(c, w_ada, b_ada.reshape(1, cols))


def _in_kernel(x_ref, sc_ref, sh_ref, g_ref, wt_ref, o_ref, wb):
    @pl.when((pl.program_id(0) == 0) & (pl.program_id(1) == 0))
    def _():
        o1, o2, o3 = POOL_WIDTH, POOL_WIDTH + CONV_CH, POOL_WIDTH + CONV_CH + GDN_WIDTH
        wb[OFF_QKV:OFF_QKV + CONV_CH] = wt_ref[o1:o2].astype(BF16)
        wb[OFF_Z:OFF_Z + GDN_WIDTH] = wt_ref[o2:o3].astype(BF16)
        wb[OFF_U:OFF_U + POOL_WIDTH] = wt_ref[:o1].astype(BF16)
        tail = wt_ref.shape[0] - o3
        wb[OFF_AB:] = jnp.concatenate(
            [wt_ref[o3:], jnp.zeros((LANES - tail, D_MODEL), F32)], axis=0).astype(BF16)

    h = _rms(x_ref[0], g_ref[...]) * (1.0 + sc_ref[0]) + sh_ref[0]
    o_ref[0] = _dot_nt(h, wb[...])


def _mod_spec(per_token, tm):
    if per_token:
        return pl.BlockSpec((1, tm, D_MODEL), lambda b, t: (0, t, 0))
    return pl.BlockSpec((1, 1, D_MODEL), lambda b, t: (b, 0, 0))


def _in_proj(x, sc, sh, g, w_in, tm, per_token):
    bsz, t, _ = x.shape
    return pl.pallas_call(
        _in_kernel,
        grid=(bsz, t // tm),
        in_specs=[pl.BlockSpec((1, tm, D_MODEL), lambda b, i: (b, i, 0)),
                  _mod_spec(per_token, tm), _mod_spec(per_token, tm),
                  pl.BlockSpec((1, D_MODEL), lambda b, i: (0, 0)),
                  pl.BlockSpec(w_in.shape, lambda b, i: (0, 0), pipeline_mode=pl.Buffered(1))],
        out_specs=pl.BlockSpec((1, tm, PROJ_WIDTH), lambda b, i: (b, i, 0)),
        out_shape=jax.ShapeDtypeStruct((bsz, t, PROJ_WIDTH), F32),
        scratch_shapes=[pltpu.VMEM((PROJ_WIDTH, D_MODEL), BF16)],
        compiler_params=_cparams(("arbitrary", "arbitrary")),
        name="in_proj",
    )(x, sc, sh, g.reshape(1, D_MODEL), w_in)


def _cumsum_rows(x):
    n = x.shape[0]
    row = lax.broadcasted_iota(jnp.int32, x.shape, 0)
    s = 1
    while s < n:
        x = x + jnp.where(row >= s, pltpu.roll(x, s, axis=0), 0.0)
        s *= 2
    return x


def _gdn_kernel(qkv_ref, z_ref, ab_ref, cs_ref, cw_ref, s0_ref, alog_ref, dtb_ref, nw_ref, o_ref, sn_ref,
                s_scr, prev_scr, *, nb, tm, tmp, chunk, group_items):
    ti = pl.program_id(1)
    nt = pl.num_programs(1)
    n_chunks = tmp // chunk
    n_sq = int(math.log2(chunk)) - 1

    @pl.when(ti == 0)
    def _():
        s_scr[...] = s0_ref[...]
        for b in range(nb):
            prev_scr[b] = jnp.concatenate(
                [jnp.zeros((SUBLANES - (CONV_WIDTH - 1), CONV_CH), F32), cs_ref[b]], axis=0)

    rowi = lax.broadcasted_iota(jnp.int32, (tmp, LANES), 0)
    ri = lax.broadcasted_iota(jnp.int32, (chunk, chunk), 0)
    ci = lax.broadcasted_iota(jnp.int32, (chunk, chunk), 1)
    tril = ri >= ci
    strict = ri > ci
    eye = (ri == ci).astype(F32)
    neg_a = -jnp.exp(alog_ref[...])
    dtb = dtb_ref[...]
    nw = nw_ref[...]

    def padrows(x):
        if tm == tmp:
            return x
        return jnp.concatenate([x, jnp.zeros((tmp - tm, x.shape[1]), F32)], axis=0)

    cw = cw_ref[...]
    valid = rowi < tm
    items = []
    for b in range(nb):
        x = padrows(qkv_ref[b])
        e = jnp.concatenate([prev_scr[b], x], axis=0)
        acc = x * cw[CONV_WIDTH - 1:CONV_WIDTH]
        for d in range(1, CONV_WIDTH):
            acc = acc + pltpu.roll(e, d, axis=0)[SUBLANES:] * cw[CONV_WIDTH - 1 - d:CONV_WIDTH - d]
        if tm >= SUBLANES:
            prev_scr[b] = x[tm - SUBLANES:tm]
        y = _silu(acc)
        ab = padrows(ab_ref[b])
        g_all = jnp.where(valid, neg_a * jax.nn.softplus(ab + dtb), 0.0)
        beta_all = jnp.where(valid, jax.nn.sigmoid(ab), 0.0)
        for h in range(GDN_HEADS):
            qc = y[:, h * HEAD_DIM:(h + 1) * HEAD_DIM]
            kc = y[:, GDN_WIDTH + h * HEAD_DIM:GDN_WIDTH + (h + 1) * HEAD_DIM]
            v = y[:, 2 * GDN_WIDTH + h * HEAD_DIM:2 * GDN_WIDTH + (h + 1) * HEAD_DIM]
            q = qc * lax.rsqrt(jnp.sum(qc * qc, axis=-1, keepdims=True) + EPS) * (HEAD_DIM ** -0.5)
            k = kc * lax.rsqrt(jnp.sum(kc * kc, axis=-1, keepdims=True) + EPS)
            gb = jnp.broadcast_to(g_all[:, h:h + 1], (tmp, LANES))
            bb = jnp.broadcast_to(beta_all[:, GDN_HEADS + h:GDN_HEADS + h + 1], (tmp, LANES))
            for c in range(n_chunks):
                sl = slice(c * chunk, (c + 1) * chunk)
                items.append(dict(b=b, h=h, c=c, q=q[sl], k=k[sl], v=v[sl], bb=bb[sl], gb=gb[sl]))

    def within_chunk(group):
        for it in group:
            gcb = _cumsum_rows(it["gb"])
            gct = jnp.transpose(gcb)[:chunk]
            diff = gcb[:, :chunk] - gct
            it["decay"] = jnp.where(tril, jnp.exp(jnp.where(tril, diff, 0.0)), 0.0)
            it["kb"] = it["k"] * it["bb"]
            it["eg"] = jnp.exp(gcb)
            gl = gcb[chunk - 1:chunk]
            it["egl"] = jnp.exp(gl)
            it["kdt"] = jnp.transpose(it["k"] * jnp.exp(gl - gcb))
        for it in group:
            kq = _dot_nt(jnp.concatenate([it["kb"], it["q"]], axis=0), it["k"])
            it["a"] = jnp.where(strict, kq[:chunk] * it["decay"], 0.0)
            it["qk"] = jnp.where(tril, kq[chunk:] * it["decay"], 0.0)
        for it in group:
            it["inv"] = eye - it["a"]
            it["p"] = _dot(it["a"], it["a"])
        for lvl in range(n_sq):
            for it in group:
                if lvl < n_sq - 1:
                    r = _dot(jnp.concatenate([it["inv"], it["p"]], axis=0), it["p"])
                    it["inv"] = it["inv"] + r[:chunk]
                    it["p"] = r[chunk:]
                else:
                    it["inv"] = it["inv"] + _dot(it["inv"], it["p"])
        for it in group:
            it["wu"] = _dot(it["inv"], jnp.concatenate([it["kb"] * it["eg"], it["v"] * it["bb"]], axis=1))
        for it in group:
            mn = _dot(it["kdt"], it["wu"])
            qo = _dot(it["qk"], it["wu"])
            it["mp"], it["nc"] = mn[:, :HEAD_DIM], mn[:, HEAD_DIM:]
            it["qp"] = it["q"] * it["eg"] - qo[:, :HEAD_DIM]
            it["oc"] = qo[:, HEAD_DIM:]

    for g0 in range(0, len(items), group_items):
        within_chunk(items[g0:g0 + group_items])

    state = {(b, h): s_scr[b, h] for b in range(nb) for h in range(GDN_HEADS)}
    for c in range(n_chunks):
        for it in items:
            if it["c"] != c:
                continue
            s = state[(it["b"], it["h"])]
            r = _dot(jnp.concatenate([it["qp"], it["mp"]], axis=0), s)
            it["o"] = r[:chunk] + it["oc"]
            state[(it["b"], it["h"])] = s * it["egl"] - r[chunk:] + it["nc"]
    for (b, h), s in state.items():
        s_scr[b, h] = s

    for b in range(nb):
        zf = padrows(z_ref[b])
        for h in range(GDN_HEADS):
            outs = [it["o"] for it in items if it["b"] == b and it["h"] == h]
            o = outs[0] if n_chunks == 1 else jnp.concatenate(outs, axis=0)
            o = (o * lax.rsqrt(jnp.mean(o * o, axis=-1, keepdims=True) + EPS) * nw
                 * _silu(zf[:, h * HEAD_DIM:(h + 1) * HEAD_DIM]))
            o_ref[b, :, h * HEAD_DIM:(h + 1) * HEAD_DIM] = o[:tm]

    @pl.when(ti == nt - 1)
    def _():
        sn_ref[...] = s_scr[...]


def _gdn(proj, state_conv, s0, conv_w, alog_row, dtb_row, norm_w, nb, tm):
    bsz, t, _ = proj.shape
    tmp = max(tm, SUBLANES)
    chunk = min(GDN_CHUNK, tmp)
    row = pl.BlockSpec((1, LANES), lambda b, i: (0, 0))
    s_all, layer = s0
    sspec = pl.BlockSpec((nb, GDN_HEADS, HEAD_DIM, HEAD_DIM), lambda b, i: (b, 0, 0, 0))
    s_in = pl.BlockSpec((None, nb, GDN_HEADS, HEAD_DIM, HEAD_DIM), lambda b, i: (layer, b, 0, 0, 0))
    n_items = nb * GDN_HEADS * (tmp // chunk)
    kern = functools.partial(_gdn_kernel, nb=nb, tm=tm, tmp=tmp, chunk=chunk,
                             group_items=n_items // GDN_ITEM_GROUPS)
    return pl.pallas_call(
        kern,
        grid=(bsz // nb, t // tm),
        in_specs=[pl.BlockSpec((nb, tm, CONV_CH), lambda b, i: (b, i, OFF_QKV // CONV_CH)),
                  pl.BlockSpec((nb, tm, GDN_WIDTH), lambda b, i: (b, i, OFF_Z // GDN_WIDTH)),
                  pl.BlockSpec((nb, tm, LANES), lambda b, i: (b, i, OFF_AB // LANES)),
                  pl.BlockSpec((nb, CONV_WIDTH - 1, CONV_CH), lambda b, i: (b, 0, 0)),
                  pl.BlockSpec((CONV_WIDTH, CONV_CH), lambda b, i: (0, 0)),
                  s_in, row, row, row],
        out_specs=[pl.BlockSpec((nb, tm, GDN_WIDTH), lambda b, i: (b, i, 0)), sspec],
        out_shape=[jax.ShapeDtypeStruct((bsz, t, GDN_WIDTH), F32),
                   jax.ShapeDtypeStruct(s_all.shape[1:], F32)],
        scratch_shapes=[pltpu.VMEM((nb, GDN_HEADS, HEAD_DIM, HEAD_DIM), F32),
                        pltpu.VMEM((nb, SUBLANES, CONV_CH), F32)],
        compiler_params=_cparams(("arbitrary", "arbitrary")),
        name="gdn",
    )(proj, proj, proj, state_conv, conv_w, s_all, alog_row, dtb_row, norm_w)


def _pool_kernel(u_ref, st_ref, w_ref, sc_ref, o_ref, prev_scr, *, nb, tm, tmp, start_pos):
    ti = pl.program_id(1)
    halo = 2 * SUBLANES

    @pl.when(ti == 0)
    def _():
        for b in range(nb):
            prev_scr[b] = jnp.concatenate([jnp.zeros((halo - POOL_BUF, POOL_WIDTH), F32), st_ref[b]], axis=0)

    lane = lax.broadcasted_iota(jnp.int32, (tmp, POOL_WIDTH), 1)
    rowi = lax.broadcasted_iota(jnp.int32, (tmp, POOL_WIDTH), 0)
    pos1 = start_pos + ti * tm + rowi + 1
    grp = lane // POOL_GROUP
    win = jnp.where(grp == 0, POOL_WINDOWS[0],
                    jnp.where(grp == 1, POOL_WINDOWS[1], jnp.where(grp == 2, POOL_WINDOWS[2], POOL_WINDOWS[3])))
    cnt = jnp.minimum(pos1, win).astype(F32)
    for b in range(nb):
        x = u_ref[b]
        if tm != tmp:
            x = jnp.concatenate([x, jnp.zeros((tmp - tm, POOL_WIDTH), F32)], axis=0)
        e = jnp.concatenate([prev_scr[b], x], axis=0)
        sums = []
        s = e
        for w in POOL_WINDOWS:
            s = s + pltpu.roll(s, w // 2, axis=0)
            sums.append(s[halo:])
        sel = jnp.where(grp == 0, sums[0], jnp.where(grp == 1, sums[1], jnp.where(grp == 2, sums[2], sums[3])))
        pooled = sel / cnt - x
        out = _dot(pooled, w_ref[...]) * sc_ref[...]
        o_ref[b] = out[:tm]
        if tm >= halo:
            prev_scr[b] = x[tm - halo:tm]


def _pool(proj, state_pool, w_bd, pool_scale, nb, tm, start_pos):
    bsz, t, _ = proj.shape
    tmp = max(tm, SUBLANES)
    kern = functools.partial(_pool_kernel, nb=nb, tm=tm, tmp=tmp, start_pos=start_pos)
    return pl.pallas_call(
        kern,
        grid=(bsz // nb, t // tm),
        in_specs=[pl.BlockSpec((nb, tm, POOL_WIDTH), lambda b, i: (b, i, OFF_U // POOL_WIDTH)),
                  pl.BlockSpec((nb, POOL_BUF, POOL_WIDTH), lambda b, i: (b, 0, 0)),
                  pl.BlockSpec((POOL_WIDTH, POOL_WIDTH), lambda b, i: (0, 0)),
                  pl.BlockSpec((1, POOL_WIDTH), lambda b, i: (0, 0))],
        out_specs=pl.BlockSpec((nb, tm, POOL_WIDTH), lambda b, i: (b, i, 0)),
        out_shape=jax.ShapeDtypeStruct((bsz, t, POOL_WIDTH), F32),
        scratch_shapes=[pltpu.VMEM((nb, 2 * SUBLANES, POOL_WIDTH), F32)],
        compiler_params=_cparams(("arbitrary", "arbitrary")),
        name="pool",
    )(proj, state_pool, w_bd, pool_scale.reshape(1, POOL_WIDTH))


def _out_kernel(x_ref, po_ref, og_ref, gt_ref, sc_ref, sh_ref, gpm_ref, gpf_ref, wo_ref, rw_ref, rb_ref, c0_ref,
                x1_ref, ht_ref, g_ref, r_ref, cnt_ref, carry, *, tm):
    @pl.when((pl.program_id(0) == 0) & (pl.program_id(1) == 0))
    def _():
        carry[...] = c0_ref[...]

    mix = _dot(po_ref[0], wo_ref[:POOL_WIDTH]) + _dot(og_ref[0], wo_ref[POOL_WIDTH:])
    x1 = x_ref[0] + gt_ref[0] * _rms(mix, gpm_ref[...])
    x1_ref[0] = x1
    h = _rms(x1, gpf_ref[...]) * (1.0 + sc_ref[0]) + sh_ref[0]
    for s in range(ROW_TILES):
        ht_ref[pl.ds(s, tm, stride=ROW_TILES), :] = h[:, s * LANES:(s + 1) * LANES]
    logits = _dot(h, rw_ref[...]) + rb_ref[...]
    lane = lax.broadcasted_iota(jnp.int32, logits.shape, 1).astype(F32)
    work = logits
    sel = jnp.zeros(logits.shape, jnp.bool_)
    top = None
    den = jnp.zeros((tm, 1), F32)
    for j in range(TOP_K):
        m = jnp.max(work, axis=-1, keepdims=True)
        idx = jnp.min(jnp.where(work == m, lane, float(N_EXPERTS)), axis=-1, keepdims=True)
        hit = lane == idx
        sel = sel | hit
        work = jnp.where(hit, -jnp.inf, work)
        if j == 0:
            top = m
        den = den + jnp.exp(m - top)
    gates = jnp.exp(logits - top) / den
    g_ref[...] = jnp.where(sel, gates, -1.0)
    mask = sel.astype(F32)
    ri = lax.broadcasted_iota(jnp.int32, (tm, tm), 0)
    ci = lax.broadcasted_iota(jnp.int32, (tm, tm), 1)
    c = carry[...]
    r_ref[...] = _dot((ri > ci).astype(F32), mask) + c
    c = c + jnp.sum(mask, axis=0, keepdims=True)
    carry[...] = c
    cnt_ref[...] = c


def _out_proj(x, pool_out, o_gdn, gt, sc, sh, g_post_mix, g_pre_ffn, w_out_b, router_w, router_b, counts0,
              tm, per_token):
    bsz, t, _ = x.shape
    n = bsz * t
    tiles = t // tm

    def tok(width):
        return pl.BlockSpec((1, tm, width), lambda b, i: (b, i, 0))

    def const(shape):
        return pl.BlockSpec(shape, lambda b, i: tuple(0 for _ in shape))

    kern = functools.partial(_out_kernel, tm=tm)
    return pl.pallas_call(
        kern,
        grid=(bsz, tiles),
        in_specs=[tok(D_MODEL), tok(POOL_WIDTH), tok(GDN_WIDTH),
                  _mod_spec(per_token, tm), _mod_spec(per_token, tm), _mod_spec(per_token, tm),
                  const((1, D_MODEL)), const((1, D_MODEL)), const((D_MODEL, D_MODEL)),
                  const((D_MODEL, N_EXPERTS)), const((1, N_EXPERTS)), const((1, N_EXPERTS))],
        out_specs=[tok(D_MODEL),
                   pl.BlockSpec((tm * ROW_TILES, LANES), lambda b, i: (b * tiles + i, 0)),
                   pl.BlockSpec((tm, N_EXPERTS), lambda b, i: (b * tiles + i, 0)),
                   pl.BlockSpec((tm, N_EXPERTS), lambda b, i: (b * tiles + i, 0)),
                   const((1, N_EXPERTS))],
        out_shape=[jax.ShapeDtypeStruct((bsz, t, D_MODEL), F32),
                   jax.ShapeDtypeStruct((n * ROW_TILES, LANES), F32),
                   jax.ShapeDtypeStruct((n, N_EXPERTS), F32),
                   jax.ShapeDtypeStruct((n, N_EXPERTS), F32),
                   jax.ShapeDtypeStruct((1, N_EXPERTS), F32)],
        scratch_shapes=[pltpu.VMEM((1, N_EXPERTS), F32)],
        compiler_params=_cparams(("arbitrary", "arbitrary")),
        name="out_proj",
    )(x, pool_out, o_gdn, gt, sc, sh, g_post_mix.reshape(1, D_MODEL), g_pre_ffn.reshape(1, D_MODEL),
      w_out_b, router_w, router_b.reshape(1, N_EXPERTS), counts0)


def _dest_kernel(g_ref, r_ref, start_ref, d_ref, gt_ref, *, tm):
    g = g_ref[...]
    mask = g >= 0.0
    maskf = mask.astype(F32)
    dest = start_ref[...] + r_ref[...]
    ri = lax.broadcasted_iota(jnp.int32, (N_EXPERTS, N_EXPERTS), 0)
    ci = lax.broadcasted_iota(jnp.int32, (N_EXPERTS, N_EXPERTS), 1)
    slot = _dot(maskf, (ri < ci).astype(F32))
    lane4 = lax.broadcasted_iota(jnp.int32, (tm, TOP_K), 1)
    d_out = jnp.zeros((tm, TOP_K), F32)
    g_out = jnp.zeros((tm, TOP_K), F32)
    for j in range(TOP_K):
        pick = mask & (slot == float(j))
        dj = jnp.sum(jnp.where(pick, dest, 0.0), axis=-1, keepdims=True)
        gj = jnp.sum(jnp.where(pick, g, 0.0), axis=-1, keepdims=True)
        d_out = jnp.where(lane4 == j, dj, d_out)
        g_out = jnp.where(lane4 == j, gj, g_out)
    d_ref[...] = d_out.astype(jnp.int32)
    gt_ref[...] = g_out


def _dest(gsel, rank, start_row, tm):
    n = gsel.shape[0]
    kern = functools.partial(_dest_kernel, tm=tm)
    return pl.pallas_call(
        kern,
        grid=(n // tm,),
        in_specs=[pl.BlockSpec((tm, N_EXPERTS), lambda i: (i, 0)),
                  pl.BlockSpec((tm, N_EXPERTS), lambda i: (i, 0)),
                  pl.BlockSpec((1, N_EXPERTS), lambda i: (0, 0))],
        out_specs=[pl.BlockSpec((tm, TOP_K), lambda i: (i, 0)),
                   pl.BlockSpec((tm, TOP_K), lambda i: (i, 0))],
        out_shape=[jax.ShapeDtypeStruct((n, TOP_K), jnp.int32),
                   jax.ShapeDtypeStruct((n, TOP_K), F32)],
        compiler_params=_cparams(("arbitrary",)),
        name="dest",
    )(gsel, rank, start_row)


def _row_copy(src, dst, s_row, d_row, sem):
    return pltpu.make_async_copy(src.at[pl.ds(pl.multiple_of(s_row * ROW_TILES, ROW_TILES), ROW_TILES), :],
                                 dst.at[pl.ds(pl.multiple_of(d_row * ROW_TILES, ROW_TILES), ROW_TILES), :], sem)


def _scatter_kernel(cnt_ref, start_ref, end_ref, nu_ref, dest_ref, ha_ref, hb_ref, xs_ref, sem, *,
                    tm, bm, n_blocks, tiles_a):
    i = pl.program_id(0)

    def scatter_from(ht_ref):
        def body(t, carry):
            src = ht_ref.at[pl.ds(pl.multiple_of(t * ROW_TILES, ROW_TILES), ROW_TILES), :]
            for k in range(TOP_K):
                d = dest_ref[t * TOP_K + k]
                pltpu.make_async_copy(
                    src, xs_ref.at[pl.ds(pl.multiple_of(d * ROW_TILES, ROW_TILES), ROW_TILES), :],
                    sem).start(priority=k % DMA_QUEUES)
            return carry

        lax.fori_loop(0, tm, body, 0)

    @pl.when(i < tiles_a)
    def _():
        scatter_from(ha_ref)

    @pl.when(i >= tiles_a)
    def _():
        scatter_from(hb_ref)

    ht_ref = hb_ref

    def tile_copy(u):
        return pltpu.make_async_copy(
            ht_ref, xs_ref.at[pl.ds(pl.multiple_of(u * (tm * ROW_TILES), tm * ROW_TILES), tm * ROW_TILES), :], sem)

    for k in range(TOP_K):
        tile_copy(0).wait()

    @pl.when(i == pl.num_programs(0) - 1)
    def _():
        per = bm // tm

        def tail_start(u, carry):
            tile_copy(u).start()
            return carry

        def tail_wait(u, carry):
            tile_copy(u).wait()
            return carry

        lax.fori_loop(nu_ref[0] * per, n_blocks * per, tail_start, 0)
        lax.fori_loop(nu_ref[0] * per, n_blocks * per, tail_wait, 0)

        def per_expert(e, carry):
            lo = start_ref[e] + cnt_ref[e]
            hi = end_ref[e]

            def fill(r, c2):
                _row_copy(ht_ref, xs_ref, 0, r, sem).start()
                return c2

            lax.fori_loop(lo, hi, fill, 0)

            def drain(r, c2):
                _row_copy(ht_ref, xs_ref, 0, r, sem).wait()
                return c2

            lax.fori_loop(lo, hi, drain, 0)
            return carry

        lax.fori_loop(0, N_EXPERTS, per_expert, 0)


def _scatter_rows(ht_a, ht_b, dest_flat, cnt, start, end, n_used, n_blocks, tm, bm):
    rows = n_blocks * bm
    tiles_a, tiles_b = ht_a.shape[0] // (tm * ROW_TILES), ht_b.shape[0] // (tm * ROW_TILES)
    assert bm % tm == 0 and ht_a.shape[0] % (tm * ROW_TILES) == 0 and ht_b.shape[0] % (tm * ROW_TILES) == 0
    assert tiles_b >= 1
    kern = functools.partial(_scatter_kernel, tm=tm, bm=bm, n_blocks=n_blocks, tiles_a=tiles_a)
    return pl.pallas_call(
        kern,
        grid_spec=pltpu.PrefetchScalarGridSpec(
            num_scalar_prefetch=4,
            grid=(tiles_a + tiles_b,),
            in_specs=[pl.BlockSpec((tm * TOP_K,), lambda i, *_: (i,), memory_space=pltpu.SMEM),
                      pl.BlockSpec((tm * ROW_TILES, LANES), lambda i, *_: (jnp.minimum(i, tiles_a - 1), 0)),
                      pl.BlockSpec((tm * ROW_TILES, LANES), lambda i, *_: (jnp.maximum(i - tiles_a, 0), 0))],
            out_specs=pl.BlockSpec(memory_space=pl.ANY),
            scratch_shapes=[pltpu.SemaphoreType.DMA(())]),
        out_shape=jax.ShapeDtypeStruct((rows * ROW_TILES, LANES), F32),
        compiler_params=_cparams(("arbitrary",)),
        name="scatter_rows",
    )(cnt, start, end, n_used, dest_flat, ht_a, ht_b)


def _expert_kernel(be_ref, nx_ref, nv_ref, nu_ref, xs_ref, wg_ref, bg_ref, wd_ref, bd_ref, ys_ref,
                   wg_f, wd_f, wg_s, wd_s, wsem, *, bm):
    j = pl.program_id(0)
    prev = be_ref[jnp.maximum(j - 1, 0)]
    fresh = (j == 0) | (be_ref[j] != prev)
    active = j < nu_ref[0]

    def weight_copies(e):
        return (pltpu.make_async_copy(wg_ref.at[e], wg_f, wsem.at[0]),
                pltpu.make_async_copy(wd_ref.at[e], wd_f, wsem.at[1]))

    @pl.when(j == 0)
    def _():
        for c in weight_copies(be_ref[0]):
            c.start()

    @pl.when(active & fresh)
    def _():
        for c in weight_copies(be_ref[j]):
            c.wait()
        wg_s[...] = wg_f[...].astype(BF16)
        wd_s[...] = wd_f[...].astype(BF16)

        @pl.when(nx_ref[j] != be_ref[j])
        def _():
            for c in weight_copies(nx_ref[j]):
                c.start()

    def run_rows(m):
        x = jnp.concatenate([xs_ref[pl.ds(s, m, stride=ROW_TILES), :] for s in range(ROW_TILES)], axis=-1)
        xb = x.astype(BF16)
        gate = jnp.dot(xb, wg_s[:, :D_FF], preferred_element_type=F32) + bg_ref[0, :, :D_FF]
        up = jnp.dot(xb, wg_s[:, D_FF:], preferred_element_type=F32) + bg_ref[0, :, D_FF:]
        gate = jnp.minimum(gate, SWIGLU_LIMIT)
        up = jnp.clip(up, -SWIGLU_LIMIT, SWIGLU_LIMIT)
        act = (up + 1.0) * (gate * jax.nn.sigmoid(SWIGLU_ALPHA * gate))
        y = jnp.dot(act.astype(BF16), wd_s[...], preferred_element_type=F32) + bd_ref[0]
        for s in range(ROW_TILES):
            ys_ref[pl.ds(s, m, stride=ROW_TILES), :] = y[:, s * LANES:(s + 1) * LANES]

    half = bm // 2
    few = nv_ref[j] <= half

    @pl.when(active & jnp.logical_not(few))
    def _():
        run_rows(bm)

    @pl.when(active & few)
    def _():
        run_rows(half)
        ys_ref[pl.ds(half * ROW_TILES, half * ROW_TILES), :] = jnp.zeros((half * ROW_TILES, LANES), F32)

    @pl.when(jnp.logical_not(active))
    def _():
        ys_ref[...] = jnp.zeros_like(ys_ref)


def _experts(xs, block_expert, next_expert, block_valid, n_used, w_gu, b_gu, w_down, b_down, bm):
    rows = xs.shape[0] // ROW_TILES
    n_blocks = rows // bm

    def blk(j, be, nx, nv, nu):
        return (jnp.minimum(j, nu[0] - 1), 0)

    def bsel(j, be, nx, nv, nu):
        return (be[j], 0, 0)

    hbm = pl.BlockSpec(memory_space=pl.ANY)
    kern = functools.partial(_expert_kernel, bm=bm)
    return pl.pallas_call(
        kern,
        grid_spec=pltpu.PrefetchScalarGridSpec(
            num_scalar_prefetch=4,
            grid=(n_blocks,),
            in_specs=[pl.BlockSpec((bm * ROW_TILES, LANES), blk), hbm,
                      pl.BlockSpec((1, 1, 2 * D_FF), bsel), hbm,
                      pl.BlockSpec((1, 1, D_MODEL), bsel)],
            out_specs=pl.BlockSpec((bm * ROW_TILES, LANES), lambda j, be, nx, nv, nu: (j, 0)),
            scratch_shapes=[pltpu.VMEM((D_MODEL, 2 * D_FF), F32),
                            pltpu.VMEM((D_FF, D_MODEL), F32),
                            pltpu.VMEM((D_MODEL, 2 * D_FF), BF16),
                            pltpu.VMEM((D_FF, D_MODEL), BF16),
                            pltpu.SemaphoreType.DMA((2,))]),
        out_shape=jax.ShapeDtypeStruct((rows * ROW_TILES, LANES), F32),
        compiler_params=_cparams(("arbitrary",)),
        name="experts",
    )(block_expert, next_expert, block_valid, n_used, xs, w_gu, b_gu.reshape(N_EXPERTS, 1, 2 * D_FF), w_down,
      b_down.reshape(N_EXPERTS, 1, D_MODEL))


def _final_kernel(dcur_ref, dnext_ref, ys_ref, gate_ref, x1_ref, gt_ref, g_ref, o_ref, sl_ref, sem, *, tm, steps):
    g = pl.program_id(0) * pl.num_programs(1) + pl.program_id(1)
    cur = g % 2

    def request(d_ref, slot):
        def body(t, carry):
            for k in range(TOP_K):
                j = t * TOP_K + k
                pltpu.make_async_copy(
                    ys_ref.at[pl.ds(pl.multiple_of(d_ref[j] * ROW_TILES, ROW_TILES), ROW_TILES), :],
                    sl_ref.at[slot, pl.ds(pl.multiple_of(j * ROW_TILES, ROW_TILES), ROW_TILES), :],
                    sem.at[slot]).start(priority=k % DMA_QUEUES)
            return carry

        lax.fori_loop(0, tm, body, 0)

    @pl.when(g == 0)
    def _():
        request(dcur_ref, 0)

    @pl.when(g + 1 < steps)
    def _():
        request(dnext_ref, 1 - cur)

    pltpu.make_async_copy(ys_ref.at[pl.ds(0, tm * TOP_K * ROW_TILES), :], sl_ref.at[cur], sem.at[cur]).wait()

    gates = gate_ref[...]
    per_tok = TOP_K * ROW_TILES
    cols = []
    for s in range(ROW_TILES):
        acc = None
        for j in range(TOP_K):
            part = sl_ref[cur, pl.ds(j * ROW_TILES + s, tm, stride=per_tok), :] * gates[:, j:j + 1]
            acc = part if acc is None else acc + part
        cols.append(acc)
    f = jnp.concatenate(cols, axis=-1)
    o_ref[0] = x1_ref[0] + gt_ref[0] * _rms(f, g_ref[...])


def _final(ys, dest_flat, gates, x1, gt, g_post_ffn, tok_off, tm, per_token):
    bsz, t, _ = x1.shape
    tiles = t // tm
    off = tok_off // tm
    steps = bsz * tiles
    kern = functools.partial(_final_kernel, tm=tm, steps=steps)

    def dspec(ahead):
        return pl.BlockSpec((tm * TOP_K,), lambda b, i: (off + jnp.minimum(b * tiles + i + ahead, steps - 1),),
                            memory_space=pltpu.SMEM)

    return pl.pallas_call(
        kern,
        grid=(bsz, tiles),
        in_specs=[dspec(0), dspec(1),
                  pl.BlockSpec(memory_space=pl.ANY),
                  pl.BlockSpec((tm, TOP_K), lambda b, i: (off + b * tiles + i, 0)),
                  pl.BlockSpec((1, tm, D_MODEL), lambda b, i: (b, i, 0)),
                  _mod_spec(per_token, tm),
                  pl.BlockSpec((1, D_MODEL), lambda b, i: (0, 0))],
        out_specs=pl.BlockSpec((1, tm, D_MODEL), lambda b, i: (b, i, 0)),
        out_shape=jax.ShapeDtypeStruct((bsz, t, D_MODEL), F32),
        scratch_shapes=[pltpu.VMEM((2, tm * TOP_K * ROW_TILES, LANES), F32), pltpu.SemaphoreType.DMA((2,))],
        compiler_params=_cparams(("arbitrary", "arbitrary")),
        name="final",
    )(dest_flat, dest_flat, ys, gates, x1, gt, g_post_ffn.reshape(1, D_MODEL))


def _moe(ht_a, ht_b, gsel, rank, counts, w_gu, b_gu, w_down, b_down, tm, bm):
    n = gsel.shape[0]
    cnt = counts[0].astype(jnp.int32)
    nblk = (cnt + bm - 1) // bm
    blk_end = jnp.cumsum(nblk)
    blk_start = blk_end - nblk
    n_blocks = -(-(n * TOP_K) // bm) + N_EXPERTS
    blocks = jnp.arange(n_blocks, dtype=jnp.int32)
    block_expert = jnp.minimum(jnp.sum((blk_end[None, :] <= blocks[:, None]).astype(jnp.int32), axis=1),
                               N_EXPERTS - 1).astype(jnp.int32)
    ids = jnp.arange(N_EXPERTS, dtype=jnp.int32)
    later = jnp.where((nblk[None, :] > 0) & (ids[None, :] > ids[:, None]), ids[None, :], N_EXPERTS)
    nearest = jnp.min(later, axis=1)
    next_used = jnp.where(nearest < N_EXPERTS, nearest, ids).astype(jnp.int32)
    n_used = blk_end[-1:].astype(jnp.int32)
    row_start = (blk_start * bm).astype(jnp.int32)
    row_end = (blk_end * bm).astype(jnp.int32)
    dest, gates = _dest(gsel, rank, row_start.astype(F32).reshape(1, N_EXPERTS), DEST_TILE)
    dest_flat = dest.reshape(n * TOP_K)
    xs = _scatter_rows(ht_a, ht_b, dest_flat, cnt, row_start, row_end, n_used, n_blocks, SCATTER_TILE, bm)
    owns = (blocks[:, None] >= blk_start[None, :]) & (blocks[:, None] < blk_end[None, :])

    def per_block(table):
        return jnp.sum(jnp.where(owns, table[None, :], 0), axis=1).astype(jnp.int32)

    block_valid = jnp.clip(per_block(cnt) - (blocks - per_block(blk_start)) * bm, 0, bm).astype(jnp.int32)
    ys = _experts(xs, block_expert, per_block(next_used), block_valid, n_used, w_gu, b_gu, w_down, b_down, bm)
    return ys, dest_flat, gates


def _mix_path(x, mod, pool_buf, conv_buf, s0, start_pos, counts0, lp, tm, nb, per_token):
    (g_pre_mix, g_post_mix, g_pre_ffn, w_in_p, w_bd, pool_scale, conv_w, alog_row, dtb_row, norm_w,
     w_out_b, router_w, router_b) = lp
    bsz, t, _ = x.shape
    sh_mix, sc_mix, gt_mix, sh_ffn, sc_ffn, gt_ffn = (mod[:, i] for i in range(6))

    def shape_mod(m):
        if per_token:
            return jnp.repeat(m, t, axis=0).reshape(1, bsz * t, D_MODEL)
        return m.reshape(bsz, 1, D_MODEL)

    sh_mix, sc_mix, gt_mix, sh_ffn, sc_ffn, gt_ffn = map(shape_mod, (sh_mix, sc_mix, gt_mix, sh_ffn, sc_ffn, gt_ffn))
    xf = x.reshape(1, bsz * t, D_MODEL) if per_token else x
    proj = _in_proj(xf, sc_mix, sh_mix, g_pre_mix, w_in_p, PROJ_TILE, per_token).reshape(bsz, t, PROJ_WIDTH)
    tseq = min(tm, t)
    o_gdn, s_new = _gdn(proj, conv_buf, s0, conv_w, alog_row, dtb_row, norm_w, nb, tseq)
    pool_out = _pool(proj, pool_buf, w_bd, pool_scale, nb, tseq, start_pos)
    fl = (lambda a: a.reshape(1, bsz * t, a.shape[-1])) if per_token else (lambda a: a)
    x1, ht, gsel, rank, counts = _out_proj(xf, fl(pool_out), fl(o_gdn), gt_mix, sc_ffn, sh_ffn, g_post_mix,
                                           g_pre_ffn, w_out_b, router_w, router_b, counts0, tm, per_token)

    def last_rows(buf, off, width):
        keep = buf.shape[1]
        if t >= keep:
            return proj[:, t - keep:, off:off + width]
        return jnp.concatenate([buf[:, t:], proj[:, :, off:off + width]], axis=1)

    pool_new = last_rows(pool_buf, OFF_U, POOL_WIDTH)
    conv_new = last_rows(conv_buf, OFF_QKV, CONV_CH)
    return x1, ht, gsel, rank, counts, gt_ffn, pool_new, conv_new, s_new


def kernel(x_prompt, x_sample, c_prompt, c_sample, state_pool, state_conv, state_ssm, w_ada, b_ada, g_pre_mix, g_post_mix, g_pre_ffn, g_post_ffn, w_in, w_pool, pool_scale, conv_w, a_log, dt_bias, gdn_norm_w, w_out, router_w, router_b, w_gu, b_gu, w_down, b_down):
    depth = w_ada.shape[0]
    bp, tp, _ = x_prompt.shape
    bs, ts, _ = x_sample.shape
    tm = TOKEN_TILE
    y_p, y_s = x_prompt, x_sample
    outs = [[] for _ in range(6)]
    for l in range(depth):
        mod = _ada(jnp.concatenate([c_prompt, c_sample], axis=0), w_ada[l], b_ada[l]).reshape(bp + bs, 6, D_MODEL)
        w_in_p = jnp.swapaxes(w_in[l], 0, 1)
        w_bd = jax.scipy.linalg.block_diag(*[w_pool[l, g] for g in range(len(POOL_WINDOWS))])
        pad_row = lambda v: jnp.pad(v, (0, LANES - v.shape[0])).reshape(1, LANES)
        lp = (g_pre_mix[l], g_post_mix[l], g_pre_ffn[l], w_in_p, w_bd, pool_scale[l], conv_w[l],
              pad_row(a_log[l]), pad_row(dt_bias[l]), gdn_norm_w[l].reshape(1, HEAD_DIM),
              w_out[l].astype(BF16), router_w[l], router_b[l])
        zero_pool = jnp.zeros((bp, POOL_BUF, POOL_WIDTH), F32)
        zero_conv = jnp.zeros((bp, CONV_WIDTH - 1, CONV_CH), F32)
        zero_ssm = (jnp.zeros((1, bp, GDN_HEADS, HEAD_DIM, HEAD_DIM), F32), 0)
        no_counts = jnp.zeros((1, N_EXPERTS), F32)
        x1p, htp, gp, rp, cnt_p, gtp, pp, cp, sp = _mix_path(
            y_p, mod[:bp], zero_pool, zero_conv, zero_ssm, 0, no_counts, lp, tm, 1, False)
        x1s, hts, gs, rs, counts, gts, ps, cs, ss = _mix_path(
            y_s, mod[bp:], state_pool[l], state_conv[l], (state_ssm, l), PAST_LEN, cnt_p, lp, tm, 8, True)
        gsel = jnp.concatenate([gp, gs], axis=0)
        rank = jnp.concatenate([rp, rs], axis=0)
        ys, dest_flat, gates = _moe(htp, hts, gsel, rank, counts, w_gu[l], b_gu[l], w_down[l], b_down[l],
                                    tm, MOE_BLOCK)
        y_p = _final(ys, dest_flat, gates, x1p, gtp, g_post_ffn[l], 0, tm, False)
        y_s = _final(ys, dest_flat, gates, x1s, gts, g_post_ffn[l], bp * tp, tm, True).reshape(bs, ts, D_MODEL)
        for lst, val in zip(outs, (pp, cp, sp, ps, cs, ss)):
            lst.append(val)
    return (y_p, y_s) + tuple(o[0][None] if depth == 1 else jnp.stack(o) for o in outs)
```

```python
import functools
import math

import jax
import jax.numpy as jnp
from jax import lax
from jax.experimental import pallas as pl
from jax.experimental.pallas import tpu as pltpu

F32 = jnp.float32
BF16 = jnp.bfloat16

D_MODEL = 1024
PAST_LEN = 16384
POOL_WINDOWS = (2, 4, 8, 16)
POOL_WIDTH = D_MODEL // 4
POOL_GROUP = POOL_WIDTH // len(POOL_WINDOWS)
POOL_BUF = max(POOL_WINDOWS) - 1
GDN_WIDTH = D_MODEL - POOL_WIDTH
HEAD_DIM = 128
GDN_HEADS = GDN_WIDTH // HEAD_DIM
CONV_WIDTH = 4
CONV_CH = 3 * GDN_WIDTH
GDN_CHUNK = 64
N_EXPERTS = 32
TOP_K = 4
D_FF = D_MODEL
SWIGLU_LIMIT = 7.0
SWIGLU_ALPHA = 1.702
EPS = 1e-6

SUBLANES = 8
LANES = 128
ROW_TILES = D_MODEL // LANES
OFF_QKV = 0
OFF_Z = OFF_QKV + CONV_CH
OFF_U = OFF_Z + GDN_WIDTH
OFF_AB = OFF_U + POOL_WIDTH
PROJ_WIDTH = OFF_AB + LANES
assert OFF_Z % GDN_WIDTH == 0 and OFF_U % POOL_WIDTH == 0 and OFF_AB % LANES == 0
TOKEN_TILE = 256
PROJ_TILE = 512
SCATTER_TILE = 512
DEST_TILE = 512
POOL_TILE = 1024
GDN_ITEM_GROUPS = 1
MOE_BLOCK = 512
DMA_QUEUES = 2
VMEM_LIMIT = 56 * 1024 * 1024


def _cparams(sem):
    return pltpu.CompilerParams(dimension_semantics=sem, vmem_limit_bytes=VMEM_LIMIT)


def _dot(a, b):
    return jnp.dot(a.astype(BF16), b.astype(BF16), preferred_element_type=F32)


def _dot_nt(a, b):
    return lax.dot_general(a.astype(BF16), b.astype(BF16), (((1,), (1,)), ((), ())),
                           preferred_element_type=F32)


def _silu(x):
    return x * jax.nn.sigmoid(x)


def _rms(x, g):
    return x * lax.rsqrt(jnp.mean(x * x, axis=-1, keepdims=True) + EPS) * g


def _ada_kernel(c_ref, w_ref, b_ref, o_ref):
    o_ref[...] = _dot(_silu(c_ref[...]), w_ref[...]) + b_ref[...]


def _ada(c, w_ada, b_ada):
    n = c.shape[0]
    cols = w_ada.shape[1]
    blk = D_MODEL
    return pl.pallas_call(
        _ada_kernel,
        grid=(cols // blk,),
        in_specs=[pl.BlockSpec((n, D_MODEL), lambda j: (0, 0)),
                  pl.BlockSpec((D_MODEL, blk), lambda j: (0, j)),
                  pl.BlockSpec((1, blk), lambda j: (0, j))],
        out_specs=pl.BlockSpec((n, blk), lambda j: (0, j)),
        out_shape=jax.ShapeDtypeStruct((n, cols), F32),
        compiler_params=_cparams(("arbitrary",)),
        name="ada",
    )(c, w_ada, b_ada.reshape(1, cols))


def _in_kernel(x_ref, sc_ref, sh_ref, g_ref, wt_ref, o_ref, wb):
    @pl.when((pl.program_id(0) == 0) & (pl.program_id(1) == 0))
    def _():
        o1, o2, o3 = POOL_WIDTH, POOL_WIDTH + CONV_CH, POOL_WIDTH + CONV_CH + GDN_WIDTH
        n_in = wt_ref.shape[0] // ROW_TILES

        def channels(lo, hi):
            return jnp.concatenate(
                [wt_ref[pl.ds(lo * ROW_TILES + s, hi - lo, stride=ROW_TILES), :] for s in range(ROW_TILES)], axis=-1)

        wb[OFF_QKV:OFF_QKV + CONV_CH] = channels(o1, o2).astype(BF16)
        wb[OFF_Z:OFF_Z + GDN_WIDTH] = channels(o2, o3).astype(BF16)
        wb[OFF_U:OFF_U + POOL_WIDTH] = channels(0, o1).astype(BF16)
        wb[OFF_AB:] = jnp.concatenate(
            [channels(o3, n_in), jnp.zeros((LANES - (n_in - o3), D_MODEL), F32)], axis=0).astype(BF16)

    h = _rms(x_ref[0], g_ref[...]) * (1.0 + sc_ref[0]) + sh_ref[0]
    o_ref[0] = _dot_nt(h, wb[...])


def _mod_spec(per_token, tm):
    if per_token:
        return pl.BlockSpec((1, tm, D_MODEL), lambda b, t: (0, t, 0))
    return pl.BlockSpec((1, 1, D_MODEL), lambda b, t: (b, 0, 0))


def _in_proj(x, sc, sh, g, w_in, tm, per_token):
    bsz, t, _ = x.shape
    return pl.pallas_call(
        _in_kernel,
        grid=(bsz, t // tm),
        in_specs=[pl.BlockSpec((1, tm, D_MODEL), lambda b, i: (b, i, 0)),
                  _mod_spec(per_token, tm), _mod_spec(per_token, tm),
                  pl.BlockSpec((1, D_MODEL), lambda b, i: (0, 0)),
                  pl.BlockSpec(w_in.shape, lambda b, i: (0, 0), pipeline_mode=pl.Buffered(1))],
        out_specs=pl.BlockSpec((1, tm, PROJ_WIDTH), lambda b, i: (b, i, 0)),
        out_shape=jax.ShapeDtypeStruct((bsz, t, PROJ_WIDTH), F32),
        scratch_shapes=[pltpu.VMEM((PROJ_WIDTH, D_MODEL), BF16)],
        compiler_params=_cparams(("arbitrary", "arbitrary")),
        name="in_proj",
    )(x, sc, sh, g.reshape(1, D_MODEL), w_in)


def _cumsum_rows(x):
    n = x.shape[0]
    row = lax.broadcasted_iota(jnp.int32, x.shape, 0)
    s = 1
    while s < n:
        x = x + jnp.where(row >= s, pltpu.roll(x, s, axis=0), 0.0)
        s *= 2
    return x


def _gdn_kernel(qkv_ref, z_ref, ab_ref, cs_ref, cw_ref, s0_ref, alog_ref, dtb_ref, nw_ref, o_ref, sn_ref,
                s_scr, prev_scr, *, nb, tm, tmp, chunk, group_items):
    ti = pl.program_id(1)
    nt = pl.num_programs(1)
    n_chunks = tmp // chunk
    n_sq = int(math.log2(chunk)) - 1

    @pl.when(ti == 0)
    def _():
        s_scr[...] = s0_ref[...]
        for b in range(nb):
            prev_scr[b] = jnp.concatenate(
                [jnp.zeros((SUBLANES - (CONV_WIDTH - 1), CONV_CH), F32), cs_ref[b]], axis=0)

    rowi = lax.broadcasted_iota(jnp.int32, (tmp, LANES), 0)
    ri = lax.broadcasted_iota(jnp.int32, (chunk, chunk), 0)
    ci = lax.broadcasted_iota(jnp.int32, (chunk, chunk), 1)
    tril = ri >= ci
    strict = ri > ci
    eye = (ri == ci).astype(F32)
    neg_a = -jnp.exp(alog_ref[...])
    dtb = dtb_ref[...]
    nw = nw_ref[...]

    def padrows(x):
        if tm == tmp:
            return x
        return jnp.concatenate([x, jnp.zeros((tmp - tm, x.shape[1]), F32)], axis=0)

    cw = cw_ref[...]
    valid = rowi < tm
    items = []
    for b in range(nb):
        x = padrows(qkv_ref[b])
        e = jnp.concatenate([prev_scr[b], x], axis=0)
        acc = x * cw[CONV_WIDTH - 1:CONV_WIDTH]
        for d in range(1, CONV_WIDTH):
            acc = acc + pltpu.roll(e, d, axis=0)[SUBLANES:] * cw[CONV_WIDTH - 1 - d:CONV_WIDTH - d]
        if tm >= SUBLANES:
            prev_scr[b] = x[tm - SUBLANES:tm]
        y = _silu(acc)
        ab = padrows(ab_ref[b])
        g_all = jnp.where(valid, neg_a * jax.nn.softplus(ab + dtb), 0.0)
        beta_all = jnp.where(valid, jax.nn.sigmoid(ab), 0.0)
        for h in range(GDN_HEADS):
            qc = y[:, h * HEAD_DIM:(h + 1) * HEAD_DIM]
            kc = y[:, GDN_WIDTH + h * HEAD_DIM:GDN_WIDTH + (h + 1) * HEAD_DIM]
            v = y[:, 2 * GDN_WIDTH + h * HEAD_DIM:2 * GDN_WIDTH + (h + 1) * HEAD_DIM]
            q = qc * lax.rsqrt(jnp.sum(qc * qc, axis=-1, keepdims=True) + EPS) * (HEAD_DIM ** -0.5)
            k = kc * lax.rsqrt(jnp.sum(kc * kc, axis=-1, keepdims=True) + EPS)
            gb = jnp.broadcast_to(g_all[:, h:h + 1], (tmp, LANES))
            bb = jnp.broadcast_to(beta_all[:, GDN_HEADS + h:GDN_HEADS + h + 1], (tmp, LANES))
            for c in range(n_chunks):
                sl = slice(c * chunk, (c + 1) * chunk)
                items.append(dict(b=b, h=h, c=c, q=q[sl], k=k[sl], v=v[sl], bb=bb[sl], gb=gb[sl]))

    def within_chunk(group):
        for it in group:
            gcb = _cumsum_rows(it["gb"])
            gct = jnp.transpose(gcb)[:chunk]
            diff = gcb[:, :chunk] - gct
            it["decay"] = jnp.where(tril, jnp.exp(jnp.where(tril, diff, 0.0)), 0.0)
            it["kb"] = it["k"] * it["bb"]
            it["eg"] = jnp.exp(gcb)
            gl = gcb[chunk - 1:chunk]
            it["egl"] = jnp.exp(gl)
            it["kdt"] = jnp.transpose(it["k"] * jnp.exp(gl - gcb))
        for it in group:
            kq = _dot_nt(jnp.concatenate([it["kb"], it["q"]], axis=0), it["k"])
            it["a"] = jnp.where(strict, kq[:chunk] * it["decay"], 0.0)
            it["qk"] = jnp.where(tril, kq[chunk:] * it["decay"], 0.0)
        for it in group:
            it["inv"] = eye - it["a"]
            it["p"] = _dot(it["a"], it["a"])
        for lvl in range(n_sq):
            for it in group:
                if lvl < n_sq - 1:
                    r = _dot(jnp.concatenate([it["inv"], it["p"]], axis=0), it["p"])
                    it["inv"] = it["inv"] + r[:chunk]
                    it["p"] = r[chunk:]
                else:
                    it["inv"] = it["inv"] + _dot(it["inv"], it["p"])
        for it in group:
            it["wu"] = _dot(it["inv"], jnp.concatenate([it["kb"] * it["eg"], it["v"] * it["bb"]], axis=1))
        for it in group:
            mn = _dot(it["kdt"], it["wu"])
            qo = _dot(it["qk"], it["wu"])
            it["mp"], it["nc"] = mn[:, :HEAD_DIM], mn[:, HEAD_DIM:]
            it["qp"] = it["q"] * it["eg"] - qo[:, :HEAD_DIM]
            it["oc"] = qo[:, HEAD_DIM:]

    for g0 in range(0, len(items), group_items):
        within_chunk(items[g0:g0 + group_items])

    state = {(b, h): s_scr[b, h] for b in range(nb) for h in range(GDN_HEADS)}
    for c in range(n_chunks):
        for it in items:
            if it["c"] != c:
                continue
            s = state[(it["b"], it["h"])]
            r = _dot(jnp.concatenate([it["qp"], it["mp"]], axis=0), s)
            it["o"] = r[:chunk] + it["oc"]
            state[(it["b"], it["h"])] = s * it["egl"] - r[chunk:] + it["nc"]
    for (b, h), s in state.items():
        s_scr[b, h] = s

    for b in range(nb):
        zf = padrows(z_ref[b])
        for h in range(GDN_HEADS):
            outs = [it["o"] for it in items if it["b"] == b and it["h"] == h]
            o = outs[0] if n_chunks == 1 else jnp.concatenate(outs, axis=0)
            o = (o * lax.rsqrt(jnp.mean(o * o, axis=-1, keepdims=True) + EPS) * nw
                 * _silu(zf[:, h * HEAD_DIM:(h + 1) * HEAD_DIM]))
            o_ref[b, :, h * HEAD_DIM:(h + 1) * HEAD_DIM] = o[:tm]

    @pl.when(ti == nt - 1)
    def _():
        sn_ref[...] = s_scr[...]


def _gdn(proj, state_conv, s0, conv_w, alog_row, dtb_row, norm_w, nb, tm):
    bsz, t, _ = proj.shape
    tmp = max(tm, SUBLANES)
    chunk = min(GDN_CHUNK, tmp)
    row = pl.BlockSpec((1, LANES), lambda b, i: (0, 0))
    s_all, layer = s0
    sspec = pl.BlockSpec((nb, GDN_HEADS, HEAD_DIM, HEAD_DIM), lambda b, i: (b, 0, 0, 0))
    s_in = pl.BlockSpec((None, nb, GDN_HEADS, HEAD_DIM, HEAD_DIM), lambda b, i: (layer, b, 0, 0, 0))
    n_items = nb * GDN_HEADS * (tmp // chunk)
    kern = functools.partial(_gdn_kernel, nb=nb, tm=tm, tmp=tmp, chunk=chunk,
                             group_items=n_items // GDN_ITEM_GROUPS)
    return pl.pallas_call(
        kern,
        grid=(bsz // nb, t // tm),
        in_specs=[pl.BlockSpec((nb, tm, CONV_CH), lambda b, i: (b, i, OFF_QKV // CONV_CH)),
                  pl.BlockSpec((nb, tm, GDN_WIDTH), lambda b, i: (b, i, OFF_Z // GDN_WIDTH)),
                  pl.BlockSpec((nb, tm, LANES), lambda b, i: (b, i, OFF_AB // LANES)),
                  pl.BlockSpec((nb, CONV_WIDTH - 1, CONV_CH), lambda b, i: (b, 0, 0)),
                  pl.BlockSpec((CONV_WIDTH, CONV_CH), lambda b, i: (0, 0)),
                  s_in, row, row, row],
        out_specs=[pl.BlockSpec((nb, tm, GDN_WIDTH), lambda b, i: (b, i, 0)), sspec],
        out_shape=[jax.ShapeDtypeStruct((bsz, t, GDN_WIDTH), F32),
                   jax.ShapeDtypeStruct(s_all.shape[1:], F32)],
        scratch_shapes=[pltpu.VMEM((nb, GDN_HEADS, HEAD_DIM, HEAD_DIM), F32),
                        pltpu.VMEM((nb, SUBLANES, CONV_CH), F32)],
        compiler_params=_cparams(("arbitrary", "arbitrary")),
        name="gdn",
    )(proj, proj, proj, state_conv, conv_w, s_all, alog_row, dtb_row, norm_w)


def _pool_kernel(u_ref, st_ref, w_ref, sc_ref, o_ref, prev_scr, *, nb, tm, tmp, start_pos):
    ti = pl.program_id(1)
    halo = 2 * SUBLANES

    @pl.when(ti == 0)
    def _():
        for b in range(nb):
            prev_scr[b] = jnp.concatenate([jnp.zeros((halo - POOL_BUF, POOL_WIDTH), F32), st_ref[b]], axis=0)

    lane = lax.broadcasted_iota(jnp.int32, (tmp, POOL_WIDTH), 1)
    rowi = lax.broadcasted_iota(jnp.int32, (tmp, POOL_WIDTH), 0)
    pos1 = start_pos + ti * tm + rowi + 1
    grp = lane // POOL_GROUP
    win = jnp.where(grp == 0, POOL_WINDOWS[0],
                    jnp.where(grp == 1, POOL_WINDOWS[1], jnp.where(grp == 2, POOL_WINDOWS[2], POOL_WINDOWS[3])))
    cnt = jnp.minimum(pos1, win).astype(F32)
    for b in range(nb):
        x = u_ref[b]
        if tm != tmp:
            x = jnp.concatenate([x, jnp.zeros((tmp - tm, POOL_WIDTH), F32)], axis=0)
        e = jnp.concatenate([prev_scr[b], x], axis=0)
        sums = []
        s = e
        for w in POOL_WINDOWS:
            s = s + pltpu.roll(s, w // 2, axis=0)
            sums.append(s[halo:])
        sel = jnp.where(grp == 0, sums[0], jnp.where(grp == 1, sums[1], jnp.where(grp == 2, sums[2], sums[3])))
        pooled = sel / cnt - x
        out = _dot(pooled, w_ref[...]) * sc_ref[...]
        o_ref[b] = out[:tm]
        if tm >= halo:
            prev_scr[b] = x[tm - halo:tm]


def _pool(proj, state_pool, w_bd, pool_scale, nb, tm, start_pos):
    bsz, t, _ = proj.shape
    tmp = max(tm, SUBLANES)
    kern = functools.partial(_pool_kernel, nb=nb, tm=tm, tmp=tmp, start_pos=start_pos)
    return pl.pallas_call(
        kern,
        grid=(bsz // nb, t // tm),
        in_specs=[pl.BlockSpec((nb, tm, POOL_WIDTH), lambda b, i: (b, i, OFF_U // POOL_WIDTH)),
                  pl.BlockSpec((nb, POOL_BUF, POOL_WIDTH), lambda b, i: (b, 0, 0)),
                  pl.BlockSpec((POOL_WIDTH, POOL_WIDTH), lambda b, i: (0, 0)),
                  pl.BlockSpec((1, POOL_WIDTH), lambda b, i: (0, 0))],
        out_specs=pl.BlockSpec((nb, tm, POOL_WIDTH), lambda b, i: (b, i, 0)),
        out_shape=jax.ShapeDtypeStruct((bsz, t, POOL_WIDTH), F32),
        scratch_shapes=[pltpu.VMEM((nb, 2 * SUBLANES, POOL_WIDTH), F32)],
        compiler_params=_cparams(("arbitrary", "arbitrary")),
        name="pool",
    )(proj, state_pool, w_bd, pool_scale.reshape(1, POOL_WIDTH))


def _out_kernel(x_ref, po_ref, og_ref, gt_ref, sc_ref, sh_ref, gpm_ref, gpf_ref, wo_ref, rw_ref, rb_ref, c0_ref,
                x1_ref, ht_ref, g_ref, r_ref, cnt_ref, carry, *, tm):
    @pl.when((pl.program_id(0) == 0) & (pl.program_id(1) == 0))
    def _():
        carry[...] = c0_ref[...]

    mix = _dot(po_ref[0], wo_ref[:POOL_WIDTH]) + _dot(og_ref[0], wo_ref[POOL_WIDTH:])
    x1 = x_ref[0] + gt_ref[0] * _rms(mix, gpm_ref[...])
    x1_ref[0] = x1
    h = _rms(x1, gpf_ref[...]) * (1.0 + sc_ref[0]) + sh_ref[0]
    for s in range(ROW_TILES):
        ht_ref[pl.ds(s, tm, stride=ROW_TILES), :] = h[:, s * LANES:(s + 1) * LANES]
    logits = _dot_nt(rw_ref[...], h) + rb_ref[...]
    eid = lax.broadcasted_iota(jnp.int32, logits.shape, 0).astype(F32)
    work = logits
    sel = jnp.zeros(logits.shape, jnp.bool_)
    top = None
    den = jnp.zeros((1, tm), F32)
    for j in range(TOP_K):
        m = jnp.max(work, axis=0, keepdims=True)
        idx = jnp.min(jnp.where(work == m, eid, float(N_EXPERTS)), axis=0, keepdims=True)
        hit = eid == idx
        sel = sel | hit
        work = jnp.where(hit, -jnp.inf, work)
        if j == 0:
            top = m
        den = den + jnp.exp(m - top)
    gates = jnp.exp(logits - top) / den
    g_ref[...] = jnp.transpose(jnp.where(sel, gates, -1.0))
    mask = sel.astype(F32)
    ri = lax.broadcasted_iota(jnp.int32, (tm, tm), 0)
    ci = lax.broadcasted_iota(jnp.int32, (tm, tm), 1)
    c = carry[...]
    r_ref[...] = jnp.transpose(_dot(mask, (ri < ci).astype(F32)) + c)
    c = c + jnp.sum(mask, axis=1, keepdims=True)
    carry[...] = c
    cnt_ref[...] = c


def _out_proj(x, pool_out, o_gdn, gt, sc, sh, g_post_mix, g_pre_ffn, w_out_b, router_w, router_b, counts0,
              tm, per_token):
    bsz, t, _ = x.shape
    n = bsz * t
    tiles = t // tm

    def tok(width):
        return pl.BlockSpec((1, tm, width), lambda b, i: (b, i, 0))

    def const(shape):
        return pl.BlockSpec(shape, lambda b, i: tuple(0 for _ in shape))

    kern = functools.partial(_out_kernel, tm=tm)
    return pl.pallas_call(
        kern,
        grid=(bsz, tiles),
        in_specs=[tok(D_MODEL), tok(POOL_WIDTH), tok(GDN_WIDTH),
                  _mod_spec(per_token, tm), _mod_spec(per_token, tm), _mod_spec(per_token, tm),
                  const((1, D_MODEL)), const((1, D_MODEL)), const((D_MODEL, D_MODEL)),
                  const((N_EXPERTS, D_MODEL)), const((N_EXPERTS, 1)), const((N_EXPERTS, 1))],
        out_specs=[tok(D_MODEL),
                   pl.BlockSpec((tm * ROW_TILES, LANES), lambda b, i: (b * tiles + i, 0)),
                   pl.BlockSpec((tm, N_EXPERTS), lambda b, i: (b * tiles + i, 0)),
                   pl.BlockSpec((tm, N_EXPERTS), lambda b, i: (b * tiles + i, 0)),
                   const((N_EXPERTS, 1))],
        out_shape=[jax.ShapeDtypeStruct((bsz, t, D_MODEL), F32),
                   jax.ShapeDtypeStruct((n * ROW_TILES, LANES), F32),
                   jax.ShapeDtypeStruct((n, N_EXPERTS), F32),
                   jax.ShapeDtypeStruct((n, N_EXPERTS), F32),
                   jax.ShapeDtypeStruct((N_EXPERTS, 1), F32)],
        scratch_shapes=[pltpu.VMEM((N_EXPERTS, 1), F32)],
        compiler_params=_cparams(("arbitrary", "arbitrary")),
        name="out_proj",
    )(x, pool_out, o_gdn, gt, sc, sh, g_post_mix.reshape(1, D_MODEL), g_pre_ffn.reshape(1, D_MODEL),
      w_out_b, jnp.swapaxes(router_w, 0, 1), router_b.reshape(N_EXPERTS, 1), counts0)


def _dest_kernel(g_ref, r_ref, start_ref, d_ref, gt_ref, *, tm):
    g = g_ref[...]
    mask = g >= 0.0
    maskf = mask.astype(F32)
    dest = start_ref[...] + r_ref[...]
    ri = lax.broadcasted_iota(jnp.int32, (N_EXPERTS, N_EXPERTS), 0)
    ci = lax.broadcasted_iota(jnp.int32, (N_EXPERTS, N_EXPERTS), 1)
    slot = _dot(maskf, (ri < ci).astype(F32))
    lane4 = lax.broadcasted_iota(jnp.int32, (tm, TOP_K), 1)
    d_out = jnp.zeros((tm, TOP_K), F32)
    g_out = jnp.zeros((tm, TOP_K), F32)
    for j in range(TOP_K):
        pick = mask & (slot == float(j))
        dj = jnp.sum(jnp.where(pick, dest, 0.0), axis=-1, keepdims=True)
        gj = jnp.sum(jnp.where(pick, g, 0.0), axis=-1, keepdims=True)
        d_out = jnp.where(lane4 == j, dj, d_out)
        g_out = jnp.where(lane4 == j, gj, g_out)
    d_ref[...] = d_out.astype(jnp.int32)
    gt_ref[...] = g_out


def _dest(gsel, rank, start_row, tm):
    n = gsel.shape[0]
    kern = functools.partial(_dest_kernel, tm=tm)
    return pl.pallas_call(
        kern,
        grid=(n // tm,),
        in_specs=[pl.BlockSpec((tm, N_EXPERTS), lambda i: (i, 0)),
                  pl.BlockSpec((tm, N_EXPERTS), lambda i: (i, 0)),
                  pl.BlockSpec((1, N_EXPERTS), lambda i: (0, 0))],
        out_specs=[pl.BlockSpec((tm, TOP_K), lambda i: (i, 0)),
                   pl.BlockSpec((tm, TOP_K), lambda i: (i, 0))],
        out_shape=[jax.ShapeDtypeStruct((n, TOP_K), jnp.int32),
                   jax.ShapeDtypeStruct((n, TOP_K), F32)],
        compiler_params=_cparams(("arbitrary",)),
        name="dest",
    )(gsel, rank, start_row)


def _row_copy(src, dst, s_row, d_row, sem):
    return pltpu.make_async_copy(src.at[pl.ds(pl.multiple_of(s_row * ROW_TILES, ROW_TILES), ROW_TILES), :],
                                 dst.at[pl.ds(pl.multiple_of(d_row * ROW_TILES, ROW_TILES), ROW_TILES), :], sem)


def _scatter_kernel(cnt_ref, start_ref, end_ref, nu_ref, dest_ref, ha_ref, hb_ref, xs_ref, sem, *,
                    tm, bm, n_blocks, tiles_a):
    i = pl.program_id(0)

    def scatter_from(ht_ref):
        def body(t, carry):
            src = ht_ref.at[pl.ds(pl.multiple_of(t * ROW_TILES, ROW_TILES), ROW_TILES), :]
            for k in range(TOP_K):
                d = dest_ref[t * TOP_K + k]
                pltpu.make_async_copy(
                    src, xs_ref.at[pl.ds(pl.multiple_of(d * ROW_TILES, ROW_TILES), ROW_TILES), :],
                    sem).start(priority=k % DMA_QUEUES)
            return carry

        lax.fori_loop(0, tm, body, 0)

    @pl.when(i < tiles_a)
    def _():
        scatter_from(ha_ref)

    @pl.when(i >= tiles_a)
    def _():
        scatter_from(hb_ref)

    ht_ref = hb_ref

    def tile_copy(u):
        return pltpu.make_async_copy(
            ht_ref, xs_ref.at[pl.ds(pl.multiple_of(u * (tm * ROW_TILES), tm * ROW_TILES), tm * ROW_TILES), :], sem)

    for k in range(TOP_K):
        tile_copy(0).wait()

    @pl.when(i == pl.num_programs(0) - 1)
    def _():
        per = bm // tm

        def tail_start(u, carry):
            tile_copy(u).start()
            return carry

        def tail_wait(u, carry):
            tile_copy(u).wait()
            return carry

        lax.fori_loop(nu_ref[0] * per, n_blocks * per, tail_start, 0)
        lax.fori_loop(nu_ref[0] * per, n_blocks * per, tail_wait, 0)

        def per_expert(e, carry):
            lo = start_ref[e] + cnt_ref[e]
            hi = end_ref[e]

            def fill(r, c2):
                _row_copy(ht_ref, xs_ref, 0, r, sem).start()
                return c2

            lax.fori_loop(lo, hi, fill, 0)

            def drain(r, c2):
                _row_copy(ht_ref, xs_ref, 0, r, sem).wait()
                return c2

            lax.fori_loop(lo, hi, drain, 0)
            return carry

        lax.fori_loop(0, N_EXPERTS, per_expert, 0)


def _scatter_rows(ht_a, ht_b, dest_flat, cnt, start, end, n_used, n_blocks, tm, bm):
    rows = n_blocks * bm
    tiles_a, tiles_b = ht_a.shape[0] // (tm * ROW_TILES), ht_b.shape[0] // (tm * ROW_TILES)
    assert bm % tm == 0 and ht_a.shape[0] % (tm * ROW_TILES) == 0 and ht_b.shape[0] % (tm * ROW_TILES) == 0
    assert tiles_b >= 1
    kern = functools.partial(_scatter_kernel, tm=tm, bm=bm, n_blocks=n_blocks, tiles_a=tiles_a)
    return pl.pallas_call(
        kern,
        grid_spec=pltpu.PrefetchScalarGridSpec(
            num_scalar_prefetch=4,
            grid=(tiles_a + tiles_b,),
            in_specs=[pl.BlockSpec((tm * TOP_K,), lambda i, *_: (i,), memory_space=pltpu.SMEM),
                      pl.BlockSpec((tm * ROW_TILES, LANES), lambda i, *_: (jnp.minimum(i, tiles_a - 1), 0)),
                      pl.BlockSpec((tm * ROW_TILES, LANES), lambda i, *_: (jnp.maximum(i - tiles_a, 0), 0))],
            out_specs=pl.BlockSpec(memory_space=pl.ANY),
            scratch_shapes=[pltpu.SemaphoreType.DMA(())]),
        out_shape=jax.ShapeDtypeStruct((rows * ROW_TILES, LANES), F32),
        compiler_params=_cparams(("arbitrary",)),
        name="scatter_rows",
    )(cnt, start, end, n_used, dest_flat, ht_a, ht_b)


def _expert_kernel(be_ref, nx_ref, nv_ref, nu_ref, xs_ref, wg_ref, bg_ref, wd_ref, bd_ref, ys_ref,
                   wg_f, wd_f, wg_s, wd_s, wsem, *, bm):
    j = pl.program_id(0)
    prev = be_ref[jnp.maximum(j - 1, 0)]
    fresh = (j == 0) | (be_ref[j] != prev)
    active = j < nu_ref[0]

    def weight_copies(e):
        return (pltpu.make_async_copy(wg_ref.at[e], wg_f, wsem.at[0]),
                pltpu.make_async_copy(wd_ref.at[e], wd_f, wsem.at[1]))

    @pl.when(j == 0)
    def _():
        for c in weight_copies(be_ref[0]):
            c.start()

    @pl.when(active & fresh)
    def _():
        for c in weight_copies(be_ref[j]):
            c.wait()
        wg_s[...] = wg_f[...].astype(BF16)
        wd_s[...] = wd_f[...].astype(BF16)

        @pl.when(nx_ref[j] != be_ref[j])
        def _():
            for c in weight_copies(nx_ref[j]):
                c.start()

    def run_rows(m):
        x = jnp.concatenate([xs_ref[pl.ds(s, m, stride=ROW_TILES), :] for s in range(ROW_TILES)], axis=-1)
        xb = x.astype(BF16)
        gate = jnp.dot(xb, wg_s[:, :D_FF], preferred_element_type=F32) + bg_ref[0, :, :D_FF]
        up = jnp.dot(xb, wg_s[:, D_FF:], preferred_element_type=F32) + bg_ref[0, :, D_FF:]
        gate = jnp.minimum(gate, SWIGLU_LIMIT)
        up = jnp.clip(up, -SWIGLU_LIMIT, SWIGLU_LIMIT)
        act = (up + 1.0) * (gate * jax.nn.sigmoid(SWIGLU_ALPHA * gate))
        y = jnp.dot(act.astype(BF16), wd_s[...], preferred_element_type=F32) + bd_ref[0]
        for s in range(ROW_TILES):
            ys_ref[pl.ds(s, m, stride=ROW_TILES), :] = y[:, s * LANES:(s + 1) * LANES]

    half = bm // 2
    few = nv_ref[j] <= half

    @pl.when(active & jnp.logical_not(few))
    def _():
        run_rows(bm)

    @pl.when(active & few)
    def _():
        run_rows(half)
        ys_ref[pl.ds(half * ROW_TILES, half * ROW_TILES), :] = jnp.zeros((half * ROW_TILES, LANES), F32)

    @pl.when(jnp.logical_not(active))
    def _():
        ys_ref[...] = jnp.zeros_like(ys_ref)


def _experts(xs, block_expert, next_expert, block_valid, n_used, w_gu, b_gu, w_down, b_down, bm):
    rows = xs.shape[0] // ROW_TILES
    n_blocks = rows // bm

    def blk(j, be, nx, nv, nu):
        return (jnp.minimum(j, nu[0] - 1), 0)

    def bsel(j, be, nx, nv, nu):
        return (be[j], 0, 0)

    hbm = pl.BlockSpec(memory_space=pl.ANY)
    kern = functools.partial(_expert_kernel, bm=bm)
    return pl.pallas_call(
        kern,
        grid_spec=pltpu.PrefetchScalarGridSpec(
            num_scalar_prefetch=4,
            grid=(n_blocks,),
            in_specs=[pl.BlockSpec((bm * ROW_TILES, LANES), blk), hbm,
                      pl.BlockSpec((1, 1, 2 * D_FF), bsel), hbm,
                      pl.BlockSpec((1, 1, D_MODEL), bsel)],
            out_specs=pl.BlockSpec((bm * ROW_TILES, LANES), lambda j, be, nx, nv, nu: (j, 0)),
            scratch_shapes=[pltpu.VMEM((D_MODEL, 2 * D_FF), F32),
                            pltpu.VMEM((D_FF, D_MODEL), F32),
                            pltpu.VMEM((D_MODEL, 2 * D_FF), BF16),
                            pltpu.VMEM((D_FF, D_MODEL), BF16),
                            pltpu.SemaphoreType.DMA((2,))]),
        out_shape=jax.ShapeDtypeStruct((rows * ROW_TILES, LANES), F32),
        compiler_params=_cparams(("arbitrary",)),
        name="experts",
    )(block_expert, next_expert, block_valid, n_used, xs, w_gu, b_gu.reshape(N_EXPERTS, 1, 2 * D_FF), w_down,
      b_down.reshape(N_EXPERTS, 1, D_MODEL))


def _final_kernel(dcur_ref, dnext_ref, ys_ref, gate_ref, x1_ref, gt_ref, g_ref, o_ref, sl_ref, sem, *, tm, steps):
    g = pl.program_id(0) * pl.num_programs(1) + pl.program_id(1)
    cur = g % 2

    def request(d_ref, slot):
        def body(t, carry):
            for k in range(TOP_K):
                j = t * TOP_K + k
                pltpu.make_async_copy(
                    ys_ref.at[pl.ds(pl.multiple_of(d_ref[j] * ROW_TILES, ROW_TILES), ROW_TILES), :],
                    sl_ref.at[slot, pl.ds(pl.multiple_of(j * ROW_TILES, ROW_TILES), ROW_TILES), :],
                    sem.at[slot]).start(priority=k % DMA_QUEUES)
            return carry

        lax.fori_loop(0, tm, body, 0)

    @pl.when(g == 0)
    def _():
        request(dcur_ref, 0)

    @pl.when(g + 1 < steps)
    def _():
        request(dnext_ref, 1 - cur)

    pltpu.make_async_copy(ys_ref.at[pl.ds(0, tm * TOP_K * ROW_TILES), :], sl_ref.at[cur], sem.at[cur]).wait()

    gates = gate_ref[...]
    per_tok = TOP_K * ROW_TILES
    cols = []
    for s in range(ROW_TILES):
        acc = None
        for j in range(TOP_K):
            part = sl_ref[cur, pl.ds(j * ROW_TILES + s, tm, stride=per_tok), :] * gates[:, j:j + 1]
            acc = part if acc is None else acc + part
        cols.append(acc)
    f = jnp.concatenate(cols, axis=-1)
    o_ref[0] = x1_ref[0] + gt_ref[0] * _rms(f, g_ref[...])


def _final(ys, dest_flat, gates, x1, gt, g_post_ffn, tok_off, tm, per_token):
    bsz, t, _ = x1.shape
    tiles = t // tm
    off = tok_off // tm
    steps = bsz * tiles
    kern = functools.partial(_final_kernel, tm=tm, steps=steps)

    def dspec(ahead):
        return pl.BlockSpec((tm * TOP_K,), lambda b, i: (off + jnp.minimum(b * tiles + i + ahead, steps - 1),),
                            memory_space=pltpu.SMEM)

    return pl.pallas_call(
        kern,
        grid=(bsz, tiles),
        in_specs=[dspec(0), dspec(1),
                  pl.BlockSpec(memory_space=pl.ANY),
                  pl.BlockSpec((tm, TOP_K), lambda b, i: (off + b * tiles + i, 0)),
                  pl.BlockSpec((1, tm, D_MODEL), lambda b, i: (b, i, 0)),
                  _mod_spec(per_token, tm),
                  pl.BlockSpec((1, D_MODEL), lambda b, i: (0, 0))],
        out_specs=pl.BlockSpec((1, tm, D_MODEL), lambda b, i: (b, i, 0)),
        out_shape=jax.ShapeDtypeStruct((bsz, t, D_MODEL), F32),
        scratch_shapes=[pltpu.VMEM((2, tm * TOP_K * ROW_TILES, LANES), F32), pltpu.SemaphoreType.DMA((2,))],
        compiler_params=_cparams(("arbitrary", "arbitrary")),
        name="final",
    )(dest_flat, dest_flat, ys, gates, x1, gt, g_post_ffn.reshape(1, D_MODEL))


def _moe(ht_a, ht_b, gsel, rank, counts, w_gu, b_gu, w_down, b_down, tm, bm):
    n = gsel.shape[0]
    cnt = counts[:, 0].astype(jnp.int32)
    nblk = (cnt + bm - 1) // bm
    blk_end = jnp.cumsum(nblk)
    blk_start = blk_end - nblk
    n_blocks = -(-(n * TOP_K) // bm) + N_EXPERTS
    blocks = jnp.arange(n_blocks, dtype=jnp.int32)
    block_expert = jnp.minimum(jnp.sum((blk_end[None, :] <= blocks[:, None]).astype(jnp.int32), axis=1),
                               N_EXPERTS - 1).astype(jnp.int32)
    ids = jnp.arange(N_EXPERTS, dtype=jnp.int32)
    later = jnp.where((nblk[None, :] > 0) & (ids[None, :] > ids[:, None]), ids[None, :], N_EXPERTS)
    nearest = jnp.min(later, axis=1)
    next_used = jnp.where(nearest < N_EXPERTS, nearest, ids).astype(jnp.int32)
    n_used = blk_end[-1:].astype(jnp.int32)
    row_start = (blk_start * bm).astype(jnp.int32)
    row_end = (blk_end * bm).astype(jnp.int32)
    dest, gates = _dest(gsel, rank, row_start.astype(F32).reshape(1, N_EXPERTS), DEST_TILE)
    dest_flat = dest.reshape(n * TOP_K)
    xs = _scatter_rows(ht_a, ht_b, dest_flat, cnt, row_start, row_end, n_used, n_blocks, SCATTER_TILE, bm)
    owns = (blocks[:, None] >= blk_start[None, :]) & (blocks[:, None] < blk_end[None, :])

    def per_block(table):
        return jnp.sum(jnp.where(owns, table[None, :], 0), axis=1).astype(jnp.int32)

    block_valid = jnp.clip(per_block(cnt) - (blocks - per_block(blk_start)) * bm, 0, bm).astype(jnp.int32)
    ys = _experts(xs, block_expert, per_block(next_used), block_valid, n_used, w_gu, b_gu, w_down, b_down, bm)
    return ys, dest_flat, gates


def _mix_path(x, mod, pool_buf, conv_buf, s0, start_pos, counts0, lp, tm, nb, per_token):
    (g_pre_mix, g_post_mix, g_pre_ffn, w_in_p, w_bd, pool_scale, conv_w, alog_row, dtb_row, norm_w,
     w_out_b, router_w, router_b) = lp
    bsz, t, _ = x.shape
    sh_mix, sc_mix, gt_mix, sh_ffn, sc_ffn, gt_ffn = (mod[:, i] for i in range(6))

    def shape_mod(m):
        if per_token:
            return jnp.repeat(m, t, axis=0).reshape(1, bsz * t, D_MODEL)
        return m.reshape(bsz, 1, D_MODEL)

    sh_mix, sc_mix, gt_mix, sh_ffn, sc_ffn, gt_ffn = map(shape_mod, (sh_mix, sc_mix, gt_mix, sh_ffn, sc_ffn, gt_ffn))
    xf = x.reshape(1, bsz * t, D_MODEL) if per_token else x
    proj = _in_proj(xf, sc_mix, sh_mix, g_pre_mix, w_in_p, PROJ_TILE, per_token).reshape(bsz, t, PROJ_WIDTH)
    tseq = min(tm, t)
    o_gdn, s_new = _gdn(proj, conv_buf, s0, conv_w, alog_row, dtb_row, norm_w, nb, tseq)
    pool_out = _pool(proj, pool_buf, w_bd, pool_scale, nb, min(POOL_TILE, t), start_pos)
    fl = (lambda a: a.reshape(1, bsz * t, a.shape[-1])) if per_token else (lambda a: a)
    x1, ht, gsel, rank, counts = _out_proj(xf, fl(pool_out), fl(o_gdn), gt_mix, sc_ffn, sh_ffn, g_post_mix,
                                           g_pre_ffn, w_out_b, router_w, router_b, counts0, tm, per_token)

    def last_rows(buf, off, width):
        keep = buf.shape[1]
        if t >= keep:
            return proj[:, t - keep:, off:off + width]
        return jnp.concatenate([buf[:, t:], proj[:, :, off:off + width]], axis=1)

    pool_new = last_rows(pool_buf, OFF_U, POOL_WIDTH)
    conv_new = last_rows(conv_buf, OFF_QKV, CONV_CH)
    return x1, ht, gsel, rank, counts, gt_ffn, pool_new, conv_new, s_new


def kernel(x_prompt, x_sample, c_prompt, c_sample, state_pool, state_conv, state_ssm, w_ada, b_ada, g_pre_mix, g_post_mix, g_pre_ffn, g_post_ffn, w_in, w_pool, pool_scale, conv_w, a_log, dt_bias, gdn_norm_w, w_out, router_w, router_b, w_gu, b_gu, w_down, b_down):
    depth = w_ada.shape[0]
    bp, tp, _ = x_prompt.shape
    bs, ts, _ = x_sample.shape
    tm = TOKEN_TILE
    y_p, y_s = x_prompt, x_sample
    outs = [[] for _ in range(6)]
    for l in range(depth):
        mod = _ada(jnp.concatenate([c_prompt, c_sample], axis=0), w_ada[l], b_ada[l]).reshape(bp + bs, 6, D_MODEL)
        w_in_p = jnp.swapaxes(w_in[l], 0, 1).reshape(-1, LANES)
        w_bd = jax.scipy.linalg.block_diag(*[w_pool[l, g] for g in range(len(POOL_WINDOWS))])
        pad_row = lambda v: jnp.pad(v, (0, LANES - v.shape[0])).reshape(1, LANES)
        lp = (g_pre_mix[l], g_post_mix[l], g_pre_ffn[l], w_in_p, w_bd, pool_scale[l], conv_w[l],
              pad_row(a_log[l]), pad_row(dt_bias[l]), gdn_norm_w[l].reshape(1, HEAD_DIM),
              w_out[l].astype(BF16), router_w[l], router_b[l])
        zero_pool = jnp.zeros((bp, POOL_BUF, POOL_WIDTH), F32)
        zero_conv = jnp.zeros((bp, CONV_WIDTH - 1, CONV_CH), F32)
        zero_ssm = (jnp.zeros((1, bp, GDN_HEADS, HEAD_DIM, HEAD_DIM), F32), 0)
        no_counts = jnp.zeros((N_EXPERTS, 1), F32)
        x1p, htp, gp, rp, cnt_p, gtp, pp, cp, sp = _mix_path(
            y_p, mod[:bp], zero_pool, zero_conv, zero_ssm, 0, no_counts, lp, tm, 1, False)
        x1s, hts, gs, rs, counts, gts, ps, cs, ss = _mix_path(
            y_s, mod[bp:], state_pool[l], state_conv[l], (state_ssm, l), PAST_LEN, cnt_p, lp, tm, 8, True)
        gsel = jnp.concatenate([gp, gs], axis=0)
        rank = jnp.concatenate([rp, rs], axis=0)
        ys, dest_flat, gates = _moe(htp, hts, gsel, rank, counts, w_gu[l], b_gu[l], w_down[l], b_down[l],
                                    tm, MOE_BLOCK)
        y_p = _final(ys, dest_flat, gates, x1p, gtp, g_post_ffn[l], 0, tm, False)
        y_s = _final(ys, dest_flat, gates, x1s, gts, g_post_ffn[l], bp * tp, tm, True).reshape(bs, ts, D_MODEL)
        for lst, val in zip(outs, (pp, cp, sp, ps, cs, ss)):
            lst.append(val)
    return (y_p, y_s) + tuple(o[0][None] if depth == 1 else jnp.stack(o) for o in outs)
```

```python
import functools
import math

import jax
import jax.numpy as jnp
from jax import lax
from jax.experimental import pallas as pl
from jax.experimental.pallas import tpu as pltpu

F32 = jnp.float32
BF16 = jnp.bfloat16

D_MODEL = 1024
PAST_LEN = 16384
POOL_WINDOWS = (2, 4, 8, 16)
POOL_WIDTH = D_MODEL // 4
POOL_GROUP = POOL_WIDTH // len(POOL_WINDOWS)
POOL_BUF = max(POOL_WINDOWS) - 1
GDN_WIDTH = D_MODEL - POOL_WIDTH
HEAD_DIM = 128
GDN_HEADS = GDN_WIDTH // HEAD_DIM
CONV_WIDTH = 4
CONV_CH = 3 * GDN_WIDTH
GDN_CHUNK = 64
N_EXPERTS = 32
TOP_K = 4
D_FF = D_MODEL
SWIGLU_LIMIT = 7.0
SWIGLU_ALPHA = 1.702
EPS = 1e-6

SUBLANES = 8
LANES = 128
ROW_TILES = D_MODEL // LANES
OFF_QKV = 0
OFF_Z = OFF_QKV + CONV_CH
OFF_U = OFF_Z + GDN_WIDTH
OFF_AB = OFF_U + POOL_WIDTH
PROJ_WIDTH = OFF_AB + LANES
assert OFF_Z % GDN_WIDTH == 0 and OFF_U % POOL_WIDTH == 0 and OFF_AB % LANES == 0
TOKEN_TILE = 256
PROJ_TILE = 512
SCATTER_TILE = 512
OUT_TILE = 512
DEST_TILE = 1536
POOL_TILE = 1024
POOL_SEQS = 4
GDN_ITEM_GROUPS = 1
MOE_BLOCK = 512
DMA_QUEUES = 2
VMEM_LIMIT = 56 * 1024 * 1024


def _cparams(sem):
    return pltpu.CompilerParams(dimension_semantics=sem, vmem_limit_bytes=VMEM_LIMIT)


def _dot(a, b):
    return jnp.dot(a.astype(BF16), b.astype(BF16), preferred_element_type=F32)


def _dot_nt(a, b):
    return lax.dot_general(a.astype(BF16), b.astype(BF16), (((1,), (1,)), ((), ())),
                           preferred_element_type=F32)


def _silu(x):
    return x * jax.nn.sigmoid(x)


def _rms(x, g):
    return x * lax.rsqrt(jnp.mean(x * x, axis=-1, keepdims=True) + EPS) * g


def _ada_kernel(c_ref, w_ref, b_ref, o_ref):
    o_ref[...] = _dot(_silu(c_ref[...]), w_ref[...]) + b_ref[...]


def _ada(c, w_ada, b_ada):
    n = c.shape[0]
    cols = w_ada.shape[1]
    blk = D_MODEL
    return pl.pallas_call(
        _ada_kernel,
        grid=(cols // blk,),
        in_specs=[pl.BlockSpec((n, D_MODEL), lambda j: (0, 0)),
                  pl.BlockSpec((D_MODEL, blk), lambda j: (0, j)),
                  pl.BlockSpec((1, blk), lambda j: (0, j))],
        out_specs=pl.BlockSpec((n, blk), lambda j: (0, j)),
        out_shape=jax.ShapeDtypeStruct((n, cols), F32),
        compiler_params=_cparams(("arbitrary",)),
        name="ada",
    )(c, w_ada, b_ada.reshape(1, cols))


def _in_kernel(x_ref, sc_ref, sh_ref, g_ref, wt_ref, o_ref, wb):
    @pl.when((pl.program_id(0) == 0) & (pl.program_id(1) == 0))
    def _():
        o1, o2, o3 = POOL_WIDTH, POOL_WIDTH + CONV_CH, POOL_WIDTH + CONV_CH + GDN_WIDTH
        wb[OFF_QKV:OFF_QKV + CONV_CH] = wt_ref[o1:o2].astype(BF16)
        wb[OFF_Z:OFF_Z + GDN_WIDTH] = wt_ref[o2:o3].astype(BF16)
        wb[OFF_U:OFF_U + POOL_WIDTH] = wt_ref[:o1].astype(BF16)
        tail = wt_ref.shape[0] - o3
        wb[OFF_AB:] = jnp.concatenate(
            [wt_ref[o3:], jnp.zeros((LANES - tail, D_MODEL), F32)], axis=0).astype(BF16)

    h = _rms(x_ref[0], g_ref[...]) * (1.0 + sc_ref[0]) + sh_ref[0]
    o_ref[0] = _dot_nt(h, wb[...])


def _mod_spec(per_token, tm):
    if per_token:
        return pl.BlockSpec((1, tm, D_MODEL), lambda b, t: (0, t, 0))
    return pl.BlockSpec((1, 1, D_MODEL), lambda b, t: (b, 0, 0))


def _in_proj(x, sc, sh, g, w_in, tm, per_token):
    bsz, t, _ = x.shape
    return pl.pallas_call(
        _in_kernel,
        grid=(bsz, t // tm),
        in_specs=[pl.BlockSpec((1, tm, D_MODEL), lambda b, i: (b, i, 0)),
                  _mod_spec(per_token, tm), _mod_spec(per_token, tm),
                  pl.BlockSpec((1, D_MODEL), lambda b, i: (0, 0)),
                  pl.BlockSpec(w_in.shape, lambda b, i: (0, 0), pipeline_mode=pl.Buffered(1))],
        out_specs=pl.BlockSpec((1, tm, PROJ_WIDTH), lambda b, i: (b, i, 0)),
        out_shape=jax.ShapeDtypeStruct((bsz, t, PROJ_WIDTH), F32),
        scratch_shapes=[pltpu.VMEM((PROJ_WIDTH, D_MODEL), BF16)],
        compiler_params=_cparams(("arbitrary", "arbitrary")),
        name="in_proj",
    )(x, sc, sh, g.reshape(1, D_MODEL), w_in)


def _cumsum_rows(x):
    n = x.shape[0]
    row = lax.broadcasted_iota(jnp.int32, x.shape, 0)
    s = 1
    while s < n:
        x = x + jnp.where(row >= s, pltpu.roll(x, s, axis=0), 0.0)
        s *= 2
    return x


def _gdn_kernel(qkv_ref, z_ref, ab_ref, cs_ref, cw_ref, s0_ref, alog_ref, dtb_ref, nw_ref, o_ref, sn_ref,
                s_scr, prev_scr, *, nb, tm, tmp, chunk, group_items):
    ti = pl.program_id(1)
    nt = pl.num_programs(1)
    n_chunks = tmp // chunk
    n_sq = int(math.log2(chunk)) - 1

    @pl.when(ti == 0)
    def _():
        s_scr[...] = s0_ref[...]
        for b in range(nb):
            prev_scr[b] = jnp.concatenate(
                [jnp.zeros((SUBLANES - (CONV_WIDTH - 1), CONV_CH), F32), cs_ref[b]], axis=0)

    rowi = lax.broadcasted_iota(jnp.int32, (tmp, LANES), 0)
    ri = lax.broadcasted_iota(jnp.int32, (chunk, chunk), 0)
    ci = lax.broadcasted_iota(jnp.int32, (chunk, chunk), 1)
    tril = ri >= ci
    strict = ri > ci
    eye = (ri == ci).astype(F32)
    neg_a = -jnp.exp(alog_ref[...])
    dtb = dtb_ref[...]
    nw = nw_ref[...]

    def padrows(x):
        if tm == tmp:
            return x
        return jnp.concatenate([x, jnp.zeros((tmp - tm, x.shape[1]), F32)], axis=0)

    cw = cw_ref[...]
    valid = rowi < tm
    items = []
    for b in range(nb):
        x = padrows(qkv_ref[b])
        e = jnp.concatenate([prev_scr[b], x], axis=0)
        acc = x * cw[CONV_WIDTH - 1:CONV_WIDTH]
        for d in range(1, CONV_WIDTH):
            acc = acc + pltpu.roll(e, d, axis=0)[SUBLANES:] * cw[CONV_WIDTH - 1 - d:CONV_WIDTH - d]
        if tm >= SUBLANES:
            prev_scr[b] = x[tm - SUBLANES:tm]
        y = _silu(acc)
        ab = padrows(ab_ref[b])
        g_all = jnp.where(valid, neg_a * jax.nn.softplus(ab + dtb), 0.0)
        beta_all = jnp.where(valid, jax.nn.sigmoid(ab), 0.0)
        for h in range(GDN_HEADS):
            qc = y[:, h * HEAD_DIM:(h + 1) * HEAD_DIM]
            kc = y[:, GDN_WIDTH + h * HEAD_DIM:GDN_WIDTH + (h + 1) * HEAD_DIM]
            v = y[:, 2 * GDN_WIDTH + h * HEAD_DIM:2 * GDN_WIDTH + (h + 1) * HEAD_DIM]
            q = qc * lax.rsqrt(jnp.sum(qc * qc, axis=-1, keepdims=True) + EPS) * (HEAD_DIM ** -0.5)
            k = kc * lax.rsqrt(jnp.sum(kc * kc, axis=-1, keepdims=True) + EPS)
            gb = jnp.broadcast_to(g_all[:, h:h + 1], (tmp, LANES))
            bb = jnp.broadcast_to(beta_all[:, GDN_HEADS + h:GDN_HEADS + h + 1], (tmp, LANES))
            for c in range(n_chunks):
                sl = slice(c * chunk, (c + 1) * chunk)
                items.append(dict(b=b, h=h, c=c, q=q[sl], k=k[sl], v=v[sl], bb=bb[sl], gb=gb[sl]))

    def within_chunk(group):
        for it in group:
            gcb = _cumsum_rows(it["gb"])
            gct = jnp.transpose(gcb)[:chunk]
            diff = gcb[:, :chunk] - gct
            it["decay"] = jnp.where(tril, jnp.exp(jnp.where(tril, diff, 0.0)), 0.0)
            it["kb"] = it["k"] * it["bb"]
            it["eg"] = jnp.exp(gcb)
            gl = gcb[chunk - 1:chunk]
            it["egl"] = jnp.exp(gl)
            it["kdt"] = jnp.transpose(it["k"] * jnp.exp(gl - gcb))
        for it in group:
            kq = _dot_nt(jnp.concatenate([it["kb"], it["q"]], axis=0), it["k"])
            it["a"] = jnp.where(strict, kq[:chunk] * it["decay"], 0.0)
            it["qk"] = jnp.where(tril, kq[chunk:] * it["decay"], 0.0)
        for it in group:
            it["inv"] = eye - it["a"]
            it["p"] = _dot(it["a"], it["a"])
        for lvl in range(n_sq):
            for it in group:
                if lvl < n_sq - 1:
                    r = _dot(jnp.concatenate([it["inv"], it["p"]], axis=0), it["p"])
                    it["inv"] = it["inv"] + r[:chunk]
                    it["p"] = r[chunk:]
                else:
                    it["inv"] = it["inv"] + _dot(it["inv"], it["p"])
        for it in group:
            it["wu"] = _dot(it["inv"], jnp.concatenate([it["kb"] * it["eg"], it["v"] * it["bb"]], axis=1))
        for it in group:
            mn = _dot(it["kdt"], it["wu"])
            qo = _dot(it["qk"], it["wu"])
            it["mp"], it["nc"] = mn[:, :HEAD_DIM], mn[:, HEAD_DIM:]
            it["qp"] = it["q"] * it["eg"] - qo[:, :HEAD_DIM]
            it["oc"] = qo[:, HEAD_DIM:]

    for g0 in range(0, len(items), group_items):
        within_chunk(items[g0:g0 + group_items])

    state = {(b, h): s_scr[b, h] for b in range(nb) for h in range(GDN_HEADS)}
    for c in range(n_chunks):
        for it in items:
            if it["c"] != c:
                continue
            s = state[(it["b"], it["h"])]
            r = _dot(jnp.concatenate([it["qp"], it["mp"]], axis=0), s)
            it["o"] = r[:chunk] + it["oc"]
            state[(it["b"], it["h"])] = s * it["egl"] - r[chunk:] + it["nc"]
    for (b, h), s in state.items():
        s_scr[b, h] = s

    for b in range(nb):
        zf = padrows(z_ref[b])
        for h in range(GDN_HEADS):
            outs = [it["o"] for it in items if it["b"] == b and it["h"] == h]
            o = outs[0] if n_chunks == 1 else jnp.concatenate(outs, axis=0)
            o = (o * lax.rsqrt(jnp.mean(o * o, axis=-1, keepdims=True) + EPS) * nw
                 * _silu(zf[:, h * HEAD_DIM:(h + 1) * HEAD_DIM]))
            o_ref[b, :, h * HEAD_DIM:(h + 1) * HEAD_DIM] = o[:tm]

    @pl.when(ti == nt - 1)
    def _():
        sn_ref[...] = s_scr[...]


def _gdn(proj, state_conv, s0, conv_w, alog_row, dtb_row, norm_w, nb, tm):
    bsz, t, _ = proj.shape
    tmp = max(tm, SUBLANES)
    chunk = min(GDN_CHUNK, tmp)
    row = pl.BlockSpec((1, LANES), lambda b, i: (0, 0))
    s_all, layer = s0
    sspec = pl.BlockSpec((nb, GDN_HEADS, HEAD_DIM, HEAD_DIM), lambda b, i: (b, 0, 0, 0))
    s_in = pl.BlockSpec((None, nb, GDN_HEADS, HEAD_DIM, HEAD_DIM), lambda b, i: (layer, b, 0, 0, 0))
    n_items = nb * GDN_HEADS * (tmp // chunk)
    kern = functools.partial(_gdn_kernel, nb=nb, tm=tm, tmp=tmp, chunk=chunk,
                             group_items=n_items // GDN_ITEM_GROUPS)
    return pl.pallas_call(
        kern,
        grid=(bsz // nb, t // tm),
        in_specs=[pl.BlockSpec((nb, tm, CONV_CH), lambda b, i: (b, i, OFF_QKV // CONV_CH)),
                  pl.BlockSpec((nb, tm, GDN_WIDTH), lambda b, i: (b, i, OFF_Z // GDN_WIDTH)),
                  pl.BlockSpec((nb, tm, LANES), lambda b, i: (b, i, OFF_AB // LANES)),
                  pl.BlockSpec((nb, CONV_WIDTH - 1, CONV_CH), lambda b, i: (b, 0, 0)),
                  pl.BlockSpec((CONV_WIDTH, CONV_CH), lambda b, i: (0, 0)),
                  s_in, row, row, row],
        out_specs=[pl.BlockSpec((nb, tm, GDN_WIDTH), lambda b, i: (b, i, 0)), sspec],
        out_shape=[jax.ShapeDtypeStruct((bsz, t, GDN_WIDTH), F32),
                   jax.ShapeDtypeStruct(s_all.shape[1:], F32)],
        scratch_shapes=[pltpu.VMEM((nb, GDN_HEADS, HEAD_DIM, HEAD_DIM), F32),
                        pltpu.VMEM((nb, SUBLANES, CONV_CH), F32)],
        compiler_params=_cparams(("arbitrary", "arbitrary")),
        name="gdn",
    )(proj, proj, proj, state_conv, conv_w, s_all, alog_row, dtb_row, norm_w)


def _pool_kernel(u_ref, st_ref, w_ref, sc_ref, o_ref, prev_scr, *, nb, tm, tmp, start_pos):
    ti = pl.program_id(1)
    halo = 2 * SUBLANES

    @pl.when(ti == 0)
    def _():
        for b in range(nb):
            prev_scr[b] = jnp.concatenate([jnp.zeros((halo - POOL_BUF, POOL_WIDTH), F32), st_ref[b]], axis=0)

    lane = lax.broadcasted_iota(jnp.int32, (tmp, POOL_WIDTH), 1)
    rowi = lax.broadcasted_iota(jnp.int32, (tmp, POOL_WIDTH), 0)
    pos1 = start_pos + ti * tm + rowi + 1
    grp = lane // POOL_GROUP
    win = jnp.where(grp == 0, POOL_WINDOWS[0],
                    jnp.where(grp == 1, POOL_WINDOWS[1], jnp.where(grp == 2, POOL_WINDOWS[2], POOL_WINDOWS[3])))
    cnt = jnp.minimum(pos1, win).astype(F32)
    for b in range(nb):
        x = u_ref[b]
        if tm != tmp:
            x = jnp.concatenate([x, jnp.zeros((tmp - tm, POOL_WIDTH), F32)], axis=0)
        e = jnp.concatenate([prev_scr[b], x], axis=0)
        sums = []
        s = e
        for w in POOL_WINDOWS:
            s = s + pltpu.roll(s, w // 2, axis=0)
            sums.append(s[halo:])
        sel = jnp.where(grp == 0, sums[0], jnp.where(grp == 1, sums[1], jnp.where(grp == 2, sums[2], sums[3])))
        pooled = sel / cnt - x
        out = _dot(pooled, w_ref[...]) * sc_ref[...]
        o_ref[b] = out[:tm]
        if tm >= halo:
            prev_scr[b] = x[tm - halo:tm]


def _pool(proj, state_pool, w_bd, pool_scale, nb, tm, start_pos):
    bsz, t, _ = proj.shape
    tmp = max(tm, SUBLANES)
    kern = functools.partial(_pool_kernel, nb=nb, tm=tm, tmp=tmp, start_pos=start_pos)
    return pl.pallas_call(
        kern,
        grid=(bsz // nb, t // tm),
        in_specs=[pl.BlockSpec((nb, tm, POOL_WIDTH), lambda b, i: (b, i, OFF_U // POOL_WIDTH)),
                  pl.BlockSpec((nb, POOL_BUF, POOL_WIDTH), lambda b, i: (b, 0, 0)),
                  pl.BlockSpec((POOL_WIDTH, POOL_WIDTH), lambda b, i: (0, 0)),
                  pl.BlockSpec((1, POOL_WIDTH), lambda b, i: (0, 0))],
        out_specs=pl.BlockSpec((nb, tm, POOL_WIDTH), lambda b, i: (b, i, 0)),
        out_shape=jax.ShapeDtypeStruct((bsz, t, POOL_WIDTH), F32),
        scratch_shapes=[pltpu.VMEM((nb, 2 * SUBLANES, POOL_WIDTH), F32)],
        compiler_params=_cparams(("arbitrary", "arbitrary")),
        name="pool",
    )(proj, state_pool, w_bd, pool_scale.reshape(1, POOL_WIDTH))


def _out_kernel(x_ref, po_ref, og_ref, gt_ref, sc_ref, sh_ref, gpm_ref, gpf_ref, wo_ref, rw_ref, rb_ref, c0_ref,
                x1_ref, ht_ref, g_ref, r_ref, cnt_ref, carry, *, tm):
    @pl.when((pl.program_id(0) == 0) & (pl.program_id(1) == 0))
    def _():
        carry[...] = c0_ref[...]

    mix = _dot(po_ref[0], wo_ref[:POOL_WIDTH]) + _dot(og_ref[0], wo_ref[POOL_WIDTH:])
    x1 = x_ref[0] + gt_ref[0] * _rms(mix, gpm_ref[...])
    x1_ref[0] = x1
    h = _rms(x1, gpf_ref[...]) * (1.0 + sc_ref[0]) + sh_ref[0]
    for s in range(ROW_TILES):
        ht_ref[pl.ds(s, tm, stride=ROW_TILES), :] = h[:, s * LANES:(s + 1) * LANES]
    logits = _dot_nt(rw_ref[...], h) + rb_ref[...]
    eid = lax.broadcasted_iota(jnp.int32, logits.shape, 0).astype(F32)
    work = logits
    sel = jnp.zeros(logits.shape, jnp.bool_)
    top = None
    den = jnp.zeros((1, tm), F32)
    for j in range(TOP_K):
        m = jnp.max(work, axis=0, keepdims=True)
        idx = jnp.min(jnp.where(work == m, eid, float(N_EXPERTS)), axis=0, keepdims=True)
        hit = eid == idx
        sel = sel | hit
        work = jnp.where(hit, -jnp.inf, work)
        if j == 0:
            top = m
        den = den + jnp.exp(m - top)
    gates = jnp.exp(logits - top) / den
    g_ref[...] = jnp.transpose(jnp.where(sel, gates, -1.0))
    mask = sel.astype(F32)
    ri = lax.broadcasted_iota(jnp.int32, (tm, tm), 0)
    ci = lax.broadcasted_iota(jnp.int32, (tm, tm), 1)
    c = carry[...]
    r_ref[...] = jnp.transpose(_dot(mask, (ri < ci).astype(F32)) + c)
    c = c + jnp.sum(mask, axis=1, keepdims=True)
    carry[...] = c
    cnt_ref[...] = c


def _out_proj(x, pool_out, o_gdn, gt, sc, sh, g_post_mix, g_pre_ffn, w_out_b, router_w, router_b, counts0,
              tm, per_token):
    bsz, t, _ = x.shape
    n = bsz * t
    tiles = t // tm

    def tok(width):
        return pl.BlockSpec((1, tm, width), lambda b, i: (b, i, 0))

    def const(shape):
        return pl.BlockSpec(shape, lambda b, i: tuple(0 for _ in shape))

    kern = functools.partial(_out_kernel, tm=tm)
    return pl.pallas_call(
        kern,
        grid=(bsz, tiles),
        in_specs=[tok(D_MODEL), tok(POOL_WIDTH), tok(GDN_WIDTH),
                  _mod_spec(per_token, tm), _mod_spec(per_token, tm), _mod_spec(per_token, tm),
                  const((1, D_MODEL)), const((1, D_MODEL)), const((D_MODEL, D_MODEL)),
                  const((N_EXPERTS, D_MODEL)), const((N_EXPERTS, 1)), const((N_EXPERTS, 1))],
        out_specs=[tok(D_MODEL),
                   pl.BlockSpec((tm * ROW_TILES, LANES), lambda b, i: (b * tiles + i, 0)),
                   pl.BlockSpec((tm, N_EXPERTS), lambda b, i: (b * tiles + i, 0)),
                   pl.BlockSpec((tm, N_EXPERTS), lambda b, i: (b * tiles + i, 0)),
                   const((N_EXPERTS, 1))],
        out_shape=[jax.ShapeDtypeStruct((bsz, t, D_MODEL), F32),
                   jax.ShapeDtypeStruct((n * ROW_TILES, LANES), F32),
                   jax.ShapeDtypeStruct((n, N_EXPERTS), F32),
                   jax.ShapeDtypeStruct((n, N_EXPERTS), F32),
                   jax.ShapeDtypeStruct((N_EXPERTS, 1), F32)],
        scratch_shapes=[pltpu.VMEM((N_EXPERTS, 1), F32)],
        compiler_params=_cparams(("arbitrary", "arbitrary")),
        name="out_proj",
    )(x, pool_out, o_gdn, gt, sc, sh, g_post_mix.reshape(1, D_MODEL), g_pre_ffn.reshape(1, D_MODEL),
      w_out_b, jnp.swapaxes(router_w, 0, 1), router_b.reshape(N_EXPERTS, 1), counts0)


def _dest_kernel(g_ref, r_ref, start_ref, d_ref, gt_ref, *, tm):
    g = g_ref[...]
    mask = g >= 0.0
    maskf = mask.astype(F32)
    dest = start_ref[...] + r_ref[...]
    ri = lax.broadcasted_iota(jnp.int32, (N_EXPERTS, N_EXPERTS), 0)
    ci = lax.broadcasted_iota(jnp.int32, (N_EXPERTS, N_EXPERTS), 1)
    slot = _dot(maskf, (ri < ci).astype(F32))
    lane4 = lax.broadcasted_iota(jnp.int32, (tm, TOP_K), 1)
    d_out = jnp.zeros((tm, TOP_K), F32)
    g_out = jnp.zeros((tm, TOP_K), F32)
    for j in range(TOP_K):
        pick = mask & (slot == float(j))
        dj = jnp.sum(jnp.where(pick, dest, 0.0), axis=-1, keepdims=True)
        gj = jnp.sum(jnp.where(pick, g, 0.0), axis=-1, keepdims=True)
        d_out = jnp.where(lane4 == j, dj, d_out)
        g_out = jnp.where(lane4 == j, gj, g_out)
    d_ref[...] = d_out.astype(jnp.int32)
    gt_ref[...] = g_out


def _dest(gsel, rank, start_row, tm):
    n = gsel.shape[0]
    kern = functools.partial(_dest_kernel, tm=tm)
    return pl.pallas_call(
        kern,
        grid=(n // tm,),
        in_specs=[pl.BlockSpec((tm, N_EXPERTS), lambda i: (i, 0)),
                  pl.BlockSpec((tm, N_EXPERTS), lambda i: (i, 0)),
                  pl.BlockSpec((1, N_EXPERTS), lambda i: (0, 0))],
        out_specs=[pl.BlockSpec((tm, TOP_K), lambda i: (i, 0)),
                   pl.BlockSpec((tm, TOP_K), lambda i: (i, 0))],
        out_shape=[jax.ShapeDtypeStruct((n, TOP_K), jnp.int32),
                   jax.ShapeDtypeStruct((n, TOP_K), F32)],
        compiler_params=_cparams(("arbitrary",)),
        name="dest",
    )(gsel, rank, start_row)


def _row_copy(src, dst, s_row, d_row, sem):
    return pltpu.make_async_copy(src.at[pl.ds(pl.multiple_of(s_row * ROW_TILES, ROW_TILES), ROW_TILES), :],
                                 dst.at[pl.ds(pl.multiple_of(d_row * ROW_TILES, ROW_TILES), ROW_TILES), :], sem)


def _scatter_kernel(cnt_ref, start_ref, end_ref, nu_ref, dest_ref, ha_ref, hb_ref, xs_ref, sem, *,
                    tm, bm, n_blocks, tiles_a):
    i = pl.program_id(0)

    def scatter_from(ht_ref):
        def body(t, carry):
            src = ht_ref.at[pl.ds(pl.multiple_of(t * ROW_TILES, ROW_TILES), ROW_TILES), :]
            for k in range(TOP_K):
                d = dest_ref[t * TOP_K + k]
                pltpu.make_async_copy(
                    src, xs_ref.at[pl.ds(pl.multiple_of(d * ROW_TILES, ROW_TILES), ROW_TILES), :],
                    sem).start(priority=k % DMA_QUEUES)
            return carry

        lax.fori_loop(0, tm, body, 0)

    @pl.when(i < tiles_a)
    def _():
        scatter_from(ha_ref)

    @pl.when(i >= tiles_a)
    def _():
        scatter_from(hb_ref)

    ht_ref = hb_ref

    def tile_copy(u):
        return pltpu.make_async_copy(
            ht_ref, xs_ref.at[pl.ds(pl.multiple_of(u * (tm * ROW_TILES), tm * ROW_TILES), tm * ROW_TILES), :], sem)

    for k in range(TOP_K):
        tile_copy(0).wait()

    @pl.when(i == pl.num_programs(0) - 1)
    def _():
        per = bm // tm

        def tail_start(u, carry):
            tile_copy(u).start()
            return carry

        def tail_wait(u, carry):
            tile_copy(u).wait()
            return carry

        lax.fori_loop(nu_ref[0] * per, n_blocks * per, tail_start, 0)
        lax.fori_loop(nu_ref[0] * per, n_blocks * per, tail_wait, 0)

        def per_expert(e, carry):
            lo = start_ref[e] + cnt_ref[e]
            hi = end_ref[e]

            def fill(r, c2):
                _row_copy(ht_ref, xs_ref, 0, r, sem).start()
                return c2

            lax.fori_loop(lo, hi, fill, 0)

            def drain(r, c2):
                _row_copy(ht_ref, xs_ref, 0, r, sem).wait()
                return c2

            lax.fori_loop(lo, hi, drain, 0)
            return carry

        lax.fori_loop(0, N_EXPERTS, per_expert, 0)


def _scatter_rows(ht_a, ht_b, dest_flat, cnt, start, end, n_used, n_blocks, tm, bm):
    rows = n_blocks * bm
    tiles_a, tiles_b = ht_a.shape[0] // (tm * ROW_TILES), ht_b.shape[0] // (tm * ROW_TILES)
    assert bm % tm == 0 and ht_a.shape[0] % (tm * ROW_TILES) == 0 and ht_b.shape[0] % (tm * ROW_TILES) == 0
    assert tiles_b >= 1
    kern = functools.partial(_scatter_kernel, tm=tm, bm=bm, n_blocks=n_blocks, tiles_a=tiles_a)
    return pl.pallas_call(
        kern,
        grid_spec=pltpu.PrefetchScalarGridSpec(
            num_scalar_prefetch=4,
            grid=(tiles_a + tiles_b,),
            in_specs=[pl.BlockSpec((tm * TOP_K,), lambda i, *_: (i,), memory_space=pltpu.SMEM),
                      pl.BlockSpec((tm * ROW_TILES, LANES), lambda i, *_: (jnp.minimum(i, tiles_a - 1), 0)),
                      pl.BlockSpec((tm * ROW_TILES, LANES), lambda i, *_: (jnp.maximum(i - tiles_a, 0), 0))],
            out_specs=pl.BlockSpec(memory_space=pl.ANY),
            scratch_shapes=[pltpu.SemaphoreType.DMA(())]),
        out_shape=jax.ShapeDtypeStruct((rows * ROW_TILES, LANES), F32),
        compiler_params=_cparams(("arbitrary",)),
        name="scatter_rows",
    )(cnt, start, end, n_used, dest_flat, ht_a, ht_b)


def _expert_kernel(be_ref, nx_ref, nv_ref, nu_ref, xs_ref, wg_ref, bg_ref, wd_ref, bd_ref, ys_ref,
                   wg_f, wd_f, wg_s, wd_s, wsem, *, bm):
    j = pl.program_id(0)
    prev = be_ref[jnp.maximum(j - 1, 0)]
    fresh = (j == 0) | (be_ref[j] != prev)
    active = j < nu_ref[0]

    def weight_copies(e):
        return (pltpu.make_async_copy(wg_ref.at[e], wg_f, wsem.at[0]),
                pltpu.make_async_copy(wd_ref.at[e], wd_f, wsem.at[1]))

    @pl.when(j == 0)
    def _():
        for c in weight_copies(be_ref[0]):
            c.start()

    @pl.when(active & fresh)
    def _():
        for c in weight_copies(be_ref[j]):
            c.wait()
        wg_s[...] = wg_f[...].astype(BF16)
        wd_s[...] = wd_f[...].astype(BF16)

        @pl.when(nx_ref[j] != be_ref[j])
        def _():
            for c in weight_copies(nx_ref[j]):
                c.start()

    def run_rows(m):
        x = jnp.concatenate([xs_ref[pl.ds(s, m, stride=ROW_TILES), :] for s in range(ROW_TILES)], axis=-1)
        xb = x.astype(BF16)
        gate = jnp.dot(xb, wg_s[:, :D_FF], preferred_element_type=F32) + bg_ref[0, :, :D_FF]
        up = jnp.dot(xb, wg_s[:, D_FF:], preferred_element_type=F32) + bg_ref[0, :, D_FF:]
        gate = jnp.minimum(gate, SWIGLU_LIMIT)
        up = jnp.clip(up, -SWIGLU_LIMIT, SWIGLU_LIMIT)
        act = (up + 1.0) * (gate * jax.nn.sigmoid(SWIGLU_ALPHA * gate))
        y = jnp.dot(act.astype(BF16), wd_s[...], preferred_element_type=F32) + bd_ref[0]
        for s in range(ROW_TILES):
            ys_ref[pl.ds(s, m, stride=ROW_TILES), :] = y[:, s * LANES:(s + 1) * LANES]

    half = bm // 2
    few = nv_ref[j] <= half

    @pl.when(active & jnp.logical_not(few))
    def _():
        run_rows(bm)

    @pl.when(active & few)
    def _():
        run_rows(half)
        ys_ref[pl.ds(half * ROW_TILES, half * ROW_TILES), :] = jnp.zeros((half * ROW_TILES, LANES), F32)

    @pl.when(jnp.logical_not(active))
    def _():
        ys_ref[...] = jnp.zeros_like(ys_ref)


def _experts(xs, block_expert, next_expert, block_valid, n_used, w_gu, b_gu, w_down, b_down, bm):
    rows = xs.shape[0] // ROW_TILES
    n_blocks = rows // bm

    def blk(j, be, nx, nv, nu):
        return (jnp.minimum(j, nu[0] - 1), 0)

    def bsel(j, be, nx, nv, nu):
        return (be[j], 0, 0)

    hbm = pl.BlockSpec(memory_space=pl.ANY)
    kern = functools.partial(_expert_kernel, bm=bm)
    return pl.pallas_call(
        kern,
        grid_spec=pltpu.PrefetchScalarGridSpec(
            num_scalar_prefetch=4,
            grid=(n_blocks,),
            in_specs=[pl.BlockSpec((bm * ROW_TILES, LANES), blk), hbm,
                      pl.BlockSpec((1, 1, 2 * D_FF), bsel), hbm,
                      pl.BlockSpec((1, 1, D_MODEL), bsel)],
            out_specs=pl.BlockSpec((bm * ROW_TILES, LANES), lambda j, be, nx, nv, nu: (j, 0)),
            scratch_shapes=[pltpu.VMEM((D_MODEL, 2 * D_FF), F32),
                            pltpu.VMEM((D_FF, D_MODEL), F32),
                            pltpu.VMEM((D_MODEL, 2 * D_FF), BF16),
                            pltpu.VMEM((D_FF, D_MODEL), BF16),
                            pltpu.SemaphoreType.DMA((2,))]),
        out_shape=jax.ShapeDtypeStruct((rows * ROW_TILES, LANES), F32),
        compiler_params=_cparams(("arbitrary",)),
        name="experts",
    )(block_expert, next_expert, block_valid, n_used, xs, w_gu, b_gu.reshape(N_EXPERTS, 1, 2 * D_FF), w_down,
      b_down.reshape(N_EXPERTS, 1, D_MODEL))


def _final_kernel(dcur_ref, dnext_ref, ys_ref, gate_ref, x1_ref, gt_ref, g_ref, o_ref, sl_ref, sem, *, tm, steps):
    g = pl.program_id(0) * pl.num_programs(1) + pl.program_id(1)
    cur = g % 2

    def request(d_ref, slot):
        def body(t, carry):
            for k in range(TOP_K):
                j = t * TOP_K + k
                pltpu.make_async_copy(
                    ys_ref.at[pl.ds(pl.multiple_of(d_ref[j] * ROW_TILES, ROW_TILES), ROW_TILES), :],
                    sl_ref.at[slot, pl.ds(pl.multiple_of(j * ROW_TILES, ROW_TILES), ROW_TILES), :],
                    sem.at[slot]).start(priority=k % DMA_QUEUES)
            return carry

        lax.fori_loop(0, tm, body, 0)

    @pl.when(g == 0)
    def _():
        request(dcur_ref, 0)

    @pl.when(g + 1 < steps)
    def _():
        request(dnext_ref, 1 - cur)

    pltpu.make_async_copy(ys_ref.at[pl.ds(0, tm * TOP_K * ROW_TILES), :], sl_ref.at[cur], sem.at[cur]).wait()

    gates = gate_ref[...]
    per_tok = TOP_K * ROW_TILES
    cols = []
    for s in range(ROW_TILES):
        acc = None
        for j in range(TOP_K):
            part = sl_ref[cur, pl.ds(j * ROW_TILES + s, tm, stride=per_tok), :] * gates[:, j:j + 1]
            acc = part if acc is None else acc + part
        cols.append(acc)
    f = jnp.concatenate(cols, axis=-1)
    o_ref[0] = x1_ref[0] + gt_ref[0] * _rms(f, g_ref[...])


def _final(ys, dest_flat, gates, x1, gt, g_post_ffn, tok_off, tm, per_token):
    bsz, t, _ = x1.shape
    tiles = t // tm
    off = tok_off // tm
    steps = bsz * tiles
    kern = functools.partial(_final_kernel, tm=tm, steps=steps)

    def dspec(ahead):
        return pl.BlockSpec((tm * TOP_K,), lambda b, i: (off + jnp.minimum(b * tiles + i + ahead, steps - 1),),
                            memory_space=pltpu.SMEM)

    return pl.pallas_call(
        kern,
        grid=(bsz, tiles),
        in_specs=[dspec(0), dspec(1),
                  pl.BlockSpec(memory_space=pl.ANY),
                  pl.BlockSpec((tm, TOP_K), lambda b, i: (off + b * tiles + i, 0)),
                  pl.BlockSpec((1, tm, D_MODEL), lambda b, i: (b, i, 0)),
                  _mod_spec(per_token, tm),
                  pl.BlockSpec((1, D_MODEL), lambda b, i: (0, 0))],
        out_specs=pl.BlockSpec((1, tm, D_MODEL), lambda b, i: (b, i, 0)),
        out_shape=jax.ShapeDtypeStruct((bsz, t, D_MODEL), F32),
        scratch_shapes=[pltpu.VMEM((2, tm * TOP_K * ROW_TILES, LANES), F32), pltpu.SemaphoreType.DMA((2,))],
        compiler_params=_cparams(("arbitrary", "arbitrary")),
        name="final",
    )(dest_flat, dest_flat, ys, gates, x1, gt, g_post_ffn.reshape(1, D_MODEL))


def _moe(ht_a, ht_b, gsel, rank, counts, w_gu, b_gu, w_down, b_down, tm, bm):
    n = gsel.shape[0]
    cnt = counts[:, 0].astype(jnp.int32)
    nblk = (cnt + bm - 1) // bm
    blk_end = jnp.cumsum(nblk)
    blk_start = blk_end - nblk
    n_blocks = -(-(n * TOP_K) // bm) + N_EXPERTS
    blocks = jnp.arange(n_blocks, dtype=jnp.int32)
    block_expert = jnp.minimum(jnp.sum((blk_end[None, :] <= blocks[:, None]).astype(jnp.int32), axis=1),
                               N_EXPERTS - 1).astype(jnp.int32)
    ids = jnp.arange(N_EXPERTS, dtype=jnp.int32)
    later = jnp.where((nblk[None, :] > 0) & (ids[None, :] > ids[:, None]), ids[None, :], N_EXPERTS)
    nearest = jnp.min(later, axis=1)
    next_used = jnp.where(nearest < N_EXPERTS, nearest, ids).astype(jnp.int32)
    n_used = blk_end[-1:].astype(jnp.int32)
    row_start = (blk_start * bm).astype(jnp.int32)
    row_end = (blk_end * bm).astype(jnp.int32)
    dest, gates = _dest(gsel, rank, row_start.astype(F32).reshape(1, N_EXPERTS), DEST_TILE)
    dest_flat = dest.reshape(n * TOP_K)
    xs = _scatter_rows(ht_a, ht_b, dest_flat, cnt, row_start, row_end, n_used, n_blocks, SCATTER_TILE, bm)
    owns = (blocks[:, None] >= blk_start[None, :]) & (blocks[:, None] < blk_end[None, :])

    def per_block(table):
        return jnp.sum(jnp.where(owns, table[None, :], 0), axis=1).astype(jnp.int32)

    block_valid = jnp.clip(per_block(cnt) - (blocks - per_block(blk_start)) * bm, 0, bm).astype(jnp.int32)
    ys = _experts(xs, block_expert, per_block(next_used), block_valid, n_used, w_gu, b_gu, w_down, b_down, bm)
    return ys, dest_flat, gates


def _mix_path(x, mod, pool_buf, conv_buf, s0, start_pos, counts0, lp, tm, nb, per_token):
    (g_pre_mix, g_post_mix, g_pre_ffn, w_in_p, w_bd, pool_scale, conv_w, alog_row, dtb_row, norm_w,
     w_out_b, router_w, router_b) = lp
    bsz, t, _ = x.shape
    sh_mix, sc_mix, gt_mix, sh_ffn, sc_ffn, gt_ffn = (mod[:, i] for i in range(6))

    def shape_mod(m):
        if per_token:
            return jnp.repeat(m, t, axis=0).reshape(1, bsz * t, D_MODEL)
        return m.reshape(bsz, 1, D_MODEL)

    sh_mix, sc_mix, gt_mix, sh_ffn, sc_ffn, gt_ffn = map(shape_mod, (sh_mix, sc_mix, gt_mix, sh_ffn, sc_ffn, gt_ffn))
    xf = x.reshape(1, bsz * t, D_MODEL) if per_token else x
    proj = _in_proj(xf, sc_mix, sh_mix, g_pre_mix, w_in_p, PROJ_TILE, per_token).reshape(bsz, t, PROJ_WIDTH)
    tseq = min(tm, t)
    o_gdn, s_new = _gdn(proj, conv_buf, s0, conv_w, alog_row, dtb_row, norm_w, nb, tseq)
    pool_out = _pool(proj, pool_buf, w_bd, pool_scale, nb * POOL_SEQS if per_token else nb, min(POOL_TILE, t),
                     start_pos)
    fl = (lambda a: a.reshape(1, bsz * t, a.shape[-1])) if per_token else (lambda a: a)
    x1, ht, gsel, rank, counts = _out_proj(xf, fl(pool_out), fl(o_gdn), gt_mix, sc_ffn, sh_ffn, g_post_mix,
                                           g_pre_ffn, w_out_b, router_w, router_b, counts0, OUT_TILE, per_token)

    def last_rows(buf, off, width):
        keep = buf.shape[1]
        if t >= keep:
            return proj[:, t - keep:, off:off + width]
        return jnp.concatenate([buf[:, t:], proj[:, :, off:off + width]], axis=1)

    pool_new = last_rows(pool_buf, OFF_U, POOL_WIDTH)
    conv_new = last_rows(conv_buf, OFF_QKV, CONV_CH)
    return x1, ht, gsel, rank, counts, gt_ffn, pool_new, conv_new, s_new


def kernel(x_prompt, x_sample, c_prompt, c_sample, state_pool, state_conv, state_ssm, w_ada, b_ada, g_pre_mix, g_post_mix, g_pre_ffn, g_post_ffn, w_in, w_pool, pool_scale, conv_w, a_log, dt_bias, gdn_norm_w, w_out, router_w, router_b, w_gu, b_gu, w_down, b_down):
    depth = w_ada.shape[0]
    bp, tp, _ = x_prompt.shape
    bs, ts, _ = x_sample.shape
    tm = TOKEN_TILE
    y_p, y_s = x_prompt, x_sample
    outs = [[] for _ in range(6)]
    for l in range(depth):
        mod = _ada(jnp.concatenate([c_prompt, c_sample], axis=0), w_ada[l], b_ada[l]).reshape(bp + bs, 6, D_MODEL)
        w_in_p = jnp.swapaxes(w_in[l], 0, 1)
        w_bd = jax.scipy.linalg.block_diag(*[w_pool[l, g] for g in range(len(POOL_WINDOWS))])
        pad_row = lambda v: jnp.pad(v, (0, LANES - v.shape[0])).reshape(1, LANES)
        lp = (g_pre_mix[l], g_post_mix[l], g_pre_ffn[l], w_in_p, w_bd, pool_scale[l], conv_w[l],
              pad_row(a_log[l]), pad_row(dt_bias[l]), gdn_norm_w[l].reshape(1, HEAD_DIM),
              w_out[l].astype(BF16), router_w[l], router_b[l])
        zero_pool = jnp.zeros((bp, POOL_BUF, POOL_WIDTH), F32)
        zero_conv = jnp.zeros((bp, CONV_WIDTH - 1, CONV_CH), F32)
        zero_ssm = (jnp.zeros((1, bp, GDN_HEADS, HEAD_DIM, HEAD_DIM), F32), 0)
        no_counts = jnp.zeros((N_EXPERTS, 1), F32)
        x1p, htp, gp, rp, cnt_p, gtp, pp, cp, sp = _mix_path(
            y_p, mod[:bp], zero_pool, zero_conv, zero_ssm, 0, no_counts, lp, tm, 1, False)
        x1s, hts, gs, rs, counts, gts, ps, cs, ss = _mix_path(
            y_s, mod[bp:], state_pool[l], state_conv[l], (state_ssm, l), PAST_LEN, cnt_p, lp, tm, 8, True)
        gsel = jnp.concatenate([gp, gs], axis=0)
        rank = jnp.concatenate([rp, rs], axis=0)
        ys, dest_flat, gates = _moe(htp, hts, gsel, rank, counts, w_gu[l], b_gu[l], w_down[l], b_down[l],
                                    tm, MOE_BLOCK)
        y_p = _final(ys, dest_flat, gates, x1p, gtp, g_post_ffn[l], 0, tm, False)
        y_s = _final(ys, dest_flat, gates, x1s, gts, g_post_ffn[l], bp * tp, tm, True).reshape(bs, ts, D_MODEL)
        for lst, val in zip(outs, (pp, cp, sp, ps, cs, ss)):
            lst.append(val)
    return (y_p, y_s) + tuple(o[0][None] if depth == 1 else jnp.stack(o) for o in outs)
```

```python
import functools
import math

import jax
import jax.numpy as jnp
from jax import lax
from jax.experimental import pallas as pl
from jax.experimental.pallas import tpu as pltpu

F32 = jnp.float32
BF16 = jnp.bfloat16

D_MODEL = 1024
PAST_LEN = 16384
POOL_WINDOWS = (2, 4, 8, 16)
POOL_WIDTH = D_MODEL // 4
POOL_GROUP = POOL_WIDTH // len(POOL_WINDOWS)
POOL_BUF = max(POOL_WINDOWS) - 1
GDN_WIDTH = D_MODEL - POOL_WIDTH
HEAD_DIM = 128
GDN_HEADS = GDN_WIDTH // HEAD_DIM
CONV_WIDTH = 4
CONV_CH = 3 * GDN_WIDTH
GDN_CHUNK = 64
N_EXPERTS = 32
TOP_K = 4
D_FF = D_MODEL
SWIGLU_LIMIT = 7.0
SWIGLU_ALPHA = 1.702
EPS = 1e-6

SUBLANES = 8
LANES = 128
ROW_TILES = D_MODEL // LANES
OFF_QKV = 0
OFF_Z = OFF_QKV + CONV_CH
OFF_U = OFF_Z + GDN_WIDTH
OFF_AB = OFF_U + POOL_WIDTH
PROJ_WIDTH = OFF_AB + LANES
assert OFF_Z % GDN_WIDTH == 0 and OFF_U % POOL_WIDTH == 0 and OFF_AB % LANES == 0
TOKEN_TILE = 256
PROJ_TILE = 512
SCATTER_TILE = 512
OUT_TILE = 512
DEST_TILE = 1536
POOL_TILE = 1024
POOL_SEQS = 4
GDN_ITEM_GROUPS = 1
MOE_BLOCK = 1024
EXPERT_COL_CHUNKS = 2
DMA_QUEUES = 2
VMEM_LIMIT = 56 * 1024 * 1024


def _cparams(sem):
    return pltpu.CompilerParams(dimension_semantics=sem, vmem_limit_bytes=VMEM_LIMIT)


def _dot(a, b):
    return jnp.dot(a.astype(BF16), b.astype(BF16), preferred_element_type=F32)


def _dot_nt(a, b):
    return lax.dot_general(a.astype(BF16), b.astype(BF16), (((1,), (1,)), ((), ())),
                           preferred_element_type=F32)


def _silu(x):
    return x * jax.nn.sigmoid(x)


def _rms(x, g):
    return x * lax.rsqrt(jnp.mean(x * x, axis=-1, keepdims=True) + EPS) * g


def _ada_kernel(c_ref, w_ref, b_ref, o_ref):
    o_ref[...] = _dot(_silu(c_ref[...]), w_ref[...]) + b_ref[...]


def _ada(c, w_ada, b_ada):
    n = c.shape[0]
    cols = w_ada.shape[1]
    blk = D_MODEL
    return pl.pallas_call(
        _ada_kernel,
        grid=(cols // blk,),
        in_specs=[pl.BlockSpec((n, D_MODEL), lambda j: (0, 0)),
                  pl.BlockSpec((D_MODEL, blk), lambda j: (0, j)),
                  pl.BlockSpec((1, blk), lambda j: (0, j))],
        out_specs=pl.BlockSpec((n, blk), lambda j: (0, j)),
        out_shape=jax.ShapeDtypeStruct((n, cols), F32),
        compiler_params=_cparams(("arbitrary",)),
        name="ada",
    )(c, w_ada, b_ada.reshape(1, cols))


def _in_kernel(x_ref, sc_ref, sh_ref, g_ref, wt_ref, o_ref, wb):
    @pl.when((pl.program_id(0) == 0) & (pl.program_id(1) == 0))
    def _():
        o1, o2, o3 = POOL_WIDTH, POOL_WIDTH + CONV_CH, POOL_WIDTH + CONV_CH + GDN_WIDTH
        wb[OFF_QKV:OFF_QKV + CONV_CH] = wt_ref[o1:o2].astype(BF16)
        wb[OFF_Z:OFF_Z + GDN_WIDTH] = wt_ref[o2:o3].astype(BF16)
        wb[OFF_U:OFF_U + POOL_WIDTH] = wt_ref[:o1].astype(BF16)
        tail = wt_ref.shape[0] - o3
        wb[OFF_AB:] = jnp.concatenate(
            [wt_ref[o3:], jnp.zeros((LANES - tail, D_MODEL), F32)], axis=0).astype(BF16)

    h = _rms(x_ref[0], g_ref[...]) * (1.0 + sc_ref[0]) + sh_ref[0]
    o_ref[0] = _dot_nt(h, wb[...])


def _mod_spec(per_token, tm):
    if per_token:
        return pl.BlockSpec((1, tm, D_MODEL), lambda b, t: (0, t, 0))
    return pl.BlockSpec((1, 1, D_MODEL), lambda b, t: (b, 0, 0))


def _in_proj(x, sc, sh, g, w_in, tm, per_token):
    bsz, t, _ = x.shape
    return pl.pallas_call(
        _in_kernel,
        grid=(bsz, t // tm),
        in_specs=[pl.BlockSpec((1, tm, D_MODEL), lambda b, i: (b, i, 0)),
                  _mod_spec(per_token, tm), _mod_spec(per_token, tm),
                  pl.BlockSpec((1, D_MODEL), lambda b, i: (0, 0)),
                  pl.BlockSpec(w_in.shape, lambda b, i: (0, 0), pipeline_mode=pl.Buffered(1))],
        out_specs=pl.BlockSpec((1, tm, PROJ_WIDTH), lambda b, i: (b, i, 0)),
        out_shape=jax.ShapeDtypeStruct((bsz, t, PROJ_WIDTH), F32),
        scratch_shapes=[pltpu.VMEM((PROJ_WIDTH, D_MODEL), BF16)],
        compiler_params=_cparams(("arbitrary", "arbitrary")),
        name="in_proj",
    )(x, sc, sh, g.reshape(1, D_MODEL), w_in)


def _cumsum_rows(x):
    n = x.shape[0]
    row = lax.broadcasted_iota(jnp.int32, x.shape, 0)
    s = 1
    while s < n:
        x = x + jnp.where(row >= s, pltpu.roll(x, s, axis=0), 0.0)
        s *= 2
    return x


def _gdn_kernel(qkv_ref, z_ref, ab_ref, cs_ref, cw_ref, s0_ref, alog_ref, dtb_ref, nw_ref, o_ref, sn_ref,
                s_scr, prev_scr, *, nb, tm, tmp, chunk, group_items):
    ti = pl.program_id(1)
    nt = pl.num_programs(1)
    n_chunks = tmp // chunk
    n_sq = int(math.log2(chunk)) - 1

    @pl.when(ti == 0)
    def _():
        s_scr[...] = s0_ref[...]
        for b in range(nb):
            prev_scr[b] = jnp.concatenate(
                [jnp.zeros((SUBLANES - (CONV_WIDTH - 1), CONV_CH), F32), cs_ref[b]], axis=0)

    rowi = lax.broadcasted_iota(jnp.int32, (tmp, LANES), 0)
    ri = lax.broadcasted_iota(jnp.int32, (chunk, chunk), 0)
    ci = lax.broadcasted_iota(jnp.int32, (chunk, chunk), 1)
    tril = ri >= ci
    strict = ri > ci
    eye = (ri == ci).astype(F32)
    neg_a = -jnp.exp(alog_ref[...])
    dtb = dtb_ref[...]
    nw = nw_ref[...]

    def padrows(x):
        if tm == tmp:
            return x
        return jnp.concatenate([x, jnp.zeros((tmp - tm, x.shape[1]), F32)], axis=0)

    cw = cw_ref[...]
    valid = rowi < tm
    items = []
    for b in range(nb):
        x = padrows(qkv_ref[b])
        e = jnp.concatenate([prev_scr[b], x], axis=0)
        acc = x * cw[CONV_WIDTH - 1:CONV_WIDTH]
        for d in range(1, CONV_WIDTH):
            acc = acc + pltpu.roll(e, d, axis=0)[SUBLANES:] * cw[CONV_WIDTH - 1 - d:CONV_WIDTH - d]
        if tm >= SUBLANES:
            prev_scr[b] = x[tm - SUBLANES:tm]
        y = _silu(acc)
        ab = padrows(ab_ref[b])
        g_all = jnp.where(valid, neg_a * jax.nn.softplus(ab + dtb), 0.0)
        beta_all = jnp.where(valid, jax.nn.sigmoid(ab), 0.0)
        for h in range(GDN_HEADS):
            qc = y[:, h * HEAD_DIM:(h + 1) * HEAD_DIM]
            kc = y[:, GDN_WIDTH + h * HEAD_DIM:GDN_WIDTH + (h + 1) * HEAD_DIM]
            v = y[:, 2 * GDN_WIDTH + h * HEAD_DIM:2 * GDN_WIDTH + (h + 1) * HEAD_DIM]
            q = qc * lax.rsqrt(jnp.sum(qc * qc, axis=-1, keepdims=True) + EPS) * (HEAD_DIM ** -0.5)
            k = kc * lax.rsqrt(jnp.sum(kc * kc, axis=-1, keepdims=True) + EPS)
            gb = jnp.broadcast_to(g_all[:, h:h + 1], (tmp, LANES))
            bb = jnp.broadcast_to(beta_all[:, GDN_HEADS + h:GDN_HEADS + h + 1], (tmp, LANES))
            for c in range(n_chunks):
                sl = slice(c * chunk, (c + 1) * chunk)
                items.append(dict(b=b, h=h, c=c, q=q[sl], k=k[sl], v=v[sl], bb=bb[sl], gb=gb[sl]))

    def within_chunk(group):
        for it in group:
            gcb = _cumsum_rows(it["gb"])
            gct = jnp.transpose(gcb)[:chunk]
            diff = gcb[:, :chunk] - gct
            it["decay"] = jnp.where(tril, jnp.exp(jnp.where(tril, diff, 0.0)), 0.0)
            it["kb"] = it["k"] * it["bb"]
            it["eg"] = jnp.exp(gcb)
            gl = gcb[chunk - 1:chunk]
            it["egl"] = jnp.exp(gl)
            it["kdt"] = jnp.transpose(it["k"] * jnp.exp(gl - gcb))
        for it in group:
            kq = _dot_nt(jnp.concatenate([it["kb"], it["q"]], axis=0), it["k"])
            it["a"] = jnp.where(strict, kq[:chunk] * it["decay"], 0.0)
            it["qk"] = jnp.where(tril, kq[chunk:] * it["decay"], 0.0)
        for it in group:
            it["inv"] = eye - it["a"]
            it["p"] = _dot(it["a"], it["a"])
        for lvl in range(n_sq):
            for it in group:
                if lvl < n_sq - 1:
                    r = _dot(jnp.concatenate([it["inv"], it["p"]], axis=0), it["p"])
                    it["inv"] = it["inv"] + r[:chunk]
                    it["p"] = r[chunk:]
                else:
                    it["inv"] = it["inv"] + _dot(it["inv"], it["p"])
        for it in group:
            it["wu"] = _dot(it["inv"], jnp.concatenate([it["kb"] * it["eg"], it["v"] * it["bb"]], axis=1))
        for it in group:
            mn = _dot(it["kdt"], it["wu"])
            qo = _dot(it["qk"], it["wu"])
            it["mp"], it["nc"] = mn[:, :HEAD_DIM], mn[:, HEAD_DIM:]
            it["qp"] = it["q"] * it["eg"] - qo[:, :HEAD_DIM]
            it["oc"] = qo[:, HEAD_DIM:]

    for g0 in range(0, len(items), group_items):
        within_chunk(items[g0:g0 + group_items])

    state = {(b, h): s_scr[b, h] for b in range(nb) for h in range(GDN_HEADS)}
    for c in range(n_chunks):
        for it in items:
            if it["c"] != c:
                continue
            s = state[(it["b"], it["h"])]
            r = _dot(jnp.concatenate([it["qp"], it["mp"]], axis=0), s)
            it["o"] = r[:chunk] + it["oc"]
            state[(it["b"], it["h"])] = s * it["egl"] - r[chunk:] + it["nc"]
    for (b, h), s in state.items():
        s_scr[b, h] = s

    for b in range(nb):
        zf = padrows(z_ref[b])
        for h in range(GDN_HEADS):
            outs = [it["o"] for it in items if it["b"] == b and it["h"] == h]
            o = outs[0] if n_chunks == 1 else jnp.concatenate(outs, axis=0)
            o = (o * lax.rsqrt(jnp.mean(o * o, axis=-1, keepdims=True) + EPS) * nw
                 * _silu(zf[:, h * HEAD_DIM:(h + 1) * HEAD_DIM]))
            o_ref[b, :, h * HEAD_DIM:(h + 1) * HEAD_DIM] = o[:tm]

    @pl.when(ti == nt - 1)
    def _():
        sn_ref[...] = s_scr[...]


def _gdn(proj, state_conv, s0, conv_w, alog_row, dtb_row, norm_w, nb, tm):
    bsz, t, _ = proj.shape
    tmp = max(tm, SUBLANES)
    chunk = min(GDN_CHUNK, tmp)
    row = pl.BlockSpec((1, LANES), lambda b, i: (0, 0))
    s_all, layer = s0
    sspec = pl.BlockSpec((nb, GDN_HEADS, HEAD_DIM, HEAD_DIM), lambda b, i: (b, 0, 0, 0))
    s_in = pl.BlockSpec((None, nb, GDN_HEADS, HEAD_DIM, HEAD_DIM), lambda b, i: (layer, b, 0, 0, 0))
    n_items = nb * GDN_HEADS * (tmp // chunk)
    kern = functools.partial(_gdn_kernel, nb=nb, tm=tm, tmp=tmp, chunk=chunk,
                             group_items=n_items // GDN_ITEM_GROUPS)
    return pl.pallas_call(
        kern,
        grid=(bsz // nb, t // tm),
        in_specs=[pl.BlockSpec((nb, tm, CONV_CH), lambda b, i: (b, i, OFF_QKV // CONV_CH)),
                  pl.BlockSpec((nb, tm, GDN_WIDTH), lambda b, i: (b, i, OFF_Z // GDN_WIDTH)),
                  pl.BlockSpec((nb, tm, LANES), lambda b, i: (b, i, OFF_AB // LANES)),
                  pl.BlockSpec((nb, CONV_WIDTH - 1, CONV_CH), lambda b, i: (b, 0, 0)),
                  pl.BlockSpec((CONV_WIDTH, CONV_CH), lambda b, i: (0, 0)),
                  s_in, row, row, row],
        out_specs=[pl.BlockSpec((nb, tm, GDN_WIDTH), lambda b, i: (b, i, 0)), sspec],
        out_shape=[jax.ShapeDtypeStruct((bsz, t, GDN_WIDTH), F32),
                   jax.ShapeDtypeStruct(s_all.shape[1:], F32)],
        scratch_shapes=[pltpu.VMEM((nb, GDN_HEADS, HEAD_DIM, HEAD_DIM), F32),
                        pltpu.VMEM((nb, SUBLANES, CONV_CH), F32)],
        compiler_params=_cparams(("arbitrary", "arbitrary")),
        name="gdn",
    )(proj, proj, proj, state_conv, conv_w, s_all, alog_row, dtb_row, norm_w)


def _pool_kernel(u_ref, st_ref, w_ref, sc_ref, o_ref, prev_scr, *, nb, tm, tmp, start_pos):
    ti = pl.program_id(1)
    halo = 2 * SUBLANES

    @pl.when(ti == 0)
    def _():
        for b in range(nb):
            prev_scr[b] = jnp.concatenate([jnp.zeros((halo - POOL_BUF, POOL_WIDTH), F32), st_ref[b]], axis=0)

    lane = lax.broadcasted_iota(jnp.int32, (tmp, POOL_WIDTH), 1)
    rowi = lax.broadcasted_iota(jnp.int32, (tmp, POOL_WIDTH), 0)
    pos1 = start_pos + ti * tm + rowi + 1
    grp = lane // POOL_GROUP
    win = jnp.where(grp == 0, POOL_WINDOWS[0],
                    jnp.where(grp == 1, POOL_WINDOWS[1], jnp.where(grp == 2, POOL_WINDOWS[2], POOL_WINDOWS[3])))
    cnt = jnp.minimum(pos1, win).astype(F32)
    for b in range(nb):
        x = u_ref[b]
        if tm != tmp:
            x = jnp.concatenate([x, jnp.zeros((tmp - tm, POOL_WIDTH), F32)], axis=0)
        e = jnp.concatenate([prev_scr[b], x], axis=0)
        sums = []
        s = e
        for w in POOL_WINDOWS:
            s = s + pltpu.roll(s, w // 2, axis=0)
            sums.append(s[halo:])
        sel = jnp.where(grp == 0, sums[0], jnp.where(grp == 1, sums[1], jnp.where(grp == 2, sums[2], sums[3])))
        pooled = sel / cnt - x
        out = _dot(pooled, w_ref[...]) * sc_ref[...]
        o_ref[b] = out[:tm]
        if tm >= halo:
            prev_scr[b] = x[tm - halo:tm]


def _pool(proj, state_pool, w_bd, pool_scale, nb, tm, start_pos):
    bsz, t, _ = proj.shape
    tmp = max(tm, SUBLANES)
    kern = functools.partial(_pool_kernel, nb=nb, tm=tm, tmp=tmp, start_pos=start_pos)
    return pl.pallas_call(
        kern,
        grid=(bsz // nb, t // tm),
        in_specs=[pl.BlockSpec((nb, tm, POOL_WIDTH), lambda b, i: (b, i, OFF_U // POOL_WIDTH)),
                  pl.BlockSpec((nb, POOL_BUF, POOL_WIDTH), lambda b, i: (b, 0, 0)),
                  pl.BlockSpec((POOL_WIDTH, POOL_WIDTH), lambda b, i: (0, 0)),
                  pl.BlockSpec((1, POOL_WIDTH), lambda b, i: (0, 0))],
        out_specs=pl.BlockSpec((nb, tm, POOL_WIDTH), lambda b, i: (b, i, 0)),
        out_shape=jax.ShapeDtypeStruct((bsz, t, POOL_WIDTH), F32),
        scratch_shapes=[pltpu.VMEM((nb, 2 * SUBLANES, POOL_WIDTH), F32)],
        compiler_params=_cparams(("arbitrary", "arbitrary")),
        name="pool",
    )(proj, state_pool, w_bd, pool_scale.reshape(1, POOL_WIDTH))


def _out_kernel(x_ref, po_ref, og_ref, gt_ref, sc_ref, sh_ref, gpm_ref, gpf_ref, wo_ref, rw_ref, rb_ref, c0_ref,
                x1_ref, ht_ref, g_ref, r_ref, cnt_ref, carry, *, tm):
    @pl.when((pl.program_id(0) == 0) & (pl.program_id(1) == 0))
    def _():
        carry[...] = c0_ref[...]

    mix = _dot(po_ref[0], wo_ref[:POOL_WIDTH]) + _dot(og_ref[0], wo_ref[POOL_WIDTH:])
    x1 = x_ref[0] + gt_ref[0] * _rms(mix, gpm_ref[...])
    x1_ref[0] = x1
    h = _rms(x1, gpf_ref[...]) * (1.0 + sc_ref[0]) + sh_ref[0]
    for s in range(ROW_TILES):
        ht_ref[pl.ds(s, tm, stride=ROW_TILES), :] = h[:, s * LANES:(s + 1) * LANES]
    logits = _dot_nt(rw_ref[...], h) + rb_ref[...]
    eid = lax.broadcasted_iota(jnp.int32, logits.shape, 0).astype(F32)
    work = logits
    sel = jnp.zeros(logits.shape, jnp.bool_)
    top = None
    den = jnp.zeros((1, tm), F32)
    for j in range(TOP_K):
        m = jnp.max(work, axis=0, keepdims=True)
        idx = jnp.min(jnp.where(work == m, eid, float(N_EXPERTS)), axis=0, keepdims=True)
        hit = eid == idx
        sel = sel | hit
        work = jnp.where(hit, -jnp.inf, work)
        if j == 0:
            top = m
        den = den + jnp.exp(m - top)
    gates = jnp.exp(logits - top) / den
    g_ref[...] = jnp.transpose(jnp.where(sel, gates, -1.0))
    mask = sel.astype(F32)
    ri = lax.broadcasted_iota(jnp.int32, (tm, tm), 0)
    ci = lax.broadcasted_iota(jnp.int32, (tm, tm), 1)
    c = carry[...]
    r_ref[...] = jnp.transpose(_dot(mask, (ri < ci).astype(F32)) + c)
    c = c + jnp.sum(mask, axis=1, keepdims=True)
    carry[...] = c
    cnt_ref[...] = c


def _out_proj(x, pool_out, o_gdn, gt, sc, sh, g_post_mix, g_pre_ffn, w_out_b, router_w, router_b, counts0,
              tm, per_token):
    bsz, t, _ = x.shape
    n = bsz * t
    tiles = t // tm

    def tok(width):
        return pl.BlockSpec((1, tm, width), lambda b, i: (b, i, 0))

    def const(shape):
        return pl.BlockSpec(shape, lambda b, i: tuple(0 for _ in shape))

    kern = functools.partial(_out_kernel, tm=tm)
    return pl.pallas_call(
        kern,
        grid=(bsz, tiles),
        in_specs=[tok(D_MODEL), tok(POOL_WIDTH), tok(GDN_WIDTH),
                  _mod_spec(per_token, tm), _mod_spec(per_token, tm), _mod_spec(per_token, tm),
                  const((1, D_MODEL)), const((1, D_MODEL)), const((D_MODEL, D_MODEL)),
                  const((N_EXPERTS, D_MODEL)), const((N_EXPERTS, 1)), const((N_EXPERTS, 1))],
        out_specs=[tok(D_MODEL),
                   pl.BlockSpec((tm * ROW_TILES, LANES), lambda b, i: (b * tiles + i, 0)),
                   pl.BlockSpec((tm, N_EXPERTS), lambda b, i: (b * tiles + i, 0)),
                   pl.BlockSpec((tm, N_EXPERTS), lambda b, i: (b * tiles + i, 0)),
                   const((N_EXPERTS, 1))],
        out_shape=[jax.ShapeDtypeStruct((bsz, t, D_MODEL), F32),
                   jax.ShapeDtypeStruct((n * ROW_TILES, LANES), F32),
                   jax.ShapeDtypeStruct((n, N_EXPERTS), F32),
                   jax.ShapeDtypeStruct((n, N_EXPERTS), F32),
                   jax.ShapeDtypeStruct((N_EXPERTS, 1), F32)],
        scratch_shapes=[pltpu.VMEM((N_EXPERTS, 1), F32)],
        compiler_params=_cparams(("arbitrary", "arbitrary")),
        name="out_proj",
    )(x, pool_out, o_gdn, gt, sc, sh, g_post_mix.reshape(1, D_MODEL), g_pre_ffn.reshape(1, D_MODEL),
      w_out_b, jnp.swapaxes(router_w, 0, 1), router_b.reshape(N_EXPERTS, 1), counts0)


def _dest_kernel(g_ref, r_ref, start_ref, d_ref, gt_ref, *, tm):
    g = g_ref[...]
    mask = g >= 0.0
    maskf = mask.astype(F32)
    dest = start_ref[...] + r_ref[...]
    ri = lax.broadcasted_iota(jnp.int32, (N_EXPERTS, N_EXPERTS), 0)
    ci = lax.broadcasted_iota(jnp.int32, (N_EXPERTS, N_EXPERTS), 1)
    slot = _dot(maskf, (ri < ci).astype(F32))
    lane4 = lax.broadcasted_iota(jnp.int32, (tm, TOP_K), 1)
    d_out = jnp.zeros((tm, TOP_K), F32)
    g_out = jnp.zeros((tm, TOP_K), F32)
    for j in range(TOP_K):
        pick = mask & (slot == float(j))
        dj = jnp.sum(jnp.where(pick, dest, 0.0), axis=-1, keepdims=True)
        gj = jnp.sum(jnp.where(pick, g, 0.0), axis=-1, keepdims=True)
        d_out = jnp.where(lane4 == j, dj, d_out)
        g_out = jnp.where(lane4 == j, gj, g_out)
    d_ref[...] = d_out.astype(jnp.int32)
    gt_ref[...] = g_out


def _dest(gsel, rank, start_row, tm):
    n = gsel.shape[0]
    kern = functools.partial(_dest_kernel, tm=tm)
    return pl.pallas_call(
        kern,
        grid=(n // tm,),
        in_specs=[pl.BlockSpec((tm, N_EXPERTS), lambda i: (i, 0)),
                  pl.BlockSpec((tm, N_EXPERTS), lambda i: (i, 0)),
                  pl.BlockSpec((1, N_EXPERTS), lambda i: (0, 0))],
        out_specs=[pl.BlockSpec((tm, TOP_K), lambda i: (i, 0)),
                   pl.BlockSpec((tm, TOP_K), lambda i: (i, 0))],
        out_shape=[jax.ShapeDtypeStruct((n, TOP_K), jnp.int32),
                   jax.ShapeDtypeStruct((n, TOP_K), F32)],
        compiler_params=_cparams(("arbitrary",)),
        name="dest",
    )(gsel, rank, start_row)


def _row_copy(src, dst, s_row, d_row, sem):
    return pltpu.make_async_copy(src.at[pl.ds(pl.multiple_of(s_row * ROW_TILES, ROW_TILES), ROW_TILES), :],
                                 dst.at[pl.ds(pl.multiple_of(d_row * ROW_TILES, ROW_TILES), ROW_TILES), :], sem)


def _scatter_kernel(cnt_ref, start_ref, end_ref, nu_ref, dest_ref, ha_ref, hb_ref, xs_ref, sem, *,
                    tm, bm, n_blocks, tiles_a):
    i = pl.program_id(0)

    def scatter_from(ht_ref):
        def body(t, carry):
            src = ht_ref.at[pl.ds(pl.multiple_of(t * ROW_TILES, ROW_TILES), ROW_TILES), :]
            for k in range(TOP_K):
                d = dest_ref[t * TOP_K + k]
                pltpu.make_async_copy(
                    src, xs_ref.at[pl.ds(pl.multiple_of(d * ROW_TILES, ROW_TILES), ROW_TILES), :],
                    sem).start(priority=k % DMA_QUEUES)
            return carry

        lax.fori_loop(0, tm, body, 0)

    @pl.when(i < tiles_a)
    def _():
        scatter_from(ha_ref)

    @pl.when(i >= tiles_a)
    def _():
        scatter_from(hb_ref)

    ht_ref = hb_ref

    def tile_copy(u):
        return pltpu.make_async_copy(
            ht_ref, xs_ref.at[pl.ds(pl.multiple_of(u * (tm * ROW_TILES), tm * ROW_TILES), tm * ROW_TILES), :], sem)

    for k in range(TOP_K):
        tile_copy(0).wait()

    @pl.when(i == pl.num_programs(0) - 1)
    def _():
        per = bm // tm

        def tail_start(u, carry):
            tile_copy(u).start()
            return carry

        def tail_wait(u, carry):
            tile_copy(u).wait()
            return carry

        lax.fori_loop(nu_ref[0] * per, n_blocks * per, tail_start, 0)
        lax.fori_loop(nu_ref[0] * per, n_blocks * per, tail_wait, 0)

        def per_expert(e, carry):
            lo = start_ref[e] + cnt_ref[e]
            hi = end_ref[e]

            def fill(r, c2):
                _row_copy(ht_ref, xs_ref, 0, r, sem).start()
                return c2

            lax.fori_loop(lo, hi, fill, 0)

            def drain(r, c2):
                _row_copy(ht_ref, xs_ref, 0, r, sem).wait()
                return c2

            lax.fori_loop(lo, hi, drain, 0)
            return carry

        lax.fori_loop(0, N_EXPERTS, per_expert, 0)


def _scatter_rows(ht_a, ht_b, dest_flat, cnt, start, end, n_used, n_blocks, tm, bm):
    rows = n_blocks * bm
    tiles_a, tiles_b = ht_a.shape[0] // (tm * ROW_TILES), ht_b.shape[0] // (tm * ROW_TILES)
    assert bm % tm == 0 and ht_a.shape[0] % (tm * ROW_TILES) == 0 and ht_b.shape[0] % (tm * ROW_TILES) == 0
    assert tiles_b >= 1
    kern = functools.partial(_scatter_kernel, tm=tm, bm=bm, n_blocks=n_blocks, tiles_a=tiles_a)
    return pl.pallas_call(
        kern,
        grid_spec=pltpu.PrefetchScalarGridSpec(
            num_scalar_prefetch=4,
            grid=(tiles_a + tiles_b,),
            in_specs=[pl.BlockSpec((tm * TOP_K,), lambda i, *_: (i,), memory_space=pltpu.SMEM),
                      pl.BlockSpec((tm * ROW_TILES, LANES), lambda i, *_: (jnp.minimum(i, tiles_a - 1), 0)),
                      pl.BlockSpec((tm * ROW_TILES, LANES), lambda i, *_: (jnp.maximum(i - tiles_a, 0), 0))],
            out_specs=pl.BlockSpec(memory_space=pl.ANY),
            scratch_shapes=[pltpu.SemaphoreType.DMA(())]),
        out_shape=jax.ShapeDtypeStruct((rows * ROW_TILES, LANES), F32),
        compiler_params=_cparams(("arbitrary",)),
        name="scatter_rows",
    )(cnt, start, end, n_used, dest_flat, ht_a, ht_b)


def _expert_kernel(be_ref, nx_ref, nv_ref, nu_ref, xs_ref, wg_ref, bg_ref, wd_ref, bd_ref, ys_ref,
                   wg_f, wd_f, wg_s, wd_s, wsem, *, bm):
    j = pl.program_id(0)
    prev = be_ref[jnp.maximum(j - 1, 0)]
    fresh = (j == 0) | (be_ref[j] != prev)
    active = j < nu_ref[0]

    def weight_copies(e):
        return (pltpu.make_async_copy(wg_ref.at[e], wg_f, wsem.at[0]),
                pltpu.make_async_copy(wd_ref.at[e], wd_f, wsem.at[1]))

    @pl.when(j == 0)
    def _():
        for c in weight_copies(be_ref[0]):
            c.start()

    @pl.when(active & fresh)
    def _():
        for c in weight_copies(be_ref[j]):
            c.wait()
        wg_s[...] = wg_f[...].astype(BF16)
        wd_s[...] = wd_f[...].astype(BF16)

        @pl.when(nx_ref[j] != be_ref[j])
        def _():
            for c in weight_copies(nx_ref[j]):
                c.start()

    def run_rows(m):
        x = jnp.concatenate([xs_ref[pl.ds(s, m, stride=ROW_TILES), :] for s in range(ROW_TILES)], axis=-1)
        xb = x.astype(BF16)
        acts = []
        for c in range(EXPERT_COL_CHUNKS):
            w = D_FF // EXPERT_COL_CHUNKS
            lo = c * w
            gate = jnp.dot(xb, wg_s[:, lo:lo + w], preferred_element_type=F32) + bg_ref[0, :, lo:lo + w]
            up = (jnp.dot(xb, wg_s[:, D_FF + lo:D_FF + lo + w], preferred_element_type=F32)
                  + bg_ref[0, :, D_FF + lo:D_FF + lo + w])
            gate = jnp.minimum(gate, SWIGLU_LIMIT)
            up = jnp.clip(up, -SWIGLU_LIMIT, SWIGLU_LIMIT)
            acts.append(((up + 1.0) * (gate * jax.nn.sigmoid(SWIGLU_ALPHA * gate))).astype(BF16))
        act = jnp.concatenate(acts, axis=-1)
        for c in range(EXPERT_COL_CHUNKS):
            w = D_MODEL // EXPERT_COL_CHUNKS
            lo = c * w
            y = jnp.dot(act, wd_s[:, lo:lo + w], preferred_element_type=F32) + bd_ref[0, :, lo:lo + w]
            for s in range(w // LANES):
                ys_ref[pl.ds(lo // LANES + s, m, stride=ROW_TILES), :] = y[:, s * LANES:(s + 1) * LANES]
        if m < bm:
            ys_ref[pl.ds(m * ROW_TILES, (bm - m) * ROW_TILES), :] = jnp.zeros(((bm - m) * ROW_TILES, LANES), F32)

    quarter = bm // 4
    quarters = (nv_ref[j] + quarter - 1) // quarter
    for nq in range(1, 5):
        @pl.when(active & ((quarters == nq) if nq > 1 else (quarters <= 1)))
        def _(nq=nq):
            run_rows(nq * quarter)

    @pl.when(jnp.logical_not(active))
    def _():
        ys_ref[...] = jnp.zeros_like(ys_ref)


def _experts(xs, block_expert, next_expert, block_valid, n_used, w_gu, b_gu, w_down, b_down, bm):
    rows = xs.shape[0] // ROW_TILES
    n_blocks = rows // bm

    def blk(j, be, nx, nv, nu):
        return (jnp.minimum(j, nu[0] - 1), 0)

    def bsel(j, be, nx, nv, nu):
        return (be[j], 0, 0)

    hbm = pl.BlockSpec(memory_space=pl.ANY)
    kern = functools.partial(_expert_kernel, bm=bm)
    return pl.pallas_call(
        kern,
        grid_spec=pltpu.PrefetchScalarGridSpec(
            num_scalar_prefetch=4,
            grid=(n_blocks,),
            in_specs=[pl.BlockSpec((bm * ROW_TILES, LANES), blk), hbm,
                      pl.BlockSpec((1, 1, 2 * D_FF), bsel), hbm,
                      pl.BlockSpec((1, 1, D_MODEL), bsel)],
            out_specs=pl.BlockSpec((bm * ROW_TILES, LANES), lambda j, be, nx, nv, nu: (j, 0)),
            scratch_shapes=[pltpu.VMEM((D_MODEL, 2 * D_FF), F32),
                            pltpu.VMEM((D_FF, D_MODEL), F32),
                            pltpu.VMEM((D_MODEL, 2 * D_FF), BF16),
                            pltpu.VMEM((D_FF, D_MODEL), BF16),
                            pltpu.SemaphoreType.DMA((2,))]),
        out_shape=jax.ShapeDtypeStruct((rows * ROW_TILES, LANES), F32),
        compiler_params=_cparams(("arbitrary",)),
        name="experts",
    )(block_expert, next_expert, block_valid, n_used, xs, w_gu, b_gu.reshape(N_EXPERTS, 1, 2 * D_FF), w_down,
      b_down.reshape(N_EXPERTS, 1, D_MODEL))


def _final_kernel(dcur_ref, dnext_ref, ys_ref, gate_ref, x1_ref, gt_ref, g_ref, o_ref, sl_ref, sem, *, tm, steps):
    g = pl.program_id(0) * pl.num_programs(1) + pl.program_id(1)
    cur = g % 2

    def request(d_ref, slot):
        def body(t, carry):
            for k in range(TOP_K):
                j = t * TOP_K + k
                pltpu.make_async_copy(
                    ys_ref.at[pl.ds(pl.multiple_of(d_ref[j] * ROW_TILES, ROW_TILES), ROW_TILES), :],
                    sl_ref.at[slot, pl.ds(pl.multiple_of(j * ROW_TILES, ROW_TILES), ROW_TILES), :],
                    sem.at[slot]).start(priority=k % DMA_QUEUES)
            return carry

        lax.fori_loop(0, tm, body, 0)

    @pl.when(g == 0)
    def _():
        request(dcur_ref, 0)

    @pl.when(g + 1 < steps)
    def _():
        request(dnext_ref, 1 - cur)

    pltpu.make_async_copy(ys_ref.at[pl.ds(0, tm * TOP_K * ROW_TILES), :], sl_ref.at[cur], sem.at[cur]).wait()

    gates = gate_ref[...]
    per_tok = TOP_K * ROW_TILES
    cols = []
    for s in range(ROW_TILES):
        acc = None
        for j in range(TOP_K):
            part = sl_ref[cur, pl.ds(j * ROW_TILES + s, tm, stride=per_tok), :] * gates[:, j:j + 1]
            acc = part if acc is None else acc + part
        cols.append(acc)
    f = jnp.concatenate(cols, axis=-1)
    o_ref[0] = x1_ref[0] + gt_ref[0] * _rms(f, g_ref[...])


def _final(ys, dest_flat, gates, x1, gt, g_post_ffn, tok_off, tm, per_token):
    bsz, t, _ = x1.shape
    tiles = t // tm
    off = tok_off // tm
    steps = bsz * tiles
    kern = functools.partial(_final_kernel, tm=tm, steps=steps)

    def dspec(ahead):
        return pl.BlockSpec((tm * TOP_K,), lambda b, i: (off + jnp.minimum(b * tiles + i + ahead, steps - 1),),
                            memory_space=pltpu.SMEM)

    return pl.pallas_call(
        kern,
        grid=(bsz, tiles),
        in_specs=[dspec(0), dspec(1),
                  pl.BlockSpec(memory_space=pl.ANY),
                  pl.BlockSpec((tm, TOP_K), lambda b, i: (off + b * tiles + i, 0)),
                  pl.BlockSpec((1, tm, D_MODEL), lambda b, i: (b, i, 0)),
                  _mod_spec(per_token, tm),
                  pl.BlockSpec((1, D_MODEL), lambda b, i: (0, 0))],
        out_specs=pl.BlockSpec((1, tm, D_MODEL), lambda b, i: (b, i, 0)),
        out_shape=jax.ShapeDtypeStruct((bsz, t, D_MODEL), F32),
        scratch_shapes=[pltpu.VMEM((2, tm * TOP_K * ROW_TILES, LANES), F32), pltpu.SemaphoreType.DMA((2,))],
        compiler_params=_cparams(("arbitrary", "arbitrary")),
        name="final",
    )(dest_flat, dest_flat, ys, gates, x1, gt, g_post_ffn.reshape(1, D_MODEL))


def _moe(ht_a, ht_b, gsel, rank, counts, w_gu, b_gu, w_down, b_down, tm, bm):
    n = gsel.shape[0]
    cnt = counts[:, 0].astype(jnp.int32)
    nblk = (cnt + bm - 1) // bm
    blk_end = jnp.cumsum(nblk)
    blk_start = blk_end - nblk
    n_blocks = -(-(n * TOP_K) // bm) + N_EXPERTS
    blocks = jnp.arange(n_blocks, dtype=jnp.int32)
    block_expert = jnp.minimum(jnp.sum((blk_end[None, :] <= blocks[:, None]).astype(jnp.int32), axis=1),
                               N_EXPERTS - 1).astype(jnp.int32)
    ids = jnp.arange(N_EXPERTS, dtype=jnp.int32)
    later = jnp.where((nblk[None, :] > 0) & (ids[None, :] > ids[:, None]), ids[None, :], N_EXPERTS)
    nearest = jnp.min(later, axis=1)
    next_used = jnp.where(nearest < N_EXPERTS, nearest, ids).astype(jnp.int32)
    n_used = blk_end[-1:].astype(jnp.int32)
    row_start = (blk_start * bm).astype(jnp.int32)
    row_end = (blk_end * bm).astype(jnp.int32)
    dest, gates = _dest(gsel, rank, row_start.astype(F32).reshape(1, N_EXPERTS), DEST_TILE)
    dest_flat = dest.reshape(n * TOP_K)
    xs = _scatter_rows(ht_a, ht_b, dest_flat, cnt, row_start, row_end, n_used, n_blocks, SCATTER_TILE, bm)
    owns = (blocks[:, None] >= blk_start[None, :]) & (blocks[:, None] < blk_end[None, :])

    def per_block(table):
        return jnp.sum(jnp.where(owns, table[None, :], 0), axis=1).astype(jnp.int32)

    block_valid = jnp.clip(per_block(cnt) - (blocks - per_block(blk_start)) * bm, 0, bm).astype(jnp.int32)
    ys = _experts(xs, block_expert, per_block(next_used), block_valid, n_used, w_gu, b_gu, w_down, b_down, bm)
    return ys, dest_flat, gates


def _mix_path(x, mod, pool_buf, conv_buf, s0, start_pos, counts0, lp, tm, nb, per_token):
    (g_pre_mix, g_post_mix, g_pre_ffn, w_in_p, w_bd, pool_scale, conv_w, alog_row, dtb_row, norm_w,
     w_out_b, router_w, router_b) = lp
    bsz, t, _ = x.shape
    sh_mix, sc_mix, gt_mix, sh_ffn, sc_ffn, gt_ffn = (mod[:, i] for i in range(6))

    def shape_mod(m):
        if per_token:
            return jnp.repeat(m, t, axis=0).reshape(1, bsz * t, D_MODEL)
        return m.reshape(bsz, 1, D_MODEL)

    sh_mix, sc_mix, gt_mix, sh_ffn, sc_ffn, gt_ffn = map(shape_mod, (sh_mix, sc_mix, gt_mix, sh_ffn, sc_ffn, gt_ffn))
    xf = x.reshape(1, bsz * t, D_MODEL) if per_token else x
    proj = _in_proj(xf, sc_mix, sh_mix, g_pre_mix, w_in_p, PROJ_TILE, per_token).reshape(bsz, t, PROJ_WIDTH)
    tseq = min(tm, t)
    o_gdn, s_new = _gdn(proj, conv_buf, s0, conv_w, alog_row, dtb_row, norm_w, nb, tseq)
    pool_out = _pool(proj, pool_buf, w_bd, pool_scale, nb * POOL_SEQS if per_token else nb, min(POOL_TILE, t),
                     start_pos)
    fl = (lambda a: a.reshape(1, bsz * t, a.shape[-1])) if per_token else (lambda a: a)
    x1, ht, gsel, rank, counts = _out_proj(xf, fl(pool_out), fl(o_gdn), gt_mix, sc_ffn, sh_ffn, g_post_mix,
                                           g_pre_ffn, w_out_b, router_w, router_b, counts0, OUT_TILE, per_token)

    def last_rows(buf, off, width):
        keep = buf.shape[1]
        if t >= keep:
            return proj[:, t - keep:, off:off + width]
        return jnp.concatenate([buf[:, t:], proj[:, :, off:off + width]], axis=1)

    pool_new = last_rows(pool_buf, OFF_U, POOL_WIDTH)
    conv_new = last_rows(conv_buf, OFF_QKV, CONV_CH)
    return x1, ht, gsel, rank, counts, gt_ffn, pool_new, conv_new, s_new


def kernel(x_prompt, x_sample, c_prompt, c_sample, state_pool, state_conv, state_ssm, w_ada, b_ada, g_pre_mix, g_post_mix, g_pre_ffn, g_post_ffn, w_in, w_pool, pool_scale, conv_w, a_log, dt_bias, gdn_norm_w, w_out, router_w, router_b, w_gu, b_gu, w_down, b_down):
    depth = w_ada.shape[0]
    bp, tp, _ = x_prompt.shape
    bs, ts, _ = x_sample.shape
    tm = TOKEN_TILE
    y_p, y_s = x_prompt, x_sample
    outs = [[] for _ in range(6)]
    for l in range(depth):
        mod = _ada(jnp.concatenate([c_prompt, c_sample], axis=0), w_ada[l], b_ada[l]).reshape(bp + bs, 6, D_MODEL)
        w_in_p = jnp.swapaxes(w_in[l], 0, 1)
        w_bd = jax.scipy.linalg.block_diag(*[w_pool[l, g] for g in range(len(POOL_WINDOWS))])
        pad_row = lambda v: jnp.pad(v, (0, LANES - v.shape[0])).reshape(1, LANES)
        lp = (g_pre_mix[l], g_post_mix[l], g_pre_ffn[l], w_in_p, w_bd, pool_scale[l], conv_w[l],
              pad_row(a_log[l]), pad_row(dt_bias[l]), gdn_norm_w[l].reshape(1, HEAD_DIM),
              w_out[l].astype(BF16), router_w[l], router_b[l])
        zero_pool = jnp.zeros((bp, POOL_BUF, POOL_WIDTH), F32)
        zero_conv = jnp.zeros((bp, CONV_WIDTH - 1, CONV_CH), F32)
        zero_ssm = (jnp.zeros((1, bp, GDN_HEADS, HEAD_DIM, HEAD_DIM), F32), 0)
        no_counts = jnp.zeros((N_EXPERTS, 1), F32)
        x1p, htp, gp, rp, cnt_p, gtp, pp, cp, sp = _mix_path(
            y_p, mod[:bp], zero_pool, zero_conv, zero_ssm, 0, no_counts, lp, tm, 1, False)
        x1s, hts, gs, rs, counts, gts, ps, cs, ss = _mix_path(
            y_s, mod[bp:], state_pool[l], state_conv[l], (state_ssm, l), PAST_LEN, cnt_p, lp, tm, 8, True)
        gsel = jnp.concatenate([gp, gs], axis=0)
        rank = jnp.concatenate([rp, rs], axis=0)
        ys, dest_flat, gates = _moe(htp, hts, gsel, rank, counts, w_gu[l], b_gu[l], w_down[l], b_down[l],
                                    tm, MOE_BLOCK)
        y_p = _final(ys, dest_flat, gates, x1p, gtp, g_post_ffn[l], 0, tm, False)
        y_s = _final(ys, dest_flat, gates, x1s, gts, g_post_ffn[l], bp * tp, tm, True).reshape(bs, ts, D_MODEL)
        for lst, val in zip(outs, (pp, cp, sp, ps, cs, ss)):
            lst.append(val)
    return (y_p, y_s) + tuple(o[0][None] if depth == 1 else jnp.stack(o) for o in outs)
```

```python
import functools
import math

import jax
import jax.numpy as jnp
from jax import lax
from jax.experimental import pallas as pl
from jax.experimental.pallas import tpu as pltpu

F32 = jnp.float32
BF16 = jnp.bfloat16

D_MODEL = 1024
PAST_LEN = 16384
POOL_WINDOWS = (2, 4, 8, 16)
POOL_WIDTH = D_MODEL // 4
POOL_GROUP = POOL_WIDTH // len(POOL_WINDOWS)
POOL_BUF = max(POOL_WINDOWS) - 1
GDN_WIDTH = D_MODEL - POOL_WIDTH
HEAD_DIM = 128
GDN_HEADS = GDN_WIDTH // HEAD_DIM
CONV_WIDTH = 4
CONV_CH = 3 * GDN_WIDTH
GDN_CHUNK = 64
N_EXPERTS = 32
TOP_K = 4
D_FF = D_MODEL
SWIGLU_LIMIT = 7.0
SWIGLU_ALPHA = 1.702
EPS = 1e-6

SUBLANES = 8
LANES = 128
ROW_TILES = D_MODEL // LANES
OFF_QKV = 0
OFF_Z = OFF_QKV + CONV_CH
OFF_U = OFF_Z + GDN_WIDTH
OFF_AB = OFF_U + POOL_WIDTH
PROJ_WIDTH = OFF_AB + LANES
assert OFF_Z % GDN_WIDTH == 0 and OFF_U % POOL_WIDTH == 0 and OFF_AB % LANES == 0
TOKEN_TILE = 256
PROJ_TILE = 512
SCATTER_TILE = 512
OUT_TILE = 512
DEST_TILE = 1536
POOL_TILE = 1024
POOL_SEQS = 4
GDN_ITEM_GROUPS = 1
MOE_BLOCK = 1024
EXPERT_COL_CHUNKS = 2
DMA_QUEUES = 2
VMEM_LIMIT = 56 * 1024 * 1024


def _cparams(sem):
    return pltpu.CompilerParams(dimension_semantics=sem, vmem_limit_bytes=VMEM_LIMIT)


def _dot(a, b):
    return jnp.dot(a.astype(BF16), b.astype(BF16), preferred_element_type=F32)


def _dot_nt(a, b):
    return lax.dot_general(a.astype(BF16), b.astype(BF16), (((1,), (1,)), ((), ())),
                           preferred_element_type=F32)


def _silu(x):
    return x * jax.nn.sigmoid(x)


def _rms(x, g):
    return x * lax.rsqrt(jnp.mean(x * x, axis=-1, keepdims=True) + EPS) * g


def _ada_kernel(c_ref, w_ref, b_ref, o_ref):
    o_ref[...] = _dot(_silu(c_ref[...]), w_ref[...]) + b_ref[...]


def _ada(c, w_ada, b_ada):
    n = c.shape[0]
    cols = w_ada.shape[1]
    blk = D_MODEL
    return pl.pallas_call(
        _ada_kernel,
        grid=(cols // blk,),
        in_specs=[pl.BlockSpec((n, D_MODEL), lambda j: (0, 0)),
                  pl.BlockSpec((D_MODEL, blk), lambda j: (0, j)),
                  pl.BlockSpec((1, blk), lambda j: (0, j))],
        out_specs=pl.BlockSpec((n, blk), lambda j: (0, j)),
        out_shape=jax.ShapeDtypeStruct((n, cols), F32),
        compiler_params=_cparams(("arbitrary",)),
        name="ada",
    )(c, w_ada, b_ada.reshape(1, cols))


def _in_kernel(x_ref, sc_ref, sh_ref, g_ref, wt_ref, o_ref, wb):
    @pl.when((pl.program_id(0) == 0) & (pl.program_id(1) == 0))
    def _():
        o1, o2, o3 = POOL_WIDTH, POOL_WIDTH + CONV_CH, POOL_WIDTH + CONV_CH + GDN_WIDTH
        wb[OFF_QKV:OFF_QKV + CONV_CH] = wt_ref[o1:o2].astype(BF16)
        wb[OFF_Z:OFF_Z + GDN_WIDTH] = wt_ref[o2:o3].astype(BF16)
        wb[OFF_U:OFF_U + POOL_WIDTH] = wt_ref[:o1].astype(BF16)
        tail = wt_ref.shape[0] - o3
        wb[OFF_AB:] = jnp.concatenate(
            [wt_ref[o3:], jnp.zeros((LANES - tail, D_MODEL), F32)], axis=0).astype(BF16)

    h = _rms(x_ref[0], g_ref[...]) * (1.0 + sc_ref[0]) + sh_ref[0]
    o_ref[0] = _dot_nt(h, wb[...])


def _mod_spec(per_token, tm):
    if per_token:
        return pl.BlockSpec((1, tm, D_MODEL), lambda b, t: (0, t, 0))
    return pl.BlockSpec((1, 1, D_MODEL), lambda b, t: (b, 0, 0))


def _in_proj(x, sc, sh, g, w_in, tm, per_token):
    bsz, t, _ = x.shape
    return pl.pallas_call(
        _in_kernel,
        grid=(bsz, t // tm),
        in_specs=[pl.BlockSpec((1, tm, D_MODEL), lambda b, i: (b, i, 0)),
                  _mod_spec(per_token, tm), _mod_spec(per_token, tm),
                  pl.BlockSpec((1, D_MODEL), lambda b, i: (0, 0)),
                  pl.BlockSpec(w_in.shape, lambda b, i: (0, 0), pipeline_mode=pl.Buffered(1))],
        out_specs=pl.BlockSpec((1, tm, PROJ_WIDTH), lambda b, i: (b, i, 0)),
        out_shape=jax.ShapeDtypeStruct((bsz, t, PROJ_WIDTH), F32),
        scratch_shapes=[pltpu.VMEM((PROJ_WIDTH, D_MODEL), BF16)],
        compiler_params=_cparams(("arbitrary", "arbitrary")),
        name="in_proj",
    )(x, sc, sh, g.reshape(1, D_MODEL), w_in)


def _cumsum_rows(x):
    n = x.shape[0]
    row = lax.broadcasted_iota(jnp.int32, x.shape, 0)
    s = 1
    while s < n:
        x = x + jnp.where(row >= s, pltpu.roll(x, s, axis=0), 0.0)
        s *= 2
    return x


def _gdn_kernel(qkv_ref, z_ref, ab_ref, cs_ref, cw_ref, s0_ref, alog_ref, dtb_ref, nw_ref, o_ref, sn_ref,
                s_scr, prev_scr, *, nb, tm, tmp, chunk, group_items):
    ti = pl.program_id(1)
    nt = pl.num_programs(1)
    n_chunks = tmp // chunk
    n_sq = int(math.log2(chunk)) - 1

    @pl.when(ti == 0)
    def _():
        s_scr[...] = s0_ref[...]
        for b in range(nb):
            prev_scr[b] = jnp.concatenate(
                [jnp.zeros((SUBLANES - (CONV_WIDTH - 1), CONV_CH), F32), cs_ref[b]], axis=0)

    rowi = lax.broadcasted_iota(jnp.int32, (tmp, LANES), 0)
    ri = lax.broadcasted_iota(jnp.int32, (chunk, chunk), 0)
    ci = lax.broadcasted_iota(jnp.int32, (chunk, chunk), 1)
    tril = ri >= ci
    strict = ri > ci
    eye = (ri == ci).astype(F32)
    neg_a = -jnp.exp(alog_ref[...])
    dtb = dtb_ref[...]
    nw = nw_ref[...]

    def padrows(x):
        if tm == tmp:
            return x
        return jnp.concatenate([x, jnp.zeros((tmp - tm, x.shape[1]), F32)], axis=0)

    cw = cw_ref[...]
    valid = rowi < tm
    items = []
    for b in range(nb):
        x = padrows(qkv_ref[b])
        e = jnp.concatenate([prev_scr[b], x], axis=0)
        acc = x * cw[CONV_WIDTH - 1:CONV_WIDTH]
        for d in range(1, CONV_WIDTH):
            acc = acc + pltpu.roll(e, d, axis=0)[SUBLANES:] * cw[CONV_WIDTH - 1 - d:CONV_WIDTH - d]
        if tm >= SUBLANES:
            prev_scr[b] = x[tm - SUBLANES:tm]
        y = _silu(acc)
        ab = padrows(ab_ref[b])
        g_all = jnp.where(valid, neg_a * jax.nn.softplus(ab + dtb), 0.0)
        beta_all = jnp.where(valid, jax.nn.sigmoid(ab), 0.0)
        for h in range(GDN_HEADS):
            qc = y[:, h * HEAD_DIM:(h + 1) * HEAD_DIM]
            kc = y[:, GDN_WIDTH + h * HEAD_DIM:GDN_WIDTH + (h + 1) * HEAD_DIM]
            v = y[:, 2 * GDN_WIDTH + h * HEAD_DIM:2 * GDN_WIDTH + (h + 1) * HEAD_DIM]
            q = qc * lax.rsqrt(jnp.sum(qc * qc, axis=-1, keepdims=True) + EPS) * (HEAD_DIM ** -0.5)
            k = kc * lax.rsqrt(jnp.sum(kc * kc, axis=-1, keepdims=True) + EPS)
            gb = jnp.broadcast_to(g_all[:, h:h + 1], (tmp, LANES))
            bb = jnp.broadcast_to(beta_all[:, GDN_HEADS + h:GDN_HEADS + h + 1], (tmp, LANES))
            for c in range(n_chunks):
                sl = slice(c * chunk, (c + 1) * chunk)
                items.append(dict(b=b, h=h, c=c, q=q[sl], k=k[sl], v=v[sl], bb=bb[sl], gb=gb[sl]))

    def within_chunk(group):
        for it in group:
            gcb = _cumsum_rows(it["gb"])
            gct = jnp.transpose(gcb)[:chunk]
            diff = gcb[:, :chunk] - gct
            it["decay"] = jnp.where(tril, jnp.exp(jnp.where(tril, diff, 0.0)), 0.0)
            it["kb"] = it["k"] * it["bb"]
            it["eg"] = jnp.exp(gcb)
            gl = gcb[chunk - 1:chunk]
            it["egl"] = jnp.exp(gl)
            it["kdt"] = jnp.transpose(it["k"] * jnp.exp(gl - gcb))
        for it in group:
            kq = _dot_nt(jnp.concatenate([it["kb"], it["q"]], axis=0), it["k"])
            it["a"] = jnp.where(strict, kq[:chunk] * it["decay"], 0.0)
            it["qk"] = jnp.where(tril, kq[chunk:] * it["decay"], 0.0)
        for it in group:
            it["inv"] = eye - it["a"]
            it["p"] = _dot(it["a"], it["a"])
        for lvl in range(n_sq):
            for it in group:
                if lvl < n_sq - 1:
                    r = _dot(jnp.concatenate([it["inv"], it["p"]], axis=0), it["p"])
                    it["inv"] = it["inv"] + r[:chunk]
                    it["p"] = r[chunk:]
                else:
                    it["inv"] = it["inv"] + _dot(it["inv"], it["p"])
        for it in group:
            it["wu"] = _dot(it["inv"], jnp.concatenate([it["kb"] * it["eg"], it["v"] * it["bb"]], axis=1))
        for it in group:
            mn = _dot(it["kdt"], it["wu"])
            qo = _dot(it["qk"], it["wu"])
            it["mp"], it["nc"] = mn[:, :HEAD_DIM], mn[:, HEAD_DIM:]
            it["qp"] = it["q"] * it["eg"] - qo[:, :HEAD_DIM]
            it["oc"] = qo[:, HEAD_DIM:]

    for g0 in range(0, len(items), group_items):
        within_chunk(items[g0:g0 + group_items])

    state = {(b, h): s_scr[b, h] for b in range(nb) for h in range(GDN_HEADS)}
    for c in range(n_chunks):
        for it in items:
            if it["c"] != c:
                continue
            s = state[(it["b"], it["h"])]
            r = _dot(jnp.concatenate([it["qp"], it["mp"]], axis=0), s)
            it["o"] = r[:chunk] + it["oc"]
            state[(it["b"], it["h"])] = s * it["egl"] - r[chunk:] + it["nc"]
    for (b, h), s in state.items():
        s_scr[b, h] = s

    for b in range(nb):
        zf = padrows(z_ref[b])
        for h in range(GDN_HEADS):
            outs = [it["o"] for it in items if it["b"] == b and it["h"] == h]
            o = outs[0] if n_chunks == 1 else jnp.concatenate(outs, axis=0)
            o = (o * lax.rsqrt(jnp.mean(o * o, axis=-1, keepdims=True) + EPS) * nw
                 * _silu(zf[:, h * HEAD_DIM:(h + 1) * HEAD_DIM]))
            o_ref[b, :, h * HEAD_DIM:(h + 1) * HEAD_DIM] = o[:tm]

    @pl.when(ti == nt - 1)
    def _():
        sn_ref[...] = s_scr[...]


def _gdn(proj, state_conv, s0, conv_w, alog_row, dtb_row, norm_w, nb, tm):
    bsz, t, _ = proj.shape
    tmp = max(tm, SUBLANES)
    chunk = min(GDN_CHUNK, tmp)
    row = pl.BlockSpec((1, LANES), lambda b, i: (0, 0))
    s_all, layer = s0
    sspec = pl.BlockSpec((nb, GDN_HEADS, HEAD_DIM, HEAD_DIM), lambda b, i: (b, 0, 0, 0))
    s_in = pl.BlockSpec((None, nb, GDN_HEADS, HEAD_DIM, HEAD_DIM), lambda b, i: (layer, b, 0, 0, 0))
    n_items = nb * GDN_HEADS * (tmp // chunk)
    kern = functools.partial(_gdn_kernel, nb=nb, tm=tm, tmp=tmp, chunk=chunk,
                             group_items=n_items // GDN_ITEM_GROUPS)
    return pl.pallas_call(
        kern,
        grid=(bsz // nb, t // tm),
        in_specs=[pl.BlockSpec((nb, tm, CONV_CH), lambda b, i: (b, i, OFF_QKV // CONV_CH)),
                  pl.BlockSpec((nb, tm, GDN_WIDTH), lambda b, i: (b, i, OFF_Z // GDN_WIDTH)),
                  pl.BlockSpec((nb, tm, LANES), lambda b, i: (b, i, OFF_AB // LANES)),
                  pl.BlockSpec((nb, CONV_WIDTH - 1, CONV_CH), lambda b, i: (b, 0, 0)),
                  pl.BlockSpec((CONV_WIDTH, CONV_CH), lambda b, i: (0, 0)),
                  s_in, row, row, row],
        out_specs=[pl.BlockSpec((nb, tm, GDN_WIDTH), lambda b, i: (b, i, 0)), sspec],
        out_shape=[jax.ShapeDtypeStruct((bsz, t, GDN_WIDTH), F32),
                   jax.ShapeDtypeStruct(s_all.shape[1:], F32)],
        scratch_shapes=[pltpu.VMEM((nb, GDN_HEADS, HEAD_DIM, HEAD_DIM), F32),
                        pltpu.VMEM((nb, SUBLANES, CONV_CH), F32)],
        compiler_params=_cparams(("arbitrary", "arbitrary")),
        name="gdn",
    )(proj, proj, proj, state_conv, conv_w, s_all, alog_row, dtb_row, norm_w)


def _pool_kernel(u_ref, st_ref, w_ref, sc_ref, o_ref, prev_scr, *, nb, tm, tmp, start_pos):
    ti = pl.program_id(1)
    halo = 2 * SUBLANES

    @pl.when(ti == 0)
    def _():
        for b in range(nb):
            prev_scr[b] = jnp.concatenate([jnp.zeros((halo - POOL_BUF, POOL_WIDTH), F32), st_ref[b]], axis=0)

    lane = lax.broadcasted_iota(jnp.int32, (tmp, POOL_WIDTH), 1)
    rowi = lax.broadcasted_iota(jnp.int32, (tmp, POOL_WIDTH), 0)
    pos1 = start_pos + ti * tm + rowi + 1
    grp = lane // POOL_GROUP
    win = jnp.where(grp == 0, POOL_WINDOWS[0],
                    jnp.where(grp == 1, POOL_WINDOWS[1], jnp.where(grp == 2, POOL_WINDOWS[2], POOL_WINDOWS[3])))
    cnt = jnp.minimum(pos1, win).astype(F32)
    for b in range(nb):
        x = u_ref[b]
        if tm != tmp:
            x = jnp.concatenate([x, jnp.zeros((tmp - tm, POOL_WIDTH), F32)], axis=0)
        e = jnp.concatenate([prev_scr[b], x], axis=0)
        sums = []
        s = e
        for w in POOL_WINDOWS:
            s = s + pltpu.roll(s, w // 2, axis=0)
            sums.append(s[halo:])
        sel = jnp.where(grp == 0, sums[0], jnp.where(grp == 1, sums[1], jnp.where(grp == 2, sums[2], sums[3])))
        pooled = sel / cnt - x
        out = _dot(pooled, w_ref[...]) * sc_ref[...]
        o_ref[b] = out[:tm]
        if tm >= halo:
            prev_scr[b] = x[tm - halo:tm]


def _pool(proj, state_pool, w_bd, pool_scale, nb, tm, start_pos):
    bsz, t, _ = proj.shape
    tmp = max(tm, SUBLANES)
    kern = functools.partial(_pool_kernel, nb=nb, tm=tm, tmp=tmp, start_pos=start_pos)
    return pl.pallas_call(
        kern,
        grid=(bsz // nb, t // tm),
        in_specs=[pl.BlockSpec((nb, tm, POOL_WIDTH), lambda b, i: (b, i, OFF_U // POOL_WIDTH)),
                  pl.BlockSpec((nb, POOL_BUF, POOL_WIDTH), lambda b, i: (b, 0, 0)),
                  pl.BlockSpec((POOL_WIDTH, POOL_WIDTH), lambda b, i: (0, 0)),
                  pl.BlockSpec((1, POOL_WIDTH), lambda b, i: (0, 0))],
        out_specs=pl.BlockSpec((nb, tm, POOL_WIDTH), lambda b, i: (b, i, 0)),
        out_shape=jax.ShapeDtypeStruct((bsz, t, POOL_WIDTH), F32),
        scratch_shapes=[pltpu.VMEM((nb, 2 * SUBLANES, POOL_WIDTH), F32)],
        compiler_params=_cparams(("arbitrary", "arbitrary")),
        name="pool",
    )(proj, state_pool, w_bd, pool_scale.reshape(1, POOL_WIDTH))


def _out_kernel(x_ref, po_ref, og_ref, gt_ref, sc_ref, sh_ref, gpm_ref, gpf_ref, wo_ref, rw_ref, rb_ref, c0_ref,
                x1_ref, ht_ref, g_ref, r_ref, cnt_ref, carry, *, tm):
    @pl.when((pl.program_id(0) == 0) & (pl.program_id(1) == 0))
    def _():
        carry[...] = c0_ref[...]

    mix = _dot(po_ref[0], wo_ref[:POOL_WIDTH]) + _dot(og_ref[0], wo_ref[POOL_WIDTH:])
    x1 = x_ref[0] + gt_ref[0] * _rms(mix, gpm_ref[...])
    x1_ref[0] = x1
    h = _rms(x1, gpf_ref[...]) * (1.0 + sc_ref[0]) + sh_ref[0]
    for s in range(ROW_TILES):
        ht_ref[pl.ds(s, tm, stride=ROW_TILES), :] = h[:, s * LANES:(s + 1) * LANES]
    logits = _dot_nt(rw_ref[...], h) + rb_ref[...]
    eid = lax.broadcasted_iota(jnp.int32, logits.shape, 0).astype(F32)
    work = logits
    sel = jnp.zeros(logits.shape, jnp.bool_)
    top = None
    den = jnp.zeros((1, tm), F32)
    for j in range(TOP_K):
        m = jnp.max(work, axis=0, keepdims=True)
        idx = jnp.min(jnp.where(work == m, eid, float(N_EXPERTS)), axis=0, keepdims=True)
        hit = eid == idx
        sel = sel | hit
        work = jnp.where(hit, -jnp.inf, work)
        if j == 0:
            top = m
        den = den + jnp.exp(m - top)
    gates = jnp.exp(logits - top) / den
    g_ref[...] = jnp.transpose(jnp.where(sel, gates, -1.0))
    mask = sel.astype(F32)
    ri = lax.broadcasted_iota(jnp.int32, (tm, tm), 0)
    ci = lax.broadcasted_iota(jnp.int32, (tm, tm), 1)
    c = carry[...]
    r_ref[...] = jnp.transpose(_dot(mask, (ri < ci).astype(F32)) + c)
    c = c + jnp.sum(mask, axis=1, keepdims=True)
    carry[...] = c
    cnt_ref[...] = c


def _out_proj(x, pool_out, o_gdn, gt, sc, sh, g_post_mix, g_pre_ffn, w_out_b, router_w, router_b, counts0,
              tm, per_token):
    bsz, t, _ = x.shape
    n = bsz * t
    tiles = t // tm

    def tok(width):
        return pl.BlockSpec((1, tm, width), lambda b, i: (b, i, 0))

    def const(shape):
        return pl.BlockSpec(shape, lambda b, i: tuple(0 for _ in shape))

    kern = functools.partial(_out_kernel, tm=tm)
    return pl.pallas_call(
        kern,
        grid=(bsz, tiles),
        in_specs=[tok(D_MODEL), tok(POOL_WIDTH), tok(GDN_WIDTH),
                  _mod_spec(per_token, tm), _mod_spec(per_token, tm), _mod_spec(per_token, tm),
                  const((1, D_MODEL)), const((1, D_MODEL)), const((D_MODEL, D_MODEL)),
                  const((N_EXPERTS, D_MODEL)), const((N_EXPERTS, 1)), const((N_EXPERTS, 1))],
        out_specs=[tok(D_MODEL),
                   pl.BlockSpec((tm * ROW_TILES, LANES), lambda b, i: (b * tiles + i, 0)),
                   pl.BlockSpec((tm, N_EXPERTS), lambda b, i: (b * tiles + i, 0)),
                   pl.BlockSpec((tm, N_EXPERTS), lambda b, i: (b * tiles + i, 0)),
                   const((N_EXPERTS, 1))],
        out_shape=[jax.ShapeDtypeStruct((bsz, t, D_MODEL), F32),
                   jax.ShapeDtypeStruct((n * ROW_TILES, LANES), F32),
                   jax.ShapeDtypeStruct((n, N_EXPERTS), F32),
                   jax.ShapeDtypeStruct((n, N_EXPERTS), F32),
                   jax.ShapeDtypeStruct((N_EXPERTS, 1), F32)],
        scratch_shapes=[pltpu.VMEM((N_EXPERTS, 1), F32)],
        compiler_params=_cparams(("arbitrary", "arbitrary")),
        name="out_proj",
    )(x, pool_out, o_gdn, gt, sc, sh, g_post_mix.reshape(1, D_MODEL), g_pre_ffn.reshape(1, D_MODEL),
      w_out_b, jnp.swapaxes(router_w, 0, 1), router_b.reshape(N_EXPERTS, 1), counts0)


def _dest_kernel(g_ref, r_ref, start_ref, d_ref, gt_ref, *, tm):
    g = g_ref[...]
    mask = g >= 0.0
    maskf = mask.astype(F32)
    dest = start_ref[...] + r_ref[...]
    ri = lax.broadcasted_iota(jnp.int32, (N_EXPERTS, N_EXPERTS), 0)
    ci = lax.broadcasted_iota(jnp.int32, (N_EXPERTS, N_EXPERTS), 1)
    slot = _dot(maskf, (ri < ci).astype(F32))
    lane4 = lax.broadcasted_iota(jnp.int32, (tm, TOP_K), 1)
    d_out = jnp.zeros((tm, TOP_K), F32)
    g_out = jnp.zeros((tm, TOP_K), F32)
    for j in range(TOP_K):
        pick = mask & (slot == float(j))
        dj = jnp.sum(jnp.where(pick, dest, 0.0), axis=-1, keepdims=True)
        gj = jnp.sum(jnp.where(pick, g, 0.0), axis=-1, keepdims=True)
        d_out = jnp.where(lane4 == j, dj, d_out)
        g_out = jnp.where(lane4 == j, gj, g_out)
    d_ref[...] = d_out.astype(jnp.int32)
    gt_ref[...] = g_out


def _dest(gsel, rank, start_row, tm):
    n = gsel.shape[0]
    kern = functools.partial(_dest_kernel, tm=tm)
    return pl.pallas_call(
        kern,
        grid=(n // tm,),
        in_specs=[pl.BlockSpec((tm, N_EXPERTS), lambda i: (i, 0)),
                  pl.BlockSpec((tm, N_EXPERTS), lambda i: (i, 0)),
                  pl.BlockSpec((1, N_EXPERTS), lambda i: (0, 0))],
        out_specs=[pl.BlockSpec((tm, TOP_K), lambda i: (i, 0)),
                   pl.BlockSpec((tm, TOP_K), lambda i: (i, 0))],
        out_shape=[jax.ShapeDtypeStruct((n, TOP_K), jnp.int32),
                   jax.ShapeDtypeStruct((n, TOP_K), F32)],
        compiler_params=_cparams(("arbitrary",)),
        name="dest",
    )(gsel, rank, start_row)


def _row_copy(src, dst, s_row, d_row, sem):
    return pltpu.make_async_copy(src.at[pl.ds(pl.multiple_of(s_row * ROW_TILES, ROW_TILES), ROW_TILES), :],
                                 dst.at[pl.ds(pl.multiple_of(d_row * ROW_TILES, ROW_TILES), ROW_TILES), :], sem)


def _scatter_kernel(cnt_ref, start_ref, end_ref, nu_ref, dest_ref, ha_ref, hb_ref, xs_ref, sem, *,
                    tm, bm, n_blocks, tiles_a):
    i = pl.program_id(0)

    def scatter_from(ht_ref):
        def body(t, carry):
            src = ht_ref.at[pl.ds(pl.multiple_of(t * ROW_TILES, ROW_TILES), ROW_TILES), :]
            for k in range(TOP_K):
                d = dest_ref[t * TOP_K + k]
                pltpu.make_async_copy(
                    src, xs_ref.at[pl.ds(pl.multiple_of(d * ROW_TILES, ROW_TILES), ROW_TILES), :],
                    sem).start(priority=k % DMA_QUEUES)
            return carry

        lax.fori_loop(0, tm, body, 0)

    @pl.when(i < tiles_a)
    def _():
        scatter_from(ha_ref)

    @pl.when(i >= tiles_a)
    def _():
        scatter_from(hb_ref)

    ht_ref = hb_ref

    def tile_copy(u):
        return pltpu.make_async_copy(
            ht_ref, xs_ref.at[pl.ds(pl.multiple_of(u * (tm * ROW_TILES), tm * ROW_TILES), tm * ROW_TILES), :], sem)

    for k in range(TOP_K):
        tile_copy(0).wait()

    @pl.when(i == pl.num_programs(0) - 1)
    def _():
        per = bm // tm

        def tail_start(u, carry):
            tile_copy(u).start()
            return carry

        def tail_wait(u, carry):
            tile_copy(u).wait()
            return carry

        lax.fori_loop(nu_ref[0] * per, n_blocks * per, tail_start, 0)
        lax.fori_loop(nu_ref[0] * per, n_blocks * per, tail_wait, 0)

        chunk = bm // 4
        assert chunk <= tm

        def chunk_copy(q):
            return pltpu.make_async_copy(
                ht_ref.at[pl.ds(0, chunk * ROW_TILES), :],
                xs_ref.at[pl.ds(pl.multiple_of(q * (chunk * ROW_TILES), chunk * ROW_TILES), chunk * ROW_TILES), :],
                sem)

        def per_expert(e, carry):
            lo = start_ref[e] + cnt_ref[e]
            mid = (lo + chunk - 1) // chunk * chunk
            hi = end_ref[e]

            def fill(r, c2):
                _row_copy(ht_ref, xs_ref, 0, r, sem).start()
                return c2

            def fill_chunk(q, c2):
                chunk_copy(q).start()
                return c2

            lax.fori_loop(lo, mid, fill, 0)
            lax.fori_loop(mid // chunk, hi // chunk, fill_chunk, 0)

            def drain(r, c2):
                _row_copy(ht_ref, xs_ref, 0, r, sem).wait()
                return c2

            def drain_chunk(q, c2):
                chunk_copy(q).wait()
                return c2

            lax.fori_loop(lo, mid, drain, 0)
            lax.fori_loop(mid // chunk, hi // chunk, drain_chunk, 0)
            return carry

        lax.fori_loop(0, N_EXPERTS, per_expert, 0)


def _scatter_rows(ht_a, ht_b, dest_flat, cnt, start, end, n_used, n_blocks, tm, bm):
    rows = n_blocks * bm
    tiles_a, tiles_b = ht_a.shape[0] // (tm * ROW_TILES), ht_b.shape[0] // (tm * ROW_TILES)
    assert bm % tm == 0 and ht_a.shape[0] % (tm * ROW_TILES) == 0 and ht_b.shape[0] % (tm * ROW_TILES) == 0
    assert tiles_b >= 1
    kern = functools.partial(_scatter_kernel, tm=tm, bm=bm, n_blocks=n_blocks, tiles_a=tiles_a)
    return pl.pallas_call(
        kern,
        grid_spec=pltpu.PrefetchScalarGridSpec(
            num_scalar_prefetch=4,
            grid=(tiles_a + tiles_b,),
            in_specs=[pl.BlockSpec((tm * TOP_K,), lambda i, *_: (i,), memory_space=pltpu.SMEM),
                      pl.BlockSpec((tm * ROW_TILES, LANES), lambda i, *_: (jnp.minimum(i, tiles_a - 1), 0)),
                      pl.BlockSpec((tm * ROW_TILES, LANES), lambda i, *_: (jnp.maximum(i - tiles_a, 0), 0))],
            out_specs=pl.BlockSpec(memory_space=pl.ANY),
            scratch_shapes=[pltpu.SemaphoreType.DMA(())]),
        out_shape=jax.ShapeDtypeStruct((rows * ROW_TILES, LANES), F32),
        compiler_params=_cparams(("arbitrary",)),
        name="scatter_rows",
    )(cnt, start, end, n_used, dest_flat, ht_a, ht_b)


def _expert_kernel(be_ref, nx_ref, nv_ref, nu_ref, xs_ref, wg_ref, bg_ref, wd_ref, bd_ref, ys_ref,
                   wg_f, wd_f, wg_s, wd_s, wsem, *, bm):
    j = pl.program_id(0)
    prev = be_ref[jnp.maximum(j - 1, 0)]
    fresh = (j == 0) | (be_ref[j] != prev)
    active = j < nu_ref[0]

    def weight_copies(e):
        return (pltpu.make_async_copy(wg_ref.at[e], wg_f, wsem.at[0]),
                pltpu.make_async_copy(wd_ref.at[e], wd_f, wsem.at[1]))

    @pl.when(j == 0)
    def _():
        for c in weight_copies(be_ref[0]):
            c.start()

    @pl.when(active & fresh)
    def _():
        for c in weight_copies(be_ref[j]):
            c.wait()
        wg_s[...] = wg_f[...].astype(BF16)
        wd_s[...] = wd_f[...].astype(BF16)

        @pl.when(nx_ref[j] != be_ref[j])
        def _():
            for c in weight_copies(nx_ref[j]):
                c.start()

    def run_rows(m):
        x = jnp.concatenate([xs_ref[pl.ds(s, m, stride=ROW_TILES), :] for s in range(ROW_TILES)], axis=-1)
        xb = x.astype(BF16)
        acts = []
        for c in range(EXPERT_COL_CHUNKS):
            w = D_FF // EXPERT_COL_CHUNKS
            lo = c * w
            gate = jnp.dot(xb, wg_s[:, lo:lo + w], preferred_element_type=F32) + bg_ref[0, :, lo:lo + w]
            up = (jnp.dot(xb, wg_s[:, D_FF + lo:D_FF + lo + w], preferred_element_type=F32)
                  + bg_ref[0, :, D_FF + lo:D_FF + lo + w])
            gate = jnp.minimum(gate, SWIGLU_LIMIT)
            up = jnp.clip(up, -SWIGLU_LIMIT, SWIGLU_LIMIT)
            acts.append(((up + 1.0) * (gate * jax.nn.sigmoid(SWIGLU_ALPHA * gate))).astype(BF16))
        act = jnp.concatenate(acts, axis=-1)
        for c in range(EXPERT_COL_CHUNKS):
            w = D_MODEL // EXPERT_COL_CHUNKS
            lo = c * w
            y = jnp.dot(act, wd_s[:, lo:lo + w], preferred_element_type=F32) + bd_ref[0, :, lo:lo + w]
            for s in range(w // LANES):
                ys_ref[pl.ds(lo // LANES + s, m, stride=ROW_TILES), :] = y[:, s * LANES:(s + 1) * LANES]
        if m < bm:
            ys_ref[pl.ds(m * ROW_TILES, (bm - m) * ROW_TILES), :] = jnp.zeros(((bm - m) * ROW_TILES, LANES), F32)

    quarter = bm // 4
    quarters = (nv_ref[j] + quarter - 1) // quarter
    for nq in range(1, 5):
        @pl.when(active & ((quarters == nq) if nq > 1 else (quarters <= 1)))
        def _(nq=nq):
            run_rows(nq * quarter)

    @pl.when(jnp.logical_not(active))
    def _():
        ys_ref[...] = jnp.zeros_like(ys_ref)


def _experts(xs, block_expert, next_expert, block_valid, n_used, w_gu, b_gu, w_down, b_down, bm):
    rows = xs.shape[0] // ROW_TILES
    n_blocks = rows // bm

    def blk(j, be, nx, nv, nu):
        return (jnp.minimum(j, nu[0] - 1), 0)

    def bsel(j, be, nx, nv, nu):
        return (be[j], 0, 0)

    hbm = pl.BlockSpec(memory_space=pl.ANY)
    kern = functools.partial(_expert_kernel, bm=bm)
    return pl.pallas_call(
        kern,
        grid_spec=pltpu.PrefetchScalarGridSpec(
            num_scalar_prefetch=4,
            grid=(n_blocks,),
            in_specs=[pl.BlockSpec((bm * ROW_TILES, LANES), blk), hbm,
                      pl.BlockSpec((1, 1, 2 * D_FF), bsel), hbm,
                      pl.BlockSpec((1, 1, D_MODEL), bsel)],
            out_specs=pl.BlockSpec((bm * ROW_TILES, LANES), lambda j, be, nx, nv, nu: (j, 0)),
            scratch_shapes=[pltpu.VMEM((D_MODEL, 2 * D_FF), F32),
                            pltpu.VMEM((D_FF, D_MODEL), F32),
                            pltpu.VMEM((D_MODEL, 2 * D_FF), BF16),
                            pltpu.VMEM((D_FF, D_MODEL), BF16),
                            pltpu.SemaphoreType.DMA((2,))]),
        out_shape=jax.ShapeDtypeStruct((rows * ROW_TILES, LANES), F32),
        compiler_params=_cparams(("arbitrary",)),
        name="experts",
    )(block_expert, next_expert, block_valid, n_used, xs, w_gu, b_gu.reshape(N_EXPERTS, 1, 2 * D_FF), w_down,
      b_down.reshape(N_EXPERTS, 1, D_MODEL))


def _final_kernel(dcur_ref, dnext_ref, ys_ref, gate_ref, x1_ref, gt_ref, g_ref, o_ref, sl_ref, sem, *, tm, steps):
    g = pl.program_id(0) * pl.num_programs(1) + pl.program_id(1)
    cur = g % 2

    def request(d_ref, slot):
        def body(t, carry):
            for k in range(TOP_K):
                j = t * TOP_K + k
                pltpu.make_async_copy(
                    ys_ref.at[pl.ds(pl.multiple_of(d_ref[j] * ROW_TILES, ROW_TILES), ROW_TILES), :],
                    sl_ref.at[slot, pl.ds(pl.multiple_of(j * ROW_TILES, ROW_TILES), ROW_TILES), :],
                    sem.at[slot]).start(priority=k % DMA_QUEUES)
            return carry

        lax.fori_loop(0, tm, body, 0)

    @pl.when(g == 0)
    def _():
        request(dcur_ref, 0)

    @pl.when(g + 1 < steps)
    def _():
        request(dnext_ref, 1 - cur)

    pltpu.make_async_copy(ys_ref.at[pl.ds(0, tm * TOP_K * ROW_TILES), :], sl_ref.at[cur], sem.at[cur]).wait()

    gates = gate_ref[...]
    per_tok = TOP_K * ROW_TILES
    cols = []
    for s in range(ROW_TILES):
        acc = None
        for j in range(TOP_K):
            part = sl_ref[cur, pl.ds(j * ROW_TILES + s, tm, stride=per_tok), :] * gates[:, j:j + 1]
            acc = part if acc is None else acc + part
        cols.append(acc)
    f = jnp.concatenate(cols, axis=-1)
    o_ref[0] = x1_ref[0] + gt_ref[0] * _rms(f, g_ref[...])


def _final(ys, dest_flat, gates, x1, gt, g_post_ffn, tok_off, tm, per_token):
    bsz, t, _ = x1.shape
    tiles = t // tm
    off = tok_off // tm
    steps = bsz * tiles
    kern = functools.partial(_final_kernel, tm=tm, steps=steps)

    def dspec(ahead):
        return pl.BlockSpec((tm * TOP_K,), lambda b, i: (off + jnp.minimum(b * tiles + i + ahead, steps - 1),),
                            memory_space=pltpu.SMEM)

    return pl.pallas_call(
        kern,
        grid=(bsz, tiles),
        in_specs=[dspec(0), dspec(1),
                  pl.BlockSpec(memory_space=pl.ANY),
                  pl.BlockSpec((tm, TOP_K), lambda b, i: (off + b * tiles + i, 0)),
                  pl.BlockSpec((1, tm, D_MODEL), lambda b, i: (b, i, 0)),
                  _mod_spec(per_token, tm),
                  pl.BlockSpec((1, D_MODEL), lambda b, i: (0, 0))],
        out_specs=pl.BlockSpec((1, tm, D_MODEL), lambda b, i: (b, i, 0)),
        out_shape=jax.ShapeDtypeStruct((bsz, t, D_MODEL), F32),
        scratch_shapes=[pltpu.VMEM((2, tm * TOP_K * ROW_TILES, LANES), F32), pltpu.SemaphoreType.DMA((2,))],
        compiler_params=_cparams(("arbitrary", "arbitrary")),
        name="final",
    )(dest_flat, dest_flat, ys, gates, x1, gt, g_post_ffn.reshape(1, D_MODEL))


def _moe(ht_a, ht_b, gsel, rank, counts, w_gu, b_gu, w_down, b_down, tm, bm):
    n = gsel.shape[0]
    cnt = counts[:, 0].astype(jnp.int32)
    nblk = (cnt + bm - 1) // bm
    blk_end = jnp.cumsum(nblk)
    blk_start = blk_end - nblk
    n_blocks = -(-(n * TOP_K) // bm) + N_EXPERTS
    blocks = jnp.arange(n_blocks, dtype=jnp.int32)
    block_expert = jnp.minimum(jnp.sum((blk_end[None, :] <= blocks[:, None]).astype(jnp.int32), axis=1),
                               N_EXPERTS - 1).astype(jnp.int32)
    ids = jnp.arange(N_EXPERTS, dtype=jnp.int32)
    later = jnp.where((nblk[None, :] > 0) & (ids[None, :] > ids[:, None]), ids[None, :], N_EXPERTS)
    nearest = jnp.min(later, axis=1)
    next_used = jnp.where(nearest < N_EXPERTS, nearest, ids).astype(jnp.int32)
    n_used = blk_end[-1:].astype(jnp.int32)
    row_start = (blk_start * bm).astype(jnp.int32)
    row_end = (blk_end * bm).astype(jnp.int32)
    dest, gates = _dest(gsel, rank, row_start.astype(F32).reshape(1, N_EXPERTS), DEST_TILE)
    dest_flat = dest.reshape(n * TOP_K)
    xs = _scatter_rows(ht_a, ht_b, dest_flat, cnt, row_start, row_end, n_used, n_blocks, SCATTER_TILE, bm)
    owns = (blocks[:, None] >= blk_start[None, :]) & (blocks[:, None] < blk_end[None, :])

    def per_block(table):
        return jnp.sum(jnp.where(owns, table[None, :], 0), axis=1).astype(jnp.int32)

    block_valid = jnp.clip(per_block(cnt) - (blocks - per_block(blk_start)) * bm, 0, bm).astype(jnp.int32)
    ys = _experts(xs, block_expert, per_block(next_used), block_valid, n_used, w_gu, b_gu, w_down, b_down, bm)
    return ys, dest_flat, gates


def _mix_path(x, mod, pool_buf, conv_buf, s0, start_pos, counts0, lp, tm, nb, per_token):
    (g_pre_mix, g_post_mix, g_pre_ffn, w_in_p, w_bd, pool_scale, conv_w, alog_row, dtb_row, norm_w,
     w_out_b, router_w, router_b) = lp
    bsz, t, _ = x.shape
    sh_mix, sc_mix, gt_mix, sh_ffn, sc_ffn, gt_ffn = (mod[:, i] for i in range(6))

    def shape_mod(m):
        if per_token:
            return jnp.repeat(m, t, axis=0).reshape(1, bsz * t, D_MODEL)
        return m.reshape(bsz, 1, D_MODEL)

    sh_mix, sc_mix, gt_mix, sh_ffn, sc_ffn, gt_ffn = map(shape_mod, (sh_mix, sc_mix, gt_mix, sh_ffn, sc_ffn, gt_ffn))
    xf = x.reshape(1, bsz * t, D_MODEL) if per_token else x
    proj = _in_proj(xf, sc_mix, sh_mix, g_pre_mix, w_in_p, PROJ_TILE, per_token).reshape(bsz, t, PROJ_WIDTH)
    tseq = min(tm, t)
    o_gdn, s_new = _gdn(proj, conv_buf, s0, conv_w, alog_row, dtb_row, norm_w, nb, tseq)
    pool_out = _pool(proj, pool_buf, w_bd, pool_scale, nb * POOL_SEQS if per_token else nb, min(POOL_TILE, t),
                     start_pos)
    fl = (lambda a: a.reshape(1, bsz * t, a.shape[-1])) if per_token else (lambda a: a)
    x1, ht, gsel, rank, counts = _out_proj(xf, fl(pool_out), fl(o_gdn), gt_mix, sc_ffn, sh_ffn, g_post_mix,
                                           g_pre_ffn, w_out_b, router_w, router_b, counts0, OUT_TILE, per_token)

    def last_rows(buf, off, width):
        keep = buf.shape[1]
        if t >= keep:
            return proj[:, t - keep:, off:off + width]
        return jnp.concatenate([buf[:, t:], proj[:, :, off:off + width]], axis=1)

    pool_new = last_rows(pool_buf, OFF_U, POOL_WIDTH)
    conv_new = last_rows(conv_buf, OFF_QKV, CONV_CH)
    return x1, ht, gsel, rank, counts, gt_ffn, pool_new, conv_new, s_new


def kernel(x_prompt, x_sample, c_prompt, c_sample, state_pool, state_conv, state_ssm, w_ada, b_ada, g_pre_mix, g_post_mix, g_pre_ffn, g_post_ffn, w_in, w_pool, pool_scale, conv_w, a_log, dt_bias, gdn_norm_w, w_out, router_w, router_b, w_gu, b_gu, w_down, b_down):
    depth = w_ada.shape[0]
    bp, tp, _ = x_prompt.shape
    bs, ts, _ = x_sample.shape
    tm = TOKEN_TILE
    y_p, y_s = x_prompt, x_sample
    outs = [[] for _ in range(6)]
    for l in range(depth):
        mod = _ada(jnp.concatenate([c_prompt, c_sample], axis=0), w_ada[l], b_ada[l]).reshape(bp + bs, 6, D_MODEL)
        w_in_p = jnp.swapaxes(w_in[l], 0, 1)
        w_bd = jax.scipy.linalg.block_diag(*[w_pool[l, g] for g in range(len(POOL_WINDOWS))])
        pad_row = lambda v: jnp.pad(v, (0, LANES - v.shape[0])).reshape(1, LANES)
        lp = (g_pre_mix[l], g_post_mix[l], g_pre_ffn[l], w_in_p, w_bd, pool_scale[l], conv_w[l],
              pad_row(a_log[l]), pad_row(dt_bias[l]), gdn_norm_w[l].reshape(1, HEAD_DIM),
              w_out[l].astype(BF16), router_w[l], router_b[l])
        zero_pool = jnp.zeros((bp, POOL_BUF, POOL_WIDTH), F32)
        zero_conv = jnp.zeros((bp, CONV_WIDTH - 1, CONV_CH), F32)
        zero_ssm = (jnp.zeros((1, bp, GDN_HEADS, HEAD_DIM, HEAD_DIM), F32), 0)
        no_counts = jnp.zeros((N_EXPERTS, 1), F32)
        x1p, htp, gp, rp, cnt_p, gtp, pp, cp, sp = _mix_path(
            y_p, mod[:bp], zero_pool, zero_conv, zero_ssm, 0, no_counts, lp, tm, 1, False)
        x1s, hts, gs, rs, counts, gts, ps, cs, ss = _mix_path(
            y_s, mod[bp:], state_pool[l], state_conv[l], (state_ssm, l), PAST_LEN, cnt_p, lp, tm, 8, True)
        gsel = jnp.concatenate([gp, gs], axis=0)
        rank = jnp.concatenate([rp, rs], axis=0)
        ys, dest_flat, gates = _moe(htp, hts, gsel, rank, counts, w_gu[l], b_gu[l], w_down[l], b_down[l],
                                    tm, MOE_BLOCK)
        y_p = _final(ys, dest_flat, gates, x1p, gtp, g_post_ffn[l], 0, tm, False)
        y_s = _final(ys, dest_flat, gates, x1s, gts, g_post_ffn[l], bp * tp, tm, True).reshape(bs, ts, D_MODEL)
        for lst, val in zip(outs, (pp, cp, sp, ps, cs, ss)):
            lst.append(val)
    return (y_p, y_s) + tuple(o[0][None] if depth == 1 else jnp.stack(o) for o in outs)
```

```python
import functools
import math

import jax
import jax.numpy as jnp
from jax import lax
from jax.experimental import pallas as pl
from jax.experimental.pallas import tpu as pltpu

F32 = jnp.float32
BF16 = jnp.bfloat16

D_MODEL = 1024
PAST_LEN = 16384
POOL_WINDOWS = (2, 4, 8, 16)
POOL_WIDTH = D_MODEL // 4
POOL_GROUP = POOL_WIDTH // len(POOL_WINDOWS)
POOL_BUF = max(POOL_WINDOWS) - 1
GDN_WIDTH = D_MODEL - POOL_WIDTH
HEAD_DIM = 128
GDN_HEADS = GDN_WIDTH // HEAD_DIM
CONV_WIDTH = 4
CONV_CH = 3 * GDN_WIDTH
GDN_CHUNK = 64
N_EXPERTS = 32
TOP_K = 4
D_FF = D_MODEL
SWIGLU_LIMIT = 7.0
SWIGLU_ALPHA = 1.702
EPS = 1e-6

SUBLANES = 8
LANES = 128
ROW_TILES = D_MODEL // LANES
OFF_QKV = 0
OFF_Z = OFF_QKV + CONV_CH
OFF_U = OFF_Z + GDN_WIDTH
OFF_AB = OFF_U + POOL_WIDTH
PROJ_WIDTH = OFF_AB + LANES
assert OFF_Z % GDN_WIDTH == 0 and OFF_U % POOL_WIDTH == 0 and OFF_AB % LANES == 0
TOKEN_TILE = 256
PROJ_TILE = 512
SCATTER_TILE = 512
OUT_TILE = 512
DEST_TILE = 1536
POOL_TILE = 1024
POOL_SEQS = 4
GDN_ITEM_GROUPS = 1
MOE_BLOCK = 1024
EXPERT_COL_CHUNKS = 2
DMA_QUEUES = 2
VMEM_LIMIT = 56 * 1024 * 1024


def _cparams(sem):
    return pltpu.CompilerParams(dimension_semantics=sem, vmem_limit_bytes=VMEM_LIMIT)


def _dot(a, b):
    return jnp.dot(a.astype(BF16), b.astype(BF16), preferred_element_type=F32)


def _dot_nt(a, b):
    return lax.dot_general(a.astype(BF16), b.astype(BF16), (((1,), (1,)), ((), ())),
                           preferred_element_type=F32)


def _silu(x):
    return x * jax.nn.sigmoid(x)


def _rms(x, g):
    return x * lax.rsqrt(jnp.mean(x * x, axis=-1, keepdims=True) + EPS) * g


def _ada_kernel(c_ref, w_ref, b_ref, o_ref):
    o_ref[...] = _dot(_silu(c_ref[...]), w_ref[...]) + b_ref[...]


def _ada(c, w_ada, b_ada):
    n = c.shape[0]
    cols = w_ada.shape[1]
    blk = D_MODEL
    return pl.pallas_call(
        _ada_kernel,
        grid=(cols // blk,),
        in_specs=[pl.BlockSpec((n, D_MODEL), lambda j: (0, 0)),
                  pl.BlockSpec((D_MODEL, blk), lambda j: (0, j)),
                  pl.BlockSpec((1, blk), lambda j: (0, j))],
        out_specs=pl.BlockSpec((n, blk), lambda j: (0, j)),
        out_shape=jax.ShapeDtypeStruct((n, cols), F32),
        compiler_params=_cparams(("arbitrary",)),
        name="ada",
    )(c, w_ada, b_ada.reshape(1, cols))


def _in_kernel(x_ref, sc_ref, sh_ref, g_ref, wt_ref, o_ref, wb):
    @pl.when((pl.program_id(0) == 0) & (pl.program_id(1) == 0))
    def _():
        o1, o2, o3 = POOL_WIDTH, POOL_WIDTH + CONV_CH, POOL_WIDTH + CONV_CH + GDN_WIDTH
        wb[OFF_QKV:OFF_QKV + CONV_CH] = wt_ref[o1:o2].astype(BF16)
        wb[OFF_Z:OFF_Z + GDN_WIDTH] = wt_ref[o2:o3].astype(BF16)
        wb[OFF_U:OFF_U + POOL_WIDTH] = wt_ref[:o1].astype(BF16)
        tail = wt_ref.shape[0] - o3
        wb[OFF_AB:] = jnp.concatenate(
            [wt_ref[o3:], jnp.zeros((LANES - tail, D_MODEL), F32)], axis=0).astype(BF16)

    h = _rms(x_ref[0], g_ref[...]) * (1.0 + sc_ref[0]) + sh_ref[0]
    o_ref[0] = _dot_nt(h, wb[...])


def _mod_spec(per_token, tm):
    if per_token:
        return pl.BlockSpec((1, tm, D_MODEL), lambda b, t: (0, t, 0))
    return pl.BlockSpec((1, 1, D_MODEL), lambda b, t: (b, 0, 0))


def _in_proj(x, sc, sh, g, w_in, tm, per_token):
    bsz, t, _ = x.shape
    return pl.pallas_call(
        _in_kernel,
        grid=(bsz, t // tm),
        in_specs=[pl.BlockSpec((1, tm, D_MODEL), lambda b, i: (b, i, 0)),
                  _mod_spec(per_token, tm), _mod_spec(per_token, tm),
                  pl.BlockSpec((1, D_MODEL), lambda b, i: (0, 0)),
                  pl.BlockSpec(w_in.shape, lambda b, i: (0, 0), pipeline_mode=pl.Buffered(1))],
        out_specs=pl.BlockSpec((1, tm, PROJ_WIDTH), lambda b, i: (b, i, 0)),
        out_shape=jax.ShapeDtypeStruct((bsz, t, PROJ_WIDTH), F32),
        scratch_shapes=[pltpu.VMEM((PROJ_WIDTH, D_MODEL), BF16)],
        compiler_params=_cparams(("arbitrary", "arbitrary")),
        name="in_proj",
    )(x, sc, sh, g.reshape(1, D_MODEL), w_in)


def _cumsum_rows(x):
    n = x.shape[0]
    row = lax.broadcasted_iota(jnp.int32, x.shape, 0)
    s = 1
    while s < n:
        x = x + jnp.where(row >= s, pltpu.roll(x, s, axis=0), 0.0)
        s *= 2
    return x


def _gdn_kernel(qkv_ref, z_ref, ab_ref, cs_ref, cw_ref, s0_ref, alog_ref, dtb_ref, nw_ref, o_ref, sn_ref,
                s_scr, prev_scr, *, nb, tm, tmp, chunk, group_items):
    ti = pl.program_id(1)
    nt = pl.num_programs(1)
    n_chunks = tmp // chunk
    n_sq = int(math.log2(chunk)) - 1

    @pl.when(ti == 0)
    def _():
        s_scr[...] = s0_ref[...]
        for b in range(nb):
            prev_scr[b] = jnp.concatenate(
                [jnp.zeros((SUBLANES - (CONV_WIDTH - 1), CONV_CH), F32), cs_ref[b]], axis=0)

    rowi = lax.broadcasted_iota(jnp.int32, (tmp, LANES), 0)
    ri = lax.broadcasted_iota(jnp.int32, (chunk, chunk), 0)
    ci = lax.broadcasted_iota(jnp.int32, (chunk, chunk), 1)
    tril = ri >= ci
    strict = ri > ci
    eye = (ri == ci).astype(F32)
    neg_a = -jnp.exp(alog_ref[...])
    dtb = dtb_ref[...]
    nw = nw_ref[...]

    def padrows(x):
        if tm == tmp:
            return x
        return jnp.concatenate([x, jnp.zeros((tmp - tm, x.shape[1]), F32)], axis=0)

    cw = cw_ref[...]
    valid = rowi < tm
    items = []
    for b in range(nb):
        x = padrows(qkv_ref[b])
        e = jnp.concatenate([prev_scr[b], x], axis=0)
        acc = x * cw[CONV_WIDTH - 1:CONV_WIDTH]
        for d in range(1, CONV_WIDTH):
            acc = acc + pltpu.roll(e, d, axis=0)[SUBLANES:] * cw[CONV_WIDTH - 1 - d:CONV_WIDTH - d]
        if tm >= SUBLANES:
            prev_scr[b] = x[tm - SUBLANES:tm]
        y = _silu(acc)
        ab = padrows(ab_ref[b])
        g_all = jnp.where(valid, neg_a * jax.nn.softplus(ab + dtb), 0.0)
        beta_all = jnp.where(valid, jax.nn.sigmoid(ab), 0.0)
        for h in range(GDN_HEADS):
            qc = y[:, h * HEAD_DIM:(h + 1) * HEAD_DIM]
            kc = y[:, GDN_WIDTH + h * HEAD_DIM:GDN_WIDTH + (h + 1) * HEAD_DIM]
            v = y[:, 2 * GDN_WIDTH + h * HEAD_DIM:2 * GDN_WIDTH + (h + 1) * HEAD_DIM]
            q = qc * lax.rsqrt(jnp.sum(qc * qc, axis=-1, keepdims=True) + EPS) * (HEAD_DIM ** -0.5)
            k = kc * lax.rsqrt(jnp.sum(kc * kc, axis=-1, keepdims=True) + EPS)
            gb = jnp.broadcast_to(g_all[:, h:h + 1], (tmp, LANES))
            bb = jnp.broadcast_to(beta_all[:, GDN_HEADS + h:GDN_HEADS + h + 1], (tmp, LANES))
            for c in range(n_chunks):
                sl = slice(c * chunk, (c + 1) * chunk)
                items.append(dict(b=b, h=h, c=c, q=q[sl], k=k[sl], v=v[sl], bb=bb[sl], gb=gb[sl]))

    def within_chunk(group):
        for it in group:
            gcb = _cumsum_rows(it["gb"])
            gct = jnp.transpose(gcb)[:chunk]
            diff = gcb[:, :chunk] - gct
            it["decay"] = jnp.where(tril, jnp.exp(jnp.where(tril, diff, 0.0)), 0.0)
            it["kb"] = it["k"] * it["bb"]
            it["eg"] = jnp.exp(gcb)
            gl = gcb[chunk - 1:chunk]
            it["egl"] = jnp.exp(gl)
            it["kdt"] = jnp.transpose(it["k"] * jnp.exp(gl - gcb))
        for it in group:
            kq = _dot_nt(jnp.concatenate([it["kb"], it["q"]], axis=0), it["k"])
            it["a"] = jnp.where(strict, kq[:chunk] * it["decay"], 0.0)
            it["qk"] = jnp.where(tril, kq[chunk:] * it["decay"], 0.0)
        for it in group:
            it["inv"] = eye - it["a"]
            it["p"] = _dot(it["a"], it["a"])
        for lvl in range(n_sq):
            for it in group:
                if lvl < n_sq - 1:
                    r = _dot(jnp.concatenate([it["inv"], it["p"]], axis=0), it["p"])
                    it["inv"] = it["inv"] + r[:chunk]
                    it["p"] = r[chunk:]
                else:
                    it["inv"] = it["inv"] + _dot(it["inv"], it["p"])
        for it in group:
            it["wu"] = _dot(it["inv"], jnp.concatenate([it["kb"] * it["eg"], it["v"] * it["bb"]], axis=1))
        for it in group:
            mn = _dot(it["kdt"], it["wu"])
            qo = _dot(it["qk"], it["wu"])
            it["mp"], it["nc"] = mn[:, :HEAD_DIM], mn[:, HEAD_DIM:]
            it["qp"] = it["q"] * it["eg"] - qo[:, :HEAD_DIM]
            it["oc"] = qo[:, HEAD_DIM:]

    for g0 in range(0, len(items), group_items):
        within_chunk(items[g0:g0 + group_items])

    state = {(b, h): s_scr[b, h] for b in range(nb) for h in range(GDN_HEADS)}
    for c in range(n_chunks):
        for it in items:
            if it["c"] != c:
                continue
            s = state[(it["b"], it["h"])]
            r = _dot(jnp.concatenate([it["qp"], it["mp"]], axis=0), s)
            it["o"] = r[:chunk] + it["oc"]
            state[(it["b"], it["h"])] = s * it["egl"] - r[chunk:] + it["nc"]
    for (b, h), s in state.items():
        s_scr[b, h] = s

    for b in range(nb):
        zf = padrows(z_ref[b])
        for h in range(GDN_HEADS):
            outs = [it["o"] for it in items if it["b"] == b and it["h"] == h]
            o = outs[0] if n_chunks == 1 else jnp.concatenate(outs, axis=0)
            o = (o * lax.rsqrt(jnp.mean(o * o, axis=-1, keepdims=True) + EPS) * nw
                 * _silu(zf[:, h * HEAD_DIM:(h + 1) * HEAD_DIM]))
            o_ref[b, :, h * HEAD_DIM:(h + 1) * HEAD_DIM] = o[:tm]

    @pl.when(ti == nt - 1)
    def _():
        sn_ref[...] = s_scr[...]


def _gdn(proj, state_conv, s0, conv_w, alog_row, dtb_row, norm_w, nb, tm):
    bsz, t, _ = proj.shape
    tmp = max(tm, SUBLANES)
    chunk = min(GDN_CHUNK, tmp)
    row = pl.BlockSpec((1, LANES), lambda b, i: (0, 0))
    s_all, layer = s0
    sspec = pl.BlockSpec((nb, GDN_HEADS, HEAD_DIM, HEAD_DIM), lambda b, i: (b, 0, 0, 0))
    s_in = pl.BlockSpec((None, nb, GDN_HEADS, HEAD_DIM, HEAD_DIM), lambda b, i: (layer, b, 0, 0, 0))
    n_items = nb * GDN_HEADS * (tmp // chunk)
    kern = functools.partial(_gdn_kernel, nb=nb, tm=tm, tmp=tmp, chunk=chunk,
                             group_items=n_items // GDN_ITEM_GROUPS)
    return pl.pallas_call(
        kern,
        grid=(bsz // nb, t // tm),
        in_specs=[pl.BlockSpec((nb, tm, CONV_CH), lambda b, i: (b, i, OFF_QKV // CONV_CH)),
                  pl.BlockSpec((nb, tm, GDN_WIDTH), lambda b, i: (b, i, OFF_Z // GDN_WIDTH)),
                  pl.BlockSpec((nb, tm, LANES), lambda b, i: (b, i, OFF_AB // LANES)),
                  pl.BlockSpec((nb, CONV_WIDTH - 1, CONV_CH), lambda b, i: (b, 0, 0)),
                  pl.BlockSpec((CONV_WIDTH, CONV_CH), lambda b, i: (0, 0)),
                  s_in, row, row, row],
        out_specs=[pl.BlockSpec((nb, tm, GDN_WIDTH), lambda b, i: (b, i, 0)), sspec],
        out_shape=[jax.ShapeDtypeStruct((bsz, t, GDN_WIDTH), F32),
                   jax.ShapeDtypeStruct(s_all.shape[1:], F32)],
        scratch_shapes=[pltpu.VMEM((nb, GDN_HEADS, HEAD_DIM, HEAD_DIM), F32),
                        pltpu.VMEM((nb, SUBLANES, CONV_CH), F32)],
        compiler_params=_cparams(("arbitrary", "arbitrary")),
        name="gdn",
    )(proj, proj, proj, state_conv, conv_w, s_all, alog_row, dtb_row, norm_w)


def _pool_kernel(u_ref, st_ref, w_ref, sc_ref, o_ref, prev_scr, *, nb, tm, tmp, start_pos):
    ti = pl.program_id(1)
    halo = 2 * SUBLANES

    @pl.when(ti == 0)
    def _():
        for b in range(nb):
            prev_scr[b] = jnp.concatenate([jnp.zeros((halo - POOL_BUF, POOL_WIDTH), F32), st_ref[b]], axis=0)

    lane = lax.broadcasted_iota(jnp.int32, (tmp, POOL_WIDTH), 1)
    rowi = lax.broadcasted_iota(jnp.int32, (tmp, POOL_WIDTH), 0)
    pos1 = start_pos + ti * tm + rowi + 1
    grp = lane // POOL_GROUP
    win = jnp.where(grp == 0, POOL_WINDOWS[0],
                    jnp.where(grp == 1, POOL_WINDOWS[1], jnp.where(grp == 2, POOL_WINDOWS[2], POOL_WINDOWS[3])))
    cnt = jnp.minimum(pos1, win).astype(F32)
    for b in range(nb):
        x = u_ref[b]
        if tm != tmp:
            x = jnp.concatenate([x, jnp.zeros((tmp - tm, POOL_WIDTH), F32)], axis=0)
        e = jnp.concatenate([prev_scr[b], x], axis=0)
        sums = []
        s = e
        for w in POOL_WINDOWS:
            s = s + pltpu.roll(s, w // 2, axis=0)
            sums.append(s[halo:])
        sel = jnp.where(grp == 0, sums[0], jnp.where(grp == 1, sums[1], jnp.where(grp == 2, sums[2], sums[3])))
        pooled = sel / cnt - x
        out = _dot(pooled, w_ref[...]) * sc_ref[...]
        o_ref[b] = out[:tm]
        if tm >= halo:
            prev_scr[b] = x[tm - halo:tm]


def _pool(proj, state_pool, w_bd, pool_scale, nb, tm, start_pos):
    bsz, t, _ = proj.shape
    tmp = max(tm, SUBLANES)
    kern = functools.partial(_pool_kernel, nb=nb, tm=tm, tmp=tmp, start_pos=start_pos)
    return pl.pallas_call(
        kern,
        grid=(bsz // nb, t // tm),
        in_specs=[pl.BlockSpec((nb, tm, POOL_WIDTH), lambda b, i: (b, i, OFF_U // POOL_WIDTH)),
                  pl.BlockSpec((nb, POOL_BUF, POOL_WIDTH), lambda b, i: (b, 0, 0)),
                  pl.BlockSpec((POOL_WIDTH, POOL_WIDTH), lambda b, i: (0, 0)),
                  pl.BlockSpec((1, POOL_WIDTH), lambda b, i: (0, 0))],
        out_specs=pl.BlockSpec((nb, tm, POOL_WIDTH), lambda b, i: (b, i, 0)),
        out_shape=jax.ShapeDtypeStruct((bsz, t, POOL_WIDTH), F32),
        scratch_shapes=[pltpu.VMEM((nb, 2 * SUBLANES, POOL_WIDTH), F32)],
        compiler_params=_cparams(("arbitrary", "arbitrary")),
        name="pool",
    )(proj, state_pool, w_bd, pool_scale.reshape(1, POOL_WIDTH))


def _out_kernel(x_ref, po_ref, og_ref, gt_ref, sc_ref, sh_ref, gpm_ref, gpf_ref, wo_ref, rw_ref, rb_ref, c0_ref,
                x1_ref, ht_ref, g_ref, r_ref, cnt_ref, carry, *, tm):
    @pl.when((pl.program_id(0) == 0) & (pl.program_id(1) == 0))
    def _():
        carry[...] = c0_ref[...]

    mix = _dot(po_ref[0], wo_ref[:POOL_WIDTH]) + _dot(og_ref[0], wo_ref[POOL_WIDTH:])
    x1 = x_ref[0] + gt_ref[0] * _rms(mix, gpm_ref[...])
    x1_ref[0] = x1
    h = _rms(x1, gpf_ref[...]) * (1.0 + sc_ref[0]) + sh_ref[0]
    for s in range(ROW_TILES):
        ht_ref[pl.ds(s, tm, stride=ROW_TILES), :] = h[:, s * LANES:(s + 1) * LANES]
    logits = _dot_nt(rw_ref[...], h) + rb_ref[...]
    eid = lax.broadcasted_iota(jnp.int32, logits.shape, 0).astype(F32)
    work = logits
    sel = jnp.zeros(logits.shape, jnp.bool_)
    top = None
    den = jnp.zeros((1, tm), F32)
    for j in range(TOP_K):
        m = jnp.max(work, axis=0, keepdims=True)
        idx = jnp.min(jnp.where(work == m, eid, float(N_EXPERTS)), axis=0, keepdims=True)
        hit = eid == idx
        sel = sel | hit
        work = jnp.where(hit, -jnp.inf, work)
        if j == 0:
            top = m
        den = den + jnp.exp(m - top)
    gates = jnp.exp(logits - top) / den
    g_ref[...] = jnp.transpose(jnp.where(sel, gates, -1.0))
    mask = sel.astype(F32)
    ri = lax.broadcasted_iota(jnp.int32, (tm, tm), 0)
    ci = lax.broadcasted_iota(jnp.int32, (tm, tm), 1)
    c = carry[...]
    r_ref[...] = jnp.transpose(_dot(mask, (ri < ci).astype(F32)) + c)
    c = c + jnp.sum(mask, axis=1, keepdims=True)
    carry[...] = c
    cnt_ref[...] = c


def _out_proj(x, pool_out, o_gdn, gt, sc, sh, g_post_mix, g_pre_ffn, w_out_b, router_w, router_b, counts0,
              tm, per_token):
    bsz, t, _ = x.shape
    n = bsz * t
    tiles = t // tm

    def tok(width):
        return pl.BlockSpec((1, tm, width), lambda b, i: (b, i, 0))

    def const(shape):
        return pl.BlockSpec(shape, lambda b, i: tuple(0 for _ in shape))

    kern = functools.partial(_out_kernel, tm=tm)
    return pl.pallas_call(
        kern,
        grid=(bsz, tiles),
        in_specs=[tok(D_MODEL), tok(POOL_WIDTH), tok(GDN_WIDTH),
                  _mod_spec(per_token, tm), _mod_spec(per_token, tm), _mod_spec(per_token, tm),
                  const((1, D_MODEL)), const((1, D_MODEL)), const((D_MODEL, D_MODEL)),
                  const((N_EXPERTS, D_MODEL)), const((N_EXPERTS, 1)), const((N_EXPERTS, 1))],
        out_specs=[tok(D_MODEL),
                   pl.BlockSpec((tm * ROW_TILES, LANES), lambda b, i: (b * tiles + i, 0)),
                   pl.BlockSpec((tm, N_EXPERTS), lambda b, i: (b * tiles + i, 0)),
                   pl.BlockSpec((tm, N_EXPERTS), lambda b, i: (b * tiles + i, 0)),
                   const((N_EXPERTS, 1))],
        out_shape=[jax.ShapeDtypeStruct((bsz, t, D_MODEL), F32),
                   jax.ShapeDtypeStruct((n * ROW_TILES, LANES), F32),
                   jax.ShapeDtypeStruct((n, N_EXPERTS), F32),
                   jax.ShapeDtypeStruct((n, N_EXPERTS), F32),
                   jax.ShapeDtypeStruct((N_EXPERTS, 1), F32)],
        scratch_shapes=[pltpu.VMEM((N_EXPERTS, 1), F32)],
        compiler_params=_cparams(("arbitrary", "arbitrary")),
        name="out_proj",
    )(x, pool_out, o_gdn, gt, sc, sh, g_post_mix.reshape(1, D_MODEL), g_pre_ffn.reshape(1, D_MODEL),
      w_out_b, jnp.swapaxes(router_w, 0, 1), router_b.reshape(N_EXPERTS, 1), counts0)


def _dest_kernel(g_ref, r_ref, start_ref, base_ref, d_ref, l_ref, gt_ref, *, tm, sub):
    g = g_ref[...]
    mask = g >= 0.0
    maskf = mask.astype(F32)
    dest = start_ref[...] + r_ref[...]
    base = jnp.concatenate([jnp.broadcast_to(base_ref[0, q:q + 1, :], (sub, N_EXPERTS)) for q in range(tm // sub)],
                           axis=0)
    local = dest - base
    ri = lax.broadcasted_iota(jnp.int32, (N_EXPERTS, N_EXPERTS), 0)
    ci = lax.broadcasted_iota(jnp.int32, (N_EXPERTS, N_EXPERTS), 1)
    slot = _dot(maskf, (ri < ci).astype(F32))
    lane4 = lax.broadcasted_iota(jnp.int32, (tm, TOP_K), 1)
    d_out = jnp.zeros((tm, TOP_K), F32)
    l_out = jnp.zeros((tm, TOP_K), F32)
    g_out = jnp.zeros((tm, TOP_K), F32)
    for j in range(TOP_K):
        pick = mask & (slot == float(j))
        dj = jnp.sum(jnp.where(pick, dest, 0.0), axis=-1, keepdims=True)
        lj = jnp.sum(jnp.where(pick, local, 0.0), axis=-1, keepdims=True)
        gj = jnp.sum(jnp.where(pick, g, 0.0), axis=-1, keepdims=True)
        d_out = jnp.where(lane4 == j, dj, d_out)
        l_out = jnp.where(lane4 == j, lj, l_out)
        g_out = jnp.where(lane4 == j, gj, g_out)
    d_ref[...] = d_out.astype(jnp.int32)
    l_ref[...] = l_out.astype(jnp.int32)
    gt_ref[...] = g_out


def _dest(gsel, rank, start_row, base, tm, sub):
    n = gsel.shape[0]
    per = tm // sub
    kern = functools.partial(_dest_kernel, tm=tm, sub=sub)
    tok4 = pl.BlockSpec((tm, TOP_K), lambda i: (i, 0))
    return pl.pallas_call(
        kern,
        grid=(n // tm,),
        in_specs=[pl.BlockSpec((tm, N_EXPERTS), lambda i: (i, 0)),
                  pl.BlockSpec((tm, N_EXPERTS), lambda i: (i, 0)),
                  pl.BlockSpec((1, N_EXPERTS), lambda i: (0, 0)),
                  pl.BlockSpec((1, per, N_EXPERTS), lambda i: (i, 0, 0))],
        out_specs=[tok4, tok4, tok4],
        out_shape=[jax.ShapeDtypeStruct((n, TOP_K), jnp.int32),
                   jax.ShapeDtypeStruct((n, TOP_K), jnp.int32),
                   jax.ShapeDtypeStruct((n, TOP_K), F32)],
        compiler_params=_cparams(("arbitrary",)),
        name="dest",
    )(gsel, rank, start_row, base.reshape(n // tm, per, N_EXPERTS))


def _row_copy(src, dst, s_row, d_row, sem):
    return pltpu.make_async_copy(src.at[pl.ds(pl.multiple_of(s_row * ROW_TILES, ROW_TILES), ROW_TILES), :],
                                 dst.at[pl.ds(pl.multiple_of(d_row * ROW_TILES, ROW_TILES), ROW_TILES), :], sem)


def _scatter_kernel(cnt_ref, start_ref, end_ref, nu_ref, dest_ref, ha_ref, hb_ref, xs_ref, sem, *,
                    tm, bm, n_blocks, tiles_a):
    i = pl.program_id(0)

    def scatter_from(ht_ref):
        def body(t, carry):
            src = ht_ref.at[pl.ds(pl.multiple_of(t * ROW_TILES, ROW_TILES), ROW_TILES), :]
            for k in range(TOP_K):
                d = dest_ref[t * TOP_K + k]
                pltpu.make_async_copy(
                    src, xs_ref.at[pl.ds(pl.multiple_of(d * ROW_TILES, ROW_TILES), ROW_TILES), :],
                    sem).start(priority=k % DMA_QUEUES)
            return carry

        lax.fori_loop(0, tm, body, 0)

    @pl.when(i < tiles_a)
    def _():
        scatter_from(ha_ref)

    @pl.when(i >= tiles_a)
    def _():
        scatter_from(hb_ref)

    ht_ref = hb_ref

    def tile_copy(u):
        return pltpu.make_async_copy(
            ht_ref, xs_ref.at[pl.ds(pl.multiple_of(u * (tm * ROW_TILES), tm * ROW_TILES), tm * ROW_TILES), :], sem)

    for k in range(TOP_K):
        tile_copy(0).wait()

    @pl.when(i == pl.num_programs(0) - 1)
    def _():
        per = bm // tm

        def tail_start(u, carry):
            tile_copy(u).start()
            return carry

        def tail_wait(u, carry):
            tile_copy(u).wait()
            return carry

        lax.fori_loop(nu_ref[0] * per, n_blocks * per, tail_start, 0)
        lax.fori_loop(nu_ref[0] * per, n_blocks * per, tail_wait, 0)

        chunk = bm // 4
        assert chunk <= tm

        def chunk_copy(q):
            return pltpu.make_async_copy(
                ht_ref.at[pl.ds(0, chunk * ROW_TILES), :],
                xs_ref.at[pl.ds(pl.multiple_of(q * (chunk * ROW_TILES), chunk * ROW_TILES), chunk * ROW_TILES), :],
                sem)

        def per_expert(e, carry):
            lo = start_ref[e] + cnt_ref[e]
            mid = (lo + chunk - 1) // chunk * chunk
            hi = end_ref[e]

            def fill(r, c2):
                _row_copy(ht_ref, xs_ref, 0, r, sem).start()
                return c2

            def fill_chunk(q, c2):
                chunk_copy(q).start()
                return c2

            lax.fori_loop(lo, mid, fill, 0)
            lax.fori_loop(mid // chunk, hi // chunk, fill_chunk, 0)

            def drain(r, c2):
                _row_copy(ht_ref, xs_ref, 0, r, sem).wait()
                return c2

            def drain_chunk(q, c2):
                chunk_copy(q).wait()
                return c2

            lax.fori_loop(lo, mid, drain, 0)
            lax.fori_loop(mid // chunk, hi // chunk, drain_chunk, 0)
            return carry

        lax.fori_loop(0, N_EXPERTS, per_expert, 0)


def _scatter_rows(ht_a, ht_b, dest_flat, cnt, start, end, n_used, n_blocks, tm, bm):
    rows = n_blocks * bm
    tiles_a, tiles_b = ht_a.shape[0] // (tm * ROW_TILES), ht_b.shape[0] // (tm * ROW_TILES)
    assert bm % tm == 0 and ht_a.shape[0] % (tm * ROW_TILES) == 0 and ht_b.shape[0] % (tm * ROW_TILES) == 0
    assert tiles_b >= 1
    kern = functools.partial(_scatter_kernel, tm=tm, bm=bm, n_blocks=n_blocks, tiles_a=tiles_a)
    return pl.pallas_call(
        kern,
        grid_spec=pltpu.PrefetchScalarGridSpec(
            num_scalar_prefetch=4,
            grid=(tiles_a + tiles_b,),
            in_specs=[pl.BlockSpec((tm * TOP_K,), lambda i, *_: (i,), memory_space=pltpu.SMEM),
                      pl.BlockSpec((tm * ROW_TILES, LANES), lambda i, *_: (jnp.minimum(i, tiles_a - 1), 0)),
                      pl.BlockSpec((tm * ROW_TILES, LANES), lambda i, *_: (jnp.maximum(i - tiles_a, 0), 0))],
            out_specs=pl.BlockSpec(memory_space=pl.ANY),
            scratch_shapes=[pltpu.SemaphoreType.DMA(())]),
        out_shape=jax.ShapeDtypeStruct((rows * ROW_TILES, LANES), F32),
        compiler_params=_cparams(("arbitrary",)),
        name="scatter_rows",
    )(cnt, start, end, n_used, dest_flat, ht_a, ht_b)


def _expert_kernel(be_ref, nx_ref, nv_ref, nu_ref, xs_ref, wg_ref, bg_ref, wd_ref, bd_ref, ys_ref,
                   wg_f, wd_f, wg_s, wd_s, wsem, *, bm):
    j = pl.program_id(0)
    prev = be_ref[jnp.maximum(j - 1, 0)]
    fresh = (j == 0) | (be_ref[j] != prev)
    active = j < nu_ref[0]

    def weight_copies(e):
        return (pltpu.make_async_copy(wg_ref.at[e], wg_f, wsem.at[0]),
                pltpu.make_async_copy(wd_ref.at[e], wd_f, wsem.at[1]))

    @pl.when(j == 0)
    def _():
        for c in weight_copies(be_ref[0]):
            c.start()

    @pl.when(active & fresh)
    def _():
        for c in weight_copies(be_ref[j]):
            c.wait()
        wg_s[...] = wg_f[...].astype(BF16)
        wd_s[...] = wd_f[...].astype(BF16)

        @pl.when(nx_ref[j] != be_ref[j])
        def _():
            for c in weight_copies(nx_ref[j]):
                c.start()

    def run_rows(m):
        x = jnp.concatenate([xs_ref[pl.ds(s, m, stride=ROW_TILES), :] for s in range(ROW_TILES)], axis=-1)
        xb = x.astype(BF16)
        acts = []
        for c in range(EXPERT_COL_CHUNKS):
            w = D_FF // EXPERT_COL_CHUNKS
            lo = c * w
            gate = jnp.dot(xb, wg_s[:, lo:lo + w], preferred_element_type=F32) + bg_ref[0, :, lo:lo + w]
            up = (jnp.dot(xb, wg_s[:, D_FF + lo:D_FF + lo + w], preferred_element_type=F32)
                  + bg_ref[0, :, D_FF + lo:D_FF + lo + w])
            gate = jnp.minimum(gate, SWIGLU_LIMIT)
            up = jnp.clip(up, -SWIGLU_LIMIT, SWIGLU_LIMIT)
            acts.append(((up + 1.0) * (gate * jax.nn.sigmoid(SWIGLU_ALPHA * gate))).astype(BF16))
        act = jnp.concatenate(acts, axis=-1)
        for c in range(EXPERT_COL_CHUNKS):
            w = D_MODEL // EXPERT_COL_CHUNKS
            lo = c * w
            y = jnp.dot(act, wd_s[:, lo:lo + w], preferred_element_type=F32) + bd_ref[0, :, lo:lo + w]
            for s in range(w // LANES):
                ys_ref[pl.ds(lo // LANES + s, m, stride=ROW_TILES), :] = y[:, s * LANES:(s + 1) * LANES]
        if m < bm:
            ys_ref[pl.ds(m * ROW_TILES, (bm - m) * ROW_TILES), :] = jnp.zeros(((bm - m) * ROW_TILES, LANES), F32)

    quarter = bm // 4
    quarters = (nv_ref[j] + quarter - 1) // quarter
    for nq in range(1, 5):
        @pl.when(active & ((quarters == nq) if nq > 1 else (quarters <= 1)))
        def _(nq=nq):
            run_rows(nq * quarter)

    @pl.when(jnp.logical_not(active))
    def _():
        ys_ref[...] = jnp.zeros_like(ys_ref)


def _experts(xs, block_expert, next_expert, block_valid, n_used, w_gu, b_gu, w_down, b_down, bm):
    rows = xs.shape[0] // ROW_TILES
    n_blocks = rows // bm

    def blk(j, be, nx, nv, nu):
        return (jnp.minimum(j, nu[0] - 1), 0)

    def bsel(j, be, nx, nv, nu):
        return (be[j], 0, 0)

    hbm = pl.BlockSpec(memory_space=pl.ANY)
    kern = functools.partial(_expert_kernel, bm=bm)
    return pl.pallas_call(
        kern,
        grid_spec=pltpu.PrefetchScalarGridSpec(
            num_scalar_prefetch=4,
            grid=(n_blocks,),
            in_specs=[pl.BlockSpec((bm * ROW_TILES, LANES), blk), hbm,
                      pl.BlockSpec((1, 1, 2 * D_FF), bsel), hbm,
                      pl.BlockSpec((1, 1, D_MODEL), bsel)],
            out_specs=pl.BlockSpec((bm * ROW_TILES, LANES), lambda j, be, nx, nv, nu: (j, 0)),
            scratch_shapes=[pltpu.VMEM((D_MODEL, 2 * D_FF), F32),
                            pltpu.VMEM((D_FF, D_MODEL), F32),
                            pltpu.VMEM((D_MODEL, 2 * D_FF), BF16),
                            pltpu.VMEM((D_FF, D_MODEL), BF16),
                            pltpu.SemaphoreType.DMA((2,))]),
        out_shape=jax.ShapeDtypeStruct((rows * ROW_TILES, LANES), F32),
        compiler_params=_cparams(("arbitrary",)),
        name="experts",
    )(block_expert, next_expert, block_valid, n_used, xs, w_gu, b_gu.reshape(N_EXPERTS, 1, 2 * D_FF), w_down,
      b_down.reshape(N_EXPERTS, 1, D_MODEL))


def _final_kernel(loc_ref, gate_ref, tcur_ref, tnext_ref, ys_ref, x1_ref, gt_ref, g_ref, o_ref,
                  st_ref, f_ref, sem, *, tm, steps):
    g = pl.program_id(0) * pl.num_programs(1) + pl.program_id(1)
    cur = g % 2
    sizes = [1 << b for b in range(tm.bit_length() - 1, -1, -1)]

    def request(t_ref, slot):
        def per_expert(e, off):
            row = t_ref[e]
            cnt = t_ref[N_EXPERTS + e]
            for size in sizes:
                take = (cnt & size) != 0

                @pl.when(take)
                def _(row=row, off=off, size=size):
                    pltpu.make_async_copy(
                        ys_ref.at[pl.ds(pl.multiple_of(row * ROW_TILES, ROW_TILES), size * ROW_TILES), :],
                        st_ref.at[slot, pl.ds(pl.multiple_of(off * ROW_TILES, ROW_TILES), size * ROW_TILES), :],
                        sem.at[slot]).start()

                step = jnp.where(take, size, 0)
                row = row + step
                off = off + step
            return off

        lax.fori_loop(0, N_EXPERTS, per_expert, 0)

    @pl.when(g == 0)
    def _():
        request(tcur_ref, 0)

    @pl.when(g + 1 < steps)
    def _():
        request(tnext_ref, 1 - cur)

    pltpu.make_async_copy(ys_ref.at[pl.ds(0, tm * TOP_K * ROW_TILES), :], st_ref.at[cur], sem.at[cur]).wait()

    def combine(t, carry):
        acc = None
        for k in range(TOP_K):
            j = t * TOP_K + k
            row = st_ref[cur, pl.ds(pl.multiple_of(loc_ref[j] * ROW_TILES, ROW_TILES), ROW_TILES), :]
            part = row * gate_ref[j]
            acc = part if acc is None else acc + part
        f_ref[pl.ds(pl.multiple_of(t * ROW_TILES, ROW_TILES), ROW_TILES), :] = acc
        return carry

    lax.fori_loop(0, tm, combine, 0)
    f = jnp.concatenate([f_ref[pl.ds(s, tm, stride=ROW_TILES), :] for s in range(ROW_TILES)], axis=-1)
    o_ref[0] = x1_ref[0] + gt_ref[0] * _rms(f, g_ref[...])


def _final(ys, local_flat, gates_flat, run_table, x1, gt, g_post_ffn, tok_off, tm, per_token):
    bsz, t, _ = x1.shape
    tiles = t // tm
    off = tok_off // tm
    steps = bsz * tiles
    kern = functools.partial(_final_kernel, tm=tm, steps=steps)

    def smem(ahead):
        return pl.BlockSpec((tm * TOP_K,), lambda b, i: (off + jnp.minimum(b * tiles + i + ahead, steps - 1),),
                            memory_space=pltpu.SMEM)

    return pl.pallas_call(
        kern,
        grid=(bsz, tiles),
        in_specs=[smem(0), smem(0), smem(0), smem(1),
                  pl.BlockSpec(memory_space=pl.ANY),
                  pl.BlockSpec((1, tm, D_MODEL), lambda b, i: (b, i, 0)),
                  _mod_spec(per_token, tm),
                  pl.BlockSpec((1, D_MODEL), lambda b, i: (0, 0))],
        out_specs=pl.BlockSpec((1, tm, D_MODEL), lambda b, i: (b, i, 0)),
        out_shape=jax.ShapeDtypeStruct((bsz, t, D_MODEL), F32),
        scratch_shapes=[pltpu.VMEM((2, tm * TOP_K * ROW_TILES, LANES), F32),
                        pltpu.VMEM((tm * ROW_TILES, LANES), F32),
                        pltpu.SemaphoreType.DMA((2,))],
        compiler_params=_cparams(("arbitrary", "arbitrary")),
        name="final",
    )(local_flat, gates_flat, run_table, run_table, ys, x1, gt, g_post_ffn.reshape(1, D_MODEL))


def _moe(ht_a, ht_b, gsel, rank, counts, w_gu, b_gu, w_down, b_down, tm, bm):
    n = gsel.shape[0]
    cnt = counts[:, 0].astype(jnp.int32)
    nblk = (cnt + bm - 1) // bm
    blk_end = jnp.cumsum(nblk)
    blk_start = blk_end - nblk
    n_blocks = -(-(n * TOP_K) // bm) + N_EXPERTS
    blocks = jnp.arange(n_blocks, dtype=jnp.int32)
    block_expert = jnp.minimum(jnp.sum((blk_end[None, :] <= blocks[:, None]).astype(jnp.int32), axis=1),
                               N_EXPERTS - 1).astype(jnp.int32)
    ids = jnp.arange(N_EXPERTS, dtype=jnp.int32)
    later = jnp.where((nblk[None, :] > 0) & (ids[None, :] > ids[:, None]), ids[None, :], N_EXPERTS)
    nearest = jnp.min(later, axis=1)
    next_used = jnp.where(nearest < N_EXPERTS, nearest, ids).astype(jnp.int32)
    n_used = blk_end[-1:].astype(jnp.int32)
    row_start = (blk_start * bm).astype(jnp.int32)
    row_end = (blk_end * bm).astype(jnp.int32)
    before = rank[::tm]
    run_len = jnp.concatenate([before[1:], counts.reshape(1, N_EXPERTS)], axis=0) - before
    run_row = row_start.astype(F32)[None, :] + before
    run_off = jnp.cumsum(run_len, axis=1) - run_len
    tiles = n // tm
    run_table = jnp.zeros((tiles, tm * TOP_K), jnp.int32)
    run_table = run_table.at[:, :N_EXPERTS].set(run_row.astype(jnp.int32))
    run_table = run_table.at[:, N_EXPERTS:2 * N_EXPERTS].set(run_len.astype(jnp.int32)).reshape(tiles * tm * TOP_K)
    dest, local, gates = _dest(gsel, rank, row_start.astype(F32).reshape(1, N_EXPERTS), run_row - run_off,
                               DEST_TILE, tm)
    dest_flat = dest.reshape(n * TOP_K)
    xs = _scatter_rows(ht_a, ht_b, dest_flat, cnt, row_start, row_end, n_used, n_blocks, SCATTER_TILE, bm)
    owns = (blocks[:, None] >= blk_start[None, :]) & (blocks[:, None] < blk_end[None, :])

    def per_block(table):
        return jnp.sum(jnp.where(owns, table[None, :], 0), axis=1).astype(jnp.int32)

    block_valid = jnp.clip(per_block(cnt) - (blocks - per_block(blk_start)) * bm, 0, bm).astype(jnp.int32)
    ys = _experts(xs, block_expert, per_block(next_used), block_valid, n_used, w_gu, b_gu, w_down, b_down, bm)
    return ys, local.reshape(n * TOP_K), gates.reshape(n * TOP_K), run_table


def _mix_path(x, mod, pool_buf, conv_buf, s0, start_pos, counts0, lp, tm, nb, per_token):
    (g_pre_mix, g_post_mix, g_pre_ffn, w_in_p, w_bd, pool_scale, conv_w, alog_row, dtb_row, norm_w,
     w_out_b, router_w, router_b) = lp
    bsz, t, _ = x.shape
    sh_mix, sc_mix, gt_mix, sh_ffn, sc_ffn, gt_ffn = (mod[:, i] for i in range(6))

    def shape_mod(m):
        if per_token:
            return jnp.repeat(m, t, axis=0).reshape(1, bsz * t, D_MODEL)
        return m.reshape(bsz, 1, D_MODEL)

    sh_mix, sc_mix, gt_mix, sh_ffn, sc_ffn, gt_ffn = map(shape_mod, (sh_mix, sc_mix, gt_mix, sh_ffn, sc_ffn, gt_ffn))
    xf = x.reshape(1, bsz * t, D_MODEL) if per_token else x
    proj = _in_proj(xf, sc_mix, sh_mix, g_pre_mix, w_in_p, PROJ_TILE, per_token).reshape(bsz, t, PROJ_WIDTH)
    tseq = min(tm, t)
    o_gdn, s_new = _gdn(proj, conv_buf, s0, conv_w, alog_row, dtb_row, norm_w, nb, tseq)
    pool_out = _pool(proj, pool_buf, w_bd, pool_scale, nb * POOL_SEQS if per_token else nb, min(POOL_TILE, t),
                     start_pos)
    fl = (lambda a: a.reshape(1, bsz * t, a.shape[-1])) if per_token else (lambda a: a)
    x1, ht, gsel, rank, counts = _out_proj(xf, fl(pool_out), fl(o_gdn), gt_mix, sc_ffn, sh_ffn, g_post_mix,
                                           g_pre_ffn, w_out_b, router_w, router_b, counts0, OUT_TILE, per_token)

    def last_rows(buf, off, width):
        keep = buf.shape[1]
        if t >= keep:
            return proj[:, t - keep:, off:off + width]
        return jnp.concatenate([buf[:, t:], proj[:, :, off:off + width]], axis=1)

    pool_new = last_rows(pool_buf, OFF_U, POOL_WIDTH)
    conv_new = last_rows(conv_buf, OFF_QKV, CONV_CH)
    return x1, ht, gsel, rank, counts, gt_ffn, pool_new, conv_new, s_new


def kernel(x_prompt, x_sample, c_prompt, c_sample, state_pool, state_conv, state_ssm, w_ada, b_ada, g_pre_mix, g_post_mix, g_pre_ffn, g_post_ffn, w_in, w_pool, pool_scale, conv_w, a_log, dt_bias, gdn_norm_w, w_out, router_w, router_b, w_gu, b_gu, w_down, b_down):
    depth = w_ada.shape[0]
    bp, tp, _ = x_prompt.shape
    bs, ts, _ = x_sample.shape
    tm = TOKEN_TILE
    y_p, y_s = x_prompt, x_sample
    outs = [[] for _ in range(6)]
    for l in range(depth):
        mod = _ada(jnp.concatenate([c_prompt, c_sample], axis=0), w_ada[l], b_ada[l]).reshape(bp + bs, 6, D_MODEL)
        w_in_p = jnp.swapaxes(w_in[l], 0, 1)
        w_bd = jax.scipy.linalg.block_diag(*[w_pool[l, g] for g in range(len(POOL_WINDOWS))])
        pad_row = lambda v: jnp.pad(v, (0, LANES - v.shape[0])).reshape(1, LANES)
        lp = (g_pre_mix[l], g_post_mix[l], g_pre_ffn[l], w_in_p, w_bd, pool_scale[l], conv_w[l],
              pad_row(a_log[l]), pad_row(dt_bias[l]), gdn_norm_w[l].reshape(1, HEAD_DIM),
              w_out[l].astype(BF16), router_w[l], router_b[l])
        zero_pool = jnp.zeros((bp, POOL_BUF, POOL_WIDTH), F32)
        zero_conv = jnp.zeros((bp, CONV_WIDTH - 1, CONV_CH), F32)
        zero_ssm = (jnp.zeros((1, bp, GDN_HEADS, HEAD_DIM, HEAD_DIM), F32), 0)
        no_counts = jnp.zeros((N_EXPERTS, 1), F32)
        x1p, htp, gp, rp, cnt_p, gtp, pp, cp, sp = _mix_path(
            y_p, mod[:bp], zero_pool, zero_conv, zero_ssm, 0, no_counts, lp, tm, 1, False)
        x1s, hts, gs, rs, counts, gts, ps, cs, ss = _mix_path(
            y_s, mod[bp:], state_pool[l], state_conv[l], (state_ssm, l), PAST_LEN, cnt_p, lp, tm, 8, True)
        gsel = jnp.concatenate([gp, gs], axis=0)
        rank = jnp.concatenate([rp, rs], axis=0)
        ys, local, gates, runs = _moe(htp, hts, gsel, rank, counts, w_gu[l], b_gu[l], w_down[l], b_down[l],
                                      tm, MOE_BLOCK)
        y_p = _final(ys, local, gates, runs, x1p, gtp, g_post_ffn[l], 0, tm, False)
        y_s = _final(ys, local, gates, runs, x1s, gts, g_post_ffn[l], bp * tp, tm, True).reshape(bs, ts, D_MODEL)
        for lst, val in zip(outs, (pp, cp, sp, ps, cs, ss)):
            lst.append(val)
    return (y_p, y_s) + tuple(o[0][None] if depth == 1 else jnp.stack(o) for o in outs)
```

```python
import functools
import math

import jax
import jax.numpy as jnp
from jax import lax
from jax.experimental import pallas as pl
from jax.experimental.pallas import tpu as pltpu

F32 = jnp.float32
BF16 = jnp.bfloat16

D_MODEL = 1024
PAST_LEN = 16384
POOL_WINDOWS = (2, 4, 8, 16)
POOL_WIDTH = D_MODEL // 4
POOL_GROUP = POOL_WIDTH // len(POOL_WINDOWS)
POOL_BUF = max(POOL_WINDOWS) - 1
GDN_WIDTH = D_MODEL - POOL_WIDTH
HEAD_DIM = 128
GDN_HEADS = GDN_WIDTH // HEAD_DIM
CONV_WIDTH = 4
CONV_CH = 3 * GDN_WIDTH
GDN_CHUNK = 64
N_EXPERTS = 32
TOP_K = 4
D_FF = D_MODEL
SWIGLU_LIMIT = 7.0
SWIGLU_ALPHA = 1.702
EPS = 1e-6

SUBLANES = 8
LANES = 128
ROW_TILES = D_MODEL // LANES
OFF_QKV = 0
OFF_Z = OFF_QKV + CONV_CH
OFF_U = OFF_Z + GDN_WIDTH
OFF_AB = OFF_U + POOL_WIDTH
PROJ_WIDTH = OFF_AB + LANES
assert OFF_Z % GDN_WIDTH == 0 and OFF_U % POOL_WIDTH == 0 and OFF_AB % LANES == 0
TOKEN_TILE = 256
PROJ_TILE = 512
OUT_TILE = 512
DEST_TILE = 1536
POOL_TILE = 1024
POOL_SEQS = 4
GDN_ITEM_GROUPS = 1
MOE_BLOCK = 1024
EXPERT_COL_CHUNKS = 2
VMEM_LIMIT = 56 * 1024 * 1024


def _cparams(sem):
    return pltpu.CompilerParams(dimension_semantics=sem, vmem_limit_bytes=VMEM_LIMIT)


def _dot(a, b):
    return jnp.dot(a.astype(BF16), b.astype(BF16), preferred_element_type=F32)


def _dot_nt(a, b):
    return lax.dot_general(a.astype(BF16), b.astype(BF16), (((1,), (1,)), ((), ())),
                           preferred_element_type=F32)


def _silu(x):
    return x * jax.nn.sigmoid(x)


def _rms(x, g):
    return x * lax.rsqrt(jnp.mean(x * x, axis=-1, keepdims=True) + EPS) * g


def _ada_kernel(c_ref, w_ref, b_ref, o_ref):
    o_ref[...] = _dot(_silu(c_ref[...]), w_ref[...]) + b_ref[...]


def _ada(c, w_ada, b_ada):
    n = c.shape[0]
    cols = w_ada.shape[1]
    blk = D_MODEL
    return pl.pallas_call(
        _ada_kernel,
        grid=(cols // blk,),
        in_specs=[pl.BlockSpec((n, D_MODEL), lambda j: (0, 0)),
                  pl.BlockSpec((D_MODEL, blk), lambda j: (0, j)),
                  pl.BlockSpec((1, blk), lambda j: (0, j))],
        out_specs=pl.BlockSpec((n, blk), lambda j: (0, j)),
        out_shape=jax.ShapeDtypeStruct((n, cols), F32),
        compiler_params=_cparams(("arbitrary",)),
        name="ada",
    )(c, w_ada, b_ada.reshape(1, cols))


def _in_kernel(x_ref, sc_ref, sh_ref, g_ref, wt_ref, o_ref, wb):
    @pl.when((pl.program_id(0) == 0) & (pl.program_id(1) == 0))
    def _():
        o1, o2, o3 = POOL_WIDTH, POOL_WIDTH + CONV_CH, POOL_WIDTH + CONV_CH + GDN_WIDTH
        wb[OFF_QKV:OFF_QKV + CONV_CH] = wt_ref[o1:o2].astype(BF16)
        wb[OFF_Z:OFF_Z + GDN_WIDTH] = wt_ref[o2:o3].astype(BF16)
        wb[OFF_U:OFF_U + POOL_WIDTH] = wt_ref[:o1].astype(BF16)
        tail = wt_ref.shape[0] - o3
        wb[OFF_AB:] = jnp.concatenate(
            [wt_ref[o3:], jnp.zeros((LANES - tail, D_MODEL), F32)], axis=0).astype(BF16)

    h = _rms(x_ref[0], g_ref[...]) * (1.0 + sc_ref[0]) + sh_ref[0]
    o_ref[0] = _dot_nt(h, wb[...])


def _mod_spec(per_token, tm):
    if per_token:
        return pl.BlockSpec((1, tm, D_MODEL), lambda b, t: (0, t, 0))
    return pl.BlockSpec((1, 1, D_MODEL), lambda b, t: (b, 0, 0))


def _in_proj(x, sc, sh, g, w_in, tm, per_token):
    bsz, t, _ = x.shape
    return pl.pallas_call(
        _in_kernel,
        grid=(bsz, t // tm),
        in_specs=[pl.BlockSpec((1, tm, D_MODEL), lambda b, i: (b, i, 0)),
                  _mod_spec(per_token, tm), _mod_spec(per_token, tm),
                  pl.BlockSpec((1, D_MODEL), lambda b, i: (0, 0)),
                  pl.BlockSpec(w_in.shape, lambda b, i: (0, 0), pipeline_mode=pl.Buffered(1))],
        out_specs=pl.BlockSpec((1, tm, PROJ_WIDTH), lambda b, i: (b, i, 0)),
        out_shape=jax.ShapeDtypeStruct((bsz, t, PROJ_WIDTH), F32),
        scratch_shapes=[pltpu.VMEM((PROJ_WIDTH, D_MODEL), BF16)],
        compiler_params=_cparams(("arbitrary", "arbitrary")),
        name="in_proj",
    )(x, sc, sh, g.reshape(1, D_MODEL), w_in)


def _cumsum_rows(x):
    n = x.shape[0]
    row = lax.broadcasted_iota(jnp.int32, x.shape, 0)
    s = 1
    while s < n:
        x = x + jnp.where(row >= s, pltpu.roll(x, s, axis=0), 0.0)
        s *= 2
    return x


def _gdn_kernel(qkv_ref, z_ref, ab_ref, cs_ref, cw_ref, s0_ref, alog_ref, dtb_ref, nw_ref, o_ref, sn_ref,
                s_scr, prev_scr, *, nb, tm, tmp, chunk, group_items):
    ti = pl.program_id(1)
    nt = pl.num_programs(1)
    n_chunks = tmp // chunk
    n_sq = int(math.log2(chunk)) - 1

    @pl.when(ti == 0)
    def _():
        s_scr[...] = s0_ref[...]
        for b in range(nb):
            prev_scr[b] = jnp.concatenate(
                [jnp.zeros((SUBLANES - (CONV_WIDTH - 1), CONV_CH), F32), cs_ref[b]], axis=0)

    rowi = lax.broadcasted_iota(jnp.int32, (tmp, LANES), 0)
    ri = lax.broadcasted_iota(jnp.int32, (chunk, chunk), 0)
    ci = lax.broadcasted_iota(jnp.int32, (chunk, chunk), 1)
    tril = ri >= ci
    strict = ri > ci
    eye = (ri == ci).astype(F32)
    neg_a = -jnp.exp(alog_ref[...])
    dtb = dtb_ref[...]
    nw = nw_ref[...]

    def padrows(x):
        if tm == tmp:
            return x
        return jnp.concatenate([x, jnp.zeros((tmp - tm, x.shape[1]), F32)], axis=0)

    cw = cw_ref[...]
    valid = rowi < tm
    items = []
    for b in range(nb):
        x = padrows(qkv_ref[b])
        e = jnp.concatenate([prev_scr[b], x], axis=0)
        acc = x * cw[CONV_WIDTH - 1:CONV_WIDTH]
        for d in range(1, CONV_WIDTH):
            acc = acc + pltpu.roll(e, d, axis=0)[SUBLANES:] * cw[CONV_WIDTH - 1 - d:CONV_WIDTH - d]
        if tm >= SUBLANES:
            prev_scr[b] = x[tm - SUBLANES:tm]
        y = _silu(acc)
        ab = padrows(ab_ref[b])
        g_all = jnp.where(valid, neg_a * jax.nn.softplus(ab + dtb), 0.0)
        beta_all = jnp.where(valid, jax.nn.sigmoid(ab), 0.0)
        for h in range(GDN_HEADS):
            qc = y[:, h * HEAD_DIM:(h + 1) * HEAD_DIM]
            kc = y[:, GDN_WIDTH + h * HEAD_DIM:GDN_WIDTH + (h + 1) * HEAD_DIM]
            v = y[:, 2 * GDN_WIDTH + h * HEAD_DIM:2 * GDN_WIDTH + (h + 1) * HEAD_DIM]
            q = qc * lax.rsqrt(jnp.sum(qc * qc, axis=-1, keepdims=True) + EPS) * (HEAD_DIM ** -0.5)
            k = kc * lax.rsqrt(jnp.sum(kc * kc, axis=-1, keepdims=True) + EPS)
            gb = jnp.broadcast_to(g_all[:, h:h + 1], (tmp, LANES))
            bb = jnp.broadcast_to(beta_all[:, GDN_HEADS + h:GDN_HEADS + h + 1], (tmp, LANES))
            for c in range(n_chunks):
                sl = slice(c * chunk, (c + 1) * chunk)
                items.append(dict(b=b, h=h, c=c, q=q[sl], k=k[sl], v=v[sl], bb=bb[sl], gb=gb[sl]))

    def within_chunk(group):
        for it in group:
            gcb = _cumsum_rows(it["gb"])
            gct = jnp.transpose(gcb)[:chunk]
            diff = gcb[:, :chunk] - gct
            it["decay"] = jnp.where(tril, jnp.exp(jnp.where(tril, diff, 0.0)), 0.0)
            it["kb"] = it["k"] * it["bb"]
            it["eg"] = jnp.exp(gcb)
            gl = gcb[chunk - 1:chunk]
            it["egl"] = jnp.exp(gl)
            it["kdt"] = jnp.transpose(it["k"] * jnp.exp(gl - gcb))
        for it in group:
            kq = _dot_nt(jnp.concatenate([it["kb"], it["q"]], axis=0), it["k"])
            it["a"] = jnp.where(strict, kq[:chunk] * it["decay"], 0.0)
            it["qk"] = jnp.where(tril, kq[chunk:] * it["decay"], 0.0)
        for it in group:
            it["inv"] = eye - it["a"]
            it["p"] = _dot(it["a"], it["a"])
        for lvl in range(n_sq):
            for it in group:
                if lvl < n_sq - 1:
                    r = _dot(jnp.concatenate([it["inv"], it["p"]], axis=0), it["p"])
                    it["inv"] = it["inv"] + r[:chunk]
                    it["p"] = r[chunk:]
                else:
                    it["inv"] = it["inv"] + _dot(it["inv"], it["p"])
        for it in group:
            it["wu"] = _dot(it["inv"], jnp.concatenate([it["kb"] * it["eg"], it["v"] * it["bb"]], axis=1))
        for it in group:
            mn = _dot(it["kdt"], it["wu"])
            qo = _dot(it["qk"], it["wu"])
            it["mp"], it["nc"] = mn[:, :HEAD_DIM], mn[:, HEAD_DIM:]
            it["qp"] = it["q"] * it["eg"] - qo[:, :HEAD_DIM]
            it["oc"] = qo[:, HEAD_DIM:]

    for g0 in range(0, len(items), group_items):
        within_chunk(items[g0:g0 + group_items])

    state = {(b, h): s_scr[b, h] for b in range(nb) for h in range(GDN_HEADS)}
    for c in range(n_chunks):
        for it in items:
            if it["c"] != c:
                continue
            s = state[(it["b"], it["h"])]
            r = _dot(jnp.concatenate([it["qp"], it["mp"]], axis=0), s)
            it["o"] = r[:chunk] + it["oc"]
            state[(it["b"], it["h"])] = s * it["egl"] - r[chunk:] + it["nc"]
    for (b, h), s in state.items():
        s_scr[b, h] = s

    for b in range(nb):
        zf = padrows(z_ref[b])
        for h in range(GDN_HEADS):
            outs = [it["o"] for it in items if it["b"] == b and it["h"] == h]
            o = outs[0] if n_chunks == 1 else jnp.concatenate(outs, axis=0)
            o = (o * lax.rsqrt(jnp.mean(o * o, axis=-1, keepdims=True) + EPS) * nw
                 * _silu(zf[:, h * HEAD_DIM:(h + 1) * HEAD_DIM]))
            o_ref[b, :, h * HEAD_DIM:(h + 1) * HEAD_DIM] = o[:tm]

    @pl.when(ti == nt - 1)
    def _():
        sn_ref[...] = s_scr[...]


def _gdn(proj, state_conv, s0, conv_w, alog_row, dtb_row, norm_w, nb, tm):
    bsz, t, _ = proj.shape
    tmp = max(tm, SUBLANES)
    chunk = min(GDN_CHUNK, tmp)
    row = pl.BlockSpec((1, LANES), lambda b, i: (0, 0))
    s_all, layer = s0
    sspec = pl.BlockSpec((nb, GDN_HEADS, HEAD_DIM, HEAD_DIM), lambda b, i: (b, 0, 0, 0))
    s_in = pl.BlockSpec((None, nb, GDN_HEADS, HEAD_DIM, HEAD_DIM), lambda b, i: (layer, b, 0, 0, 0))
    n_items = nb * GDN_HEADS * (tmp // chunk)
    kern = functools.partial(_gdn_kernel, nb=nb, tm=tm, tmp=tmp, chunk=chunk,
                             group_items=n_items // GDN_ITEM_GROUPS)
    return pl.pallas_call(
        kern,
        grid=(bsz // nb, t // tm),
        in_specs=[pl.BlockSpec((nb, tm, CONV_CH), lambda b, i: (b, i, OFF_QKV // CONV_CH)),
                  pl.BlockSpec((nb, tm, GDN_WIDTH), lambda b, i: (b, i, OFF_Z // GDN_WIDTH)),
                  pl.BlockSpec((nb, tm, LANES), lambda b, i: (b, i, OFF_AB // LANES)),
                  pl.BlockSpec((nb, CONV_WIDTH - 1, CONV_CH), lambda b, i: (b, 0, 0)),
                  pl.BlockSpec((CONV_WIDTH, CONV_CH), lambda b, i: (0, 0)),
                  s_in, row, row, row],
        out_specs=[pl.BlockSpec((nb, tm, GDN_WIDTH), lambda b, i: (b, i, 0)), sspec],
        out_shape=[jax.ShapeDtypeStruct((bsz, t, GDN_WIDTH), F32),
                   jax.ShapeDtypeStruct(s_all.shape[1:], F32)],
        scratch_shapes=[pltpu.VMEM((nb, GDN_HEADS, HEAD_DIM, HEAD_DIM), F32),
                        pltpu.VMEM((nb, SUBLANES, CONV_CH), F32)],
        compiler_params=_cparams(("arbitrary", "arbitrary")),
        name="gdn",
    )(proj, proj, proj, state_conv, conv_w, s_all, alog_row, dtb_row, norm_w)


def _pool_kernel(u_ref, st_ref, w_ref, sc_ref, o_ref, prev_scr, *, nb, tm, tmp, start_pos):
    ti = pl.program_id(1)
    halo = 2 * SUBLANES

    @pl.when(ti == 0)
    def _():
        for b in range(nb):
            prev_scr[b] = jnp.concatenate([jnp.zeros((halo - POOL_BUF, POOL_WIDTH), F32), st_ref[b]], axis=0)

    lane = lax.broadcasted_iota(jnp.int32, (tmp, POOL_WIDTH), 1)
    rowi = lax.broadcasted_iota(jnp.int32, (tmp, POOL_WIDTH), 0)
    pos1 = start_pos + ti * tm + rowi + 1
    grp = lane // POOL_GROUP
    win = jnp.where(grp == 0, POOL_WINDOWS[0],
                    jnp.where(grp == 1, POOL_WINDOWS[1], jnp.where(grp == 2, POOL_WINDOWS[2], POOL_WINDOWS[3])))
    cnt = jnp.minimum(pos1, win).astype(F32)
    for b in range(nb):
        x = u_ref[b]
        if tm != tmp:
            x = jnp.concatenate([x, jnp.zeros((tmp - tm, POOL_WIDTH), F32)], axis=0)
        e = jnp.concatenate([prev_scr[b], x], axis=0)
        sums = []
        s = e
        for w in POOL_WINDOWS:
            s = s + pltpu.roll(s, w // 2, axis=0)
            sums.append(s[halo:])
        sel = jnp.where(grp == 0, sums[0], jnp.where(grp == 1, sums[1], jnp.where(grp == 2, sums[2], sums[3])))
        pooled = sel / cnt - x
        out = _dot(pooled, w_ref[...]) * sc_ref[...]
        o_ref[b] = out[:tm]
        if tm >= halo:
            prev_scr[b] = x[tm - halo:tm]


def _pool(proj, state_pool, w_bd, pool_scale, nb, tm, start_pos):
    bsz, t, _ = proj.shape
    tmp = max(tm, SUBLANES)
    kern = functools.partial(_pool_kernel, nb=nb, tm=tm, tmp=tmp, start_pos=start_pos)
    return pl.pallas_call(
        kern,
        grid=(bsz // nb, t // tm),
        in_specs=[pl.BlockSpec((nb, tm, POOL_WIDTH), lambda b, i: (b, i, OFF_U // POOL_WIDTH)),
                  pl.BlockSpec((nb, POOL_BUF, POOL_WIDTH), lambda b, i: (b, 0, 0)),
                  pl.BlockSpec((POOL_WIDTH, POOL_WIDTH), lambda b, i: (0, 0)),
                  pl.BlockSpec((1, POOL_WIDTH), lambda b, i: (0, 0))],
        out_specs=pl.BlockSpec((nb, tm, POOL_WIDTH), lambda b, i: (b, i, 0)),
        out_shape=jax.ShapeDtypeStruct((bsz, t, POOL_WIDTH), F32),
        scratch_shapes=[pltpu.VMEM((nb, 2 * SUBLANES, POOL_WIDTH), F32)],
        compiler_params=_cparams(("arbitrary", "arbitrary")),
        name="pool",
    )(proj, state_pool, w_bd, pool_scale.reshape(1, POOL_WIDTH))


def _out_kernel(x_ref, po_ref, og_ref, gt_ref, sc_ref, sh_ref, gpm_ref, gpf_ref, wo_ref, rw_ref, rb_ref, c0_ref,
                x1_ref, ht_ref, g_ref, r_ref, cnt_ref, carry, *, tm):
    @pl.when((pl.program_id(0) == 0) & (pl.program_id(1) == 0))
    def _():
        carry[...] = c0_ref[...]

    mix = _dot(po_ref[0], wo_ref[:POOL_WIDTH]) + _dot(og_ref[0], wo_ref[POOL_WIDTH:])
    x1 = x_ref[0] + gt_ref[0] * _rms(mix, gpm_ref[...])
    x1_ref[0] = x1
    h = _rms(x1, gpf_ref[...]) * (1.0 + sc_ref[0]) + sh_ref[0]
    for s in range(ROW_TILES):
        ht_ref[pl.ds(s, tm, stride=ROW_TILES), :] = h[:, s * LANES:(s + 1) * LANES]
    logits = _dot_nt(rw_ref[...], h) + rb_ref[...]
    eid = lax.broadcasted_iota(jnp.int32, logits.shape, 0).astype(F32)
    work = logits
    sel = jnp.zeros(logits.shape, jnp.bool_)
    top = None
    den = jnp.zeros((1, tm), F32)
    for j in range(TOP_K):
        m = jnp.max(work, axis=0, keepdims=True)
        idx = jnp.min(jnp.where(work == m, eid, float(N_EXPERTS)), axis=0, keepdims=True)
        hit = eid == idx
        sel = sel | hit
        work = jnp.where(hit, -jnp.inf, work)
        if j == 0:
            top = m
        den = den + jnp.exp(m - top)
    gates = jnp.exp(logits - top) / den
    g_ref[...] = jnp.transpose(jnp.where(sel, gates, -1.0))
    mask = sel.astype(F32)
    ri = lax.broadcasted_iota(jnp.int32, (tm, tm), 0)
    ci = lax.broadcasted_iota(jnp.int32, (tm, tm), 1)
    c = carry[...]
    r_ref[...] = jnp.transpose(_dot(mask, (ri < ci).astype(F32)) + c)
    c = c + jnp.sum(mask, axis=1, keepdims=True)
    carry[...] = c
    cnt_ref[...] = c


def _out_proj(x, pool_out, o_gdn, gt, sc, sh, g_post_mix, g_pre_ffn, w_out_b, router_w, router_b, counts0,
              tm, per_token):
    bsz, t, _ = x.shape
    n = bsz * t
    tiles = t // tm

    def tok(width):
        return pl.BlockSpec((1, tm, width), lambda b, i: (b, i, 0))

    def const(shape):
        return pl.BlockSpec(shape, lambda b, i: tuple(0 for _ in shape))

    kern = functools.partial(_out_kernel, tm=tm)
    return pl.pallas_call(
        kern,
        grid=(bsz, tiles),
        in_specs=[tok(D_MODEL), tok(POOL_WIDTH), tok(GDN_WIDTH),
                  _mod_spec(per_token, tm), _mod_spec(per_token, tm), _mod_spec(per_token, tm),
                  const((1, D_MODEL)), const((1, D_MODEL)), const((D_MODEL, D_MODEL)),
                  const((N_EXPERTS, D_MODEL)), const((N_EXPERTS, 1)), const((N_EXPERTS, 1))],
        out_specs=[tok(D_MODEL),
                   pl.BlockSpec((tm * ROW_TILES, LANES), lambda b, i: (b * tiles + i, 0)),
                   pl.BlockSpec((tm, N_EXPERTS), lambda b, i: (b * tiles + i, 0)),
                   pl.BlockSpec((tm, N_EXPERTS), lambda b, i: (b * tiles + i, 0)),
                   const((N_EXPERTS, 1))],
        out_shape=[jax.ShapeDtypeStruct((bsz, t, D_MODEL), F32),
                   jax.ShapeDtypeStruct((n * ROW_TILES, LANES), F32),
                   jax.ShapeDtypeStruct((n, N_EXPERTS), F32),
                   jax.ShapeDtypeStruct((n, N_EXPERTS), F32),
                   jax.ShapeDtypeStruct((N_EXPERTS, 1), F32)],
        scratch_shapes=[pltpu.VMEM((N_EXPERTS, 1), F32)],
        compiler_params=_cparams(("arbitrary", "arbitrary")),
        name="out_proj",
    )(x, pool_out, o_gdn, gt, sc, sh, g_post_mix.reshape(1, D_MODEL), g_pre_ffn.reshape(1, D_MODEL),
      w_out_b, jnp.swapaxes(router_w, 0, 1), router_b.reshape(N_EXPERTS, 1), counts0)


def _dest_kernel(g_ref, r_ref, start_ref, base_ref, l_ref, gt_ref, *, tm, sub):
    g = g_ref[...]
    mask = g >= 0.0
    maskf = mask.astype(F32)
    dest = start_ref[...] + r_ref[...]
    base = jnp.concatenate([jnp.broadcast_to(base_ref[0, q:q + 1, :], (sub, N_EXPERTS)) for q in range(tm // sub)],
                           axis=0)
    local = dest - base
    ri = lax.broadcasted_iota(jnp.int32, (N_EXPERTS, N_EXPERTS), 0)
    ci = lax.broadcasted_iota(jnp.int32, (N_EXPERTS, N_EXPERTS), 1)
    slot = _dot(maskf, (ri < ci).astype(F32))
    lane4 = lax.broadcasted_iota(jnp.int32, (tm, TOP_K), 1)
    l_out = jnp.zeros((tm, TOP_K), F32)
    g_out = jnp.zeros((tm, TOP_K), F32)
    for j in range(TOP_K):
        pick = mask & (slot == float(j))
        lj = jnp.sum(jnp.where(pick, local, 0.0), axis=-1, keepdims=True)
        gj = jnp.sum(jnp.where(pick, g, 0.0), axis=-1, keepdims=True)
        l_out = jnp.where(lane4 == j, lj, l_out)
        g_out = jnp.where(lane4 == j, gj, g_out)
    l_ref[...] = l_out.astype(jnp.int32)
    gt_ref[...] = g_out


def _dest(gsel, rank, start_row, base, tm, sub):
    n = gsel.shape[0]
    per = tm // sub
    kern = functools.partial(_dest_kernel, tm=tm, sub=sub)
    tok4 = pl.BlockSpec((tm, TOP_K), lambda i: (i, 0))
    return pl.pallas_call(
        kern,
        grid=(n // tm,),
        in_specs=[pl.BlockSpec((tm, N_EXPERTS), lambda i: (i, 0)),
                  pl.BlockSpec((tm, N_EXPERTS), lambda i: (i, 0)),
                  pl.BlockSpec((1, N_EXPERTS), lambda i: (0, 0)),
                  pl.BlockSpec((1, per, N_EXPERTS), lambda i: (i, 0, 0))],
        out_specs=[tok4, tok4],
        out_shape=[jax.ShapeDtypeStruct((n, TOP_K), jnp.int32),
                   jax.ShapeDtypeStruct((n, TOP_K), F32)],
        compiler_params=_cparams(("arbitrary",)),
        name="dest",
    )(gsel, rank, start_row, base.reshape(n // tm, per, N_EXPERTS))


def _row_copy(src, dst, s_row, d_row, sem):
    return pltpu.make_async_copy(src.at[pl.ds(pl.multiple_of(s_row * ROW_TILES, ROW_TILES), ROW_TILES), :],
                                 dst.at[pl.ds(pl.multiple_of(d_row * ROW_TILES, ROW_TILES), ROW_TILES), :], sem)


def _scatter_kernel(cnt_ref, start_ref, end_ref, nu_ref, loc_ref, tab_ref, ha_ref, hb_ref, xs_ref,
                    st_ref, ssem, sem, *, tm, bm, n_blocks, tiles_a, n_tiles):
    i = pl.program_id(0)
    slot = i % 2
    sizes = [1 << b for b in range(tm.bit_length() - 1, -1, -1)]

    def wait_sent(s):
        pltpu.make_async_copy(st_ref.at[s], xs_ref.at[pl.ds(0, tm * TOP_K * ROW_TILES), :], ssem.at[s]).wait()

    @pl.when(i >= 2)
    def _():
        wait_sent(slot)

    def group_from(ht_ref):
        def body(t, carry):
            row = ht_ref[pl.ds(pl.multiple_of(t * ROW_TILES, ROW_TILES), ROW_TILES), :]
            for k in range(TOP_K):
                at = pl.multiple_of(loc_ref[t * TOP_K + k] * ROW_TILES, ROW_TILES)
                st_ref[slot, pl.ds(at, ROW_TILES), :] = row
            return carry

        lax.fori_loop(0, tm, body, 0)

    @pl.when(i < tiles_a)
    def _():
        group_from(ha_ref)

    @pl.when(i >= tiles_a)
    def _():
        group_from(hb_ref)

    def per_expert(e, off):
        row = tab_ref[e]
        cnt = tab_ref[N_EXPERTS + e]
        for size in sizes:
            take = (cnt & size) != 0

            @pl.when(take)
            def _(row=row, off=off, size=size):
                pltpu.make_async_copy(
                    st_ref.at[slot, pl.ds(pl.multiple_of(off * ROW_TILES, ROW_TILES), size * ROW_TILES), :],
                    xs_ref.at[pl.ds(pl.multiple_of(row * ROW_TILES, ROW_TILES), size * ROW_TILES), :],
                    ssem.at[slot]).start()

            step = jnp.where(take, size, 0)
            row = row + step
            off = off + step
        return off

    lax.fori_loop(0, N_EXPERTS, per_expert, 0)

    ht_ref = hb_ref

    def tile_copy(u):
        return pltpu.make_async_copy(
            ht_ref, xs_ref.at[pl.ds(pl.multiple_of(u * (tm * ROW_TILES), tm * ROW_TILES), tm * ROW_TILES), :], sem)

    @pl.when(i == n_tiles - 1)
    def _():
        wait_sent(slot)
        if n_tiles > 1:
            wait_sent(1 - slot)

    @pl.when(i == pl.num_programs(0) - 1)
    def _():
        per = bm // tm

        def tail_start(u, carry):
            tile_copy(u).start()
            return carry

        def tail_wait(u, carry):
            tile_copy(u).wait()
            return carry

        lax.fori_loop(nu_ref[0] * per, n_blocks * per, tail_start, 0)
        lax.fori_loop(nu_ref[0] * per, n_blocks * per, tail_wait, 0)

        chunk = bm // 4
        assert chunk <= tm

        def chunk_copy(q):
            return pltpu.make_async_copy(
                ht_ref.at[pl.ds(0, chunk * ROW_TILES), :],
                xs_ref.at[pl.ds(pl.multiple_of(q * (chunk * ROW_TILES), chunk * ROW_TILES), chunk * ROW_TILES), :],
                sem)

        def per_expert(e, carry):
            lo = start_ref[e] + cnt_ref[e]
            mid = (lo + chunk - 1) // chunk * chunk
            hi = end_ref[e]

            def fill(r, c2):
                _row_copy(ht_ref, xs_ref, 0, r, sem).start()
                return c2

            def fill_chunk(q, c2):
                chunk_copy(q).start()
                return c2

            lax.fori_loop(lo, mid, fill, 0)
            lax.fori_loop(mid // chunk, hi // chunk, fill_chunk, 0)

            def drain(r, c2):
                _row_copy(ht_ref, xs_ref, 0, r, sem).wait()
                return c2

            def drain_chunk(q, c2):
                chunk_copy(q).wait()
                return c2

            lax.fori_loop(lo, mid, drain, 0)
            lax.fori_loop(mid // chunk, hi // chunk, drain_chunk, 0)
            return carry

        lax.fori_loop(0, N_EXPERTS, per_expert, 0)


def _scatter_rows(ht_a, ht_b, local_flat, run_table, cnt, start, end, n_used, n_blocks, tm, bm):
    rows = n_blocks * bm
    tiles_a, tiles_b = ht_a.shape[0] // (tm * ROW_TILES), ht_b.shape[0] // (tm * ROW_TILES)
    assert bm % tm == 0 and ht_a.shape[0] % (tm * ROW_TILES) == 0 and ht_b.shape[0] % (tm * ROW_TILES) == 0
    assert tiles_b >= 1
    kern = functools.partial(_scatter_kernel, tm=tm, bm=bm, n_blocks=n_blocks, tiles_a=tiles_a,
                             n_tiles=tiles_a + tiles_b)
    smem = pl.BlockSpec((tm * TOP_K,), lambda i, *_: (i,), memory_space=pltpu.SMEM)
    return pl.pallas_call(
        kern,
        grid_spec=pltpu.PrefetchScalarGridSpec(
            num_scalar_prefetch=4,
            grid=(tiles_a + tiles_b,),
            in_specs=[smem, smem,
                      pl.BlockSpec((tm * ROW_TILES, LANES), lambda i, *_: (jnp.minimum(i, tiles_a - 1), 0)),
                      pl.BlockSpec((tm * ROW_TILES, LANES), lambda i, *_: (jnp.maximum(i - tiles_a, 0), 0))],
            out_specs=pl.BlockSpec(memory_space=pl.ANY),
            scratch_shapes=[pltpu.VMEM((2, tm * TOP_K * ROW_TILES, LANES), F32),
                            pltpu.SemaphoreType.DMA((2,)),
                            pltpu.SemaphoreType.DMA(())]),
        out_shape=jax.ShapeDtypeStruct((rows * ROW_TILES, LANES), F32),
        compiler_params=_cparams(("arbitrary",)),
        name="scatter_rows",
    )(cnt, start, end, n_used, local_flat, run_table, ht_a, ht_b)


def _expert_kernel(be_ref, nx_ref, nv_ref, nu_ref, xs_ref, wg_ref, bg_ref, wd_ref, bd_ref, ys_ref,
                   wg_f, wd_f, wg_s, wd_s, wsem, *, bm):
    j = pl.program_id(0)
    prev = be_ref[jnp.maximum(j - 1, 0)]
    fresh = (j == 0) | (be_ref[j] != prev)
    active = j < nu_ref[0]

    def weight_copies(e):
        return (pltpu.make_async_copy(wg_ref.at[e], wg_f, wsem.at[0]),
                pltpu.make_async_copy(wd_ref.at[e], wd_f, wsem.at[1]))

    @pl.when(j == 0)
    def _():
        for c in weight_copies(be_ref[0]):
            c.start()

    @pl.when(active & fresh)
    def _():
        for c in weight_copies(be_ref[j]):
            c.wait()
        wg_s[...] = wg_f[...].astype(BF16)
        wd_s[...] = wd_f[...].astype(BF16)

        @pl.when(nx_ref[j] != be_ref[j])
        def _():
            for c in weight_copies(nx_ref[j]):
                c.start()

    def run_rows(m):
        x = jnp.concatenate([xs_ref[pl.ds(s, m, stride=ROW_TILES), :] for s in range(ROW_TILES)], axis=-1)
        xb = x.astype(BF16)
        acts = []
        for c in range(EXPERT_COL_CHUNKS):
            w = D_FF // EXPERT_COL_CHUNKS
            lo = c * w
            gate = jnp.dot(xb, wg_s[:, lo:lo + w], preferred_element_type=F32) + bg_ref[0, :, lo:lo + w]
            up = (jnp.dot(xb, wg_s[:, D_FF + lo:D_FF + lo + w], preferred_element_type=F32)
                  + bg_ref[0, :, D_FF + lo:D_FF + lo + w])
            gate = jnp.minimum(gate, SWIGLU_LIMIT)
            up = jnp.clip(up, -SWIGLU_LIMIT, SWIGLU_LIMIT)
            acts.append(((up + 1.0) * (gate * jax.nn.sigmoid(SWIGLU_ALPHA * gate))).astype(BF16))
        act = jnp.concatenate(acts, axis=-1)
        for c in range(EXPERT_COL_CHUNKS):
            w = D_MODEL // EXPERT_COL_CHUNKS
            lo = c * w
            y = jnp.dot(act, wd_s[:, lo:lo + w], preferred_element_type=F32) + bd_ref[0, :, lo:lo + w]
            for s in range(w // LANES):
                ys_ref[pl.ds(lo // LANES + s, m, stride=ROW_TILES), :] = y[:, s * LANES:(s + 1) * LANES]
        if m < bm:
            ys_ref[pl.ds(m * ROW_TILES, (bm - m) * ROW_TILES), :] = jnp.zeros(((bm - m) * ROW_TILES, LANES), F32)

    quarter = bm // 4
    quarters = (nv_ref[j] + quarter - 1) // quarter
    for nq in range(1, 5):
        @pl.when(active & ((quarters == nq) if nq > 1 else (quarters <= 1)))
        def _(nq=nq):
            run_rows(nq * quarter)

    @pl.when(jnp.logical_not(active))
    def _():
        ys_ref[...] = jnp.zeros_like(ys_ref)


def _experts(xs, block_expert, next_expert, block_valid, n_used, w_gu, b_gu, w_down, b_down, bm):
    rows = xs.shape[0] // ROW_TILES
    n_blocks = rows // bm

    def blk(j, be, nx, nv, nu):
        return (jnp.minimum(j, nu[0] - 1), 0)

    def bsel(j, be, nx, nv, nu):
        return (be[j], 0, 0)

    hbm = pl.BlockSpec(memory_space=pl.ANY)
    kern = functools.partial(_expert_kernel, bm=bm)
    return pl.pallas_call(
        kern,
        grid_spec=pltpu.PrefetchScalarGridSpec(
            num_scalar_prefetch=4,
            grid=(n_blocks,),
            in_specs=[pl.BlockSpec((bm * ROW_TILES, LANES), blk), hbm,
                      pl.BlockSpec((1, 1, 2 * D_FF), bsel), hbm,
                      pl.BlockSpec((1, 1, D_MODEL), bsel)],
            out_specs=pl.BlockSpec((bm * ROW_TILES, LANES), lambda j, be, nx, nv, nu: (j, 0)),
            scratch_shapes=[pltpu.VMEM((D_MODEL, 2 * D_FF), F32),
                            pltpu.VMEM((D_FF, D_MODEL), F32),
                            pltpu.VMEM((D_MODEL, 2 * D_FF), BF16),
                            pltpu.VMEM((D_FF, D_MODEL), BF16),
                            pltpu.SemaphoreType.DMA((2,))]),
        out_shape=jax.ShapeDtypeStruct((rows * ROW_TILES, LANES), F32),
        compiler_params=_cparams(("arbitrary",)),
        name="experts",
    )(block_expert, next_expert, block_valid, n_used, xs, w_gu, b_gu.reshape(N_EXPERTS, 1, 2 * D_FF), w_down,
      b_down.reshape(N_EXPERTS, 1, D_MODEL))


def _final_kernel(loc_ref, gate_ref, tcur_ref, tnext_ref, ys_ref, x1_ref, gt_ref, g_ref, o_ref,
                  st_ref, f_ref, sem, *, tm, steps):
    g = pl.program_id(0) * pl.num_programs(1) + pl.program_id(1)
    cur = g % 2
    sizes = [1 << b for b in range(tm.bit_length() - 1, -1, -1)]

    def request(t_ref, slot):
        def per_expert(e, off):
            row = t_ref[e]
            cnt = t_ref[N_EXPERTS + e]
            for size in sizes:
                take = (cnt & size) != 0

                @pl.when(take)
                def _(row=row, off=off, size=size):
                    pltpu.make_async_copy(
                        ys_ref.at[pl.ds(pl.multiple_of(row * ROW_TILES, ROW_TILES), size * ROW_TILES), :],
                        st_ref.at[slot, pl.ds(pl.multiple_of(off * ROW_TILES, ROW_TILES), size * ROW_TILES), :],
                        sem.at[slot]).start()

                step = jnp.where(take, size, 0)
                row = row + step
                off = off + step
            return off

        lax.fori_loop(0, N_EXPERTS, per_expert, 0)

    @pl.when(g == 0)
    def _():
        request(tcur_ref, 0)

    @pl.when(g + 1 < steps)
    def _():
        request(tnext_ref, 1 - cur)

    pltpu.make_async_copy(ys_ref.at[pl.ds(0, tm * TOP_K * ROW_TILES), :], st_ref.at[cur], sem.at[cur]).wait()

    def combine(t, carry):
        acc = None
        for k in range(TOP_K):
            j = t * TOP_K + k
            row = st_ref[cur, pl.ds(pl.multiple_of(loc_ref[j] * ROW_TILES, ROW_TILES), ROW_TILES), :]
            part = row * gate_ref[j]
            acc = part if acc is None else acc + part
        f_ref[pl.ds(pl.multiple_of(t * ROW_TILES, ROW_TILES), ROW_TILES), :] = acc
        return carry

    lax.fori_loop(0, tm, combine, 0)
    f = jnp.concatenate([f_ref[pl.ds(s, tm, stride=ROW_TILES), :] for s in range(ROW_TILES)], axis=-1)
    o_ref[0] = x1_ref[0] + gt_ref[0] * _rms(f, g_ref[...])


def _final(ys, local_flat, gates_flat, run_table, x1, gt, g_post_ffn, tok_off, tm, per_token):
    bsz, t, _ = x1.shape
    tiles = t // tm
    off = tok_off // tm
    steps = bsz * tiles
    kern = functools.partial(_final_kernel, tm=tm, steps=steps)

    def smem(ahead):
        return pl.BlockSpec((tm * TOP_K,), lambda b, i: (off + jnp.minimum(b * tiles + i + ahead, steps - 1),),
                            memory_space=pltpu.SMEM)

    return pl.pallas_call(
        kern,
        grid=(bsz, tiles),
        in_specs=[smem(0), smem(0), smem(0), smem(1),
                  pl.BlockSpec(memory_space=pl.ANY),
                  pl.BlockSpec((1, tm, D_MODEL), lambda b, i: (b, i, 0)),
                  _mod_spec(per_token, tm),
                  pl.BlockSpec((1, D_MODEL), lambda b, i: (0, 0))],
        out_specs=pl.BlockSpec((1, tm, D_MODEL), lambda b, i: (b, i, 0)),
        out_shape=jax.ShapeDtypeStruct((bsz, t, D_MODEL), F32),
        scratch_shapes=[pltpu.VMEM((2, tm * TOP_K * ROW_TILES, LANES), F32),
                        pltpu.VMEM((tm * ROW_TILES, LANES), F32),
                        pltpu.SemaphoreType.DMA((2,))],
        compiler_params=_cparams(("arbitrary", "arbitrary")),
        name="final",
    )(local_flat, gates_flat, run_table, run_table, ys, x1, gt, g_post_ffn.reshape(1, D_MODEL))


def _moe(ht_a, ht_b, gsel, rank, counts, w_gu, b_gu, w_down, b_down, tm, bm):
    n = gsel.shape[0]
    cnt = counts[:, 0].astype(jnp.int32)
    nblk = (cnt + bm - 1) // bm
    blk_end = jnp.cumsum(nblk)
    blk_start = blk_end - nblk
    n_blocks = -(-(n * TOP_K) // bm) + N_EXPERTS
    blocks = jnp.arange(n_blocks, dtype=jnp.int32)
    block_expert = jnp.minimum(jnp.sum((blk_end[None, :] <= blocks[:, None]).astype(jnp.int32), axis=1),
                               N_EXPERTS - 1).astype(jnp.int32)
    ids = jnp.arange(N_EXPERTS, dtype=jnp.int32)
    later = jnp.where((nblk[None, :] > 0) & (ids[None, :] > ids[:, None]), ids[None, :], N_EXPERTS)
    nearest = jnp.min(later, axis=1)
    next_used = jnp.where(nearest < N_EXPERTS, nearest, ids).astype(jnp.int32)
    n_used = blk_end[-1:].astype(jnp.int32)
    row_start = (blk_start * bm).astype(jnp.int32)
    row_end = (blk_end * bm).astype(jnp.int32)
    before = rank[::tm]
    run_len = jnp.concatenate([before[1:], counts.reshape(1, N_EXPERTS)], axis=0) - before
    run_row = row_start.astype(F32)[None, :] + before
    run_off = jnp.cumsum(run_len, axis=1) - run_len
    tiles = n // tm
    run_table = jnp.zeros((tiles, tm * TOP_K), jnp.int32)
    run_table = run_table.at[:, :N_EXPERTS].set(run_row.astype(jnp.int32))
    run_table = run_table.at[:, N_EXPERTS:2 * N_EXPERTS].set(run_len.astype(jnp.int32)).reshape(tiles * tm * TOP_K)
    local, gates = _dest(gsel, rank, row_start.astype(F32).reshape(1, N_EXPERTS), run_row - run_off,
                         DEST_TILE, tm)
    local = local.reshape(n * TOP_K)
    xs = _scatter_rows(ht_a, ht_b, local, run_table, cnt, row_start, row_end, n_used, n_blocks, tm, bm)
    owns = (blocks[:, None] >= blk_start[None, :]) & (blocks[:, None] < blk_end[None, :])

    def per_block(table):
        return jnp.sum(jnp.where(owns, table[None, :], 0), axis=1).astype(jnp.int32)

    block_valid = jnp.clip(per_block(cnt) - (blocks - per_block(blk_start)) * bm, 0, bm).astype(jnp.int32)
    ys = _experts(xs, block_expert, per_block(next_used), block_valid, n_used, w_gu, b_gu, w_down, b_down, bm)
    return ys, local, gates.reshape(n * TOP_K), run_table


def _mix_path(x, mod, pool_buf, conv_buf, s0, start_pos, counts0, lp, tm, nb, per_token):
    (g_pre_mix, g_post_mix, g_pre_ffn, w_in_p, w_bd, pool_scale, conv_w, alog_row, dtb_row, norm_w,
     w_out_b, router_w, router_b) = lp
    bsz, t, _ = x.shape
    sh_mix, sc_mix, gt_mix, sh_ffn, sc_ffn, gt_ffn = (mod[:, i] for i in range(6))

    def shape_mod(m):
        if per_token:
            return jnp.repeat(m, t, axis=0).reshape(1, bsz * t, D_MODEL)
        return m.reshape(bsz, 1, D_MODEL)

    sh_mix, sc_mix, gt_mix, sh_ffn, sc_ffn, gt_ffn = map(shape_mod, (sh_mix, sc_mix, gt_mix, sh_ffn, sc_ffn, gt_ffn))
    xf = x.reshape(1, bsz * t, D_MODEL) if per_token else x
    proj = _in_proj(xf, sc_mix, sh_mix, g_pre_mix, w_in_p, PROJ_TILE, per_token).reshape(bsz, t, PROJ_WIDTH)
    tseq = min(tm, t)
    o_gdn, s_new = _gdn(proj, conv_buf, s0, conv_w, alog_row, dtb_row, norm_w, nb, tseq)
    pool_out = _pool(proj, pool_buf, w_bd, pool_scale, nb * POOL_SEQS if per_token else nb, min(POOL_TILE, t),
                     start_pos)
    fl = (lambda a: a.reshape(1, bsz * t, a.shape[-1])) if per_token else (lambda a: a)
    x1, ht, gsel, rank, counts = _out_proj(xf, fl(pool_out), fl(o_gdn), gt_mix, sc_ffn, sh_ffn, g_post_mix,
                                           g_pre_ffn, w_out_b, router_w, router_b, counts0, OUT_TILE, per_token)

    def last_rows(buf, off, width):
        keep = buf.shape[1]
        if t >= keep:
            return proj[:, t - keep:, off:off + width]
        return jnp.concatenate([buf[:, t:], proj[:, :, off:off + width]], axis=1)

    pool_new = last_rows(pool_buf, OFF_U, POOL_WIDTH)
    conv_new = last_rows(conv_buf, OFF_QKV, CONV_CH)
    return x1, ht, gsel, rank, counts, gt_ffn, pool_new, conv_new, s_new


def kernel(x_prompt, x_sample, c_prompt, c_sample, state_pool, state_conv, state_ssm, w_ada, b_ada, g_pre_mix, g_post_mix, g_pre_ffn, g_post_ffn, w_in, w_pool, pool_scale, conv_w, a_log, dt_bias, gdn_norm_w, w_out, router_w, router_b, w_gu, b_gu, w_down, b_down):
    depth = w_ada.shape[0]
    bp, tp, _ = x_prompt.shape
    bs, ts, _ = x_sample.shape
    tm = TOKEN_TILE
    y_p, y_s = x_prompt, x_sample
    outs = [[] for _ in range(6)]
    for l in range(depth):
        mod = _ada(jnp.concatenate([c_prompt, c_sample], axis=0), w_ada[l], b_ada[l]).reshape(bp + bs, 6, D_MODEL)
        w_in_p = jnp.swapaxes(w_in[l], 0, 1)
        w_bd = jax.scipy.linalg.block_diag(*[w_pool[l, g] for g in range(len(POOL_WINDOWS))])
        pad_row = lambda v: jnp.pad(v, (0, LANES - v.shape[0])).reshape(1, LANES)
        lp = (g_pre_mix[l], g_post_mix[l], g_pre_ffn[l], w_in_p, w_bd, pool_scale[l], conv_w[l],
              pad_row(a_log[l]), pad_row(dt_bias[l]), gdn_norm_w[l].reshape(1, HEAD_DIM),
              w_out[l].astype(BF16), router_w[l], router_b[l])
        zero_pool = jnp.zeros((bp, POOL_BUF, POOL_WIDTH), F32)
        zero_conv = jnp.zeros((bp, CONV_WIDTH - 1, CONV_CH), F32)
        zero_ssm = (jnp.zeros((1, bp, GDN_HEADS, HEAD_DIM, HEAD_DIM), F32), 0)
        no_counts = jnp.zeros((N_EXPERTS, 1), F32)
        x1p, htp, gp, rp, cnt_p, gtp, pp, cp, sp = _mix_path(
            y_p, mod[:bp], zero_pool, zero_conv, zero_ssm, 0, no_counts, lp, tm, 1, False)
        x1s, hts, gs, rs, counts, gts, ps, cs, ss = _mix_path(
            y_s, mod[bp:], state_pool[l], state_conv[l], (state_ssm, l), PAST_LEN, cnt_p, lp, tm, 8, True)
        gsel = jnp.concatenate([gp, gs], axis=0)
        rank = jnp.concatenate([rp, rs], axis=0)
        ys, local, gates, runs = _moe(htp, hts, gsel, rank, counts, w_gu[l], b_gu[l], w_down[l], b_down[l],
                                      tm, MOE_BLOCK)
        y_p = _final(ys, local, gates, runs, x1p, gtp, g_post_ffn[l], 0, tm, False)
        y_s = _final(ys, local, gates, runs, x1s, gts, g_post_ffn[l], bp * tp, tm, True).reshape(bs, ts, D_MODEL)
        for lst, val in zip(outs, (pp, cp, sp, ps, cs, ss)):
            lst.append(val)
    return (y_p, y_s) + tuple(o[0][None] if depth == 1 else jnp.stack(o) for o in outs)
```

```python
import functools
import math

import jax
import jax.numpy as jnp
from jax import lax
from jax.experimental import pallas as pl
from jax.experimental.pallas import tpu as pltpu

F32 = jnp.float32
BF16 = jnp.bfloat16

D_MODEL = 1024
PAST_LEN = 16384
POOL_WINDOWS = (2, 4, 8, 16)
POOL_WIDTH = D_MODEL // 4
POOL_GROUP = POOL_WIDTH // len(POOL_WINDOWS)
POOL_BUF = max(POOL_WINDOWS) - 1
GDN_WIDTH = D_MODEL - POOL_WIDTH
HEAD_DIM = 128
GDN_HEADS = GDN_WIDTH // HEAD_DIM
CONV_WIDTH = 4
CONV_CH = 3 * GDN_WIDTH
GDN_CHUNK = 64
N_EXPERTS = 32
TOP_K = 4
D_FF = D_MODEL
SWIGLU_LIMIT = 7.0
SWIGLU_ALPHA = 1.702
EPS = 1e-6

SUBLANES = 8
LANES = 128
ROW_TILES = D_MODEL // LANES
OFF_QKV = 0
OFF_Z = OFF_QKV + CONV_CH
OFF_U = OFF_Z + GDN_WIDTH
OFF_AB = OFF_U + POOL_WIDTH
PROJ_WIDTH = OFF_AB + LANES
assert OFF_Z % GDN_WIDTH == 0 and OFF_U % POOL_WIDTH == 0 and OFF_AB % LANES == 0
TOKEN_TILE = 256
PROJ_TILE = 512
SCATTER_TILE = 512
OUT_TILE = 512
FINAL_TILE = 512
DEST_TILE = 1536
POOL_TILE = 1024
POOL_SEQS = 4
GDN_ITEM_GROUPS = 1
MOE_BLOCK = 1024
EXPERT_COL_CHUNKS = 2
DMA_QUEUES = 2
VMEM_LIMIT = 56 * 1024 * 1024


def _cparams(sem):
    return pltpu.CompilerParams(dimension_semantics=sem, vmem_limit_bytes=VMEM_LIMIT)


def _dot(a, b):
    return jnp.dot(a.astype(BF16), b.astype(BF16), preferred_element_type=F32)


def _dot_nt(a, b):
    return lax.dot_general(a.astype(BF16), b.astype(BF16), (((1,), (1,)), ((), ())),
                           preferred_element_type=F32)


def _silu(x):
    return x * jax.nn.sigmoid(x)


def _rms(x, g):
    return x * lax.rsqrt(jnp.mean(x * x, axis=-1, keepdims=True) + EPS) * g


def _ada_kernel(c_ref, w_ref, b_ref, o_ref):
    o_ref[...] = _dot(_silu(c_ref[...]), w_ref[...]) + b_ref[...]


def _ada(c, w_ada, b_ada):
    n = c.shape[0]
    cols = w_ada.shape[1]
    blk = D_MODEL
    return pl.pallas_call(
        _ada_kernel,
        grid=(cols // blk,),
        in_specs=[pl.BlockSpec((n, D_MODEL), lambda j: (0, 0)),
                  pl.BlockSpec((D_MODEL, blk), lambda j: (0, j)),
                  pl.BlockSpec((1, blk), lambda j: (0, j))],
        out_specs=pl.BlockSpec((n, blk), lambda j: (0, j)),
        out_shape=jax.ShapeDtypeStruct((n, cols), F32),
        compiler_params=_cparams(("arbitrary",)),
        name="ada",
    )(c, w_ada, b_ada.reshape(1, cols))


def _in_kernel(x_ref, sc_ref, sh_ref, g_ref, wt_ref, o_ref, wb):
    @pl.when((pl.program_id(0) == 0) & (pl.program_id(1) == 0))
    def _():
        o1, o2, o3 = POOL_WIDTH, POOL_WIDTH + CONV_CH, POOL_WIDTH + CONV_CH + GDN_WIDTH
        wb[OFF_QKV:OFF_QKV + CONV_CH] = wt_ref[o1:o2].astype(BF16)
        wb[OFF_Z:OFF_Z + GDN_WIDTH] = wt_ref[o2:o3].astype(BF16)
        wb[OFF_U:OFF_U + POOL_WIDTH] = wt_ref[:o1].astype(BF16)
        tail = wt_ref.shape[0] - o3
        wb[OFF_AB:] = jnp.concatenate(
            [wt_ref[o3:], jnp.zeros((LANES - tail, D_MODEL), F32)], axis=0).astype(BF16)

    h = _rms(x_ref[0], g_ref[...]) * (1.0 + sc_ref[0]) + sh_ref[0]
    o_ref[0] = _dot_nt(h, wb[...])


def _mod_spec(per_token, tm):
    if per_token:
        return pl.BlockSpec((1, tm, D_MODEL), lambda b, t: (0, t, 0))
    return pl.BlockSpec((1, 1, D_MODEL), lambda b, t: (b, 0, 0))


def _in_proj(x, sc, sh, g, w_in, tm, per_token):
    bsz, t, _ = x.shape
    return pl.pallas_call(
        _in_kernel,
        grid=(bsz, t // tm),
        in_specs=[pl.BlockSpec((1, tm, D_MODEL), lambda b, i: (b, i, 0)),
                  _mod_spec(per_token, tm), _mod_spec(per_token, tm),
                  pl.BlockSpec((1, D_MODEL), lambda b, i: (0, 0)),
                  pl.BlockSpec(w_in.shape, lambda b, i: (0, 0), pipeline_mode=pl.Buffered(1))],
        out_specs=pl.BlockSpec((1, tm, PROJ_WIDTH), lambda b, i: (b, i, 0)),
        out_shape=jax.ShapeDtypeStruct((bsz, t, PROJ_WIDTH), F32),
        scratch_shapes=[pltpu.VMEM((PROJ_WIDTH, D_MODEL), BF16)],
        compiler_params=_cparams(("arbitrary", "arbitrary")),
        name="in_proj",
    )(x, sc, sh, g.reshape(1, D_MODEL), w_in)


def _cumsum_rows(x):
    n = x.shape[0]
    row = lax.broadcasted_iota(jnp.int32, x.shape, 0)
    s = 1
    while s < n:
        x = x + jnp.where(row >= s, pltpu.roll(x, s, axis=0), 0.0)
        s *= 2
    return x


def _gdn_kernel(qkv_ref, z_ref, ab_ref, cs_ref, cw_ref, s0_ref, alog_ref, dtb_ref, nw_ref, o_ref, sn_ref,
                s_scr, prev_scr, *, nb, tm, tmp, chunk, group_items):
    ti = pl.program_id(1)
    nt = pl.num_programs(1)
    n_chunks = tmp // chunk
    n_sq = int(math.log2(chunk)) - 1

    @pl.when(ti == 0)
    def _():
        s_scr[...] = s0_ref[...]
        for b in range(nb):
            prev_scr[b] = jnp.concatenate(
                [jnp.zeros((SUBLANES - (CONV_WIDTH - 1), CONV_CH), F32), cs_ref[b]], axis=0)

    rowi = lax.broadcasted_iota(jnp.int32, (tmp, LANES), 0)
    ri = lax.broadcasted_iota(jnp.int32, (chunk, chunk), 0)
    ci = lax.broadcasted_iota(jnp.int32, (chunk, chunk), 1)
    tril = ri >= ci
    strict = ri > ci
    eye = (ri == ci).astype(F32)
    neg_a = -jnp.exp(alog_ref[...])
    dtb = dtb_ref[...]
    nw = nw_ref[...]

    def padrows(x):
        if tm == tmp:
            return x
        return jnp.concatenate([x, jnp.zeros((tmp - tm, x.shape[1]), F32)], axis=0)

    cw = cw_ref[...]
    valid = rowi < tm
    items = []
    for b in range(nb):
        x = padrows(qkv_ref[b])
        e = jnp.concatenate([prev_scr[b], x], axis=0)
        acc = x * cw[CONV_WIDTH - 1:CONV_WIDTH]
        for d in range(1, CONV_WIDTH):
            acc = acc + pltpu.roll(e, d, axis=0)[SUBLANES:] * cw[CONV_WIDTH - 1 - d:CONV_WIDTH - d]
        if tm >= SUBLANES:
            prev_scr[b] = x[tm - SUBLANES:tm]
        y = _silu(acc)
        ab = padrows(ab_ref[b])
        g_all = jnp.where(valid, neg_a * jax.nn.softplus(ab + dtb), 0.0)
        beta_all = jnp.where(valid, jax.nn.sigmoid(ab), 0.0)
        for h in range(GDN_HEADS):
            qc = y[:, h * HEAD_DIM:(h + 1) * HEAD_DIM]
            kc = y[:, GDN_WIDTH + h * HEAD_DIM:GDN_WIDTH + (h + 1) * HEAD_DIM]
            v = y[:, 2 * GDN_WIDTH + h * HEAD_DIM:2 * GDN_WIDTH + (h + 1) * HEAD_DIM]
            q = qc * lax.rsqrt(jnp.sum(qc * qc, axis=-1, keepdims=True) + EPS) * (HEAD_DIM ** -0.5)
            k = kc * lax.rsqrt(jnp.sum(kc * kc, axis=-1, keepdims=True) + EPS)
            gb = jnp.broadcast_to(g_all[:, h:h + 1], (tmp, LANES))
            bb = jnp.broadcast_to(beta_all[:, GDN_HEADS + h:GDN_HEADS + h + 1], (tmp, LANES))
            for c in range(n_chunks):
                sl = slice(c * chunk, (c + 1) * chunk)
                items.append(dict(b=b, h=h, c=c, q=q[sl], k=k[sl], v=v[sl], bb=bb[sl], gb=gb[sl]))

    def within_chunk(group):
        for it in group:
            gcb = _cumsum_rows(it["gb"])
            gct = jnp.transpose(gcb)[:chunk]
            diff = gcb[:, :chunk] - gct
            it["decay"] = jnp.where(tril, jnp.exp(jnp.where(tril, diff, 0.0)), 0.0)
            it["kb"] = it["k"] * it["bb"]
            it["eg"] = jnp.exp(gcb)
            gl = gcb[chunk - 1:chunk]
            it["egl"] = jnp.exp(gl)
            it["kdt"] = jnp.transpose(it["k"] * jnp.exp(gl - gcb))
        for it in group:
            kq = _dot_nt(jnp.concatenate([it["kb"], it["q"]], axis=0), it["k"])
            it["a"] = jnp.where(strict, kq[:chunk] * it["decay"], 0.0)
            it["qk"] = jnp.where(tril, kq[chunk:] * it["decay"], 0.0)
        for it in group:
            it["inv"] = eye - it["a"]
            it["p"] = _dot(it["a"], it["a"])
        for lvl in range(n_sq):
            for it in group:
                if lvl < n_sq - 1:
                    r = _dot(jnp.concatenate([it["inv"], it["p"]], axis=0), it["p"])
                    it["inv"] = it["inv"] + r[:chunk]
                    it["p"] = r[chunk:]
                else:
                    it["inv"] = it["inv"] + _dot(it["inv"], it["p"])
        for it in group:
            it["wu"] = _dot(it["inv"], jnp.concatenate([it["kb"] * it["eg"], it["v"] * it["bb"]], axis=1))
        for it in group:
            mn = _dot(it["kdt"], it["wu"])
            qo = _dot(it["qk"], it["wu"])
            it["mp"], it["nc"] = mn[:, :HEAD_DIM], mn[:, HEAD_DIM:]
            it["qp"] = it["q"] * it["eg"] - qo[:, :HEAD_DIM]
            it["oc"] = qo[:, HEAD_DIM:]

    for g0 in range(0, len(items), group_items):
        within_chunk(items[g0:g0 + group_items])

    state = {(b, h): s_scr[b, h] for b in range(nb) for h in range(GDN_HEADS)}
    for c in range(n_chunks):
        for it in items:
            if it["c"] != c:
                continue
            s = state[(it["b"], it["h"])]
            r = _dot(jnp.concatenate([it["qp"], it["mp"]], axis=0), s)
            it["o"] = r[:chunk] + it["oc"]
            state[(it["b"], it["h"])] = s * it["egl"] - r[chunk:] + it["nc"]
    for (b, h), s in state.items():
        s_scr[b, h] = s

    for b in range(nb):
        zf = padrows(z_ref[b])
        for h in range(GDN_HEADS):
            outs = [it["o"] for it in items if it["b"] == b and it["h"] == h]
            o = outs[0] if n_chunks == 1 else jnp.concatenate(outs, axis=0)
            o = (o * lax.rsqrt(jnp.mean(o * o, axis=-1, keepdims=True) + EPS) * nw
                 * _silu(zf[:, h * HEAD_DIM:(h + 1) * HEAD_DIM]))
            o_ref[b, :, h * HEAD_DIM:(h + 1) * HEAD_DIM] = o[:tm]

    @pl.when(ti == nt - 1)
    def _():
        sn_ref[...] = s_scr[...]


def _gdn(proj, state_conv, s0, conv_w, alog_row, dtb_row, norm_w, nb, tm):
    bsz, t, _ = proj.shape
    tmp = max(tm, SUBLANES)
    chunk = min(GDN_CHUNK, tmp)
    row = pl.BlockSpec((1, LANES), lambda b, i: (0, 0))
    s_all, layer = s0
    sspec = pl.BlockSpec((nb, GDN_HEADS, HEAD_DIM, HEAD_DIM), lambda b, i: (b, 0, 0, 0))
    s_in = pl.BlockSpec((None, nb, GDN_HEADS, HEAD_DIM, HEAD_DIM), lambda b, i: (layer, b, 0, 0, 0))
    n_items = nb * GDN_HEADS * (tmp // chunk)
    kern = functools.partial(_gdn_kernel, nb=nb, tm=tm, tmp=tmp, chunk=chunk,
                             group_items=n_items // GDN_ITEM_GROUPS)
    return pl.pallas_call(
        kern,
        grid=(bsz // nb, t // tm),
        in_specs=[pl.BlockSpec((nb, tm, CONV_CH), lambda b, i: (b, i, OFF_QKV // CONV_CH)),
                  pl.BlockSpec((nb, tm, GDN_WIDTH), lambda b, i: (b, i, OFF_Z // GDN_WIDTH)),
                  pl.BlockSpec((nb, tm, LANES), lambda b, i: (b, i, OFF_AB // LANES)),
                  pl.BlockSpec((nb, CONV_WIDTH - 1, CONV_CH), lambda b, i: (b, 0, 0)),
                  pl.BlockSpec((CONV_WIDTH, CONV_CH), lambda b, i: (0, 0)),
                  s_in, row, row, row],
        out_specs=[pl.BlockSpec((nb, tm, GDN_WIDTH), lambda b, i: (b, i, 0)), sspec],
        out_shape=[jax.ShapeDtypeStruct((bsz, t, GDN_WIDTH), F32),
                   jax.ShapeDtypeStruct(s_all.shape[1:], F32)],
        scratch_shapes=[pltpu.VMEM((nb, GDN_HEADS, HEAD_DIM, HEAD_DIM), F32),
                        pltpu.VMEM((nb, SUBLANES, CONV_CH), F32)],
        compiler_params=_cparams(("arbitrary", "arbitrary")),
        name="gdn",
    )(proj, proj, proj, state_conv, conv_w, s_all, alog_row, dtb_row, norm_w)


def _pool_kernel(u_ref, st_ref, w_ref, sc_ref, o_ref, prev_scr, *, nb, tm, tmp, start_pos):
    ti = pl.program_id(1)
    halo = 2 * SUBLANES

    @pl.when(ti == 0)
    def _():
        for b in range(nb):
            prev_scr[b] = jnp.concatenate([jnp.zeros((halo - POOL_BUF, POOL_WIDTH), F32), st_ref[b]], axis=0)

    lane = lax.broadcasted_iota(jnp.int32, (tmp, POOL_WIDTH), 1)
    rowi = lax.broadcasted_iota(jnp.int32, (tmp, POOL_WIDTH), 0)
    pos1 = start_pos + ti * tm + rowi + 1
    grp = lane // POOL_GROUP
    win = jnp.where(grp == 0, POOL_WINDOWS[0],
                    jnp.where(grp == 1, POOL_WINDOWS[1], jnp.where(grp == 2, POOL_WINDOWS[2], POOL_WINDOWS[3])))
    cnt = jnp.minimum(pos1, win).astype(F32)
    for b in range(nb):
        x = u_ref[b]
        if tm != tmp:
            x = jnp.concatenate([x, jnp.zeros((tmp - tm, POOL_WIDTH), F32)], axis=0)
        e = jnp.concatenate([prev_scr[b], x], axis=0)
        sums = []
        s = e
        for w in POOL_WINDOWS:
            s = s + pltpu.roll(s, w // 2, axis=0)
            sums.append(s[halo:])
        sel = jnp.where(grp == 0, sums[0], jnp.where(grp == 1, sums[1], jnp.where(grp == 2, sums[2], sums[3])))
        pooled = sel / cnt - x
        out = _dot(pooled, w_ref[...]) * sc_ref[...]
        o_ref[b] = out[:tm]
        if tm >= halo:
            prev_scr[b] = x[tm - halo:tm]


def _pool(proj, state_pool, w_bd, pool_scale, nb, tm, start_pos):
    bsz, t, _ = proj.shape
    tmp = max(tm, SUBLANES)
    kern = functools.partial(_pool_kernel, nb=nb, tm=tm, tmp=tmp, start_pos=start_pos)
    return pl.pallas_call(
        kern,
        grid=(bsz // nb, t // tm),
        in_specs=[pl.BlockSpec((nb, tm, POOL_WIDTH), lambda b, i: (b, i, OFF_U // POOL_WIDTH)),
                  pl.BlockSpec((nb, POOL_BUF, POOL_WIDTH), lambda b, i: (b, 0, 0)),
                  pl.BlockSpec((POOL_WIDTH, POOL_WIDTH), lambda b, i: (0, 0)),
                  pl.BlockSpec((1, POOL_WIDTH), lambda b, i: (0, 0))],
        out_specs=pl.BlockSpec((nb, tm, POOL_WIDTH), lambda b, i: (b, i, 0)),
        out_shape=jax.ShapeDtypeStruct((bsz, t, POOL_WIDTH), F32),
        scratch_shapes=[pltpu.VMEM((nb, 2 * SUBLANES, POOL_WIDTH), F32)],
        compiler_params=_cparams(("arbitrary", "arbitrary")),
        name="pool",
    )(proj, state_pool, w_bd, pool_scale.reshape(1, POOL_WIDTH))


def _out_kernel(x_ref, po_ref, og_ref, gt_ref, sc_ref, sh_ref, gpm_ref, gpf_ref, wo_ref, rw_ref, rb_ref, c0_ref,
                x1_ref, ht_ref, g_ref, r_ref, cnt_ref, carry, *, tm):
    @pl.when((pl.program_id(0) == 0) & (pl.program_id(1) == 0))
    def _():
        carry[...] = c0_ref[...]

    mix = _dot(po_ref[0], wo_ref[:POOL_WIDTH]) + _dot(og_ref[0], wo_ref[POOL_WIDTH:])
    x1 = x_ref[0] + gt_ref[0] * _rms(mix, gpm_ref[...])
    x1_ref[0] = x1
    h = _rms(x1, gpf_ref[...]) * (1.0 + sc_ref[0]) + sh_ref[0]
    for s in range(ROW_TILES):
        ht_ref[pl.ds(s, tm, stride=ROW_TILES), :] = h[:, s * LANES:(s + 1) * LANES]
    logits = _dot_nt(rw_ref[...], h) + rb_ref[...]
    eid = lax.broadcasted_iota(jnp.int32, logits.shape, 0).astype(F32)
    work = logits
    sel = jnp.zeros(logits.shape, jnp.bool_)
    top = None
    den = jnp.zeros((1, tm), F32)
    for j in range(TOP_K):
        m = jnp.max(work, axis=0, keepdims=True)
        idx = jnp.min(jnp.where(work == m, eid, float(N_EXPERTS)), axis=0, keepdims=True)
        hit = eid == idx
        sel = sel | hit
        work = jnp.where(hit, -jnp.inf, work)
        if j == 0:
            top = m
        den = den + jnp.exp(m - top)
    gates = jnp.exp(logits - top) / den
    g_ref[...] = jnp.transpose(jnp.where(sel, gates, -1.0))
    mask = sel.astype(F32)
    ri = lax.broadcasted_iota(jnp.int32, (tm, tm), 0)
    ci = lax.broadcasted_iota(jnp.int32, (tm, tm), 1)
    c = carry[...]
    r_ref[...] = jnp.transpose(_dot(mask, (ri < ci).astype(F32)) + c)
    c = c + jnp.sum(mask, axis=1, keepdims=True)
    carry[...] = c
    cnt_ref[...] = c


def _out_proj(x, pool_out, o_gdn, gt, sc, sh, g_post_mix, g_pre_ffn, w_out_b, router_w, router_b, counts0,
              tm, per_token):
    bsz, t, _ = x.shape
    n = bsz * t
    tiles = t // tm

    def tok(width):
        return pl.BlockSpec((1, tm, width), lambda b, i: (b, i, 0))

    def const(shape):
        return pl.BlockSpec(shape, lambda b, i: tuple(0 for _ in shape))

    kern = functools.partial(_out_kernel, tm=tm)
    return pl.pallas_call(
        kern,
        grid=(bsz, tiles),
        in_specs=[tok(D_MODEL), tok(POOL_WIDTH), tok(GDN_WIDTH),
                  _mod_spec(per_token, tm), _mod_spec(per_token, tm), _mod_spec(per_token, tm),
                  const((1, D_MODEL)), const((1, D_MODEL)), const((D_MODEL, D_MODEL)),
                  const((N_EXPERTS, D_MODEL)), const((N_EXPERTS, 1)), const((N_EXPERTS, 1))],
        out_specs=[tok(D_MODEL),
                   pl.BlockSpec((tm * ROW_TILES, LANES), lambda b, i: (b * tiles + i, 0)),
                   pl.BlockSpec((tm, N_EXPERTS), lambda b, i: (b * tiles + i, 0)),
                   pl.BlockSpec((tm, N_EXPERTS), lambda b, i: (b * tiles + i, 0)),
                   const((N_EXPERTS, 1))],
        out_shape=[jax.ShapeDtypeStruct((bsz, t, D_MODEL), F32),
                   jax.ShapeDtypeStruct((n * ROW_TILES, LANES), F32),
                   jax.ShapeDtypeStruct((n, N_EXPERTS), F32),
                   jax.ShapeDtypeStruct((n, N_EXPERTS), F32),
                   jax.ShapeDtypeStruct((N_EXPERTS, 1), F32)],
        scratch_shapes=[pltpu.VMEM((N_EXPERTS, 1), F32)],
        compiler_params=_cparams(("arbitrary", "arbitrary")),
        name="out_proj",
    )(x, pool_out, o_gdn, gt, sc, sh, g_post_mix.reshape(1, D_MODEL), g_pre_ffn.reshape(1, D_MODEL),
      w_out_b, jnp.swapaxes(router_w, 0, 1), router_b.reshape(N_EXPERTS, 1), counts0)


def _dest_kernel(g_ref, r_ref, start_ref, base_ref, d_ref, l_ref, gt_ref, *, tm, sub):
    g = g_ref[...]
    mask = g >= 0.0
    maskf = mask.astype(F32)
    dest = start_ref[...] + r_ref[...]
    base = jnp.concatenate([jnp.broadcast_to(base_ref[0, q:q + 1, :], (sub, N_EXPERTS)) for q in range(tm // sub)],
                           axis=0)
    local = dest - base
    ri = lax.broadcasted_iota(jnp.int32, (N_EXPERTS, N_EXPERTS), 0)
    ci = lax.broadcasted_iota(jnp.int32, (N_EXPERTS, N_EXPERTS), 1)
    slot = _dot(maskf, (ri < ci).astype(F32))
    lane4 = lax.broadcasted_iota(jnp.int32, (tm, TOP_K), 1)
    d_out = jnp.zeros((tm, TOP_K), F32)
    l_out = jnp.zeros((tm, TOP_K), F32)
    g_out = jnp.zeros((tm, TOP_K), F32)
    for j in range(TOP_K):
        pick = mask & (slot == float(j))
        dj = jnp.sum(jnp.where(pick, dest, 0.0), axis=-1, keepdims=True)
        lj = jnp.sum(jnp.where(pick, local, 0.0), axis=-1, keepdims=True)
        gj = jnp.sum(jnp.where(pick, g, 0.0), axis=-1, keepdims=True)
        d_out = jnp.where(lane4 == j, dj, d_out)
        l_out = jnp.where(lane4 == j, lj, l_out)
        g_out = jnp.where(lane4 == j, gj, g_out)
    d_ref[...] = d_out.astype(jnp.int32)
    l_ref[...] = l_out.astype(jnp.int32)
    gt_ref[...] = g_out


def _dest(gsel, rank, start_row, base, tm, sub):
    n = gsel.shape[0]
    per = tm // sub
    kern = functools.partial(_dest_kernel, tm=tm, sub=sub)
    tok4 = pl.BlockSpec((tm, TOP_K), lambda i: (i, 0))
    return pl.pallas_call(
        kern,
        grid=(n // tm,),
        in_specs=[pl.BlockSpec((tm, N_EXPERTS), lambda i: (i, 0)),
                  pl.BlockSpec((tm, N_EXPERTS), lambda i: (i, 0)),
                  pl.BlockSpec((1, N_EXPERTS), lambda i: (0, 0)),
                  pl.BlockSpec((1, per, N_EXPERTS), lambda i: (i, 0, 0))],
        out_specs=[tok4, tok4, tok4],
        out_shape=[jax.ShapeDtypeStruct((n, TOP_K), jnp.int32),
                   jax.ShapeDtypeStruct((n, TOP_K), jnp.int32),
                   jax.ShapeDtypeStruct((n, TOP_K), F32)],
        compiler_params=_cparams(("arbitrary",)),
        name="dest",
    )(gsel, rank, start_row, base.reshape(n // tm, per, N_EXPERTS))


def _row_copy(src, dst, s_row, d_row, sem):
    return pltpu.make_async_copy(src.at[pl.ds(pl.multiple_of(s_row * ROW_TILES, ROW_TILES), ROW_TILES), :],
                                 dst.at[pl.ds(pl.multiple_of(d_row * ROW_TILES, ROW_TILES), ROW_TILES), :], sem)


def _scatter_kernel(cnt_ref, start_ref, end_ref, nu_ref, dest_ref, ha_ref, hb_ref, xs_ref, sem, *,
                    tm, bm, n_blocks, tiles_a):
    i = pl.program_id(0)

    def scatter_from(ht_ref):
        def body(t, carry):
            src = ht_ref.at[pl.ds(pl.multiple_of(t * ROW_TILES, ROW_TILES), ROW_TILES), :]
            for k in range(TOP_K):
                d = dest_ref[t * TOP_K + k]
                pltpu.make_async_copy(
                    src, xs_ref.at[pl.ds(pl.multiple_of(d * ROW_TILES, ROW_TILES), ROW_TILES), :],
                    sem).start(priority=k % DMA_QUEUES)
            return carry

        lax.fori_loop(0, tm, body, 0)

    @pl.when(i < tiles_a)
    def _():
        scatter_from(ha_ref)

    @pl.when(i >= tiles_a)
    def _():
        scatter_from(hb_ref)

    ht_ref = hb_ref

    def tile_copy(u):
        return pltpu.make_async_copy(
            ht_ref, xs_ref.at[pl.ds(pl.multiple_of(u * (tm * ROW_TILES), tm * ROW_TILES), tm * ROW_TILES), :], sem)

    for k in range(TOP_K):
        tile_copy(0).wait()

    @pl.when(i == pl.num_programs(0) - 1)
    def _():
        per = bm // tm

        def tail_start(u, carry):
            tile_copy(u).start()
            return carry

        def tail_wait(u, carry):
            tile_copy(u).wait()
            return carry

        lax.fori_loop(nu_ref[0] * per, n_blocks * per, tail_start, 0)
        lax.fori_loop(nu_ref[0] * per, n_blocks * per, tail_wait, 0)

        chunk = bm // 4
        assert chunk <= tm

        def chunk_copy(q):
            return pltpu.make_async_copy(
                ht_ref.at[pl.ds(0, chunk * ROW_TILES), :],
                xs_ref.at[pl.ds(pl.multiple_of(q * (chunk * ROW_TILES), chunk * ROW_TILES), chunk * ROW_TILES), :],
                sem)

        def per_expert(e, carry):
            lo = start_ref[e] + cnt_ref[e]
            mid = (lo + chunk - 1) // chunk * chunk
            hi = end_ref[e]

            def fill(r, c2):
                _row_copy(ht_ref, xs_ref, 0, r, sem).start()
                return c2

            def fill_chunk(q, c2):
                chunk_copy(q).start()
                return c2

            lax.fori_loop(lo, mid, fill, 0)
            lax.fori_loop(mid // chunk, hi // chunk, fill_chunk, 0)

            def drain(r, c2):
                _row_copy(ht_ref, xs_ref, 0, r, sem).wait()
                return c2

            def drain_chunk(q, c2):
                chunk_copy(q).wait()
                return c2

            lax.fori_loop(lo, mid, drain, 0)
            lax.fori_loop(mid // chunk, hi // chunk, drain_chunk, 0)
            return carry

        lax.fori_loop(0, N_EXPERTS, per_expert, 0)


def _scatter_rows(ht_a, ht_b, dest_flat, cnt, start, end, n_used, n_blocks, tm, bm):
    rows = n_blocks * bm
    tiles_a, tiles_b = ht_a.shape[0] // (tm * ROW_TILES), ht_b.shape[0] // (tm * ROW_TILES)
    assert bm % tm == 0 and ht_a.shape[0] % (tm * ROW_TILES) == 0 and ht_b.shape[0] % (tm * ROW_TILES) == 0
    assert tiles_b >= 1
    kern = functools.partial(_scatter_kernel, tm=tm, bm=bm, n_blocks=n_blocks, tiles_a=tiles_a)
    return pl.pallas_call(
        kern,
        grid_spec=pltpu.PrefetchScalarGridSpec(
            num_scalar_prefetch=4,
            grid=(tiles_a + tiles_b,),
            in_specs=[pl.BlockSpec((tm * TOP_K,), lambda i, *_: (i,), memory_space=pltpu.SMEM),
                      pl.BlockSpec((tm * ROW_TILES, LANES), lambda i, *_: (jnp.minimum(i, tiles_a - 1), 0)),
                      pl.BlockSpec((tm * ROW_TILES, LANES), lambda i, *_: (jnp.maximum(i - tiles_a, 0), 0))],
            out_specs=pl.BlockSpec(memory_space=pl.ANY),
            scratch_shapes=[pltpu.SemaphoreType.DMA(())]),
        out_shape=jax.ShapeDtypeStruct((rows * ROW_TILES, LANES), F32),
        compiler_params=_cparams(("arbitrary",)),
        name="scatter_rows",
    )(cnt, start, end, n_used, dest_flat, ht_a, ht_b)


def _expert_kernel(be_ref, nx_ref, nv_ref, nu_ref, xs_ref, wg_ref, bg_ref, wd_ref, bd_ref, ys_ref,
                   wg_f, wd_f, wg_s, wd_s, wsem, *, bm):
    j = pl.program_id(0)
    prev = be_ref[jnp.maximum(j - 1, 0)]
    fresh = (j == 0) | (be_ref[j] != prev)
    active = j < nu_ref[0]

    def weight_copies(e):
        return (pltpu.make_async_copy(wg_ref.at[e], wg_f, wsem.at[0]),
                pltpu.make_async_copy(wd_ref.at[e], wd_f, wsem.at[1]))

    @pl.when(j == 0)
    def _():
        for c in weight_copies(be_ref[0]):
            c.start()

    @pl.when(active & fresh)
    def _():
        for c in weight_copies(be_ref[j]):
            c.wait()
        wg_s[...] = wg_f[...].astype(BF16)
        wd_s[...] = wd_f[...].astype(BF16)

        @pl.when(nx_ref[j] != be_ref[j])
        def _():
            for c in weight_copies(nx_ref[j]):
                c.start()

    def run_rows(m):
        x = jnp.concatenate([xs_ref[pl.ds(s, m, stride=ROW_TILES), :] for s in range(ROW_TILES)], axis=-1)
        xb = x.astype(BF16)
        acts = []
        for c in range(EXPERT_COL_CHUNKS):
            w = D_FF // EXPERT_COL_CHUNKS
            lo = c * w
            gate = jnp.dot(xb, wg_s[:, lo:lo + w], preferred_element_type=F32) + bg_ref[0, :, lo:lo + w]
            up = (jnp.dot(xb, wg_s[:, D_FF + lo:D_FF + lo + w], preferred_element_type=F32)
                  + bg_ref[0, :, D_FF + lo:D_FF + lo + w])
            gate = jnp.minimum(gate, SWIGLU_LIMIT)
            up = jnp.clip(up, -SWIGLU_LIMIT, SWIGLU_LIMIT)
            acts.append(((up + 1.0) * (gate * jax.nn.sigmoid(SWIGLU_ALPHA * gate))).astype(BF16))
        act = jnp.concatenate(acts, axis=-1)
        for c in range(EXPERT_COL_CHUNKS):
            w = D_MODEL // EXPERT_COL_CHUNKS
            lo = c * w
            y = jnp.dot(act, wd_s[:, lo:lo + w], preferred_element_type=F32) + bd_ref[0, :, lo:lo + w]
            for s in range(w // LANES):
                ys_ref[pl.ds(lo // LANES + s, m, stride=ROW_TILES), :] = y[:, s * LANES:(s + 1) * LANES]
        if m < bm:
            ys_ref[pl.ds(m * ROW_TILES, (bm - m) * ROW_TILES), :] = jnp.zeros(((bm - m) * ROW_TILES, LANES), F32)

    quarter = bm // 4
    quarters = (nv_ref[j] + quarter - 1) // quarter
    for nq in range(1, 5):
        @pl.when(active & ((quarters == nq) if nq > 1 else (quarters <= 1)))
        def _(nq=nq):
            run_rows(nq * quarter)

    @pl.when(jnp.logical_not(active))
    def _():
        ys_ref[...] = jnp.zeros_like(ys_ref)


def _experts(xs, block_expert, next_expert, block_valid, n_used, w_gu, b_gu, w_down, b_down, bm):
    rows = xs.shape[0] // ROW_TILES
    n_blocks = rows // bm

    def blk(j, be, nx, nv, nu):
        return (jnp.minimum(j, nu[0] - 1), 0)

    def bsel(j, be, nx, nv, nu):
        return (be[j], 0, 0)

    hbm = pl.BlockSpec(memory_space=pl.ANY)
    kern = functools.partial(_expert_kernel, bm=bm)
    return pl.pallas_call(
        kern,
        grid_spec=pltpu.PrefetchScalarGridSpec(
            num_scalar_prefetch=4,
            grid=(n_blocks,),
            in_specs=[pl.BlockSpec((bm * ROW_TILES, LANES), blk), hbm,
                      pl.BlockSpec((1, 1, 2 * D_FF), bsel), hbm,
                      pl.BlockSpec((1, 1, D_MODEL), bsel)],
            out_specs=pl.BlockSpec((bm * ROW_TILES, LANES), lambda j, be, nx, nv, nu: (j, 0)),
            scratch_shapes=[pltpu.VMEM((D_MODEL, 2 * D_FF), F32),
                            pltpu.VMEM((D_FF, D_MODEL), F32),
                            pltpu.VMEM((D_MODEL, 2 * D_FF), BF16),
                            pltpu.VMEM((D_FF, D_MODEL), BF16),
                            pltpu.SemaphoreType.DMA((2,))]),
        out_shape=jax.ShapeDtypeStruct((rows * ROW_TILES, LANES), F32),
        compiler_params=_cparams(("arbitrary",)),
        name="experts",
    )(block_expert, next_expert, block_valid, n_used, xs, w_gu, b_gu.reshape(N_EXPERTS, 1, 2 * D_FF), w_down,
      b_down.reshape(N_EXPERTS, 1, D_MODEL))


def _final_kernel(loc_ref, gate_ref, tcur_ref, tnext_ref, ys_ref, x1_ref, gt_ref, g_ref, o_ref,
                  st_ref, f_ref, sem, *, tm, steps):
    g = pl.program_id(0) * pl.num_programs(1) + pl.program_id(1)
    cur = g % 2
    sizes = [1 << b for b in range(tm.bit_length() - 1, -1, -1)]

    def request(t_ref, slot):
        def per_expert(e, off):
            row = t_ref[e]
            cnt = t_ref[N_EXPERTS + e]
            for size in sizes:
                take = (cnt & size) != 0

                @pl.when(take)
                def _(row=row, off=off, size=size):
                    pltpu.make_async_copy(
                        ys_ref.at[pl.ds(pl.multiple_of(row * ROW_TILES, ROW_TILES), size * ROW_TILES), :],
                        st_ref.at[slot, pl.ds(pl.multiple_of(off * ROW_TILES, ROW_TILES), size * ROW_TILES), :],
                        sem.at[slot]).start()

                step = jnp.where(take, size, 0)
                row = row + step
                off = off + step
            return off

        lax.fori_loop(0, N_EXPERTS, per_expert, 0)

    @pl.when(g == 0)
    def _():
        request(tcur_ref, 0)

    @pl.when(g + 1 < steps)
    def _():
        request(tnext_ref, 1 - cur)

    pltpu.make_async_copy(ys_ref.at[pl.ds(0, tm * TOP_K * ROW_TILES), :], st_ref.at[cur], sem.at[cur]).wait()

    def pick(t, carry):
        for k in range(TOP_K):
            at = pl.multiple_of(loc_ref[t * TOP_K + k] * ROW_TILES, ROW_TILES)
            f_ref[k, pl.ds(pl.multiple_of(t * ROW_TILES, ROW_TILES), ROW_TILES), :] = (
                st_ref[cur, pl.ds(at, ROW_TILES), :])
        return carry

    lax.fori_loop(0, tm, pick, 0)
    gates = gate_ref[...]
    cols = []
    for s in range(ROW_TILES):
        acc = None
        for k in range(TOP_K):
            part = f_ref[k, pl.ds(s, tm, stride=ROW_TILES), :] * gates[:, k:k + 1]
            acc = part if acc is None else acc + part
        cols.append(acc)
    f = jnp.concatenate(cols, axis=-1)
    o_ref[0] = x1_ref[0] + gt_ref[0] * _rms(f, g_ref[...])


def _final(ys, local_flat, gates, run_table, x1, gt, g_post_ffn, tok_off, tm, per_token):
    bsz, t, _ = x1.shape
    tiles = t // tm
    off = tok_off // tm
    steps = bsz * tiles
    kern = functools.partial(_final_kernel, tm=tm, steps=steps)

    def smem(ahead):
        return pl.BlockSpec((tm * TOP_K,), lambda b, i: (off + jnp.minimum(b * tiles + i + ahead, steps - 1),),
                            memory_space=pltpu.SMEM)

    return pl.pallas_call(
        kern,
        grid=(bsz, tiles),
        in_specs=[smem(0), pl.BlockSpec((tm, TOP_K), lambda b, i: (off + b * tiles + i, 0)), smem(0), smem(1),
                  pl.BlockSpec(memory_space=pl.ANY),
                  pl.BlockSpec((1, tm, D_MODEL), lambda b, i: (b, i, 0)),
                  _mod_spec(per_token, tm),
                  pl.BlockSpec((1, D_MODEL), lambda b, i: (0, 0))],
        out_specs=pl.BlockSpec((1, tm, D_MODEL), lambda b, i: (b, i, 0)),
        out_shape=jax.ShapeDtypeStruct((bsz, t, D_MODEL), F32),
        scratch_shapes=[pltpu.VMEM((2, tm * TOP_K * ROW_TILES, LANES), F32),
                        pltpu.VMEM((TOP_K, tm * ROW_TILES, LANES), F32),
                        pltpu.SemaphoreType.DMA((2,))],
        compiler_params=_cparams(("arbitrary", "arbitrary")),
        name="final",
    )(local_flat, gates, run_table, run_table, ys, x1, gt, g_post_ffn.reshape(1, D_MODEL))


def _moe(ht_a, ht_b, gsel, rank, counts, w_gu, b_gu, w_down, b_down, tm, bm):
    n = gsel.shape[0]
    cnt = counts[:, 0].astype(jnp.int32)
    nblk = (cnt + bm - 1) // bm
    blk_end = jnp.cumsum(nblk)
    blk_start = blk_end - nblk
    n_blocks = -(-(n * TOP_K) // bm) + N_EXPERTS
    blocks = jnp.arange(n_blocks, dtype=jnp.int32)
    block_expert = jnp.minimum(jnp.sum((blk_end[None, :] <= blocks[:, None]).astype(jnp.int32), axis=1),
                               N_EXPERTS - 1).astype(jnp.int32)
    ids = jnp.arange(N_EXPERTS, dtype=jnp.int32)
    later = jnp.where((nblk[None, :] > 0) & (ids[None, :] > ids[:, None]), ids[None, :], N_EXPERTS)
    nearest = jnp.min(later, axis=1)
    next_used = jnp.where(nearest < N_EXPERTS, nearest, ids).astype(jnp.int32)
    n_used = blk_end[-1:].astype(jnp.int32)
    row_start = (blk_start * bm).astype(jnp.int32)
    row_end = (blk_end * bm).astype(jnp.int32)
    before = rank[::tm]
    run_len = jnp.concatenate([before[1:], counts.reshape(1, N_EXPERTS)], axis=0) - before
    run_row = row_start.astype(F32)[None, :] + before
    run_off = jnp.cumsum(run_len, axis=1) - run_len
    tiles = n // tm
    run_table = jnp.zeros((tiles, tm * TOP_K), jnp.int32)
    run_table = run_table.at[:, :N_EXPERTS].set(run_row.astype(jnp.int32))
    run_table = run_table.at[:, N_EXPERTS:2 * N_EXPERTS].set(run_len.astype(jnp.int32)).reshape(tiles * tm * TOP_K)
    dest, local, gates = _dest(gsel, rank, row_start.astype(F32).reshape(1, N_EXPERTS), run_row - run_off,
                               DEST_TILE, tm)
    dest_flat = dest.reshape(n * TOP_K)
    xs = _scatter_rows(ht_a, ht_b, dest_flat, cnt, row_start, row_end, n_used, n_blocks, SCATTER_TILE, bm)
    owns = (blocks[:, None] >= blk_start[None, :]) & (blocks[:, None] < blk_end[None, :])

    def per_block(table):
        return jnp.sum(jnp.where(owns, table[None, :], 0), axis=1).astype(jnp.int32)

    block_valid = jnp.clip(per_block(cnt) - (blocks - per_block(blk_start)) * bm, 0, bm).astype(jnp.int32)
    ys = _experts(xs, block_expert, per_block(next_used), block_valid, n_used, w_gu, b_gu, w_down, b_down, bm)
    return ys, local.reshape(n * TOP_K), gates, run_table


def _mix_path(x, mod, pool_buf, conv_buf, s0, start_pos, counts0, lp, tm, nb, per_token):
    (g_pre_mix, g_post_mix, g_pre_ffn, w_in_p, w_bd, pool_scale, conv_w, alog_row, dtb_row, norm_w,
     w_out_b, router_w, router_b) = lp
    bsz, t, _ = x.shape
    sh_mix, sc_mix, gt_mix, sh_ffn, sc_ffn, gt_ffn = (mod[:, i] for i in range(6))

    def shape_mod(m):
        if per_token:
            return jnp.repeat(m, t, axis=0).reshape(1, bsz * t, D_MODEL)
        return m.reshape(bsz, 1, D_MODEL)

    sh_mix, sc_mix, gt_mix, sh_ffn, sc_ffn, gt_ffn = map(shape_mod, (sh_mix, sc_mix, gt_mix, sh_ffn, sc_ffn, gt_ffn))
    xf = x.reshape(1, bsz * t, D_MODEL) if per_token else x
    proj = _in_proj(xf, sc_mix, sh_mix, g_pre_mix, w_in_p, PROJ_TILE, per_token).reshape(bsz, t, PROJ_WIDTH)
    tseq = min(tm, t)
    o_gdn, s_new = _gdn(proj, conv_buf, s0, conv_w, alog_row, dtb_row, norm_w, nb, tseq)
    pool_out = _pool(proj, pool_buf, w_bd, pool_scale, nb * POOL_SEQS if per_token else nb, min(POOL_TILE, t),
                     start_pos)
    fl = (lambda a: a.reshape(1, bsz * t, a.shape[-1])) if per_token else (lambda a: a)
    x1, ht, gsel, rank, counts = _out_proj(xf, fl(pool_out), fl(o_gdn), gt_mix, sc_ffn, sh_ffn, g_post_mix,
                                           g_pre_ffn, w_out_b, router_w, router_b, counts0, OUT_TILE, per_token)

    def last_rows(buf, off, width):
        keep = buf.shape[1]
        if t >= keep:
            return proj[:, t - keep:, off:off + width]
        return jnp.concatenate([buf[:, t:], proj[:, :, off:off + width]], axis=1)

    pool_new = last_rows(pool_buf, OFF_U, POOL_WIDTH)
    conv_new = last_rows(conv_buf, OFF_QKV, CONV_CH)
    return x1, ht, gsel, rank, counts, gt_ffn, pool_new, conv_new, s_new


def kernel(x_prompt, x_sample, c_prompt, c_sample, state_pool, state_conv, state_ssm, w_ada, b_ada, g_pre_mix, g_post_mix, g_pre_ffn, g_post_ffn, w_in, w_pool, pool_scale, conv_w, a_log, dt_bias, gdn_norm_w, w_out, router_w, router_b, w_gu, b_gu, w_down, b_down):
    depth = w_ada.shape[0]
    bp, tp, _ = x_prompt.shape
    bs, ts, _ = x_sample.shape
    tm = TOKEN_TILE
    y_p, y_s = x_prompt, x_sample
    outs = [[] for _ in range(6)]
    for l in range(depth):
        mod = _ada(jnp.concatenate([c_prompt, c_sample], axis=0), w_ada[l], b_ada[l]).reshape(bp + bs, 6, D_MODEL)
        w_in_p = jnp.swapaxes(w_in[l], 0, 1)
        w_bd = jax.scipy.linalg.block_diag(*[w_pool[l, g] for g in range(len(POOL_WINDOWS))])
        pad_row = lambda v: jnp.pad(v, (0, LANES - v.shape[0])).reshape(1, LANES)
        lp = (g_pre_mix[l], g_post_mix[l], g_pre_ffn[l], w_in_p, w_bd, pool_scale[l], conv_w[l],
              pad_row(a_log[l]), pad_row(dt_bias[l]), gdn_norm_w[l].reshape(1, HEAD_DIM),
              w_out[l].astype(BF16), router_w[l], router_b[l])
        zero_pool = jnp.zeros((bp, POOL_BUF, POOL_WIDTH), F32)
        zero_conv = jnp.zeros((bp, CONV_WIDTH - 1, CONV_CH), F32)
        zero_ssm = (jnp.zeros((1, bp, GDN_HEADS, HEAD_DIM, HEAD_DIM), F32), 0)
        no_counts = jnp.zeros((N_EXPERTS, 1), F32)
        x1p, htp, gp, rp, cnt_p, gtp, pp, cp, sp = _mix_path(
            y_p, mod[:bp], zero_pool, zero_conv, zero_ssm, 0, no_counts, lp, tm, 1, False)
        x1s, hts, gs, rs, counts, gts, ps, cs, ss = _mix_path(
            y_s, mod[bp:], state_pool[l], state_conv[l], (state_ssm, l), PAST_LEN, cnt_p, lp, tm, 8, True)
        gsel = jnp.concatenate([gp, gs], axis=0)
        rank = jnp.concatenate([rp, rs], axis=0)
        ys, local, gates, runs = _moe(htp, hts, gsel, rank, counts, w_gu[l], b_gu[l], w_down[l], b_down[l],
                                      FINAL_TILE, MOE_BLOCK)
        y_p = _final(ys, local, gates, runs, x1p, gtp, g_post_ffn[l], 0, FINAL_TILE, False)
        y_s = _final(ys, local, gates, runs, x1s, gts, g_post_ffn[l], bp * tp, FINAL_TILE,
                     True).reshape(bs, ts, D_MODEL)
        for lst, val in zip(outs, (pp, cp, sp, ps, cs, ss)):
            lst.append(val)
    return (y_p, y_s) + tuple(o[0][None] if depth == 1 else jnp.stack(o) for o in outs)
```

```python
import functools
import math

import jax
import jax.numpy as jnp
from jax import lax
from jax.experimental import pallas as pl
from jax.experimental.pallas import tpu as pltpu

F32 = jnp.float32
BF16 = jnp.bfloat16

D_MODEL = 1024
PAST_LEN = 16384
POOL_WINDOWS = (2, 4, 8, 16)
POOL_WIDTH = D_MODEL // 4
POOL_GROUP = POOL_WIDTH // len(POOL_WINDOWS)
POOL_BUF = max(POOL_WINDOWS) - 1
GDN_WIDTH = D_MODEL - POOL_WIDTH
HEAD_DIM = 128
GDN_HEADS = GDN_WIDTH // HEAD_DIM
CONV_WIDTH = 4
CONV_CH = 3 * GDN_WIDTH
GDN_CHUNK = 64
N_EXPERTS = 32
TOP_K = 4
D_FF = D_MODEL
SWIGLU_LIMIT = 7.0
SWIGLU_ALPHA = 1.702
EPS = 1e-6

SUBLANES = 8
LANES = 128
ROW_TILES = D_MODEL // LANES
OFF_QKV = 0
OFF_Z = OFF_QKV + CONV_CH
OFF_U = OFF_Z + GDN_WIDTH
OFF_AB = OFF_U + POOL_WIDTH
PROJ_WIDTH = OFF_AB + LANES
assert OFF_Z % GDN_WIDTH == 0 and OFF_U % POOL_WIDTH == 0 and OFF_AB % LANES == 0
TOKEN_TILE = 256
PROJ_TILE = 512
SCATTER_TILE = 512
OUT_TILE = 512
FINAL_TILE = 512
DEST_TILE = 1536
POOL_TILE = 1024
POOL_SEQS = 4
GDN_ITEM_GROUPS = 1
MOE_BLOCK = 1024
EXPERT_COL_CHUNKS = 2
DMA_QUEUES = 2
ROW_LOOP_UNROLL = 4
VMEM_LIMIT = 56 * 1024 * 1024


def _cparams(sem):
    return pltpu.CompilerParams(dimension_semantics=sem, vmem_limit_bytes=VMEM_LIMIT)


def _dot(a, b):
    return jnp.dot(a.astype(BF16), b.astype(BF16), preferred_element_type=F32)


def _dot_nt(a, b):
    return lax.dot_general(a.astype(BF16), b.astype(BF16), (((1,), (1,)), ((), ())),
                           preferred_element_type=F32)


def _silu(x):
    return x * jax.nn.sigmoid(x)


def _rms(x, g):
    return x * lax.rsqrt(jnp.mean(x * x, axis=-1, keepdims=True) + EPS) * g


def _ada_kernel(c_ref, w_ref, b_ref, o_ref):
    o_ref[...] = _dot(_silu(c_ref[...]), w_ref[...]) + b_ref[...]


def _ada(c, w_ada, b_ada):
    n = c.shape[0]
    cols = w_ada.shape[1]
    blk = D_MODEL
    return pl.pallas_call(
        _ada_kernel,
        grid=(cols // blk,),
        in_specs=[pl.BlockSpec((n, D_MODEL), lambda j: (0, 0)),
                  pl.BlockSpec((D_MODEL, blk), lambda j: (0, j)),
                  pl.BlockSpec((1, blk), lambda j: (0, j))],
        out_specs=pl.BlockSpec((n, blk), lambda j: (0, j)),
        out_shape=jax.ShapeDtypeStruct((n, cols), F32),
        compiler_params=_cparams(("arbitrary",)),
        name="ada",
    )(c, w_ada, b_ada.reshape(1, cols))


def _in_kernel(x_ref, sc_ref, sh_ref, g_ref, wt_ref, o_ref, wb):
    @pl.when((pl.program_id(0) == 0) & (pl.program_id(1) == 0))
    def _():
        o1, o2, o3 = POOL_WIDTH, POOL_WIDTH + CONV_CH, POOL_WIDTH + CONV_CH + GDN_WIDTH
        wb[OFF_QKV:OFF_QKV + CONV_CH] = wt_ref[o1:o2].astype(BF16)
        wb[OFF_Z:OFF_Z + GDN_WIDTH] = wt_ref[o2:o3].astype(BF16)
        wb[OFF_U:OFF_U + POOL_WIDTH] = wt_ref[:o1].astype(BF16)
        tail = wt_ref.shape[0] - o3
        wb[OFF_AB:] = jnp.concatenate(
            [wt_ref[o3:], jnp.zeros((LANES - tail, D_MODEL), F32)], axis=0).astype(BF16)

    h = _rms(x_ref[0], g_ref[...]) * (1.0 + sc_ref[0]) + sh_ref[0]
    o_ref[0] = _dot_nt(h, wb[...])


def _mod_spec(per_token, tm):
    if per_token:
        return pl.BlockSpec((1, tm, D_MODEL), lambda b, t: (0, t, 0))
    return pl.BlockSpec((1, 1, D_MODEL), lambda b, t: (b, 0, 0))


def _in_proj(x, sc, sh, g, w_in, tm, per_token):
    bsz, t, _ = x.shape
    return pl.pallas_call(
        _in_kernel,
        grid=(bsz, t // tm),
        in_specs=[pl.BlockSpec((1, tm, D_MODEL), lambda b, i: (b, i, 0)),
                  _mod_spec(per_token, tm), _mod_spec(per_token, tm),
                  pl.BlockSpec((1, D_MODEL), lambda b, i: (0, 0)),
                  pl.BlockSpec(w_in.shape, lambda b, i: (0, 0), pipeline_mode=pl.Buffered(1))],
        out_specs=pl.BlockSpec((1, tm, PROJ_WIDTH), lambda b, i: (b, i, 0)),
        out_shape=jax.ShapeDtypeStruct((bsz, t, PROJ_WIDTH), F32),
        scratch_shapes=[pltpu.VMEM((PROJ_WIDTH, D_MODEL), BF16)],
        compiler_params=_cparams(("arbitrary", "arbitrary")),
        name="in_proj",
    )(x, sc, sh, g.reshape(1, D_MODEL), w_in)


def _cumsum_rows(x):
    n = x.shape[0]
    row = lax.broadcasted_iota(jnp.int32, x.shape, 0)
    s = 1
    while s < n:
        x = x + jnp.where(row >= s, pltpu.roll(x, s, axis=0), 0.0)
        s *= 2
    return x


def _gdn_kernel(qkv_ref, z_ref, ab_ref, cs_ref, cw_ref, s0_ref, alog_ref, dtb_ref, nw_ref, o_ref, sn_ref,
                s_scr, prev_scr, *, nb, tm, tmp, chunk, group_items):
    ti = pl.program_id(1)
    nt = pl.num_programs(1)
    n_chunks = tmp // chunk
    n_sq = int(math.log2(chunk)) - 1

    @pl.when(ti == 0)
    def _():
        s_scr[...] = s0_ref[...]
        for b in range(nb):
            prev_scr[b] = jnp.concatenate(
                [jnp.zeros((SUBLANES - (CONV_WIDTH - 1), CONV_CH), F32), cs_ref[b]], axis=0)

    rowi = lax.broadcasted_iota(jnp.int32, (tmp, LANES), 0)
    ri = lax.broadcasted_iota(jnp.int32, (chunk, chunk), 0)
    ci = lax.broadcasted_iota(jnp.int32, (chunk, chunk), 1)
    tril = ri >= ci
    strict = ri > ci
    eye = (ri == ci).astype(F32)
    neg_a = -jnp.exp(alog_ref[...])
    dtb = dtb_ref[...]
    nw = nw_ref[...]

    def padrows(x):
        if tm == tmp:
            return x
        return jnp.concatenate([x, jnp.zeros((tmp - tm, x.shape[1]), F32)], axis=0)

    cw = cw_ref[...]
    valid = rowi < tm
    items = []
    for b in range(nb):
        x = padrows(qkv_ref[b])
        e = jnp.concatenate([prev_scr[b], x], axis=0)
        acc = x * cw[CONV_WIDTH - 1:CONV_WIDTH]
        for d in range(1, CONV_WIDTH):
            acc = acc + pltpu.roll(e, d, axis=0)[SUBLANES:] * cw[CONV_WIDTH - 1 - d:CONV_WIDTH - d]
        if tm >= SUBLANES:
            prev_scr[b] = x[tm - SUBLANES:tm]
        y = _silu(acc)
        ab = padrows(ab_ref[b])
        g_all = jnp.where(valid, neg_a * jax.nn.softplus(ab + dtb), 0.0)
        beta_all = jnp.where(valid, jax.nn.sigmoid(ab), 0.0)
        for h in range(GDN_HEADS):
            qc = y[:, h * HEAD_DIM:(h + 1) * HEAD_DIM]
            kc = y[:, GDN_WIDTH + h * HEAD_DIM:GDN_WIDTH + (h + 1) * HEAD_DIM]
            v = y[:, 2 * GDN_WIDTH + h * HEAD_DIM:2 * GDN_WIDTH + (h + 1) * HEAD_DIM]
            q = qc * lax.rsqrt(jnp.sum(qc * qc, axis=-1, keepdims=True) + EPS) * (HEAD_DIM ** -0.5)
            k = kc * lax.rsqrt(jnp.sum(kc * kc, axis=-1, keepdims=True) + EPS)
            gb = jnp.broadcast_to(g_all[:, h:h + 1], (tmp, LANES))
            bb = jnp.broadcast_to(beta_all[:, GDN_HEADS + h:GDN_HEADS + h + 1], (tmp, LANES))
            for c in range(n_chunks):
                sl = slice(c * chunk, (c + 1) * chunk)
                items.append(dict(b=b, h=h, c=c, q=q[sl], k=k[sl], v=v[sl], bb=bb[sl], gb=gb[sl]))

    def within_chunk(group):
        for it in group:
            gcb = _cumsum_rows(it["gb"])
            gct = jnp.transpose(gcb)[:chunk]
            diff = gcb[:, :chunk] - gct
            it["decay"] = jnp.where(tril, jnp.exp(jnp.where(tril, diff, 0.0)), 0.0)
            it["kb"] = it["k"] * it["bb"]
            it["eg"] = jnp.exp(gcb)
            gl = gcb[chunk - 1:chunk]
            it["egl"] = jnp.exp(gl)
            it["kdt"] = jnp.transpose(it["k"] * jnp.exp(gl - gcb))
        for it in group:
            kq = _dot_nt(jnp.concatenate([it["kb"], it["q"]], axis=0), it["k"])
            it["a"] = jnp.where(strict, kq[:chunk] * it["decay"], 0.0)
            it["qk"] = jnp.where(tril, kq[chunk:] * it["decay"], 0.0)
        for it in group:
            it["inv"] = eye - it["a"]
            it["p"] = _dot(it["a"], it["a"])
        for lvl in range(n_sq):
            for it in group:
                if lvl < n_sq - 1:
                    r = _dot(jnp.concatenate([it["inv"], it["p"]], axis=0), it["p"])
                    it["inv"] = it["inv"] + r[:chunk]
                    it["p"] = r[chunk:]
                else:
                    it["inv"] = it["inv"] + _dot(it["inv"], it["p"])
        for it in group:
            it["wu"] = _dot(it["inv"], jnp.concatenate([it["kb"] * it["eg"], it["v"] * it["bb"]], axis=1))
        for it in group:
            mn = _dot(it["kdt"], it["wu"])
            qo = _dot(it["qk"], it["wu"])
            it["mp"], it["nc"] = mn[:, :HEAD_DIM], mn[:, HEAD_DIM:]
            it["qp"] = it["q"] * it["eg"] - qo[:, :HEAD_DIM]
            it["oc"] = qo[:, HEAD_DIM:]

    for g0 in range(0, len(items), group_items):
        within_chunk(items[g0:g0 + group_items])

    state = {(b, h): s_scr[b, h] for b in range(nb) for h in range(GDN_HEADS)}
    for c in range(n_chunks):
        for it in items:
            if it["c"] != c:
                continue
            s = state[(it["b"], it["h"])]
            r = _dot(jnp.concatenate([it["qp"], it["mp"]], axis=0), s)
            it["o"] = r[:chunk] + it["oc"]
            state[(it["b"], it["h"])] = s * it["egl"] - r[chunk:] + it["nc"]
    for (b, h), s in state.items():
        s_scr[b, h] = s

    for b in range(nb):
        zf = padrows(z_ref[b])
        for h in range(GDN_HEADS):
            outs = [it["o"] for it in items if it["b"] == b and it["h"] == h]
            o = outs[0] if n_chunks == 1 else jnp.concatenate(outs, axis=0)
            o = (o * lax.rsqrt(jnp.mean(o * o, axis=-1, keepdims=True) + EPS) * nw
                 * _silu(zf[:, h * HEAD_DIM:(h + 1) * HEAD_DIM]))
            o_ref[b, :, h * HEAD_DIM:(h + 1) * HEAD_DIM] = o[:tm]

    @pl.when(ti == nt - 1)
    def _():
        sn_ref[...] = s_scr[...]


def _gdn(proj, state_conv, s0, conv_w, alog_row, dtb_row, norm_w, nb, tm):
    bsz, t, _ = proj.shape
    tmp = max(tm, SUBLANES)
    chunk = min(GDN_CHUNK, tmp)
    row = pl.BlockSpec((1, LANES), lambda b, i: (0, 0))
    s_all, layer = s0
    sspec = pl.BlockSpec((nb, GDN_HEADS, HEAD_DIM, HEAD_DIM), lambda b, i: (b, 0, 0, 0))
    s_in = pl.BlockSpec((None, nb, GDN_HEADS, HEAD_DIM, HEAD_DIM), lambda b, i: (layer, b, 0, 0, 0))
    n_items = nb * GDN_HEADS * (tmp // chunk)
    kern = functools.partial(_gdn_kernel, nb=nb, tm=tm, tmp=tmp, chunk=chunk,
                             group_items=n_items // GDN_ITEM_GROUPS)
    return pl.pallas_call(
        kern,
        grid=(bsz // nb, t // tm),
        in_specs=[pl.BlockSpec((nb, tm, CONV_CH), lambda b, i: (b, i, OFF_QKV // CONV_CH)),
                  pl.BlockSpec((nb, tm, GDN_WIDTH), lambda b, i: (b, i, OFF_Z // GDN_WIDTH)),
                  pl.BlockSpec((nb, tm, LANES), lambda b, i: (b, i, OFF_AB // LANES)),
                  pl.BlockSpec((nb, CONV_WIDTH - 1, CONV_CH), lambda b, i: (b, 0, 0)),
                  pl.BlockSpec((CONV_WIDTH, CONV_CH), lambda b, i: (0, 0)),
                  s_in, row, row, row],
        out_specs=[pl.BlockSpec((nb, tm, GDN_WIDTH), lambda b, i: (b, i, 0)), sspec],
        out_shape=[jax.ShapeDtypeStruct((bsz, t, GDN_WIDTH), F32),
                   jax.ShapeDtypeStruct(s_all.shape[1:], F32)],
        scratch_shapes=[pltpu.VMEM((nb, GDN_HEADS, HEAD_DIM, HEAD_DIM), F32),
                        pltpu.VMEM((nb, SUBLANES, CONV_CH), F32)],
        compiler_params=_cparams(("arbitrary", "arbitrary")),
        name="gdn",
    )(proj, proj, proj, state_conv, conv_w, s_all, alog_row, dtb_row, norm_w)


def _pool_kernel(u_ref, st_ref, w_ref, sc_ref, o_ref, prev_scr, *, nb, tm, tmp, start_pos):
    ti = pl.program_id(1)
    halo = 2 * SUBLANES

    @pl.when(ti == 0)
    def _():
        for b in range(nb):
            prev_scr[b] = jnp.concatenate([jnp.zeros((halo - POOL_BUF, POOL_WIDTH), F32), st_ref[b]], axis=0)

    lane = lax.broadcasted_iota(jnp.int32, (tmp, POOL_WIDTH), 1)
    rowi = lax.broadcasted_iota(jnp.int32, (tmp, POOL_WIDTH), 0)
    pos1 = start_pos + ti * tm + rowi + 1
    grp = lane // POOL_GROUP
    win = jnp.where(grp == 0, POOL_WINDOWS[0],
                    jnp.where(grp == 1, POOL_WINDOWS[1], jnp.where(grp == 2, POOL_WINDOWS[2], POOL_WINDOWS[3])))
    cnt = jnp.minimum(pos1, win).astype(F32)
    for b in range(nb):
        x = u_ref[b]
        if tm != tmp:
            x = jnp.concatenate([x, jnp.zeros((tmp - tm, POOL_WIDTH), F32)], axis=0)
        e = jnp.concatenate([prev_scr[b], x], axis=0)
        sums = []
        s = e
        for w in POOL_WINDOWS:
            s = s + pltpu.roll(s, w // 2, axis=0)
            sums.append(s[halo:])
        sel = jnp.where(grp == 0, sums[0], jnp.where(grp == 1, sums[1], jnp.where(grp == 2, sums[2], sums[3])))
        pooled = sel / cnt - x
        out = _dot(pooled, w_ref[...]) * sc_ref[...]
        o_ref[b] = out[:tm]
        if tm >= halo:
            prev_scr[b] = x[tm - halo:tm]


def _pool(proj, state_pool, w_bd, pool_scale, nb, tm, start_pos):
    bsz, t, _ = proj.shape
    tmp = max(tm, SUBLANES)
    kern = functools.partial(_pool_kernel, nb=nb, tm=tm, tmp=tmp, start_pos=start_pos)
    return pl.pallas_call(
        kern,
        grid=(bsz // nb, t // tm),
        in_specs=[pl.BlockSpec((nb, tm, POOL_WIDTH), lambda b, i: (b, i, OFF_U // POOL_WIDTH)),
                  pl.BlockSpec((nb, POOL_BUF, POOL_WIDTH), lambda b, i: (b, 0, 0)),
                  pl.BlockSpec((POOL_WIDTH, POOL_WIDTH), lambda b, i: (0, 0)),
                  pl.BlockSpec((1, POOL_WIDTH), lambda b, i: (0, 0))],
        out_specs=pl.BlockSpec((nb, tm, POOL_WIDTH), lambda b, i: (b, i, 0)),
        out_shape=jax.ShapeDtypeStruct((bsz, t, POOL_WIDTH), F32),
        scratch_shapes=[pltpu.VMEM((nb, 2 * SUBLANES, POOL_WIDTH), F32)],
        compiler_params=_cparams(("arbitrary", "arbitrary")),
        name="pool",
    )(proj, state_pool, w_bd, pool_scale.reshape(1, POOL_WIDTH))


def _out_kernel(x_ref, po_ref, og_ref, gt_ref, sc_ref, sh_ref, gpm_ref, gpf_ref, wo_ref, rw_ref, rb_ref, c0_ref,
                x1_ref, ht_ref, g_ref, r_ref, cnt_ref, carry, *, tm):
    @pl.when((pl.program_id(0) == 0) & (pl.program_id(1) == 0))
    def _():
        carry[...] = c0_ref[...]

    mix = _dot(po_ref[0], wo_ref[:POOL_WIDTH]) + _dot(og_ref[0], wo_ref[POOL_WIDTH:])
    x1 = x_ref[0] + gt_ref[0] * _rms(mix, gpm_ref[...])
    x1_ref[0] = x1
    h = _rms(x1, gpf_ref[...]) * (1.0 + sc_ref[0]) + sh_ref[0]
    for s in range(ROW_TILES):
        ht_ref[pl.ds(s, tm, stride=ROW_TILES), :] = h[:, s * LANES:(s + 1) * LANES]
    logits = _dot_nt(rw_ref[...], h) + rb_ref[...]
    eid = lax.broadcasted_iota(jnp.int32, logits.shape, 0).astype(F32)
    work = logits
    sel = jnp.zeros(logits.shape, jnp.bool_)
    top = None
    den = jnp.zeros((1, tm), F32)
    for j in range(TOP_K):
        m = jnp.max(work, axis=0, keepdims=True)
        idx = jnp.min(jnp.where(work == m, eid, float(N_EXPERTS)), axis=0, keepdims=True)
        hit = eid == idx
        sel = sel | hit
        work = jnp.where(hit, -jnp.inf, work)
        if j == 0:
            top = m
        den = den + jnp.exp(m - top)
    gates = jnp.exp(logits - top) / den
    g_ref[...] = jnp.transpose(jnp.where(sel, gates, -1.0))
    mask = sel.astype(F32)
    ri = lax.broadcasted_iota(jnp.int32, (tm, tm), 0)
    ci = lax.broadcasted_iota(jnp.int32, (tm, tm), 1)
    c = carry[...]
    r_ref[...] = jnp.transpose(_dot(mask, (ri < ci).astype(F32)) + c)
    c = c + jnp.sum(mask, axis=1, keepdims=True)
    carry[...] = c
    cnt_ref[...] = c


def _out_proj(x, pool_out, o_gdn, gt, sc, sh, g_post_mix, g_pre_ffn, w_out_b, router_w, router_b, counts0,
              tm, per_token):
    bsz, t, _ = x.shape
    n = bsz * t
    tiles = t // tm

    def tok(width):
        return pl.BlockSpec((1, tm, width), lambda b, i: (b, i, 0))

    def const(shape):
        return pl.BlockSpec(shape, lambda b, i: tuple(0 for _ in shape))

    kern = functools.partial(_out_kernel, tm=tm)
    return pl.pallas_call(
        kern,
        grid=(bsz, tiles),
        in_specs=[tok(D_MODEL), tok(POOL_WIDTH), tok(GDN_WIDTH),
                  _mod_spec(per_token, tm), _mod_spec(per_token, tm), _mod_spec(per_token, tm),
                  const((1, D_MODEL)), const((1, D_MODEL)), const((D_MODEL, D_MODEL)),
                  const((N_EXPERTS, D_MODEL)), const((N_EXPERTS, 1)), const((N_EXPERTS, 1))],
        out_specs=[tok(D_MODEL),
                   pl.BlockSpec((tm * ROW_TILES, LANES), lambda b, i: (b * tiles + i, 0)),
                   pl.BlockSpec((tm, N_EXPERTS), lambda b, i: (b * tiles + i, 0)),
                   pl.BlockSpec((tm, N_EXPERTS), lambda b, i: (b * tiles + i, 0)),
                   const((N_EXPERTS, 1))],
        out_shape=[jax.ShapeDtypeStruct((bsz, t, D_MODEL), F32),
                   jax.ShapeDtypeStruct((n * ROW_TILES, LANES), F32),
                   jax.ShapeDtypeStruct((n, N_EXPERTS), F32),
                   jax.ShapeDtypeStruct((n, N_EXPERTS), F32),
                   jax.ShapeDtypeStruct((N_EXPERTS, 1), F32)],
        scratch_shapes=[pltpu.VMEM((N_EXPERTS, 1), F32)],
        compiler_params=_cparams(("arbitrary", "arbitrary")),
        name="out_proj",
    )(x, pool_out, o_gdn, gt, sc, sh, g_post_mix.reshape(1, D_MODEL), g_pre_ffn.reshape(1, D_MODEL),
      w_out_b, jnp.swapaxes(router_w, 0, 1), router_b.reshape(N_EXPERTS, 1), counts0)


def _dest_kernel(g_ref, r_ref, start_ref, base_ref, d_ref, l_ref, gt_ref, *, tm, sub):
    g = g_ref[...]
    mask = g >= 0.0
    maskf = mask.astype(F32)
    dest = start_ref[...] + r_ref[...]
    base = jnp.concatenate([jnp.broadcast_to(base_ref[0, q:q + 1, :], (sub, N_EXPERTS)) for q in range(tm // sub)],
                           axis=0)
    local = dest - base
    ri = lax.broadcasted_iota(jnp.int32, (N_EXPERTS, N_EXPERTS), 0)
    ci = lax.broadcasted_iota(jnp.int32, (N_EXPERTS, N_EXPERTS), 1)
    slot = _dot(maskf, (ri < ci).astype(F32))
    lane4 = lax.broadcasted_iota(jnp.int32, (tm, TOP_K), 1)
    d_out = jnp.zeros((tm, TOP_K), F32)
    l_out = jnp.zeros((tm, TOP_K), F32)
    g_out = jnp.zeros((tm, TOP_K), F32)
    for j in range(TOP_K):
        pick = mask & (slot == float(j))
        dj = jnp.sum(jnp.where(pick, dest, 0.0), axis=-1, keepdims=True)
        lj = jnp.sum(jnp.where(pick, local, 0.0), axis=-1, keepdims=True)
        gj = jnp.sum(jnp.where(pick, g, 0.0), axis=-1, keepdims=True)
        d_out = jnp.where(lane4 == j, dj, d_out)
        l_out = jnp.where(lane4 == j, lj, l_out)
        g_out = jnp.where(lane4 == j, gj, g_out)
    d_ref[...] = d_out.astype(jnp.int32)
    l_ref[...] = l_out.astype(jnp.int32)
    gt_ref[...] = g_out


def _dest(gsel, rank, start_row, base, tm, sub):
    n = gsel.shape[0]
    per = tm // sub
    kern = functools.partial(_dest_kernel, tm=tm, sub=sub)
    tok4 = pl.BlockSpec((tm, TOP_K), lambda i: (i, 0))
    return pl.pallas_call(
        kern,
        grid=(n // tm,),
        in_specs=[pl.BlockSpec((tm, N_EXPERTS), lambda i: (i, 0)),
                  pl.BlockSpec((tm, N_EXPERTS), lambda i: (i, 0)),
                  pl.BlockSpec((1, N_EXPERTS), lambda i: (0, 0)),
                  pl.BlockSpec((1, per, N_EXPERTS), lambda i: (i, 0, 0))],
        out_specs=[tok4, tok4, tok4],
        out_shape=[jax.ShapeDtypeStruct((n, TOP_K), jnp.int32),
                   jax.ShapeDtypeStruct((n, TOP_K), jnp.int32),
                   jax.ShapeDtypeStruct((n, TOP_K), F32)],
        compiler_params=_cparams(("arbitrary",)),
        name="dest",
    )(gsel, rank, start_row, base.reshape(n // tm, per, N_EXPERTS))


def _row_copy(src, dst, s_row, d_row, sem):
    return pltpu.make_async_copy(src.at[pl.ds(pl.multiple_of(s_row * ROW_TILES, ROW_TILES), ROW_TILES), :],
                                 dst.at[pl.ds(pl.multiple_of(d_row * ROW_TILES, ROW_TILES), ROW_TILES), :], sem)


def _scatter_kernel(cnt_ref, start_ref, end_ref, nu_ref, dest_ref, ha_ref, hb_ref, xs_ref, sem, *,
                    tm, bm, n_blocks, tiles_a):
    i = pl.program_id(0)

    def scatter_from(ht_ref):
        def body(tt, carry):
            for u in range(ROW_LOOP_UNROLL):
                t = tt * ROW_LOOP_UNROLL + u
                src = ht_ref.at[pl.ds(pl.multiple_of(t * ROW_TILES, ROW_TILES), ROW_TILES), :]
                for k in range(TOP_K):
                    d = dest_ref[t * TOP_K + k]
                    pltpu.make_async_copy(
                        src, xs_ref.at[pl.ds(pl.multiple_of(d * ROW_TILES, ROW_TILES), ROW_TILES), :],
                        sem).start(priority=k % DMA_QUEUES)
            return carry

        lax.fori_loop(0, tm // ROW_LOOP_UNROLL, body, 0)

    @pl.when(i < tiles_a)
    def _():
        scatter_from(ha_ref)

    @pl.when(i >= tiles_a)
    def _():
        scatter_from(hb_ref)

    ht_ref = hb_ref

    def tile_copy(u):
        return pltpu.make_async_copy(
            ht_ref, xs_ref.at[pl.ds(pl.multiple_of(u * (tm * ROW_TILES), tm * ROW_TILES), tm * ROW_TILES), :], sem)

    for k in range(TOP_K):
        tile_copy(0).wait()

    @pl.when(i == pl.num_programs(0) - 1)
    def _():
        per = bm // tm

        def tail_start(u, carry):
            tile_copy(u).start()
            return carry

        def tail_wait(u, carry):
            tile_copy(u).wait()
            return carry

        lax.fori_loop(nu_ref[0] * per, n_blocks * per, tail_start, 0)
        lax.fori_loop(nu_ref[0] * per, n_blocks * per, tail_wait, 0)

        chunk = bm // 4
        assert chunk <= tm

        def chunk_copy(q):
            return pltpu.make_async_copy(
                ht_ref.at[pl.ds(0, chunk * ROW_TILES), :],
                xs_ref.at[pl.ds(pl.multiple_of(q * (chunk * ROW_TILES), chunk * ROW_TILES), chunk * ROW_TILES), :],
                sem)

        def per_expert(e, carry):
            lo = start_ref[e] + cnt_ref[e]
            mid = (lo + chunk - 1) // chunk * chunk
            hi = end_ref[e]

            def fill(r, c2):
                _row_copy(ht_ref, xs_ref, 0, r, sem).start()
                return c2

            def fill_chunk(q, c2):
                chunk_copy(q).start()
                return c2

            lax.fori_loop(lo, mid, fill, 0)
            lax.fori_loop(mid // chunk, hi // chunk, fill_chunk, 0)

            def drain(r, c2):
                _row_copy(ht_ref, xs_ref, 0, r, sem).wait()
                return c2

            def drain_chunk(q, c2):
                chunk_copy(q).wait()
                return c2

            lax.fori_loop(lo, mid, drain, 0)
            lax.fori_loop(mid // chunk, hi // chunk, drain_chunk, 0)
            return carry

        lax.fori_loop(0, N_EXPERTS, per_expert, 0)


def _scatter_rows(ht_a, ht_b, dest_flat, cnt, start, end, n_used, n_blocks, tm, bm):
    rows = n_blocks * bm
    tiles_a, tiles_b = ht_a.shape[0] // (tm * ROW_TILES), ht_b.shape[0] // (tm * ROW_TILES)
    assert bm % tm == 0 and ht_a.shape[0] % (tm * ROW_TILES) == 0 and ht_b.shape[0] % (tm * ROW_TILES) == 0
    assert tiles_b >= 1
    kern = functools.partial(_scatter_kernel, tm=tm, bm=bm, n_blocks=n_blocks, tiles_a=tiles_a)
    return pl.pallas_call(
        kern,
        grid_spec=pltpu.PrefetchScalarGridSpec(
            num_scalar_prefetch=4,
            grid=(tiles_a + tiles_b,),
            in_specs=[pl.BlockSpec((tm * TOP_K,), lambda i, *_: (i,), memory_space=pltpu.SMEM),
                      pl.BlockSpec((tm * ROW_TILES, LANES), lambda i, *_: (jnp.minimum(i, tiles_a - 1), 0)),
                      pl.BlockSpec((tm * ROW_TILES, LANES), lambda i, *_: (jnp.maximum(i - tiles_a, 0), 0))],
            out_specs=pl.BlockSpec(memory_space=pl.ANY),
            scratch_shapes=[pltpu.SemaphoreType.DMA(())]),
        out_shape=jax.ShapeDtypeStruct((rows * ROW_TILES, LANES), F32),
        compiler_params=_cparams(("arbitrary",)),
        name="scatter_rows",
    )(cnt, start, end, n_used, dest_flat, ht_a, ht_b)


def _expert_kernel(be_ref, nx_ref, nv_ref, nu_ref, xs_ref, wg_ref, bg_ref, wd_ref, bd_ref, ys_ref,
                   wg_f, wd_f, wg_s, wd_s, wsem, *, bm):
    j = pl.program_id(0)
    prev = be_ref[jnp.maximum(j - 1, 0)]
    fresh = (j == 0) | (be_ref[j] != prev)
    active = j < nu_ref[0]

    def weight_copies(e):
        return (pltpu.make_async_copy(wg_ref.at[e], wg_f, wsem.at[0]),
                pltpu.make_async_copy(wd_ref.at[e], wd_f, wsem.at[1]))

    @pl.when(j == 0)
    def _():
        for c in weight_copies(be_ref[0]):
            c.start()

    @pl.when(active & fresh)
    def _():
        for c in weight_copies(be_ref[j]):
            c.wait()
        wg_s[...] = wg_f[...].astype(BF16)
        wd_s[...] = wd_f[...].astype(BF16)

        @pl.when(nx_ref[j] != be_ref[j])
        def _():
            for c in weight_copies(nx_ref[j]):
                c.start()

    def run_rows(m):
        x = jnp.concatenate([xs_ref[pl.ds(s, m, stride=ROW_TILES), :] for s in range(ROW_TILES)], axis=-1)
        xb = x.astype(BF16)
        acts = []
        for c in range(EXPERT_COL_CHUNKS):
            w = D_FF // EXPERT_COL_CHUNKS
            lo = c * w
            gate = jnp.dot(xb, wg_s[:, lo:lo + w], preferred_element_type=F32) + bg_ref[0, :, lo:lo + w]
            up = (jnp.dot(xb, wg_s[:, D_FF + lo:D_FF + lo + w], preferred_element_type=F32)
                  + bg_ref[0, :, D_FF + lo:D_FF + lo + w])
            gate = jnp.minimum(gate, SWIGLU_LIMIT)
            up = jnp.clip(up, -SWIGLU_LIMIT, SWIGLU_LIMIT)
            acts.append(((up + 1.0) * (gate * jax.nn.sigmoid(SWIGLU_ALPHA * gate))).astype(BF16))
        act = jnp.concatenate(acts, axis=-1)
        for c in range(EXPERT_COL_CHUNKS):
            w = D_MODEL // EXPERT_COL_CHUNKS
            lo = c * w
            y = jnp.dot(act, wd_s[:, lo:lo + w], preferred_element_type=F32) + bd_ref[0, :, lo:lo + w]
            for s in range(w // LANES):
                ys_ref[pl.ds(lo // LANES + s, m, stride=ROW_TILES), :] = y[:, s * LANES:(s + 1) * LANES]
        if m < bm:
            ys_ref[pl.ds(m * ROW_TILES, (bm - m) * ROW_TILES), :] = jnp.zeros(((bm - m) * ROW_TILES, LANES), F32)

    quarter = bm // 4
    quarters = (nv_ref[j] + quarter - 1) // quarter
    for nq in range(1, 5):
        @pl.when(active & ((quarters == nq) if nq > 1 else (quarters <= 1)))
        def _(nq=nq):
            run_rows(nq * quarter)

    @pl.when(jnp.logical_not(active))
    def _():
        ys_ref[...] = jnp.zeros_like(ys_ref)


def _experts(xs, block_expert, next_expert, block_valid, n_used, w_gu, b_gu, w_down, b_down, bm):
    rows = xs.shape[0] // ROW_TILES
    n_blocks = rows // bm

    def blk(j, be, nx, nv, nu):
        return (jnp.minimum(j, nu[0] - 1), 0)

    def bsel(j, be, nx, nv, nu):
        return (be[j], 0, 0)

    hbm = pl.BlockSpec(memory_space=pl.ANY)
    kern = functools.partial(_expert_kernel, bm=bm)
    return pl.pallas_call(
        kern,
        grid_spec=pltpu.PrefetchScalarGridSpec(
            num_scalar_prefetch=4,
            grid=(n_blocks,),
            in_specs=[pl.BlockSpec((bm * ROW_TILES, LANES), blk), hbm,
                      pl.BlockSpec((1, 1, 2 * D_FF), bsel), hbm,
                      pl.BlockSpec((1, 1, D_MODEL), bsel)],
            out_specs=pl.BlockSpec((bm * ROW_TILES, LANES), lambda j, be, nx, nv, nu: (j, 0)),
            scratch_shapes=[pltpu.VMEM((D_MODEL, 2 * D_FF), F32),
                            pltpu.VMEM((D_FF, D_MODEL), F32),
                            pltpu.VMEM((D_MODEL, 2 * D_FF), BF16),
                            pltpu.VMEM((D_FF, D_MODEL), BF16),
                            pltpu.SemaphoreType.DMA((2,))]),
        out_shape=jax.ShapeDtypeStruct((rows * ROW_TILES, LANES), F32),
        compiler_params=_cparams(("arbitrary",)),
        name="experts",
    )(block_expert, next_expert, block_valid, n_used, xs, w_gu, b_gu.reshape(N_EXPERTS, 1, 2 * D_FF), w_down,
      b_down.reshape(N_EXPERTS, 1, D_MODEL))


def _final_kernel(loc_ref, gate_ref, tcur_ref, tnext_ref, ys_ref, x1_ref, gt_ref, g_ref, o_ref,
                  st_ref, f_ref, sem, *, tm, steps):
    g = pl.program_id(0) * pl.num_programs(1) + pl.program_id(1)
    cur = g % 2
    sizes = [1 << b for b in range(tm.bit_length() - 1, -1, -1)]

    def request(t_ref, slot):
        def per_expert(e, off):
            row = t_ref[e]
            cnt = t_ref[N_EXPERTS + e]
            for size in sizes:
                take = (cnt & size) != 0

                @pl.when(take)
                def _(row=row, off=off, size=size):
                    pltpu.make_async_copy(
                        ys_ref.at[pl.ds(pl.multiple_of(row * ROW_TILES, ROW_TILES), size * ROW_TILES), :],
                        st_ref.at[slot, pl.ds(pl.multiple_of(off * ROW_TILES, ROW_TILES), size * ROW_TILES), :],
                        sem.at[slot]).start()

                step = jnp.where(take, size, 0)
                row = row + step
                off = off + step
            return off

        lax.fori_loop(0, N_EXPERTS, per_expert, 0)

    @pl.when(g == 0)
    def _():
        request(tcur_ref, 0)

    @pl.when(g + 1 < steps)
    def _():
        request(tnext_ref, 1 - cur)

    pltpu.make_async_copy(ys_ref.at[pl.ds(0, tm * TOP_K * ROW_TILES), :], st_ref.at[cur], sem.at[cur]).wait()

    def pick(tt, carry):
        for u in range(ROW_LOOP_UNROLL):
            t = tt * ROW_LOOP_UNROLL + u
            for k in range(TOP_K):
                at = pl.multiple_of(loc_ref[t * TOP_K + k] * ROW_TILES, ROW_TILES)
                f_ref[k, pl.ds(pl.multiple_of(t * ROW_TILES, ROW_TILES), ROW_TILES), :] = (
                    st_ref[cur, pl.ds(at, ROW_TILES), :])
        return carry

    lax.fori_loop(0, tm // ROW_LOOP_UNROLL, pick, 0)
    gates = gate_ref[...]
    cols = []
    for s in range(ROW_TILES):
        acc = None
        for k in range(TOP_K):
            part = f_ref[k, pl.ds(s, tm, stride=ROW_TILES), :] * gates[:, k:k + 1]
            acc = part if acc is None else acc + part
        cols.append(acc)
    f = jnp.concatenate(cols, axis=-1)
    o_ref[0] = x1_ref[0] + gt_ref[0] * _rms(f, g_ref[...])


def _final(ys, local_flat, gates, run_table, x1, gt, g_post_ffn, tok_off, tm, per_token):
    bsz, t, _ = x1.shape
    tiles = t // tm
    off = tok_off // tm
    steps = bsz * tiles
    kern = functools.partial(_final_kernel, tm=tm, steps=steps)

    def smem(ahead):
        return pl.BlockSpec((tm * TOP_K,), lambda b, i: (off + jnp.minimum(b * tiles + i + ahead, steps - 1),),
                            memory_space=pltpu.SMEM)

    return pl.pallas_call(
        kern,
        grid=(bsz, tiles),
        in_specs=[smem(0), pl.BlockSpec((tm, TOP_K), lambda b, i: (off + b * tiles + i, 0)), smem(0), smem(1),
                  pl.BlockSpec(memory_space=pl.ANY),
                  pl.BlockSpec((1, tm, D_MODEL), lambda b, i: (b, i, 0)),
                  _mod_spec(per_token, tm),
                  pl.BlockSpec((1, D_MODEL), lambda b, i: (0, 0))],
        out_specs=pl.BlockSpec((1, tm, D_MODEL), lambda b, i: (b, i, 0)),
        out_shape=jax.ShapeDtypeStruct((bsz, t, D_MODEL), F32),
        scratch_shapes=[pltpu.VMEM((2, tm * TOP_K * ROW_TILES, LANES), F32),
                        pltpu.VMEM((TOP_K, tm * ROW_TILES, LANES), F32),
                        pltpu.SemaphoreType.DMA((2,))],
        compiler_params=_cparams(("arbitrary", "arbitrary")),
        name="final",
    )(local_flat, gates, run_table, run_table, ys, x1, gt, g_post_ffn.reshape(1, D_MODEL))


def _moe(ht_a, ht_b, gsel, rank, counts, w_gu, b_gu, w_down, b_down, tm, bm):
    n = gsel.shape[0]
    cnt = counts[:, 0].astype(jnp.int32)
    nblk = (cnt + bm - 1) // bm
    blk_end = jnp.cumsum(nblk)
    blk_start = blk_end - nblk
    n_blocks = -(-(n * TOP_K) // bm) + N_EXPERTS
    blocks = jnp.arange(n_blocks, dtype=jnp.int32)
    block_expert = jnp.minimum(jnp.sum((blk_end[None, :] <= blocks[:, None]).astype(jnp.int32), axis=1),
                               N_EXPERTS - 1).astype(jnp.int32)
    ids = jnp.arange(N_EXPERTS, dtype=jnp.int32)
    later = jnp.where((nblk[None, :] > 0) & (ids[None, :] > ids[:, None]), ids[None, :], N_EXPERTS)
    nearest = jnp.min(later, axis=1)
    next_used = jnp.where(nearest < N_EXPERTS, nearest, ids).astype(jnp.int32)
    n_used = blk_end[-1:].astype(jnp.int32)
    row_start = (blk_start * bm).astype(jnp.int32)
    row_end = (blk_end * bm).astype(jnp.int32)
    before = rank[::tm]
    run_len = jnp.concatenate([before[1:], counts.reshape(1, N_EXPERTS)], axis=0) - before
    run_row = row_start.astype(F32)[None, :] + before
    run_off = jnp.cumsum(run_len, axis=1) - run_len
    tiles = n // tm
    run_table = jnp.zeros((tiles, tm * TOP_K), jnp.int32)
    run_table = run_table.at[:, :N_EXPERTS].set(run_row.astype(jnp.int32))
    run_table = run_table.at[:, N_EXPERTS:2 * N_EXPERTS].set(run_len.astype(jnp.int32)).reshape(tiles * tm * TOP_K)
    dest, local, gates = _dest(gsel, rank, row_start.astype(F32).reshape(1, N_EXPERTS), run_row - run_off,
                               DEST_TILE, tm)
    dest_flat = dest.reshape(n * TOP_K)
    xs = _scatter_rows(ht_a, ht_b, dest_flat, cnt, row_start, row_end, n_used, n_blocks, SCATTER_TILE, bm)
    owns = (blocks[:, None] >= blk_start[None, :]) & (blocks[:, None] < blk_end[None, :])

    def per_block(table):
        return jnp.sum(jnp.where(owns, table[None, :], 0), axis=1).astype(jnp.int32)

    block_valid = jnp.clip(per_block(cnt) - (blocks - per_block(blk_start)) * bm, 0, bm).astype(jnp.int32)
    ys = _experts(xs, block_expert, per_block(next_used), block_valid, n_used, w_gu, b_gu, w_down, b_down, bm)
    return ys, local.reshape(n * TOP_K), gates, run_table


def _mix_path(x, mod, pool_buf, conv_buf, s0, start_pos, counts0, lp, tm, nb, per_token):
    (g_pre_mix, g_post_mix, g_pre_ffn, w_in_p, w_bd, pool_scale, conv_w, alog_row, dtb_row, norm_w,
     w_out_b, router_w, router_b) = lp
    bsz, t, _ = x.shape
    sh_mix, sc_mix, gt_mix, sh_ffn, sc_ffn, gt_ffn = (mod[:, i] for i in range(6))

    def shape_mod(m):
        if per_token:
            return jnp.repeat(m, t, axis=0).reshape(1, bsz * t, D_MODEL)
        return m.reshape(bsz, 1, D_MODEL)

    sh_mix, sc_mix, gt_mix, sh_ffn, sc_ffn, gt_ffn = map(shape_mod, (sh_mix, sc_mix, gt_mix, sh_ffn, sc_ffn, gt_ffn))
    xf = x.reshape(1, bsz * t, D_MODEL) if per_token else x
    proj = _in_proj(xf, sc_mix, sh_mix, g_pre_mix, w_in_p, PROJ_TILE, per_token).reshape(bsz, t, PROJ_WIDTH)
    tseq = min(tm, t)
    o_gdn, s_new = _gdn(proj, conv_buf, s0, conv_w, alog_row, dtb_row, norm_w, nb, tseq)
    pool_out = _pool(proj, pool_buf, w_bd, pool_scale, nb * POOL_SEQS if per_token else nb, min(POOL_TILE, t),
                     start_pos)
    fl = (lambda a: a.reshape(1, bsz * t, a.shape[-1])) if per_token else (lambda a: a)
    x1, ht, gsel, rank, counts = _out_proj(xf, fl(pool_out), fl(o_gdn), gt_mix, sc_ffn, sh_ffn, g_post_mix,
                                           g_pre_ffn, w_out_b, router_w, router_b, counts0, OUT_TILE, per_token)

    def last_rows(buf, off, width):
        keep = buf.shape[1]
        if t >= keep:
            return proj[:, t - keep:, off:off + width]
        return jnp.concatenate([buf[:, t:], proj[:, :, off:off + width]], axis=1)

    pool_new = last_rows(pool_buf, OFF_U, POOL_WIDTH)
    conv_new = last_rows(conv_buf, OFF_QKV, CONV_CH)
    return x1, ht, gsel, rank, counts, gt_ffn, pool_new, conv_new, s_new


def kernel(x_prompt, x_sample, c_prompt, c_sample, state_pool, state_conv, state_ssm, w_ada, b_ada, g_pre_mix, g_post_mix, g_pre_ffn, g_post_ffn, w_in, w_pool, pool_scale, conv_w, a_log, dt_bias, gdn_norm_w, w_out, router_w, router_b, w_gu, b_gu, w_down, b_down):
    depth = w_ada.shape[0]
    bp, tp, _ = x_prompt.shape
    bs, ts, _ = x_sample.shape
    tm = TOKEN_TILE
    y_p, y_s = x_prompt, x_sample
    outs = [[] for _ in range(6)]
    for l in range(depth):
        mod = _ada(jnp.concatenate([c_prompt, c_sample], axis=0), w_ada[l], b_ada[l]).reshape(bp + bs, 6, D_MODEL)
        w_in_p = jnp.swapaxes(w_in[l], 0, 1)
        w_bd = jax.scipy.linalg.block_diag(*[w_pool[l, g] for g in range(len(POOL_WINDOWS))])
        pad_row = lambda v: jnp.pad(v, (0, LANES - v.shape[0])).reshape(1, LANES)
        lp = (g_pre_mix[l], g_post_mix[l], g_pre_ffn[l], w_in_p, w_bd, pool_scale[l], conv_w[l],
              pad_row(a_log[l]), pad_row(dt_bias[l]), gdn_norm_w[l].reshape(1, HEAD_DIM),
              w_out[l].astype(BF16), router_w[l], router_b[l])
        zero_pool = jnp.zeros((bp, POOL_BUF, POOL_WIDTH), F32)
        zero_conv = jnp.zeros((bp, CONV_WIDTH - 1, CONV_CH), F32)
        zero_ssm = (jnp.zeros((1, bp, GDN_HEADS, HEAD_DIM, HEAD_DIM), F32), 0)
        no_counts = jnp.zeros((N_EXPERTS, 1), F32)
        x1p, htp, gp, rp, cnt_p, gtp, pp, cp, sp = _mix_path(
            y_p, mod[:bp], zero_pool, zero_conv, zero_ssm, 0, no_counts, lp, tm, 1, False)
        x1s, hts, gs, rs, counts, gts, ps, cs, ss = _mix_path(
            y_s, mod[bp:], state_pool[l], state_conv[l], (state_ssm, l), PAST_LEN, cnt_p, lp, tm, 8, True)
        gsel = jnp.concatenate([gp, gs], axis=0)
        rank = jnp.concatenate([rp, rs], axis=0)
        ys, local, gates, runs = _moe(htp, hts, gsel, rank, counts, w_gu[l], b_gu[l], w_down[l], b_down[l],
                                      FINAL_TILE, MOE_BLOCK)
        y_p = _final(ys, local, gates, runs, x1p, gtp, g_post_ffn[l], 0, FINAL_TILE, False)
        y_s = _final(ys, local, gates, runs, x1s, gts, g_post_ffn[l], bp * tp, FINAL_TILE,
                     True).reshape(bs, ts, D_MODEL)
        for lst, val in zip(outs, (pp, cp, sp, ps, cs, ss)):
            lst.append(val)
    return (y_p, y_s) + tuple(o[0][None] if depth == 1 else jnp.stack(o) for o in outs)
```

```python
import functools
import math

import jax
import jax.numpy as jnp
from jax import lax
from jax.experimental import pallas as pl
from jax.experimental.pallas import tpu as pltpu

F32 = jnp.float32
BF16 = jnp.bfloat16

D_MODEL = 1024
PAST_LEN = 16384
POOL_WINDOWS = (2, 4, 8, 16)
POOL_WIDTH = D_MODEL // 4
POOL_GROUP = POOL_WIDTH // len(POOL_WINDOWS)
POOL_BUF = max(POOL_WINDOWS) - 1
GDN_WIDTH = D_MODEL - POOL_WIDTH
HEAD_DIM = 128
GDN_HEADS = GDN_WIDTH // HEAD_DIM
CONV_WIDTH = 4
CONV_CH = 3 * GDN_WIDTH
GDN_CHUNK = 64
N_EXPERTS = 32
TOP_K = 4
D_FF = D_MODEL
SWIGLU_LIMIT = 7.0
SWIGLU_ALPHA = 1.702
EPS = 1e-6

SUBLANES = 8
LANES = 128
ROW_TILES = D_MODEL // LANES
OFF_QKV = 0
OFF_Z = OFF_QKV + CONV_CH
OFF_U = OFF_Z + GDN_WIDTH
OFF_AB = OFF_U + POOL_WIDTH
PROJ_WIDTH = OFF_AB + LANES
assert OFF_Z % GDN_WIDTH == 0 and OFF_U % POOL_WIDTH == 0 and OFF_AB % LANES == 0
TOKEN_TILE = 256
PROJ_TILE = 512
SCATTER_TILE = 512
OUT_TILE = 512
FINAL_TILE = 512
DEST_TILE = 5632
PAD_CHUNK = 64
POOL_TILE = 1024
POOL_SEQS = 4
GDN_ITEM_GROUPS = 1
MOE_BLOCK = 1024
EXPERT_COL_CHUNKS = 2
DMA_QUEUES = 2
ROW_LOOP_UNROLL = 4
VMEM_LIMIT = 56 * 1024 * 1024


def _cparams(sem):
    return pltpu.CompilerParams(dimension_semantics=sem, vmem_limit_bytes=VMEM_LIMIT)


def _dot(a, b):
    return jnp.dot(a.astype(BF16), b.astype(BF16), preferred_element_type=F32)


def _dot_nt(a, b):
    return lax.dot_general(a.astype(BF16), b.astype(BF16), (((1,), (1,)), ((), ())),
                           preferred_element_type=F32)


def _silu(x):
    return x * jax.nn.sigmoid(x)


def _rms(x, g):
    return x * lax.rsqrt(jnp.mean(x * x, axis=-1, keepdims=True) + EPS) * g


def _ada_kernel(c_ref, w_ref, b_ref, o_ref):
    o_ref[...] = _dot(_silu(c_ref[...]), w_ref[...]) + b_ref[...]


def _ada(c, w_ada, b_ada):
    n = c.shape[0]
    cols = w_ada.shape[1]
    blk = D_MODEL
    return pl.pallas_call(
        _ada_kernel,
        grid=(cols // blk,),
        in_specs=[pl.BlockSpec((n, D_MODEL), lambda j: (0, 0)),
                  pl.BlockSpec((D_MODEL, blk), lambda j: (0, j)),
                  pl.BlockSpec((1, blk), lambda j: (0, j))],
        out_specs=pl.BlockSpec((n, blk), lambda j: (0, j)),
        out_shape=jax.ShapeDtypeStruct((n, cols), F32),
        compiler_params=_cparams(("arbitrary",)),
        name="ada",
    )(c, w_ada, b_ada.reshape(1, cols))


def _in_kernel(x_ref, sc_ref, sh_ref, g_ref, wt_ref, o_ref, wb):
    @pl.when((pl.program_id(0) == 0) & (pl.program_id(1) == 0))
    def _():
        o1, o2, o3 = POOL_WIDTH, POOL_WIDTH + CONV_CH, POOL_WIDTH + CONV_CH + GDN_WIDTH
        wb[OFF_QKV:OFF_QKV + CONV_CH] = wt_ref[o1:o2].astype(BF16)
        wb[OFF_Z:OFF_Z + GDN_WIDTH] = wt_ref[o2:o3].astype(BF16)
        wb[OFF_U:OFF_U + POOL_WIDTH] = wt_ref[:o1].astype(BF16)
        tail = wt_ref.shape[0] - o3
        wb[OFF_AB:] = jnp.concatenate(
            [wt_ref[o3:], jnp.zeros((LANES - tail, D_MODEL), F32)], axis=0).astype(BF16)

    h = _rms(x_ref[0], g_ref[...]) * (1.0 + sc_ref[0]) + sh_ref[0]
    o_ref[0] = _dot_nt(h, wb[...])


def _mod_spec(per_token, tm):
    if per_token:
        return pl.BlockSpec((1, tm, D_MODEL), lambda b, t: (0, t, 0))
    return pl.BlockSpec((1, 1, D_MODEL), lambda b, t: (b, 0, 0))


def _in_proj(x, sc, sh, g, w_in, tm, per_token):
    bsz, t, _ = x.shape
    return pl.pallas_call(
        _in_kernel,
        grid=(bsz, t // tm),
        in_specs=[pl.BlockSpec((1, tm, D_MODEL), lambda b, i: (b, i, 0)),
                  _mod_spec(per_token, tm), _mod_spec(per_token, tm),
                  pl.BlockSpec((1, D_MODEL), lambda b, i: (0, 0)),
                  pl.BlockSpec(w_in.shape, lambda b, i: (0, 0), pipeline_mode=pl.Buffered(1))],
        out_specs=pl.BlockSpec((1, tm, PROJ_WIDTH), lambda b, i: (b, i, 0)),
        out_shape=jax.ShapeDtypeStruct((bsz, t, PROJ_WIDTH), F32),
        scratch_shapes=[pltpu.VMEM((PROJ_WIDTH, D_MODEL), BF16)],
        compiler_params=_cparams(("arbitrary", "arbitrary")),
        name="in_proj",
    )(x, sc, sh, g.reshape(1, D_MODEL), w_in)


def _cumsum_rows(x):
    n = x.shape[0]
    row = lax.broadcasted_iota(jnp.int32, x.shape, 0)
    s = 1
    while s < n:
        x = x + jnp.where(row >= s, pltpu.roll(x, s, axis=0), 0.0)
        s *= 2
    return x


def _gdn_kernel(qkv_ref, z_ref, ab_ref, cs_ref, cw_ref, s0_ref, alog_ref, dtb_ref, nw_ref, o_ref, sn_ref,
                s_scr, prev_scr, *, nb, tm, tmp, chunk, group_items):
    ti = pl.program_id(1)
    nt = pl.num_programs(1)
    n_chunks = tmp // chunk
    n_sq = int(math.log2(chunk)) - 1

    @pl.when(ti == 0)
    def _():
        s_scr[...] = s0_ref[...]
        for b in range(nb):
            prev_scr[b] = jnp.concatenate(
                [jnp.zeros((SUBLANES - (CONV_WIDTH - 1), CONV_CH), F32), cs_ref[b]], axis=0)

    rowi = lax.broadcasted_iota(jnp.int32, (tmp, LANES), 0)
    ri = lax.broadcasted_iota(jnp.int32, (chunk, chunk), 0)
    ci = lax.broadcasted_iota(jnp.int32, (chunk, chunk), 1)
    tril = ri >= ci
    strict = ri > ci
    eye = (ri == ci).astype(F32)
    neg_a = -jnp.exp(alog_ref[...])
    dtb = dtb_ref[...]
    nw = nw_ref[...]

    def padrows(x):
        if tm == tmp:
            return x
        return jnp.concatenate([x, jnp.zeros((tmp - tm, x.shape[1]), F32)], axis=0)

    cw = cw_ref[...]
    valid = rowi < tm
    items = []
    for b in range(nb):
        x = padrows(qkv_ref[b])
        e = jnp.concatenate([prev_scr[b], x], axis=0)
        acc = x * cw[CONV_WIDTH - 1:CONV_WIDTH]
        for d in range(1, CONV_WIDTH):
            acc = acc + pltpu.roll(e, d, axis=0)[SUBLANES:] * cw[CONV_WIDTH - 1 - d:CONV_WIDTH - d]
        if tm >= SUBLANES:
            prev_scr[b] = x[tm - SUBLANES:tm]
        y = _silu(acc)
        ab = padrows(ab_ref[b])
        g_all = jnp.where(valid, neg_a * jax.nn.softplus(ab + dtb), 0.0)
        beta_all = jnp.where(valid, jax.nn.sigmoid(ab), 0.0)
        for h in range(GDN_HEADS):
            qc = y[:, h * HEAD_DIM:(h + 1) * HEAD_DIM]
            kc = y[:, GDN_WIDTH + h * HEAD_DIM:GDN_WIDTH + (h + 1) * HEAD_DIM]
            v = y[:, 2 * GDN_WIDTH + h * HEAD_DIM:2 * GDN_WIDTH + (h + 1) * HEAD_DIM]
            q = qc * lax.rsqrt(jnp.sum(qc * qc, axis=-1, keepdims=True) + EPS) * (HEAD_DIM ** -0.5)
            k = kc * lax.rsqrt(jnp.sum(kc * kc, axis=-1, keepdims=True) + EPS)
            gb = jnp.broadcast_to(g_all[:, h:h + 1], (tmp, LANES))
            bb = jnp.broadcast_to(beta_all[:, GDN_HEADS + h:GDN_HEADS + h + 1], (tmp, LANES))
            for c in range(n_chunks):
                sl = slice(c * chunk, (c + 1) * chunk)
                items.append(dict(b=b, h=h, c=c, q=q[sl], k=k[sl], v=v[sl], bb=bb[sl], gb=gb[sl]))

    def within_chunk(group):
        for it in group:
            gcb = _cumsum_rows(it["gb"])
            gct = jnp.transpose(gcb)[:chunk]
            diff = gcb[:, :chunk] - gct
            it["decay"] = jnp.where(tril, jnp.exp(jnp.where(tril, diff, 0.0)), 0.0)
            it["kb"] = it["k"] * it["bb"]
            it["eg"] = jnp.exp(gcb)
            gl = gcb[chunk - 1:chunk]
            it["egl"] = jnp.exp(gl)
            it["kdt"] = jnp.transpose(it["k"] * jnp.exp(gl - gcb))
        for it in group:
            kq = _dot_nt(jnp.concatenate([it["kb"], it["q"]], axis=0), it["k"])
            it["a"] = jnp.where(strict, kq[:chunk] * it["decay"], 0.0)
            it["qk"] = jnp.where(tril, kq[chunk:] * it["decay"], 0.0)
        for it in group:
            it["inv"] = eye - it["a"]
            it["p"] = _dot(it["a"], it["a"])
        for lvl in range(n_sq):
            for it in group:
                if lvl < n_sq - 1:
                    r = _dot(jnp.concatenate([it["inv"], it["p"]], axis=0), it["p"])
                    it["inv"] = it["inv"] + r[:chunk]
                    it["p"] = r[chunk:]
                else:
                    it["inv"] = it["inv"] + _dot(it["inv"], it["p"])
        for it in group:
            it["wu"] = _dot(it["inv"], jnp.concatenate([it["kb"] * it["eg"], it["v"] * it["bb"]], axis=1))
        for it in group:
            mn = _dot(it["kdt"], it["wu"])
            qo = _dot(it["qk"], it["wu"])
            it["mp"], it["nc"] = mn[:, :HEAD_DIM], mn[:, HEAD_DIM:]
            it["qp"] = it["q"] * it["eg"] - qo[:, :HEAD_DIM]
            it["oc"] = qo[:, HEAD_DIM:]

    for g0 in range(0, len(items), group_items):
        within_chunk(items[g0:g0 + group_items])

    state = {(b, h): s_scr[b, h] for b in range(nb) for h in range(GDN_HEADS)}
    for c in range(n_chunks):
        for it in items:
            if it["c"] != c:
                continue
            s = state[(it["b"], it["h"])]
            r = _dot(jnp.concatenate([it["qp"], it["mp"]], axis=0), s)
            it["o"] = r[:chunk] + it["oc"]
            state[(it["b"], it["h"])] = s * it["egl"] - r[chunk:] + it["nc"]
    for (b, h), s in state.items():
        s_scr[b, h] = s

    for b in range(nb):
        zf = padrows(z_ref[b])
        for h in range(GDN_HEADS):
            outs = [it["o"] for it in items if it["b"] == b and it["h"] == h]
            o = outs[0] if n_chunks == 1 else jnp.concatenate(outs, axis=0)
            o = (o * lax.rsqrt(jnp.mean(o * o, axis=-1, keepdims=True) + EPS) * nw
                 * _silu(zf[:, h * HEAD_DIM:(h + 1) * HEAD_DIM]))
            o_ref[b, :, h * HEAD_DIM:(h + 1) * HEAD_DIM] = o[:tm]

    @pl.when(ti == nt - 1)
    def _():
        sn_ref[...] = s_scr[...]


def _gdn(proj, state_conv, s0, conv_w, alog_row, dtb_row, norm_w, nb, tm):
    bsz, t, _ = proj.shape
    tmp = max(tm, SUBLANES)
    chunk = min(GDN_CHUNK, tmp)
    row = pl.BlockSpec((1, LANES), lambda b, i: (0, 0))
    s_all, layer = s0
    sspec = pl.BlockSpec((nb, GDN_HEADS, HEAD_DIM, HEAD_DIM), lambda b, i: (b, 0, 0, 0))
    s_in = pl.BlockSpec((None, nb, GDN_HEADS, HEAD_DIM, HEAD_DIM), lambda b, i: (layer, b, 0, 0, 0))
    n_items = nb * GDN_HEADS * (tmp // chunk)
    kern = functools.partial(_gdn_kernel, nb=nb, tm=tm, tmp=tmp, chunk=chunk,
                             group_items=n_items // GDN_ITEM_GROUPS)
    return pl.pallas_call(
        kern,
        grid=(bsz // nb, t // tm),
        in_specs=[pl.BlockSpec((nb, tm, CONV_CH), lambda b, i: (b, i, OFF_QKV // CONV_CH)),
                  pl.BlockSpec((nb, tm, GDN_WIDTH), lambda b, i: (b, i, OFF_Z // GDN_WIDTH)),
                  pl.BlockSpec((nb, tm, LANES), lambda b, i: (b, i, OFF_AB // LANES)),
                  pl.BlockSpec((nb, CONV_WIDTH - 1, CONV_CH), lambda b, i: (b, 0, 0)),
                  pl.BlockSpec((CONV_WIDTH, CONV_CH), lambda b, i: (0, 0)),
                  s_in, row, row, row],
        out_specs=[pl.BlockSpec((nb, tm, GDN_WIDTH), lambda b, i: (b, i, 0)), sspec],
        out_shape=[jax.ShapeDtypeStruct((bsz, t, GDN_WIDTH), F32),
                   jax.ShapeDtypeStruct(s_all.shape[1:], F32)],
        scratch_shapes=[pltpu.VMEM((nb, GDN_HEADS, HEAD_DIM, HEAD_DIM), F32),
                        pltpu.VMEM((nb, SUBLANES, CONV_CH), F32)],
        compiler_params=_cparams(("arbitrary", "arbitrary")),
        name="gdn",
    )(proj, proj, proj, state_conv, conv_w, s_all, alog_row, dtb_row, norm_w)


def _pool_kernel(u_ref, st_ref, w_ref, sc_ref, o_ref, prev_scr, *, nb, tm, tmp, start_pos):
    ti = pl.program_id(1)
    halo = 2 * SUBLANES

    @pl.when(ti == 0)
    def _():
        for b in range(nb):
            prev_scr[b] = jnp.concatenate([jnp.zeros((halo - POOL_BUF, POOL_WIDTH), F32), st_ref[b]], axis=0)

    lane = lax.broadcasted_iota(jnp.int32, (tmp, POOL_WIDTH), 1)
    rowi = lax.broadcasted_iota(jnp.int32, (tmp, POOL_WIDTH), 0)
    pos1 = start_pos + ti * tm + rowi + 1
    grp = lane // POOL_GROUP
    win = jnp.where(grp == 0, POOL_WINDOWS[0],
                    jnp.where(grp == 1, POOL_WINDOWS[1], jnp.where(grp == 2, POOL_WINDOWS[2], POOL_WINDOWS[3])))
    cnt = jnp.minimum(pos1, win).astype(F32)
    for b in range(nb):
        x = u_ref[b]
        if tm != tmp:
            x = jnp.concatenate([x, jnp.zeros((tmp - tm, POOL_WIDTH), F32)], axis=0)
        e = jnp.concatenate([prev_scr[b], x], axis=0)
        sums = []
        s = e
        for w in POOL_WINDOWS:
            s = s + pltpu.roll(s, w // 2, axis=0)
            sums.append(s[halo:])
        sel = jnp.where(grp == 0, sums[0], jnp.where(grp == 1, sums[1], jnp.where(grp == 2, sums[2], sums[3])))
        pooled = sel / cnt - x
        out = _dot(pooled, w_ref[...]) * sc_ref[...]
        o_ref[b] = out[:tm]
        if tm >= halo:
            prev_scr[b] = x[tm - halo:tm]


def _pool(proj, state_pool, w_bd, pool_scale, nb, tm, start_pos):
    bsz, t, _ = proj.shape
    tmp = max(tm, SUBLANES)
    kern = functools.partial(_pool_kernel, nb=nb, tm=tm, tmp=tmp, start_pos=start_pos)
    return pl.pallas_call(
        kern,
        grid=(bsz // nb, t // tm),
        in_specs=[pl.BlockSpec((nb, tm, POOL_WIDTH), lambda b, i: (b, i, OFF_U // POOL_WIDTH)),
                  pl.BlockSpec((nb, POOL_BUF, POOL_WIDTH), lambda b, i: (b, 0, 0)),
                  pl.BlockSpec((POOL_WIDTH, POOL_WIDTH), lambda b, i: (0, 0)),
                  pl.BlockSpec((1, POOL_WIDTH), lambda b, i: (0, 0))],
        out_specs=pl.BlockSpec((nb, tm, POOL_WIDTH), lambda b, i: (b, i, 0)),
        out_shape=jax.ShapeDtypeStruct((bsz, t, POOL_WIDTH), F32),
        scratch_shapes=[pltpu.VMEM((nb, 2 * SUBLANES, POOL_WIDTH), F32)],
        compiler_params=_cparams(("arbitrary", "arbitrary")),
        name="pool",
    )(proj, state_pool, w_bd, pool_scale.reshape(1, POOL_WIDTH))


def _out_kernel(x_ref, po_ref, og_ref, gt_ref, sc_ref, sh_ref, gpm_ref, gpf_ref, wo_ref, rw_ref, rb_ref, c0_ref,
                x1_ref, ht_ref, g_ref, r_ref, cnt_ref, carry, *, tm):
    @pl.when((pl.program_id(0) == 0) & (pl.program_id(1) == 0))
    def _():
        carry[...] = c0_ref[...]

    mix = _dot(po_ref[0], wo_ref[:POOL_WIDTH]) + _dot(og_ref[0], wo_ref[POOL_WIDTH:])
    x1 = x_ref[0] + gt_ref[0] * _rms(mix, gpm_ref[...])
    x1_ref[0] = x1
    h = _rms(x1, gpf_ref[...]) * (1.0 + sc_ref[0]) + sh_ref[0]
    for s in range(ROW_TILES):
        ht_ref[pl.ds(s, tm, stride=ROW_TILES), :] = h[:, s * LANES:(s + 1) * LANES]
    logits = _dot_nt(rw_ref[...], h) + rb_ref[...]
    eid = lax.broadcasted_iota(jnp.int32, logits.shape, 0).astype(F32)
    work = logits
    sel = jnp.zeros(logits.shape, jnp.bool_)
    top = None
    den = jnp.zeros((1, tm), F32)
    for j in range(TOP_K):
        m = jnp.max(work, axis=0, keepdims=True)
        idx = jnp.min(jnp.where(work == m, eid, float(N_EXPERTS)), axis=0, keepdims=True)
        hit = eid == idx
        sel = sel | hit
        work = jnp.where(hit, -jnp.inf, work)
        if j == 0:
            top = m
        den = den + jnp.exp(m - top)
    gates = jnp.exp(logits - top) / den
    g_ref[...] = jnp.transpose(jnp.where(sel, gates, -1.0))
    mask = sel.astype(F32)
    ri = lax.broadcasted_iota(jnp.int32, (tm, tm), 0)
    ci = lax.broadcasted_iota(jnp.int32, (tm, tm), 1)
    c = carry[...]
    r_ref[...] = jnp.transpose(_dot(mask, (ri < ci).astype(F32)) + c)
    c = c + jnp.sum(mask, axis=1, keepdims=True)
    carry[...] = c
    cnt_ref[...] = c


def _out_proj(x, pool_out, o_gdn, gt, sc, sh, g_post_mix, g_pre_ffn, w_out_b, router_w, router_b, counts0,
              tm, per_token):
    bsz, t, _ = x.shape
    n = bsz * t
    tiles = t // tm

    def tok(width):
        return pl.BlockSpec((1, tm, width), lambda b, i: (b, i, 0))

    def const(shape):
        return pl.BlockSpec(shape, lambda b, i: tuple(0 for _ in shape))

    kern = functools.partial(_out_kernel, tm=tm)
    return pl.pallas_call(
        kern,
        grid=(bsz, tiles),
        in_specs=[tok(D_MODEL), tok(POOL_WIDTH), tok(GDN_WIDTH),
                  _mod_spec(per_token, tm), _mod_spec(per_token, tm), _mod_spec(per_token, tm),
                  const((1, D_MODEL)), const((1, D_MODEL)), const((D_MODEL, D_MODEL)),
                  const((N_EXPERTS, D_MODEL)), const((N_EXPERTS, 1)), const((N_EXPERTS, 1))],
        out_specs=[tok(D_MODEL),
                   pl.BlockSpec((tm * ROW_TILES, LANES), lambda b, i: (b * tiles + i, 0)),
                   pl.BlockSpec((tm, N_EXPERTS), lambda b, i: (b * tiles + i, 0)),
                   pl.BlockSpec((tm, N_EXPERTS), lambda b, i: (b * tiles + i, 0)),
                   const((N_EXPERTS, 1))],
        out_shape=[jax.ShapeDtypeStruct((bsz, t, D_MODEL), F32),
                   jax.ShapeDtypeStruct((n * ROW_TILES, LANES), F32),
                   jax.ShapeDtypeStruct((n, N_EXPERTS), F32),
                   jax.ShapeDtypeStruct((n, N_EXPERTS), F32),
                   jax.ShapeDtypeStruct((N_EXPERTS, 1), F32)],
        scratch_shapes=[pltpu.VMEM((N_EXPERTS, 1), F32)],
        compiler_params=_cparams(("arbitrary", "arbitrary")),
        name="out_proj",
    )(x, pool_out, o_gdn, gt, sc, sh, g_post_mix.reshape(1, D_MODEL), g_pre_ffn.reshape(1, D_MODEL),
      w_out_b, jnp.swapaxes(router_w, 0, 1), router_b.reshape(N_EXPERTS, 1), counts0)


def _dest_kernel(g_ref, r_ref, start_ref, base_ref, d_ref, l_ref, gt_ref, *, tm, sub):
    g = g_ref[...]
    mask = g >= 0.0
    maskf = mask.astype(F32)
    dest = start_ref[...] + r_ref[...]
    base = jnp.concatenate([jnp.broadcast_to(base_ref[0, q:q + 1, :], (sub, N_EXPERTS)) for q in range(tm // sub)],
                           axis=0)
    local = dest - base
    ri = lax.broadcasted_iota(jnp.int32, (N_EXPERTS, N_EXPERTS), 0)
    ci = lax.broadcasted_iota(jnp.int32, (N_EXPERTS, N_EXPERTS), 1)
    slot = _dot(maskf, (ri < ci).astype(F32))
    lane4 = lax.broadcasted_iota(jnp.int32, (tm, TOP_K), 1)
    d_out = jnp.zeros((tm, TOP_K), F32)
    l_out = jnp.zeros((tm, TOP_K), F32)
    g_out = jnp.zeros((tm, TOP_K), F32)
    for j in range(TOP_K):
        pick = mask & (slot == float(j))
        dj = jnp.sum(jnp.where(pick, dest, 0.0), axis=-1, keepdims=True)
        lj = jnp.sum(jnp.where(pick, local, 0.0), axis=-1, keepdims=True)
        gj = jnp.sum(jnp.where(pick, g, 0.0), axis=-1, keepdims=True)
        d_out = jnp.where(lane4 == j, dj, d_out)
        l_out = jnp.where(lane4 == j, lj, l_out)
        g_out = jnp.where(lane4 == j, gj, g_out)
    d_ref[...] = d_out.astype(jnp.int32)
    l_ref[...] = l_out.astype(jnp.int32)
    gt_ref[...] = g_out


def _dest(gsel, rank, start_row, base, tm, sub):
    n = gsel.shape[0]
    per = tm // sub
    kern = functools.partial(_dest_kernel, tm=tm, sub=sub)
    tok4 = pl.BlockSpec((tm, TOP_K), lambda i: (i, 0))
    return pl.pallas_call(
        kern,
        grid=(n // tm,),
        in_specs=[pl.BlockSpec((tm, N_EXPERTS), lambda i: (i, 0)),
                  pl.BlockSpec((tm, N_EXPERTS), lambda i: (i, 0)),
                  pl.BlockSpec((1, N_EXPERTS), lambda i: (0, 0)),
                  pl.BlockSpec((1, per, N_EXPERTS), lambda i: (i, 0, 0))],
        out_specs=[tok4, tok4, tok4],
        out_shape=[jax.ShapeDtypeStruct((n, TOP_K), jnp.int32),
                   jax.ShapeDtypeStruct((n, TOP_K), jnp.int32),
                   jax.ShapeDtypeStruct((n, TOP_K), F32)],
        compiler_params=_cparams(("arbitrary",)),
        name="dest",
    )(gsel, rank, start_row, base.reshape(n // tm, per, N_EXPERTS))


def _row_copy(src, dst, s_row, d_row, sem):
    return pltpu.make_async_copy(src.at[pl.ds(pl.multiple_of(s_row * ROW_TILES, ROW_TILES), ROW_TILES), :],
                                 dst.at[pl.ds(pl.multiple_of(d_row * ROW_TILES, ROW_TILES), ROW_TILES), :], sem)


def _scatter_kernel(cnt_ref, start_ref, end_ref, nu_ref, dest_ref, ha_ref, hb_ref, xs_ref, sem, *,
                    tm, bm, n_blocks, tiles_a):
    i = pl.program_id(0)

    def scatter_from(ht_ref):
        def body(tt, carry):
            for u in range(ROW_LOOP_UNROLL):
                t = tt * ROW_LOOP_UNROLL + u
                src = ht_ref.at[pl.ds(pl.multiple_of(t * ROW_TILES, ROW_TILES), ROW_TILES), :]
                for k in range(TOP_K):
                    d = dest_ref[t * TOP_K + k]
                    pltpu.make_async_copy(
                        src, xs_ref.at[pl.ds(pl.multiple_of(d * ROW_TILES, ROW_TILES), ROW_TILES), :],
                        sem).start(priority=k % DMA_QUEUES)
            return carry

        lax.fori_loop(0, tm // ROW_LOOP_UNROLL, body, 0)

    @pl.when(i < tiles_a)
    def _():
        scatter_from(ha_ref)

    @pl.when(i >= tiles_a)
    def _():
        scatter_from(hb_ref)

    ht_ref = hb_ref

    def tile_copy(u):
        return pltpu.make_async_copy(
            ht_ref, xs_ref.at[pl.ds(pl.multiple_of(u * (tm * ROW_TILES), tm * ROW_TILES), tm * ROW_TILES), :], sem)

    for k in range(TOP_K):
        tile_copy(0).wait()

    @pl.when(i == pl.num_programs(0) - 1)
    def _():
        per = bm // tm

        def tail_start(u, carry):
            tile_copy(u).start()
            return carry

        def tail_wait(u, carry):
            tile_copy(u).wait()
            return carry

        lax.fori_loop(nu_ref[0] * per, n_blocks * per, tail_start, 0)
        lax.fori_loop(nu_ref[0] * per, n_blocks * per, tail_wait, 0)

        chunk = PAD_CHUNK
        assert chunk <= tm and bm % chunk == 0

        def chunk_copy(q):
            return pltpu.make_async_copy(
                ht_ref.at[pl.ds(0, chunk * ROW_TILES), :],
                xs_ref.at[pl.ds(pl.multiple_of(q * (chunk * ROW_TILES), chunk * ROW_TILES), chunk * ROW_TILES), :],
                sem)

        def per_expert(e, carry):
            lo = start_ref[e] + cnt_ref[e]
            mid = (lo + chunk - 1) // chunk * chunk
            hi = end_ref[e]

            def fill(r, c2):
                _row_copy(ht_ref, xs_ref, 0, r, sem).start()
                return c2

            def fill_chunk(q, c2):
                chunk_copy(q).start()
                return c2

            lax.fori_loop(lo, mid, fill, 0)
            lax.fori_loop(mid // chunk, hi // chunk, fill_chunk, 0)

            def drain(r, c2):
                _row_copy(ht_ref, xs_ref, 0, r, sem).wait()
                return c2

            def drain_chunk(q, c2):
                chunk_copy(q).wait()
                return c2

            lax.fori_loop(lo, mid, drain, 0)
            lax.fori_loop(mid // chunk, hi // chunk, drain_chunk, 0)
            return carry

        lax.fori_loop(0, N_EXPERTS, per_expert, 0)


def _scatter_rows(ht_a, ht_b, dest_flat, cnt, start, end, n_used, n_blocks, tm, bm):
    rows = n_blocks * bm
    tiles_a, tiles_b = ht_a.shape[0] // (tm * ROW_TILES), ht_b.shape[0] // (tm * ROW_TILES)
    assert bm % tm == 0 and ht_a.shape[0] % (tm * ROW_TILES) == 0 and ht_b.shape[0] % (tm * ROW_TILES) == 0
    assert tiles_b >= 1
    kern = functools.partial(_scatter_kernel, tm=tm, bm=bm, n_blocks=n_blocks, tiles_a=tiles_a)
    return pl.pallas_call(
        kern,
        grid_spec=pltpu.PrefetchScalarGridSpec(
            num_scalar_prefetch=4,
            grid=(tiles_a + tiles_b,),
            in_specs=[pl.BlockSpec((tm * TOP_K,), lambda i, *_: (i,), memory_space=pltpu.SMEM),
                      pl.BlockSpec((tm * ROW_TILES, LANES), lambda i, *_: (jnp.minimum(i, tiles_a - 1), 0)),
                      pl.BlockSpec((tm * ROW_TILES, LANES), lambda i, *_: (jnp.maximum(i - tiles_a, 0), 0))],
            out_specs=pl.BlockSpec(memory_space=pl.ANY),
            scratch_shapes=[pltpu.SemaphoreType.DMA(())]),
        out_shape=jax.ShapeDtypeStruct((rows * ROW_TILES, LANES), F32),
        compiler_params=_cparams(("arbitrary",)),
        name="scatter_rows",
    )(cnt, start, end, n_used, dest_flat, ht_a, ht_b)


def _expert_kernel(be_ref, nx_ref, nv_ref, nu_ref, xs_ref, wg_ref, bg_ref, wd_ref, bd_ref, ys_ref,
                   wg_f, wd_f, wg_s, wd_s, wsem, *, bm):
    j = pl.program_id(0)
    prev = be_ref[jnp.maximum(j - 1, 0)]
    fresh = (j == 0) | (be_ref[j] != prev)
    active = j < nu_ref[0]

    def weight_copies(e):
        return (pltpu.make_async_copy(wg_ref.at[e], wg_f, wsem.at[0]),
                pltpu.make_async_copy(wd_ref.at[e], wd_f, wsem.at[1]))

    @pl.when(j == 0)
    def _():
        for c in weight_copies(be_ref[0]):
            c.start()

    @pl.when(active & fresh)
    def _():
        for c in weight_copies(be_ref[j]):
            c.wait()
        wg_s[...] = wg_f[...].astype(BF16)
        wd_s[...] = wd_f[...].astype(BF16)

        @pl.when(nx_ref[j] != be_ref[j])
        def _():
            for c in weight_copies(nx_ref[j]):
                c.start()

    def run_rows(m):
        x = jnp.concatenate([xs_ref[pl.ds(s, m, stride=ROW_TILES), :] for s in range(ROW_TILES)], axis=-1)
        xb = x.astype(BF16)
        acts = []
        for c in range(EXPERT_COL_CHUNKS):
            w = D_FF // EXPERT_COL_CHUNKS
            lo = c * w
            gate = jnp.dot(xb, wg_s[:, lo:lo + w], preferred_element_type=F32) + bg_ref[0, :, lo:lo + w]
            up = (jnp.dot(xb, wg_s[:, D_FF + lo:D_FF + lo + w], preferred_element_type=F32)
                  + bg_ref[0, :, D_FF + lo:D_FF + lo + w])
            gate = jnp.minimum(gate, SWIGLU_LIMIT)
            up = jnp.clip(up, -SWIGLU_LIMIT, SWIGLU_LIMIT)
            acts.append(((up + 1.0) * (gate * jax.nn.sigmoid(SWIGLU_ALPHA * gate))).astype(BF16))
        act = jnp.concatenate(acts, axis=-1)
        for c in range(EXPERT_COL_CHUNKS):
            w = D_MODEL // EXPERT_COL_CHUNKS
            lo = c * w
            y = jnp.dot(act, wd_s[:, lo:lo + w], preferred_element_type=F32) + bd_ref[0, :, lo:lo + w]
            for s in range(w // LANES):
                ys_ref[pl.ds(lo // LANES + s, m, stride=ROW_TILES), :] = y[:, s * LANES:(s + 1) * LANES]
        if m < bm:
            ys_ref[pl.ds(m * ROW_TILES, (bm - m) * ROW_TILES), :] = jnp.zeros(((bm - m) * ROW_TILES, LANES), F32)

    quarter = bm // 4
    quarters = (nv_ref[j] + quarter - 1) // quarter
    for nq in range(1, 5):
        @pl.when(active & ((quarters == nq) if nq > 1 else (quarters <= 1)))
        def _(nq=nq):
            run_rows(nq * quarter)

    @pl.when(jnp.logical_not(active))
    def _():
        ys_ref[...] = jnp.zeros_like(ys_ref)


def _experts(xs, block_expert, next_expert, block_valid, n_used, w_gu, b_gu, w_down, b_down, bm):
    rows = xs.shape[0] // ROW_TILES
    n_blocks = rows // bm

    def blk(j, be, nx, nv, nu):
        return (jnp.minimum(j, nu[0] - 1), 0)

    def bsel(j, be, nx, nv, nu):
        return (be[j], 0, 0)

    hbm = pl.BlockSpec(memory_space=pl.ANY)
    kern = functools.partial(_expert_kernel, bm=bm)
    return pl.pallas_call(
        kern,
        grid_spec=pltpu.PrefetchScalarGridSpec(
            num_scalar_prefetch=4,
            grid=(n_blocks,),
            in_specs=[pl.BlockSpec((bm * ROW_TILES, LANES), blk), hbm,
                      pl.BlockSpec((1, 1, 2 * D_FF), bsel), hbm,
                      pl.BlockSpec((1, 1, D_MODEL), bsel)],
            out_specs=pl.BlockSpec((bm * ROW_TILES, LANES), lambda j, be, nx, nv, nu: (j, 0)),
            scratch_shapes=[pltpu.VMEM((D_MODEL, 2 * D_FF), F32),
                            pltpu.VMEM((D_FF, D_MODEL), F32),
                            pltpu.VMEM((D_MODEL, 2 * D_FF), BF16),
                            pltpu.VMEM((D_FF, D_MODEL), BF16),
                            pltpu.SemaphoreType.DMA((2,))]),
        out_shape=jax.ShapeDtypeStruct((rows * ROW_TILES, LANES), F32),
        compiler_params=_cparams(("arbitrary",)),
        name="experts",
    )(block_expert, next_expert, block_valid, n_used, xs, w_gu, b_gu.reshape(N_EXPERTS, 1, 2 * D_FF), w_down,
      b_down.reshape(N_EXPERTS, 1, D_MODEL))


def _final_kernel(loc_ref, gate_ref, tcur_ref, tnext_ref, ys_ref, x1_ref, gt_ref, g_ref, o_ref,
                  st_ref, f_ref, sem, *, tm, steps):
    g = pl.program_id(0) * pl.num_programs(1) + pl.program_id(1)
    cur = g % 2
    sizes = [1 << b for b in range(tm.bit_length() - 1, -1, -1)]

    def request(t_ref, slot):
        def per_expert(e, off):
            row = t_ref[e]
            cnt = t_ref[N_EXPERTS + e]
            for size in sizes:
                take = (cnt & size) != 0

                @pl.when(take)
                def _(row=row, off=off, size=size):
                    pltpu.make_async_copy(
                        ys_ref.at[pl.ds(pl.multiple_of(row * ROW_TILES, ROW_TILES), size * ROW_TILES), :],
                        st_ref.at[slot, pl.ds(pl.multiple_of(off * ROW_TILES, ROW_TILES), size * ROW_TILES), :],
                        sem.at[slot]).start()

                step = jnp.where(take, size, 0)
                row = row + step
                off = off + step
            return off

        lax.fori_loop(0, N_EXPERTS, per_expert, 0)

    @pl.when(g == 0)
    def _():
        request(tcur_ref, 0)

    @pl.when(g + 1 < steps)
    def _():
        request(tnext_ref, 1 - cur)

    pltpu.make_async_copy(ys_ref.at[pl.ds(0, tm * TOP_K * ROW_TILES), :], st_ref.at[cur], sem.at[cur]).wait()

    def pick(tt, carry):
        for u in range(ROW_LOOP_UNROLL):
            t = tt * ROW_LOOP_UNROLL + u
            for k in range(TOP_K):
                at = pl.multiple_of(loc_ref[t * TOP_K + k] * ROW_TILES, ROW_TILES)
                f_ref[k, pl.ds(pl.multiple_of(t * ROW_TILES, ROW_TILES), ROW_TILES), :] = (
                    st_ref[cur, pl.ds(at, ROW_TILES), :])
        return carry

    lax.fori_loop(0, tm // ROW_LOOP_UNROLL, pick, 0)
    gates = gate_ref[...]
    cols = []
    for s in range(ROW_TILES):
        acc = None
        for k in range(TOP_K):
            part = f_ref[k, pl.ds(s, tm, stride=ROW_TILES), :] * gates[:, k:k + 1]
            acc = part if acc is None else acc + part
        cols.append(acc)
    f = jnp.concatenate(cols, axis=-1)
    o_ref[0] = x1_ref[0] + gt_ref[0] * _rms(f, g_ref[...])


def _final(ys, local_flat, gates, run_table, x1, gt, g_post_ffn, tok_off, tm, per_token):
    bsz, t, _ = x1.shape
    tiles = t // tm
    off = tok_off // tm
    steps = bsz * tiles
    kern = functools.partial(_final_kernel, tm=tm, steps=steps)

    def smem(ahead):
        return pl.BlockSpec((tm * TOP_K,), lambda b, i: (off + jnp.minimum(b * tiles + i + ahead, steps - 1),),
                            memory_space=pltpu.SMEM)

    return pl.pallas_call(
        kern,
        grid=(bsz, tiles),
        in_specs=[smem(0), pl.BlockSpec((tm, TOP_K), lambda b, i: (off + b * tiles + i, 0)), smem(0), smem(1),
                  pl.BlockSpec(memory_space=pl.ANY),
                  pl.BlockSpec((1, tm, D_MODEL), lambda b, i: (b, i, 0)),
                  _mod_spec(per_token, tm),
                  pl.BlockSpec((1, D_MODEL), lambda b, i: (0, 0))],
        out_specs=pl.BlockSpec((1, tm, D_MODEL), lambda b, i: (b, i, 0)),
        out_shape=jax.ShapeDtypeStruct((bsz, t, D_MODEL), F32),
        scratch_shapes=[pltpu.VMEM((2, tm * TOP_K * ROW_TILES, LANES), F32),
                        pltpu.VMEM((TOP_K, tm * ROW_TILES, LANES), F32),
                        pltpu.SemaphoreType.DMA((2,))],
        compiler_params=_cparams(("arbitrary", "arbitrary")),
        name="final",
    )(local_flat, gates, run_table, run_table, ys, x1, gt, g_post_ffn.reshape(1, D_MODEL))


def _moe(ht_a, ht_b, gsel, rank, counts, w_gu, b_gu, w_down, b_down, tm, bm):
    n = gsel.shape[0]
    cnt = counts[:, 0].astype(jnp.int32)
    nblk = (cnt + bm - 1) // bm
    blk_end = jnp.cumsum(nblk)
    blk_start = blk_end - nblk
    n_blocks = -(-(n * TOP_K) // bm) + N_EXPERTS
    blocks = jnp.arange(n_blocks, dtype=jnp.int32)
    block_expert = jnp.minimum(jnp.sum((blk_end[None, :] <= blocks[:, None]).astype(jnp.int32), axis=1),
                               N_EXPERTS - 1).astype(jnp.int32)
    ids = jnp.arange(N_EXPERTS, dtype=jnp.int32)
    later = jnp.where((nblk[None, :] > 0) & (ids[None, :] > ids[:, None]), ids[None, :], N_EXPERTS)
    nearest = jnp.min(later, axis=1)
    next_used = jnp.where(nearest < N_EXPERTS, nearest, ids).astype(jnp.int32)
    n_used = blk_end[-1:].astype(jnp.int32)
    row_start = (blk_start * bm).astype(jnp.int32)
    row_end = (blk_end * bm).astype(jnp.int32)
    before = rank[::tm]
    run_len = jnp.concatenate([before[1:], counts.reshape(1, N_EXPERTS)], axis=0) - before
    run_row = row_start.astype(F32)[None, :] + before
    run_off = jnp.cumsum(run_len, axis=1) - run_len
    tiles = n // tm
    run_table = jnp.zeros((tiles, tm * TOP_K), jnp.int32)
    run_table = run_table.at[:, :N_EXPERTS].set(run_row.astype(jnp.int32))
    run_table = run_table.at[:, N_EXPERTS:2 * N_EXPERTS].set(run_len.astype(jnp.int32)).reshape(tiles * tm * TOP_K)
    dest, local, gates = _dest(gsel, rank, row_start.astype(F32).reshape(1, N_EXPERTS), run_row - run_off,
                               math.gcd(n, DEST_TILE), tm)
    dest_flat = dest.reshape(n * TOP_K)
    xs = _scatter_rows(ht_a, ht_b, dest_flat, cnt, row_start, row_end, n_used, n_blocks, SCATTER_TILE, bm)
    owns = (blocks[:, None] >= blk_start[None, :]) & (blocks[:, None] < blk_end[None, :])

    def per_block(table):
        return jnp.sum(jnp.where(owns, table[None, :], 0), axis=1).astype(jnp.int32)

    block_valid = jnp.clip(per_block(cnt) - (blocks - per_block(blk_start)) * bm, 0, bm).astype(jnp.int32)
    ys = _experts(xs, block_expert, per_block(next_used), block_valid, n_used, w_gu, b_gu, w_down, b_down, bm)
    return ys, local.reshape(n * TOP_K), gates, run_table


def _mix_path(x, mod, pool_buf, conv_buf, s0, start_pos, counts0, lp, tm, nb, per_token):
    (g_pre_mix, g_post_mix, g_pre_ffn, w_in_p, w_bd, pool_scale, conv_w, alog_row, dtb_row, norm_w,
     w_out_b, router_w, router_b) = lp
    bsz, t, _ = x.shape
    sh_mix, sc_mix, gt_mix, sh_ffn, sc_ffn, gt_ffn = (mod[:, i] for i in range(6))

    def shape_mod(m):
        if per_token:
            return jnp.repeat(m, t, axis=0).reshape(1, bsz * t, D_MODEL)
        return m.reshape(bsz, 1, D_MODEL)

    sh_mix, sc_mix, gt_mix, sh_ffn, sc_ffn, gt_ffn = map(shape_mod, (sh_mix, sc_mix, gt_mix, sh_ffn, sc_ffn, gt_ffn))
    xf = x.reshape(1, bsz * t, D_MODEL) if per_token else x
    proj = _in_proj(xf, sc_mix, sh_mix, g_pre_mix, w_in_p, PROJ_TILE, per_token).reshape(bsz, t, PROJ_WIDTH)
    tseq = min(tm, t)
    o_gdn, s_new = _gdn(proj, conv_buf, s0, conv_w, alog_row, dtb_row, norm_w, nb, tseq)
    pool_out = _pool(proj, pool_buf, w_bd, pool_scale, nb * POOL_SEQS if per_token else nb, min(POOL_TILE, t),
                     start_pos)
    fl = (lambda a: a.reshape(1, bsz * t, a.shape[-1])) if per_token else (lambda a: a)
    x1, ht, gsel, rank, counts = _out_proj(xf, fl(pool_out), fl(o_gdn), gt_mix, sc_ffn, sh_ffn, g_post_mix,
                                           g_pre_ffn, w_out_b, router_w, router_b, counts0, OUT_TILE, per_token)

    def last_rows(buf, off, width):
        keep = buf.shape[1]
        if t >= keep:
            return proj[:, t - keep:, off:off + width]
        return jnp.concatenate([buf[:, t:], proj[:, :, off:off + width]], axis=1)

    pool_new = last_rows(pool_buf, OFF_U, POOL_WIDTH)
    conv_new = last_rows(conv_buf, OFF_QKV, CONV_CH)
    return x1, ht, gsel, rank, counts, gt_ffn, pool_new, conv_new, s_new


def kernel(x_prompt, x_sample, c_prompt, c_sample, state_pool, state_conv, state_ssm, w_ada, b_ada, g_pre_mix, g_post_mix, g_pre_ffn, g_post_ffn, w_in, w_pool, pool_scale, conv_w, a_log, dt_bias, gdn_norm_w, w_out, router_w, router_b, w_gu, b_gu, w_down, b_down):
    depth = w_ada.shape[0]
    bp, tp, _ = x_prompt.shape
    bs, ts, _ = x_sample.shape
    tm = TOKEN_TILE
    y_p, y_s = x_prompt, x_sample
    outs = [[] for _ in range(6)]
    for l in range(depth):
        mod = _ada(jnp.concatenate([c_prompt, c_sample], axis=0), w_ada[l], b_ada[l]).reshape(bp + bs, 6, D_MODEL)
        w_in_p = jnp.swapaxes(w_in[l], 0, 1)
        w_bd = jax.scipy.linalg.block_diag(*[w_pool[l, g] for g in range(len(POOL_WINDOWS))])
        pad_row = lambda v: jnp.pad(v, (0, LANES - v.shape[0])).reshape(1, LANES)
        lp = (g_pre_mix[l], g_post_mix[l], g_pre_ffn[l], w_in_p, w_bd, pool_scale[l], conv_w[l],
              pad_row(a_log[l]), pad_row(dt_bias[l]), gdn_norm_w[l].reshape(1, HEAD_DIM),
              w_out[l].astype(BF16), router_w[l], router_b[l])
        zero_pool = jnp.zeros((bp, POOL_BUF, POOL_WIDTH), F32)
        zero_conv = jnp.zeros((bp, CONV_WIDTH - 1, CONV_CH), F32)
        zero_ssm = (jnp.zeros((1, bp, GDN_HEADS, HEAD_DIM, HEAD_DIM), F32), 0)
        no_counts = jnp.zeros((N_EXPERTS, 1), F32)
        x1p, htp, gp, rp, cnt_p, gtp, pp, cp, sp = _mix_path(
            y_p, mod[:bp], zero_pool, zero_conv, zero_ssm, 0, no_counts, lp, tm, 1, False)
        x1s, hts, gs, rs, counts, gts, ps, cs, ss = _mix_path(
            y_s, mod[bp:], state_pool[l], state_conv[l], (state_ssm, l), PAST_LEN, cnt_p, lp, tm, 8, True)
        gsel = jnp.concatenate([gp, gs], axis=0)
        rank = jnp.concatenate([rp, rs], axis=0)
        ys, local, gates, runs = _moe(htp, hts, gsel, rank, counts, w_gu[l], b_gu[l], w_down[l], b_down[l],
                                      FINAL_TILE, MOE_BLOCK)
        y_p = _final(ys, local, gates, runs, x1p, gtp, g_post_ffn[l], 0, FINAL_TILE, False)
        y_s = _final(ys, local, gates, runs, x1s, gts, g_post_ffn[l], bp * tp, FINAL_TILE,
                     True).reshape(bs, ts, D_MODEL)
        for lst, val in zip(outs, (pp, cp, sp, ps, cs, ss)):
            lst.append(val)
    return (y_p, y_s) + tuple(o[0][None] if depth == 1 else jnp.stack(o) for o in outs)
```

```python
import functools
import math

import jax
import jax.numpy as jnp
from jax import lax
from jax.experimental import pallas as pl
from jax.experimental.pallas import tpu as pltpu

F32 = jnp.float32
BF16 = jnp.bfloat16

D_MODEL = 1024
PAST_LEN = 16384
POOL_WINDOWS = (2, 4, 8, 16)
POOL_WIDTH = D_MODEL // 4
POOL_GROUP = POOL_WIDTH // len(POOL_WINDOWS)
POOL_BUF = max(POOL_WINDOWS) - 1
GDN_WIDTH = D_MODEL - POOL_WIDTH
HEAD_DIM = 128
GDN_HEADS = GDN_WIDTH // HEAD_DIM
CONV_WIDTH = 4
CONV_CH = 3 * GDN_WIDTH
GDN_CHUNK = 64
N_EXPERTS = 32
TOP_K = 4
D_FF = D_MODEL
SWIGLU_LIMIT = 7.0
SWIGLU_ALPHA = 1.702
EPS = 1e-6

SUBLANES = 8
LANES = 128
ROW_TILES = D_MODEL // LANES
OFF_QKV = 0
OFF_Z = OFF_QKV + CONV_CH
OFF_U = OFF_Z + GDN_WIDTH
OFF_AB = OFF_U + POOL_WIDTH
PROJ_WIDTH = OFF_AB + LANES
assert OFF_Z % GDN_WIDTH == 0 and OFF_U % POOL_WIDTH == 0 and OFF_AB % LANES == 0
TOKEN_TILE = 256
PROJ_TILE = 512
SCATTER_TILE = 512
OUT_TILE = 512
FINAL_TILE = 512
DEST_TILE = 5632
PAD_CHUNK = 64
POOL_TILE = 1024
POOL_SEQS = 4
GDN_ITEM_GROUPS = 1
MOE_BLOCK = 1024
EXPERT_COL_CHUNKS = 2
DMA_QUEUES = 2
ROW_LOOP_UNROLL = 8
VMEM_LIMIT = 56 * 1024 * 1024


def _cparams(sem):
    return pltpu.CompilerParams(dimension_semantics=sem, vmem_limit_bytes=VMEM_LIMIT)


def _dot(a, b):
    return jnp.dot(a.astype(BF16), b.astype(BF16), preferred_element_type=F32)


def _dot_nt(a, b):
    return lax.dot_general(a.astype(BF16), b.astype(BF16), (((1,), (1,)), ((), ())),
                           preferred_element_type=F32)


def _silu(x):
    return x * jax.nn.sigmoid(x)


def _rms(x, g):
    return x * lax.rsqrt(jnp.mean(x * x, axis=-1, keepdims=True) + EPS) * g


def _ada_kernel(c_ref, w_ref, b_ref, o_ref):
    o_ref[...] = _dot(_silu(c_ref[...]), w_ref[...]) + b_ref[...]


def _ada(c, w_ada, b_ada):
    n = c.shape[0]
    cols = w_ada.shape[1]
    blk = D_MODEL
    return pl.pallas_call(
        _ada_kernel,
        grid=(cols // blk,),
        in_specs=[pl.BlockSpec((n, D_MODEL), lambda j: (0, 0)),
                  pl.BlockSpec((D_MODEL, blk), lambda j: (0, j)),
                  pl.BlockSpec((1, blk), lambda j: (0, j))],
        out_specs=pl.BlockSpec((n, blk), lambda j: (0, j)),
        out_shape=jax.ShapeDtypeStruct((n, cols), F32),
        compiler_params=_cparams(("arbitrary",)),
        name="ada",
    )(c, w_ada, b_ada.reshape(1, cols))


def _in_kernel(x_ref, sc_ref, sh_ref, g_ref, wt_ref, o_ref, wb):
    @pl.when((pl.program_id(0) == 0) & (pl.program_id(1) == 0))
    def _():
        o1, o2, o3 = POOL_WIDTH, POOL_WIDTH + CONV_CH, POOL_WIDTH + CONV_CH + GDN_WIDTH
        wb[OFF_QKV:OFF_QKV + CONV_CH] = wt_ref[o1:o2].astype(BF16)
        wb[OFF_Z:OFF_Z + GDN_WIDTH] = wt_ref[o2:o3].astype(BF16)
        wb[OFF_U:OFF_U + POOL_WIDTH] = wt_ref[:o1].astype(BF16)
        tail = wt_ref.shape[0] - o3
        wb[OFF_AB:] = jnp.concatenate(
            [wt_ref[o3:], jnp.zeros((LANES - tail, D_MODEL), F32)], axis=0).astype(BF16)

    h = _rms(x_ref[0], g_ref[...]) * (1.0 + sc_ref[0]) + sh_ref[0]
    o_ref[0] = _dot_nt(h, wb[...])


def _mod_spec(per_token, tm):
    if per_token:
        return pl.BlockSpec((1, tm, D_MODEL), lambda b, t: (0, t, 0))
    return pl.BlockSpec((1, 1, D_MODEL), lambda b, t: (b, 0, 0))


def _in_proj(x, sc, sh, g, w_in, tm, per_token):
    bsz, t, _ = x.shape
    return pl.pallas_call(
        _in_kernel,
        grid=(bsz, t // tm),
        in_specs=[pl.BlockSpec((1, tm, D_MODEL), lambda b, i: (b, i, 0)),
                  _mod_spec(per_token, tm), _mod_spec(per_token, tm),
                  pl.BlockSpec((1, D_MODEL), lambda b, i: (0, 0)),
                  pl.BlockSpec(w_in.shape, lambda b, i: (0, 0), pipeline_mode=pl.Buffered(1))],
        out_specs=pl.BlockSpec((1, tm, PROJ_WIDTH), lambda b, i: (b, i, 0)),
        out_shape=jax.ShapeDtypeStruct((bsz, t, PROJ_WIDTH), F32),
        scratch_shapes=[pltpu.VMEM((PROJ_WIDTH, D_MODEL), BF16)],
        compiler_params=_cparams(("arbitrary", "arbitrary")),
        name="in_proj",
    )(x, sc, sh, g.reshape(1, D_MODEL), w_in)


def _cumsum_rows(x):
    n = x.shape[0]
    row = lax.broadcasted_iota(jnp.int32, x.shape, 0)
    s = 1
    while s < n:
        x = x + jnp.where(row >= s, pltpu.roll(x, s, axis=0), 0.0)
        s *= 2
    return x


def _gdn_kernel(qkv_ref, z_ref, ab_ref, cs_ref, cw_ref, s0_ref, alog_ref, dtb_ref, nw_ref, o_ref, sn_ref,
                s_scr, prev_scr, *, nb, tm, tmp, chunk, group_items):
    ti = pl.program_id(1)
    nt = pl.num_programs(1)
    n_chunks = tmp // chunk
    n_sq = int(math.log2(chunk)) - 1

    @pl.when(ti == 0)
    def _():
        s_scr[...] = s0_ref[...]
        for b in range(nb):
            prev_scr[b] = jnp.concatenate(
                [jnp.zeros((SUBLANES - (CONV_WIDTH - 1), CONV_CH), F32), cs_ref[b]], axis=0)

    rowi = lax.broadcasted_iota(jnp.int32, (tmp, LANES), 0)
    ri = lax.broadcasted_iota(jnp.int32, (chunk, chunk), 0)
    ci = lax.broadcasted_iota(jnp.int32, (chunk, chunk), 1)
    tril = ri >= ci
    strict = ri > ci
    eye = (ri == ci).astype(F32)
    neg_a = -jnp.exp(alog_ref[...])
    dtb = dtb_ref[...]
    nw = nw_ref[...]

    def padrows(x):
        if tm == tmp:
            return x
        return jnp.concatenate([x, jnp.zeros((tmp - tm, x.shape[1]), F32)], axis=0)

    cw = cw_ref[...]
    valid = rowi < tm
    items = []
    for b in range(nb):
        x = padrows(qkv_ref[b])
        e = jnp.concatenate([prev_scr[b], x], axis=0)
        acc = x * cw[CONV_WIDTH - 1:CONV_WIDTH]
        for d in range(1, CONV_WIDTH):
            acc = acc + pltpu.roll(e, d, axis=0)[SUBLANES:] * cw[CONV_WIDTH - 1 - d:CONV_WIDTH - d]
        if tm >= SUBLANES:
            prev_scr[b] = x[tm - SUBLANES:tm]
        y = _silu(acc)
        ab = padrows(ab_ref[b])
        g_all = jnp.where(valid, neg_a * jax.nn.softplus(ab + dtb), 0.0)
        beta_all = jnp.where(valid, jax.nn.sigmoid(ab), 0.0)
        for h in range(GDN_HEADS):
            qc = y[:, h * HEAD_DIM:(h + 1) * HEAD_DIM]
            kc = y[:, GDN_WIDTH + h * HEAD_DIM:GDN_WIDTH + (h + 1) * HEAD_DIM]
            v = y[:, 2 * GDN_WIDTH + h * HEAD_DIM:2 * GDN_WIDTH + (h + 1) * HEAD_DIM]
            q = qc * lax.rsqrt(jnp.sum(qc * qc, axis=-1, keepdims=True) + EPS) * (HEAD_DIM ** -0.5)
            k = kc * lax.rsqrt(jnp.sum(kc * kc, axis=-1, keepdims=True) + EPS)
            gb = jnp.broadcast_to(g_all[:, h:h + 1], (tmp, LANES))
            bb = jnp.broadcast_to(beta_all[:, GDN_HEADS + h:GDN_HEADS + h + 1], (tmp, LANES))
            for c in range(n_chunks):
                sl = slice(c * chunk, (c + 1) * chunk)
                items.append(dict(b=b, h=h, c=c, q=q[sl], k=k[sl], v=v[sl], bb=bb[sl], gb=gb[sl]))

    def within_chunk(group):
        for it in group:
            gcb = _cumsum_rows(it["gb"])
            gct = jnp.transpose(gcb)[:chunk]
            diff = gcb[:, :chunk] - gct
            it["decay"] = jnp.where(tril, jnp.exp(jnp.where(tril, diff, 0.0)), 0.0)
            it["kb"] = it["k"] * it["bb"]
            it["eg"] = jnp.exp(gcb)
            gl = gcb[chunk - 1:chunk]
            it["egl"] = jnp.exp(gl)
            it["kdt"] = jnp.transpose(it["k"] * jnp.exp(gl - gcb))
        for it in group:
            kq = _dot_nt(jnp.concatenate([it["kb"], it["q"]], axis=0), it["k"])
            it["a"] = jnp.where(strict, kq[:chunk] * it["decay"], 0.0)
            it["qk"] = jnp.where(tril, kq[chunk:] * it["decay"], 0.0)
        for it in group:
            it["inv"] = eye - it["a"]
            it["p"] = _dot(it["a"], it["a"])
        for lvl in range(n_sq):
            for it in group:
                if lvl < n_sq - 1:
                    r = _dot(jnp.concatenate([it["inv"], it["p"]], axis=0), it["p"])
                    it["inv"] = it["inv"] + r[:chunk]
                    it["p"] = r[chunk:]
                else:
                    it["inv"] = it["inv"] + _dot(it["inv"], it["p"])
        for it in group:
            it["wu"] = _dot(it["inv"], jnp.concatenate([it["kb"] * it["eg"], it["v"] * it["bb"]], axis=1))
        for it in group:
            mn = _dot(it["kdt"], it["wu"])
            qo = _dot(it["qk"], it["wu"])
            it["mp"], it["nc"] = mn[:, :HEAD_DIM], mn[:, HEAD_DIM:]
            it["qp"] = it["q"] * it["eg"] - qo[:, :HEAD_DIM]
            it["oc"] = qo[:, HEAD_DIM:]

    for g0 in range(0, len(items), group_items):
        within_chunk(items[g0:g0 + group_items])

    state = {(b, h): s_scr[b, h] for b in range(nb) for h in range(GDN_HEADS)}
    for c in range(n_chunks):
        for it in items:
            if it["c"] != c:
                continue
            s = state[(it["b"], it["h"])]
            r = _dot(jnp.concatenate([it["qp"], it["mp"]], axis=0), s)
            it["o"] = r[:chunk] + it["oc"]
            state[(it["b"], it["h"])] = s * it["egl"] - r[chunk:] + it["nc"]
    for (b, h), s in state.items():
        s_scr[b, h] = s

    for b in range(nb):
        zf = padrows(z_ref[b])
        for h in range(GDN_HEADS):
            outs = [it["o"] for it in items if it["b"] == b and it["h"] == h]
            o = outs[0] if n_chunks == 1 else jnp.concatenate(outs, axis=0)
            o = (o * lax.rsqrt(jnp.mean(o * o, axis=-1, keepdims=True) + EPS) * nw
                 * _silu(zf[:, h * HEAD_DIM:(h + 1) * HEAD_DIM]))
            o_ref[b, :, h * HEAD_DIM:(h + 1) * HEAD_DIM] = o[:tm]

    @pl.when(ti == nt - 1)
    def _():
        sn_ref[...] = s_scr[...]


def _gdn(proj, state_conv, s0, conv_w, alog_row, dtb_row, norm_w, nb, tm):
    bsz, t, _ = proj.shape
    tmp = max(tm, SUBLANES)
    chunk = min(GDN_CHUNK, tmp)
    row = pl.BlockSpec((1, LANES), lambda b, i: (0, 0))
    s_all, layer = s0
    sspec = pl.BlockSpec((nb, GDN_HEADS, HEAD_DIM, HEAD_DIM), lambda b, i: (b, 0, 0, 0))
    s_in = pl.BlockSpec((None, nb, GDN_HEADS, HEAD_DIM, HEAD_DIM), lambda b, i: (layer, b, 0, 0, 0))
    n_items = nb * GDN_HEADS * (tmp // chunk)
    kern = functools.partial(_gdn_kernel, nb=nb, tm=tm, tmp=tmp, chunk=chunk,
                             group_items=n_items // GDN_ITEM_GROUPS)
    return pl.pallas_call(
        kern,
        grid=(bsz // nb, t // tm),
        in_specs=[pl.BlockSpec((nb, tm, CONV_CH), lambda b, i: (b, i, OFF_QKV // CONV_CH)),
                  pl.BlockSpec((nb, tm, GDN_WIDTH), lambda b, i: (b, i, OFF_Z // GDN_WIDTH)),
                  pl.BlockSpec((nb, tm, LANES), lambda b, i: (b, i, OFF_AB // LANES)),
                  pl.BlockSpec((nb, CONV_WIDTH - 1, CONV_CH), lambda b, i: (b, 0, 0)),
                  pl.BlockSpec((CONV_WIDTH, CONV_CH), lambda b, i: (0, 0)),
                  s_in, row, row, row],
        out_specs=[pl.BlockSpec((nb, tm, GDN_WIDTH), lambda b, i: (b, i, 0)), sspec],
        out_shape=[jax.ShapeDtypeStruct((bsz, t, GDN_WIDTH), F32),
                   jax.ShapeDtypeStruct(s_all.shape[1:], F32)],
        scratch_shapes=[pltpu.VMEM((nb, GDN_HEADS, HEAD_DIM, HEAD_DIM), F32),
                        pltpu.VMEM((nb, SUBLANES, CONV_CH), F32)],
        compiler_params=_cparams(("arbitrary", "arbitrary")),
        name="gdn",
    )(proj, proj, proj, state_conv, conv_w, s_all, alog_row, dtb_row, norm_w)


def _pool_kernel(u_ref, st_ref, w_ref, sc_ref, o_ref, prev_scr, *, nb, tm, tmp, start_pos):
    ti = pl.program_id(1)
    halo = 2 * SUBLANES

    @pl.when(ti == 0)
    def _():
        for b in range(nb):
            prev_scr[b] = jnp.concatenate([jnp.zeros((halo - POOL_BUF, POOL_WIDTH), F32), st_ref[b]], axis=0)

    lane = lax.broadcasted_iota(jnp.int32, (tmp, POOL_WIDTH), 1)
    rowi = lax.broadcasted_iota(jnp.int32, (tmp, POOL_WIDTH), 0)
    pos1 = start_pos + ti * tm + rowi + 1
    grp = lane // POOL_GROUP
    win = jnp.where(grp == 0, POOL_WINDOWS[0],
                    jnp.where(grp == 1, POOL_WINDOWS[1], jnp.where(grp == 2, POOL_WINDOWS[2], POOL_WINDOWS[3])))
    cnt = jnp.minimum(pos1, win).astype(F32)
    for b in range(nb):
        x = u_ref[b]
        if tm != tmp:
            x = jnp.concatenate([x, jnp.zeros((tmp - tm, POOL_WIDTH), F32)], axis=0)
        e = jnp.concatenate([prev_scr[b], x], axis=0)
        sums = []
        s = e
        for w in POOL_WINDOWS:
            s = s + pltpu.roll(s, w // 2, axis=0)
            sums.append(s[halo:])
        sel = jnp.where(grp == 0, sums[0], jnp.where(grp == 1, sums[1], jnp.where(grp == 2, sums[2], sums[3])))
        pooled = sel / cnt - x
        out = _dot(pooled, w_ref[...]) * sc_ref[...]
        o_ref[b] = out[:tm]
        if tm >= halo:
            prev_scr[b] = x[tm - halo:tm]


def _pool(proj, state_pool, w_bd, pool_scale, nb, tm, start_pos):
    bsz, t, _ = proj.shape
    tmp = max(tm, SUBLANES)
    kern = functools.partial(_pool_kernel, nb=nb, tm=tm, tmp=tmp, start_pos=start_pos)
    return pl.pallas_call(
        kern,
        grid=(bsz // nb, t // tm),
        in_specs=[pl.BlockSpec((nb, tm, POOL_WIDTH), lambda b, i: (b, i, OFF_U // POOL_WIDTH)),
                  pl.BlockSpec((nb, POOL_BUF, POOL_WIDTH), lambda b, i: (b, 0, 0)),
                  pl.BlockSpec((POOL_WIDTH, POOL_WIDTH), lambda b, i: (0, 0)),
                  pl.BlockSpec((1, POOL_WIDTH), lambda b, i: (0, 0))],
        out_specs=pl.BlockSpec((nb, tm, POOL_WIDTH), lambda b, i: (b, i, 0)),
        out_shape=jax.ShapeDtypeStruct((bsz, t, POOL_WIDTH), F32),
        scratch_shapes=[pltpu.VMEM((nb, 2 * SUBLANES, POOL_WIDTH), F32)],
        compiler_params=_cparams(("arbitrary", "arbitrary")),
        name="pool",
    )(proj, state_pool, w_bd, pool_scale.reshape(1, POOL_WIDTH))


def _out_kernel(x_ref, po_ref, og_ref, gt_ref, sc_ref, sh_ref, gpm_ref, gpf_ref, wo_ref, rw_ref, rb_ref, c0_ref,
                x1_ref, ht_ref, g_ref, r_ref, cnt_ref, carry, *, tm):
    @pl.when((pl.program_id(0) == 0) & (pl.program_id(1) == 0))
    def _():
        carry[...] = c0_ref[...]

    mix = _dot(po_ref[0], wo_ref[:POOL_WIDTH]) + _dot(og_ref[0], wo_ref[POOL_WIDTH:])
    x1 = x_ref[0] + gt_ref[0] * _rms(mix, gpm_ref[...])
    x1_ref[0] = x1
    h = _rms(x1, gpf_ref[...]) * (1.0 + sc_ref[0]) + sh_ref[0]
    for s in range(ROW_TILES):
        ht_ref[pl.ds(s, tm, stride=ROW_TILES), :] = h[:, s * LANES:(s + 1) * LANES]
    logits = _dot_nt(rw_ref[...], h) + rb_ref[...]
    eid = lax.broadcasted_iota(jnp.int32, logits.shape, 0).astype(F32)
    work = logits
    sel = jnp.zeros(logits.shape, jnp.bool_)
    top = None
    den = jnp.zeros((1, tm), F32)
    for j in range(TOP_K):
        m = jnp.max(work, axis=0, keepdims=True)
        idx = jnp.min(jnp.where(work == m, eid, float(N_EXPERTS)), axis=0, keepdims=True)
        hit = eid == idx
        sel = sel | hit
        work = jnp.where(hit, -jnp.inf, work)
        if j == 0:
            top = m
        den = den + jnp.exp(m - top)
    gates = jnp.exp(logits - top) / den
    g_ref[...] = jnp.transpose(jnp.where(sel, gates, -1.0))
    mask = sel.astype(F32)
    ri = lax.broadcasted_iota(jnp.int32, (tm, tm), 0)
    ci = lax.broadcasted_iota(jnp.int32, (tm, tm), 1)
    c = carry[...]
    r_ref[...] = jnp.transpose(_dot(mask, (ri < ci).astype(F32)) + c)
    c = c + jnp.sum(mask, axis=1, keepdims=True)
    carry[...] = c
    cnt_ref[...] = c


def _out_proj(x, pool_out, o_gdn, gt, sc, sh, g_post_mix, g_pre_ffn, w_out_b, router_w, router_b, counts0,
              tm, per_token):
    bsz, t, _ = x.shape
    n = bsz * t
    tiles = t // tm

    def tok(width):
        return pl.BlockSpec((1, tm, width), lambda b, i: (b, i, 0))

    def const(shape):
        return pl.BlockSpec(shape, lambda b, i: tuple(0 for _ in shape))

    kern = functools.partial(_out_kernel, tm=tm)
    return pl.pallas_call(
        kern,
        grid=(bsz, tiles),
        in_specs=[tok(D_MODEL), tok(POOL_WIDTH), tok(GDN_WIDTH),
                  _mod_spec(per_token, tm), _mod_spec(per_token, tm), _mod_spec(per_token, tm),
                  const((1, D_MODEL)), const((1, D_MODEL)), const((D_MODEL, D_MODEL)),
                  const((N_EXPERTS, D_MODEL)), const((N_EXPERTS, 1)), const((N_EXPERTS, 1))],
        out_specs=[tok(D_MODEL),
                   pl.BlockSpec((tm * ROW_TILES, LANES), lambda b, i: (b * tiles + i, 0)),
                   pl.BlockSpec((tm, N_EXPERTS), lambda b, i: (b * tiles + i, 0)),
                   pl.BlockSpec((tm, N_EXPERTS), lambda b, i: (b * tiles + i, 0)),
                   const((N_EXPERTS, 1))],
        out_shape=[jax.ShapeDtypeStruct((bsz, t, D_MODEL), F32),
                   jax.ShapeDtypeStruct((n * ROW_TILES, LANES), F32),
                   jax.ShapeDtypeStruct((n, N_EXPERTS), F32),
                   jax.ShapeDtypeStruct((n, N_EXPERTS), F32),
                   jax.ShapeDtypeStruct((N_EXPERTS, 1), F32)],
        scratch_shapes=[pltpu.VMEM((N_EXPERTS, 1), F32)],
        compiler_params=_cparams(("arbitrary", "arbitrary")),
        name="out_proj",
    )(x, pool_out, o_gdn, gt, sc, sh, g_post_mix.reshape(1, D_MODEL), g_pre_ffn.reshape(1, D_MODEL),
      w_out_b, jnp.swapaxes(router_w, 0, 1), router_b.reshape(N_EXPERTS, 1), counts0)


def _dest_kernel(g_ref, r_ref, start_ref, base_ref, d_ref, l_ref, gt_ref, *, tm, sub):
    g = g_ref[...]
    mask = g >= 0.0
    maskf = mask.astype(F32)
    dest = start_ref[...] + r_ref[...]
    base = jnp.concatenate([jnp.broadcast_to(base_ref[0, q:q + 1, :], (sub, N_EXPERTS)) for q in range(tm // sub)],
                           axis=0)
    local = dest - base
    ri = lax.broadcasted_iota(jnp.int32, (N_EXPERTS, N_EXPERTS), 0)
    ci = lax.broadcasted_iota(jnp.int32, (N_EXPERTS, N_EXPERTS), 1)
    slot = _dot(maskf, (ri < ci).astype(F32))
    lane4 = lax.broadcasted_iota(jnp.int32, (tm, TOP_K), 1)
    d_out = jnp.zeros((tm, TOP_K), F32)
    l_out = jnp.zeros((tm, TOP_K), F32)
    g_out = jnp.zeros((tm, TOP_K), F32)
    for j in range(TOP_K):
        pick = mask & (slot == float(j))
        dj = jnp.sum(jnp.where(pick, dest, 0.0), axis=-1, keepdims=True)
        lj = jnp.sum(jnp.where(pick, local, 0.0), axis=-1, keepdims=True)
        gj = jnp.sum(jnp.where(pick, g, 0.0), axis=-1, keepdims=True)
        d_out = jnp.where(lane4 == j, dj, d_out)
        l_out = jnp.where(lane4 == j, lj, l_out)
        g_out = jnp.where(lane4 == j, gj, g_out)
    d_ref[...] = d_out.astype(jnp.int32)
    l_ref[...] = l_out.astype(jnp.int32)
    gt_ref[...] = g_out


def _dest(gsel, rank, start_row, base, tm, sub):
    n = gsel.shape[0]
    per = tm // sub
    kern = functools.partial(_dest_kernel, tm=tm, sub=sub)
    tok4 = pl.BlockSpec((tm, TOP_K), lambda i: (i, 0))
    return pl.pallas_call(
        kern,
        grid=(n // tm,),
        in_specs=[pl.BlockSpec((tm, N_EXPERTS), lambda i: (i, 0)),
                  pl.BlockSpec((tm, N_EXPERTS), lambda i: (i, 0)),
                  pl.BlockSpec((1, N_EXPERTS), lambda i: (0, 0)),
                  pl.BlockSpec((1, per, N_EXPERTS), lambda i: (i, 0, 0))],
        out_specs=[tok4, tok4, tok4],
        out_shape=[jax.ShapeDtypeStruct((n, TOP_K), jnp.int32),
                   jax.ShapeDtypeStruct((n, TOP_K), jnp.int32),
                   jax.ShapeDtypeStruct((n, TOP_K), F32)],
        compiler_params=_cparams(("arbitrary",)),
        name="dest",
    )(gsel, rank, start_row, base.reshape(n // tm, per, N_EXPERTS))


def _row_copy(src, dst, s_row, d_row, sem):
    return pltpu.make_async_copy(src.at[pl.ds(pl.multiple_of(s_row * ROW_TILES, ROW_TILES), ROW_TILES), :],
                                 dst.at[pl.ds(pl.multiple_of(d_row * ROW_TILES, ROW_TILES), ROW_TILES), :], sem)


def _scatter_kernel(cnt_ref, start_ref, end_ref, nu_ref, dest_ref, ha_ref, hb_ref, xs_ref, sem, *,
                    tm, bm, n_blocks, tiles_a):
    i = pl.program_id(0)

    def scatter_from(ht_ref):
        def body(tt, carry):
            for u in range(ROW_LOOP_UNROLL):
                t = tt * ROW_LOOP_UNROLL + u
                src = ht_ref.at[pl.ds(pl.multiple_of(t * ROW_TILES, ROW_TILES), ROW_TILES), :]
                for k in range(TOP_K):
                    d = dest_ref[t * TOP_K + k]
                    pltpu.make_async_copy(
                        src, xs_ref.at[pl.ds(pl.multiple_of(d * ROW_TILES, ROW_TILES), ROW_TILES), :],
                        sem).start(priority=k % DMA_QUEUES)
            return carry

        lax.fori_loop(0, tm // ROW_LOOP_UNROLL, body, 0)

    @pl.when(i < tiles_a)
    def _():
        scatter_from(ha_ref)

    @pl.when(i >= tiles_a)
    def _():
        scatter_from(hb_ref)

    ht_ref = hb_ref

    def tile_copy(u):
        return pltpu.make_async_copy(
            ht_ref, xs_ref.at[pl.ds(pl.multiple_of(u * (tm * ROW_TILES), tm * ROW_TILES), tm * ROW_TILES), :], sem)

    for k in range(TOP_K):
        tile_copy(0).wait()

    @pl.when(i == pl.num_programs(0) - 1)
    def _():
        per = bm // tm

        def tail_start(u, carry):
            tile_copy(u).start()
            return carry

        def tail_wait(u, carry):
            tile_copy(u).wait()
            return carry

        lax.fori_loop(nu_ref[0] * per, n_blocks * per, tail_start, 0)
        lax.fori_loop(nu_ref[0] * per, n_blocks * per, tail_wait, 0)

        chunk = PAD_CHUNK
        assert chunk <= tm and bm % chunk == 0

        def chunk_copy(q):
            return pltpu.make_async_copy(
                ht_ref.at[pl.ds(0, chunk * ROW_TILES), :],
                xs_ref.at[pl.ds(pl.multiple_of(q * (chunk * ROW_TILES), chunk * ROW_TILES), chunk * ROW_TILES), :],
                sem)

        def per_expert(e, carry):
            lo = start_ref[e] + cnt_ref[e]
            mid = (lo + chunk - 1) // chunk * chunk
            hi = end_ref[e]

            def fill(r, c2):
                _row_copy(ht_ref, xs_ref, 0, r, sem).start()
                return c2

            def fill_chunk(q, c2):
                chunk_copy(q).start()
                return c2

            lax.fori_loop(lo, mid, fill, 0)
            lax.fori_loop(mid // chunk, hi // chunk, fill_chunk, 0)

            def drain(r, c2):
                _row_copy(ht_ref, xs_ref, 0, r, sem).wait()
                return c2

            def drain_chunk(q, c2):
                chunk_copy(q).wait()
                return c2

            lax.fori_loop(lo, mid, drain, 0)
            lax.fori_loop(mid // chunk, hi // chunk, drain_chunk, 0)
            return carry

        lax.fori_loop(0, N_EXPERTS, per_expert, 0)


def _scatter_rows(ht_a, ht_b, dest_flat, cnt, start, end, n_used, n_blocks, tm, bm):
    rows = n_blocks * bm
    tiles_a, tiles_b = ht_a.shape[0] // (tm * ROW_TILES), ht_b.shape[0] // (tm * ROW_TILES)
    assert bm % tm == 0 and ht_a.shape[0] % (tm * ROW_TILES) == 0 and ht_b.shape[0] % (tm * ROW_TILES) == 0
    assert tiles_b >= 1
    kern = functools.partial(_scatter_kernel, tm=tm, bm=bm, n_blocks=n_blocks, tiles_a=tiles_a)
    return pl.pallas_call(
        kern,
        grid_spec=pltpu.PrefetchScalarGridSpec(
            num_scalar_prefetch=4,
            grid=(tiles_a + tiles_b,),
            in_specs=[pl.BlockSpec((tm * TOP_K,), lambda i, *_: (i,), memory_space=pltpu.SMEM),
                      pl.BlockSpec((tm * ROW_TILES, LANES), lambda i, *_: (jnp.minimum(i, tiles_a - 1), 0)),
                      pl.BlockSpec((tm * ROW_TILES, LANES), lambda i, *_: (jnp.maximum(i - tiles_a, 0), 0))],
            out_specs=pl.BlockSpec(memory_space=pl.ANY),
            scratch_shapes=[pltpu.SemaphoreType.DMA(())]),
        out_shape=jax.ShapeDtypeStruct((rows * ROW_TILES, LANES), F32),
        compiler_params=_cparams(("arbitrary",)),
        name="scatter_rows",
    )(cnt, start, end, n_used, dest_flat, ht_a, ht_b)


def _expert_kernel(be_ref, nx_ref, nv_ref, nu_ref, xs_ref, wg_ref, bg_ref, wd_ref, bd_ref, ys_ref,
                   wg_f, wd_f, wg_s, wd_s, wsem, *, bm):
    j = pl.program_id(0)
    prev = be_ref[jnp.maximum(j - 1, 0)]
    fresh = (j == 0) | (be_ref[j] != prev)
    active = j < nu_ref[0]

    def weight_copies(e):
        return (pltpu.make_async_copy(wg_ref.at[e], wg_f, wsem.at[0]),
                pltpu.make_async_copy(wd_ref.at[e], wd_f, wsem.at[1]))

    @pl.when(j == 0)
    def _():
        for c in weight_copies(be_ref[0]):
            c.start()

    @pl.when(active & fresh)
    def _():
        for c in weight_copies(be_ref[j]):
            c.wait()
        wg_s[...] = wg_f[...].astype(BF16)
        wd_s[...] = wd_f[...].astype(BF16)

        @pl.when(nx_ref[j] != be_ref[j])
        def _():
            for c in weight_copies(nx_ref[j]):
                c.start()

    def run_rows(m):
        x = jnp.concatenate([xs_ref[pl.ds(s, m, stride=ROW_TILES), :] for s in range(ROW_TILES)], axis=-1)
        xb = x.astype(BF16)
        acts = []
        for c in range(EXPERT_COL_CHUNKS):
            w = D_FF // EXPERT_COL_CHUNKS
            lo = c * w
            gate = jnp.dot(xb, wg_s[:, lo:lo + w], preferred_element_type=F32) + bg_ref[0, :, lo:lo + w]
            up = (jnp.dot(xb, wg_s[:, D_FF + lo:D_FF + lo + w], preferred_element_type=F32)
                  + bg_ref[0, :, D_FF + lo:D_FF + lo + w])
            gate = jnp.minimum(gate, SWIGLU_LIMIT)
            up = jnp.clip(up, -SWIGLU_LIMIT, SWIGLU_LIMIT)
            acts.append(((up + 1.0) * (gate * jax.nn.sigmoid(SWIGLU_ALPHA * gate))).astype(BF16))
        act = jnp.concatenate(acts, axis=-1)
        for c in range(EXPERT_COL_CHUNKS):
            w = D_MODEL // EXPERT_COL_CHUNKS
            lo = c * w
            y = jnp.dot(act, wd_s[:, lo:lo + w], preferred_element_type=F32) + bd_ref[0, :, lo:lo + w]
            for s in range(w // LANES):
                ys_ref[pl.ds(lo // LANES + s, m, stride=ROW_TILES), :] = y[:, s * LANES:(s + 1) * LANES]
        if m < bm:
            ys_ref[pl.ds(m * ROW_TILES, (bm - m) * ROW_TILES), :] = jnp.zeros(((bm - m) * ROW_TILES, LANES), F32)

    quarter = bm // 4
    quarters = (nv_ref[j] + quarter - 1) // quarter
    for nq in range(1, 5):
        @pl.when(active & ((quarters == nq) if nq > 1 else (quarters <= 1)))
        def _(nq=nq):
            run_rows(nq * quarter)

    @pl.when(jnp.logical_not(active))
    def _():
        ys_ref[...] = jnp.zeros_like(ys_ref)


def _experts(xs, block_expert, next_expert, block_valid, n_used, w_gu, b_gu, w_down, b_down, bm):
    rows = xs.shape[0] // ROW_TILES
    n_blocks = rows // bm

    def blk(j, be, nx, nv, nu):
        return (jnp.minimum(j, nu[0] - 1), 0)

    def bsel(j, be, nx, nv, nu):
        return (be[j], 0, 0)

    hbm = pl.BlockSpec(memory_space=pl.ANY)
    kern = functools.partial(_expert_kernel, bm=bm)
    return pl.pallas_call(
        kern,
        grid_spec=pltpu.PrefetchScalarGridSpec(
            num_scalar_prefetch=4,
            grid=(n_blocks,),
            in_specs=[pl.BlockSpec((bm * ROW_TILES, LANES), blk), hbm,
                      pl.BlockSpec((1, 1, 2 * D_FF), bsel), hbm,
                      pl.BlockSpec((1, 1, D_MODEL), bsel)],
            out_specs=pl.BlockSpec((bm * ROW_TILES, LANES), lambda j, be, nx, nv, nu: (j, 0)),
            scratch_shapes=[pltpu.VMEM((D_MODEL, 2 * D_FF), F32),
                            pltpu.VMEM((D_FF, D_MODEL), F32),
                            pltpu.VMEM((D_MODEL, 2 * D_FF), BF16),
                            pltpu.VMEM((D_FF, D_MODEL), BF16),
                            pltpu.SemaphoreType.DMA((2,))]),
        out_shape=jax.ShapeDtypeStruct((rows * ROW_TILES, LANES), F32),
        compiler_params=_cparams(("arbitrary",)),
        name="experts",
    )(block_expert, next_expert, block_valid, n_used, xs, w_gu, b_gu.reshape(N_EXPERTS, 1, 2 * D_FF), w_down,
      b_down.reshape(N_EXPERTS, 1, D_MODEL))


def _final_kernel(loc_ref, gate_ref, tcur_ref, tnext_ref, ys_ref, x1_ref, gt_ref, g_ref, o_ref,
                  st_ref, f_ref, sem, *, tm, steps):
    g = pl.program_id(0) * pl.num_programs(1) + pl.program_id(1)
    cur = g % 2
    sizes = [1 << b for b in range(tm.bit_length() - 1, -1, -1)]

    def request(t_ref, slot):
        def per_expert(e, off):
            row = t_ref[e]
            cnt = t_ref[N_EXPERTS + e]
            for size in sizes:
                take = (cnt & size) != 0

                @pl.when(take)
                def _(row=row, off=off, size=size):
                    pltpu.make_async_copy(
                        ys_ref.at[pl.ds(pl.multiple_of(row * ROW_TILES, ROW_TILES), size * ROW_TILES), :],
                        st_ref.at[slot, pl.ds(pl.multiple_of(off * ROW_TILES, ROW_TILES), size * ROW_TILES), :],
                        sem.at[slot]).start()

                step = jnp.where(take, size, 0)
                row = row + step
                off = off + step
            return off

        lax.fori_loop(0, N_EXPERTS, per_expert, 0)

    @pl.when(g == 0)
    def _():
        request(tcur_ref, 0)

    @pl.when(g + 1 < steps)
    def _():
        request(tnext_ref, 1 - cur)

    pltpu.make_async_copy(ys_ref.at[pl.ds(0, tm * TOP_K * ROW_TILES), :], st_ref.at[cur], sem.at[cur]).wait()

    def pick(tt, carry):
        for u in range(ROW_LOOP_UNROLL):
            t = tt * ROW_LOOP_UNROLL + u
            for k in range(TOP_K):
                at = pl.multiple_of(loc_ref[t * TOP_K + k] * ROW_TILES, ROW_TILES)
                f_ref[k, pl.ds(pl.multiple_of(t * ROW_TILES, ROW_TILES), ROW_TILES), :] = (
                    st_ref[cur, pl.ds(at, ROW_TILES), :])
        return carry

    lax.fori_loop(0, tm // ROW_LOOP_UNROLL, pick, 0)
    gates = gate_ref[...]
    cols = []
    for s in range(ROW_TILES):
        acc = None
        for k in range(TOP_K):
            part = f_ref[k, pl.ds(s, tm, stride=ROW_TILES), :] * gates[:, k:k + 1]
            acc = part if acc is None else acc + part
        cols.append(acc)
    f = jnp.concatenate(cols, axis=-1)
    o_ref[0] = x1_ref[0] + gt_ref[0] * _rms(f, g_ref[...])


def _final(ys, local_flat, gates, run_table, x1, gt, g_post_ffn, tok_off, tm, per_token):
    bsz, t, _ = x1.shape
    tiles = t // tm
    off = tok_off // tm
    steps = bsz * tiles
    kern = functools.partial(_final_kernel, tm=tm, steps=steps)

    def smem(ahead):
        return pl.BlockSpec((tm * TOP_K,), lambda b, i: (off + jnp.minimum(b * tiles + i + ahead, steps - 1),),
                            memory_space=pltpu.SMEM)

    return pl.pallas_call(
        kern,
        grid=(bsz, tiles),
        in_specs=[smem(0), pl.BlockSpec((tm, TOP_K), lambda b, i: (off + b * tiles + i, 0)), smem(0), smem(1),
                  pl.BlockSpec(memory_space=pl.ANY),
                  pl.BlockSpec((1, tm, D_MODEL), lambda b, i: (b, i, 0)),
                  _mod_spec(per_token, tm),
                  pl.BlockSpec((1, D_MODEL), lambda b, i: (0, 0))],
        out_specs=pl.BlockSpec((1, tm, D_MODEL), lambda b, i: (b, i, 0)),
        out_shape=jax.ShapeDtypeStruct((bsz, t, D_MODEL), F32),
        scratch_shapes=[pltpu.VMEM((2, tm * TOP_K * ROW_TILES, LANES), F32),
                        pltpu.VMEM((TOP_K, tm * ROW_TILES, LANES), F32),
                        pltpu.SemaphoreType.DMA((2,))],
        compiler_params=_cparams(("arbitrary", "arbitrary")),
        name="final",
    )(local_flat, gates, run_table, run_table, ys, x1, gt, g_post_ffn.reshape(1, D_MODEL))


def _moe(ht_a, ht_b, gsel, rank, counts, w_gu, b_gu, w_down, b_down, tm, bm):
    n = gsel.shape[0]
    cnt = counts[:, 0].astype(jnp.int32)
    nblk = (cnt + bm - 1) // bm
    blk_end = jnp.cumsum(nblk)
    blk_start = blk_end - nblk
    n_blocks = -(-(n * TOP_K) // bm) + N_EXPERTS
    blocks = jnp.arange(n_blocks, dtype=jnp.int32)
    block_expert = jnp.minimum(jnp.sum((blk_end[None, :] <= blocks[:, None]).astype(jnp.int32), axis=1),
                               N_EXPERTS - 1).astype(jnp.int32)
    ids = jnp.arange(N_EXPERTS, dtype=jnp.int32)
    later = jnp.where((nblk[None, :] > 0) & (ids[None, :] > ids[:, None]), ids[None, :], N_EXPERTS)
    nearest = jnp.min(later, axis=1)
    next_used = jnp.where(nearest < N_EXPERTS, nearest, ids).astype(jnp.int32)
    n_used = blk_end[-1:].astype(jnp.int32)
    row_start = (blk_start * bm).astype(jnp.int32)
    row_end = (blk_end * bm).astype(jnp.int32)
    before = rank[::tm]
    run_len = jnp.concatenate([before[1:], counts.reshape(1, N_EXPERTS)], axis=0) - before
    run_row = row_start.astype(F32)[None, :] + before
    run_off = jnp.cumsum(run_len, axis=1) - run_len
    tiles = n // tm
    run_table = jnp.zeros((tiles, tm * TOP_K), jnp.int32)
    run_table = run_table.at[:, :N_EXPERTS].set(run_row.astype(jnp.int32))
    run_table = run_table.at[:, N_EXPERTS:2 * N_EXPERTS].set(run_len.astype(jnp.int32)).reshape(tiles * tm * TOP_K)
    dest, local, gates = _dest(gsel, rank, row_start.astype(F32).reshape(1, N_EXPERTS), run_row - run_off,
                               math.gcd(n, DEST_TILE), tm)
    dest_flat = dest.reshape(n * TOP_K)
    xs = _scatter_rows(ht_a, ht_b, dest_flat, cnt, row_start, row_end, n_used, n_blocks, SCATTER_TILE, bm)
    owns = (blocks[:, None] >= blk_start[None, :]) & (blocks[:, None] < blk_end[None, :])

    def per_block(table):
        return jnp.sum(jnp.where(owns, table[None, :], 0), axis=1).astype(jnp.int32)

    block_valid = jnp.clip(per_block(cnt) - (blocks - per_block(blk_start)) * bm, 0, bm).astype(jnp.int32)
    ys = _experts(xs, block_expert, per_block(next_used), block_valid, n_used, w_gu, b_gu, w_down, b_down, bm)
    return ys, local.reshape(n * TOP_K), gates, run_table


def _mix_path(x, mod, pool_buf, conv_buf, s0, start_pos, counts0, lp, tm, nb, per_token):
    (g_pre_mix, g_post_mix, g_pre_ffn, w_in_p, w_bd, pool_scale, conv_w, alog_row, dtb_row, norm_w,
     w_out_b, router_w, router_b) = lp
    bsz, t, _ = x.shape
    sh_mix, sc_mix, gt_mix, sh_ffn, sc_ffn, gt_ffn = (mod[:, i] for i in range(6))

    def shape_mod(m):
        if per_token:
            return jnp.repeat(m, t, axis=0).reshape(1, bsz * t, D_MODEL)
        return m.reshape(bsz, 1, D_MODEL)

    sh_mix, sc_mix, gt_mix, sh_ffn, sc_ffn, gt_ffn = map(shape_mod, (sh_mix, sc_mix, gt_mix, sh_ffn, sc_ffn, gt_ffn))
    xf = x.reshape(1, bsz * t, D_MODEL) if per_token else x
    proj = _in_proj(xf, sc_mix, sh_mix, g_pre_mix, w_in_p, PROJ_TILE, per_token).reshape(bsz, t, PROJ_WIDTH)
    tseq = min(tm, t)
    o_gdn, s_new = _gdn(proj, conv_buf, s0, conv_w, alog_row, dtb_row, norm_w, nb, tseq)
    pool_out = _pool(proj, pool_buf, w_bd, pool_scale, nb * POOL_SEQS if per_token else nb, min(POOL_TILE, t),
                     start_pos)
    fl = (lambda a: a.reshape(1, bsz * t, a.shape[-1])) if per_token else (lambda a: a)
    x1, ht, gsel, rank, counts = _out_proj(xf, fl(pool_out), fl(o_gdn), gt_mix, sc_ffn, sh_ffn, g_post_mix,
                                           g_pre_ffn, w_out_b, router_w, router_b, counts0, OUT_TILE, per_token)

    def last_rows(buf, off, width):
        keep = buf.shape[1]
        if t >= keep:
            return proj[:, t - keep:, off:off + width]
        return jnp.concatenate([buf[:, t:], proj[:, :, off:off + width]], axis=1)

    pool_new = last_rows(pool_buf, OFF_U, POOL_WIDTH)
    conv_new = last_rows(conv_buf, OFF_QKV, CONV_CH)
    return x1, ht, gsel, rank, counts, gt_ffn, pool_new, conv_new, s_new


def kernel(x_prompt, x_sample, c_prompt, c_sample, state_pool, state_conv, state_ssm, w_ada, b_ada, g_pre_mix, g_post_mix, g_pre_ffn, g_post_ffn, w_in, w_pool, pool_scale, conv_w, a_log, dt_bias, gdn_norm_w, w_out, router_w, router_b, w_gu, b_gu, w_down, b_down):
    depth = w_ada.shape[0]
    bp, tp, _ = x_prompt.shape
    bs, ts, _ = x_sample.shape
    tm = TOKEN_TILE
    y_p, y_s = x_prompt, x_sample
    outs = [[] for _ in range(6)]
    for l in range(depth):
        mod = _ada(jnp.concatenate([c_prompt, c_sample], axis=0), w_ada[l], b_ada[l]).reshape(bp + bs, 6, D_MODEL)
        w_in_p = jnp.swapaxes(w_in[l], 0, 1)
        w_bd = jax.scipy.linalg.block_diag(*[w_pool[l, g] for g in range(len(POOL_WINDOWS))])
        pad_row = lambda v: jnp.pad(v, (0, LANES - v.shape[0])).reshape(1, LANES)
        lp = (g_pre_mix[l], g_post_mix[l], g_pre_ffn[l], w_in_p, w_bd, pool_scale[l], conv_w[l],
              pad_row(a_log[l]), pad_row(dt_bias[l]), gdn_norm_w[l].reshape(1, HEAD_DIM),
              w_out[l].astype(BF16), router_w[l], router_b[l])
        zero_pool = jnp.zeros((bp, POOL_BUF, POOL_WIDTH), F32)
        zero_conv = jnp.zeros((bp, CONV_WIDTH - 1, CONV_CH), F32)
        zero_ssm = (jnp.zeros((1, bp, GDN_HEADS, HEAD_DIM, HEAD_DIM), F32), 0)
        no_counts = jnp.zeros((N_EXPERTS, 1), F32)
        x1p, htp, gp, rp, cnt_p, gtp, pp, cp, sp = _mix_path(
            y_p, mod[:bp], zero_pool, zero_conv, zero_ssm, 0, no_counts, lp, tm, 1, False)
        x1s, hts, gs, rs, counts, gts, ps, cs, ss = _mix_path(
            y_s, mod[bp:], state_pool[l], state_conv[l], (state_ssm, l), PAST_LEN, cnt_p, lp, tm, 8, True)
        gsel = jnp.concatenate([gp, gs], axis=0)
        rank = jnp.concatenate([rp, rs], axis=0)
        ys, local, gates, runs = _moe(htp, hts, gsel, rank, counts, w_gu[l], b_gu[l], w_down[l], b_down[l],
                                      FINAL_TILE, MOE_BLOCK)
        y_p = _final(ys, local, gates, runs, x1p, gtp, g_post_ffn[l], 0, FINAL_TILE, False)
        y_s = _final(ys, local, gates, runs, x1s, gts, g_post_ffn[l], bp * tp, FINAL_TILE,
                     True).reshape(bs, ts, D_MODEL)
        for lst, val in zip(outs, (pp, cp, sp, ps, cs, ss)):
            lst.append(val)
    return (y_p, y_s) + tuple(o[0][None] if depth == 1 else jnp.stack(o) for o in outs)
```

```python
import functools
import math

import jax
import jax.numpy as jnp
from jax import lax
from jax.experimental import pallas as pl
from jax.experimental.pallas import tpu as pltpu

F32 = jnp.float32
BF16 = jnp.bfloat16

D_MODEL = 1024
PAST_LEN = 16384
POOL_WINDOWS = (2, 4, 8, 16)
POOL_WIDTH = D_MODEL // 4
POOL_GROUP = POOL_WIDTH // len(POOL_WINDOWS)
POOL_BUF = max(POOL_WINDOWS) - 1
GDN_WIDTH = D_MODEL - POOL_WIDTH
HEAD_DIM = 128
GDN_HEADS = GDN_WIDTH // HEAD_DIM
CONV_WIDTH = 4
CONV_CH = 3 * GDN_WIDTH
GDN_CHUNK = 64
N_EXPERTS = 32
TOP_K = 4
D_FF = D_MODEL
SWIGLU_LIMIT = 7.0
SWIGLU_ALPHA = 1.702
EPS = 1e-6

SUBLANES = 8
LANES = 128
ROW_TILES = D_MODEL // LANES
OFF_QKV = 0
OFF_Z = OFF_QKV + CONV_CH
OFF_U = OFF_Z + GDN_WIDTH
OFF_AB = OFF_U + POOL_WIDTH
PROJ_WIDTH = OFF_AB + LANES
assert OFF_Z % GDN_WIDTH == 0 and OFF_U % POOL_WIDTH == 0 and OFF_AB % LANES == 0
TOKEN_TILE = 256
PROJ_TILE = 512
SCATTER_TILE = 512
OUT_TILE = 512
FINAL_TILE = 512
DEST_TILE = 5632
PAD_CHUNK = 64
POOL_TILE = 1024
POOL_SEQS = 4
GDN_ITEM_GROUPS = 1
MOE_BLOCK = 1024
EXPERT_COL_CHUNKS = 2
EXPERT_IN_BUFFERS = 3
DMA_QUEUES = 2
ROW_LOOP_UNROLL = 8
VMEM_LIMIT = 56 * 1024 * 1024


def _cparams(sem):
    return pltpu.CompilerParams(dimension_semantics=sem, vmem_limit_bytes=VMEM_LIMIT)


def _dot(a, b):
    return jnp.dot(a.astype(BF16), b.astype(BF16), preferred_element_type=F32)


def _dot_nt(a, b):
    return lax.dot_general(a.astype(BF16), b.astype(BF16), (((1,), (1,)), ((), ())),
                           preferred_element_type=F32)


def _silu(x):
    return x * jax.nn.sigmoid(x)


def _rms(x, g):
    return x * lax.rsqrt(jnp.mean(x * x, axis=-1, keepdims=True) + EPS) * g


def _ada_kernel(c_ref, w_ref, b_ref, o_ref):
    o_ref[...] = _dot(_silu(c_ref[...]), w_ref[...]) + b_ref[...]


def _ada(c, w_ada, b_ada):
    n = c.shape[0]
    cols = w_ada.shape[1]
    blk = D_MODEL
    return pl.pallas_call(
        _ada_kernel,
        grid=(cols // blk,),
        in_specs=[pl.BlockSpec((n, D_MODEL), lambda j: (0, 0)),
                  pl.BlockSpec((D_MODEL, blk), lambda j: (0, j)),
                  pl.BlockSpec((1, blk), lambda j: (0, j))],
        out_specs=pl.BlockSpec((n, blk), lambda j: (0, j)),
        out_shape=jax.ShapeDtypeStruct((n, cols), F32),
        compiler_params=_cparams(("arbitrary",)),
        name="ada",
    )(c, w_ada, b_ada.reshape(1, cols))


def _in_kernel(x_ref, sc_ref, sh_ref, g_ref, wt_ref, o_ref, wb):
    @pl.when((pl.program_id(0) == 0) & (pl.program_id(1) == 0))
    def _():
        o1, o2, o3 = POOL_WIDTH, POOL_WIDTH + CONV_CH, POOL_WIDTH + CONV_CH + GDN_WIDTH
        wb[OFF_QKV:OFF_QKV + CONV_CH] = wt_ref[o1:o2].astype(BF16)
        wb[OFF_Z:OFF_Z + GDN_WIDTH] = wt_ref[o2:o3].astype(BF16)
        wb[OFF_U:OFF_U + POOL_WIDTH] = wt_ref[:o1].astype(BF16)
        tail = wt_ref.shape[0] - o3
        wb[OFF_AB:] = jnp.concatenate(
            [wt_ref[o3:], jnp.zeros((LANES - tail, D_MODEL), F32)], axis=0).astype(BF16)

    h = _rms(x_ref[0], g_ref[...]) * (1.0 + sc_ref[0]) + sh_ref[0]
    o_ref[0] = _dot_nt(h, wb[...])


def _mod_spec(per_token, tm):
    if per_token:
        return pl.BlockSpec((1, tm, D_MODEL), lambda b, t: (0, t, 0))
    return pl.BlockSpec((1, 1, D_MODEL), lambda b, t: (b, 0, 0))


def _in_proj(x, sc, sh, g, w_in, tm, per_token):
    bsz, t, _ = x.shape
    return pl.pallas_call(
        _in_kernel,
        grid=(bsz, t // tm),
        in_specs=[pl.BlockSpec((1, tm, D_MODEL), lambda b, i: (b, i, 0)),
                  _mod_spec(per_token, tm), _mod_spec(per_token, tm),
                  pl.BlockSpec((1, D_MODEL), lambda b, i: (0, 0)),
                  pl.BlockSpec(w_in.shape, lambda b, i: (0, 0), pipeline_mode=pl.Buffered(1))],
        out_specs=pl.BlockSpec((1, tm, PROJ_WIDTH), lambda b, i: (b, i, 0)),
        out_shape=jax.ShapeDtypeStruct((bsz, t, PROJ_WIDTH), F32),
        scratch_shapes=[pltpu.VMEM((PROJ_WIDTH, D_MODEL), BF16)],
        compiler_params=_cparams(("arbitrary", "arbitrary")),
        name="in_proj",
    )(x, sc, sh, g.reshape(1, D_MODEL), w_in)


def _cumsum_rows(x):
    n = x.shape[0]
    row = lax.broadcasted_iota(jnp.int32, x.shape, 0)
    s = 1
    while s < n:
        x = x + jnp.where(row >= s, pltpu.roll(x, s, axis=0), 0.0)
        s *= 2
    return x


def _gdn_kernel(qkv_ref, z_ref, ab_ref, cs_ref, cw_ref, s0_ref, alog_ref, dtb_ref, nw_ref, o_ref, sn_ref,
                s_scr, prev_scr, *, nb, tm, tmp, chunk, group_items):
    ti = pl.program_id(1)
    nt = pl.num_programs(1)
    n_chunks = tmp // chunk
    n_sq = int(math.log2(chunk)) - 1

    @pl.when(ti == 0)
    def _():
        s_scr[...] = s0_ref[...]
        for b in range(nb):
            prev_scr[b] = jnp.concatenate(
                [jnp.zeros((SUBLANES - (CONV_WIDTH - 1), CONV_CH), F32), cs_ref[b]], axis=0)

    rowi = lax.broadcasted_iota(jnp.int32, (tmp, LANES), 0)
    ri = lax.broadcasted_iota(jnp.int32, (chunk, chunk), 0)
    ci = lax.broadcasted_iota(jnp.int32, (chunk, chunk), 1)
    tril = ri >= ci
    strict = ri > ci
    eye = (ri == ci).astype(F32)
    neg_a = -jnp.exp(alog_ref[...])
    dtb = dtb_ref[...]
    nw = nw_ref[...]

    def padrows(x):
        if tm == tmp:
            return x
        return jnp.concatenate([x, jnp.zeros((tmp - tm, x.shape[1]), F32)], axis=0)

    cw = cw_ref[...]
    valid = rowi < tm
    items = []
    for b in range(nb):
        x = padrows(qkv_ref[b])
        e = jnp.concatenate([prev_scr[b], x], axis=0)
        acc = x * cw[CONV_WIDTH - 1:CONV_WIDTH]
        for d in range(1, CONV_WIDTH):
            acc = acc + pltpu.roll(e, d, axis=0)[SUBLANES:] * cw[CONV_WIDTH - 1 - d:CONV_WIDTH - d]
        if tm >= SUBLANES:
            prev_scr[b] = x[tm - SUBLANES:tm]
        y = _silu(acc)
        ab = padrows(ab_ref[b])
        g_all = jnp.where(valid, neg_a * jax.nn.softplus(ab + dtb), 0.0)
        beta_all = jnp.where(valid, jax.nn.sigmoid(ab), 0.0)
        for h in range(GDN_HEADS):
            qc = y[:, h * HEAD_DIM:(h + 1) * HEAD_DIM]
            kc = y[:, GDN_WIDTH + h * HEAD_DIM:GDN_WIDTH + (h + 1) * HEAD_DIM]
            v = y[:, 2 * GDN_WIDTH + h * HEAD_DIM:2 * GDN_WIDTH + (h + 1) * HEAD_DIM]
            q = qc * lax.rsqrt(jnp.sum(qc * qc, axis=-1, keepdims=True) + EPS) * (HEAD_DIM ** -0.5)
            k = kc * lax.rsqrt(jnp.sum(kc * kc, axis=-1, keepdims=True) + EPS)
            gb = jnp.broadcast_to(g_all[:, h:h + 1], (tmp, LANES))
            bb = jnp.broadcast_to(beta_all[:, GDN_HEADS + h:GDN_HEADS + h + 1], (tmp, LANES))
            for c in range(n_chunks):
                sl = slice(c * chunk, (c + 1) * chunk)
                items.append(dict(b=b, h=h, c=c, q=q[sl], k=k[sl], v=v[sl], bb=bb[sl], gb=gb[sl]))

    def within_chunk(group):
        for it in group:
            gcb = _cumsum_rows(it["gb"])
            gct = jnp.transpose(gcb)[:chunk]
            diff = gcb[:, :chunk] - gct
            it["decay"] = jnp.where(tril, jnp.exp(jnp.where(tril, diff, 0.0)), 0.0)
            it["kb"] = it["k"] * it["bb"]
            it["eg"] = jnp.exp(gcb)
            gl = gcb[chunk - 1:chunk]
            it["egl"] = jnp.exp(gl)
            it["kdt"] = jnp.transpose(it["k"] * jnp.exp(gl - gcb))
        for it in group:
            kq = _dot_nt(jnp.concatenate([it["kb"], it["q"]], axis=0), it["k"])
            it["a"] = jnp.where(strict, kq[:chunk] * it["decay"], 0.0)
            it["qk"] = jnp.where(tril, kq[chunk:] * it["decay"], 0.0)
        for it in group:
            it["inv"] = eye - it["a"]
            it["p"] = _dot(it["a"], it["a"])
        for lvl in range(n_sq):
            for it in group:
                if lvl < n_sq - 1:
                    r = _dot(jnp.concatenate([it["inv"], it["p"]], axis=0), it["p"])
                    it["inv"] = it["inv"] + r[:chunk]
                    it["p"] = r[chunk:]
                else:
                    it["inv"] = it["inv"] + _dot(it["inv"], it["p"])
        for it in group:
            it["wu"] = _dot(it["inv"], jnp.concatenate([it["kb"] * it["eg"], it["v"] * it["bb"]], axis=1))
        for it in group:
            mn = _dot(it["kdt"], it["wu"])
            qo = _dot(it["qk"], it["wu"])
            it["mp"], it["nc"] = mn[:, :HEAD_DIM], mn[:, HEAD_DIM:]
            it["qp"] = it["q"] * it["eg"] - qo[:, :HEAD_DIM]
            it["oc"] = qo[:, HEAD_DIM:]

    for g0 in range(0, len(items), group_items):
        within_chunk(items[g0:g0 + group_items])

    state = {(b, h): s_scr[b, h] for b in range(nb) for h in range(GDN_HEADS)}
    for c in range(n_chunks):
        for it in items:
            if it["c"] != c:
                continue
            s = state[(it["b"], it["h"])]
            r = _dot(jnp.concatenate([it["qp"], it["mp"]], axis=0), s)
            it["o"] = r[:chunk] + it["oc"]
            state[(it["b"], it["h"])] = s * it["egl"] - r[chunk:] + it["nc"]
    for (b, h), s in state.items():
        s_scr[b, h] = s

    for b in range(nb):
        zf = padrows(z_ref[b])
        for h in range(GDN_HEADS):
            outs = [it["o"] for it in items if it["b"] == b and it["h"] == h]
            o = outs[0] if n_chunks == 1 else jnp.concatenate(outs, axis=0)
            o = (o * lax.rsqrt(jnp.mean(o * o, axis=-1, keepdims=True) + EPS) * nw
                 * _silu(zf[:, h * HEAD_DIM:(h + 1) * HEAD_DIM]))
            o_ref[b, :, h * HEAD_DIM:(h + 1) * HEAD_DIM] = o[:tm]

    @pl.when(ti == nt - 1)
    def _():
        sn_ref[...] = s_scr[...]


def _gdn(proj, state_conv, s0, conv_w, alog_row, dtb_row, norm_w, nb, tm):
    bsz, t, _ = proj.shape
    tmp = max(tm, SUBLANES)
    chunk = min(GDN_CHUNK, tmp)
    row = pl.BlockSpec((1, LANES), lambda b, i: (0, 0))
    s_all, layer = s0
    sspec = pl.BlockSpec((nb, GDN_HEADS, HEAD_DIM, HEAD_DIM), lambda b, i: (b, 0, 0, 0))
    s_in = pl.BlockSpec((None, nb, GDN_HEADS, HEAD_DIM, HEAD_DIM), lambda b, i: (layer, b, 0, 0, 0))
    n_items = nb * GDN_HEADS * (tmp // chunk)
    kern = functools.partial(_gdn_kernel, nb=nb, tm=tm, tmp=tmp, chunk=chunk,
                             group_items=n_items // GDN_ITEM_GROUPS)
    return pl.pallas_call(
        kern,
        grid=(bsz // nb, t // tm),
        in_specs=[pl.BlockSpec((nb, tm, CONV_CH), lambda b, i: (b, i, OFF_QKV // CONV_CH)),
                  pl.BlockSpec((nb, tm, GDN_WIDTH), lambda b, i: (b, i, OFF_Z // GDN_WIDTH)),
                  pl.BlockSpec((nb, tm, LANES), lambda b, i: (b, i, OFF_AB // LANES)),
                  pl.BlockSpec((nb, CONV_WIDTH - 1, CONV_CH), lambda b, i: (b, 0, 0)),
                  pl.BlockSpec((CONV_WIDTH, CONV_CH), lambda b, i: (0, 0)),
                  s_in, row, row, row],
        out_specs=[pl.BlockSpec((nb, tm, GDN_WIDTH), lambda b, i: (b, i, 0)), sspec],
        out_shape=[jax.ShapeDtypeStruct((bsz, t, GDN_WIDTH), F32),
                   jax.ShapeDtypeStruct(s_all.shape[1:], F32)],
        scratch_shapes=[pltpu.VMEM((nb, GDN_HEADS, HEAD_DIM, HEAD_DIM), F32),
                        pltpu.VMEM((nb, SUBLANES, CONV_CH), F32)],
        compiler_params=_cparams(("arbitrary", "arbitrary")),
        name="gdn",
    )(proj, proj, proj, state_conv, conv_w, s_all, alog_row, dtb_row, norm_w)


def _pool_kernel(u_ref, st_ref, w_ref, sc_ref, o_ref, prev_scr, *, nb, tm, tmp, start_pos):
    ti = pl.program_id(1)
    halo = 2 * SUBLANES

    @pl.when(ti == 0)
    def _():
        for b in range(nb):
            prev_scr[b] = jnp.concatenate([jnp.zeros((halo - POOL_BUF, POOL_WIDTH), F32), st_ref[b]], axis=0)

    lane = lax.broadcasted_iota(jnp.int32, (tmp, POOL_WIDTH), 1)
    rowi = lax.broadcasted_iota(jnp.int32, (tmp, POOL_WIDTH), 0)
    pos1 = start_pos + ti * tm + rowi + 1
    grp = lane // POOL_GROUP
    win = jnp.where(grp == 0, POOL_WINDOWS[0],
                    jnp.where(grp == 1, POOL_WINDOWS[1], jnp.where(grp == 2, POOL_WINDOWS[2], POOL_WINDOWS[3])))
    cnt = jnp.minimum(pos1, win).astype(F32)
    for b in range(nb):
        x = u_ref[b]
        if tm != tmp:
            x = jnp.concatenate([x, jnp.zeros((tmp - tm, POOL_WIDTH), F32)], axis=0)
        e = jnp.concatenate([prev_scr[b], x], axis=0)
        sums = []
        s = e
        for w in POOL_WINDOWS:
            s = s + pltpu.roll(s, w // 2, axis=0)
            sums.append(s[halo:])
        sel = jnp.where(grp == 0, sums[0], jnp.where(grp == 1, sums[1], jnp.where(grp == 2, sums[2], sums[3])))
        pooled = sel / cnt - x
        out = _dot(pooled, w_ref[...]) * sc_ref[...]
        o_ref[b] = out[:tm]
        if tm >= halo:
            prev_scr[b] = x[tm - halo:tm]


def _pool(proj, state_pool, w_bd, pool_scale, nb, tm, start_pos):
    bsz, t, _ = proj.shape
    tmp = max(tm, SUBLANES)
    kern = functools.partial(_pool_kernel, nb=nb, tm=tm, tmp=tmp, start_pos=start_pos)
    return pl.pallas_call(
        kern,
        grid=(bsz // nb, t // tm),
        in_specs=[pl.BlockSpec((nb, tm, POOL_WIDTH), lambda b, i: (b, i, OFF_U // POOL_WIDTH)),
                  pl.BlockSpec((nb, POOL_BUF, POOL_WIDTH), lambda b, i: (b, 0, 0)),
                  pl.BlockSpec((POOL_WIDTH, POOL_WIDTH), lambda b, i: (0, 0)),
                  pl.BlockSpec((1, POOL_WIDTH), lambda b, i: (0, 0))],
        out_specs=pl.BlockSpec((nb, tm, POOL_WIDTH), lambda b, i: (b, i, 0)),
        out_shape=jax.ShapeDtypeStruct((bsz, t, POOL_WIDTH), F32),
        scratch_shapes=[pltpu.VMEM((nb, 2 * SUBLANES, POOL_WIDTH), F32)],
        compiler_params=_cparams(("arbitrary", "arbitrary")),
        name="pool",
    )(proj, state_pool, w_bd, pool_scale.reshape(1, POOL_WIDTH))


def _out_kernel(x_ref, po_ref, og_ref, gt_ref, sc_ref, sh_ref, gpm_ref, gpf_ref, wo_ref, rw_ref, rb_ref, c0_ref,
                x1_ref, ht_ref, g_ref, r_ref, cnt_ref, carry, *, tm):
    @pl.when((pl.program_id(0) == 0) & (pl.program_id(1) == 0))
    def _():
        carry[...] = c0_ref[...]

    mix = _dot(po_ref[0], wo_ref[:POOL_WIDTH]) + _dot(og_ref[0], wo_ref[POOL_WIDTH:])
    x1 = x_ref[0] + gt_ref[0] * _rms(mix, gpm_ref[...])
    x1_ref[0] = x1
    h = _rms(x1, gpf_ref[...]) * (1.0 + sc_ref[0]) + sh_ref[0]
    for s in range(ROW_TILES):
        ht_ref[pl.ds(s, tm, stride=ROW_TILES), :] = h[:, s * LANES:(s + 1) * LANES]
    logits = _dot_nt(rw_ref[...], h) + rb_ref[...]
    eid = lax.broadcasted_iota(jnp.int32, logits.shape, 0).astype(F32)
    work = logits
    sel = jnp.zeros(logits.shape, jnp.bool_)
    top = None
    den = jnp.zeros((1, tm), F32)
    for j in range(TOP_K):
        m = jnp.max(work, axis=0, keepdims=True)
        idx = jnp.min(jnp.where(work == m, eid, float(N_EXPERTS)), axis=0, keepdims=True)
        hit = eid == idx
        sel = sel | hit
        work = jnp.where(hit, -jnp.inf, work)
        if j == 0:
            top = m
        den = den + jnp.exp(m - top)
    gates = jnp.exp(logits - top) / den
    g_ref[...] = jnp.transpose(jnp.where(sel, gates, -1.0))
    mask = sel.astype(F32)
    ri = lax.broadcasted_iota(jnp.int32, (tm, tm), 0)
    ci = lax.broadcasted_iota(jnp.int32, (tm, tm), 1)
    c = carry[...]
    r_ref[...] = jnp.transpose(_dot(mask, (ri < ci).astype(F32)) + c)
    c = c + jnp.sum(mask, axis=1, keepdims=True)
    carry[...] = c
    cnt_ref[...] = c


def _out_proj(x, pool_out, o_gdn, gt, sc, sh, g_post_mix, g_pre_ffn, w_out_b, router_w, router_b, counts0,
              tm, per_token):
    bsz, t, _ = x.shape
    n = bsz * t
    tiles = t // tm

    def tok(width):
        return pl.BlockSpec((1, tm, width), lambda b, i: (b, i, 0))

    def const(shape):
        return pl.BlockSpec(shape, lambda b, i: tuple(0 for _ in shape))

    kern = functools.partial(_out_kernel, tm=tm)
    return pl.pallas_call(
        kern,
        grid=(bsz, tiles),
        in_specs=[tok(D_MODEL), tok(POOL_WIDTH), tok(GDN_WIDTH),
                  _mod_spec(per_token, tm), _mod_spec(per_token, tm), _mod_spec(per_token, tm),
                  const((1, D_MODEL)), const((1, D_MODEL)), const((D_MODEL, D_MODEL)),
                  const((N_EXPERTS, D_MODEL)), const((N_EXPERTS, 1)), const((N_EXPERTS, 1))],
        out_specs=[tok(D_MODEL),
                   pl.BlockSpec((tm * ROW_TILES, LANES), lambda b, i: (b * tiles + i, 0)),
                   pl.BlockSpec((tm, N_EXPERTS), lambda b, i: (b * tiles + i, 0)),
                   pl.BlockSpec((tm, N_EXPERTS), lambda b, i: (b * tiles + i, 0)),
                   const((N_EXPERTS, 1))],
        out_shape=[jax.ShapeDtypeStruct((bsz, t, D_MODEL), F32),
                   jax.ShapeDtypeStruct((n * ROW_TILES, LANES), F32),
                   jax.ShapeDtypeStruct((n, N_EXPERTS), F32),
                   jax.ShapeDtypeStruct((n, N_EXPERTS), F32),
                   jax.ShapeDtypeStruct((N_EXPERTS, 1), F32)],
        scratch_shapes=[pltpu.VMEM((N_EXPERTS, 1), F32)],
        compiler_params=_cparams(("arbitrary", "arbitrary")),
        name="out_proj",
    )(x, pool_out, o_gdn, gt, sc, sh, g_post_mix.reshape(1, D_MODEL), g_pre_ffn.reshape(1, D_MODEL),
      w_out_b, jnp.swapaxes(router_w, 0, 1), router_b.reshape(N_EXPERTS, 1), counts0)


def _dest_kernel(g_ref, r_ref, start_ref, base_ref, d_ref, l_ref, gt_ref, *, tm, sub):
    g = g_ref[...]
    mask = g >= 0.0
    maskf = mask.astype(F32)
    dest = start_ref[...] + r_ref[...]
    base = jnp.concatenate([jnp.broadcast_to(base_ref[0, q:q + 1, :], (sub, N_EXPERTS)) for q in range(tm // sub)],
                           axis=0)
    local = dest - base
    ri = lax.broadcasted_iota(jnp.int32, (N_EXPERTS, N_EXPERTS), 0)
    ci = lax.broadcasted_iota(jnp.int32, (N_EXPERTS, N_EXPERTS), 1)
    slot = _dot(maskf, (ri < ci).astype(F32))
    lane4 = lax.broadcasted_iota(jnp.int32, (tm, TOP_K), 1)
    d_out = jnp.zeros((tm, TOP_K), F32)
    l_out = jnp.zeros((tm, TOP_K), F32)
    g_out = jnp.zeros((tm, TOP_K), F32)
    for j in range(TOP_K):
        pick = mask & (slot == float(j))
        dj = jnp.sum(jnp.where(pick, dest, 0.0), axis=-1, keepdims=True)
        lj = jnp.sum(jnp.where(pick, local, 0.0), axis=-1, keepdims=True)
        gj = jnp.sum(jnp.where(pick, g, 0.0), axis=-1, keepdims=True)
        d_out = jnp.where(lane4 == j, dj, d_out)
        l_out = jnp.where(lane4 == j, lj, l_out)
        g_out = jnp.where(lane4 == j, gj, g_out)
    d_ref[...] = d_out.astype(jnp.int32)
    l_ref[...] = l_out.astype(jnp.int32)
    gt_ref[...] = g_out


def _dest(gsel, rank, start_row, base, tm, sub):
    n = gsel.shape[0]
    per = tm // sub
    kern = functools.partial(_dest_kernel, tm=tm, sub=sub)
    tok4 = pl.BlockSpec((tm, TOP_K), lambda i: (i, 0))
    return pl.pallas_call(
        kern,
        grid=(n // tm,),
        in_specs=[pl.BlockSpec((tm, N_EXPERTS), lambda i: (i, 0)),
                  pl.BlockSpec((tm, N_EXPERTS), lambda i: (i, 0)),
                  pl.BlockSpec((1, N_EXPERTS), lambda i: (0, 0)),
                  pl.BlockSpec((1, per, N_EXPERTS), lambda i: (i, 0, 0))],
        out_specs=[tok4, tok4, tok4],
        out_shape=[jax.ShapeDtypeStruct((n, TOP_K), jnp.int32),
                   jax.ShapeDtypeStruct((n, TOP_K), jnp.int32),
                   jax.ShapeDtypeStruct((n, TOP_K), F32)],
        compiler_params=_cparams(("arbitrary",)),
        name="dest",
    )(gsel, rank, start_row, base.reshape(n // tm, per, N_EXPERTS))


def _row_copy(src, dst, s_row, d_row, sem):
    return pltpu.make_async_copy(src.at[pl.ds(pl.multiple_of(s_row * ROW_TILES, ROW_TILES), ROW_TILES), :],
                                 dst.at[pl.ds(pl.multiple_of(d_row * ROW_TILES, ROW_TILES), ROW_TILES), :], sem)


def _scatter_kernel(cnt_ref, start_ref, end_ref, nu_ref, dest_ref, ha_ref, hb_ref, xs_ref, sem, *,
                    tm, bm, n_blocks, tiles_a):
    i = pl.program_id(0)

    def scatter_from(ht_ref):
        def body(tt, carry):
            for u in range(ROW_LOOP_UNROLL):
                t = tt * ROW_LOOP_UNROLL + u
                src = ht_ref.at[pl.ds(pl.multiple_of(t * ROW_TILES, ROW_TILES), ROW_TILES), :]
                for k in range(TOP_K):
                    d = dest_ref[t * TOP_K + k]
                    pltpu.make_async_copy(
                        src, xs_ref.at[pl.ds(pl.multiple_of(d * ROW_TILES, ROW_TILES), ROW_TILES), :],
                        sem).start(priority=k % DMA_QUEUES)
            return carry

        lax.fori_loop(0, tm // ROW_LOOP_UNROLL, body, 0)

    @pl.when(i < tiles_a)
    def _():
        scatter_from(ha_ref)

    @pl.when(i >= tiles_a)
    def _():
        scatter_from(hb_ref)

    ht_ref = hb_ref

    def tile_copy(u):
        return pltpu.make_async_copy(
            ht_ref, xs_ref.at[pl.ds(pl.multiple_of(u * (tm * ROW_TILES), tm * ROW_TILES), tm * ROW_TILES), :], sem)

    for k in range(TOP_K):
        tile_copy(0).wait()

    @pl.when(i == pl.num_programs(0) - 1)
    def _():
        per = bm // tm

        def tail_start(u, carry):
            tile_copy(u).start()
            return carry

        def tail_wait(u, carry):
            tile_copy(u).wait()
            return carry

        lax.fori_loop(nu_ref[0] * per, n_blocks * per, tail_start, 0)
        lax.fori_loop(nu_ref[0] * per, n_blocks * per, tail_wait, 0)

        chunk = PAD_CHUNK
        assert chunk <= tm and bm % chunk == 0

        def chunk_copy(q):
            return pltpu.make_async_copy(
                ht_ref.at[pl.ds(0, chunk * ROW_TILES), :],
                xs_ref.at[pl.ds(pl.multiple_of(q * (chunk * ROW_TILES), chunk * ROW_TILES), chunk * ROW_TILES), :],
                sem)

        def per_expert(e, carry):
            lo = start_ref[e] + cnt_ref[e]
            mid = (lo + chunk - 1) // chunk * chunk
            hi = end_ref[e]

            def fill(r, c2):
                _row_copy(ht_ref, xs_ref, 0, r, sem).start()
                return c2

            def fill_chunk(q, c2):
                chunk_copy(q).start()
                return c2

            lax.fori_loop(lo, mid, fill, 0)
            lax.fori_loop(mid // chunk, hi // chunk, fill_chunk, 0)

            def drain(r, c2):
                _row_copy(ht_ref, xs_ref, 0, r, sem).wait()
                return c2

            def drain_chunk(q, c2):
                chunk_copy(q).wait()
                return c2

            lax.fori_loop(lo, mid, drain, 0)
            lax.fori_loop(mid // chunk, hi // chunk, drain_chunk, 0)
            return carry

        lax.fori_loop(0, N_EXPERTS, per_expert, 0)


def _scatter_rows(ht_a, ht_b, dest_flat, cnt, start, end, n_used, n_blocks, tm, bm):
    rows = n_blocks * bm
    tiles_a, tiles_b = ht_a.shape[0] // (tm * ROW_TILES), ht_b.shape[0] // (tm * ROW_TILES)
    assert bm % tm == 0 and ht_a.shape[0] % (tm * ROW_TILES) == 0 and ht_b.shape[0] % (tm * ROW_TILES) == 0
    assert tiles_b >= 1
    kern = functools.partial(_scatter_kernel, tm=tm, bm=bm, n_blocks=n_blocks, tiles_a=tiles_a)
    return pl.pallas_call(
        kern,
        grid_spec=pltpu.PrefetchScalarGridSpec(
            num_scalar_prefetch=4,
            grid=(tiles_a + tiles_b,),
            in_specs=[pl.BlockSpec((tm * TOP_K,), lambda i, *_: (i,), memory_space=pltpu.SMEM),
                      pl.BlockSpec((tm * ROW_TILES, LANES), lambda i, *_: (jnp.minimum(i, tiles_a - 1), 0)),
                      pl.BlockSpec((tm * ROW_TILES, LANES), lambda i, *_: (jnp.maximum(i - tiles_a, 0), 0))],
            out_specs=pl.BlockSpec(memory_space=pl.ANY),
            scratch_shapes=[pltpu.SemaphoreType.DMA(())]),
        out_shape=jax.ShapeDtypeStruct((rows * ROW_TILES, LANES), F32),
        compiler_params=_cparams(("arbitrary",)),
        name="scatter_rows",
    )(cnt, start, end, n_used, dest_flat, ht_a, ht_b)


def _expert_kernel(be_ref, nx_ref, nv_ref, nu_ref, xs_ref, wg_ref, bg_ref, wd_ref, bd_ref, ys_ref,
                   wg_f, wd_f, wg_s, wd_s, xbuf, wsem, xsem, *, bm):
    j = pl.program_id(0)
    prev = be_ref[jnp.maximum(j - 1, 0)]
    fresh = (j == 0) | (be_ref[j] != prev)
    active = j < nu_ref[0]
    slot = j % EXPERT_IN_BUFFERS

    def block_copy(b):
        return pltpu.make_async_copy(
            xs_ref.at[pl.ds(pl.multiple_of(b * (bm * ROW_TILES), bm * ROW_TILES), bm * ROW_TILES), :],
            xbuf.at[b % EXPERT_IN_BUFFERS], xsem.at[b % EXPERT_IN_BUFFERS])

    @pl.when(j == 0)
    def _():
        for b in range(EXPERT_IN_BUFFERS - 1):
            @pl.when(b < nu_ref[0])
            def _(b=b):
                block_copy(b).start()

    @pl.when(j + (EXPERT_IN_BUFFERS - 1) < nu_ref[0])
    def _():
        block_copy(j + (EXPERT_IN_BUFFERS - 1)).start()

    @pl.when(active)
    def _():
        block_copy(j).wait()

    def weight_copies(e):
        return (pltpu.make_async_copy(wg_ref.at[e], wg_f, wsem.at[0]),
                pltpu.make_async_copy(wd_ref.at[e], wd_f, wsem.at[1]))

    @pl.when(j == 0)
    def _():
        for c in weight_copies(be_ref[0]):
            c.start()

    @pl.when(active & fresh)
    def _():
        for c in weight_copies(be_ref[j]):
            c.wait()
        wg_s[...] = wg_f[...].astype(BF16)
        wd_s[...] = wd_f[...].astype(BF16)

        @pl.when(nx_ref[j] != be_ref[j])
        def _():
            for c in weight_copies(nx_ref[j]):
                c.start()

    def run_rows(m):
        x = jnp.concatenate([xbuf[slot, pl.ds(s, m, stride=ROW_TILES), :] for s in range(ROW_TILES)], axis=-1)
        xb = x.astype(BF16)
        acts = []
        for c in range(EXPERT_COL_CHUNKS):
            w = D_FF // EXPERT_COL_CHUNKS
            lo = c * w
            gate = jnp.dot(xb, wg_s[:, lo:lo + w], preferred_element_type=F32) + bg_ref[0, :, lo:lo + w]
            up = (jnp.dot(xb, wg_s[:, D_FF + lo:D_FF + lo + w], preferred_element_type=F32)
                  + bg_ref[0, :, D_FF + lo:D_FF + lo + w])
            gate = jnp.minimum(gate, SWIGLU_LIMIT)
            up = jnp.clip(up, -SWIGLU_LIMIT, SWIGLU_LIMIT)
            acts.append(((up + 1.0) * (gate * jax.nn.sigmoid(SWIGLU_ALPHA * gate))).astype(BF16))
        act = jnp.concatenate(acts, axis=-1)
        for c in range(EXPERT_COL_CHUNKS):
            w = D_MODEL // EXPERT_COL_CHUNKS
            lo = c * w
            y = jnp.dot(act, wd_s[:, lo:lo + w], preferred_element_type=F32) + bd_ref[0, :, lo:lo + w]
            for s in range(w // LANES):
                ys_ref[pl.ds(lo // LANES + s, m, stride=ROW_TILES), :] = y[:, s * LANES:(s + 1) * LANES]
        if m < bm:
            ys_ref[pl.ds(m * ROW_TILES, (bm - m) * ROW_TILES), :] = jnp.zeros(((bm - m) * ROW_TILES, LANES), F32)

    quarter = bm // 4
    quarters = (nv_ref[j] + quarter - 1) // quarter
    for nq in range(1, 5):
        @pl.when(active & ((quarters == nq) if nq > 1 else (quarters <= 1)))
        def _(nq=nq):
            run_rows(nq * quarter)

    @pl.when(jnp.logical_not(active))
    def _():
        ys_ref[...] = jnp.zeros_like(ys_ref)


def _experts(xs, block_expert, next_expert, block_valid, n_used, w_gu, b_gu, w_down, b_down, bm):
    rows = xs.shape[0] // ROW_TILES
    n_blocks = rows // bm

    def blk(j, be, nx, nv, nu):
        return (jnp.minimum(j, nu[0] - 1), 0)

    def bsel(j, be, nx, nv, nu):
        return (be[j], 0, 0)

    hbm = pl.BlockSpec(memory_space=pl.ANY)
    kern = functools.partial(_expert_kernel, bm=bm)
    return pl.pallas_call(
        kern,
        grid_spec=pltpu.PrefetchScalarGridSpec(
            num_scalar_prefetch=4,
            grid=(n_blocks,),
            in_specs=[hbm, hbm,
                      pl.BlockSpec((1, 1, 2 * D_FF), bsel), hbm,
                      pl.BlockSpec((1, 1, D_MODEL), bsel)],
            out_specs=pl.BlockSpec((bm * ROW_TILES, LANES), lambda j, be, nx, nv, nu: (j, 0)),
            scratch_shapes=[pltpu.VMEM((D_MODEL, 2 * D_FF), F32),
                            pltpu.VMEM((D_FF, D_MODEL), F32),
                            pltpu.VMEM((D_MODEL, 2 * D_FF), BF16),
                            pltpu.VMEM((D_FF, D_MODEL), BF16),
                            pltpu.VMEM((EXPERT_IN_BUFFERS, bm * ROW_TILES, LANES), F32),
                            pltpu.SemaphoreType.DMA((2,)),
                            pltpu.SemaphoreType.DMA((EXPERT_IN_BUFFERS,))]),
        out_shape=jax.ShapeDtypeStruct((rows * ROW_TILES, LANES), F32),
        compiler_params=_cparams(("arbitrary",)),
        name="experts",
    )(block_expert, next_expert, block_valid, n_used, xs, w_gu, b_gu.reshape(N_EXPERTS, 1, 2 * D_FF), w_down,
      b_down.reshape(N_EXPERTS, 1, D_MODEL))


def _final_kernel(loc_ref, gate_ref, tcur_ref, tnext_ref, ys_ref, x1_ref, gt_ref, g_ref, o_ref,
                  st_ref, f_ref, sem, *, tm, steps):
    g = pl.program_id(0) * pl.num_programs(1) + pl.program_id(1)
    cur = g % 2
    sizes = [1 << b for b in range(tm.bit_length() - 1, -1, -1)]

    def request(t_ref, slot):
        def per_expert(e, off):
            row = t_ref[e]
            cnt = t_ref[N_EXPERTS + e]
            for size in sizes:
                take = (cnt & size) != 0

                @pl.when(take)
                def _(row=row, off=off, size=size):
                    pltpu.make_async_copy(
                        ys_ref.at[pl.ds(pl.multiple_of(row * ROW_TILES, ROW_TILES), size * ROW_TILES), :],
                        st_ref.at[slot, pl.ds(pl.multiple_of(off * ROW_TILES, ROW_TILES), size * ROW_TILES), :],
                        sem.at[slot]).start()

                step = jnp.where(take, size, 0)
                row = row + step
                off = off + step
            return off

        lax.fori_loop(0, N_EXPERTS, per_expert, 0)

    @pl.when(g == 0)
    def _():
        request(tcur_ref, 0)

    @pl.when(g + 1 < steps)
    def _():
        request(tnext_ref, 1 - cur)

    pltpu.make_async_copy(ys_ref.at[pl.ds(0, tm * TOP_K * ROW_TILES), :], st_ref.at[cur], sem.at[cur]).wait()

    def pick(tt, carry):
        for u in range(ROW_LOOP_UNROLL):
            t = tt * ROW_LOOP_UNROLL + u
            for k in range(TOP_K):
                at = pl.multiple_of(loc_ref[t * TOP_K + k] * ROW_TILES, ROW_TILES)
                f_ref[k, pl.ds(pl.multiple_of(t * ROW_TILES, ROW_TILES), ROW_TILES), :] = (
                    st_ref[cur, pl.ds(at, ROW_TILES), :])
        return carry

    lax.fori_loop(0, tm // ROW_LOOP_UNROLL, pick, 0)
    gates = gate_ref[...]
    cols = []
    for s in range(ROW_TILES):
        acc = None
        for k in range(TOP_K):
            part = f_ref[k, pl.ds(s, tm, stride=ROW_TILES), :] * gates[:, k:k + 1]
            acc = part if acc is None else acc + part
        cols.append(acc)
    f = jnp.concatenate(cols, axis=-1)
    o_ref[0] = x1_ref[0] + gt_ref[0] * _rms(f, g_ref[...])


def _final(ys, local_flat, gates, run_table, x1, gt, g_post_ffn, tok_off, tm, per_token):
    bsz, t, _ = x1.shape
    tiles = t // tm
    off = tok_off // tm
    steps = bsz * tiles
    kern = functools.partial(_final_kernel, tm=tm, steps=steps)

    def smem(ahead):
        return pl.BlockSpec((tm * TOP_K,), lambda b, i: (off + jnp.minimum(b * tiles + i + ahead, steps - 1),),
                            memory_space=pltpu.SMEM)

    return pl.pallas_call(
        kern,
        grid=(bsz, tiles),
        in_specs=[smem(0), pl.BlockSpec((tm, TOP_K), lambda b, i: (off + b * tiles + i, 0)), smem(0), smem(1),
                  pl.BlockSpec(memory_space=pl.ANY),
                  pl.BlockSpec((1, tm, D_MODEL), lambda b, i: (b, i, 0)),
                  _mod_spec(per_token, tm),
                  pl.BlockSpec((1, D_MODEL), lambda b, i: (0, 0))],
        out_specs=pl.BlockSpec((1, tm, D_MODEL), lambda b, i: (b, i, 0)),
        out_shape=jax.ShapeDtypeStruct((bsz, t, D_MODEL), F32),
        scratch_shapes=[pltpu.VMEM((2, tm * TOP_K * ROW_TILES, LANES), F32),
                        pltpu.VMEM((TOP_K, tm * ROW_TILES, LANES), F32),
                        pltpu.SemaphoreType.DMA((2,))],
        compiler_params=_cparams(("arbitrary", "arbitrary")),
        name="final",
    )(local_flat, gates, run_table, run_table, ys, x1, gt, g_post_ffn.reshape(1, D_MODEL))


def _moe(ht_a, ht_b, gsel, rank, counts, w_gu, b_gu, w_down, b_down, tm, bm):
    n = gsel.shape[0]
    cnt = counts[:, 0].astype(jnp.int32)
    nblk = (cnt + bm - 1) // bm
    blk_end = jnp.cumsum(nblk)
    blk_start = blk_end - nblk
    n_blocks = -(-(n * TOP_K) // bm) + N_EXPERTS
    blocks = jnp.arange(n_blocks, dtype=jnp.int32)
    block_expert = jnp.minimum(jnp.sum((blk_end[None, :] <= blocks[:, None]).astype(jnp.int32), axis=1),
                               N_EXPERTS - 1).astype(jnp.int32)
    ids = jnp.arange(N_EXPERTS, dtype=jnp.int32)
    later = jnp.where((nblk[None, :] > 0) & (ids[None, :] > ids[:, None]), ids[None, :], N_EXPERTS)
    nearest = jnp.min(later, axis=1)
    next_used = jnp.where(nearest < N_EXPERTS, nearest, ids).astype(jnp.int32)
    n_used = blk_end[-1:].astype(jnp.int32)
    row_start = (blk_start * bm).astype(jnp.int32)
    row_end = (blk_end * bm).astype(jnp.int32)
    before = rank[::tm]
    run_len = jnp.concatenate([before[1:], counts.reshape(1, N_EXPERTS)], axis=0) - before
    run_row = row_start.astype(F32)[None, :] + before
    run_off = jnp.cumsum(run_len, axis=1) - run_len
    tiles = n // tm
    run_table = jnp.zeros((tiles, tm * TOP_K), jnp.int32)
    run_table = run_table.at[:, :N_EXPERTS].set(run_row.astype(jnp.int32))
    run_table = run_table.at[:, N_EXPERTS:2 * N_EXPERTS].set(run_len.astype(jnp.int32)).reshape(tiles * tm * TOP_K)
    dest, local, gates = _dest(gsel, rank, row_start.astype(F32).reshape(1, N_EXPERTS), run_row - run_off,
                               math.gcd(n, DEST_TILE), tm)
    dest_flat = dest.reshape(n * TOP_K)
    xs = _scatter_rows(ht_a, ht_b, dest_flat, cnt, row_start, row_end, n_used, n_blocks, SCATTER_TILE, bm)
    owns = (blocks[:, None] >= blk_start[None, :]) & (blocks[:, None] < blk_end[None, :])

    def per_block(table):
        return jnp.sum(jnp.where(owns, table[None, :], 0), axis=1).astype(jnp.int32)

    block_valid = jnp.clip(per_block(cnt) - (blocks - per_block(blk_start)) * bm, 0, bm).astype(jnp.int32)
    ys = _experts(xs, block_expert, per_block(next_used), block_valid, n_used, w_gu, b_gu, w_down, b_down, bm)
    return ys, local.reshape(n * TOP_K), gates, run_table


def _mix_path(x, mod, pool_buf, conv_buf, s0, start_pos, counts0, lp, tm, nb, per_token):
    (g_pre_mix, g_post_mix, g_pre_ffn, w_in_p, w_bd, pool_scale, conv_w, alog_row, dtb_row, norm_w,
     w_out_b, router_w, router_b) = lp
    bsz, t, _ = x.shape
    sh_mix, sc_mix, gt_mix, sh_ffn, sc_ffn, gt_ffn = (mod[:, i] for i in range(6))

    def shape_mod(m):
        if per_token:
            return jnp.repeat(m, t, axis=0).reshape(1, bsz * t, D_MODEL)
        return m.reshape(bsz, 1, D_MODEL)

    sh_mix, sc_mix, gt_mix, sh_ffn, sc_ffn, gt_ffn = map(shape_mod, (sh_mix, sc_mix, gt_mix, sh_ffn, sc_ffn, gt_ffn))
    xf = x.reshape(1, bsz * t, D_MODEL) if per_token else x
    proj = _in_proj(xf, sc_mix, sh_mix, g_pre_mix, w_in_p, PROJ_TILE, per_token).reshape(bsz, t, PROJ_WIDTH)
    tseq = min(tm, t)
    o_gdn, s_new = _gdn(proj, conv_buf, s0, conv_w, alog_row, dtb_row, norm_w, nb, tseq)
    pool_out = _pool(proj, pool_buf, w_bd, pool_scale, nb * POOL_SEQS if per_token else nb, min(POOL_TILE, t),
                     start_pos)
    fl = (lambda a: a.reshape(1, bsz * t, a.shape[-1])) if per_token else (lambda a: a)
    x1, ht, gsel, rank, counts = _out_proj(xf, fl(pool_out), fl(o_gdn), gt_mix, sc_ffn, sh_ffn, g_post_mix,
                                           g_pre_ffn, w_out_b, router_w, router_b, counts0, OUT_TILE, per_token)

    def last_rows(buf, off, width):
        keep = buf.shape[1]
        if t >= keep:
            return proj[:, t - keep:, off:off + width]
        return jnp.concatenate([buf[:, t:], proj[:, :, off:off + width]], axis=1)

    pool_new = last_rows(pool_buf, OFF_U, POOL_WIDTH)
    conv_new = last_rows(conv_buf, OFF_QKV, CONV_CH)
    return x1, ht, gsel, rank, counts, gt_ffn, pool_new, conv_new, s_new


def kernel(x_prompt, x_sample, c_prompt, c_sample, state_pool, state_conv, state_ssm, w_ada, b_ada, g_pre_mix, g_post_mix, g_pre_ffn, g_post_ffn, w_in, w_pool, pool_scale, conv_w, a_log, dt_bias, gdn_norm_w, w_out, router_w, router_b, w_gu, b_gu, w_down, b_down):
    depth = w_ada.shape[0]
    bp, tp, _ = x_prompt.shape
    bs, ts, _ = x_sample.shape
    tm = TOKEN_TILE
    y_p, y_s = x_prompt, x_sample
    outs = [[] for _ in range(6)]
    for l in range(depth):
        mod = _ada(jnp.concatenate([c_prompt, c_sample], axis=0), w_ada[l], b_ada[l]).reshape(bp + bs, 6, D_MODEL)
        w_in_p = jnp.swapaxes(w_in[l], 0, 1)
        w_bd = jax.scipy.linalg.block_diag(*[w_pool[l, g] for g in range(len(POOL_WINDOWS))])
        pad_row = lambda v: jnp.pad(v, (0, LANES - v.shape[0])).reshape(1, LANES)
        lp = (g_pre_mix[l], g_post_mix[l], g_pre_ffn[l], w_in_p, w_bd, pool_scale[l], conv_w[l],
              pad_row(a_log[l]), pad_row(dt_bias[l]), gdn_norm_w[l].reshape(1, HEAD_DIM),
              w_out[l].astype(BF16), router_w[l], router_b[l])
        zero_pool = jnp.zeros((bp, POOL_BUF, POOL_WIDTH), F32)
        zero_conv = jnp.zeros((bp, CONV_WIDTH - 1, CONV_CH), F32)
        zero_ssm = (jnp.zeros((1, bp, GDN_HEADS, HEAD_DIM, HEAD_DIM), F32), 0)
        no_counts = jnp.zeros((N_EXPERTS, 1), F32)
        x1p, htp, gp, rp, cnt_p, gtp, pp, cp, sp = _mix_path(
            y_p, mod[:bp], zero_pool, zero_conv, zero_ssm, 0, no_counts, lp, tm, 1, False)
        x1s, hts, gs, rs, counts, gts, ps, cs, ss = _mix_path(
            y_s, mod[bp:], state_pool[l], state_conv[l], (state_ssm, l), PAST_LEN, cnt_p, lp, tm, 8, True)
        gsel = jnp.concatenate([gp, gs], axis=0)
        rank = jnp.concatenate([rp, rs], axis=0)
        ys, local, gates, runs = _moe(htp, hts, gsel, rank, counts, w_gu[l], b_gu[l], w_down[l], b_down[l],
                                      FINAL_TILE, MOE_BLOCK)
        y_p = _final(ys, local, gates, runs, x1p, gtp, g_post_ffn[l], 0, FINAL_TILE, False)
        y_s = _final(ys, local, gates, runs, x1s, gts, g_post_ffn[l], bp * tp, FINAL_TILE,
                     True).reshape(bs, ts, D_MODEL)
        for lst, val in zip(outs, (pp, cp, sp, ps, cs, ss)):
            lst.append(val)
    return (y_p, y_s) + tuple(o[0][None] if depth == 1 else jnp.stack(o) for o in outs)
```
